```python
import math
import jax, jax.numpy as jnp
from jax import lax
import numpy as np

D_MODEL = 1024
BATCH = 8
SEQ = 8192
DEPTH = 1

D_MIX = D_MODEL
CONV_CH = D_MIX // 2
CONV_GROUPS = 8
CONV_WIDTH = 31
SB_HEADS = 8
SB_HEAD_DIM = 64
SB_WIDTH = SB_HEADS * SB_HEAD_DIM
Q_BLOCK = 128
D_FF = int(math.ceil((8 * D_MODEL / 3) / 256) * 256)
IN_COLS = 2 * CONV_CH + 3 * SB_WIDTH
EPS = 1e-6

kernel_name = "hymba_conformer_stickbreaking_sandwich"


def _rmsnorm(x, g):
    xf = x.astype(jnp.float32)
    y = xf * lax.rsqrt(jnp.mean(xf * xf, axis=-1, keepdims=True) + EPS)
    return (y * g.astype(jnp.float32)).astype(x.dtype)


def _layernorm(x, g, b):
    xf = x.astype(jnp.float32)
    mu = jnp.mean(xf, axis=-1, keepdims=True)
    var = jnp.mean(jnp.square(xf - mu), axis=-1, keepdims=True)
    y = (xf - mu) * lax.rsqrt(var + EPS)
    return (y * g.astype(jnp.float32) + b.astype(jnp.float32)).astype(x.dtype)


def _conformer_conv(val, gate, conv_w, conv_b, ln_g, ln_b):
    glu = val * jax.nn.sigmoid(gate)
    padded = jnp.pad(glu, ((0, 0), (CONV_WIDTH - 1, 0), (0, 0)))
    y = lax.conv_general_dilated(
        padded, conv_w.astype(glu.dtype), window_strides=(1,), padding="VALID",
        dimension_numbers=("NWC", "WIO", "NWC"), feature_group_count=CONV_CH)
    y = y + conv_b
    y = _layernorm(y, ln_g, ln_b)
    return jax.nn.silu(y)


def _stick_breaking(q, k, v):
    B, S = q.shape[0], q.shape[1]
    scale = 1.0 / math.sqrt(SB_HEAD_DIM)
    qh = jnp.transpose(q, (0, 2, 1, 3))
    kh = jnp.transpose(k, (0, 2, 1, 3))
    vh = jnp.transpose(v, (0, 2, 1, 3))
    nb = S // Q_BLOCK
    q_blocks = jnp.transpose(qh.reshape(B, SB_HEADS, nb, Q_BLOCK, SB_HEAD_DIM), (2, 0, 1, 3, 4))
    starts = jnp.arange(nb, dtype=jnp.int32) * Q_BLOCK
    key_pos = jnp.arange(S, dtype=jnp.int32)

    def one_block(args):
        qb, t0 = args
        z = jnp.einsum("bhqd,bhkd->bhqk", qb, kh,
                       preferred_element_type=jnp.float32) * scale
        q_pos = t0 + jnp.arange(Q_BLOCK, dtype=jnp.int32)
        mask = key_pos[None, :] < q_pos[:, None]
        log_beta = jax.nn.log_sigmoid(z)
        log_1m_beta = jnp.where(mask, jax.nn.log_sigmoid(-z), 0.0)
        between = lax.cumsum(log_1m_beta, axis=3, reverse=True) - log_1m_beta
        attn = jnp.where(mask, jnp.exp(log_beta + between), 0.0)
        return jnp.einsum("bhqk,bhkd->bhqd", attn.astype(vh.dtype), vh)

    out = lax.map(one_block, (q_blocks, starts))
    out = jnp.transpose(out, (1, 0, 3, 2, 4))
    return out.reshape(B, S, SB_HEADS, SB_HEAD_DIM)


def _fwd_setup_inputs(seed: int = 0) -> dict:
    key = jax.random.key(seed)
    ks = jax.random.split(key, 16)
    nrm = jax.random.normal
    def gain(k, shape):
        return 1.0 + 0.05 * nrm(k, shape, jnp.float32)
    return {
        "x": nrm(ks[0], (BATCH, SEQ, D_MODEL), jnp.float32),
        "g_pre_mix": gain(ks[1], (DEPTH, D_MODEL)),
        "w_in": nrm(ks[2], (DEPTH, D_MODEL, IN_COLS), jnp.float32) * D_MODEL ** -0.5,
        "conv_w": nrm(ks[3], (DEPTH, CONV_WIDTH, 1, CONV_CH), jnp.float32) * CONV_WIDTH ** -0.5,
        "conv_b": 0.02 * nrm(ks[4], (DEPTH, CONV_CH), jnp.float32),
        "conv_ln_g": gain(ks[5], (DEPTH, CONV_CH)),
        "conv_ln_b": 0.02 * nrm(ks[6], (DEPTH, CONV_CH), jnp.float32),
        "attn_norm_g": gain(ks[7], (DEPTH, SB_HEADS, SB_HEAD_DIM)),
        "w_out": nrm(ks[8], (DEPTH, D_MIX, D_MODEL), jnp.float32) * D_MIX ** -0.5,
        "g_post_mix": gain(ks[9], (DEPTH, D_MODEL)),
        "g_pre_ffn": gain(ks[10], (DEPTH, D_MODEL)),
        "w_gate": nrm(ks[11], (DEPTH, D_MODEL, D_FF), jnp.float32) * D_MODEL ** -0.5,
        "w_up": nrm(ks[12], (DEPTH, D_MODEL, D_FF), jnp.float32) * D_MODEL ** -0.5,
        "w_down": nrm(ks[13], (DEPTH, D_FF, D_MODEL), jnp.float32) * D_FF ** -0.5,
        "g_post_ffn": gain(ks[14], (DEPTH, D_MODEL)),
    }


def _fwd_reference(x, g_pre_mix, w_in, conv_w, conv_b, conv_ln_g, conv_ln_b, attn_norm_g,
              w_out, g_post_mix, g_pre_ffn, w_gate, w_up, w_down, g_post_ffn):
    B, S, _ = x.shape
    h = x
    for l in range(DEPTH):
        a = _rmsnorm(h, g_pre_mix[l])
        u = jnp.einsum("bsd,dc->bsc", a, w_in[l])
        c_val = u[..., :CONV_CH]
        c_gate = u[..., CONV_CH:2 * CONV_CH]
        qkv = u[..., 2 * CONV_CH:].reshape(B, S, 3, SB_HEADS, SB_HEAD_DIM)
        conv_out = _conformer_conv(c_val, c_gate, conv_w[l], conv_b[l], conv_ln_g[l], conv_ln_b[l])
        attn_out = _stick_breaking(qkv[:, :, 0], qkv[:, :, 1], qkv[:, :, 2])
        attn_out = _rmsnorm(attn_out, attn_norm_g[l]).reshape(B, S, SB_WIDTH)
        mixed = jnp.concatenate([conv_out, attn_out], axis=-1)
        y = jnp.einsum("bsc,cd->bsd", mixed, w_out[l])
        h = h + _rmsnorm(y, g_post_mix[l])
        f_in = _rmsnorm(h, g_pre_ffn[l])
        gt = jnp.einsum("bsd,df->bsf", f_in, w_gate[l])
        up = jnp.einsum("bsd,df->bsf", f_in, w_up[l])
        f = jnp.einsum("bsf,fd->bsd", jax.nn.silu(gt) * up, w_down[l])
        h = h + _rmsnorm(f, g_post_ffn[l])
    return h


import jax as _jax
import jax.numpy as _jnp

TWIN_FORMAT = 'train_step'
FWD_PARAMS = ['x', 'g_pre_mix', 'w_in', 'conv_w', 'conv_b', 'conv_ln_g', 'conv_ln_b', 'attn_norm_g', 'w_out', 'g_post_mix', 'g_pre_ffn', 'w_gate', 'w_up', 'w_down', 'g_post_ffn']
TWIN_WEIGHTS = ['g_pre_mix', 'w_in', 'conv_w', 'conv_b', 'conv_ln_g', 'conv_ln_b', 'attn_norm_g', 'w_out', 'g_post_mix', 'g_pre_ffn', 'w_gate', 'w_up', 'w_down', 'g_post_ffn']
TWIN_DIFF_INPUT = 'x'
TWIN_INPUTS = ['x', 'g_pre_mix', 'w_in', 'conv_w', 'conv_b', 'conv_ln_g', 'conv_ln_b', 'attn_norm_g', 'w_out', 'g_post_mix', 'g_pre_ffn', 'w_gate', 'w_up', 'w_down', 'g_post_ffn', 'loss_target', 'm_g_pre_mix', 'm_w_in', 'm_conv_w', 'm_conv_b', 'm_conv_ln_g', 'm_conv_ln_b', 'm_attn_norm_g', 'm_w_out', 'm_g_post_mix', 'm_g_pre_ffn', 'm_w_gate', 'm_w_up', 'm_w_down', 'm_g_post_ffn', 'v_g_pre_mix', 'v_w_in', 'v_conv_w', 'v_conv_b', 'v_conv_ln_g', 'v_conv_ln_b', 'v_attn_norm_g', 'v_w_out', 'v_g_post_mix', 'v_g_pre_ffn', 'v_w_gate', 'v_w_up', 'v_w_down', 'v_g_post_ffn']
TWIN_OUTPUTS = ['loss', 'grad_x', 'grad_g_pre_mix', 'grad_w_in', 'grad_conv_w', 'grad_conv_b', 'grad_conv_ln_g', 'grad_conv_ln_b', 'grad_attn_norm_g', 'grad_w_out', 'grad_g_post_mix', 'grad_g_pre_ffn', 'grad_w_gate', 'grad_w_up', 'grad_w_down', 'grad_g_post_ffn', 'delta_g_pre_mix', 'delta_w_in', 'delta_conv_w', 'delta_conv_b', 'delta_conv_ln_g', 'delta_conv_ln_b', 'delta_attn_norm_g', 'delta_w_out', 'delta_g_post_mix', 'delta_g_pre_ffn', 'delta_w_gate', 'delta_w_up', 'delta_w_down', 'delta_g_post_ffn', 'new_m_g_pre_mix', 'new_m_w_in', 'new_m_conv_w', 'new_m_conv_b', 'new_m_conv_ln_g', 'new_m_conv_ln_b', 'new_m_attn_norm_g', 'new_m_w_out', 'new_m_g_post_mix', 'new_m_g_pre_ffn', 'new_m_w_gate', 'new_m_w_up', 'new_m_w_down', 'new_m_g_post_ffn', 'new_v_g_pre_mix', 'new_v_w_in', 'new_v_conv_w', 'new_v_conv_b', 'new_v_conv_ln_g', 'new_v_conv_ln_b', 'new_v_attn_norm_g', 'new_v_w_out', 'new_v_g_post_mix', 'new_v_g_pre_ffn', 'new_v_w_gate', 'new_v_w_up', 'new_v_w_down', 'new_v_g_post_ffn']
TWIN_LEAF_KINDS = {'loss': 'loss', 'grad_x': 'grad_x', 'grad_g_pre_mix': 'grad_w', 'grad_w_in': 'grad_w', 'grad_conv_w': 'grad_w', 'grad_conv_b': 'grad_w', 'grad_conv_ln_g': 'grad_w', 'grad_conv_ln_b': 'grad_w', 'grad_attn_norm_g': 'grad_w', 'grad_w_out': 'grad_w', 'grad_g_post_mix': 'grad_w', 'grad_g_pre_ffn': 'grad_w', 'grad_w_gate': 'grad_w', 'grad_w_up': 'grad_w', 'grad_w_down': 'grad_w', 'grad_g_post_ffn': 'grad_w', 'delta_g_pre_mix': 'delta_w', 'delta_w_in': 'delta_w', 'delta_conv_w': 'delta_w', 'delta_conv_b': 'delta_w', 'delta_conv_ln_g': 'delta_w', 'delta_conv_ln_b': 'delta_w', 'delta_attn_norm_g': 'delta_w', 'delta_w_out': 'delta_w', 'delta_g_post_mix': 'delta_w', 'delta_g_pre_ffn': 'delta_w', 'delta_w_gate': 'delta_w', 'delta_w_up': 'delta_w', 'delta_w_down': 'delta_w', 'delta_g_post_ffn': 'delta_w', 'new_m_g_pre_mix': 'new_m', 'new_m_w_in': 'new_m', 'new_m_conv_w': 'new_m', 'new_m_conv_b': 'new_m', 'new_m_conv_ln_g': 'new_m', 'new_m_conv_ln_b': 'new_m', 'new_m_attn_norm_g': 'new_m', 'new_m_w_out': 'new_m', 'new_m_g_post_mix': 'new_m', 'new_m_g_pre_ffn': 'new_m', 'new_m_w_gate': 'new_m', 'new_m_w_up': 'new_m', 'new_m_w_down': 'new_m', 'new_m_g_post_ffn': 'new_m', 'new_v_g_pre_mix': 'new_v', 'new_v_w_in': 'new_v', 'new_v_conv_w': 'new_v', 'new_v_conv_b': 'new_v', 'new_v_conv_ln_g': 'new_v', 'new_v_conv_ln_b': 'new_v', 'new_v_attn_norm_g': 'new_v', 'new_v_w_out': 'new_v', 'new_v_g_post_mix': 'new_v', 'new_v_g_pre_ffn': 'new_v', 'new_v_w_gate': 'new_v', 'new_v_w_up': 'new_v', 'new_v_w_down': 'new_v', 'new_v_g_post_ffn': 'new_v'}


def _forward(args):
    return _fwd_reference(*[args[k] for k in FWD_PARAMS])


def _output_shape():
    def fwd():
        inp = _fwd_setup_inputs(0)
        return _fwd_reference(*[inp[k] for k in FWD_PARAMS])
    out = _jax.eval_shape(fwd)
    return out.shape, out.dtype

N_MICROBATCH = 1
ADAM_LR = 0.001
ADAM_B1 = 0.9
ADAM_B2 = 0.999
ADAM_EPS = 1e-08
ADAM_WD = 0.01
ADAM_STEP = 10
PER_EXAMPLE_BATCH_AXIS = {'x': 0, 'loss_target': 0}
SHARED_INPUTS = []
_WEIGHT_DTYPES = {'g_pre_mix': _jnp.float32, 'w_in': _jnp.float32, 'conv_w': _jnp.float32, 'conv_b': _jnp.float32, 'conv_ln_g': _jnp.float32, 'conv_ln_b': _jnp.float32, 'attn_norm_g': _jnp.float32, 'w_out': _jnp.float32, 'g_post_mix': _jnp.float32, 'g_pre_ffn': _jnp.float32, 'w_gate': _jnp.float32, 'w_up': _jnp.float32, 'w_down': _jnp.float32, 'g_post_ffn': _jnp.float32}
MOMENT_SCALE = {'g_pre_mix': 8.540406e-01, 'w_in': 5.030409e-01, 'conv_w': 5.426342e-01, 'conv_b': 4.943586e+00, 'conv_ln_g': 2.233059e+00, 'conv_ln_b': 3.314206e+00, 'attn_norm_g': 1.458622e+00, 'w_out': 1.088030e+00, 'g_post_mix': 6.433356e+01, 'g_pre_ffn': 1.120779e+00, 'w_gate': 3.525066e-01, 'w_up': 6.109893e-01, 'w_down': 1.025550e+00, 'g_post_ffn': 6.413683e+01}


def _to_microbatches(a, axis):
    t = _jnp.moveaxis(a, axis, 0)
    t = t.reshape((N_MICROBATCH, t.shape[0] // N_MICROBATCH) + t.shape[1:])
    return _jnp.moveaxis(t, 1, axis + 1)


def setup_inputs(seed: int = 0) -> dict:
    inp = _fwd_setup_inputs(seed)
    key = _jax.random.fold_in(_jax.random.key(seed), 7919)
    shape, _ = _output_shape()
    out = dict(inp)
    out["loss_target"] = _jax.random.normal(_jax.random.fold_in(key, 0), shape, _jnp.float32)
    for i, name in enumerate(TWIN_WEIGHTS):
        w = inp[name].astype(_jnp.float32)
        if MOMENT_SCALE is None:
            s = _jnp.sqrt(_jnp.mean(_jnp.square(w)) + 1e-30)
        else:
            s = MOMENT_SCALE[name]
        km, kv = _jax.random.split(_jax.random.fold_in(key, i + 1))
        out[name] = w
        out["m_" + name] = s * _jax.random.normal(km, w.shape, _jnp.float32)
        out["v_" + name] = (s * s) * _jax.random.uniform(kv, w.shape, _jnp.float32, 0.5, 1.5)
    if N_MICROBATCH > 1:
        for name, axis in PER_EXAMPLE_BATCH_AXIS.items():
            out[name] = _to_microbatches(out[name], axis)
    return {'x': out['x'], 'g_pre_mix': out['g_pre_mix'], 'w_in': out['w_in'], 'conv_w': out['conv_w'], 'conv_b': out['conv_b'], 'conv_ln_g': out['conv_ln_g'], 'conv_ln_b': out['conv_ln_b'], 'attn_norm_g': out['attn_norm_g'], 'w_out': out['w_out'], 'g_post_mix': out['g_post_mix'], 'g_pre_ffn': out['g_pre_ffn'], 'w_gate': out['w_gate'], 'w_up': out['w_up'], 'w_down': out['w_down'], 'g_post_ffn': out['g_post_ffn'], 'loss_target': out['loss_target'], 'm_g_pre_mix': out['m_g_pre_mix'], 'm_w_in': out['m_w_in'], 'm_conv_w': out['m_conv_w'], 'm_conv_b': out['m_conv_b'], 'm_conv_ln_g': out['m_conv_ln_g'], 'm_conv_ln_b': out['m_conv_ln_b'], 'm_attn_norm_g': out['m_attn_norm_g'], 'm_w_out': out['m_w_out'], 'm_g_post_mix': out['m_g_post_mix'], 'm_g_pre_ffn': out['m_g_pre_ffn'], 'm_w_gate': out['m_w_gate'], 'm_w_up': out['m_w_up'], 'm_w_down': out['m_w_down'], 'm_g_post_ffn': out['m_g_post_ffn'], 'v_g_pre_mix': out['v_g_pre_mix'], 'v_w_in': out['v_w_in'], 'v_conv_w': out['v_conv_w'], 'v_conv_b': out['v_conv_b'], 'v_conv_ln_g': out['v_conv_ln_g'], 'v_conv_ln_b': out['v_conv_ln_b'], 'v_attn_norm_g': out['v_attn_norm_g'], 'v_w_out': out['v_w_out'], 'v_g_post_mix': out['v_g_post_mix'], 'v_g_pre_ffn': out['v_g_pre_ffn'], 'v_w_gate': out['v_w_gate'], 'v_w_up': out['v_w_up'], 'v_w_down': out['v_w_down'], 'v_g_post_ffn': out['v_g_post_ffn']}


def _loss(weights, diff, rest, loss_target):
    with _jax.named_scope("forward"):
        args = {**rest, TWIN_DIFF_INPUT: diff, **{k: w.astype(_WEIGHT_DTYPES[k]) for k, w in weights.items()}}
        y = _forward(args)
    with _jax.named_scope("loss_head"):
        err = _jnp.square(y.astype(_jnp.float32) - loss_target)
        return 0.5 * _jnp.sum(_jnp.mean(err, axis=-1)) if err.ndim else 0.5 * err


def _adamw(w, g, m, v):
    m = ADAM_B1 * m + (1.0 - ADAM_B1) * g
    v = ADAM_B2 * v + (1.0 - ADAM_B2) * _jnp.square(g)
    m_hat = m / (1.0 - ADAM_B1 ** ADAM_STEP)
    v_hat = v / (1.0 - ADAM_B2 ** ADAM_STEP)
    delta = -ADAM_LR * (m_hat / (_jnp.sqrt(v_hat) + ADAM_EPS) + ADAM_WD * w)
    return delta, m, v


def reference(x, g_pre_mix, w_in, conv_w, conv_b, conv_ln_g, conv_ln_b, attn_norm_g, w_out, g_post_mix, g_pre_ffn, w_gate, w_up, w_down, g_post_ffn, loss_target, m_g_pre_mix, m_w_in, m_conv_w, m_conv_b, m_conv_ln_g, m_conv_ln_b, m_attn_norm_g, m_w_out, m_g_post_mix, m_g_pre_ffn, m_w_gate, m_w_up, m_w_down, m_g_post_ffn, v_g_pre_mix, v_w_in, v_conv_w, v_conv_b, v_conv_ln_g, v_conv_ln_b, v_attn_norm_g, v_w_out, v_g_post_mix, v_g_pre_ffn, v_w_gate, v_w_up, v_w_down, v_g_post_ffn):
    given = dict(x=x, g_pre_mix=g_pre_mix, w_in=w_in, conv_w=conv_w, conv_b=conv_b, conv_ln_g=conv_ln_g, conv_ln_b=conv_ln_b, attn_norm_g=attn_norm_g, w_out=w_out, g_post_mix=g_post_mix, g_pre_ffn=g_pre_ffn, w_gate=w_gate, w_up=w_up, w_down=w_down, g_post_ffn=g_post_ffn, loss_target=loss_target, m_g_pre_mix=m_g_pre_mix, m_w_in=m_w_in, m_conv_w=m_conv_w, m_conv_b=m_conv_b, m_conv_ln_g=m_conv_ln_g, m_conv_ln_b=m_conv_ln_b, m_attn_norm_g=m_attn_norm_g, m_w_out=m_w_out, m_g_post_mix=m_g_post_mix, m_g_pre_ffn=m_g_pre_ffn, m_w_gate=m_w_gate, m_w_up=m_w_up, m_w_down=m_w_down, m_g_post_ffn=m_g_post_ffn, v_g_pre_mix=v_g_pre_mix, v_w_in=v_w_in, v_conv_w=v_conv_w, v_conv_b=v_conv_b, v_conv_ln_g=v_conv_ln_g, v_conv_ln_b=v_conv_ln_b, v_attn_norm_g=v_attn_norm_g, v_w_out=v_w_out, v_g_post_mix=v_g_post_mix, v_g_pre_ffn=v_g_pre_ffn, v_w_gate=v_w_gate, v_w_up=v_w_up, v_w_down=v_w_down, v_g_post_ffn=v_g_post_ffn)
    weights = {n: given[n] for n in TWIN_WEIGHTS}
    shared = {n: given[n] for n in SHARED_INPUTS}
    per_example = {n: given[n] for n in ['x']}
    grad_fn = _jax.value_and_grad(_loss, argnums=(0, 1))

    def one_microbatch(ex, loss_target):
        ex = dict(ex)
        diff = ex.pop(TWIN_DIFF_INPUT)
        return grad_fn(weights, diff, {**shared, **ex}, loss_target)

    if N_MICROBATCH == 1:
        loss, (grad_w, grad_x) = one_microbatch(per_example, given["loss_target"])
    else:
        def body(carry, xs):
            loss_sum, grad_sum = carry
            l_k, (gw_k, gx_k) = one_microbatch(xs[0], xs[1])
            with _jax.named_scope("update"):
                return (loss_sum + l_k, _jax.tree.map(_jnp.add, grad_sum, gw_k)), gx_k

        init = (_jnp.zeros((), _jnp.float32), _jax.tree.map(_jnp.zeros_like, weights))
        (loss, grad_w), grad_x = _jax.lax.scan(body, init, (per_example, given["loss_target"]))
    with _jax.named_scope("update"):
        delta_w, new_m, new_v = {}, {}, {}
        for n in TWIN_WEIGHTS:
            delta_w[n], new_m[n], new_v[n] = _adamw(weights[n], grad_w[n], given["m_" + n], given["v_" + n])
    return (loss, grad_x, *[grad_w[n] for n in TWIN_WEIGHTS], *[delta_w[n] for n in TWIN_WEIGHTS],
            *[new_m[n] for n in TWIN_WEIGHTS], *[new_v[n] for n in TWIN_WEIGHTS])
```

```python
import functools
import math

import jax
import jax.numpy as jnp
from jax import lax
from jax.experimental import pallas as pl
from jax.experimental.pallas import tpu as pltpu

F32 = jnp.float32
BF16 = jnp.bfloat16
MESH = pl.DeviceIdType.MESH

D_MODEL = 1024
CONV_CH = 512
CONV_WIDTH = 31
HEAD_DIM = 64
PAIR = 2 * HEAD_DIM
N_PAIRS = 4
D_FF = 2816
N_CHIPS = 4
IN_SH = 2560 // N_CHIPS
FF_SH = D_FF // N_CHIPS
OUT_SH = D_MODEL // N_CHIPS
C_ROWS = OUT_SH + FF_SH
EPS = 1e-6
HALO = 32

ADAM_LR = 0.001
ADAM_B1 = 0.9
ADAM_B2 = 0.999
ADAM_EPS = 1e-08
ADAM_WD = 0.01
ADAM_STEP = 10

VMEM_LIMIT = 56 * 2 ** 20


def _cp(sem=None, vmem=VMEM_LIMIT):
    return pltpu.CompilerParams(dimension_semantics=sem, vmem_limit_bytes=vmem)


def _hbm():
    return pl.BlockSpec(memory_space=pltpu.HBM)


def _const_spec(shape):
    nd = len(shape)
    return pl.BlockSpec(shape, lambda *_: (0,) * nd, pipeline_mode=pl.Buffered(1))


def _dot(a, b):
    return jnp.dot(a, b, preferred_element_type=F32)


def _dot_nt(a, b):
    return lax.dot_general(a, b, (((1,), (1,)), ((), ())), preferred_element_type=F32)


def _dot_tn(a, b):
    return lax.dot_general(a, b, (((0,), (0,)), ((), ())), preferred_element_type=F32)


def _split3(x):
    b0 = x.astype(BF16)
    r1 = x - b0.astype(F32)
    b1 = r1.astype(BF16)
    b2 = (r1 - b1.astype(F32)).astype(BF16)
    return b0, b1, b2


def _split2(x):
    hi = x.astype(BF16)
    lo = (x - hi.astype(F32)).astype(BF16)
    return hi, lo


def _sigmoid(x):
    return 1.0 / (1.0 + jnp.exp(-x))


def _head_mean(x, seg):
    b0, b1, b2 = _split3(x)
    return (_dot(b0, seg) + _dot(b1, seg) + _dot(b2, seg)) * (1.0 / HEAD_DIM)


def _seg_matrix(n):
    r = lax.broadcasted_iota(jnp.int32, (n, n), 0) // HEAD_DIM
    c = lax.broadcasted_iota(jnp.int32, (n, n), 1) // HEAD_DIM
    return (r == c).astype(BF16)


def _rms(x):
    return lax.rsqrt(jnp.mean(x * x, axis=-1, keepdims=True) + EPS)


def _rms_bwd(dy, n, r, g):
    dn = dy * g
    dx = r * (dn - n * jnp.mean(dn * n, axis=-1, keepdims=True))
    return dx, dy * n


def _gather_weights(a_sh, b_sh, c_sh, w_sh):
    ha, hc, hw = a_sh.shape[0] // 2, c_sh.shape[0] // 2, w_sh.shape[0] // 2

    def body(a_ref, b_ref, c_ref, w_ref, ao, bo, co, wo, lsem, ssem, rsem):
        x, y, c = lax.axis_index("x"), lax.axis_index("y"), lax.axis_index("c")
        me = 2 * x + y
        sibling = (x, y, 1 - c)
        chips = [(1 - x, y), (x, 1 - y), (1 - x, 1 - y)]

        def src_half(i, h):
            return [a_ref.at[pl.ds(h * ha, ha), :], b_ref.at[h], c_ref.at[pl.ds(h * hc, hc), :],
                    w_ref.at[pl.ds(h * hw, hw), :]][i]

        def out_half(i, j, h):
            return [ao.at[j, pl.ds(h * ha, ha), :], bo.at[j, h], co.at[j, pl.ds(h * hc, hc), :],
                    wo.at[j, pl.ds(h * hw, hw), :]][i]

        srcs = [a_ref, b_ref, c_ref, w_ref]
        outs = [ao, bo, co, wo]
        local = [pltpu.make_async_copy(srcs[i], outs[i].at[me], lsem.at[i]) for i in range(4)]
        for cp in local:
            cp.start()

        def ici(i, k, chip, origin):
            return pltpu.make_async_remote_copy(
                src_ref=src_half(i, c), dst_ref=out_half(i, origin, c),
                send_sem=ssem.at[6 * i + k], recv_sem=rsem.at[6 * i + k],
                device_id=(chip[0], chip[1], c), device_id_type=MESH)

        def d2d(i, k, origin, h):
            return pltpu.make_async_remote_copy(
                src_ref=out_half(i, origin, h), dst_ref=out_half(i, origin, h),
                send_sem=ssem.at[6 * i + 3 + k], recv_sem=rsem.at[6 * i + 3 + k],
                device_id=sibling, device_id_type=MESH)

        sends = []
        for i in range(4):
            for k, chip in enumerate(chips):
                cp = ici(i, k, chip, me)
                cp.start()
                sends.append(cp)
        for i in range(4):
            for k, chip in enumerate(chips):
                origin = 2 * chip[0] + chip[1]
                ici(i, k, chip, origin).wait_recv()
                cp = d2d(i, k, origin, c)
                cp.start()
                sends.append(cp)
        for i in range(4):
            for k, chip in enumerate(chips):
                origin = 2 * chip[0] + chip[1]
                d2d(i, k, origin, 1 - c).wait_recv()
        for cp in sends:
            cp.wait_send()
        for cp in local:
            cp.wait()

    shapes = [jax.ShapeDtypeStruct((N_CHIPS,) + s.shape, s.dtype) for s in (a_sh, b_sh, c_sh, w_sh)]
    return pl.pallas_call(
        body, name="gather_weights", out_shape=shapes,
        in_specs=[_hbm()] * 4, out_specs=[_hbm()] * 4,
        scratch_shapes=[pltpu.SemaphoreType.DMA((4,)), pltpu.SemaphoreType.DMA((24,)),
                        pltpu.SemaphoreType.DMA((24,))],
    )(a_sh, b_sh, c_sh, w_sh)


def _in_proj(x2, g1, wa, tm):
    S = x2.shape[0]

    def body(x_ref, g_ref, w_ref, a_ref, uc_ref, qkv_ref):
        x = x_ref[...]
        a = (x * _rms(x) * g_ref[...]).astype(BF16)
        a_ref[...] = a
        uc_ref[...] = _dot(a, w_ref[:, 0:2 * CONV_CH])
        qkv_ref[...] = _dot(a, w_ref[:, 2 * CONV_CH:]).astype(BF16)

    return pl.pallas_call(
        body, name="in_proj", grid=(S // tm,),
        in_specs=[pl.BlockSpec((tm, D_MODEL), lambda i: (i, 0)), _const_spec((1, D_MODEL)),
                  _const_spec(wa.shape)],
        out_specs=[pl.BlockSpec((tm, D_MODEL), lambda i: (i, 0)),
                   pl.BlockSpec((tm, 2 * CONV_CH), lambda i: (i, 0)),
                   pl.BlockSpec((tm, 1536), lambda i: (i, 0))],
        out_shape=[jax.ShapeDtypeStruct((S, D_MODEL), BF16), jax.ShapeDtypeStruct((S, 2 * CONV_CH), F32),
                   jax.ShapeDtypeStruct((S, 1536), BF16)],
        compiler_params=_cp(("parallel",)),
    )(x2, g1, wa)


def _conv_taps(cw_ref, src_ref, off, rows):
    acc = cw_ref[0:1, :] * src_ref[pl.ds(off, rows), :]
    for w in range(1, CONV_WIDTH):
        acc = acc + cw_ref[w:w + 1, :] * src_ref[pl.ds(off + w, rows), :]
    return acc


def _glu(uc):
    return uc[:, :CONV_CH] * _sigmoid(uc[:, CONV_CH:])


def _conv_fwd(uc, cwf, cb, lg, lb, tm):
    S = uc.shape[0]
    hb = tm // HALO

    def body(uc_ref, prev_ref, cw_ref, cb_ref, lg_ref, lb_ref, out_ref, glu_ref):
        i = pl.program_id(0)
        glu_ref[0:HALO, :] = jnp.where(i == 0, 0.0, _glu(prev_ref[...]))
        glu_ref[HALO:HALO + tm, :] = _glu(uc_ref[...])
        y = _conv_taps(cw_ref, glu_ref, HALO - (CONV_WIDTH - 1), tm) + cb_ref[...]
        mu = jnp.mean(y, axis=-1, keepdims=True)
        yc = y - mu
        rstd = lax.rsqrt(jnp.mean(yc * yc, axis=-1, keepdims=True) + EPS)
        ln = yc * rstd * lg_ref[...] + lb_ref[...]
        out_ref[...] = (ln * _sigmoid(ln)).astype(BF16)

    return pl.pallas_call(
        body, name="conv_fwd", grid=(S // tm,),
        in_specs=[pl.BlockSpec((tm, 2 * CONV_CH), lambda i: (i, 0)),
                  pl.BlockSpec((HALO, 2 * CONV_CH), lambda i: (jnp.maximum(i * hb - 1, 0), 0)),
                  _const_spec(cwf.shape), _const_spec((1, CONV_CH)), _const_spec((1, CONV_CH)),
                  _const_spec((1, CONV_CH))],
        out_specs=pl.BlockSpec((tm, CONV_CH), lambda i: (i, 0)),
        out_shape=jax.ShapeDtypeStruct((S, CONV_CH), BF16),
        scratch_shapes=[pltpu.VMEM((HALO + tm, CONV_CH), F32)],
        compiler_params=_cp(("parallel",)),
    )(uc, uc, cwf, cb, lg, lb)


def _lane_mask(h):
    lane = lax.broadcasted_iota(jnp.int32, (1, PAIR), 1)
    return (lane >= HEAD_DIM * h) & (lane < HEAD_DIM * (h + 1))


def _sb_tile(qm, kt, r, tri, m_suf):
    z = _dot_nt(qm, kt)
    e = jnp.exp(-jnp.abs(z))
    L = -(jnp.maximum(z, 0.0) + jnp.log(1.0 + e))
    if tri is not None:
        L = jnp.where(tri, L, 0.0)
    hi, lo = _split2(L)
    cl = _dot(hi, m_suf) + _dot(lo, m_suf)
    ex = z + cl + r
    if tri is not None:
        ex = jnp.where(tri, ex, -1e30)
    return z, L, jnp.exp(ex), e


def _attn_fwd(qkv, t):
    S = qkv.shape[0]

    def body(q_ref, k_ref, v_ref, o_ref, acc_ref, r_ref):
        i = pl.program_id(1)
        row = lax.broadcasted_iota(jnp.int32, (t, t), 0)
        col = lax.broadcasted_iota(jnp.int32, (t, t), 1)
        m_suf = (row >= col).astype(BF16)
        tri = col < row
        q = q_ref[...]
        acc_ref[...] = jnp.zeros_like(acc_ref)
        for h in range(2):
            hm = _lane_mask(h)
            qm = jnp.where(hm, q, 0) * 0.125
            r_ref[...] = jnp.zeros_like(r_ref)

            def tile(kb, masked, qm=qm, hm=hm):
                ks = pl.multiple_of(kb * t, t)
                kt = k_ref[pl.ds(ks, t), :]
                vt = jnp.where(hm, v_ref[pl.ds(ks, t), :], 0)
                _, L, A, _ = _sb_tile(qm, kt, r_ref[...], tri if masked else None, m_suf)
                acc_ref[...] += _dot(A.astype(BF16), vt)
                r_ref[...] += jnp.sum(L, axis=1, keepdims=True)

            tile(i, True)

            def step(n, carry, tile=tile):
                tile(i - 1 - n, False)
                return carry

            lax.fori_loop(0, i, step, 0)
        o_ref[...] = acc_ref[...]

    return pl.pallas_call(
        body, name="attn_fwd", grid=(N_PAIRS, S // t),
        in_specs=[pl.BlockSpec((t, PAIR), lambda p, i: (i, p)),
                  pl.BlockSpec((S, PAIR), lambda p, i: (0, N_PAIRS + p)),
                  pl.BlockSpec((S, PAIR), lambda p, i: (0, 2 * N_PAIRS + p))],
        out_specs=pl.BlockSpec((t, PAIR), lambda p, i: (i, p)),
        out_shape=jax.ShapeDtypeStruct((S, N_PAIRS * PAIR), F32),
        scratch_shapes=[pltpu.VMEM((t, PAIR), F32), pltpu.VMEM((t, 1), F32)],
        compiler_params=_cp(("parallel", "arbitrary")),
    )(qkv, qkv, qkv)


def _out_proj(conv_out, o, ag, wc, x2, g2, g3, tm):
    S = o.shape[0]

    def body(co_ref, o_ref, ag_ref, w_ref, x_ref, g2_ref, g3_ref, mix_ref, y_ref, h1_ref, fin_ref):
        seg = _seg_matrix(CONV_CH)
        o = o_ref[...]
        att = (o * lax.rsqrt(_head_mean(o * o, seg) + EPS) * ag_ref[...]).astype(BF16)
        co = co_ref[...]
        mix_ref[:, :CONV_CH] = co
        mix_ref[:, CONV_CH:] = att
        y = _dot(co, w_ref[0:CONV_CH, :]) + _dot(att, w_ref[CONV_CH:, :])
        y_ref[...] = y
        h1 = x_ref[...] + y * _rms(y) * g2_ref[...]
        h1_ref[...] = h1
        fin_ref[...] = (h1 * _rms(h1) * g3_ref[...]).astype(BF16)

    row = lambda w: pl.BlockSpec((tm, w), lambda i: (i, 0))
    return pl.pallas_call(
        body, name="out_proj", grid=(S // tm,),
        in_specs=[row(CONV_CH), row(CONV_CH), _const_spec((1, CONV_CH)), _const_spec(wc.shape),
                  row(D_MODEL), _const_spec((1, D_MODEL)), _const_spec((1, D_MODEL))],
        out_specs=[row(D_MODEL)] * 4,
        out_shape=[jax.ShapeDtypeStruct((S, D_MODEL), BF16), jax.ShapeDtypeStruct((S, D_MODEL), F32),
                   jax.ShapeDtypeStruct((S, D_MODEL), F32), jax.ShapeDtypeStruct((S, D_MODEL), BF16)],
        compiler_params=_cp(("parallel",)),
    )(conv_out, o, ag, wc, x2, g2, g3)


def _ffn_fwd(f_in, h1, tgt, wg, wu, wd, g4, tm):
    S = f_in.shape[0]

    def body(fin_ref, h1_ref, tgt_ref, wg_ref, wu_ref, wd_ref, g4_ref, df_ref, dh2_ref, dg4_ref, loss_ref):
        i = pl.program_id(0)
        fin = fin_ref[...]
        gt = _dot(fin, wg_ref[...])
        up = _dot(fin, wu_ref[...])
        f = _dot((gt * _sigmoid(gt) * up).astype(BF16), wd_ref[...])
        r = _rms(f)
        n = f * r
        g4 = g4_ref[...]
        err = h1_ref[...] + n * g4 - tgt_ref[...]
        dh2 = err * (1.0 / D_MODEL)
        dh2_ref[...] = dh2
        df, dg = _rms_bwd(dh2, n, r, g4)
        df_ref[...] = df.astype(BF16)

        @pl.when(i == 0)
        def _():
            dg4_ref[...] = jnp.zeros_like(dg4_ref)
            loss_ref[...] = jnp.zeros_like(loss_ref)

        dg4_ref[...] += jnp.sum(dg, axis=0, keepdims=True)
        part = jnp.sum(jnp.sum(err * err, axis=1, keepdims=True), axis=0, keepdims=True)
        loss_ref[...] += part * (0.5 / D_MODEL)

    row = lambda w: pl.BlockSpec((tm, w), lambda i: (i, 0))
    return pl.pallas_call(
        body, name="ffn_fwd", grid=(S // tm,),
        in_specs=[row(D_MODEL), row(D_MODEL), row(D_MODEL), _const_spec(wg.shape), _const_spec(wu.shape),
                  _const_spec(wd.shape), _const_spec((1, D_MODEL))],
        out_specs=[row(D_MODEL), row(D_MODEL), pl.BlockSpec((1, D_MODEL), lambda i: (0, 0)),
                   pl.BlockSpec((1, 128), lambda i: (0, 0))],
        out_shape=[jax.ShapeDtypeStruct((S, D_MODEL), BF16), jax.ShapeDtypeStruct((S, D_MODEL), F32),
                   jax.ShapeDtypeStruct((1, D_MODEL), F32), jax.ShapeDtypeStruct((1, 128), F32)],
        compiler_params=_cp(("arbitrary",)),
    )(f_in, h1, tgt, wg, wu, wd, g4)


def _ffn_bwd(f_in, df, dh2, h1, yv, wg, wu, wd, g3, g2, tm):
    S = f_in.shape[0]

    def body(fin_ref, df_ref, dh2_ref, h1_ref, y_ref, wg_ref, wu_ref, wd_ref, g3_ref, g2_ref,
             act_ref, dgt_ref, dup_ref, dh1_ref, dy_ref, dg3_ref, dg2_ref):
        i = pl.program_id(0)
        fin = fin_ref[...]
        df = df_ref[...]
        gt = _dot(fin, wg_ref[...])
        up = _dot(fin, wu_ref[...])
        sg = _sigmoid(gt)
        silu = gt * sg
        act_ref[...] = (silu * up).astype(BF16)
        dact = _dot_nt(df, wd_ref[...])
        dgt = (dact * up * (sg * (1.0 + gt * (1.0 - sg)))).astype(BF16)
        dup = (dact * silu).astype(BF16)
        dgt_ref[...] = dgt
        dup_ref[...] = dup
        dfin = _dot_nt(dgt, wg_ref[...]) + _dot_nt(dup, wu_ref[...])
        h1 = h1_ref[...]
        r3 = _rms(h1)
        dh1_n, dg3 = _rms_bwd(dfin, h1 * r3, r3, g3_ref[...])
        dh1 = dh2_ref[...] + dh1_n
        dh1_ref[...] = dh1
        y = y_ref[...]
        r2 = _rms(y)
        dy, dg2 = _rms_bwd(dh1, y * r2, r2, g2_ref[...])
        dy_ref[...] = dy.astype(BF16)

        @pl.when(i == 0)
        def _():
            dg3_ref[...] = jnp.zeros_like(dg3_ref)
            dg2_ref[...] = jnp.zeros_like(dg2_ref)

        dg3_ref[...] += jnp.sum(dg3, axis=0, keepdims=True)
        dg2_ref[...] += jnp.sum(dg2, axis=0, keepdims=True)

    row = lambda w: pl.BlockSpec((tm, w), lambda i: (i, 0))
    vec = pl.BlockSpec((1, D_MODEL), lambda i: (0, 0))
    return pl.pallas_call(
        body, name="ffn_bwd", grid=(S // tm,),
        in_specs=[row(D_MODEL)] * 5 + [_const_spec(wg.shape), _const_spec(wu.shape), _const_spec(wd.shape),
                                       _const_spec((1, D_MODEL)), _const_spec((1, D_MODEL))],
        out_specs=[row(D_FF), row(D_FF), row(D_FF), row(D_MODEL), row(D_MODEL), vec, vec],
        out_shape=[jax.ShapeDtypeStruct((S, D_FF), BF16)] * 3
        + [jax.ShapeDtypeStruct((S, D_MODEL), F32), jax.ShapeDtypeStruct((S, D_MODEL), BF16),
           jax.ShapeDtypeStruct((1, D_MODEL), F32), jax.ShapeDtypeStruct((1, D_MODEL), F32)],
        compiler_params=_cp(("arbitrary",)),
    )(f_in, df, dh2, h1, yv, wg, wu, wd, g3, g2)


def _out_bwd(dy, o, ag, wc, tm):
    S = o.shape[0]

    def body(dy_ref, o_ref, ag_ref, w_ref, dco_ref, do_ref, dag_ref):
        i = pl.program_id(0)
        seg = _seg_matrix(CONV_CH)
        dy = dy_ref[...]
        dco_ref[...] = _dot_nt(dy, w_ref[0:CONV_CH, :])
        datt = _dot_nt(dy, w_ref[CONV_CH:, :])
        o = o_ref[...]
        r = lax.rsqrt(_head_mean(o * o, seg) + EPS)
        n = o * r
        dn = datt * ag_ref[...]
        do_ref[...] = (r * (dn - n * _head_mean(dn * n, seg))).astype(BF16)

        @pl.when(i == 0)
        def _():
            dag_ref[...] = jnp.zeros_like(dag_ref)

        dag_ref[...] += jnp.sum(datt * n, axis=0, keepdims=True)

    row = lambda w: pl.BlockSpec((tm, w), lambda i: (i, 0))
    return pl.pallas_call(
        body, name="out_bwd", grid=(S // tm,),
        in_specs=[row(D_MODEL), row(CONV_CH), _const_spec((1, CONV_CH)), _const_spec(wc.shape)],
        out_specs=[row(CONV_CH), row(CONV_CH), pl.BlockSpec((1, CONV_CH), lambda i: (0, 0))],
        out_shape=[jax.ShapeDtypeStruct((S, CONV_CH), F32), jax.ShapeDtypeStruct((S, CONV_CH), BF16),
                   jax.ShapeDtypeStruct((1, CONV_CH), F32)],
        compiler_params=_cp(("arbitrary",)),
    )(dy, o, ag, wc)


def _attn_bwd(qkv, do, t):
    S = qkv.shape[0]
    nk = S // t

    def body(q_ref, k_ref, v_ref, do_ref, dq_ref, dk_ref, dv_ref, g_buf, s_buf, r_ref, dq_acc):
        i = pl.program_id(1)

        @pl.when(i == 0)
        def _():
            dk_ref[...] = jnp.zeros_like(dk_ref)
            dv_ref[...] = jnp.zeros_like(dv_ref)

        row = lax.broadcasted_iota(jnp.int32, (t, t), 0)
        col = lax.broadcasted_iota(jnp.int32, (t, t), 1)
        m_suf = (row >= col).astype(BF16)
        m_pre = (row <= col).astype(BF16)
        tri = col < row
        q = q_ref[...]
        do = do_ref[...]
        dq_acc[...] = jnp.zeros_like(dq_acc)
        for h in range(2):
            hm = _lane_mask(h)
            qm = jnp.where(hm, q, 0) * 0.125
            dom = jnp.where(hm, do, 0)
            r_ref[...] = jnp.zeros_like(r_ref)

            def sweep1(kb, masked, qm=qm, dom=dom):
                ks = pl.multiple_of(kb * t, t)
                kt = k_ref[pl.ds(ks, t), :]
                vt = v_ref[pl.ds(ks, t), :]
                z, L, A, e = _sb_tile(qm, kt, r_ref[...], tri if masked else None, m_suf)
                g_buf[kb] = A * _dot_nt(dom, vt)
                s_buf[kb] = jnp.where(z >= 0.0, 1.0, e) / (1.0 + e)
                dv_ref[pl.ds(ks, t), :] += _dot_tn(A.astype(BF16), dom)
                r_ref[...] += jnp.sum(L, axis=1, keepdims=True)

            sweep1(i, True)

            def step1(n, carry, sweep1=sweep1):
                sweep1(i - 1 - n, False)
                return carry

            lax.fori_loop(0, i, step1, 0)
            r_ref[...] = jnp.zeros_like(r_ref)

            def sweep2(kb, masked, qm=qm, hm=hm):
                ks = pl.multiple_of(kb * t, t)
                g = g_buf[kb]
                hi, lo = _split2(g)
                p = _dot(hi, m_pre) + _dot(lo, m_pre) + r_ref[...]
                dz = g - s_buf[kb] * p
                if masked:
                    dz = jnp.where(tri, dz, 0.0)
                dzb = dz.astype(BF16)
                kt = jnp.where(hm, k_ref[pl.ds(ks, t), :], 0)
                dq_acc[...] += _dot(dzb, kt)
                dk_ref[pl.ds(ks, t), :] += _dot_tn(dzb, qm)
                r_ref[...] += jnp.sum(g, axis=1, keepdims=True)

            def step2(kb, carry, sweep2=sweep2):
                sweep2(kb, False)
                return carry

            lax.fori_loop(0, i, step2, 0)
            sweep2(i, True)
        dq_ref[...] = dq_acc[...] * 0.125

    return pl.pallas_call(
        body, name="attn_bwd", grid=(N_PAIRS, S // t),
        in_specs=[pl.BlockSpec((t, PAIR), lambda p, i: (i, p)),
                  pl.BlockSpec((S, PAIR), lambda p, i: (0, N_PAIRS + p)),
                  pl.BlockSpec((S, PAIR), lambda p, i: (0, 2 * N_PAIRS + p)),
                  pl.BlockSpec((t, PAIR), lambda p, i: (i, p))],
        out_specs=[pl.BlockSpec((t, PAIR), lambda p, i: (i, p)),
                   pl.BlockSpec((S, PAIR), lambda p, i: (0, p)),
                   pl.BlockSpec((S, PAIR), lambda p, i: (0, p))],
        out_shape=[jax.ShapeDtypeStruct((S, N_PAIRS * PAIR), F32)] * 3,
        scratch_shapes=[pltpu.VMEM((nk, t, t), F32), pltpu.VMEM((nk, t, t), F32),
                        pltpu.VMEM((t, 1), F32), pltpu.VMEM((t, PAIR), F32)],
        compiler_params=_cp(("parallel", "arbitrary")),
    )(qkv, qkv, qkv, do)


def _conv_bwd(uc, dco, cwf, cb, lg, lb, tm):
    S = uc.shape[0]
    hb = tm // HALO
    nb = S // tm
    ext = tm + HALO

    def body(uc_ref, prev_ref, next_ref, dco_ref, dnext_ref, cw_ref, cb_ref, lg_ref, lb_ref,
             duc_ref, dcw_ref, dcb_ref, dlg_ref, dlb_ref, glu_ref, dyc_ref):
        i = pl.program_id(0)
        last = i == nb - 1

        @pl.when(i == 0)
        def _():
            for ref in (dcw_ref, dcb_ref, dlg_ref, dlb_ref):
                ref[...] = jnp.zeros_like(ref)

        uc = uc_ref[...]
        glu_ref[0:HALO, :] = jnp.where(i == 0, 0.0, _glu(prev_ref[...]))
        glu_ref[HALO:ext, :] = _glu(uc)
        glu_ref[ext:ext + HALO, :] = _glu(next_ref[...])
        y = _conv_taps(cw_ref, glu_ref, HALO - (CONV_WIDTH - 1), ext) + cb_ref[...]
        mu = jnp.mean(y, axis=-1, keepdims=True)
        yc = y - mu
        rstd = lax.rsqrt(jnp.mean(yc * yc, axis=-1, keepdims=True) + EPS)
        yhat = yc * rstd
        lg = lg_ref[...]
        ln = yhat * lg + lb_ref[...]
        sg = _sigmoid(ln)
        dout = jnp.concatenate([dco_ref[...], jnp.where(last, 0.0, dnext_ref[...])], axis=0)
        dln = dout * (sg * (1.0 + ln * (1.0 - sg)))
        dyh = dln * lg
        dyc = rstd * (dyh - jnp.mean(dyh, axis=-1, keepdims=True)
                      - yhat * jnp.mean(dyh * yhat, axis=-1, keepdims=True))
        dyc_ref[...] = dyc
        dlg_ref[...] += jnp.sum((dln * yhat)[0:tm], axis=0, keepdims=True)
        dlb_ref[...] += jnp.sum(dln[0:tm], axis=0, keepdims=True)
        dcb_ref[...] += jnp.sum(dyc[0:tm], axis=0, keepdims=True)
        dglu = cw_ref[0:1, :] * dyc_ref[pl.ds(CONV_WIDTH - 1, tm), :]
        for w in range(1, CONV_WIDTH):
            dglu = dglu + cw_ref[w:w + 1, :] * dyc_ref[pl.ds(CONV_WIDTH - 1 - w, tm), :]
        d0 = dyc[0:tm]
        for w in range(CONV_WIDTH):
            off = HALO - (CONV_WIDTH - 1) + w
            dcw_ref[w:w + 1, :] += jnp.sum(d0 * glu_ref[pl.ds(off, tm), :], axis=0, keepdims=True)
        val, gate = uc[:, :CONV_CH], uc[:, CONV_CH:]
        sgate = _sigmoid(gate)
        duc_ref[:, :CONV_CH] = (dglu * sgate).astype(BF16)
        duc_ref[:, CONV_CH:] = (dglu * val * sgate * (1.0 - sgate)).astype(BF16)

    vec = pl.BlockSpec((1, CONV_CH), lambda i: (0, 0))
    nxt = lambda i: (jnp.minimum((i + 1) * hb, S // HALO - 1), 0)
    return pl.pallas_call(
        body, name="conv_bwd", grid=(nb,),
        in_specs=[pl.BlockSpec((tm, 2 * CONV_CH), lambda i: (i, 0)),
                  pl.BlockSpec((HALO, 2 * CONV_CH), lambda i: (jnp.maximum(i * hb - 1, 0), 0)),
                  pl.BlockSpec((HALO, 2 * CONV_CH), nxt),
                  pl.BlockSpec((tm, CONV_CH), lambda i: (i, 0)),
                  pl.BlockSpec((HALO, CONV_CH), nxt),
                  _const_spec(cwf.shape), _const_spec((1, CONV_CH)), _const_spec((1, CONV_CH)),
                  _const_spec((1, CONV_CH))],
        out_specs=[pl.BlockSpec((tm, 2 * CONV_CH), lambda i: (i, 0)),
                   pl.BlockSpec(cwf.shape, lambda i: (0, 0)), vec, vec, vec],
        out_shape=[jax.ShapeDtypeStruct((S, 2 * CONV_CH), BF16), jax.ShapeDtypeStruct(cwf.shape, F32)]
        + [jax.ShapeDtypeStruct((1, CONV_CH), F32)] * 3,
        scratch_shapes=[pltpu.VMEM((ext + HALO, CONV_CH), F32), pltpu.VMEM((ext, CONV_CH), F32)],
        compiler_params=_cp(("arbitrary",)),
    )(uc, uc, uc, dco, dco, cwf, cb, lg, lb)


def _in_bwd(duc, dq, dk, dv, x2, dh1, g1, wa, tm):
    S = x2.shape[0]

    def body(duc_ref, dq_ref, dk_ref, dv_ref, x_ref, dh1_ref, g_ref, w_ref, gx_ref, du_ref, dg_ref):
        i = pl.program_id(0)
        du = jnp.concatenate([duc_ref[...], dq_ref[...].astype(BF16), dk_ref[...].astype(BF16),
                              dv_ref[...].astype(BF16)], axis=1)
        du_ref[...] = du
        da = _dot_nt(du, w_ref[...])
        x = x_ref[...]
        r = _rms(x)
        dx, dg = _rms_bwd(da, x * r, r, g_ref[...])
        gx_ref[...] = dh1_ref[...] + dx

        @pl.when(i == 0)
        def _():
            dg_ref[...] = jnp.zeros_like(dg_ref)

        dg_ref[...] += jnp.sum(dg, axis=0, keepdims=True)

    row = lambda w: pl.BlockSpec((tm, w), lambda i: (i, 0))
    return pl.pallas_call(
        body, name="in_bwd", grid=(S // tm,),
        in_specs=[row(2 * CONV_CH), row(CONV_CH), row(CONV_CH), row(CONV_CH), row(D_MODEL), row(D_MODEL),
                  _const_spec((1, D_MODEL)), _const_spec(wa.shape)],
        out_specs=[row(D_MODEL), row(2560), pl.BlockSpec((1, D_MODEL), lambda i: (0, 0))],
        out_shape=[jax.ShapeDtypeStruct((S, D_MODEL), F32), jax.ShapeDtypeStruct((S, 2560), BF16),
                   jax.ShapeDtypeStruct((1, D_MODEL), F32)],
        compiler_params=_cp(("arbitrary",)),
    )(duc, dq, dk, dv, x2, dh1, g1, wa)


def _matmul_tn(xm, ym, tm, tn, ts, name):
    S, M = xm.shape
    N = ym.shape[1]

    def body(x_ref, y_ref, o_ref):
        @pl.when(pl.program_id(2) == 0)
        def _():
            o_ref[...] = jnp.zeros_like(o_ref)

        o_ref[...] += _dot_tn(x_ref[...], y_ref[...])

    return pl.pallas_call(
        body, name=name, grid=(M // tm, N // tn, S // ts),
        in_specs=[pl.BlockSpec((ts, tm), lambda m, n, s: (s, m)), pl.BlockSpec((ts, tn), lambda m, n, s: (s, n))],
        out_specs=pl.BlockSpec((tm, tn), lambda m, n, s: (m, n)),
        out_shape=jax.ShapeDtypeStruct((M, N), F32),
        compiler_params=_cp(("parallel", "parallel", "arbitrary")),
    )(xm, ym)


def _sibling_halves(grads):
    n = len(grads)

    def body(*refs):
        ins, outs, ssem, rsem = refs[:n], refs[n:2 * n], refs[2 * n], refs[2 * n + 1]
        x, y, c = lax.axis_index("x"), lax.axis_index("y"), lax.axis_index("c")
        copies = []
        for k in range(n):
            for j in range(N_CHIPS):
                copies.append(pltpu.make_async_remote_copy(
                    src_ref=ins[k].at[j, 1 - c], dst_ref=outs[k].at[j],
                    send_sem=ssem.at[N_CHIPS * k + j], recv_sem=rsem.at[N_CHIPS * k + j],
                    device_id=(x, y, 1 - c), device_id_type=MESH))
        for cp in copies:
            cp.start()
        for cp in copies:
            cp.wait()

    shapes = [jax.ShapeDtypeStruct((g.shape[0],) + g.shape[2:], F32) for g in grads]
    return pl.pallas_call(
        body, name="grad_sibling_halves", out_shape=shapes,
        in_specs=[_hbm()] * n, out_specs=[_hbm()] * n,
        scratch_shapes=[pltpu.SemaphoreType.DMA((N_CHIPS * n,)), pltpu.SemaphoreType.DMA((N_CHIPS * n,))],
    )(*grads)


def _add_half(c_arr, g, landed, name):
    def body(c_ref, g_ref, l_ref, o_ref):
        o_ref[...] = g_ref[...] + l_ref[...]

    rows, n = g.shape[2], g.shape[3]
    grid = (N_CHIPS,)
    g_spec = pl.BlockSpec((None, None, rows, n), lambda j, c: (j, c[0], 0, 0))
    l_spec = pl.BlockSpec((None, rows, n), lambda j, c: (j, 0, 0))
    return pl.pallas_call(
        body, name=name,
        grid_spec=pltpu.PrefetchScalarGridSpec(num_scalar_prefetch=1, grid=grid, in_specs=[g_spec, l_spec],
                                               out_specs=l_spec),
        out_shape=jax.ShapeDtypeStruct(landed.shape, F32),
        compiler_params=_cp(("parallel",)),
    )(c_arr, g, landed)


def _chip_scatter(parts):
    n = len(parts)

    def piece(ref, j):
        return ref.at[j]

    def body(*refs):
        ins, outs = refs[:n], refs[n:2 * n]
        lsem, ssem, rsem = refs[2 * n:]
        x, y, c = lax.axis_index("x"), lax.axis_index("y"), lax.axis_index("c")
        me = 2 * x + y
        chips = [(1 - x, y), (x, 1 - y), (1 - x, 1 - y)]
        copies = []
        for k in range(n):
            cp = pltpu.make_async_copy(piece(ins[k], me), outs[k].at[me], lsem.at[k])
            cp.start()
            copies.append(cp)
            for r, chip in enumerate(chips):
                cp = pltpu.make_async_remote_copy(
                    src_ref=piece(ins[k], 2 * chip[0] + chip[1]), dst_ref=outs[k].at[me],
                    send_sem=ssem.at[3 * k + r], recv_sem=rsem.at[3 * k + r],
                    device_id=(chip[0], chip[1], c), device_id_type=MESH)
                cp.start()
                copies.append(cp)
        for cp in copies:
            cp.wait()

    shapes = [jax.ShapeDtypeStruct(p.shape, F32) for p in parts]
    return pl.pallas_call(
        body, name="grad_chip_scatter", out_shape=shapes,
        in_specs=[_hbm()] * n, out_specs=[_hbm()] * n,
        scratch_shapes=[pltpu.SemaphoreType.DMA((n,)), pltpu.SemaphoreType.DMA((3 * n,)),
                        pltpu.SemaphoreType.DMA((3 * n,))],
    )(*parts)


def _sum_chips(landed, name):
    _, rows, n = landed.shape
    tr = 256 if rows % 256 == 0 else rows

    def body(a_ref, b_ref, c_ref, d_ref, o_ref):
        o_ref[...] = ((a_ref[...] + b_ref[...]) + c_ref[...]) + d_ref[...]

    specs = [pl.BlockSpec((None, tr, n), functools.partial(lambda i, j: (j, i, 0), j=j)) for j in range(N_CHIPS)]
    return pl.pallas_call(
        body, name=name, grid=(rows // tr,), in_specs=specs,
        out_specs=pl.BlockSpec((tr, n), lambda i: (i, 0)),
        out_shape=jax.ShapeDtypeStruct((rows, n), F32),
        compiler_params=_cp(("parallel",)),
    )(landed, landed, landed, landed)


def _share_halves(halves):
    n = len(halves)

    def body(*refs):
        ins, outs = refs[:n], refs[n:2 * n]
        lsem, ssem, rsem = refs[2 * n:]
        x, y, c = lax.axis_index("x"), lax.axis_index("y"), lax.axis_index("c")
        copies = []
        for k in range(n):
            copies.append(pltpu.make_async_copy(ins[k], outs[k].at[c], lsem.at[k]))
            copies.append(pltpu.make_async_remote_copy(
                src_ref=ins[k], dst_ref=outs[k].at[c], send_sem=ssem.at[k], recv_sem=rsem.at[k],
                device_id=(x, y, 1 - c), device_id_type=MESH))
        for cp in copies:
            cp.start()
        for cp in copies:
            cp.wait()

    shapes = [jax.ShapeDtypeStruct((2,) + h.shape, F32) for h in halves]
    return pl.pallas_call(
        body, name="grad_share_halves", out_shape=shapes,
        in_specs=[_hbm()] * n, out_specs=[_hbm()] * n,
        scratch_shapes=[pltpu.SemaphoreType.DMA((n,)), pltpu.SemaphoreType.DMA((n,)),
                        pltpu.SemaphoreType.DMA((n,))],
    )(*halves)


def _allreduce_small(packed):
    rows, n = packed.shape

    def body(in_ref, out_ref, land_ref, ssem, rsem):
        x, y, c = lax.axis_index("x"), lax.axis_index("y"), lax.axis_index("c")
        me = 4 * x + 2 * y + c
        land_ref[me] = in_ref[...]
        copies = []
        for r in range(1, 8):
            tx = 1 - x if r & 4 else x
            ty = 1 - y if r & 2 else y
            tc = 1 - c if r & 1 else c
            cp = pltpu.make_async_remote_copy(
                src_ref=in_ref, dst_ref=land_ref.at[me], send_sem=ssem.at[r - 1], recv_sem=rsem.at[r - 1],
                device_id=(tx, ty, tc), device_id_type=MESH)
            cp.start()
            copies.append(cp)
        for cp in copies:
            cp.wait()
        acc = land_ref[0]
        for k in range(1, 8):
            acc = acc + land_ref[k]
        out_ref[...] = acc

    return pl.pallas_call(
        body, name="allreduce_small", out_shape=jax.ShapeDtypeStruct((rows, n), F32),
        in_specs=[pl.BlockSpec(memory_space=pltpu.VMEM)], out_specs=pl.BlockSpec(memory_space=pltpu.VMEM),
        scratch_shapes=[pltpu.VMEM((8, rows, n), F32), pltpu.SemaphoreType.DMA((7,)),
                        pltpu.SemaphoreType.DMA((7,))],
    )(packed)


def _adamw(w, g, m, v, name):
    rows, n = w.shape
    tr = 256 if rows % 256 == 0 else rows

    def body(w_ref, g_ref, m_ref, v_ref, d_ref, mo_ref, vo_ref):
        g = g_ref[...]
        m = ADAM_B1 * m_ref[...] + (1.0 - ADAM_B1) * g
        v = ADAM_B2 * v_ref[...] + (1.0 - ADAM_B2) * (g * g)
        m_hat = m / (1.0 - ADAM_B1 ** ADAM_STEP)
        v_hat = v / (1.0 - ADAM_B2 ** ADAM_STEP)
        d_ref[...] = -ADAM_LR * (m_hat / (jnp.sqrt(v_hat) + ADAM_EPS) + ADAM_WD * w_ref[...])
        mo_ref[...] = m
        vo_ref[...] = v

    spec = pl.BlockSpec((tr, n), lambda i: (i, 0))
    return pl.pallas_call(
        body, name=name, grid=(rows // tr,), in_specs=[spec] * 4, out_specs=[spec] * 3,
        out_shape=[jax.ShapeDtypeStruct((rows, n), F32)] * 3,
        compiler_params=_cp(("parallel",)),
    )(w, g, m, v)


def _rows8(a):
    a = a.reshape(-1, 128)
    return jnp.pad(a, ((0, (-a.shape[0]) % 8), (0, 0)))


def kernel(x, g_pre_mix, w_in, conv_w, conv_b, conv_ln_g, conv_ln_b, attn_norm_g, w_out, g_post_mix, g_pre_ffn, w_gate, w_up, w_down, g_post_ffn, loss_target, m_g_pre_mix, m_w_in, m_conv_w, m_conv_b, m_conv_ln_g, m_conv_ln_b, m_attn_norm_g, m_w_out, m_g_post_mix, m_g_pre_ffn, m_w_gate, m_w_up, m_w_down, m_g_post_ffn, v_g_pre_mix, v_w_in, v_conv_w, v_conv_b, v_conv_ln_g, v_conv_ln_b, v_attn_norm_g, v_w_out, v_g_post_mix, v_g_pre_ffn, v_w_gate, v_w_up, v_w_down, v_g_post_ffn):
    S = x.shape[1]
    tm_big = min(512, S)
    tm_ffn = min(256, S)
    t_att = min(256, S)
    chip = 2 * lax.axis_index("x") + lax.axis_index("y")
    core = lax.axis_index("c")
    x2 = x.reshape(S, D_MODEL)
    tgt = loss_target.reshape(S, D_MODEL)
    ag = attn_norm_g.reshape(1, CONV_CH)

    a_sh = w_in[0].astype(BF16)
    b_sh = jnp.stack([w_gate[0], w_up[0]]).astype(BF16)
    c_sh = jnp.concatenate([w_out[0], w_down[0]], axis=0).astype(BF16)
    cw_sh = jnp.pad(conv_w[0, :, 0, :], ((0, 1), (0, 0)))
    wa4, wb4, wc4, cw4 = _gather_weights(a_sh, b_sh, c_sh, cw_sh)
    cols = lambda w4: jnp.transpose(w4, (1, 0, 2)).reshape(w4.shape[1], N_CHIPS * w4.shape[2])
    wa = cols(wa4)
    wg, wu = cols(wb4[:, 0]), cols(wb4[:, 1])
    wo = wc4[:, :OUT_SH].reshape(D_MODEL, D_MODEL)
    wd = wc4[:, OUT_SH:].reshape(D_FF, D_MODEL)
    cwf = cols(cw4)

    a_bf, uc, qkv = _in_proj(x2, g_pre_mix, wa, tm_big)
    conv_out = _conv_fwd(uc, cwf, conv_b, conv_ln_g, conv_ln_b, tm_big)
    o = _attn_fwd(qkv, t_att)
    mixed, yv, h1, f_in = _out_proj(conv_out, o, ag, wo, x2, g_post_mix, g_pre_ffn, tm_big)
    df, dh2, dg4, loss_part = _ffn_fwd(f_in, h1, tgt, wg, wu, wd, g_post_ffn, tm_ffn)

    act, dgt, dup, dh1, dy, dg3, dg2 = _ffn_bwd(f_in, df, dh2, h1, yv, wg, wu, wd, g_pre_ffn, g_post_mix, tm_ffn)
    dco, do, dag = _out_bwd(dy, o, ag, wo, tm_big)
    dq, dk, dv = _attn_bwd(qkv, do, t_att)
    duc, dcw, dcb, dlg, dlb = _conv_bwd(uc, dco, cwf, conv_b, conv_ln_g, conv_ln_b, tm_big)
    grad_x, du, dg1 = _in_bwd(duc, dq, dk, dv, x2, dh1, g_pre_mix, wa, tm_big)
    ts = min(512, S)
    gw_in = _matmul_tn(a_bf, du, D_MODEL, 1280, ts, "grad_w_in")
    gw_out = _matmul_tn(mixed, dy, D_MODEL, D_MODEL, ts, "grad_w_out")
    gw_gate = _matmul_tn(f_in, dgt, D_MODEL, D_FF // 2, ts, "grad_w_gate")
    gw_up = _matmul_tn(f_in, dup, D_MODEL, D_FF // 2, ts, "grad_w_up")
    gw_down = _matmul_tn(act, df, D_FF // 2, D_MODEL, ts, "grad_w_down")

    by_cols = lambda g: jnp.transpose(g.reshape(2, D_MODEL // 2, N_CHIPS, -1), (2, 0, 1, 3))
    by_rows = lambda g: g.reshape(N_CHIPS, 2, g.shape[0] // (2 * N_CHIPS), g.shape[1])
    views = [by_cols(gw_in), by_cols(gw_gate), by_cols(gw_up), by_rows(gw_out), by_rows(gw_down)]
    names = ["w_in", "w_gate", "w_up", "w_out", "w_down"]
    landed = _sibling_halves(views)
    c_arr = core.reshape(1).astype(jnp.int32)
    parts = [_add_half(c_arr, g, l, "grad_half_" + nm) for g, l, nm in zip(views, landed, names)]
    slots = _chip_scatter(parts)
    halves = [_sum_chips(s, "grad_sum_" + nm) for s, nm in zip(slots, names)]
    full = _share_halves(halves)
    g_in, g_gate, g_up, g_out, g_down = [f.reshape(2 * f.shape[1], f.shape[2]) for f in full]

    small = [dg1, dcb, dlg, dlb, dag, dg2, dg3, dg4]
    packed = jnp.concatenate([_rows8(s) for s in small] + [_rows8(dcw), _rows8(loss_part)], axis=0)
    red = _allreduce_small(packed)
    sizes = [D_MODEL, CONV_CH, CONV_CH, CONV_CH, CONV_CH, D_MODEL, D_MODEL, D_MODEL]
    g_small = [red[8 * k:8 * k + n // 128].reshape(1, n) for k, n in enumerate(sizes)]
    cw_red = red[64:64 + 128].reshape(HALO, CONV_CH)
    g_cw = lax.dynamic_slice(cw_red, (0, chip * 128), (HALO, 128))
    loss = red[192, 0]

    big = []
    for w, g, m, v, nm in [(w_in, g_in, m_w_in, v_w_in, "w_in"), (w_out, g_out, m_w_out, v_w_out, "w_out"),
                           (w_gate, g_gate, m_w_gate, v_w_gate, "w_gate"), (w_up, g_up, m_w_up, v_w_up, "w_up"),
                           (w_down, g_down, m_w_down, v_w_down, "w_down")]:
        big.append(_adamw(w[0], g, m[0], v[0], "adamw_" + nm))
    sm_w = [g_pre_mix, conv_b, conv_ln_g, conv_ln_b, ag, g_post_mix, g_pre_ffn, g_post_ffn]
    sm_m = [m_g_pre_mix, m_conv_b, m_conv_ln_g, m_conv_ln_b, m_attn_norm_g, m_g_post_mix, m_g_pre_ffn, m_g_post_ffn]
    sm_v = [v_g_pre_mix, v_conv_b, v_conv_ln_g, v_conv_ln_b, v_attn_norm_g, v_g_post_mix, v_g_pre_ffn, v_g_post_ffn]
    pad_cw = lambda a: jnp.pad(a[0, :, 0, :], ((0, 1), (0, 0)))

    def pack(vecs, cw):
        return jnp.concatenate([_rows8(a) for a in vecs] + [cw], axis=0)

    sd, smn, svn = _adamw(pack(sm_w, pad_cw(conv_w)), pack(g_small, g_cw), pack(sm_m, pad_cw(m_conv_w)),
                          pack(sm_v, pad_cw(v_conv_w)), "adamw_small")

    def unpack(p):
        vecs = [p[8 * k:8 * k + n // 128].reshape(1, n) for k, n in enumerate(sizes)]
        return vecs, p[64:64 + CONV_WIDTH].reshape(1, CONV_WIDTH, 1, 128)

    def ordered(vecs, cw, w_in_, w_out_, w_gate_, w_up_, w_down_):
        g1_, cb_, lg_, lb_, ag_, g2_, g3_, g4_ = vecs
        return [g1_, w_in_[None], cw, cb_, lg_, lb_, ag_.reshape(1, 8, HEAD_DIM), w_out_[None], g2_, g3_,
                w_gate_[None], w_up_[None], w_down_[None], g4_]

    grads = ordered(g_small, g_cw[:CONV_WIDTH].reshape(1, CONV_WIDTH, 1, 128), g_in, g_out, g_gate, g_up, g_down)
    outs = []
    for idx, p in enumerate((sd, smn, svn)):
        vecs, cw = unpack(p)
        outs += ordered(vecs, cw, *[b[idx] for b in (big[0], big[1], big[2], big[3], big[4])])
    return (loss, grad_x.reshape(1, S, D_MODEL), *grads, *outs)
```

```python
import functools
import math

import jax
import jax.numpy as jnp
from jax import lax
from jax.experimental import pallas as pl
from jax.experimental.pallas import tpu as pltpu

F32 = jnp.float32
BF16 = jnp.bfloat16
MESH = pl.DeviceIdType.MESH

D_MODEL = 1024
CONV_CH = 512
CONV_WIDTH = 31
HEAD_DIM = 64
PAIR = 2 * HEAD_DIM
N_PAIRS = 4
D_FF = 2816
N_CHIPS = 4
IN_SH = 2560 // N_CHIPS
FF_SH = D_FF // N_CHIPS
OUT_SH = D_MODEL // N_CHIPS
C_ROWS = OUT_SH + FF_SH
EPS = 1e-6
HALO = 32

ADAM_LR = 0.001
ADAM_B1 = 0.9
ADAM_B2 = 0.999
ADAM_EPS = 1e-08
ADAM_WD = 0.01
ADAM_STEP = 10

VMEM_LIMIT = 56 * 2 ** 20


def _cp(sem=None, vmem=VMEM_LIMIT):
    return pltpu.CompilerParams(dimension_semantics=sem, vmem_limit_bytes=vmem)


def _hbm():
    return pl.BlockSpec(memory_space=pltpu.HBM)


def _const_spec(shape):
    nd = len(shape)
    return pl.BlockSpec(shape, lambda *_: (0,) * nd, pipeline_mode=pl.Buffered(1))


def _dot(a, b):
    return jnp.dot(a, b, preferred_element_type=F32)


def _dot_nt(a, b):
    return lax.dot_general(a, b, (((1,), (1,)), ((), ())), preferred_element_type=F32)


def _dot_tn(a, b):
    return lax.dot_general(a, b, (((0,), (0,)), ((), ())), preferred_element_type=F32)


def _split3(x):
    b0 = x.astype(BF16)
    r1 = x - b0.astype(F32)
    b1 = r1.astype(BF16)
    b2 = (r1 - b1.astype(F32)).astype(BF16)
    return b0, b1, b2


def _split2(x):
    hi = x.astype(BF16)
    lo = (x - hi.astype(F32)).astype(BF16)
    return hi, lo


def _sigmoid(x):
    return 1.0 / (1.0 + jnp.exp(-x))


def _head_mean(x, seg):
    b0, b1, b2 = _split3(x)
    return (_dot(b0, seg) + _dot(b1, seg) + _dot(b2, seg)) * (1.0 / HEAD_DIM)


def _seg_matrix(n):
    r = lax.broadcasted_iota(jnp.int32, (n, n), 0) // HEAD_DIM
    c = lax.broadcasted_iota(jnp.int32, (n, n), 1) // HEAD_DIM
    return (r == c).astype(BF16)


def _rms(x):
    return lax.rsqrt(jnp.mean(x * x, axis=-1, keepdims=True) + EPS)


def _rms_bwd(dy, n, r, g):
    dn = dy * g
    dx = r * (dn - n * jnp.mean(dn * n, axis=-1, keepdims=True))
    return dx, dy * n


def _gather_weights(a_sh, b_sh, c_sh, w_sh):
    ha, hc, hw = a_sh.shape[0] // 2, c_sh.shape[0] // 2, w_sh.shape[0] // 2

    def body(a_ref, b_ref, c_ref, w_ref, ao, bo, co, wo, lsem, ssem, rsem):
        x, y, c = lax.axis_index("x"), lax.axis_index("y"), lax.axis_index("c")
        me = 2 * x + y
        sibling = (x, y, 1 - c)
        chips = [(1 - x, y), (x, 1 - y), (1 - x, 1 - y)]

        def src_half(i, h):
            return [a_ref.at[pl.ds(h * ha, ha), :], b_ref.at[h], c_ref.at[pl.ds(h * hc, hc), :],
                    w_ref.at[pl.ds(h * hw, hw), :]][i]

        def out_half(i, j, h):
            return [ao.at[j, pl.ds(h * ha, ha), :], bo.at[j, h], co.at[j, pl.ds(h * hc, hc), :],
                    wo.at[j, pl.ds(h * hw, hw), :]][i]

        srcs = [a_ref, b_ref, c_ref, w_ref]
        outs = [ao, bo, co, wo]
        local = [pltpu.make_async_copy(srcs[i], outs[i].at[me], lsem.at[i]) for i in range(4)]
        for cp in local:
            cp.start()

        def ici(i, k, chip, origin):
            return pltpu.make_async_remote_copy(
                src_ref=src_half(i, c), dst_ref=out_half(i, origin, c),
                send_sem=ssem.at[6 * i + k], recv_sem=rsem.at[6 * i + k],
                device_id=(chip[0], chip[1], c), device_id_type=MESH)

        def d2d(i, k, origin, h):
            return pltpu.make_async_remote_copy(
                src_ref=out_half(i, origin, h), dst_ref=out_half(i, origin, h),
                send_sem=ssem.at[6 * i + 3 + k], recv_sem=rsem.at[6 * i + 3 + k],
                device_id=sibling, device_id_type=MESH)

        sends = []
        for i in range(4):
            for k, chip in enumerate(chips):
                cp = ici(i, k, chip, me)
                cp.start()
                sends.append(cp)
        for i in range(4):
            for k, chip in enumerate(chips):
                origin = 2 * chip[0] + chip[1]
                ici(i, k, chip, origin).wait_recv()
                cp = d2d(i, k, origin, c)
                cp.start()
                sends.append(cp)
        for i in range(4):
            for k, chip in enumerate(chips):
                origin = 2 * chip[0] + chip[1]
                d2d(i, k, origin, 1 - c).wait_recv()
        for cp in sends:
            cp.wait_send()
        for cp in local:
            cp.wait()

    shapes = [jax.ShapeDtypeStruct((N_CHIPS,) + s.shape, s.dtype) for s in (a_sh, b_sh, c_sh, w_sh)]
    return pl.pallas_call(
        body, name="gather_weights", out_shape=shapes,
        in_specs=[_hbm()] * 4, out_specs=[_hbm()] * 4,
        scratch_shapes=[pltpu.SemaphoreType.DMA((4,)), pltpu.SemaphoreType.DMA((24,)),
                        pltpu.SemaphoreType.DMA((24,))],
    )(a_sh, b_sh, c_sh, w_sh)


def _in_proj(x2, g1, wa, tm):
    S = x2.shape[0]

    def body(x_ref, g_ref, w_ref, a_ref, uc_ref, qkv_ref):
        x = x_ref[...]
        a = (x * _rms(x) * g_ref[...]).astype(BF16)
        a_ref[...] = a
        uc_ref[...] = _dot(a, w_ref[:, 0:2 * CONV_CH])
        qkv_ref[...] = _dot(a, w_ref[:, 2 * CONV_CH:]).astype(BF16)

    return pl.pallas_call(
        body, name="in_proj", grid=(S // tm,),
        in_specs=[pl.BlockSpec((tm, D_MODEL), lambda i: (i, 0)), _const_spec((1, D_MODEL)),
                  _const_spec(wa.shape)],
        out_specs=[pl.BlockSpec((tm, D_MODEL), lambda i: (i, 0)),
                   pl.BlockSpec((tm, 2 * CONV_CH), lambda i: (i, 0)),
                   pl.BlockSpec((tm, 1536), lambda i: (i, 0))],
        out_shape=[jax.ShapeDtypeStruct((S, D_MODEL), BF16), jax.ShapeDtypeStruct((S, 2 * CONV_CH), F32),
                   jax.ShapeDtypeStruct((S, 1536), BF16)],
        compiler_params=_cp(("parallel",)),
    )(x2, g1, wa)


def _conv_taps(cw_ref, src_ref, off, rows):
    acc = cw_ref[0:1, :] * src_ref[pl.ds(off, rows), :]
    for w in range(1, CONV_WIDTH):
        acc = acc + cw_ref[w:w + 1, :] * src_ref[pl.ds(off + w, rows), :]
    return acc


def _glu(uc):
    return uc[:, :CONV_CH] * _sigmoid(uc[:, CONV_CH:])


def _conv_fwd(uc, cwf, cb, lg, lb, tm):
    S = uc.shape[0]
    hb = tm // HALO

    def body(uc_ref, prev_ref, cw_ref, cb_ref, lg_ref, lb_ref, out_ref, glu_ref):
        i = pl.program_id(0)
        glu_ref[0:HALO, :] = jnp.where(i == 0, 0.0, _glu(prev_ref[...]))
        glu_ref[HALO:HALO + tm, :] = _glu(uc_ref[...])
        y = _conv_taps(cw_ref, glu_ref, HALO - (CONV_WIDTH - 1), tm) + cb_ref[...]
        mu = jnp.mean(y, axis=-1, keepdims=True)
        yc = y - mu
        rstd = lax.rsqrt(jnp.mean(yc * yc, axis=-1, keepdims=True) + EPS)
        ln = yc * rstd * lg_ref[...] + lb_ref[...]
        out_ref[...] = (ln * _sigmoid(ln)).astype(BF16)

    return pl.pallas_call(
        body, name="conv_fwd", grid=(S // tm,),
        in_specs=[pl.BlockSpec((tm, 2 * CONV_CH), lambda i: (i, 0)),
                  pl.BlockSpec((HALO, 2 * CONV_CH), lambda i: (jnp.maximum(i * hb - 1, 0), 0)),
                  _const_spec(cwf.shape), _const_spec((1, CONV_CH)), _const_spec((1, CONV_CH)),
                  _const_spec((1, CONV_CH))],
        out_specs=pl.BlockSpec((tm, CONV_CH), lambda i: (i, 0)),
        out_shape=jax.ShapeDtypeStruct((S, CONV_CH), BF16),
        scratch_shapes=[pltpu.VMEM((HALO + tm, CONV_CH), F32)],
        compiler_params=_cp(("parallel",)),
    )(uc, uc, cwf, cb, lg, lb)


def _lane_mask(h):
    lane = lax.broadcasted_iota(jnp.int32, (1, PAIR), 1)
    return (lane >= HEAD_DIM * h) & (lane < HEAD_DIM * (h + 1))


def _sb_tile(qm, kt, r, mask, m_suf):
    t = m_suf.shape[0]
    z = _dot_nt(qm, kt)
    e = jnp.exp(-jnp.abs(z))
    L = -(jnp.maximum(z, 0.0) + jnp.log(1.0 + e))
    if mask is not None:
        L = jnp.where(mask, L, 0.0)
    hi, lo = _split2(L)
    rs_r = jnp.sum(L[:, t:], axis=1, keepdims=True)
    c_r = _dot(hi[:, t:], m_suf) + _dot(lo[:, t:], m_suf)
    c_l = _dot(hi[:, :t], m_suf) + _dot(lo[:, :t], m_suf) + rs_r
    ex = z + jnp.concatenate([c_l, c_r], axis=1) + r
    if mask is not None:
        ex = jnp.where(mask, ex, -1e30)
    return z, jnp.exp(ex), e, rs_r + jnp.sum(L[:, :t], axis=1, keepdims=True)


def _causal_mask(i, sb, t):
    row = lax.broadcasted_iota(jnp.int32, (t, 2 * t), 0) + i * t
    col = lax.broadcasted_iota(jnp.int32, (t, 2 * t), 1) + sb * (2 * t)
    return col < row


def _suffix_matrix(t, prefix=False):
    row = lax.broadcasted_iota(jnp.int32, (t, t), 0)
    col = lax.broadcasted_iota(jnp.int32, (t, t), 1)
    return ((row <= col) if prefix else (row >= col)).astype(BF16)


def _attn_fwd(qkv, t):
    S = qkv.shape[0]
    tk = 2 * t

    def body(q_ref, k_ref, v_ref, o_ref, acc_ref, r_ref):
        i = pl.program_id(1)
        last = i // 2
        m_suf = _suffix_matrix(t)
        q = q_ref[...]
        hms = [_lane_mask(h) for h in range(2)]
        qms = [jnp.where(hm, q, 0) * 0.125 for hm in hms]
        acc_ref[...] = jnp.zeros_like(acc_ref)
        r_ref[...] = jnp.zeros_like(r_ref)

        def tile(sb, masked):
            ks = pl.multiple_of(sb * tk, tk)
            kt = k_ref[pl.ds(ks, tk), :]
            vt = v_ref[pl.ds(ks, tk), :]
            mask = _causal_mask(i, sb, t) if masked else None
            for h in range(2):
                _, A, _, rs = _sb_tile(qms[h], kt, r_ref[h], mask, m_suf)
                acc_ref[...] += _dot(A.astype(BF16), jnp.where(hms[h], vt, 0))
                r_ref[h] += rs

        tile(last, True)

        def step(n, carry):
            tile(last - 1 - n, False)
            return carry

        lax.fori_loop(0, last, step, 0)
        o_ref[...] = acc_ref[...]

    return pl.pallas_call(
        body, name="attn_fwd", grid=(N_PAIRS, S // t),
        in_specs=[pl.BlockSpec((t, PAIR), lambda p, i: (i, p)),
                  pl.BlockSpec((S, PAIR), lambda p, i: (0, N_PAIRS + p)),
                  pl.BlockSpec((S, PAIR), lambda p, i: (0, 2 * N_PAIRS + p))],
        out_specs=pl.BlockSpec((t, PAIR), lambda p, i: (i, p)),
        out_shape=jax.ShapeDtypeStruct((S, N_PAIRS * PAIR), F32),
        scratch_shapes=[pltpu.VMEM((t, PAIR), F32), pltpu.VMEM((2, t, 1), F32)],
        compiler_params=_cp(("parallel", "arbitrary")),
    )(qkv, qkv, qkv)


def _out_proj(conv_out, o, ag, wc, x2, g2, g3, tm):
    S = o.shape[0]

    def body(co_ref, o_ref, ag_ref, w_ref, x_ref, g2_ref, g3_ref, mix_ref, y_ref, h1_ref, fin_ref):
        seg = _seg_matrix(CONV_CH)
        o = o_ref[...]
        att = (o * lax.rsqrt(_head_mean(o * o, seg) + EPS) * ag_ref[...]).astype(BF16)
        co = co_ref[...]
        mix_ref[:, :CONV_CH] = co
        mix_ref[:, CONV_CH:] = att
        y = _dot(co, w_ref[0:CONV_CH, :]) + _dot(att, w_ref[CONV_CH:, :])
        y_ref[...] = y
        h1 = x_ref[...] + y * _rms(y) * g2_ref[...]
        h1_ref[...] = h1
        fin_ref[...] = (h1 * _rms(h1) * g3_ref[...]).astype(BF16)

    row = lambda w: pl.BlockSpec((tm, w), lambda i: (i, 0))
    return pl.pallas_call(
        body, name="out_proj", grid=(S // tm,),
        in_specs=[row(CONV_CH), row(CONV_CH), _const_spec((1, CONV_CH)), _const_spec(wc.shape),
                  row(D_MODEL), _const_spec((1, D_MODEL)), _const_spec((1, D_MODEL))],
        out_specs=[row(D_MODEL)] * 4,
        out_shape=[jax.ShapeDtypeStruct((S, D_MODEL), BF16), jax.ShapeDtypeStruct((S, D_MODEL), F32),
                   jax.ShapeDtypeStruct((S, D_MODEL), F32), jax.ShapeDtypeStruct((S, D_MODEL), BF16)],
        compiler_params=_cp(("parallel",)),
    )(conv_out, o, ag, wc, x2, g2, g3)


def _ffn_fwd(f_in, h1, tgt, wg, wu, wd, g4, tm):
    S = f_in.shape[0]

    def body(fin_ref, h1_ref, tgt_ref, wg_ref, wu_ref, wd_ref, g4_ref, df_ref, dh2_ref, dg4_ref, loss_ref):
        i = pl.program_id(0)
        fin = fin_ref[...]
        gt = _dot(fin, wg_ref[...])
        up = _dot(fin, wu_ref[...])
        f = _dot((gt * _sigmoid(gt) * up).astype(BF16), wd_ref[...])
        r = _rms(f)
        n = f * r
        g4 = g4_ref[...]
        err = h1_ref[...] + n * g4 - tgt_ref[...]
        dh2 = err * (1.0 / D_MODEL)
        dh2_ref[...] = dh2
        df, dg = _rms_bwd(dh2, n, r, g4)
        df_ref[...] = df.astype(BF16)

        @pl.when(i == 0)
        def _():
            dg4_ref[...] = jnp.zeros_like(dg4_ref)
            loss_ref[...] = jnp.zeros_like(loss_ref)

        dg4_ref[...] += jnp.sum(dg, axis=0, keepdims=True)
        part = jnp.sum(jnp.sum(err * err, axis=1, keepdims=True), axis=0, keepdims=True)
        loss_ref[...] += part * (0.5 / D_MODEL)

    row = lambda w: pl.BlockSpec((tm, w), lambda i: (i, 0))
    return pl.pallas_call(
        body, name="ffn_fwd", grid=(S // tm,),
        in_specs=[row(D_MODEL), row(D_MODEL), row(D_MODEL), _const_spec(wg.shape), _const_spec(wu.shape),
                  _const_spec(wd.shape), _const_spec((1, D_MODEL))],
        out_specs=[row(D_MODEL), row(D_MODEL), pl.BlockSpec((1, D_MODEL), lambda i: (0, 0)),
                   pl.BlockSpec((1, 128), lambda i: (0, 0))],
        out_shape=[jax.ShapeDtypeStruct((S, D_MODEL), BF16), jax.ShapeDtypeStruct((S, D_MODEL), F32),
                   jax.ShapeDtypeStruct((1, D_MODEL), F32), jax.ShapeDtypeStruct((1, 128), F32)],
        compiler_params=_cp(("arbitrary",)),
    )(f_in, h1, tgt, wg, wu, wd, g4)


def _ffn_bwd(f_in, df, dh2, h1, yv, wg, wu, wd, g3, g2, tm):
    S = f_in.shape[0]

    def body(fin_ref, df_ref, dh2_ref, h1_ref, y_ref, wg_ref, wu_ref, wd_ref, g3_ref, g2_ref,
             act_ref, dgt_ref, dup_ref, dh1_ref, dy_ref, dg3_ref, dg2_ref):
        i = pl.program_id(0)
        fin = fin_ref[...]
        df = df_ref[...]
        gt = _dot(fin, wg_ref[...])
        up = _dot(fin, wu_ref[...])
        sg = _sigmoid(gt)
        silu = gt * sg
        act_ref[...] = (silu * up).astype(BF16)
        dact = _dot_nt(df, wd_ref[...])
        dgt = (dact * up * (sg * (1.0 + gt * (1.0 - sg)))).astype(BF16)
        dup = (dact * silu).astype(BF16)
        dgt_ref[...] = dgt
        dup_ref[...] = dup
        dfin = _dot_nt(dgt, wg_ref[...]) + _dot_nt(dup, wu_ref[...])
        h1 = h1_ref[...]
        r3 = _rms(h1)
        dh1_n, dg3 = _rms_bwd(dfin, h1 * r3, r3, g3_ref[...])
        dh1 = dh2_ref[...] + dh1_n
        dh1_ref[...] = dh1
        y = y_ref[...]
        r2 = _rms(y)
        dy, dg2 = _rms_bwd(dh1, y * r2, r2, g2_ref[...])
        dy_ref[...] = dy.astype(BF16)

        @pl.when(i == 0)
        def _():
            dg3_ref[...] = jnp.zeros_like(dg3_ref)
            dg2_ref[...] = jnp.zeros_like(dg2_ref)

        dg3_ref[...] += jnp.sum(dg3, axis=0, keepdims=True)
        dg2_ref[...] += jnp.sum(dg2, axis=0, keepdims=True)

    row = lambda w: pl.BlockSpec((tm, w), lambda i: (i, 0))
    vec = pl.BlockSpec((1, D_MODEL), lambda i: (0, 0))
    return pl.pallas_call(
        body, name="ffn_bwd", grid=(S // tm,),
        in_specs=[row(D_MODEL)] * 5 + [_const_spec(wg.shape), _const_spec(wu.shape), _const_spec(wd.shape),
                                       _const_spec((1, D_MODEL)), _const_spec((1, D_MODEL))],
        out_specs=[row(D_FF), row(D_FF), row(D_FF), row(D_MODEL), row(D_MODEL), vec, vec],
        out_shape=[jax.ShapeDtypeStruct((S, D_FF), BF16)] * 3
        + [jax.ShapeDtypeStruct((S, D_MODEL), F32), jax.ShapeDtypeStruct((S, D_MODEL), BF16),
           jax.ShapeDtypeStruct((1, D_MODEL), F32), jax.ShapeDtypeStruct((1, D_MODEL), F32)],
        compiler_params=_cp(("arbitrary",)),
    )(f_in, df, dh2, h1, yv, wg, wu, wd, g3, g2)


def _out_bwd(dy, o, ag, wc, tm):
    S = o.shape[0]

    def body(dy_ref, o_ref, ag_ref, w_ref, dco_ref, do_ref, dag_ref):
        i = pl.program_id(0)
        seg = _seg_matrix(CONV_CH)
        dy = dy_ref[...]
        dco_ref[...] = _dot_nt(dy, w_ref[0:CONV_CH, :])
        datt = _dot_nt(dy, w_ref[CONV_CH:, :])
        o = o_ref[...]
        r = lax.rsqrt(_head_mean(o * o, seg) + EPS)
        n = o * r
        dn = datt * ag_ref[...]
        do_ref[...] = (r * (dn - n * _head_mean(dn * n, seg))).astype(BF16)

        @pl.when(i == 0)
        def _():
            dag_ref[...] = jnp.zeros_like(dag_ref)

        dag_ref[...] += jnp.sum(datt * n, axis=0, keepdims=True)

    row = lambda w: pl.BlockSpec((tm, w), lambda i: (i, 0))
    return pl.pallas_call(
        body, name="out_bwd", grid=(S // tm,),
        in_specs=[row(D_MODEL), row(CONV_CH), _const_spec((1, CONV_CH)), _const_spec(wc.shape)],
        out_specs=[row(CONV_CH), row(CONV_CH), pl.BlockSpec((1, CONV_CH), lambda i: (0, 0))],
        out_shape=[jax.ShapeDtypeStruct((S, CONV_CH), F32), jax.ShapeDtypeStruct((S, CONV_CH), BF16),
                   jax.ShapeDtypeStruct((1, CONV_CH), F32)],
        compiler_params=_cp(("arbitrary",)),
    )(dy, o, ag, wc)


def _attn_bwd(qkv, do, t):
    S = qkv.shape[0]
    tk = 2 * t
    nk = S // tk

    def body(q_ref, k_ref, v_ref, do_ref, dq_ref, dk_hbm, dv_hbm, g_buf, s_buf, r_ref, dq_acc, dk_ref, dv_ref):
        p = pl.program_id(0)
        i = pl.program_id(1)
        last = i // 2

        @pl.when(i == 0)
        def _():
            dk_ref[...] = jnp.zeros_like(dk_ref)
            dv_ref[...] = jnp.zeros_like(dv_ref)

        m_suf = _suffix_matrix(t)
        m_pre = _suffix_matrix(t, prefix=True)
        q = q_ref[...]
        do = do_ref[...]
        hms = [_lane_mask(h) for h in range(2)]
        qms = [jnp.where(hm, q, 0) * 0.125 for hm in hms]
        doms = [jnp.where(hm, do, 0) for hm in hms]
        dq_acc[...] = jnp.zeros_like(dq_acc)
        r_ref[...] = jnp.zeros_like(r_ref)

        def sweep1(sb, masked):
            ks = pl.multiple_of(sb * tk, tk)
            kt = k_ref[pl.ds(ks, tk), :]
            vt = v_ref[pl.ds(ks, tk), :]
            mask = _causal_mask(i, sb, t) if masked else None
            dv = jnp.zeros((tk, PAIR), F32)
            for h in range(2):
                z, A, e, rs = _sb_tile(qms[h], kt, r_ref[h], mask, m_suf)
                g_buf[h, sb] = A * _dot_nt(doms[h], vt)
                s_buf[h, sb] = jnp.where(z >= 0.0, 1.0, e) / (1.0 + e)
                dv = dv + _dot_tn(A.astype(BF16), doms[h])
                r_ref[h] += rs
            dv_ref[pl.ds(ks, tk), :] += dv

        sweep1(last, True)

        def step1(n, carry):
            sweep1(last - 1 - n, False)
            return carry

        lax.fori_loop(0, last, step1, 0)
        r_ref[...] = jnp.zeros_like(r_ref)

        def sweep2(sb, masked):
            ks = pl.multiple_of(sb * tk, tk)
            kt = k_ref[pl.ds(ks, tk), :]
            mask = _causal_mask(i, sb, t) if masked else None
            dk = jnp.zeros((tk, PAIR), F32)
            for h in range(2):
                g = g_buf[h, sb]
                hi, lo = _split2(g)
                rs_l = jnp.sum(g[:, :t], axis=1, keepdims=True)
                p_l = _dot(hi[:, :t], m_pre) + _dot(lo[:, :t], m_pre)
                p_r = _dot(hi[:, t:], m_pre) + _dot(lo[:, t:], m_pre) + rs_l
                dz = g - s_buf[h, sb] * (jnp.concatenate([p_l, p_r], axis=1) + r_ref[h])
                if masked:
                    dz = jnp.where(mask, dz, 0.0)
                dzb = dz.astype(BF16)
                dq_acc[...] += _dot(dzb, jnp.where(hms[h], kt, 0))
                dk = dk + _dot_tn(dzb, qms[h])
                r_ref[h] += rs_l + jnp.sum(g[:, t:], axis=1, keepdims=True)
            dk_ref[pl.ds(ks, tk), :] += dk

        def step2(sb, carry):
            sweep2(sb, False)
            return carry

        lax.fori_loop(0, last, step2, 0)
        sweep2(last, True)
        dq_ref[...] = dq_acc[...] * 0.125

        @pl.when(i == S // t - 1)
        def _():
            cols = pl.ds(pl.multiple_of(p * PAIR, PAIR), PAIR)
            pltpu.sync_copy(dk_ref, dk_hbm.at[:, cols])
            pltpu.sync_copy(dv_ref, dv_hbm.at[:, cols])

    once = lambda cb: pl.BlockSpec((S, PAIR), cb, pipeline_mode=pl.Buffered(1))
    return pl.pallas_call(
        body, name="attn_bwd", grid=(N_PAIRS, S // t),
        in_specs=[pl.BlockSpec((t, PAIR), lambda p, i: (i, p)),
                  once(lambda p, i: (0, N_PAIRS + p)), once(lambda p, i: (0, 2 * N_PAIRS + p)),
                  pl.BlockSpec((t, PAIR), lambda p, i: (i, p))],
        out_specs=[pl.BlockSpec((t, PAIR), lambda p, i: (i, p)), _hbm(), _hbm()],
        out_shape=[jax.ShapeDtypeStruct((S, N_PAIRS * PAIR), F32)] * 3,
        scratch_shapes=[pltpu.VMEM((2, nk, t, tk), F32), pltpu.VMEM((2, nk, t, tk), F32),
                        pltpu.VMEM((2, t, 1), F32), pltpu.VMEM((t, PAIR), F32),
                        pltpu.VMEM((S, PAIR), F32), pltpu.VMEM((S, PAIR), F32)],
        compiler_params=_cp(("arbitrary", "arbitrary")),
    )(qkv, qkv, qkv, do)


def _conv_bwd(uc, dco, cwf, cb, lg, lb, tm):
    S = uc.shape[0]
    hb = tm // HALO
    nb = S // tm
    ext = tm + HALO

    def body(uc_ref, prev_ref, next_ref, dco_ref, dnext_ref, cw_ref, cb_ref, lg_ref, lb_ref,
             duc_ref, dcw_ref, dcb_ref, dlg_ref, dlb_ref, glu_ref, dyc_ref):
        i = pl.program_id(0)
        last = i == nb - 1

        @pl.when(i == 0)
        def _():
            for ref in (dcw_ref, dcb_ref, dlg_ref, dlb_ref):
                ref[...] = jnp.zeros_like(ref)

        uc = uc_ref[...]
        glu_ref[0:HALO, :] = jnp.where(i == 0, 0.0, _glu(prev_ref[...]))
        glu_ref[HALO:ext, :] = _glu(uc)
        glu_ref[ext:ext + HALO, :] = _glu(next_ref[...])
        y = _conv_taps(cw_ref, glu_ref, HALO - (CONV_WIDTH - 1), ext) + cb_ref[...]
        mu = jnp.mean(y, axis=-1, keepdims=True)
        yc = y - mu
        rstd = lax.rsqrt(jnp.mean(yc * yc, axis=-1, keepdims=True) + EPS)
        yhat = yc * rstd
        lg = lg_ref[...]
        ln = yhat * lg + lb_ref[...]
        sg = _sigmoid(ln)
        dout = jnp.concatenate([dco_ref[...], jnp.where(last, 0.0, dnext_ref[...])], axis=0)
        dln = dout * (sg * (1.0 + ln * (1.0 - sg)))
        dyh = dln * lg
        dyc = rstd * (dyh - jnp.mean(dyh, axis=-1, keepdims=True)
                      - yhat * jnp.mean(dyh * yhat, axis=-1, keepdims=True))
        dyc_ref[...] = dyc
        dlg_ref[...] += jnp.sum((dln * yhat)[0:tm], axis=0, keepdims=True)
        dlb_ref[...] += jnp.sum(dln[0:tm], axis=0, keepdims=True)
        dcb_ref[...] += jnp.sum(dyc[0:tm], axis=0, keepdims=True)
        dglu = cw_ref[0:1, :] * dyc_ref[pl.ds(CONV_WIDTH - 1, tm), :]
        for w in range(1, CONV_WIDTH):
            dglu = dglu + cw_ref[w:w + 1, :] * dyc_ref[pl.ds(CONV_WIDTH - 1 - w, tm), :]
        d0 = dyc[0:tm]
        for w in range(CONV_WIDTH):
            off = HALO - (CONV_WIDTH - 1) + w
            dcw_ref[w:w + 1, :] += jnp.sum(d0 * glu_ref[pl.ds(off, tm), :], axis=0, keepdims=True)
        val, gate = uc[:, :CONV_CH], uc[:, CONV_CH:]
        sgate = _sigmoid(gate)
        duc_ref[:, :CONV_CH] = (dglu * sgate).astype(BF16)
        duc_ref[:, CONV_CH:] = (dglu * val * sgate * (1.0 - sgate)).astype(BF16)

    vec = pl.BlockSpec((1, CONV_CH), lambda i: (0, 0))
    nxt = lambda i: (jnp.minimum((i + 1) * hb, S // HALO - 1), 0)
    return pl.pallas_call(
        body, name="conv_bwd", grid=(nb,),
        in_specs=[pl.BlockSpec((tm, 2 * CONV_CH), lambda i: (i, 0)),
                  pl.BlockSpec((HALO, 2 * CONV_CH), lambda i: (jnp.maximum(i * hb - 1, 0), 0)),
                  pl.BlockSpec((HALO, 2 * CONV_CH), nxt),
                  pl.BlockSpec((tm, CONV_CH), lambda i: (i, 0)),
                  pl.BlockSpec((HALO, CONV_CH), nxt),
                  _const_spec(cwf.shape), _const_spec((1, CONV_CH)), _const_spec((1, CONV_CH)),
                  _const_spec((1, CONV_CH))],
        out_specs=[pl.BlockSpec((tm, 2 * CONV_CH), lambda i: (i, 0)),
                   pl.BlockSpec(cwf.shape, lambda i: (0, 0)), vec, vec, vec],
        out_shape=[jax.ShapeDtypeStruct((S, 2 * CONV_CH), BF16), jax.ShapeDtypeStruct(cwf.shape, F32)]
        + [jax.ShapeDtypeStruct((1, CONV_CH), F32)] * 3,
        scratch_shapes=[pltpu.VMEM((ext + HALO, CONV_CH), F32), pltpu.VMEM((ext, CONV_CH), F32)],
        compiler_params=_cp(("arbitrary",)),
    )(uc, uc, uc, dco, dco, cwf, cb, lg, lb)


def _in_bwd(duc, dq, dk, dv, x2, dh1, g1, wa, tm):
    S = x2.shape[0]

    def body(duc_ref, dq_ref, dk_ref, dv_ref, x_ref, dh1_ref, g_ref, w_ref, gx_ref, du_ref, dg_ref):
        i = pl.program_id(0)
        du = jnp.concatenate([duc_ref[...], dq_ref[...].astype(BF16), dk_ref[...].astype(BF16),
                              dv_ref[...].astype(BF16)], axis=1)
        du_ref[...] = du
        da = _dot_nt(du, w_ref[...])
        x = x_ref[...]
        r = _rms(x)
        dx, dg = _rms_bwd(da, x * r, r, g_ref[...])
        gx_ref[...] = dh1_ref[...] + dx

        @pl.when(i == 0)
        def _():
            dg_ref[...] = jnp.zeros_like(dg_ref)

        dg_ref[...] += jnp.sum(dg, axis=0, keepdims=True)

    row = lambda w: pl.BlockSpec((tm, w), lambda i: (i, 0))
    return pl.pallas_call(
        body, name="in_bwd", grid=(S // tm,),
        in_specs=[row(2 * CONV_CH), row(CONV_CH), row(CONV_CH), row(CONV_CH), row(D_MODEL), row(D_MODEL),
                  _const_spec((1, D_MODEL)), _const_spec(wa.shape)],
        out_specs=[row(D_MODEL), row(2560), pl.BlockSpec((1, D_MODEL), lambda i: (0, 0))],
        out_shape=[jax.ShapeDtypeStruct((S, D_MODEL), F32), jax.ShapeDtypeStruct((S, 2560), BF16),
                   jax.ShapeDtypeStruct((1, D_MODEL), F32)],
        compiler_params=_cp(("arbitrary",)),
    )(duc, dq, dk, dv, x2, dh1, g1, wa)


def _matmul_tn(xm, ym, tm, tn, ts, name):
    S, M = xm.shape
    N = ym.shape[1]

    def body(x_ref, y_ref, o_ref):
        @pl.when(pl.program_id(2) == 0)
        def _():
            o_ref[...] = jnp.zeros_like(o_ref)

        o_ref[...] += _dot_tn(x_ref[...], y_ref[...])

    return pl.pallas_call(
        body, name=name, grid=(M // tm, N // tn, S // ts),
        in_specs=[pl.BlockSpec((ts, tm), lambda m, n, s: (s, m)), pl.BlockSpec((ts, tn), lambda m, n, s: (s, n))],
        out_specs=pl.BlockSpec((tm, tn), lambda m, n, s: (m, n)),
        out_shape=jax.ShapeDtypeStruct((M, N), F32),
        compiler_params=_cp(("parallel", "parallel", "arbitrary")),
    )(xm, ym)


def _sibling_halves(grads):
    n = len(grads)

    def body(*refs):
        ins, outs, ssem, rsem = refs[:n], refs[n:2 * n], refs[2 * n], refs[2 * n + 1]
        x, y, c = lax.axis_index("x"), lax.axis_index("y"), lax.axis_index("c")
        copies = []
        for k in range(n):
            for j in range(N_CHIPS):
                copies.append(pltpu.make_async_remote_copy(
                    src_ref=ins[k].at[j, 1 - c], dst_ref=outs[k].at[j],
                    send_sem=ssem.at[N_CHIPS * k + j], recv_sem=rsem.at[N_CHIPS * k + j],
                    device_id=(x, y, 1 - c), device_id_type=MESH))
        for cp in copies:
            cp.start()
        for cp in copies:
            cp.wait()

    shapes = [jax.ShapeDtypeStruct((g.shape[0],) + g.shape[2:], F32) for g in grads]
    return pl.pallas_call(
        body, name="grad_sibling_halves", out_shape=shapes,
        in_specs=[_hbm()] * n, out_specs=[_hbm()] * n,
        scratch_shapes=[pltpu.SemaphoreType.DMA((N_CHIPS * n,)), pltpu.SemaphoreType.DMA((N_CHIPS * n,))],
    )(*grads)


def _add_half(c_arr, g, landed, name):
    def body(c_ref, g_ref, l_ref, o_ref):
        o_ref[...] = g_ref[...] + l_ref[...]

    rows, n = g.shape[2], g.shape[3]
    grid = (N_CHIPS,)
    g_spec = pl.BlockSpec((None, None, rows, n), lambda j, c: (j, c[0], 0, 0))
    l_spec = pl.BlockSpec((None, rows, n), lambda j, c: (j, 0, 0))
    return pl.pallas_call(
        body, name=name,
        grid_spec=pltpu.PrefetchScalarGridSpec(num_scalar_prefetch=1, grid=grid, in_specs=[g_spec, l_spec],
                                               out_specs=l_spec),
        out_shape=jax.ShapeDtypeStruct(landed.shape, F32),
        compiler_params=_cp(("parallel",)),
    )(c_arr, g, landed)


def _chip_scatter(parts):
    n = len(parts)

    def piece(ref, j):
        return ref.at[j]

    def body(*refs):
        ins, outs = refs[:n], refs[n:2 * n]
        lsem, ssem, rsem = refs[2 * n:]
        x, y, c = lax.axis_index("x"), lax.axis_index("y"), lax.axis_index("c")
        me = 2 * x + y
        chips = [(1 - x, y), (x, 1 - y), (1 - x, 1 - y)]
        copies = []
        for k in range(n):
            cp = pltpu.make_async_copy(piece(ins[k], me), outs[k].at[me], lsem.at[k])
            cp.start()
            copies.append(cp)
            for r, chip in enumerate(chips):
                cp = pltpu.make_async_remote_copy(
                    src_ref=piece(ins[k], 2 * chip[0] + chip[1]), dst_ref=outs[k].at[me],
                    send_sem=ssem.at[3 * k + r], recv_sem=rsem.at[3 * k + r],
                    device_id=(chip[0], chip[1], c), device_id_type=MESH)
                cp.start()
                copies.append(cp)
        for cp in copies:
            cp.wait()

    shapes = [jax.ShapeDtypeStruct(p.shape, F32) for p in parts]
    return pl.pallas_call(
        body, name="grad_chip_scatter", out_shape=shapes,
        in_specs=[_hbm()] * n, out_specs=[_hbm()] * n,
        scratch_shapes=[pltpu.SemaphoreType.DMA((n,)), pltpu.SemaphoreType.DMA((3 * n,)),
                        pltpu.SemaphoreType.DMA((3 * n,))],
    )(*parts)


def _sum_chips(landed, name):
    _, rows, n = landed.shape
    tr = 256 if rows % 256 == 0 else rows

    def body(a_ref, b_ref, c_ref, d_ref, o_ref):
        o_ref[...] = ((a_ref[...] + b_ref[...]) + c_ref[...]) + d_ref[...]

    specs = [pl.BlockSpec((None, tr, n), functools.partial(lambda i, j: (j, i, 0), j=j)) for j in range(N_CHIPS)]
    return pl.pallas_call(
        body, name=name, grid=(rows // tr,), in_specs=specs,
        out_specs=pl.BlockSpec((tr, n), lambda i: (i, 0)),
        out_shape=jax.ShapeDtypeStruct((rows, n), F32),
        compiler_params=_cp(("parallel",)),
    )(landed, landed, landed, landed)


def _share_halves(halves):
    n = len(halves)

    def body(*refs):
        ins, outs = refs[:n], refs[n:2 * n]
        lsem, ssem, rsem = refs[2 * n:]
        x, y, c = lax.axis_index("x"), lax.axis_index("y"), lax.axis_index("c")
        copies = []
        for k in range(n):
            copies.append(pltpu.make_async_copy(ins[k], outs[k].at[c], lsem.at[k]))
            copies.append(pltpu.make_async_remote_copy(
                src_ref=ins[k], dst_ref=outs[k].at[c], send_sem=ssem.at[k], recv_sem=rsem.at[k],
                device_id=(x, y, 1 - c), device_id_type=MESH))
        for cp in copies:
            cp.start()
        for cp in copies:
            cp.wait()

    shapes = [jax.ShapeDtypeStruct((2,) + h.shape, F32) for h in halves]
    return pl.pallas_call(
        body, name="grad_share_halves", out_shape=shapes,
        in_specs=[_hbm()] * n, out_specs=[_hbm()] * n,
        scratch_shapes=[pltpu.SemaphoreType.DMA((n,)), pltpu.SemaphoreType.DMA((n,)),
                        pltpu.SemaphoreType.DMA((n,))],
    )(*halves)


def _allreduce_small(packed):
    rows, n = packed.shape

    def body(in_ref, out_ref, land_ref, ssem, rsem):
        x, y, c = lax.axis_index("x"), lax.axis_index("y"), lax.axis_index("c")
        me = 4 * x + 2 * y + c
        land_ref[me] = in_ref[...]
        copies = []
        for r in range(1, 8):
            tx = 1 - x if r & 4 else x
            ty = 1 - y if r & 2 else y
            tc = 1 - c if r & 1 else c
            cp = pltpu.make_async_remote_copy(
                src_ref=in_ref, dst_ref=land_ref.at[me], send_sem=ssem.at[r - 1], recv_sem=rsem.at[r - 1],
                device_id=(tx, ty, tc), device_id_type=MESH)
            cp.start()
            copies.append(cp)
        for cp in copies:
            cp.wait()
        acc = land_ref[0]
        for k in range(1, 8):
            acc = acc + land_ref[k]
        out_ref[...] = acc

    return pl.pallas_call(
        body, name="allreduce_small", out_shape=jax.ShapeDtypeStruct((rows, n), F32),
        in_specs=[pl.BlockSpec(memory_space=pltpu.VMEM)], out_specs=pl.BlockSpec(memory_space=pltpu.VMEM),
        scratch_shapes=[pltpu.VMEM((8, rows, n), F32), pltpu.SemaphoreType.DMA((7,)),
                        pltpu.SemaphoreType.DMA((7,))],
    )(packed)


def _adamw(w, g, m, v, name):
    rows, n = w.shape
    tr = 256 if rows % 256 == 0 else rows

    def body(w_ref, g_ref, m_ref, v_ref, d_ref, mo_ref, vo_ref):
        g = g_ref[...]
        m = ADAM_B1 * m_ref[...] + (1.0 - ADAM_B1) * g
        v = ADAM_B2 * v_ref[...] + (1.0 - ADAM_B2) * (g * g)
        m_hat = m / (1.0 - ADAM_B1 ** ADAM_STEP)
        v_hat = v / (1.0 - ADAM_B2 ** ADAM_STEP)
        d_ref[...] = -ADAM_LR * (m_hat / (jnp.sqrt(v_hat) + ADAM_EPS) + ADAM_WD * w_ref[...])
        mo_ref[...] = m
        vo_ref[...] = v

    spec = pl.BlockSpec((tr, n), lambda i: (i, 0))
    return pl.pallas_call(
        body, name=name, grid=(rows // tr,), in_specs=[spec] * 4, out_specs=[spec] * 3,
        out_shape=[jax.ShapeDtypeStruct((rows, n), F32)] * 3,
        compiler_params=_cp(("parallel",)),
    )(w, g, m, v)


def _rows8(a):
    a = a.reshape(-1, 128)
    return jnp.pad(a, ((0, (-a.shape[0]) % 8), (0, 0)))


def kernel(x, g_pre_mix, w_in, conv_w, conv_b, conv_ln_g, conv_ln_b, attn_norm_g, w_out, g_post_mix, g_pre_ffn, w_gate, w_up, w_down, g_post_ffn, loss_target, m_g_pre_mix, m_w_in, m_conv_w, m_conv_b, m_conv_ln_g, m_conv_ln_b, m_attn_norm_g, m_w_out, m_g_post_mix, m_g_pre_ffn, m_w_gate, m_w_up, m_w_down, m_g_post_ffn, v_g_pre_mix, v_w_in, v_conv_w, v_conv_b, v_conv_ln_g, v_conv_ln_b, v_attn_norm_g, v_w_out, v_g_post_mix, v_g_pre_ffn, v_w_gate, v_w_up, v_w_down, v_g_post_ffn):
    S = x.shape[1]
    tm_big = min(512, S)
    tm_ffn = min(256, S)
    t_att = min(256, S)
    chip = 2 * lax.axis_index("x") + lax.axis_index("y")
    core = lax.axis_index("c")
    x2 = x.reshape(S, D_MODEL)
    tgt = loss_target.reshape(S, D_MODEL)
    ag = attn_norm_g.reshape(1, CONV_CH)

    a_sh = w_in[0].astype(BF16)
    b_sh = jnp.stack([w_gate[0], w_up[0]]).astype(BF16)
    c_sh = jnp.concatenate([w_out[0], w_down[0]], axis=0).astype(BF16)
    cw_sh = jnp.pad(conv_w[0, :, 0, :], ((0, 1), (0, 0)))
    wa4, wb4, wc4, cw4 = _gather_weights(a_sh, b_sh, c_sh, cw_sh)
    cols = lambda w4: jnp.transpose(w4, (1, 0, 2)).reshape(w4.shape[1], N_CHIPS * w4.shape[2])
    wa = cols(wa4)
    wg, wu = cols(wb4[:, 0]), cols(wb4[:, 1])
    wo = wc4[:, :OUT_SH].reshape(D_MODEL, D_MODEL)
    wd = wc4[:, OUT_SH:].reshape(D_FF, D_MODEL)
    cwf = cols(cw4)

    a_bf, uc, qkv = _in_proj(x2, g_pre_mix, wa, tm_big)
    conv_out = _conv_fwd(uc, cwf, conv_b, conv_ln_g, conv_ln_b, tm_big)
    o = _attn_fwd(qkv, t_att)
    mixed, yv, h1, f_in = _out_proj(conv_out, o, ag, wo, x2, g_post_mix, g_pre_ffn, tm_big)
    df, dh2, dg4, loss_part = _ffn_fwd(f_in, h1, tgt, wg, wu, wd, g_post_ffn, tm_ffn)

    act, dgt, dup, dh1, dy, dg3, dg2 = _ffn_bwd(f_in, df, dh2, h1, yv, wg, wu, wd, g_pre_ffn, g_post_mix, tm_ffn)
    dco, do, dag = _out_bwd(dy, o, ag, wo, tm_big)
    dq, dk, dv = _attn_bwd(qkv, do, t_att)
    duc, dcw, dcb, dlg, dlb = _conv_bwd(uc, dco, cwf, conv_b, conv_ln_g, conv_ln_b, tm_big)
    grad_x, du, dg1 = _in_bwd(duc, dq, dk, dv, x2, dh1, g_pre_mix, wa, tm_big)
    ts = min(512, S)
    gw_in = _matmul_tn(a_bf, du, D_MODEL, 1280, ts, "grad_w_in")
    gw_out = _matmul_tn(mixed, dy, D_MODEL, D_MODEL, ts, "grad_w_out")
    gw_gate = _matmul_tn(f_in, dgt, D_MODEL, D_FF // 2, ts, "grad_w_gate")
    gw_up = _matmul_tn(f_in, dup, D_MODEL, D_FF // 2, ts, "grad_w_up")
    gw_down = _matmul_tn(act, df, D_FF // 2, D_MODEL, ts, "grad_w_down")

    by_cols = lambda g: jnp.transpose(g.reshape(2, D_MODEL // 2, N_CHIPS, -1), (2, 0, 1, 3))
    by_rows = lambda g: g.reshape(N_CHIPS, 2, g.shape[0] // (2 * N_CHIPS), g.shape[1])
    views = [by_cols(gw_in), by_cols(gw_gate), by_cols(gw_up), by_rows(gw_out), by_rows(gw_down)]
    names = ["w_in", "w_gate", "w_up", "w_out", "w_down"]
    landed = _sibling_halves(views)
    c_arr = core.reshape(1).astype(jnp.int32)
    parts = [_add_half(c_arr, g, l, "grad_half_" + nm) for g, l, nm in zip(views, landed, names)]
    slots = _chip_scatter(parts)
    halves = [_sum_chips(s, "grad_sum_" + nm) for s, nm in zip(slots, names)]
    full = _share_halves(halves)
    g_in, g_gate, g_up, g_out, g_down = [f.reshape(2 * f.shape[1], f.shape[2]) for f in full]

    small = [dg1, dcb, dlg, dlb, dag, dg2, dg3, dg4]
    packed = jnp.concatenate([_rows8(s) for s in small] + [_rows8(dcw), _rows8(loss_part)], axis=0)
    red = _allreduce_small(packed)
    sizes = [D_MODEL, CONV_CH, CONV_CH, CONV_CH, CONV_CH, D_MODEL, D_MODEL, D_MODEL]
    g_small = [red[8 * k:8 * k + n // 128].reshape(1, n) for k, n in enumerate(sizes)]
    cw_red = red[64:64 + 128].reshape(HALO, CONV_CH)
    g_cw = lax.dynamic_slice(cw_red, (0, chip * 128), (HALO, 128))
    loss = red[192, 0]

    big = []
    for w, g, m, v, nm in [(w_in, g_in, m_w_in, v_w_in, "w_in"), (w_out, g_out, m_w_out, v_w_out, "w_out"),
                           (w_gate, g_gate, m_w_gate, v_w_gate, "w_gate"), (w_up, g_up, m_w_up, v_w_up, "w_up"),
                           (w_down, g_down, m_w_down, v_w_down, "w_down")]:
        big.append(_adamw(w[0], g, m[0], v[0], "adamw_" + nm))
    sm_w = [g_pre_mix, conv_b, conv_ln_g, conv_ln_b, ag, g_post_mix, g_pre_ffn, g_post_ffn]
    sm_m = [m_g_pre_mix, m_conv_b, m_conv_ln_g, m_conv_ln_b, m_attn_norm_g, m_g_post_mix, m_g_pre_ffn, m_g_post_ffn]
    sm_v = [v_g_pre_mix, v_conv_b, v_conv_ln_g, v_conv_ln_b, v_attn_norm_g, v_g_post_mix, v_g_pre_ffn, v_g_post_ffn]
    pad_cw = lambda a: jnp.pad(a[0, :, 0, :], ((0, 1), (0, 0)))

    def pack(vecs, cw):
        return jnp.concatenate([_rows8(a) for a in vecs] + [cw], axis=0)

    sd, smn, svn = _adamw(pack(sm_w, pad_cw(conv_w)), pack(g_small, g_cw), pack(sm_m, pad_cw(m_conv_w)),
                          pack(sm_v, pad_cw(v_conv_w)), "adamw_small")

    def unpack(p):
        vecs = [p[8 * k:8 * k + n // 128].reshape(1, n) for k, n in enumerate(sizes)]
        return vecs, p[64:64 + CONV_WIDTH].reshape(1, CONV_WIDTH, 1, 128)

    def ordered(vecs, cw, w_in_, w_out_, w_gate_, w_up_, w_down_):
        g1_, cb_, lg_, lb_, ag_, g2_, g3_, g4_ = vecs
        return [g1_, w_in_[None], cw, cb_, lg_, lb_, ag_.reshape(1, 8, HEAD_DIM), w_out_[None], g2_, g3_,
                w_gate_[None], w_up_[None], w_down_[None], g4_]

    grads = ordered(g_small, g_cw[:CONV_WIDTH].reshape(1, CONV_WIDTH, 1, 128), g_in, g_out, g_gate, g_up, g_down)
    outs = []
    for idx, p in enumerate((sd, smn, svn)):
        vecs, cw = unpack(p)
        outs += ordered(vecs, cw, *[b[idx] for b in (big[0], big[1], big[2], big[3], big[4])])
    return (loss, grad_x.reshape(1, S, D_MODEL), *grads, *outs)
```

```python
import functools
import math

import jax
import jax.numpy as jnp
from jax import lax
from jax.experimental import pallas as pl
from jax.experimental.pallas import tpu as pltpu

F32 = jnp.float32
BF16 = jnp.bfloat16
MESH = pl.DeviceIdType.MESH

D_MODEL = 1024
CONV_CH = 512
CONV_WIDTH = 31
HEAD_DIM = 64
PAIR = 2 * HEAD_DIM
N_PAIRS = 4
D_FF = 2816
N_CHIPS = 4
IN_SH = 2560 // N_CHIPS
FF_SH = D_FF // N_CHIPS
OUT_SH = D_MODEL // N_CHIPS
C_ROWS = OUT_SH + FF_SH
EPS = 1e-6
HALO = 32

ADAM_LR = 0.001
ADAM_B1 = 0.9
ADAM_B2 = 0.999
ADAM_EPS = 1e-08
ADAM_WD = 0.01
ADAM_STEP = 10

VMEM_LIMIT = 56 * 2 ** 20


def _cp(sem=None, vmem=VMEM_LIMIT):
    return pltpu.CompilerParams(dimension_semantics=sem, vmem_limit_bytes=vmem)


def _hbm():
    return pl.BlockSpec(memory_space=pltpu.HBM)


def _const_spec(shape):
    nd = len(shape)
    return pl.BlockSpec(shape, lambda *_: (0,) * nd, pipeline_mode=pl.Buffered(1))


def _dot(a, b):
    return jnp.dot(a, b, preferred_element_type=F32)


def _dot_nt(a, b):
    return lax.dot_general(a, b, (((1,), (1,)), ((), ())), preferred_element_type=F32)


def _dot_tn(a, b):
    return lax.dot_general(a, b, (((0,), (0,)), ((), ())), preferred_element_type=F32)


def _split3(x):
    b0 = x.astype(BF16)
    r1 = x - b0.astype(F32)
    b1 = r1.astype(BF16)
    b2 = (r1 - b1.astype(F32)).astype(BF16)
    return b0, b1, b2


def _split2(x):
    hi = x.astype(BF16)
    lo = (x - hi.astype(F32)).astype(BF16)
    return hi, lo


def _sigmoid(x):
    return 1.0 / (1.0 + jnp.exp(-x))


def _head_mean(x, seg):
    b0, b1, b2 = _split3(x)
    return (_dot(b0, seg) + _dot(b1, seg) + _dot(b2, seg)) * (1.0 / HEAD_DIM)


def _seg_matrix(n):
    r = lax.broadcasted_iota(jnp.int32, (n, n), 0) // HEAD_DIM
    c = lax.broadcasted_iota(jnp.int32, (n, n), 1) // HEAD_DIM
    return (r == c).astype(BF16)


def _rms(x):
    return lax.rsqrt(jnp.mean(x * x, axis=-1, keepdims=True) + EPS)


def _rms_bwd(dy, n, r, g):
    dn = dy * g
    dx = r * (dn - n * jnp.mean(dn * n, axis=-1, keepdims=True))
    return dx, dy * n


def _gather_weights(a_sh, b_sh, c_sh, w_sh):
    ha, hc, hw = a_sh.shape[0] // 2, c_sh.shape[0] // 2, w_sh.shape[0] // 2

    def body(a_ref, b_ref, c_ref, w_ref, ao, bo, co, wo, lsem, ssem, rsem):
        x, y, c = lax.axis_index("x"), lax.axis_index("y"), lax.axis_index("c")
        me = 2 * x + y
        sibling = (x, y, 1 - c)
        chips = [(1 - x, y), (x, 1 - y), (1 - x, 1 - y)]

        def src_half(i, h):
            return [a_ref.at[pl.ds(h * ha, ha), :], b_ref.at[h], c_ref.at[pl.ds(h * hc, hc), :],
                    w_ref.at[pl.ds(h * hw, hw), :]][i]

        def out_half(i, j, h):
            return [ao.at[j, pl.ds(h * ha, ha), :], bo.at[j, h], co.at[j, pl.ds(h * hc, hc), :],
                    wo.at[j, pl.ds(h * hw, hw), :]][i]

        srcs = [a_ref, b_ref, c_ref, w_ref]
        outs = [ao, bo, co, wo]
        local = [pltpu.make_async_copy(srcs[i], outs[i].at[me], lsem.at[i]) for i in range(4)]
        for cp in local:
            cp.start()

        def ici(i, k, chip, origin):
            return pltpu.make_async_remote_copy(
                src_ref=src_half(i, c), dst_ref=out_half(i, origin, c),
                send_sem=ssem.at[6 * i + k], recv_sem=rsem.at[6 * i + k],
                device_id=(chip[0], chip[1], c), device_id_type=MESH)

        def d2d(i, k, origin, h):
            return pltpu.make_async_remote_copy(
                src_ref=out_half(i, origin, h), dst_ref=out_half(i, origin, h),
                send_sem=ssem.at[6 * i + 3 + k], recv_sem=rsem.at[6 * i + 3 + k],
                device_id=sibling, device_id_type=MESH)

        sends = []
        for i in range(4):
            for k, chip in enumerate(chips):
                cp = ici(i, k, chip, me)
                cp.start()
                sends.append(cp)
        for i in range(4):
            for k, chip in enumerate(chips):
                origin = 2 * chip[0] + chip[1]
                ici(i, k, chip, origin).wait_recv()
                cp = d2d(i, k, origin, c)
                cp.start()
                sends.append(cp)
        for i in range(4):
            for k, chip in enumerate(chips):
                origin = 2 * chip[0] + chip[1]
                d2d(i, k, origin, 1 - c).wait_recv()
        for cp in sends:
            cp.wait_send()
        for cp in local:
            cp.wait()

    shapes = [jax.ShapeDtypeStruct((N_CHIPS,) + s.shape, s.dtype) for s in (a_sh, b_sh, c_sh, w_sh)]
    return pl.pallas_call(
        body, name="gather_weights", out_shape=shapes,
        in_specs=[_hbm()] * 4, out_specs=[_hbm()] * 4,
        scratch_shapes=[pltpu.SemaphoreType.DMA((4,)), pltpu.SemaphoreType.DMA((24,)),
                        pltpu.SemaphoreType.DMA((24,))],
    )(a_sh, b_sh, c_sh, w_sh)


def _in_proj(x2, g1, wa, tm):
    S = x2.shape[0]

    def body(x_ref, g_ref, w_ref, a_ref, uc_ref, qkv_ref):
        x = x_ref[...]
        a = (x * _rms(x) * g_ref[...]).astype(BF16)
        a_ref[...] = a
        uc_ref[...] = _dot(a, w_ref[:, 0:2 * CONV_CH])
        qkv_ref[...] = _dot(a, w_ref[:, 2 * CONV_CH:]).astype(BF16)

    return pl.pallas_call(
        body, name="in_proj", grid=(S // tm,),
        in_specs=[pl.BlockSpec((tm, D_MODEL), lambda i: (i, 0)), _const_spec((1, D_MODEL)),
                  _const_spec(wa.shape)],
        out_specs=[pl.BlockSpec((tm, D_MODEL), lambda i: (i, 0)),
                   pl.BlockSpec((tm, 2 * CONV_CH), lambda i: (i, 0)),
                   pl.BlockSpec((tm, 1536), lambda i: (i, 0))],
        out_shape=[jax.ShapeDtypeStruct((S, D_MODEL), BF16), jax.ShapeDtypeStruct((S, 2 * CONV_CH), F32),
                   jax.ShapeDtypeStruct((S, 1536), BF16)],
        compiler_params=_cp(("parallel",)),
    )(x2, g1, wa)


def _conv_taps(cw_ref, src_ref, off, rows):
    acc = cw_ref[0:1, :] * src_ref[pl.ds(off, rows), :]
    for w in range(1, CONV_WIDTH):
        acc = acc + cw_ref[w:w + 1, :] * src_ref[pl.ds(off + w, rows), :]
    return acc


def _glu(uc):
    return uc[:, :CONV_CH] * _sigmoid(uc[:, CONV_CH:])


def _conv_fwd(uc, cwf, cb, lg, lb, tm):
    S = uc.shape[0]
    hb = tm // HALO

    def body(uc_ref, prev_ref, cw_ref, cb_ref, lg_ref, lb_ref, out_ref, glu_ref):
        i = pl.program_id(0)
        glu_ref[0:HALO, :] = jnp.where(i == 0, 0.0, _glu(prev_ref[...]))
        glu_ref[HALO:HALO + tm, :] = _glu(uc_ref[...])
        y = _conv_taps(cw_ref, glu_ref, HALO - (CONV_WIDTH - 1), tm) + cb_ref[...]
        mu = jnp.mean(y, axis=-1, keepdims=True)
        yc = y - mu
        rstd = lax.rsqrt(jnp.mean(yc * yc, axis=-1, keepdims=True) + EPS)
        ln = yc * rstd * lg_ref[...] + lb_ref[...]
        out_ref[...] = (ln * _sigmoid(ln)).astype(BF16)

    return pl.pallas_call(
        body, name="conv_fwd", grid=(S // tm,),
        in_specs=[pl.BlockSpec((tm, 2 * CONV_CH), lambda i: (i, 0)),
                  pl.BlockSpec((HALO, 2 * CONV_CH), lambda i: (jnp.maximum(i * hb - 1, 0), 0)),
                  _const_spec(cwf.shape), _const_spec((1, CONV_CH)), _const_spec((1, CONV_CH)),
                  _const_spec((1, CONV_CH))],
        out_specs=pl.BlockSpec((tm, CONV_CH), lambda i: (i, 0)),
        out_shape=jax.ShapeDtypeStruct((S, CONV_CH), BF16),
        scratch_shapes=[pltpu.VMEM((HALO + tm, CONV_CH), F32)],
        compiler_params=_cp(("parallel",)),
    )(uc, uc, cwf, cb, lg, lb)


def _lane_mask(h):
    lane = lax.broadcasted_iota(jnp.int32, (1, PAIR), 1)
    return (lane >= HEAD_DIM * h) & (lane < HEAD_DIM * (h + 1))


def _neg_abs(x):
    bits = lax.bitcast_convert_type(x, jnp.uint32) | jnp.uint32(0x80000000)
    return lax.bitcast_convert_type(bits, F32)


def _tri_dot(x, m):
    return _dot(x.astype(BF16), m)


def _sb_tile(qm, kt, r, mask, m2):
    t = m2.shape[1]
    z = _dot_nt(qm, kt)
    sp = jnp.maximum(z, 0.0) + jnp.log(1.0 + jnp.exp(_neg_abs(z)))
    if mask is not None:
        sp = jnp.where(mask, sp, 0.0)
    rs_r = jnp.sum(sp[:, t:], axis=1, keepdims=True)
    c_r = _tri_dot(sp[:, t:], m2)
    c_l = _tri_dot(sp[:, :t], m2) + rs_r
    ex = z - jnp.concatenate([c_l, c_r], axis=1)
    if r is not None:
        ex = ex - r
    if mask is not None:
        ex = jnp.where(mask, ex, -1e30)
    return jnp.exp(ex), sp, rs_r + jnp.sum(sp[:, :t], axis=1, keepdims=True)


def _causal_mask(i, sb, t):
    row = lax.broadcasted_iota(jnp.int32, (t, 2 * t), 0) + i * t
    col = lax.broadcasted_iota(jnp.int32, (t, 2 * t), 1) + sb * (2 * t)
    return col < row


def _suffix_matrix(t, prefix=False):
    row = lax.broadcasted_iota(jnp.int32, (t, t), 0)
    col = lax.broadcasted_iota(jnp.int32, (t, t), 1)
    return ((row <= col) if prefix else (row >= col)).astype(BF16)


def _attn_fwd(qkv, t):
    S = qkv.shape[0]
    tk = 2 * t

    def body(q_ref, k_ref, v_ref, o_ref, acc_ref, r_ref):
        i = pl.program_id(1)
        last = i // 2
        m_suf = _suffix_matrix(t)
        q = q_ref[...]
        hms = [_lane_mask(h) for h in range(2)]
        qms = [jnp.where(hm, q, 0) * 0.125 for hm in hms]
        acc_ref[...] = jnp.zeros_like(acc_ref)
        r_ref[...] = jnp.zeros_like(r_ref)

        def tile(sb, masked):
            ks = pl.multiple_of(sb * tk, tk)
            kt = k_ref[pl.ds(ks, tk), :]
            vt = v_ref[pl.ds(ks, tk), :]
            mask = _causal_mask(i, sb, t) if masked else None
            for h in range(2):
                a_loc, _, rs = _sb_tile(qms[h], kt, None, mask, m_suf)
                r = r_ref[h]
                acc_ref[...] += _dot(a_loc.astype(BF16), jnp.where(hms[h], vt, 0)) * jnp.exp(-r)
                r_ref[h] = r + rs

        tile(last, True)

        def step(n, carry):
            tile(last - 1 - n, False)
            return carry

        lax.fori_loop(0, last, step, 0)
        o_ref[...] = acc_ref[...]

    return pl.pallas_call(
        body, name="attn_fwd", grid=(N_PAIRS, S // t),
        in_specs=[pl.BlockSpec((t, PAIR), lambda p, i: (i, p)),
                  pl.BlockSpec((S, PAIR), lambda p, i: (0, N_PAIRS + p)),
                  pl.BlockSpec((S, PAIR), lambda p, i: (0, 2 * N_PAIRS + p))],
        out_specs=pl.BlockSpec((t, PAIR), lambda p, i: (i, p)),
        out_shape=jax.ShapeDtypeStruct((S, N_PAIRS * PAIR), F32),
        scratch_shapes=[pltpu.VMEM((t, PAIR), F32), pltpu.VMEM((2, t, 1), F32)],
        compiler_params=_cp(("parallel", "arbitrary")),
    )(qkv, qkv, qkv)


def _out_proj(conv_out, o, ag, wc, x2, g2, g3, tm):
    S = o.shape[0]

    def body(co_ref, o_ref, ag_ref, w_ref, x_ref, g2_ref, g3_ref, mix_ref, y_ref, h1_ref, fin_ref):
        seg = _seg_matrix(CONV_CH)
        o = o_ref[...]
        att = (o * lax.rsqrt(_head_mean(o * o, seg) + EPS) * ag_ref[...]).astype(BF16)
        co = co_ref[...]
        mix_ref[:, :CONV_CH] = co
        mix_ref[:, CONV_CH:] = att
        y = _dot(co, w_ref[0:CONV_CH, :]) + _dot(att, w_ref[CONV_CH:, :])
        y_ref[...] = y
        h1 = x_ref[...] + y * _rms(y) * g2_ref[...]
        h1_ref[...] = h1
        fin_ref[...] = (h1 * _rms(h1) * g3_ref[...]).astype(BF16)

    row = lambda w: pl.BlockSpec((tm, w), lambda i: (i, 0))
    return pl.pallas_call(
        body, name="out_proj", grid=(S // tm,),
        in_specs=[row(CONV_CH), row(CONV_CH), _const_spec((1, CONV_CH)), _const_spec(wc.shape),
                  row(D_MODEL), _const_spec((1, D_MODEL)), _const_spec((1, D_MODEL))],
        out_specs=[row(D_MODEL)] * 4,
        out_shape=[jax.ShapeDtypeStruct((S, D_MODEL), BF16), jax.ShapeDtypeStruct((S, D_MODEL), F32),
                   jax.ShapeDtypeStruct((S, D_MODEL), F32), jax.ShapeDtypeStruct((S, D_MODEL), BF16)],
        compiler_params=_cp(("parallel",)),
    )(conv_out, o, ag, wc, x2, g2, g3)


def _ffn_fwd(f_in, h1, tgt, wg, wu, wd, g4, tm):
    S = f_in.shape[0]

    def body(fin_ref, h1_ref, tgt_ref, wg_ref, wu_ref, wd_ref, g4_ref, df_ref, dh2_ref, dg4_ref, loss_ref):
        i = pl.program_id(0)
        fin = fin_ref[...]
        gt = _dot(fin, wg_ref[...])
        up = _dot(fin, wu_ref[...])
        f = _dot((gt * _sigmoid(gt) * up).astype(BF16), wd_ref[...])
        r = _rms(f)
        n = f * r
        g4 = g4_ref[...]
        err = h1_ref[...] + n * g4 - tgt_ref[...]
        dh2 = err * (1.0 / D_MODEL)
        dh2_ref[...] = dh2
        df, dg = _rms_bwd(dh2, n, r, g4)
        df_ref[...] = df.astype(BF16)

        @pl.when(i == 0)
        def _():
            dg4_ref[...] = jnp.zeros_like(dg4_ref)
            loss_ref[...] = jnp.zeros_like(loss_ref)

        dg4_ref[...] += jnp.sum(dg, axis=0, keepdims=True)
        part = jnp.sum(jnp.sum(err * err, axis=1, keepdims=True), axis=0, keepdims=True)
        loss_ref[...] += part * (0.5 / D_MODEL)

    row = lambda w: pl.BlockSpec((tm, w), lambda i: (i, 0))
    return pl.pallas_call(
        body, name="ffn_fwd", grid=(S // tm,),
        in_specs=[row(D_MODEL), row(D_MODEL), row(D_MODEL), _const_spec(wg.shape), _const_spec(wu.shape),
                  _const_spec(wd.shape), _const_spec((1, D_MODEL))],
        out_specs=[row(D_MODEL), row(D_MODEL), pl.BlockSpec((1, D_MODEL), lambda i: (0, 0)),
                   pl.BlockSpec((1, 128), lambda i: (0, 0))],
        out_shape=[jax.ShapeDtypeStruct((S, D_MODEL), BF16), jax.ShapeDtypeStruct((S, D_MODEL), F32),
                   jax.ShapeDtypeStruct((1, D_MODEL), F32), jax.ShapeDtypeStruct((1, 128), F32)],
        compiler_params=_cp(("arbitrary",)),
    )(f_in, h1, tgt, wg, wu, wd, g4)


def _ffn_bwd(f_in, df, dh2, h1, yv, wg, wu, wd, g3, g2, tm):
    S = f_in.shape[0]

    def body(fin_ref, df_ref, dh2_ref, h1_ref, y_ref, wg_ref, wu_ref, wd_ref, g3_ref, g2_ref,
             act_ref, dgt_ref, dup_ref, dh1_ref, dy_ref, dg3_ref, dg2_ref):
        i = pl.program_id(0)
        fin = fin_ref[...]
        df = df_ref[...]
        gt = _dot(fin, wg_ref[...])
        up = _dot(fin, wu_ref[...])
        sg = _sigmoid(gt)
        silu = gt * sg
        act_ref[...] = (silu * up).astype(BF16)
        dact = _dot_nt(df, wd_ref[...])
        dgt = (dact * up * (sg * (1.0 + gt * (1.0 - sg)))).astype(BF16)
        dup = (dact * silu).astype(BF16)
        dgt_ref[...] = dgt
        dup_ref[...] = dup
        dfin = _dot_nt(dgt, wg_ref[...]) + _dot_nt(dup, wu_ref[...])
        h1 = h1_ref[...]
        r3 = _rms(h1)
        dh1_n, dg3 = _rms_bwd(dfin, h1 * r3, r3, g3_ref[...])
        dh1 = dh2_ref[...] + dh1_n
        dh1_ref[...] = dh1
        y = y_ref[...]
        r2 = _rms(y)
        dy, dg2 = _rms_bwd(dh1, y * r2, r2, g2_ref[...])
        dy_ref[...] = dy.astype(BF16)

        @pl.when(i == 0)
        def _():
            dg3_ref[...] = jnp.zeros_like(dg3_ref)
            dg2_ref[...] = jnp.zeros_like(dg2_ref)

        dg3_ref[...] += jnp.sum(dg3, axis=0, keepdims=True)
        dg2_ref[...] += jnp.sum(dg2, axis=0, keepdims=True)

    row = lambda w: pl.BlockSpec((tm, w), lambda i: (i, 0))
    vec = pl.BlockSpec((1, D_MODEL), lambda i: (0, 0))
    return pl.pallas_call(
        body, name="ffn_bwd", grid=(S // tm,),
        in_specs=[row(D_MODEL)] * 5 + [_const_spec(wg.shape), _const_spec(wu.shape), _const_spec(wd.shape),
                                       _const_spec((1, D_MODEL)), _const_spec((1, D_MODEL))],
        out_specs=[row(D_FF), row(D_FF), row(D_FF), row(D_MODEL), row(D_MODEL), vec, vec],
        out_shape=[jax.ShapeDtypeStruct((S, D_FF), BF16)] * 3
        + [jax.ShapeDtypeStruct((S, D_MODEL), F32), jax.ShapeDtypeStruct((S, D_MODEL), BF16),
           jax.ShapeDtypeStruct((1, D_MODEL), F32), jax.ShapeDtypeStruct((1, D_MODEL), F32)],
        compiler_params=_cp(("arbitrary",)),
    )(f_in, df, dh2, h1, yv, wg, wu, wd, g3, g2)


def _out_bwd(dy, o, ag, wc, tm):
    S = o.shape[0]

    def body(dy_ref, o_ref, ag_ref, w_ref, dco_ref, do_ref, dag_ref):
        i = pl.program_id(0)
        seg = _seg_matrix(CONV_CH)
        dy = dy_ref[...]
        dco_ref[...] = _dot_nt(dy, w_ref[0:CONV_CH, :])
        datt = _dot_nt(dy, w_ref[CONV_CH:, :])
        o = o_ref[...]
        r = lax.rsqrt(_head_mean(o * o, seg) + EPS)
        n = o * r
        dn = datt * ag_ref[...]
        do_ref[...] = (r * (dn - n * _head_mean(dn * n, seg))).astype(BF16)

        @pl.when(i == 0)
        def _():
            dag_ref[...] = jnp.zeros_like(dag_ref)

        dag_ref[...] += jnp.sum(datt * n, axis=0, keepdims=True)

    row = lambda w: pl.BlockSpec((tm, w), lambda i: (i, 0))
    return pl.pallas_call(
        body, name="out_bwd", grid=(S // tm,),
        in_specs=[row(D_MODEL), row(CONV_CH), _const_spec((1, CONV_CH)), _const_spec(wc.shape)],
        out_specs=[row(CONV_CH), row(CONV_CH), pl.BlockSpec((1, CONV_CH), lambda i: (0, 0))],
        out_shape=[jax.ShapeDtypeStruct((S, CONV_CH), F32), jax.ShapeDtypeStruct((S, CONV_CH), BF16),
                   jax.ShapeDtypeStruct((1, CONV_CH), F32)],
        compiler_params=_cp(("arbitrary",)),
    )(dy, o, ag, wc)


def _attn_bwd(qkv, do, t):
    S = qkv.shape[0]
    tk = 2 * t
    nk = S // tk

    def body(q_ref, k_ref, v_ref, do_ref, dq_ref, dk_hbm, dv_hbm, g_buf, s_buf, r_ref, dq_acc, dk_ref, dv_ref):
        p = pl.program_id(0)
        i = pl.program_id(1)
        last = i // 2

        @pl.when(i == 0)
        def _():
            dk_ref[...] = jnp.zeros_like(dk_ref)
            dv_ref[...] = jnp.zeros_like(dv_ref)

        m_suf = _suffix_matrix(t)
        m_pre = _suffix_matrix(t, prefix=True)
        q = q_ref[...]
        do = do_ref[...]
        hms = [_lane_mask(h) for h in range(2)]
        qms = [jnp.where(hm, q, 0) * 0.125 for hm in hms]
        doms = [jnp.where(hm, do, 0) for hm in hms]
        dq_acc[...] = jnp.zeros_like(dq_acc)
        r_ref[...] = jnp.zeros_like(r_ref)

        def sweep1(sb, masked):
            ks = pl.multiple_of(sb * tk, tk)
            kt = k_ref[pl.ds(ks, tk), :]
            vt = v_ref[pl.ds(ks, tk), :]
            mask = _causal_mask(i, sb, t) if masked else None
            dv = jnp.zeros((tk, PAIR), F32)
            for h in range(2):
                A, sp, rs = _sb_tile(qms[h], kt, r_ref[h], mask, m_suf)
                g_buf[h, sb] = A * _dot_nt(doms[h], vt)
                s_buf[h, sb] = 1.0 - jnp.exp(-sp)
                dv = dv + _dot_tn(A.astype(BF16), doms[h])
                r_ref[h] += rs
            dv_ref[pl.ds(ks, tk), :] += dv

        sweep1(last, True)

        def step1(n, carry):
            sweep1(last - 1 - n, False)
            return carry

        lax.fori_loop(0, last, step1, 0)
        r_ref[...] = jnp.zeros_like(r_ref)

        def sweep2(sb, masked):
            ks = pl.multiple_of(sb * tk, tk)
            kt = k_ref[pl.ds(ks, tk), :]
            mask = _causal_mask(i, sb, t) if masked else None
            dk = jnp.zeros((tk, PAIR), F32)
            for h in range(2):
                g = g_buf[h, sb]
                rs_l = jnp.sum(g[:, :t], axis=1, keepdims=True)
                p_l = _tri_dot(g[:, :t], m_pre)
                p_r = _tri_dot(g[:, t:], m_pre) + rs_l
                dz = g - s_buf[h, sb] * (jnp.concatenate([p_l, p_r], axis=1) + r_ref[h])
                if masked:
                    dz = jnp.where(mask, dz, 0.0)
                dzb = dz.astype(BF16)
                dq_acc[...] += _dot(dzb, jnp.where(hms[h], kt, 0))
                dk = dk + _dot_tn(dzb, qms[h])
                r_ref[h] += rs_l + jnp.sum(g[:, t:], axis=1, keepdims=True)
            dk_ref[pl.ds(ks, tk), :] += dk

        def step2(sb, carry):
            sweep2(sb, False)
            return carry

        lax.fori_loop(0, last, step2, 0)
        sweep2(last, True)
        dq_ref[...] = dq_acc[...] * 0.125

        @pl.when(i == S // t - 1)
        def _():
            cols = pl.ds(pl.multiple_of(p * PAIR, PAIR), PAIR)
            pltpu.sync_copy(dk_ref, dk_hbm.at[:, cols])
            pltpu.sync_copy(dv_ref, dv_hbm.at[:, cols])

    once = lambda cb: pl.BlockSpec((S, PAIR), cb, pipeline_mode=pl.Buffered(1))
    return pl.pallas_call(
        body, name="attn_bwd", grid=(N_PAIRS, S // t),
        in_specs=[pl.BlockSpec((t, PAIR), lambda p, i: (i, p)),
                  once(lambda p, i: (0, N_PAIRS + p)), once(lambda p, i: (0, 2 * N_PAIRS + p)),
                  pl.BlockSpec((t, PAIR), lambda p, i: (i, p))],
        out_specs=[pl.BlockSpec((t, PAIR), lambda p, i: (i, p)), _hbm(), _hbm()],
        out_shape=[jax.ShapeDtypeStruct((S, N_PAIRS * PAIR), F32)] * 3,
        scratch_shapes=[pltpu.VMEM((2, nk, t, tk), F32), pltpu.VMEM((2, nk, t, tk), F32),
                        pltpu.VMEM((2, t, 1), F32), pltpu.VMEM((t, PAIR), F32),
                        pltpu.VMEM((S, PAIR), F32), pltpu.VMEM((S, PAIR), F32)],
        compiler_params=_cp(("arbitrary", "arbitrary")),
    )(qkv, qkv, qkv, do)


def _conv_bwd(uc, dco, cwf, cb, lg, lb, tm):
    S = uc.shape[0]
    hb = tm // HALO
    nb = S // tm
    ext = tm + HALO

    def body(uc_ref, prev_ref, next_ref, dco_ref, dnext_ref, cw_ref, cb_ref, lg_ref, lb_ref,
             duc_ref, dcw_ref, dcb_ref, dlg_ref, dlb_ref, glu_ref, dyc_ref):
        i = pl.program_id(0)
        last = i == nb - 1

        @pl.when(i == 0)
        def _():
            for ref in (dcw_ref, dcb_ref, dlg_ref, dlb_ref):
                ref[...] = jnp.zeros_like(ref)

        uc = uc_ref[...]
        glu_ref[0:HALO, :] = jnp.where(i == 0, 0.0, _glu(prev_ref[...]))
        glu_ref[HALO:ext, :] = _glu(uc)
        glu_ref[ext:ext + HALO, :] = _glu(next_ref[...])
        y = _conv_taps(cw_ref, glu_ref, HALO - (CONV_WIDTH - 1), ext) + cb_ref[...]
        mu = jnp.mean(y, axis=-1, keepdims=True)
        yc = y - mu
        rstd = lax.rsqrt(jnp.mean(yc * yc, axis=-1, keepdims=True) + EPS)
        yhat = yc * rstd
        lg = lg_ref[...]
        ln = yhat * lg + lb_ref[...]
        sg = _sigmoid(ln)
        dout = jnp.concatenate([dco_ref[...], jnp.where(last, 0.0, dnext_ref[...])], axis=0)
        dln = dout * (sg * (1.0 + ln * (1.0 - sg)))
        dyh = dln * lg
        dyc = rstd * (dyh - jnp.mean(dyh, axis=-1, keepdims=True)
                      - yhat * jnp.mean(dyh * yhat, axis=-1, keepdims=True))
        dyc_ref[...] = dyc
        dlg_ref[...] += jnp.sum((dln * yhat)[0:tm], axis=0, keepdims=True)
        dlb_ref[...] += jnp.sum(dln[0:tm], axis=0, keepdims=True)
        dcb_ref[...] += jnp.sum(dyc[0:tm], axis=0, keepdims=True)
        dglu = cw_ref[0:1, :] * dyc_ref[pl.ds(CONV_WIDTH - 1, tm), :]
        for w in range(1, CONV_WIDTH):
            dglu = dglu + cw_ref[w:w + 1, :] * dyc_ref[pl.ds(CONV_WIDTH - 1 - w, tm), :]
        d0 = dyc[0:tm]
        for w in range(CONV_WIDTH):
            off = HALO - (CONV_WIDTH - 1) + w
            dcw_ref[w:w + 1, :] += jnp.sum(d0 * glu_ref[pl.ds(off, tm), :], axis=0, keepdims=True)
        val, gate = uc[:, :CONV_CH], uc[:, CONV_CH:]
        sgate = _sigmoid(gate)
        duc_ref[:, :CONV_CH] = (dglu * sgate).astype(BF16)
        duc_ref[:, CONV_CH:] = (dglu * val * sgate * (1.0 - sgate)).astype(BF16)

    vec = pl.BlockSpec((1, CONV_CH), lambda i: (0, 0))
    nxt = lambda i: (jnp.minimum((i + 1) * hb, S // HALO - 1), 0)
    return pl.pallas_call(
        body, name="conv_bwd", grid=(nb,),
        in_specs=[pl.BlockSpec((tm, 2 * CONV_CH), lambda i: (i, 0)),
                  pl.BlockSpec((HALO, 2 * CONV_CH), lambda i: (jnp.maximum(i * hb - 1, 0), 0)),
                  pl.BlockSpec((HALO, 2 * CONV_CH), nxt),
                  pl.BlockSpec((tm, CONV_CH), lambda i: (i, 0)),
                  pl.BlockSpec((HALO, CONV_CH), nxt),
                  _const_spec(cwf.shape), _const_spec((1, CONV_CH)), _const_spec((1, CONV_CH)),
                  _const_spec((1, CONV_CH))],
        out_specs=[pl.BlockSpec((tm, 2 * CONV_CH), lambda i: (i, 0)),
                   pl.BlockSpec(cwf.shape, lambda i: (0, 0)), vec, vec, vec],
        out_shape=[jax.ShapeDtypeStruct((S, 2 * CONV_CH), BF16), jax.ShapeDtypeStruct(cwf.shape, F32)]
        + [jax.ShapeDtypeStruct((1, CONV_CH), F32)] * 3,
        scratch_shapes=[pltpu.VMEM((ext + HALO, CONV_CH), F32), pltpu.VMEM((ext, CONV_CH), F32)],
        compiler_params=_cp(("arbitrary",)),
    )(uc, uc, uc, dco, dco, cwf, cb, lg, lb)


def _in_bwd(duc, dq, dk, dv, x2, dh1, g1, wa, tm):
    S = x2.shape[0]

    def body(duc_ref, dq_ref, dk_ref, dv_ref, x_ref, dh1_ref, g_ref, w_ref, gx_ref, du_ref, dg_ref):
        i = pl.program_id(0)
        du = jnp.concatenate([duc_ref[...], dq_ref[...].astype(BF16), dk_ref[...].astype(BF16),
                              dv_ref[...].astype(BF16)], axis=1)
        du_ref[...] = du
        da = _dot_nt(du, w_ref[...])
        x = x_ref[...]
        r = _rms(x)
        dx, dg = _rms_bwd(da, x * r, r, g_ref[...])
        gx_ref[...] = dh1_ref[...] + dx

        @pl.when(i == 0)
        def _():
            dg_ref[...] = jnp.zeros_like(dg_ref)

        dg_ref[...] += jnp.sum(dg, axis=0, keepdims=True)

    row = lambda w: pl.BlockSpec((tm, w), lambda i: (i, 0))
    return pl.pallas_call(
        body, name="in_bwd", grid=(S // tm,),
        in_specs=[row(2 * CONV_CH), row(CONV_CH), row(CONV_CH), row(CONV_CH), row(D_MODEL), row(D_MODEL),
                  _const_spec((1, D_MODEL)), _const_spec(wa.shape)],
        out_specs=[row(D_MODEL), row(2560), pl.BlockSpec((1, D_MODEL), lambda i: (0, 0))],
        out_shape=[jax.ShapeDtypeStruct((S, D_MODEL), F32), jax.ShapeDtypeStruct((S, 2560), BF16),
                   jax.ShapeDtypeStruct((1, D_MODEL), F32)],
        compiler_params=_cp(("arbitrary",)),
    )(duc, dq, dk, dv, x2, dh1, g1, wa)


def _matmul_tn(xm, ym, tm, tn, ts, name):
    S, M = xm.shape
    N = ym.shape[1]

    def body(x_ref, y_ref, o_ref):
        @pl.when(pl.program_id(2) == 0)
        def _():
            o_ref[...] = jnp.zeros_like(o_ref)

        o_ref[...] += _dot_tn(x_ref[...], y_ref[...])

    return pl.pallas_call(
        body, name=name, grid=(M // tm, N // tn, S // ts),
        in_specs=[pl.BlockSpec((ts, tm), lambda m, n, s: (s, m)), pl.BlockSpec((ts, tn), lambda m, n, s: (s, n))],
        out_specs=pl.BlockSpec((tm, tn), lambda m, n, s: (m, n)),
        out_shape=jax.ShapeDtypeStruct((M, N), F32),
        compiler_params=_cp(("parallel", "parallel", "arbitrary")),
    )(xm, ym)


def _sibling_halves(grads):
    n = len(grads)

    def body(*refs):
        ins, outs, ssem, rsem = refs[:n], refs[n:2 * n], refs[2 * n], refs[2 * n + 1]
        x, y, c = lax.axis_index("x"), lax.axis_index("y"), lax.axis_index("c")
        copies = []
        for k in range(n):
            for j in range(N_CHIPS):
                copies.append(pltpu.make_async_remote_copy(
                    src_ref=ins[k].at[j, 1 - c], dst_ref=outs[k].at[j],
                    send_sem=ssem.at[N_CHIPS * k + j], recv_sem=rsem.at[N_CHIPS * k + j],
                    device_id=(x, y, 1 - c), device_id_type=MESH))
        for cp in copies:
            cp.start()
        for cp in copies:
            cp.wait()

    shapes = [jax.ShapeDtypeStruct((g.shape[0],) + g.shape[2:], F32) for g in grads]
    return pl.pallas_call(
        body, name="grad_sibling_halves", out_shape=shapes,
        in_specs=[_hbm()] * n, out_specs=[_hbm()] * n,
        scratch_shapes=[pltpu.SemaphoreType.DMA((N_CHIPS * n,)), pltpu.SemaphoreType.DMA((N_CHIPS * n,))],
    )(*grads)


def _add_half(c_arr, g, landed, name):
    def body(c_ref, g_ref, l_ref, o_ref):
        o_ref[...] = (g_ref[...] + l_ref[...]).astype(BF16)

    rows, n = g.shape[2], g.shape[3]
    grid = (N_CHIPS,)
    g_spec = pl.BlockSpec((None, None, rows, n), lambda j, c: (j, c[0], 0, 0))
    l_spec = pl.BlockSpec((None, rows, n), lambda j, c: (j, 0, 0))
    return pl.pallas_call(
        body, name=name,
        grid_spec=pltpu.PrefetchScalarGridSpec(num_scalar_prefetch=1, grid=grid, in_specs=[g_spec, l_spec],
                                               out_specs=l_spec),
        out_shape=jax.ShapeDtypeStruct(landed.shape, BF16),
        compiler_params=_cp(("parallel",)),
    )(c_arr, g, landed)


def _chip_scatter(parts):
    n = len(parts)

    def piece(ref, j):
        return ref.at[j]

    def body(*refs):
        ins, outs = refs[:n], refs[n:2 * n]
        lsem, ssem, rsem = refs[2 * n:]
        x, y, c = lax.axis_index("x"), lax.axis_index("y"), lax.axis_index("c")
        me = 2 * x + y
        chips = [(1 - x, y), (x, 1 - y), (1 - x, 1 - y)]
        copies = []
        for k in range(n):
            cp = pltpu.make_async_copy(piece(ins[k], me), outs[k].at[me], lsem.at[k])
            cp.start()
            copies.append(cp)
            for r, chip in enumerate(chips):
                cp = pltpu.make_async_remote_copy(
                    src_ref=piece(ins[k], 2 * chip[0] + chip[1]), dst_ref=outs[k].at[me],
                    send_sem=ssem.at[3 * k + r], recv_sem=rsem.at[3 * k + r],
                    device_id=(chip[0], chip[1], c), device_id_type=MESH)
                cp.start()
                copies.append(cp)
        for cp in copies:
            cp.wait()

    shapes = [jax.ShapeDtypeStruct(p.shape, p.dtype) for p in parts]
    return pl.pallas_call(
        body, name="grad_chip_scatter", out_shape=shapes,
        in_specs=[_hbm()] * n, out_specs=[_hbm()] * n,
        scratch_shapes=[pltpu.SemaphoreType.DMA((n,)), pltpu.SemaphoreType.DMA((3 * n,)),
                        pltpu.SemaphoreType.DMA((3 * n,))],
    )(*parts)


def _sum_chips(landed, name):
    _, rows, n = landed.shape
    tr = 256 if rows % 256 == 0 else rows

    def body(a_ref, b_ref, c_ref, d_ref, o_ref):
        f = lambda ref: ref[...].astype(F32)
        o_ref[...] = ((f(a_ref) + f(b_ref)) + f(c_ref)) + f(d_ref)

    specs = [pl.BlockSpec((None, tr, n), functools.partial(lambda i, j: (j, i, 0), j=j)) for j in range(N_CHIPS)]
    return pl.pallas_call(
        body, name=name, grid=(rows // tr,), in_specs=specs,
        out_specs=pl.BlockSpec((tr, n), lambda i: (i, 0)),
        out_shape=jax.ShapeDtypeStruct((rows, n), F32),
        compiler_params=_cp(("parallel",)),
    )(landed, landed, landed, landed)


def _share_halves(halves):
    n = len(halves)

    def body(*refs):
        ins, outs = refs[:n], refs[n:2 * n]
        lsem, ssem, rsem = refs[2 * n:]
        x, y, c = lax.axis_index("x"), lax.axis_index("y"), lax.axis_index("c")
        copies = []
        for k in range(n):
            copies.append(pltpu.make_async_copy(ins[k], outs[k].at[c], lsem.at[k]))
            copies.append(pltpu.make_async_remote_copy(
                src_ref=ins[k], dst_ref=outs[k].at[c], send_sem=ssem.at[k], recv_sem=rsem.at[k],
                device_id=(x, y, 1 - c), device_id_type=MESH))
        for cp in copies:
            cp.start()
        for cp in copies:
            cp.wait()

    shapes = [jax.ShapeDtypeStruct((2,) + h.shape, F32) for h in halves]
    return pl.pallas_call(
        body, name="grad_share_halves", out_shape=shapes,
        in_specs=[_hbm()] * n, out_specs=[_hbm()] * n,
        scratch_shapes=[pltpu.SemaphoreType.DMA((n,)), pltpu.SemaphoreType.DMA((n,)),
                        pltpu.SemaphoreType.DMA((n,))],
    )(*halves)


def _allreduce_small(packed):
    rows, n = packed.shape

    def body(in_ref, out_ref, land_ref, ssem, rsem):
        x, y, c = lax.axis_index("x"), lax.axis_index("y"), lax.axis_index("c")
        me = 4 * x + 2 * y + c
        land_ref[me] = in_ref[...]
        copies = []
        for r in range(1, 8):
            tx = 1 - x if r & 4 else x
            ty = 1 - y if r & 2 else y
            tc = 1 - c if r & 1 else c
            cp = pltpu.make_async_remote_copy(
                src_ref=in_ref, dst_ref=land_ref.at[me], send_sem=ssem.at[r - 1], recv_sem=rsem.at[r - 1],
                device_id=(tx, ty, tc), device_id_type=MESH)
            cp.start()
            copies.append(cp)
        for cp in copies:
            cp.wait()
        acc = land_ref[0]
        for k in range(1, 8):
            acc = acc + land_ref[k]
        out_ref[...] = acc

    return pl.pallas_call(
        body, name="allreduce_small", out_shape=jax.ShapeDtypeStruct((rows, n), F32),
        in_specs=[pl.BlockSpec(memory_space=pltpu.VMEM)], out_specs=pl.BlockSpec(memory_space=pltpu.VMEM),
        scratch_shapes=[pltpu.VMEM((8, rows, n), F32), pltpu.SemaphoreType.DMA((7,)),
                        pltpu.SemaphoreType.DMA((7,))],
    )(packed)


def _adamw(w, g, m, v, name):
    rows, n = w.shape
    tr = 256 if rows % 256 == 0 else rows

    def body(w_ref, g_ref, m_ref, v_ref, d_ref, mo_ref, vo_ref):
        g = g_ref[...]
        m = ADAM_B1 * m_ref[...] + (1.0 - ADAM_B1) * g
        v = ADAM_B2 * v_ref[...] + (1.0 - ADAM_B2) * (g * g)
        m_hat = m / (1.0 - ADAM_B1 ** ADAM_STEP)
        v_hat = v / (1.0 - ADAM_B2 ** ADAM_STEP)
        d_ref[...] = -ADAM_LR * (m_hat / (jnp.sqrt(v_hat) + ADAM_EPS) + ADAM_WD * w_ref[...])
        mo_ref[...] = m
        vo_ref[...] = v

    spec = pl.BlockSpec((tr, n), lambda i: (i, 0))
    return pl.pallas_call(
        body, name=name, grid=(rows // tr,), in_specs=[spec] * 4, out_specs=[spec] * 3,
        out_shape=[jax.ShapeDtypeStruct((rows, n), F32)] * 3,
        compiler_params=_cp(("parallel",)),
    )(w, g, m, v)


def _rows8(a):
    a = a.reshape(-1, 128)
    return jnp.pad(a, ((0, (-a.shape[0]) % 8), (0, 0)))


def kernel(x, g_pre_mix, w_in, conv_w, conv_b, conv_ln_g, conv_ln_b, attn_norm_g, w_out, g_post_mix, g_pre_ffn, w_gate, w_up, w_down, g_post_ffn, loss_target, m_g_pre_mix, m_w_in, m_conv_w, m_conv_b, m_conv_ln_g, m_conv_ln_b, m_attn_norm_g, m_w_out, m_g_post_mix, m_g_pre_ffn, m_w_gate, m_w_up, m_w_down, m_g_post_ffn, v_g_pre_mix, v_w_in, v_conv_w, v_conv_b, v_conv_ln_g, v_conv_ln_b, v_attn_norm_g, v_w_out, v_g_post_mix, v_g_pre_ffn, v_w_gate, v_w_up, v_w_down, v_g_post_ffn):
    S = x.shape[1]
    tm_big = min(512, S)
    tm_ffn = min(256, S)
    t_att = min(256, S)
    chip = 2 * lax.axis_index("x") + lax.axis_index("y")
    core = lax.axis_index("c")
    x2 = x.reshape(S, D_MODEL)
    tgt = loss_target.reshape(S, D_MODEL)
    ag = attn_norm_g.reshape(1, CONV_CH)

    a_sh = w_in[0].astype(BF16)
    b_sh = jnp.stack([w_gate[0], w_up[0]]).astype(BF16)
    c_sh = jnp.concatenate([w_out[0], w_down[0]], axis=0).astype(BF16)
    cw_sh = jnp.pad(conv_w[0, :, 0, :], ((0, 1), (0, 0)))
    wa4, wb4, wc4, cw4 = _gather_weights(a_sh, b_sh, c_sh, cw_sh)
    cols = lambda w4: jnp.transpose(w4, (1, 0, 2)).reshape(w4.shape[1], N_CHIPS * w4.shape[2])
    wa = cols(wa4)
    wg, wu = cols(wb4[:, 0]), cols(wb4[:, 1])
    wo = wc4[:, :OUT_SH].reshape(D_MODEL, D_MODEL)
    wd = wc4[:, OUT_SH:].reshape(D_FF, D_MODEL)
    cwf = cols(cw4)

    a_bf, uc, qkv = _in_proj(x2, g_pre_mix, wa, tm_big)
    conv_out = _conv_fwd(uc, cwf, conv_b, conv_ln_g, conv_ln_b, tm_big)
    o = _attn_fwd(qkv, t_att)
    mixed, yv, h1, f_in = _out_proj(conv_out, o, ag, wo, x2, g_post_mix, g_pre_ffn, tm_big)
    df, dh2, dg4, loss_part = _ffn_fwd(f_in, h1, tgt, wg, wu, wd, g_post_ffn, tm_ffn)

    act, dgt, dup, dh1, dy, dg3, dg2 = _ffn_bwd(f_in, df, dh2, h1, yv, wg, wu, wd, g_pre_ffn, g_post_mix, tm_ffn)
    dco, do, dag = _out_bwd(dy, o, ag, wo, tm_big)
    dq, dk, dv = _attn_bwd(qkv, do, t_att)
    duc, dcw, dcb, dlg, dlb = _conv_bwd(uc, dco, cwf, conv_b, conv_ln_g, conv_ln_b, tm_big)
    grad_x, du, dg1 = _in_bwd(duc, dq, dk, dv, x2, dh1, g_pre_mix, wa, tm_big)
    ts = min(512, S)
    gw_in = _matmul_tn(a_bf, du, D_MODEL, 1280, ts, "grad_w_in")
    gw_out = _matmul_tn(mixed, dy, D_MODEL, D_MODEL, ts, "grad_w_out")
    gw_gate = _matmul_tn(f_in, dgt, D_MODEL, D_FF // 2, ts, "grad_w_gate")
    gw_up = _matmul_tn(f_in, dup, D_MODEL, D_FF // 2, ts, "grad_w_up")
    gw_down = _matmul_tn(act, df, D_FF // 2, D_MODEL, ts, "grad_w_down")

    by_cols = lambda g: jnp.transpose(g.reshape(2, D_MODEL // 2, N_CHIPS, -1), (2, 0, 1, 3))
    by_rows = lambda g: g.reshape(N_CHIPS, 2, g.shape[0] // (2 * N_CHIPS), g.shape[1])
    views = [by_cols(gw_in), by_cols(gw_gate), by_cols(gw_up), by_rows(gw_out), by_rows(gw_down)]
    names = ["w_in", "w_gate", "w_up", "w_out", "w_down"]
    landed = _sibling_halves(views)
    c_arr = core.reshape(1).astype(jnp.int32)
    parts = [_add_half(c_arr, g, l, "grad_half_" + nm) for g, l, nm in zip(views, landed, names)]
    slots = _chip_scatter(parts)
    halves = [_sum_chips(s, "grad_sum_" + nm) for s, nm in zip(slots, names)]
    full = _share_halves(halves)
    g_in, g_gate, g_up, g_out, g_down = [f.reshape(2 * f.shape[1], f.shape[2]) for f in full]

    small = [dg1, dcb, dlg, dlb, dag, dg2, dg3, dg4]
    packed = jnp.concatenate([_rows8(s) for s in small] + [_rows8(dcw), _rows8(loss_part)], axis=0)
    red = _allreduce_small(packed)
    sizes = [D_MODEL, CONV_CH, CONV_CH, CONV_CH, CONV_CH, D_MODEL, D_MODEL, D_MODEL]
    g_small = [red[8 * k:8 * k + n // 128].reshape(1, n) for k, n in enumerate(sizes)]
    cw_red = red[64:64 + 128].reshape(HALO, CONV_CH)
    g_cw = lax.dynamic_slice(cw_red, (0, chip * 128), (HALO, 128))
    loss = red[192, 0]

    big = []
    for w, g, m, v, nm in [(w_in, g_in, m_w_in, v_w_in, "w_in"), (w_out, g_out, m_w_out, v_w_out, "w_out"),
                           (w_gate, g_gate, m_w_gate, v_w_gate, "w_gate"), (w_up, g_up, m_w_up, v_w_up, "w_up"),
                           (w_down, g_down, m_w_down, v_w_down, "w_down")]:
        big.append(_adamw(w[0], g, m[0], v[0], "adamw_" + nm))
    sm_w = [g_pre_mix, conv_b, conv_ln_g, conv_ln_b, ag, g_post_mix, g_pre_ffn, g_post_ffn]
    sm_m = [m_g_pre_mix, m_conv_b, m_conv_ln_g, m_conv_ln_b, m_attn_norm_g, m_g_post_mix, m_g_pre_ffn, m_g_post_ffn]
    sm_v = [v_g_pre_mix, v_conv_b, v_conv_ln_g, v_conv_ln_b, v_attn_norm_g, v_g_post_mix, v_g_pre_ffn, v_g_post_ffn]
    pad_cw = lambda a: jnp.pad(a[0, :, 0, :], ((0, 1), (0, 0)))

    def pack(vecs, cw):
        return jnp.concatenate([_rows8(a) for a in vecs] + [cw], axis=0)

    sd, smn, svn = _adamw(pack(sm_w, pad_cw(conv_w)), pack(g_small, g_cw), pack(sm_m, pad_cw(m_conv_w)),
                          pack(sm_v, pad_cw(v_conv_w)), "adamw_small")

    def unpack(p):
        vecs = [p[8 * k:8 * k + n // 128].reshape(1, n) for k, n in enumerate(sizes)]
        return vecs, p[64:64 + CONV_WIDTH].reshape(1, CONV_WIDTH, 1, 128)

    def ordered(vecs, cw, w_in_, w_out_, w_gate_, w_up_, w_down_):
        g1_, cb_, lg_, lb_, ag_, g2_, g3_, g4_ = vecs
        return [g1_, w_in_[None], cw, cb_, lg_, lb_, ag_.reshape(1, 8, HEAD_DIM), w_out_[None], g2_, g3_,
                w_gate_[None], w_up_[None], w_down_[None], g4_]

    grads = ordered(g_small, g_cw[:CONV_WIDTH].reshape(1, CONV_WIDTH, 1, 128), g_in, g_out, g_gate, g_up, g_down)
    outs = []
    for idx, p in enumerate((sd, smn, svn)):
        vecs, cw = unpack(p)
        outs += ordered(vecs, cw, *[b[idx] for b in (big[0], big[1], big[2], big[3], big[4])])
    return (loss, grad_x.reshape(1, S, D_MODEL), *grads, *outs)
```

```python
import functools
import math

import jax
import jax.numpy as jnp
from jax import lax
from jax.experimental import pallas as pl
from jax.experimental.pallas import tpu as pltpu

F32 = jnp.float32
BF16 = jnp.bfloat16
MESH = pl.DeviceIdType.MESH

D_MODEL = 1024
CONV_CH = 512
CONV_WIDTH = 31
HEAD_DIM = 64
PAIR = 2 * HEAD_DIM
N_PAIRS = 4
D_FF = 2816
N_CHIPS = 4
IN_SH = 2560 // N_CHIPS
FF_SH = D_FF // N_CHIPS
OUT_SH = D_MODEL // N_CHIPS
C_ROWS = OUT_SH + FF_SH
EPS = 1e-6
HALO = 32

ADAM_LR = 0.001
ADAM_B1 = 0.9
ADAM_B2 = 0.999
ADAM_EPS = 1e-08
ADAM_WD = 0.01
ADAM_STEP = 10

VMEM_LIMIT = 56 * 2 ** 20


def _cp(sem=None, vmem=VMEM_LIMIT):
    return pltpu.CompilerParams(dimension_semantics=sem, vmem_limit_bytes=vmem)


def _hbm():
    return pl.BlockSpec(memory_space=pltpu.HBM)


def _const_spec(shape):
    nd = len(shape)
    return pl.BlockSpec(shape, lambda *_: (0,) * nd, pipeline_mode=pl.Buffered(1))


def _dot(a, b):
    return jnp.dot(a, b, preferred_element_type=F32)


def _dot_nt(a, b):
    return lax.dot_general(a, b, (((1,), (1,)), ((), ())), preferred_element_type=F32)


def _dot_tn(a, b):
    return lax.dot_general(a, b, (((0,), (0,)), ((), ())), preferred_element_type=F32)


def _split3(x):
    b0 = x.astype(BF16)
    r1 = x - b0.astype(F32)
    b1 = r1.astype(BF16)
    b2 = (r1 - b1.astype(F32)).astype(BF16)
    return b0, b1, b2


def _split2(x):
    hi = x.astype(BF16)
    lo = (x - hi.astype(F32)).astype(BF16)
    return hi, lo


def _sigmoid(x):
    return 1.0 / (1.0 + jnp.exp(-x))


def _head_mean(x, seg):
    b0, b1, b2 = _split3(x)
    return (_dot(b0, seg) + _dot(b1, seg) + _dot(b2, seg)) * (1.0 / HEAD_DIM)


def _seg_matrix(n):
    r = lax.broadcasted_iota(jnp.int32, (n, n), 0) // HEAD_DIM
    c = lax.broadcasted_iota(jnp.int32, (n, n), 1) // HEAD_DIM
    return (r == c).astype(BF16)


def _rms(x):
    return lax.rsqrt(jnp.mean(x * x, axis=-1, keepdims=True) + EPS)


def _rms_bwd(dy, n, r, g):
    dn = dy * g
    dx = r * (dn - n * jnp.mean(dn * n, axis=-1, keepdims=True))
    return dx, dy * n


def _gather_weights(a_sh, b_sh, c_sh, w_sh):
    ha, hc, hw = a_sh.shape[0] // 2, c_sh.shape[0] // 2, w_sh.shape[0] // 2

    def body(a_ref, b_ref, c_ref, w_ref, ao, bo, co, wo, lsem, ssem, rsem):
        x, y, c = lax.axis_index("x"), lax.axis_index("y"), lax.axis_index("c")
        me = 2 * x + y
        sibling = (x, y, 1 - c)
        chips = [(1 - x, y), (x, 1 - y), (1 - x, 1 - y)]

        def src_half(i, h):
            return [a_ref.at[pl.ds(h * ha, ha), :], b_ref.at[h], c_ref.at[pl.ds(h * hc, hc), :],
                    w_ref.at[pl.ds(h * hw, hw), :]][i]

        def out_half(i, j, h):
            return [ao.at[j, pl.ds(h * ha, ha), :], bo.at[j, h], co.at[j, pl.ds(h * hc, hc), :],
                    wo.at[j, pl.ds(h * hw, hw), :]][i]

        srcs = [a_ref, b_ref, c_ref, w_ref]
        outs = [ao, bo, co, wo]
        local = [pltpu.make_async_copy(srcs[i], outs[i].at[me], lsem.at[i]) for i in range(4)]
        for cp in local:
            cp.start()

        def ici(i, k, chip, origin):
            return pltpu.make_async_remote_copy(
                src_ref=src_half(i, c), dst_ref=out_half(i, origin, c),
                send_sem=ssem.at[6 * i + k], recv_sem=rsem.at[6 * i + k],
                device_id=(chip[0], chip[1], c), device_id_type=MESH)

        def d2d(i, k, origin, h):
            return pltpu.make_async_remote_copy(
                src_ref=out_half(i, origin, h), dst_ref=out_half(i, origin, h),
                send_sem=ssem.at[6 * i + 3 + k], recv_sem=rsem.at[6 * i + 3 + k],
                device_id=sibling, device_id_type=MESH)

        sends = []
        for i in range(4):
            for k, chip in enumerate(chips):
                cp = ici(i, k, chip, me)
                cp.start()
                sends.append(cp)
        for i in range(4):
            for k, chip in enumerate(chips):
                origin = 2 * chip[0] + chip[1]
                ici(i, k, chip, origin).wait_recv()
                cp = d2d(i, k, origin, c)
                cp.start()
                sends.append(cp)
        for i in range(4):
            for k, chip in enumerate(chips):
                origin = 2 * chip[0] + chip[1]
                d2d(i, k, origin, 1 - c).wait_recv()
        for cp in sends:
            cp.wait_send()
        for cp in local:
            cp.wait()

    shapes = [jax.ShapeDtypeStruct((N_CHIPS,) + s.shape, s.dtype) for s in (a_sh, b_sh, c_sh, w_sh)]
    return pl.pallas_call(
        body, name="gather_weights", out_shape=shapes,
        in_specs=[_hbm()] * 4, out_specs=[_hbm()] * 4,
        scratch_shapes=[pltpu.SemaphoreType.DMA((4,)), pltpu.SemaphoreType.DMA((24,)),
                        pltpu.SemaphoreType.DMA((24,))],
    )(a_sh, b_sh, c_sh, w_sh)


def _in_proj(x2, g1, wa, tm):
    S = x2.shape[0]

    def body(x_ref, g_ref, w_ref, a_ref, uc_ref, qkv_ref):
        x = x_ref[...]
        a = (x * _rms(x) * g_ref[...]).astype(BF16)
        a_ref[...] = a
        uc_ref[...] = _dot(a, w_ref[:, 0:2 * CONV_CH])
        qkv_ref[...] = _dot(a, w_ref[:, 2 * CONV_CH:]).astype(BF16)

    return pl.pallas_call(
        body, name="in_proj", grid=(S // tm,),
        in_specs=[pl.BlockSpec((tm, D_MODEL), lambda i: (i, 0)), _const_spec((1, D_MODEL)),
                  _const_spec(wa.shape)],
        out_specs=[pl.BlockSpec((tm, D_MODEL), lambda i: (i, 0)),
                   pl.BlockSpec((tm, 2 * CONV_CH), lambda i: (i, 0)),
                   pl.BlockSpec((tm, 1536), lambda i: (i, 0))],
        out_shape=[jax.ShapeDtypeStruct((S, D_MODEL), BF16), jax.ShapeDtypeStruct((S, 2 * CONV_CH), F32),
                   jax.ShapeDtypeStruct((S, 1536), BF16)],
        compiler_params=_cp(("parallel",)),
    )(x2, g1, wa)


def _conv_taps(cw_ref, src_ref, off, rows):
    acc = cw_ref[0:1, :] * src_ref[pl.ds(off, rows), :]
    for w in range(1, CONV_WIDTH):
        acc = acc + cw_ref[w:w + 1, :] * src_ref[pl.ds(off + w, rows), :]
    return acc


def _glu(uc):
    return uc[:, :CONV_CH] * _sigmoid(uc[:, CONV_CH:])


def _conv_fwd(uc, cwf, cb, lg, lb, tm):
    S = uc.shape[0]
    hb = tm // HALO

    def body(uc_ref, prev_ref, cw_ref, cb_ref, lg_ref, lb_ref, out_ref, glu_ref):
        i = pl.program_id(0)
        glu_ref[0:HALO, :] = jnp.where(i == 0, 0.0, _glu(prev_ref[...]))
        glu_ref[HALO:HALO + tm, :] = _glu(uc_ref[...])
        y = _conv_taps(cw_ref, glu_ref, HALO - (CONV_WIDTH - 1), tm) + cb_ref[...]
        mu = jnp.mean(y, axis=-1, keepdims=True)
        yc = y - mu
        rstd = lax.rsqrt(jnp.mean(yc * yc, axis=-1, keepdims=True) + EPS)
        ln = yc * rstd * lg_ref[...] + lb_ref[...]
        out_ref[...] = (ln * _sigmoid(ln)).astype(BF16)

    return pl.pallas_call(
        body, name="conv_fwd", grid=(S // tm,),
        in_specs=[pl.BlockSpec((tm, 2 * CONV_CH), lambda i: (i, 0)),
                  pl.BlockSpec((HALO, 2 * CONV_CH), lambda i: (jnp.maximum(i * hb - 1, 0), 0)),
                  _const_spec(cwf.shape), _const_spec((1, CONV_CH)), _const_spec((1, CONV_CH)),
                  _const_spec((1, CONV_CH))],
        out_specs=pl.BlockSpec((tm, CONV_CH), lambda i: (i, 0)),
        out_shape=jax.ShapeDtypeStruct((S, CONV_CH), BF16),
        scratch_shapes=[pltpu.VMEM((HALO + tm, CONV_CH), F32)],
        compiler_params=_cp(("parallel",)),
    )(uc, uc, cwf, cb, lg, lb)


def _lane_mask(h):
    lane = lax.broadcasted_iota(jnp.int32, (1, PAIR), 1)
    return (lane >= HEAD_DIM * h) & (lane < HEAD_DIM * (h + 1))


def _neg_abs(x):
    bits = lax.bitcast_convert_type(x, jnp.uint32) | jnp.uint32(0x80000000)
    return lax.bitcast_convert_type(bits, F32)


def _tri_dot(x, m):
    return _dot(x.astype(BF16), m)


MASKED = -1e30
KEY_BLOCKS = 4


def _running_sums(x, m, reverse):
    t = m.shape[0]
    order = range(KEY_BLOCKS - 1, -1, -1) if reverse else range(KEY_BLOCKS)
    out = [None] * KEY_BLOCKS
    carry = None
    for b in order:
        xb = x[:, b * t:(b + 1) * t]
        cb = _tri_dot(xb, m)
        out[b] = cb if carry is None else cb + carry
        rs = jnp.sum(xb, axis=1, keepdims=True)
        carry = rs if carry is None else carry + rs
    return jnp.concatenate(out, axis=1), carry


def _sb_tile(z, r, m_suf):
    sp = jnp.maximum(z, 0.0) + jnp.log(1.0 + jnp.exp(_neg_abs(z)))
    c, rs = _running_sums(sp, m_suf, reverse=True)
    ex = z - c
    if r is not None:
        ex = ex - r
    return jnp.exp(ex), sp, rs


def _scores(qm, kt, mask):
    z = _dot_nt(qm, kt)
    return z if mask is None else jnp.where(mask, z, MASKED)


def _causal_mask(i, sb, t):
    row = lax.broadcasted_iota(jnp.int32, (t, KEY_BLOCKS * t), 0) + i * t
    col = lax.broadcasted_iota(jnp.int32, (t, KEY_BLOCKS * t), 1) + sb * (KEY_BLOCKS * t)
    return col < row


def _sweep(first, count, down, tile):
    tile(first, True)

    def step(n, carry):
        tile(first - 1 - n if down else first + 1 + n, False)
        return carry

    lax.fori_loop(0, count, step, 0)


def _suffix_matrix(t, prefix=False):
    row = lax.broadcasted_iota(jnp.int32, (t, t), 0)
    col = lax.broadcasted_iota(jnp.int32, (t, t), 1)
    return ((row <= col) if prefix else (row >= col)).astype(BF16)


def _attn_fwd(qkv, t):
    S = qkv.shape[0]
    tk = KEY_BLOCKS * t

    def body(q_ref, k_ref, v_ref, o_ref, acc_ref, r_ref):
        i = pl.program_id(1)
        last = i // KEY_BLOCKS
        m_suf = _suffix_matrix(t)
        q = q_ref[...]
        hms = [_lane_mask(h) for h in range(2)]
        qms = [jnp.where(hm, q, 0) * 0.125 for hm in hms]
        acc_ref[...] = jnp.zeros_like(acc_ref)
        r_ref[...] = jnp.zeros_like(r_ref)

        def tile(sb, diagonal):
            rows = pl.ds(pl.multiple_of(sb * tk, tk), tk)
            kt = k_ref[rows, :]
            vt = v_ref[rows, :]
            mask = _causal_mask(i, sb, t) if diagonal else None
            for h in range(2):
                a_loc, _, rs = _sb_tile(_scores(qms[h], kt, mask), None, m_suf)
                r = r_ref[h]
                acc_ref[...] += _dot(a_loc.astype(BF16), jnp.where(hms[h], vt, 0)) * jnp.exp(-r)
                r_ref[h] = r + rs

        _sweep(last, last, True, tile)
        o_ref[...] = acc_ref[...]

    return pl.pallas_call(
        body, name="attn_fwd", grid=(N_PAIRS, S // t),
        in_specs=[pl.BlockSpec((t, PAIR), lambda p, i: (i, p)),
                  pl.BlockSpec((S, PAIR), lambda p, i: (0, N_PAIRS + p)),
                  pl.BlockSpec((S, PAIR), lambda p, i: (0, 2 * N_PAIRS + p))],
        out_specs=pl.BlockSpec((t, PAIR), lambda p, i: (i, p)),
        out_shape=jax.ShapeDtypeStruct((S, N_PAIRS * PAIR), F32),
        scratch_shapes=[pltpu.VMEM((t, PAIR), F32), pltpu.VMEM((2, t, 1), F32)],
        compiler_params=_cp(("parallel", "arbitrary")),
    )(qkv, qkv, qkv)


def _out_proj(conv_out, o, ag, wc, x2, g2, g3, tm):
    S = o.shape[0]

    def body(co_ref, o_ref, ag_ref, w_ref, x_ref, g2_ref, g3_ref, mix_ref, y_ref, h1_ref, fin_ref):
        seg = _seg_matrix(CONV_CH)
        o = o_ref[...]
        att = (o * lax.rsqrt(_head_mean(o * o, seg) + EPS) * ag_ref[...]).astype(BF16)
        co = co_ref[...]
        mix_ref[:, :CONV_CH] = co
        mix_ref[:, CONV_CH:] = att
        y = _dot(co, w_ref[0:CONV_CH, :]) + _dot(att, w_ref[CONV_CH:, :])
        y_ref[...] = y
        h1 = x_ref[...] + y * _rms(y) * g2_ref[...]
        h1_ref[...] = h1
        fin_ref[...] = (h1 * _rms(h1) * g3_ref[...]).astype(BF16)

    row = lambda w: pl.BlockSpec((tm, w), lambda i: (i, 0))
    return pl.pallas_call(
        body, name="out_proj", grid=(S // tm,),
        in_specs=[row(CONV_CH), row(CONV_CH), _const_spec((1, CONV_CH)), _const_spec(wc.shape),
                  row(D_MODEL), _const_spec((1, D_MODEL)), _const_spec((1, D_MODEL))],
        out_specs=[row(D_MODEL)] * 4,
        out_shape=[jax.ShapeDtypeStruct((S, D_MODEL), BF16), jax.ShapeDtypeStruct((S, D_MODEL), F32),
                   jax.ShapeDtypeStruct((S, D_MODEL), F32), jax.ShapeDtypeStruct((S, D_MODEL), BF16)],
        compiler_params=_cp(("parallel",)),
    )(conv_out, o, ag, wc, x2, g2, g3)


def _ffn_fwd(f_in, h1, tgt, wg, wu, wd, g4, tm):
    S = f_in.shape[0]

    def body(fin_ref, h1_ref, tgt_ref, wg_ref, wu_ref, wd_ref, g4_ref, df_ref, dh2_ref, dg4_ref, loss_ref):
        i = pl.program_id(0)
        fin = fin_ref[...]
        gt = _dot(fin, wg_ref[...])
        up = _dot(fin, wu_ref[...])
        f = _dot((gt * _sigmoid(gt) * up).astype(BF16), wd_ref[...])
        r = _rms(f)
        n = f * r
        g4 = g4_ref[...]
        err = h1_ref[...] + n * g4 - tgt_ref[...]
        dh2 = err * (1.0 / D_MODEL)
        dh2_ref[...] = dh2
        df, dg = _rms_bwd(dh2, n, r, g4)
        df_ref[...] = df.astype(BF16)

        @pl.when(i == 0)
        def _():
            dg4_ref[...] = jnp.zeros_like(dg4_ref)
            loss_ref[...] = jnp.zeros_like(loss_ref)

        dg4_ref[...] += jnp.sum(dg, axis=0, keepdims=True)
        part = jnp.sum(jnp.sum(err * err, axis=1, keepdims=True), axis=0, keepdims=True)
        loss_ref[...] += part * (0.5 / D_MODEL)

    row = lambda w: pl.BlockSpec((tm, w), lambda i: (i, 0))
    return pl.pallas_call(
        body, name="ffn_fwd", grid=(S // tm,),
        in_specs=[row(D_MODEL), row(D_MODEL), row(D_MODEL), _const_spec(wg.shape), _const_spec(wu.shape),
                  _const_spec(wd.shape), _const_spec((1, D_MODEL))],
        out_specs=[row(D_MODEL), row(D_MODEL), pl.BlockSpec((1, D_MODEL), lambda i: (0, 0)),
                   pl.BlockSpec((1, 128), lambda i: (0, 0))],
        out_shape=[jax.ShapeDtypeStruct((S, D_MODEL), BF16), jax.ShapeDtypeStruct((S, D_MODEL), F32),
                   jax.ShapeDtypeStruct((1, D_MODEL), F32), jax.ShapeDtypeStruct((1, 128), F32)],
        compiler_params=_cp(("arbitrary",)),
    )(f_in, h1, tgt, wg, wu, wd, g4)


def _ffn_bwd(f_in, df, dh2, h1, yv, wg, wu, wd, g3, g2, tm):
    S = f_in.shape[0]

    def body(fin_ref, df_ref, dh2_ref, h1_ref, y_ref, wg_ref, wu_ref, wd_ref, g3_ref, g2_ref,
             act_ref, dgt_ref, dup_ref, dh1_ref, dy_ref, dg3_ref, dg2_ref):
        i = pl.program_id(0)
        fin = fin_ref[...]
        df = df_ref[...]
        gt = _dot(fin, wg_ref[...])
        up = _dot(fin, wu_ref[...])
        sg = _sigmoid(gt)
        silu = gt * sg
        act_ref[...] = (silu * up).astype(BF16)
        dact = _dot_nt(df, wd_ref[...])
        dgt = (dact * up * (sg * (1.0 + gt * (1.0 - sg)))).astype(BF16)
        dup = (dact * silu).astype(BF16)
        dgt_ref[...] = dgt
        dup_ref[...] = dup
        dfin = _dot_nt(dgt, wg_ref[...]) + _dot_nt(dup, wu_ref[...])
        h1 = h1_ref[...]
        r3 = _rms(h1)
        dh1_n, dg3 = _rms_bwd(dfin, h1 * r3, r3, g3_ref[...])
        dh1 = dh2_ref[...] + dh1_n
        dh1_ref[...] = dh1
        y = y_ref[...]
        r2 = _rms(y)
        dy, dg2 = _rms_bwd(dh1, y * r2, r2, g2_ref[...])
        dy_ref[...] = dy.astype(BF16)

        @pl.when(i == 0)
        def _():
            dg3_ref[...] = jnp.zeros_like(dg3_ref)
            dg2_ref[...] = jnp.zeros_like(dg2_ref)

        dg3_ref[...] += jnp.sum(dg3, axis=0, keepdims=True)
        dg2_ref[...] += jnp.sum(dg2, axis=0, keepdims=True)

    row = lambda w: pl.BlockSpec((tm, w), lambda i: (i, 0))
    vec = pl.BlockSpec((1, D_MODEL), lambda i: (0, 0))
    return pl.pallas_call(
        body, name="ffn_bwd", grid=(S // tm,),
        in_specs=[row(D_MODEL)] * 5 + [_const_spec(wg.shape), _const_spec(wu.shape), _const_spec(wd.shape),
                                       _const_spec((1, D_MODEL)), _const_spec((1, D_MODEL))],
        out_specs=[row(D_FF), row(D_FF), row(D_FF), row(D_MODEL), row(D_MODEL), vec, vec],
        out_shape=[jax.ShapeDtypeStruct((S, D_FF), BF16)] * 3
        + [jax.ShapeDtypeStruct((S, D_MODEL), F32), jax.ShapeDtypeStruct((S, D_MODEL), BF16),
           jax.ShapeDtypeStruct((1, D_MODEL), F32), jax.ShapeDtypeStruct((1, D_MODEL), F32)],
        compiler_params=_cp(("arbitrary",)),
    )(f_in, df, dh2, h1, yv, wg, wu, wd, g3, g2)


def _out_bwd(dy, o, ag, wc, tm):
    S = o.shape[0]

    def body(dy_ref, o_ref, ag_ref, w_ref, dco_ref, do_ref, dag_ref):
        i = pl.program_id(0)
        seg = _seg_matrix(CONV_CH)
        dy = dy_ref[...]
        dco_ref[...] = _dot_nt(dy, w_ref[0:CONV_CH, :])
        datt = _dot_nt(dy, w_ref[CONV_CH:, :])
        o = o_ref[...]
        r = lax.rsqrt(_head_mean(o * o, seg) + EPS)
        n = o * r
        dn = datt * ag_ref[...]
        do_ref[...] = (r * (dn - n * _head_mean(dn * n, seg))).astype(BF16)

        @pl.when(i == 0)
        def _():
            dag_ref[...] = jnp.zeros_like(dag_ref)

        dag_ref[...] += jnp.sum(datt * n, axis=0, keepdims=True)

    row = lambda w: pl.BlockSpec((tm, w), lambda i: (i, 0))
    return pl.pallas_call(
        body, name="out_bwd", grid=(S // tm,),
        in_specs=[row(D_MODEL), row(CONV_CH), _const_spec((1, CONV_CH)), _const_spec(wc.shape)],
        out_specs=[row(CONV_CH), row(CONV_CH), pl.BlockSpec((1, CONV_CH), lambda i: (0, 0))],
        out_shape=[jax.ShapeDtypeStruct((S, CONV_CH), F32), jax.ShapeDtypeStruct((S, CONV_CH), BF16),
                   jax.ShapeDtypeStruct((1, CONV_CH), F32)],
        compiler_params=_cp(("arbitrary",)),
    )(dy, o, ag, wc)


def _attn_bwd(qkv, do, t):
    S = qkv.shape[0]
    tk = KEY_BLOCKS * t
    nk = S // tk

    def body(q_ref, k_ref, v_ref, do_ref, dq_ref, dk_hbm, dv_hbm, g_buf, s_buf, r_ref, dq_acc, dk_ref, dv_ref):
        p = pl.program_id(0)
        i = pl.program_id(1)
        last = i // KEY_BLOCKS

        @pl.when(i == 0)
        def _():
            dk_ref[...] = jnp.zeros_like(dk_ref)
            dv_ref[...] = jnp.zeros_like(dv_ref)

        m_suf = _suffix_matrix(t)
        m_pre = _suffix_matrix(t, prefix=True)
        q = q_ref[...]
        do = do_ref[...]
        hms = [_lane_mask(h) for h in range(2)]
        qms = [jnp.where(hm, q, 0) * 0.125 for hm in hms]
        doms = [jnp.where(hm, do, 0) for hm in hms]
        dq_acc[...] = jnp.zeros_like(dq_acc)
        r_ref[...] = jnp.zeros_like(r_ref)

        def rows(sb):
            return pl.ds(pl.multiple_of(sb * tk, tk), tk)

        def sweep1(sb, diagonal):
            kt = k_ref[rows(sb), :]
            vt = v_ref[rows(sb), :]
            mask = _causal_mask(i, sb, t) if diagonal else None
            dv = jnp.zeros((tk, PAIR), F32)
            for h in range(2):
                A, sp, rs = _sb_tile(_scores(qms[h], kt, mask), r_ref[h], m_suf)
                g_buf[h, sb] = A * _dot_nt(doms[h], vt)
                s_buf[h, sb] = 1.0 - jnp.exp(-sp)
                dv = dv + _dot_tn(A.astype(BF16), doms[h])
                r_ref[h] += rs
            dv_ref[rows(sb), :] += dv

        _sweep(last, last, True, sweep1)
        r_ref[...] = jnp.zeros_like(r_ref)

        def sweep2(sb, first):
            kt = k_ref[rows(sb), :]
            dk = jnp.zeros((tk, PAIR), F32)
            for h in range(2):
                g = g_buf[h, sb]
                pre, rs = _running_sums(g, m_pre, reverse=False)
                dzb = (g - s_buf[h, sb] * (pre + r_ref[h])).astype(BF16)
                dq_acc[...] += _dot(dzb, jnp.where(hms[h], kt, 0))
                dk = dk + _dot_tn(dzb, qms[h])
                r_ref[h] += rs
            dk_ref[rows(sb), :] += dk

        _sweep(0, last, False, sweep2)
        dq_ref[...] = dq_acc[...] * 0.125

        @pl.when(i == S // t - 1)
        def _():
            cols = pl.ds(pl.multiple_of(p * PAIR, PAIR), PAIR)
            pltpu.sync_copy(dk_ref, dk_hbm.at[:, cols])
            pltpu.sync_copy(dv_ref, dv_hbm.at[:, cols])

    once = lambda cb: pl.BlockSpec((S, PAIR), cb, pipeline_mode=pl.Buffered(1))
    return pl.pallas_call(
        body, name="attn_bwd", grid=(N_PAIRS, S // t),
        in_specs=[pl.BlockSpec((t, PAIR), lambda p, i: (i, p)),
                  once(lambda p, i: (0, N_PAIRS + p)), once(lambda p, i: (0, 2 * N_PAIRS + p)),
                  pl.BlockSpec((t, PAIR), lambda p, i: (i, p))],
        out_specs=[pl.BlockSpec((t, PAIR), lambda p, i: (i, p)), _hbm(), _hbm()],
        out_shape=[jax.ShapeDtypeStruct((S, N_PAIRS * PAIR), F32)] * 3,
        scratch_shapes=[pltpu.VMEM((2, nk, t, tk), F32), pltpu.VMEM((2, nk, t, tk), F32),
                        pltpu.VMEM((2, t, 1), F32), pltpu.VMEM((t, PAIR), F32),
                        pltpu.VMEM((S, PAIR), F32), pltpu.VMEM((S, PAIR), F32)],
        compiler_params=_cp(("arbitrary", "arbitrary")),
    )(qkv, qkv, qkv, do)


def _conv_bwd(uc, dco, cwf, cb, lg, lb, tm):
    S = uc.shape[0]
    hb = tm // HALO
    nb = S // tm
    ext = tm + HALO

    def body(uc_ref, prev_ref, next_ref, dco_ref, dnext_ref, cw_ref, cb_ref, lg_ref, lb_ref,
             duc_ref, dcw_ref, dcb_ref, dlg_ref, dlb_ref, glu_ref, dyc_ref):
        i = pl.program_id(0)
        last = i == nb - 1

        @pl.when(i == 0)
        def _():
            for ref in (dcw_ref, dcb_ref, dlg_ref, dlb_ref):
                ref[...] = jnp.zeros_like(ref)

        uc = uc_ref[...]
        glu_ref[0:HALO, :] = jnp.where(i == 0, 0.0, _glu(prev_ref[...]))
        glu_ref[HALO:ext, :] = _glu(uc)
        glu_ref[ext:ext + HALO, :] = _glu(next_ref[...])
        y = _conv_taps(cw_ref, glu_ref, HALO - (CONV_WIDTH - 1), ext) + cb_ref[...]
        mu = jnp.mean(y, axis=-1, keepdims=True)
        yc = y - mu
        rstd = lax.rsqrt(jnp.mean(yc * yc, axis=-1, keepdims=True) + EPS)
        yhat = yc * rstd
        lg = lg_ref[...]
        ln = yhat * lg + lb_ref[...]
        sg = _sigmoid(ln)
        dout = jnp.concatenate([dco_ref[...], jnp.where(last, 0.0, dnext_ref[...])], axis=0)
        dln = dout * (sg * (1.0 + ln * (1.0 - sg)))
        dyh = dln * lg
        dyc = rstd * (dyh - jnp.mean(dyh, axis=-1, keepdims=True)
                      - yhat * jnp.mean(dyh * yhat, axis=-1, keepdims=True))
        dyc_ref[...] = dyc
        dlg_ref[...] += jnp.sum((dln * yhat)[0:tm], axis=0, keepdims=True)
        dlb_ref[...] += jnp.sum(dln[0:tm], axis=0, keepdims=True)
        dcb_ref[...] += jnp.sum(dyc[0:tm], axis=0, keepdims=True)
        dglu = cw_ref[0:1, :] * dyc_ref[pl.ds(CONV_WIDTH - 1, tm), :]
        for w in range(1, CONV_WIDTH):
            dglu = dglu + cw_ref[w:w + 1, :] * dyc_ref[pl.ds(CONV_WIDTH - 1 - w, tm), :]
        d0 = dyc[0:tm]
        for w in range(CONV_WIDTH):
            off = HALO - (CONV_WIDTH - 1) + w
            dcw_ref[w:w + 1, :] += jnp.sum(d0 * glu_ref[pl.ds(off, tm), :], axis=0, keepdims=True)
        val, gate = uc[:, :CONV_CH], uc[:, CONV_CH:]
        sgate = _sigmoid(gate)
        duc_ref[:, :CONV_CH] = (dglu * sgate).astype(BF16)
        duc_ref[:, CONV_CH:] = (dglu * val * sgate * (1.0 - sgate)).astype(BF16)

    vec = pl.BlockSpec((1, CONV_CH), lambda i: (0, 0))
    nxt = lambda i: (jnp.minimum((i + 1) * hb, S // HALO - 1), 0)
    return pl.pallas_call(
        body, name="conv_bwd", grid=(nb,),
        in_specs=[pl.BlockSpec((tm, 2 * CONV_CH), lambda i: (i, 0)),
                  pl.BlockSpec((HALO, 2 * CONV_CH), lambda i: (jnp.maximum(i * hb - 1, 0), 0)),
                  pl.BlockSpec((HALO, 2 * CONV_CH), nxt),
                  pl.BlockSpec((tm, CONV_CH), lambda i: (i, 0)),
                  pl.BlockSpec((HALO, CONV_CH), nxt),
                  _const_spec(cwf.shape), _const_spec((1, CONV_CH)), _const_spec((1, CONV_CH)),
                  _const_spec((1, CONV_CH))],
        out_specs=[pl.BlockSpec((tm, 2 * CONV_CH), lambda i: (i, 0)),
                   pl.BlockSpec(cwf.shape, lambda i: (0, 0)), vec, vec, vec],
        out_shape=[jax.ShapeDtypeStruct((S, 2 * CONV_CH), BF16), jax.ShapeDtypeStruct(cwf.shape, F32)]
        + [jax.ShapeDtypeStruct((1, CONV_CH), F32)] * 3,
        scratch_shapes=[pltpu.VMEM((ext + HALO, CONV_CH), F32), pltpu.VMEM((ext, CONV_CH), F32)],
        compiler_params=_cp(("arbitrary",)),
    )(uc, uc, uc, dco, dco, cwf, cb, lg, lb)


def _in_bwd(duc, dq, dk, dv, x2, dh1, g1, wa, tm):
    S = x2.shape[0]

    def body(duc_ref, dq_ref, dk_ref, dv_ref, x_ref, dh1_ref, g_ref, w_ref, gx_ref, du_ref, dg_ref):
        i = pl.program_id(0)
        du = jnp.concatenate([duc_ref[...], dq_ref[...].astype(BF16), dk_ref[...].astype(BF16),
                              dv_ref[...].astype(BF16)], axis=1)
        du_ref[...] = du
        da = _dot_nt(du, w_ref[...])
        x = x_ref[...]
        r = _rms(x)
        dx, dg = _rms_bwd(da, x * r, r, g_ref[...])
        gx_ref[...] = dh1_ref[...] + dx

        @pl.when(i == 0)
        def _():
            dg_ref[...] = jnp.zeros_like(dg_ref)

        dg_ref[...] += jnp.sum(dg, axis=0, keepdims=True)

    row = lambda w: pl.BlockSpec((tm, w), lambda i: (i, 0))
    return pl.pallas_call(
        body, name="in_bwd", grid=(S // tm,),
        in_specs=[row(2 * CONV_CH), row(CONV_CH), row(CONV_CH), row(CONV_CH), row(D_MODEL), row(D_MODEL),
                  _const_spec((1, D_MODEL)), _const_spec(wa.shape)],
        out_specs=[row(D_MODEL), row(2560), pl.BlockSpec((1, D_MODEL), lambda i: (0, 0))],
        out_shape=[jax.ShapeDtypeStruct((S, D_MODEL), F32), jax.ShapeDtypeStruct((S, 2560), BF16),
                   jax.ShapeDtypeStruct((1, D_MODEL), F32)],
        compiler_params=_cp(("arbitrary",)),
    )(duc, dq, dk, dv, x2, dh1, g1, wa)


def _matmul_tn(xm, ym, tm, tn, ts, name):
    S, M = xm.shape
    N = ym.shape[1]

    def body(x_ref, y_ref, o_ref):
        @pl.when(pl.program_id(2) == 0)
        def _():
            o_ref[...] = jnp.zeros_like(o_ref)

        o_ref[...] += _dot_tn(x_ref[...], y_ref[...])

    return pl.pallas_call(
        body, name=name, grid=(M // tm, N // tn, S // ts),
        in_specs=[pl.BlockSpec((ts, tm), lambda m, n, s: (s, m)), pl.BlockSpec((ts, tn), lambda m, n, s: (s, n))],
        out_specs=pl.BlockSpec((tm, tn), lambda m, n, s: (m, n)),
        out_shape=jax.ShapeDtypeStruct((M, N), F32),
        compiler_params=_cp(("parallel", "parallel", "arbitrary")),
    )(xm, ym)


def _sibling_halves(grads):
    n = len(grads)

    def body(*refs):
        ins, outs, ssem, rsem = refs[:n], refs[n:2 * n], refs[2 * n], refs[2 * n + 1]
        x, y, c = lax.axis_index("x"), lax.axis_index("y"), lax.axis_index("c")
        copies = []
        for k in range(n):
            for j in range(N_CHIPS):
                copies.append(pltpu.make_async_remote_copy(
                    src_ref=ins[k].at[j, 1 - c], dst_ref=outs[k].at[j],
                    send_sem=ssem.at[N_CHIPS * k + j], recv_sem=rsem.at[N_CHIPS * k + j],
                    device_id=(x, y, 1 - c), device_id_type=MESH))
        for cp in copies:
            cp.start()
        for cp in copies:
            cp.wait()

    shapes = [jax.ShapeDtypeStruct((g.shape[0],) + g.shape[2:], F32) for g in grads]
    return pl.pallas_call(
        body, name="grad_sibling_halves", out_shape=shapes,
        in_specs=[_hbm()] * n, out_specs=[_hbm()] * n,
        scratch_shapes=[pltpu.SemaphoreType.DMA((N_CHIPS * n,)), pltpu.SemaphoreType.DMA((N_CHIPS * n,))],
    )(*grads)


def _add_half(c_arr, g, landed, name):
    def body(c_ref, g_ref, l_ref, o_ref):
        o_ref[...] = (g_ref[...] + l_ref[...]).astype(BF16)

    rows, n = g.shape[2], g.shape[3]
    grid = (N_CHIPS,)
    g_spec = pl.BlockSpec((None, None, rows, n), lambda j, c: (j, c[0], 0, 0))
    l_spec = pl.BlockSpec((None, rows, n), lambda j, c: (j, 0, 0))
    return pl.pallas_call(
        body, name=name,
        grid_spec=pltpu.PrefetchScalarGridSpec(num_scalar_prefetch=1, grid=grid, in_specs=[g_spec, l_spec],
                                               out_specs=l_spec),
        out_shape=jax.ShapeDtypeStruct(landed.shape, BF16),
        compiler_params=_cp(("parallel",)),
    )(c_arr, g, landed)


def _chip_scatter(parts):
    n = len(parts)

    def piece(ref, j):
        return ref.at[j]

    def body(*refs):
        ins, outs = refs[:n], refs[n:2 * n]
        lsem, ssem, rsem = refs[2 * n:]
        x, y, c = lax.axis_index("x"), lax.axis_index("y"), lax.axis_index("c")
        me = 2 * x + y
        chips = [(1 - x, y), (x, 1 - y), (1 - x, 1 - y)]
        copies = []
        for k in range(n):
            cp = pltpu.make_async_copy(piece(ins[k], me), outs[k].at[me], lsem.at[k])
            cp.start()
            copies.append(cp)
            for r, chip in enumerate(chips):
                cp = pltpu.make_async_remote_copy(
                    src_ref=piece(ins[k], 2 * chip[0] + chip[1]), dst_ref=outs[k].at[me],
                    send_sem=ssem.at[3 * k + r], recv_sem=rsem.at[3 * k + r],
                    device_id=(chip[0], chip[1], c), device_id_type=MESH)
                cp.start()
                copies.append(cp)
        for cp in copies:
            cp.wait()

    shapes = [jax.ShapeDtypeStruct(p.shape, p.dtype) for p in parts]
    return pl.pallas_call(
        body, name="grad_chip_scatter", out_shape=shapes,
        in_specs=[_hbm()] * n, out_specs=[_hbm()] * n,
        scratch_shapes=[pltpu.SemaphoreType.DMA((n,)), pltpu.SemaphoreType.DMA((3 * n,)),
                        pltpu.SemaphoreType.DMA((3 * n,))],
    )(*parts)


def _sum_chips(landed, name):
    _, rows, n = landed.shape
    tr = 256 if rows % 256 == 0 else rows

    def body(a_ref, b_ref, c_ref, d_ref, o_ref):
        f = lambda ref: ref[...].astype(F32)
        o_ref[...] = ((f(a_ref) + f(b_ref)) + f(c_ref)) + f(d_ref)

    specs = [pl.BlockSpec((None, tr, n), functools.partial(lambda i, j: (j, i, 0), j=j)) for j in range(N_CHIPS)]
    return pl.pallas_call(
        body, name=name, grid=(rows // tr,), in_specs=specs,
        out_specs=pl.BlockSpec((tr, n), lambda i: (i, 0)),
        out_shape=jax.ShapeDtypeStruct((rows, n), F32),
        compiler_params=_cp(("parallel",)),
    )(landed, landed, landed, landed)


def _share_halves(halves):
    n = len(halves)

    def body(*refs):
        ins, outs = refs[:n], refs[n:2 * n]
        ssem, rsem = refs[2 * n:]
        x, y, c = lax.axis_index("x"), lax.axis_index("y"), lax.axis_index("c")
        copies = [pltpu.make_async_remote_copy(
            src_ref=ins[k], dst_ref=outs[k], send_sem=ssem.at[k], recv_sem=rsem.at[k],
            device_id=(x, y, 1 - c), device_id_type=MESH) for k in range(n)]
        for cp in copies:
            cp.start()
        for cp in copies:
            cp.wait()

    shapes = [jax.ShapeDtypeStruct(h.shape, F32) for h in halves]
    return pl.pallas_call(
        body, name="grad_share_halves", out_shape=shapes,
        in_specs=[_hbm()] * n, out_specs=[_hbm()] * n,
        scratch_shapes=[pltpu.SemaphoreType.DMA((n,)), pltpu.SemaphoreType.DMA((n,))],
    )(*halves)


def _allreduce_small(packed):
    rows, n = packed.shape

    def body(in_ref, out_ref, land_ref, ssem, rsem):
        x, y, c = lax.axis_index("x"), lax.axis_index("y"), lax.axis_index("c")
        me = 4 * x + 2 * y + c
        land_ref[me] = in_ref[...]
        copies = []
        for r in range(1, 8):
            tx = 1 - x if r & 4 else x
            ty = 1 - y if r & 2 else y
            tc = 1 - c if r & 1 else c
            cp = pltpu.make_async_remote_copy(
                src_ref=in_ref, dst_ref=land_ref.at[me], send_sem=ssem.at[r - 1], recv_sem=rsem.at[r - 1],
                device_id=(tx, ty, tc), device_id_type=MESH)
            cp.start()
            copies.append(cp)
        for cp in copies:
            cp.wait()
        acc = land_ref[0]
        for k in range(1, 8):
            acc = acc + land_ref[k]
        out_ref[...] = acc

    return pl.pallas_call(
        body, name="allreduce_small", out_shape=jax.ShapeDtypeStruct((rows, n), F32),
        in_specs=[pl.BlockSpec(memory_space=pltpu.VMEM)], out_specs=pl.BlockSpec(memory_space=pltpu.VMEM),
        scratch_shapes=[pltpu.VMEM((8, rows, n), F32), pltpu.SemaphoreType.DMA((7,)),
                        pltpu.SemaphoreType.DMA((7,))],
    )(packed)


def _adamw_math(w, g, m, v):
    m = ADAM_B1 * m + (1.0 - ADAM_B1) * g
    v = ADAM_B2 * v + (1.0 - ADAM_B2) * (g * g)
    m_hat = m / (1.0 - ADAM_B1 ** ADAM_STEP)
    v_hat = v / (1.0 - ADAM_B2 ** ADAM_STEP)
    return -ADAM_LR * (m_hat / (jnp.sqrt(v_hat) + ADAM_EPS) + ADAM_WD * w), m, v


def _adamw_halves(c_arr, w, mine, other, m, v, name):
    rows, n = mine.shape
    tr = 256 if rows % 256 == 0 else rows
    nb = rows // tr

    def body(c_ref, w_ref, a_ref, b_ref, m_ref, v_ref, g_ref, d_ref, mo_ref, vo_ref):
        g = jnp.where(pl.program_id(0) == c_ref[0], a_ref[...], b_ref[...])
        g_ref[...] = g
        d_ref[...], mo_ref[...], vo_ref[...] = _adamw_math(w_ref[...], g, m_ref[...], v_ref[...])

    full = pl.BlockSpec((tr, n), lambda h, i, c: (h * nb + i, 0))
    half = pl.BlockSpec((tr, n), lambda h, i, c: (i, 0))
    return pl.pallas_call(
        body, name=name,
        grid_spec=pltpu.PrefetchScalarGridSpec(num_scalar_prefetch=1, grid=(2, nb),
                                               in_specs=[full, half, half, full, full], out_specs=[full] * 4),
        out_shape=[jax.ShapeDtypeStruct((2 * rows, n), F32)] * 4,
        compiler_params=_cp(("parallel", "parallel")),
    )(c_arr, w, mine, other, m, v)


def _adamw(w, g, m, v, name):
    rows, n = w.shape
    tr = 256 if rows % 256 == 0 else rows

    def body(w_ref, g_ref, m_ref, v_ref, d_ref, mo_ref, vo_ref):
        d_ref[...], mo_ref[...], vo_ref[...] = _adamw_math(w_ref[...], g_ref[...], m_ref[...], v_ref[...])

    spec = pl.BlockSpec((tr, n), lambda i: (i, 0))
    return pl.pallas_call(
        body, name=name, grid=(rows // tr,), in_specs=[spec] * 4, out_specs=[spec] * 3,
        out_shape=[jax.ShapeDtypeStruct((rows, n), F32)] * 3,
        compiler_params=_cp(("parallel",)),
    )(w, g, m, v)


def _rows8(a):
    a = a.reshape(-1, 128)
    return jnp.pad(a, ((0, (-a.shape[0]) % 8), (0, 0)))


def kernel(x, g_pre_mix, w_in, conv_w, conv_b, conv_ln_g, conv_ln_b, attn_norm_g, w_out, g_post_mix, g_pre_ffn, w_gate, w_up, w_down, g_post_ffn, loss_target, m_g_pre_mix, m_w_in, m_conv_w, m_conv_b, m_conv_ln_g, m_conv_ln_b, m_attn_norm_g, m_w_out, m_g_post_mix, m_g_pre_ffn, m_w_gate, m_w_up, m_w_down, m_g_post_ffn, v_g_pre_mix, v_w_in, v_conv_w, v_conv_b, v_conv_ln_g, v_conv_ln_b, v_attn_norm_g, v_w_out, v_g_post_mix, v_g_pre_ffn, v_w_gate, v_w_up, v_w_down, v_g_post_ffn):
    S = x.shape[1]
    tm_big = min(512, S)
    tm_ffn = min(256, S)
    t_att = min(256, S // KEY_BLOCKS)
    chip = 2 * lax.axis_index("x") + lax.axis_index("y")
    core = lax.axis_index("c")
    x2 = x.reshape(S, D_MODEL)
    tgt = loss_target.reshape(S, D_MODEL)
    ag = attn_norm_g.reshape(1, CONV_CH)

    a_sh = w_in[0].astype(BF16)
    b_sh = jnp.stack([w_gate[0], w_up[0]]).astype(BF16)
    c_sh = jnp.concatenate([w_out[0], w_down[0]], axis=0).astype(BF16)
    cw_sh = jnp.pad(conv_w[0, :, 0, :], ((0, 1), (0, 0)))
    wa4, wb4, wc4, cw4 = _gather_weights(a_sh, b_sh, c_sh, cw_sh)
    cols = lambda w4: jnp.transpose(w4, (1, 0, 2)).reshape(w4.shape[1], N_CHIPS * w4.shape[2])
    wa = cols(wa4)
    wg, wu = cols(wb4[:, 0]), cols(wb4[:, 1])
    wo = wc4[:, :OUT_SH].reshape(D_MODEL, D_MODEL)
    wd = wc4[:, OUT_SH:].reshape(D_FF, D_MODEL)
    cwf = cols(cw4)

    a_bf, uc, qkv = _in_proj(x2, g_pre_mix, wa, tm_big)
    conv_out = _conv_fwd(uc, cwf, conv_b, conv_ln_g, conv_ln_b, tm_big)
    o = _attn_fwd(qkv, t_att)
    mixed, yv, h1, f_in = _out_proj(conv_out, o, ag, wo, x2, g_post_mix, g_pre_ffn, tm_big)
    df, dh2, dg4, loss_part = _ffn_fwd(f_in, h1, tgt, wg, wu, wd, g_post_ffn, tm_ffn)

    act, dgt, dup, dh1, dy, dg3, dg2 = _ffn_bwd(f_in, df, dh2, h1, yv, wg, wu, wd, g_pre_ffn, g_post_mix, tm_ffn)
    dco, do, dag = _out_bwd(dy, o, ag, wo, tm_big)
    dq, dk, dv = _attn_bwd(qkv, do, t_att)
    duc, dcw, dcb, dlg, dlb = _conv_bwd(uc, dco, cwf, conv_b, conv_ln_g, conv_ln_b, tm_big)
    grad_x, du, dg1 = _in_bwd(duc, dq, dk, dv, x2, dh1, g_pre_mix, wa, tm_big)
    ts = min(512, S)
    gw_in = _matmul_tn(a_bf, du, D_MODEL, 1280, ts, "grad_w_in")
    gw_out = _matmul_tn(mixed, dy, D_MODEL, D_MODEL, ts, "grad_w_out")
    gw_gate = _matmul_tn(f_in, dgt, D_MODEL, D_FF // 2, ts, "grad_w_gate")
    gw_up = _matmul_tn(f_in, dup, D_MODEL, D_FF // 2, ts, "grad_w_up")
    gw_down = _matmul_tn(act, df, D_FF // 2, D_MODEL, ts, "grad_w_down")

    by_cols = lambda g: jnp.transpose(g.reshape(2, D_MODEL // 2, N_CHIPS, -1), (2, 0, 1, 3))
    by_rows = lambda g: g.reshape(N_CHIPS, 2, g.shape[0] // (2 * N_CHIPS), g.shape[1])
    views = [by_cols(gw_in), by_cols(gw_gate), by_cols(gw_up), by_rows(gw_out), by_rows(gw_down)]
    names = ["w_in", "w_gate", "w_up", "w_out", "w_down"]
    landed = _sibling_halves(views)
    c_arr = core.reshape(1).astype(jnp.int32)
    parts = [_add_half(c_arr, g, l, "grad_half_" + nm) for g, l, nm in zip(views, landed, names)]
    slots = _chip_scatter(parts)
    halves = [_sum_chips(s, "grad_sum_" + nm) for s, nm in zip(slots, names)]
    others = _share_halves(halves)
    mine = dict(zip(names, halves))
    other = dict(zip(names, others))

    small = [dg1, dcb, dlg, dlb, dag, dg2, dg3, dg4]
    packed = jnp.concatenate([_rows8(s) for s in small] + [_rows8(dcw), _rows8(loss_part)], axis=0)
    red = _allreduce_small(packed)
    sizes = [D_MODEL, CONV_CH, CONV_CH, CONV_CH, CONV_CH, D_MODEL, D_MODEL, D_MODEL]
    g_small = [red[8 * k:8 * k + n // 128].reshape(1, n) for k, n in enumerate(sizes)]
    cw_red = red[64:64 + 128].reshape(HALO, CONV_CH)
    g_cw = lax.dynamic_slice(cw_red, (0, chip * 128), (HALO, 128))
    loss = red[192, 0]

    big = []
    for w, m, v, nm in [(w_in, m_w_in, v_w_in, "w_in"), (w_out, m_w_out, v_w_out, "w_out"),
                        (w_gate, m_w_gate, v_w_gate, "w_gate"), (w_up, m_w_up, v_w_up, "w_up"),
                        (w_down, m_w_down, v_w_down, "w_down")]:
        big.append(_adamw_halves(c_arr, w[0], mine[nm], other[nm], m[0], v[0], "adamw_" + nm))
    sm_w = [g_pre_mix, conv_b, conv_ln_g, conv_ln_b, ag, g_post_mix, g_pre_ffn, g_post_ffn]
    sm_m = [m_g_pre_mix, m_conv_b, m_conv_ln_g, m_conv_ln_b, m_attn_norm_g, m_g_post_mix, m_g_pre_ffn, m_g_post_ffn]
    sm_v = [v_g_pre_mix, v_conv_b, v_conv_ln_g, v_conv_ln_b, v_attn_norm_g, v_g_post_mix, v_g_pre_ffn, v_g_post_ffn]
    pad_cw = lambda a: jnp.pad(a[0, :, 0, :], ((0, 1), (0, 0)))

    def pack(vecs, cw):
        return jnp.concatenate([_rows8(a) for a in vecs] + [cw], axis=0)

    sd, smn, svn = _adamw(pack(sm_w, pad_cw(conv_w)), pack(g_small, g_cw), pack(sm_m, pad_cw(m_conv_w)),
                          pack(sm_v, pad_cw(v_conv_w)), "adamw_small")

    def unpack(p):
        vecs = [p[8 * k:8 * k + n // 128].reshape(1, n) for k, n in enumerate(sizes)]
        return vecs, p[64:64 + CONV_WIDTH].reshape(1, CONV_WIDTH, 1, 128)

    def ordered(vecs, cw, w_in_, w_out_, w_gate_, w_up_, w_down_):
        g1_, cb_, lg_, lb_, ag_, g2_, g3_, g4_ = vecs
        return [g1_, w_in_[None], cw, cb_, lg_, lb_, ag_.reshape(1, 8, HEAD_DIM), w_out_[None], g2_, g3_,
                w_gate_[None], w_up_[None], w_down_[None], g4_]

    grads = ordered(g_small, g_cw[:CONV_WIDTH].reshape(1, CONV_WIDTH, 1, 128), *[b[0] for b in big])
    outs = []
    for idx, p in enumerate((sd, smn, svn)):
        vecs, cw = unpack(p)
        outs += ordered(vecs, cw, *[b[idx + 1] for b in big])
    return (loss, grad_x.reshape(1, S, D_MODEL), *grads, *outs)
```

```python
import functools
import math

import jax
import jax.numpy as jnp
from jax import lax
from jax.experimental import pallas as pl
from jax.experimental.pallas import tpu as pltpu

F32 = jnp.float32
BF16 = jnp.bfloat16
MESH = pl.DeviceIdType.MESH

D_MODEL = 1024
CONV_CH = 512
CONV_WIDTH = 31
HEAD_DIM = 64
PAIR = 2 * HEAD_DIM
N_PAIRS = 4
D_FF = 2816
N_CHIPS = 4
IN_SH = 2560 // N_CHIPS
FF_SH = D_FF // N_CHIPS
OUT_SH = D_MODEL // N_CHIPS
C_ROWS = OUT_SH + FF_SH
EPS = 1e-6
HALO = 32

ADAM_LR = 0.001
ADAM_B1 = 0.9
ADAM_B2 = 0.999
ADAM_EPS = 1e-08
ADAM_WD = 0.01
ADAM_STEP = 10

VMEM_LIMIT = 56 * 2 ** 20


def _cp(sem=None, vmem=VMEM_LIMIT):
    return pltpu.CompilerParams(dimension_semantics=sem, vmem_limit_bytes=vmem)


def _hbm():
    return pl.BlockSpec(memory_space=pltpu.HBM)


def _const_spec(shape):
    nd = len(shape)
    return pl.BlockSpec(shape, lambda *_: (0,) * nd, pipeline_mode=pl.Buffered(1))


def _dot(a, b):
    return jnp.dot(a, b, preferred_element_type=F32)


def _dot_nt(a, b):
    return lax.dot_general(a, b, (((1,), (1,)), ((), ())), preferred_element_type=F32)


def _dot_tn(a, b):
    return lax.dot_general(a, b, (((0,), (0,)), ((), ())), preferred_element_type=F32)


def _split3(x):
    b0 = x.astype(BF16)
    r1 = x - b0.astype(F32)
    b1 = r1.astype(BF16)
    b2 = (r1 - b1.astype(F32)).astype(BF16)
    return b0, b1, b2


def _split2(x):
    hi = x.astype(BF16)
    lo = (x - hi.astype(F32)).astype(BF16)
    return hi, lo


def _sigmoid(x):
    return 1.0 / (1.0 + jnp.exp(-x))


def _head_mean(x, seg):
    b0, b1, b2 = _split3(x)
    return (_dot(b0, seg) + _dot(b1, seg) + _dot(b2, seg)) * (1.0 / HEAD_DIM)


def _seg_matrix(n):
    r = lax.broadcasted_iota(jnp.int32, (n, n), 0) // HEAD_DIM
    c = lax.broadcasted_iota(jnp.int32, (n, n), 1) // HEAD_DIM
    return (r == c).astype(BF16)


def _rms(x):
    return lax.rsqrt(jnp.mean(x * x, axis=-1, keepdims=True) + EPS)


def _rms_bwd(dy, n, r, g):
    dn = dy * g
    dx = r * (dn - n * jnp.mean(dn * n, axis=-1, keepdims=True))
    return dx, dy * n


def _gather_weights(a_sh, b_sh, c_sh, w_sh):
    ha, hc, hw = a_sh.shape[0] // 2, c_sh.shape[0] // 2, w_sh.shape[0] // 2

    def body(a_ref, b_ref, c_ref, w_ref, ao, bo, co, wo, ssem, rsem):
        x, y, c = lax.axis_index("x"), lax.axis_index("y"), lax.axis_index("c")
        me = 2 * x + y
        sibling = (x, y, 1 - c)
        chips = [(1 - x, y), (x, 1 - y), (1 - x, 1 - y)]

        def src_half(i, h):
            return [a_ref.at[pl.ds(h * ha, ha), :], b_ref.at[h], c_ref.at[pl.ds(h * hc, hc), :],
                    w_ref.at[pl.ds(h * hw, hw), :]][i]

        def out_half(i, j, h):
            return [ao.at[j, pl.ds(h * ha, ha), :], bo.at[j, h], co.at[j, pl.ds(h * hc, hc), :],
                    wo.at[j, pl.ds(h * hw, hw), :]][i]

        def ici(i, k, chip, origin):
            return pltpu.make_async_remote_copy(
                src_ref=src_half(i, c), dst_ref=out_half(i, origin, c),
                send_sem=ssem.at[6 * i + k], recv_sem=rsem.at[6 * i + k],
                device_id=(chip[0], chip[1], c), device_id_type=MESH)

        def d2d(i, k, origin, h):
            return pltpu.make_async_remote_copy(
                src_ref=out_half(i, origin, h), dst_ref=out_half(i, origin, h),
                send_sem=ssem.at[6 * i + 3 + k], recv_sem=rsem.at[6 * i + 3 + k],
                device_id=sibling, device_id_type=MESH)

        sends = []
        for i in range(4):
            for k, chip in enumerate(chips):
                cp = ici(i, k, chip, me)
                cp.start()
                sends.append(cp)
        for i in range(4):
            for k, chip in enumerate(chips):
                origin = 2 * chip[0] + chip[1]
                ici(i, k, chip, origin).wait_recv()
                cp = d2d(i, k, origin, c)
                cp.start()
                sends.append(cp)
        for i in range(4):
            for k, chip in enumerate(chips):
                origin = 2 * chip[0] + chip[1]
                d2d(i, k, origin, 1 - c).wait_recv()
        for cp in sends:
            cp.wait_send()

    shapes = [jax.ShapeDtypeStruct((N_CHIPS,) + s.shape, s.dtype) for s in (a_sh, b_sh, c_sh, w_sh)]
    return pl.pallas_call(
        body, name="gather_weights", out_shape=shapes,
        in_specs=[_hbm()] * 4, out_specs=[_hbm()] * 4,
        scratch_shapes=[pltpu.SemaphoreType.DMA((24,)), pltpu.SemaphoreType.DMA((24,))],
    )(a_sh, b_sh, c_sh, w_sh)


def _in_proj(x2, g1, wa, tm):
    S = x2.shape[0]

    def body(x_ref, g_ref, w_ref, a_ref, uc_ref, qkv_ref):
        x = x_ref[...]
        a = (x * _rms(x) * g_ref[...]).astype(BF16)
        a_ref[...] = a
        uc_ref[...] = _dot(a, w_ref[:, 0:2 * CONV_CH])
        qkv_ref[...] = _dot(a, w_ref[:, 2 * CONV_CH:]).astype(BF16)

    return pl.pallas_call(
        body, name="in_proj", grid=(S // tm,),
        in_specs=[pl.BlockSpec((tm, D_MODEL), lambda i: (i, 0)), _const_spec((1, D_MODEL)),
                  _const_spec(wa.shape)],
        out_specs=[pl.BlockSpec((tm, D_MODEL), lambda i: (i, 0)),
                   pl.BlockSpec((tm, 2 * CONV_CH), lambda i: (i, 0)),
                   pl.BlockSpec((tm, 1536), lambda i: (i, 0))],
        out_shape=[jax.ShapeDtypeStruct((S, D_MODEL), BF16), jax.ShapeDtypeStruct((S, 2 * CONV_CH), F32),
                   jax.ShapeDtypeStruct((S, 1536), BF16)],
        compiler_params=_cp(("parallel",)),
    )(x2, g1, wa)


SUBLANES = 8


def _shift_copies(src_ref, sh_ref):
    rows = sh_ref.shape[1]
    for b in range(1, SUBLANES):
        sh_ref[b - 1] = src_ref[pl.ds(b, rows), :]


def _rows_at(src_ref, sh_ref, off, rows):
    a, b = divmod(off, SUBLANES)
    if b == 0:
        return src_ref[pl.ds(SUBLANES * a, rows), :]
    return sh_ref[b - 1, pl.ds(SUBLANES * a, rows), :]


def _conv_taps(cw_ref, src_ref, sh_ref, offs, rows):
    acc = None
    for w, off in enumerate(offs):
        term = cw_ref[w:w + 1, :] * _rows_at(src_ref, sh_ref, off, rows)
        acc = term if acc is None else acc + term
    return acc


def _glu(uc):
    return uc[:, :CONV_CH] * _sigmoid(uc[:, CONV_CH:])


def _conv_fwd(uc, cwf, cb, lg, lb, tm):
    S = uc.shape[0]
    hb = tm // HALO

    def body(uc_ref, prev_ref, cw_ref, cb_ref, lg_ref, lb_ref, out_ref, glu_ref, sh_ref):
        i = pl.program_id(0)
        glu_ref[0:HALO, :] = jnp.where(i == 0, 0.0, _glu(prev_ref[...]))
        glu_ref[HALO:HALO + tm, :] = _glu(uc_ref[...])
        glu_ref[HALO + tm:HALO + tm + SUBLANES, :] = jnp.zeros((SUBLANES, CONV_CH), F32)
        _shift_copies(glu_ref, sh_ref)
        offs = [HALO - (CONV_WIDTH - 1) + w for w in range(CONV_WIDTH)]
        y = _conv_taps(cw_ref, glu_ref, sh_ref, offs, tm) + cb_ref[...]
        mu = jnp.mean(y, axis=-1, keepdims=True)
        yc = y - mu
        rstd = lax.rsqrt(jnp.mean(yc * yc, axis=-1, keepdims=True) + EPS)
        ln = yc * rstd * lg_ref[...] + lb_ref[...]
        out_ref[...] = (ln * _sigmoid(ln)).astype(BF16)

    return pl.pallas_call(
        body, name="conv_fwd", grid=(S // tm,),
        in_specs=[pl.BlockSpec((tm, 2 * CONV_CH), lambda i: (i, 0)),
                  pl.BlockSpec((HALO, 2 * CONV_CH), lambda i: (jnp.maximum(i * hb - 1, 0), 0)),
                  _const_spec(cwf.shape), _const_spec((1, CONV_CH)), _const_spec((1, CONV_CH)),
                  _const_spec((1, CONV_CH))],
        out_specs=pl.BlockSpec((tm, CONV_CH), lambda i: (i, 0)),
        out_shape=jax.ShapeDtypeStruct((S, CONV_CH), BF16),
        scratch_shapes=[pltpu.VMEM((HALO + tm + SUBLANES, CONV_CH), F32),
                        pltpu.VMEM((SUBLANES - 1, HALO + tm, CONV_CH), F32)],
        compiler_params=_cp(("parallel",)),
    )(uc, uc, cwf, cb, lg, lb)


def _lane_mask(h):
    lane = lax.broadcasted_iota(jnp.int32, (1, PAIR), 1)
    return (lane >= HEAD_DIM * h) & (lane < HEAD_DIM * (h + 1))


def _neg_abs(x):
    bits = lax.bitcast_convert_type(x, jnp.uint32) | jnp.uint32(0x80000000)
    return lax.bitcast_convert_type(bits, F32)


def _tri_dot(x, m):
    return _dot(x.astype(BF16), m)


MASKED = -1e30
KEY_BLOCKS = 4


def _running_sums(x, m, reverse):
    t = m.shape[0]
    order = range(KEY_BLOCKS - 1, -1, -1) if reverse else range(KEY_BLOCKS)
    out = [None] * KEY_BLOCKS
    carry = None
    for b in order:
        xb = x[:, b * t:(b + 1) * t]
        cb = _tri_dot(xb, m)
        out[b] = cb if carry is None else cb + carry
        rs = jnp.sum(xb, axis=1, keepdims=True)
        carry = rs if carry is None else carry + rs
    return jnp.concatenate(out, axis=1), carry


def _sb_tile(z, r, m_suf):
    sp = jnp.maximum(z, 0.0) + jnp.log(1.0 + jnp.exp(_neg_abs(z)))
    c, rs = _running_sums(sp, m_suf, reverse=True)
    ex = z - c
    if r is not None:
        ex = ex - r
    return jnp.exp(ex), sp, rs


def _scores(qm, kt, mask):
    z = _dot_nt(qm, kt)
    return z if mask is None else jnp.where(mask, z, MASKED)


def _causal_mask(i, sb, t):
    row = lax.broadcasted_iota(jnp.int32, (t, KEY_BLOCKS * t), 0) + i * t
    col = lax.broadcasted_iota(jnp.int32, (t, KEY_BLOCKS * t), 1) + sb * (KEY_BLOCKS * t)
    return col < row


def _sweep(first, count, down, tile):
    tile(first, True)

    def step(n, carry):
        tile(first - 1 - n if down else first + 1 + n, False)
        return carry

    lax.fori_loop(0, count, step, 0)


def _suffix_matrix(t, prefix=False):
    row = lax.broadcasted_iota(jnp.int32, (t, t), 0)
    col = lax.broadcasted_iota(jnp.int32, (t, t), 1)
    return ((row <= col) if prefix else (row >= col)).astype(BF16)


def _attn_fwd(qkv, t):
    S = qkv.shape[0]
    tk = KEY_BLOCKS * t

    def body(q_ref, k_ref, v_ref, o_ref, acc_ref, r_ref):
        i = pl.program_id(1)
        last = i // KEY_BLOCKS
        m_suf = _suffix_matrix(t)
        q = q_ref[...]
        hms = [_lane_mask(h) for h in range(2)]
        qms = [jnp.where(hm, q, 0) * 0.125 for hm in hms]
        acc_ref[...] = jnp.zeros_like(acc_ref)
        r_ref[...] = jnp.zeros_like(r_ref)

        def tile(sb, diagonal):
            rows = pl.ds(pl.multiple_of(sb * tk, tk), tk)
            kt = k_ref[rows, :]
            vt = v_ref[rows, :]
            mask = _causal_mask(i, sb, t) if diagonal else None
            for h in range(2):
                a_loc, _, rs = _sb_tile(_scores(qms[h], kt, mask), None, m_suf)
                r = r_ref[h]
                acc_ref[...] += _dot(a_loc.astype(BF16), jnp.where(hms[h], vt, 0)) * jnp.exp(-r)
                r_ref[h] = r + rs

        _sweep(last, last, True, tile)
        o_ref[...] = acc_ref[...]

    return pl.pallas_call(
        body, name="attn_fwd", grid=(N_PAIRS, S // t),
        in_specs=[pl.BlockSpec((t, PAIR), lambda p, i: (i, p)),
                  pl.BlockSpec((S, PAIR), lambda p, i: (0, N_PAIRS + p)),
                  pl.BlockSpec((S, PAIR), lambda p, i: (0, 2 * N_PAIRS + p))],
        out_specs=pl.BlockSpec((t, PAIR), lambda p, i: (i, p)),
        out_shape=jax.ShapeDtypeStruct((S, N_PAIRS * PAIR), F32),
        scratch_shapes=[pltpu.VMEM((t, PAIR), F32), pltpu.VMEM((2, t, 1), F32)],
        compiler_params=_cp(("parallel", "arbitrary")),
    )(qkv, qkv, qkv)


def _out_proj(conv_out, o, ag, wc, x2, g2, g3, tm):
    S = o.shape[0]

    def body(co_ref, o_ref, ag_ref, w_ref, x_ref, g2_ref, g3_ref, mix_ref, y_ref, h1_ref, fin_ref):
        seg = _seg_matrix(CONV_CH)
        o = o_ref[...]
        att = (o * lax.rsqrt(_head_mean(o * o, seg) + EPS) * ag_ref[...]).astype(BF16)
        co = co_ref[...]
        mix_ref[:, :CONV_CH] = co
        mix_ref[:, CONV_CH:] = att
        y = _dot(co, w_ref[0:CONV_CH, :]) + _dot(att, w_ref[CONV_CH:, :])
        y_ref[...] = y
        h1 = x_ref[...] + y * _rms(y) * g2_ref[...]
        h1_ref[...] = h1
        fin_ref[...] = (h1 * _rms(h1) * g3_ref[...]).astype(BF16)

    row = lambda w: pl.BlockSpec((tm, w), lambda i: (i, 0))
    return pl.pallas_call(
        body, name="out_proj", grid=(S // tm,),
        in_specs=[row(CONV_CH), row(CONV_CH), _const_spec((1, CONV_CH)), _const_spec(wc.shape),
                  row(D_MODEL), _const_spec((1, D_MODEL)), _const_spec((1, D_MODEL))],
        out_specs=[row(D_MODEL)] * 4,
        out_shape=[jax.ShapeDtypeStruct((S, D_MODEL), BF16), jax.ShapeDtypeStruct((S, D_MODEL), F32),
                   jax.ShapeDtypeStruct((S, D_MODEL), F32), jax.ShapeDtypeStruct((S, D_MODEL), BF16)],
        compiler_params=_cp(("parallel",)),
    )(conv_out, o, ag, wc, x2, g2, g3)


def _ffn_fwd(f_in, h1, tgt, wg, wu, wd, g4, tm):
    S = f_in.shape[0]

    def body(fin_ref, h1_ref, tgt_ref, wg_ref, wu_ref, wd_ref, g4_ref, df_ref, dh2_ref, dg4_ref, loss_ref):
        i = pl.program_id(0)
        fin = fin_ref[...]
        gt = _dot(fin, wg_ref[...])
        up = _dot(fin, wu_ref[...])
        f = _dot((gt * _sigmoid(gt) * up).astype(BF16), wd_ref[...])
        r = _rms(f)
        n = f * r
        g4 = g4_ref[...]
        err = h1_ref[...] + n * g4 - tgt_ref[...]
        dh2 = err * (1.0 / D_MODEL)
        dh2_ref[...] = dh2
        df, dg = _rms_bwd(dh2, n, r, g4)
        df_ref[...] = df.astype(BF16)

        @pl.when(i == 0)
        def _():
            dg4_ref[...] = jnp.zeros_like(dg4_ref)
            loss_ref[...] = jnp.zeros_like(loss_ref)

        dg4_ref[...] += jnp.sum(dg, axis=0, keepdims=True)
        part = jnp.sum(jnp.sum(err * err, axis=1, keepdims=True), axis=0, keepdims=True)
        loss_ref[...] += part * (0.5 / D_MODEL)

    row = lambda w: pl.BlockSpec((tm, w), lambda i: (i, 0))
    return pl.pallas_call(
        body, name="ffn_fwd", grid=(S // tm,),
        in_specs=[row(D_MODEL), row(D_MODEL), row(D_MODEL), _const_spec(wg.shape), _const_spec(wu.shape),
                  _const_spec(wd.shape), _const_spec((1, D_MODEL))],
        out_specs=[row(D_MODEL), row(D_MODEL), pl.BlockSpec((1, D_MODEL), lambda i: (0, 0)),
                   pl.BlockSpec((1, 128), lambda i: (0, 0))],
        out_shape=[jax.ShapeDtypeStruct((S, D_MODEL), BF16), jax.ShapeDtypeStruct((S, D_MODEL), F32),
                   jax.ShapeDtypeStruct((1, D_MODEL), F32), jax.ShapeDtypeStruct((1, 128), F32)],
        compiler_params=_cp(("arbitrary",)),
    )(f_in, h1, tgt, wg, wu, wd, g4)


def _ffn_bwd(f_in, df, dh2, h1, yv, wg, wu, wd, g3, g2, tm):
    S = f_in.shape[0]

    def body(fin_ref, df_ref, dh2_ref, h1_ref, y_ref, wg_ref, wu_ref, wd_ref, g3_ref, g2_ref,
             act_ref, dgt_ref, dup_ref, dh1_ref, dy_ref, dg3_ref, dg2_ref):
        i = pl.program_id(0)
        fin = fin_ref[...]
        df = df_ref[...]
        gt = _dot(fin, wg_ref[...])
        up = _dot(fin, wu_ref[...])
        sg = _sigmoid(gt)
        silu = gt * sg
        act_ref[...] = (silu * up).astype(BF16)
        dact = _dot_nt(df, wd_ref[...])
        dgt = (dact * up * (sg * (1.0 + gt * (1.0 - sg)))).astype(BF16)
        dup = (dact * silu).astype(BF16)
        dgt_ref[...] = dgt
        dup_ref[...] = dup
        dfin = _dot_nt(dgt, wg_ref[...]) + _dot_nt(dup, wu_ref[...])
        h1 = h1_ref[...]
        r3 = _rms(h1)
        dh1_n, dg3 = _rms_bwd(dfin, h1 * r3, r3, g3_ref[...])
        dh1 = dh2_ref[...] + dh1_n
        dh1_ref[...] = dh1
        y = y_ref[...]
        r2 = _rms(y)
        dy, dg2 = _rms_bwd(dh1, y * r2, r2, g2_ref[...])
        dy_ref[...] = dy.astype(BF16)

        @pl.when(i == 0)
        def _():
            dg3_ref[...] = jnp.zeros_like(dg3_ref)
            dg2_ref[...] = jnp.zeros_like(dg2_ref)

        dg3_ref[...] += jnp.sum(dg3, axis=0, keepdims=True)
        dg2_ref[...] += jnp.sum(dg2, axis=0, keepdims=True)

    row = lambda w: pl.BlockSpec((tm, w), lambda i: (i, 0))
    vec = pl.BlockSpec((1, D_MODEL), lambda i: (0, 0))
    return pl.pallas_call(
        body, name="ffn_bwd", grid=(S // tm,),
        in_specs=[row(D_MODEL)] * 5 + [_const_spec(wg.shape), _const_spec(wu.shape), _const_spec(wd.shape),
                                       _const_spec((1, D_MODEL)), _const_spec((1, D_MODEL))],
        out_specs=[row(D_FF), row(D_FF), row(D_FF), row(D_MODEL), row(D_MODEL), vec, vec],
        out_shape=[jax.ShapeDtypeStruct((S, D_FF), BF16)] * 3
        + [jax.ShapeDtypeStruct((S, D_MODEL), F32), jax.ShapeDtypeStruct((S, D_MODEL), BF16),
           jax.ShapeDtypeStruct((1, D_MODEL), F32), jax.ShapeDtypeStruct((1, D_MODEL), F32)],
        compiler_params=_cp(("arbitrary",)),
    )(f_in, df, dh2, h1, yv, wg, wu, wd, g3, g2)


def _out_bwd(dy, o, ag, wc, tm):
    S = o.shape[0]

    def body(dy_ref, o_ref, ag_ref, w_ref, dco_ref, do_ref, dag_ref):
        i = pl.program_id(0)
        seg = _seg_matrix(CONV_CH)
        dy = dy_ref[...]
        dco_ref[...] = _dot_nt(dy, w_ref[0:CONV_CH, :])
        datt = _dot_nt(dy, w_ref[CONV_CH:, :])
        o = o_ref[...]
        r = lax.rsqrt(_head_mean(o * o, seg) + EPS)
        n = o * r
        dn = datt * ag_ref[...]
        do_ref[...] = (r * (dn - n * _head_mean(dn * n, seg))).astype(BF16)

        @pl.when(i == 0)
        def _():
            dag_ref[...] = jnp.zeros_like(dag_ref)

        dag_ref[...] += jnp.sum(datt * n, axis=0, keepdims=True)

    row = lambda w: pl.BlockSpec((tm, w), lambda i: (i, 0))
    return pl.pallas_call(
        body, name="out_bwd", grid=(S // tm,),
        in_specs=[row(D_MODEL), row(CONV_CH), _const_spec((1, CONV_CH)), _const_spec(wc.shape)],
        out_specs=[row(CONV_CH), row(CONV_CH), pl.BlockSpec((1, CONV_CH), lambda i: (0, 0))],
        out_shape=[jax.ShapeDtypeStruct((S, CONV_CH), F32), jax.ShapeDtypeStruct((S, CONV_CH), BF16),
                   jax.ShapeDtypeStruct((1, CONV_CH), F32)],
        compiler_params=_cp(("arbitrary",)),
    )(dy, o, ag, wc)


def _attn_bwd(qkv, do, t):
    S = qkv.shape[0]
    tk = KEY_BLOCKS * t
    nk = S // tk

    def body(q_ref, k_ref, v_ref, do_ref, dq_ref, dk_hbm, dv_hbm, g_buf, s_buf, r_ref, dq_acc, dk_ref, dv_ref):
        p = pl.program_id(0)
        i = pl.program_id(1)
        last = i // KEY_BLOCKS

        @pl.when(i == 0)
        def _():
            dk_ref[...] = jnp.zeros_like(dk_ref)
            dv_ref[...] = jnp.zeros_like(dv_ref)

        m_suf = _suffix_matrix(t)
        m_pre = _suffix_matrix(t, prefix=True)
        q = q_ref[...]
        do = do_ref[...]
        hms = [_lane_mask(h) for h in range(2)]
        qms = [jnp.where(hm, q, 0) * 0.125 for hm in hms]
        doms = [jnp.where(hm, do, 0) for hm in hms]
        dq_acc[...] = jnp.zeros_like(dq_acc)
        r_ref[...] = jnp.zeros_like(r_ref)

        def rows(sb):
            return pl.ds(pl.multiple_of(sb * tk, tk), tk)

        def sweep1(sb, diagonal):
            kt = k_ref[rows(sb), :]
            vt = v_ref[rows(sb), :]
            mask = _causal_mask(i, sb, t) if diagonal else None
            dv = jnp.zeros((tk, PAIR), F32)
            for h in range(2):
                A, sp, rs = _sb_tile(_scores(qms[h], kt, mask), r_ref[h], m_suf)
                g_buf[h, sb] = A * _dot_nt(doms[h], vt)
                s_buf[h, sb] = 1.0 - jnp.exp(-sp)
                dv = dv + _dot_tn(A.astype(BF16), doms[h])
                r_ref[h] += rs
            dv_ref[rows(sb), :] += dv

        _sweep(last, last, True, sweep1)
        r_ref[...] = jnp.zeros_like(r_ref)

        def sweep2(sb, first):
            kt = k_ref[rows(sb), :]
            dk = jnp.zeros((tk, PAIR), F32)
            for h in range(2):
                g = g_buf[h, sb]
                pre, rs = _running_sums(g, m_pre, reverse=False)
                dzb = (g - s_buf[h, sb] * (pre + r_ref[h])).astype(BF16)
                dq_acc[...] += _dot(dzb, jnp.where(hms[h], kt, 0))
                dk = dk + _dot_tn(dzb, qms[h])
                r_ref[h] += rs
            dk_ref[rows(sb), :] += dk

        _sweep(0, last, False, sweep2)
        dq_ref[...] = dq_acc[...] * 0.125

        @pl.when(i == S // t - 1)
        def _():
            cols = pl.ds(pl.multiple_of(p * PAIR, PAIR), PAIR)
            pltpu.sync_copy(dk_ref, dk_hbm.at[:, cols])
            pltpu.sync_copy(dv_ref, dv_hbm.at[:, cols])

    once = lambda cb: pl.BlockSpec((S, PAIR), cb, pipeline_mode=pl.Buffered(1))
    return pl.pallas_call(
        body, name="attn_bwd", grid=(N_PAIRS, S // t),
        in_specs=[pl.BlockSpec((t, PAIR), lambda p, i: (i, p)),
                  once(lambda p, i: (0, N_PAIRS + p)), once(lambda p, i: (0, 2 * N_PAIRS + p)),
                  pl.BlockSpec((t, PAIR), lambda p, i: (i, p))],
        out_specs=[pl.BlockSpec((t, PAIR), lambda p, i: (i, p)), _hbm(), _hbm()],
        out_shape=[jax.ShapeDtypeStruct((S, N_PAIRS * PAIR), F32)] * 3,
        scratch_shapes=[pltpu.VMEM((2, nk, t, tk), F32), pltpu.VMEM((2, nk, t, tk), F32),
                        pltpu.VMEM((2, t, 1), F32), pltpu.VMEM((t, PAIR), F32),
                        pltpu.VMEM((S, PAIR), F32), pltpu.VMEM((S, PAIR), F32)],
        compiler_params=_cp(("arbitrary", "arbitrary")),
    )(qkv, qkv, qkv, do)


def _conv_bwd(uc, dco, cwf, cb, lg, lb, tm):
    S = uc.shape[0]
    hb = tm // HALO
    nb = S // tm
    ext = tm + HALO

    def body(uc_ref, prev_ref, next_ref, dco_ref, dnext_ref, cw_ref, cb_ref, lg_ref, lb_ref,
             duc_ref, dcw_ref, dcb_ref, dlg_ref, dlb_ref, glu_ref, dyc_ref, shg_ref, shd_ref):
        i = pl.program_id(0)
        last = i == nb - 1

        @pl.when(i == 0)
        def _():
            for ref in (dcw_ref, dcb_ref, dlg_ref, dlb_ref):
                ref[...] = jnp.zeros_like(ref)

        uc = uc_ref[...]
        glu_ref[0:HALO, :] = jnp.where(i == 0, 0.0, _glu(prev_ref[...]))
        glu_ref[HALO:ext, :] = _glu(uc)
        glu_ref[ext:ext + HALO, :] = _glu(next_ref[...])
        glu_ref[ext + HALO:ext + HALO + SUBLANES, :] = jnp.zeros((SUBLANES, CONV_CH), F32)
        _shift_copies(glu_ref, shg_ref)
        fwd_offs = [HALO - (CONV_WIDTH - 1) + w for w in range(CONV_WIDTH)]
        y = _conv_taps(cw_ref, glu_ref, shg_ref, fwd_offs, ext) + cb_ref[...]
        mu = jnp.mean(y, axis=-1, keepdims=True)
        yc = y - mu
        rstd = lax.rsqrt(jnp.mean(yc * yc, axis=-1, keepdims=True) + EPS)
        yhat = yc * rstd
        lg = lg_ref[...]
        ln = yhat * lg + lb_ref[...]
        sg = _sigmoid(ln)
        dout = jnp.concatenate([dco_ref[...], jnp.where(last, 0.0, dnext_ref[...])], axis=0)
        dln = dout * (sg * (1.0 + ln * (1.0 - sg)))
        dyh = dln * lg
        dyc = rstd * (dyh - jnp.mean(dyh, axis=-1, keepdims=True)
                      - yhat * jnp.mean(dyh * yhat, axis=-1, keepdims=True))
        dyc_ref[0:ext, :] = dyc
        dyc_ref[ext:ext + SUBLANES, :] = jnp.zeros((SUBLANES, CONV_CH), F32)
        _shift_copies(dyc_ref, shd_ref)
        dlg_ref[...] += jnp.sum((dln * yhat)[0:tm], axis=0, keepdims=True)
        dlb_ref[...] += jnp.sum(dln[0:tm], axis=0, keepdims=True)
        dcb_ref[...] += jnp.sum(dyc[0:tm], axis=0, keepdims=True)
        dglu = _conv_taps(cw_ref, dyc_ref, shd_ref, [CONV_WIDTH - 1 - w for w in range(CONV_WIDTH)], tm)
        d0 = dyc[0:tm]
        for w, off in enumerate(fwd_offs):
            dcw_ref[w:w + 1, :] += jnp.sum(d0 * _rows_at(glu_ref, shg_ref, off, tm), axis=0, keepdims=True)
        val, gate = uc[:, :CONV_CH], uc[:, CONV_CH:]
        sgate = _sigmoid(gate)
        duc_ref[:, :CONV_CH] = (dglu * sgate).astype(BF16)
        duc_ref[:, CONV_CH:] = (dglu * val * sgate * (1.0 - sgate)).astype(BF16)

    vec = pl.BlockSpec((1, CONV_CH), lambda i: (0, 0))
    nxt = lambda i: (jnp.minimum((i + 1) * hb, S // HALO - 1), 0)
    return pl.pallas_call(
        body, name="conv_bwd", grid=(nb,),
        in_specs=[pl.BlockSpec((tm, 2 * CONV_CH), lambda i: (i, 0)),
                  pl.BlockSpec((HALO, 2 * CONV_CH), lambda i: (jnp.maximum(i * hb - 1, 0), 0)),
                  pl.BlockSpec((HALO, 2 * CONV_CH), nxt),
                  pl.BlockSpec((tm, CONV_CH), lambda i: (i, 0)),
                  pl.BlockSpec((HALO, CONV_CH), nxt),
                  _const_spec(cwf.shape), _const_spec((1, CONV_CH)), _const_spec((1, CONV_CH)),
                  _const_spec((1, CONV_CH))],
        out_specs=[pl.BlockSpec((tm, 2 * CONV_CH), lambda i: (i, 0)),
                   pl.BlockSpec(cwf.shape, lambda i: (0, 0)), vec, vec, vec],
        out_shape=[jax.ShapeDtypeStruct((S, 2 * CONV_CH), BF16), jax.ShapeDtypeStruct(cwf.shape, F32)]
        + [jax.ShapeDtypeStruct((1, CONV_CH), F32)] * 3,
        scratch_shapes=[pltpu.VMEM((ext + HALO + SUBLANES, CONV_CH), F32), pltpu.VMEM((ext + SUBLANES, CONV_CH), F32),
                        pltpu.VMEM((SUBLANES - 1, ext + HALO, CONV_CH), F32),
                        pltpu.VMEM((SUBLANES - 1, ext, CONV_CH), F32)],
        compiler_params=_cp(("arbitrary",)),
    )(uc, uc, uc, dco, dco, cwf, cb, lg, lb)


def _in_bwd(duc, dq, dk, dv, x2, dh1, g1, wa, tm):
    S = x2.shape[0]

    def body(duc_ref, dq_ref, dk_ref, dv_ref, x_ref, dh1_ref, g_ref, w_ref, gx_ref, du_ref, dg_ref):
        i = pl.program_id(0)
        du = jnp.concatenate([duc_ref[...], dq_ref[...].astype(BF16), dk_ref[...].astype(BF16),
                              dv_ref[...].astype(BF16)], axis=1)
        du_ref[...] = du
        da = _dot_nt(du, w_ref[...])
        x = x_ref[...]
        r = _rms(x)
        dx, dg = _rms_bwd(da, x * r, r, g_ref[...])
        gx_ref[...] = dh1_ref[...] + dx

        @pl.when(i == 0)
        def _():
            dg_ref[...] = jnp.zeros_like(dg_ref)

        dg_ref[...] += jnp.sum(dg, axis=0, keepdims=True)

    row = lambda w: pl.BlockSpec((tm, w), lambda i: (i, 0))
    return pl.pallas_call(
        body, name="in_bwd", grid=(S // tm,),
        in_specs=[row(2 * CONV_CH), row(CONV_CH), row(CONV_CH), row(CONV_CH), row(D_MODEL), row(D_MODEL),
                  _const_spec((1, D_MODEL)), _const_spec(wa.shape)],
        out_specs=[row(D_MODEL), row(2560), pl.BlockSpec((1, D_MODEL), lambda i: (0, 0))],
        out_shape=[jax.ShapeDtypeStruct((S, D_MODEL), F32), jax.ShapeDtypeStruct((S, 2560), BF16),
                   jax.ShapeDtypeStruct((1, D_MODEL), F32)],
        compiler_params=_cp(("arbitrary",)),
    )(duc, dq, dk, dv, x2, dh1, g1, wa)


def _matmul_tn(xm, ym, tm, tn, ts, name):
    S, M = xm.shape
    N = ym.shape[1]

    def body(x_ref, y_ref, o_ref):
        @pl.when(pl.program_id(2) == 0)
        def _():
            o_ref[...] = jnp.zeros_like(o_ref)

        o_ref[...] += _dot_tn(x_ref[...], y_ref[...])

    return pl.pallas_call(
        body, name=name, grid=(M // tm, N // tn, S // ts),
        in_specs=[pl.BlockSpec((ts, tm), lambda m, n, s: (s, m)), pl.BlockSpec((ts, tn), lambda m, n, s: (s, n))],
        out_specs=pl.BlockSpec((tm, tn), lambda m, n, s: (m, n)),
        out_shape=jax.ShapeDtypeStruct((M, N), F32),
        compiler_params=_cp(("parallel", "parallel", "arbitrary")),
    )(xm, ym)


def _sibling_halves(grads):
    n = len(grads)

    def body(*refs):
        ins, outs, ssem, rsem = refs[:n], refs[n:2 * n], refs[2 * n], refs[2 * n + 1]
        x, y, c = lax.axis_index("x"), lax.axis_index("y"), lax.axis_index("c")
        copies = []
        for k in range(n):
            for j in range(N_CHIPS):
                copies.append(pltpu.make_async_remote_copy(
                    src_ref=ins[k].at[j, 1 - c], dst_ref=outs[k].at[j],
                    send_sem=ssem.at[N_CHIPS * k + j], recv_sem=rsem.at[N_CHIPS * k + j],
                    device_id=(x, y, 1 - c), device_id_type=MESH))
        for cp in copies:
            cp.start()
        for cp in copies:
            cp.wait()

    shapes = [jax.ShapeDtypeStruct((g.shape[0],) + g.shape[2:], F32) for g in grads]
    return pl.pallas_call(
        body, name="grad_sibling_halves", out_shape=shapes,
        in_specs=[_hbm()] * n, out_specs=[_hbm()] * n,
        scratch_shapes=[pltpu.SemaphoreType.DMA((N_CHIPS * n,)), pltpu.SemaphoreType.DMA((N_CHIPS * n,))],
    )(*grads)


def _add_half(c_arr, g, landed, name):
    def body(c_ref, g_ref, l_ref, o_ref):
        o_ref[...] = (g_ref[...] + l_ref[...]).astype(BF16)

    rows, n = g.shape[2], g.shape[3]
    grid = (N_CHIPS,)
    g_spec = pl.BlockSpec((None, None, rows, n), lambda j, c: (j, c[0], 0, 0))
    l_spec = pl.BlockSpec((None, rows, n), lambda j, c: (j, 0, 0))
    return pl.pallas_call(
        body, name=name,
        grid_spec=pltpu.PrefetchScalarGridSpec(num_scalar_prefetch=1, grid=grid, in_specs=[g_spec, l_spec],
                                               out_specs=l_spec),
        out_shape=jax.ShapeDtypeStruct(landed.shape, BF16),
        compiler_params=_cp(("parallel",)),
    )(c_arr, g, landed)


def _chip_scatter(parts):
    n = len(parts)

    def piece(ref, j):
        return ref.at[j]

    def body(*refs):
        ins, outs = refs[:n], refs[n:2 * n]
        lsem, ssem, rsem = refs[2 * n:]
        x, y, c = lax.axis_index("x"), lax.axis_index("y"), lax.axis_index("c")
        me = 2 * x + y
        chips = [(1 - x, y), (x, 1 - y), (1 - x, 1 - y)]
        copies = []
        for k in range(n):
            cp = pltpu.make_async_copy(piece(ins[k], me), outs[k].at[me], lsem.at[k])
            cp.start()
            copies.append(cp)
            for r, chip in enumerate(chips):
                cp = pltpu.make_async_remote_copy(
                    src_ref=piece(ins[k], 2 * chip[0] + chip[1]), dst_ref=outs[k].at[me],
                    send_sem=ssem.at[3 * k + r], recv_sem=rsem.at[3 * k + r],
                    device_id=(chip[0], chip[1], c), device_id_type=MESH)
                cp.start()
                copies.append(cp)
        for cp in copies:
            cp.wait()

    shapes = [jax.ShapeDtypeStruct(p.shape, p.dtype) for p in parts]
    return pl.pallas_call(
        body, name="grad_chip_scatter", out_shape=shapes,
        in_specs=[_hbm()] * n, out_specs=[_hbm()] * n,
        scratch_shapes=[pltpu.SemaphoreType.DMA((n,)), pltpu.SemaphoreType.DMA((3 * n,)),
                        pltpu.SemaphoreType.DMA((3 * n,))],
    )(*parts)


def _sum_chips(landed, name):
    _, rows, n = landed.shape
    tr = 256 if rows % 256 == 0 else rows

    def body(a_ref, b_ref, c_ref, d_ref, o_ref):
        f = lambda ref: ref[...].astype(F32)
        o_ref[...] = ((f(a_ref) + f(b_ref)) + f(c_ref)) + f(d_ref)

    specs = [pl.BlockSpec((None, tr, n), functools.partial(lambda i, j: (j, i, 0), j=j)) for j in range(N_CHIPS)]
    return pl.pallas_call(
        body, name=name, grid=(rows // tr,), in_specs=specs,
        out_specs=pl.BlockSpec((tr, n), lambda i: (i, 0)),
        out_shape=jax.ShapeDtypeStruct((rows, n), F32),
        compiler_params=_cp(("parallel",)),
    )(landed, landed, landed, landed)


def _share_halves(halves):
    n = len(halves)

    def body(*refs):
        ins, outs = refs[:n], refs[n:2 * n]
        ssem, rsem = refs[2 * n:]
        x, y, c = lax.axis_index("x"), lax.axis_index("y"), lax.axis_index("c")
        copies = [pltpu.make_async_remote_copy(
            src_ref=ins[k], dst_ref=outs[k], send_sem=ssem.at[k], recv_sem=rsem.at[k],
            device_id=(x, y, 1 - c), device_id_type=MESH) for k in range(n)]
        for cp in copies:
            cp.start()
        for cp in copies:
            cp.wait()

    shapes = [jax.ShapeDtypeStruct(h.shape, F32) for h in halves]
    return pl.pallas_call(
        body, name="grad_share_halves", out_shape=shapes,
        in_specs=[_hbm()] * n, out_specs=[_hbm()] * n,
        scratch_shapes=[pltpu.SemaphoreType.DMA((n,)), pltpu.SemaphoreType.DMA((n,))],
    )(*halves)


def _allreduce_small(packed):
    rows, n = packed.shape

    def body(in_ref, out_ref, land_ref, ssem, rsem):
        x, y, c = lax.axis_index("x"), lax.axis_index("y"), lax.axis_index("c")
        me = 4 * x + 2 * y + c
        land_ref[me] = in_ref[...]
        copies = []
        for r in range(1, 8):
            tx = 1 - x if r & 4 else x
            ty = 1 - y if r & 2 else y
            tc = 1 - c if r & 1 else c
            cp = pltpu.make_async_remote_copy(
                src_ref=in_ref, dst_ref=land_ref.at[me], send_sem=ssem.at[r - 1], recv_sem=rsem.at[r - 1],
                device_id=(tx, ty, tc), device_id_type=MESH)
            cp.start()
            copies.append(cp)
        for cp in copies:
            cp.wait()
        acc = land_ref[0]
        for k in range(1, 8):
            acc = acc + land_ref[k]
        out_ref[...] = acc

    return pl.pallas_call(
        body, name="allreduce_small", out_shape=jax.ShapeDtypeStruct((rows, n), F32),
        in_specs=[pl.BlockSpec(memory_space=pltpu.VMEM)], out_specs=pl.BlockSpec(memory_space=pltpu.VMEM),
        scratch_shapes=[pltpu.VMEM((8, rows, n), F32), pltpu.SemaphoreType.DMA((7,)),
                        pltpu.SemaphoreType.DMA((7,))],
    )(packed)


def _adamw_math(w, g, m, v):
    m = ADAM_B1 * m + (1.0 - ADAM_B1) * g
    v = ADAM_B2 * v + (1.0 - ADAM_B2) * (g * g)
    m_hat = m / (1.0 - ADAM_B1 ** ADAM_STEP)
    v_hat = v / (1.0 - ADAM_B2 ** ADAM_STEP)
    return -ADAM_LR * (m_hat / (jnp.sqrt(v_hat) + ADAM_EPS) + ADAM_WD * w), m, v


def _adamw_halves(c_arr, w, mine, other, m, v, name):
    rows, n = mine.shape
    tr = 256 if rows % 256 == 0 else rows
    nb = rows // tr

    def body(c_ref, w_ref, a_ref, b_ref, m_ref, v_ref, g_ref, d_ref, mo_ref, vo_ref):
        g = jnp.where(pl.program_id(0) == c_ref[0], a_ref[...], b_ref[...])
        g_ref[...] = g
        d_ref[...], mo_ref[...], vo_ref[...] = _adamw_math(w_ref[...], g, m_ref[...], v_ref[...])

    full = pl.BlockSpec((tr, n), lambda h, i, c: (h * nb + i, 0))
    half = pl.BlockSpec((tr, n), lambda h, i, c: (i, 0))
    return pl.pallas_call(
        body, name=name,
        grid_spec=pltpu.PrefetchScalarGridSpec(num_scalar_prefetch=1, grid=(2, nb),
                                               in_specs=[full, half, half, full, full], out_specs=[full] * 4),
        out_shape=[jax.ShapeDtypeStruct((2 * rows, n), F32)] * 4,
        compiler_params=_cp(("parallel", "parallel")),
    )(c_arr, w, mine, other, m, v)


def _adamw(w, g, m, v, name):
    rows, n = w.shape
    tr = 256 if rows % 256 == 0 else rows

    def body(w_ref, g_ref, m_ref, v_ref, d_ref, mo_ref, vo_ref):
        d_ref[...], mo_ref[...], vo_ref[...] = _adamw_math(w_ref[...], g_ref[...], m_ref[...], v_ref[...])

    spec = pl.BlockSpec((tr, n), lambda i: (i, 0))
    return pl.pallas_call(
        body, name=name, grid=(rows // tr,), in_specs=[spec] * 4, out_specs=[spec] * 3,
        out_shape=[jax.ShapeDtypeStruct((rows, n), F32)] * 3,
        compiler_params=_cp(("parallel",)),
    )(w, g, m, v)


def _rows8(a):
    a = a.reshape(-1, 128)
    return jnp.pad(a, ((0, (-a.shape[0]) % 8), (0, 0)))


def kernel(x, g_pre_mix, w_in, conv_w, conv_b, conv_ln_g, conv_ln_b, attn_norm_g, w_out, g_post_mix, g_pre_ffn, w_gate, w_up, w_down, g_post_ffn, loss_target, m_g_pre_mix, m_w_in, m_conv_w, m_conv_b, m_conv_ln_g, m_conv_ln_b, m_attn_norm_g, m_w_out, m_g_post_mix, m_g_pre_ffn, m_w_gate, m_w_up, m_w_down, m_g_post_ffn, v_g_pre_mix, v_w_in, v_conv_w, v_conv_b, v_conv_ln_g, v_conv_ln_b, v_attn_norm_g, v_w_out, v_g_post_mix, v_g_pre_ffn, v_w_gate, v_w_up, v_w_down, v_g_post_ffn):
    S = x.shape[1]
    tm_big = min(512, S)
    tm_ffn = min(256, S)
    t_att = min(256, S // KEY_BLOCKS)
    chip = 2 * lax.axis_index("x") + lax.axis_index("y")
    core = lax.axis_index("c")
    x2 = x.reshape(S, D_MODEL)
    tgt = loss_target.reshape(S, D_MODEL)
    ag = attn_norm_g.reshape(1, CONV_CH)

    a_sh = w_in[0].astype(BF16)
    b_sh = jnp.stack([w_gate[0], w_up[0]]).astype(BF16)
    c_sh = jnp.concatenate([w_out[0], w_down[0]], axis=0).astype(BF16)
    cw_sh = jnp.pad(conv_w[0, :, 0, :], ((0, 1), (0, 0)))
    wa4, wb4, wc4, cw4 = [lax.dynamic_update_index_in_dim(full, own, chip, 0) for full, own in
                          zip(_gather_weights(a_sh, b_sh, c_sh, cw_sh), (a_sh, b_sh, c_sh, cw_sh))]
    cols = lambda w4: jnp.transpose(w4, (1, 0, 2)).reshape(w4.shape[1], N_CHIPS * w4.shape[2])
    wa = cols(wa4)
    wg, wu = cols(wb4[:, 0]), cols(wb4[:, 1])
    wo = wc4[:, :OUT_SH].reshape(D_MODEL, D_MODEL)
    wd = wc4[:, OUT_SH:].reshape(D_FF, D_MODEL)
    cwf = cols(cw4)

    a_bf, uc, qkv = _in_proj(x2, g_pre_mix, wa, tm_big)
    conv_out = _conv_fwd(uc, cwf, conv_b, conv_ln_g, conv_ln_b, tm_big)
    o = _attn_fwd(qkv, t_att)
    mixed, yv, h1, f_in = _out_proj(conv_out, o, ag, wo, x2, g_post_mix, g_pre_ffn, tm_big)
    df, dh2, dg4, loss_part = _ffn_fwd(f_in, h1, tgt, wg, wu, wd, g_post_ffn, tm_ffn)

    act, dgt, dup, dh1, dy, dg3, dg2 = _ffn_bwd(f_in, df, dh2, h1, yv, wg, wu, wd, g_pre_ffn, g_post_mix, tm_ffn)
    dco, do, dag = _out_bwd(dy, o, ag, wo, tm_big)
    dq, dk, dv = _attn_bwd(qkv, do, t_att)
    duc, dcw, dcb, dlg, dlb = _conv_bwd(uc, dco, cwf, conv_b, conv_ln_g, conv_ln_b, tm_big)
    grad_x, du, dg1 = _in_bwd(duc, dq, dk, dv, x2, dh1, g_pre_mix, wa, tm_big)
    ts = min(512, S)
    gw_in = _matmul_tn(a_bf, du, D_MODEL, 1280, ts, "grad_w_in")
    gw_out = _matmul_tn(mixed, dy, D_MODEL, D_MODEL, ts, "grad_w_out")
    gw_gate = _matmul_tn(f_in, dgt, D_MODEL, D_FF // 2, ts, "grad_w_gate")
    gw_up = _matmul_tn(f_in, dup, D_MODEL, D_FF // 2, ts, "grad_w_up")
    gw_down = _matmul_tn(act, df, D_FF // 2, D_MODEL, ts, "grad_w_down")

    by_cols = lambda g: jnp.transpose(g.reshape(2, D_MODEL // 2, N_CHIPS, -1), (2, 0, 1, 3))
    by_rows = lambda g: g.reshape(N_CHIPS, 2, g.shape[0] // (2 * N_CHIPS), g.shape[1])
    views = [by_cols(gw_in), by_cols(gw_gate), by_cols(gw_up), by_rows(gw_out), by_rows(gw_down)]
    names = ["w_in", "w_gate", "w_up", "w_out", "w_down"]
    landed = _sibling_halves(views)
    c_arr = core.reshape(1).astype(jnp.int32)
    parts = [_add_half(c_arr, g, l, "grad_half_" + nm) for g, l, nm in zip(views, landed, names)]
    slots = _chip_scatter(parts)
    halves = [_sum_chips(s, "grad_sum_" + nm) for s, nm in zip(slots, names)]
    others = _share_halves(halves)
    mine = dict(zip(names, halves))
    other = dict(zip(names, others))

    small = [dg1, dcb, dlg, dlb, dag, dg2, dg3, dg4]
    packed = jnp.concatenate([_rows8(s) for s in small] + [_rows8(dcw), _rows8(loss_part)], axis=0)
    red = _allreduce_small(packed)
    sizes = [D_MODEL, CONV_CH, CONV_CH, CONV_CH, CONV_CH, D_MODEL, D_MODEL, D_MODEL]
    g_small = [red[8 * k:8 * k + n // 128].reshape(1, n) for k, n in enumerate(sizes)]
    cw_red = red[64:64 + 128].reshape(HALO, CONV_CH)
    g_cw = lax.dynamic_slice(cw_red, (0, chip * 128), (HALO, 128))
    loss = red[192, 0]

    big = []
    for w, m, v, nm in [(w_in, m_w_in, v_w_in, "w_in"), (w_out, m_w_out, v_w_out, "w_out"),
                        (w_gate, m_w_gate, v_w_gate, "w_gate"), (w_up, m_w_up, v_w_up, "w_up"),
                        (w_down, m_w_down, v_w_down, "w_down")]:
        big.append(_adamw_halves(c_arr, w[0], mine[nm], other[nm], m[0], v[0], "adamw_" + nm))
    sm_w = [g_pre_mix, conv_b, conv_ln_g, conv_ln_b, ag, g_post_mix, g_pre_ffn, g_post_ffn]
    sm_m = [m_g_pre_mix, m_conv_b, m_conv_ln_g, m_conv_ln_b, m_attn_norm_g, m_g_post_mix, m_g_pre_ffn, m_g_post_ffn]
    sm_v = [v_g_pre_mix, v_conv_b, v_conv_ln_g, v_conv_ln_b, v_attn_norm_g, v_g_post_mix, v_g_pre_ffn, v_g_post_ffn]
    pad_cw = lambda a: jnp.pad(a[0, :, 0, :], ((0, 1), (0, 0)))

    def pack(vecs, cw):
        return jnp.concatenate([_rows8(a) for a in vecs] + [cw], axis=0)

    sd, smn, svn = _adamw(pack(sm_w, pad_cw(conv_w)), pack(g_small, g_cw), pack(sm_m, pad_cw(m_conv_w)),
                          pack(sm_v, pad_cw(v_conv_w)), "adamw_small")

    def unpack(p):
        vecs = [p[8 * k:8 * k + n // 128].reshape(1, n) for k, n in enumerate(sizes)]
        return vecs, p[64:64 + CONV_WIDTH].reshape(1, CONV_WIDTH, 1, 128)

    def ordered(vecs, cw, w_in_, w_out_, w_gate_, w_up_, w_down_):
        g1_, cb_, lg_, lb_, ag_, g2_, g3_, g4_ = vecs
        return [g1_, w_in_[None], cw, cb_, lg_, lb_, ag_.reshape(1, 8, HEAD_DIM), w_out_[None], g2_, g3_,
                w_gate_[None], w_up_[None], w_down_[None], g4_]

    grads = ordered(g_small, g_cw[:CONV_WIDTH].reshape(1, CONV_WIDTH, 1, 128), *[b[0] for b in big])
    outs = []
    for idx, p in enumerate((sd, smn, svn)):
        vecs, cw = unpack(p)
        outs += ordered(vecs, cw, *[b[idx + 1] for b in big])
    return (loss, grad_x.reshape(1, S, D_MODEL), *grads, *outs)
```

```python
import functools
import math

import jax
import jax.numpy as jnp
from jax import lax
from jax.experimental import pallas as pl
from jax.experimental.pallas import tpu as pltpu

F32 = jnp.float32
BF16 = jnp.bfloat16
MESH = pl.DeviceIdType.MESH

D_MODEL = 1024
CONV_CH = 512
CONV_WIDTH = 31
HEAD_DIM = 64
PAIR = 2 * HEAD_DIM
N_PAIRS = 4
D_FF = 2816
N_CHIPS = 4
IN_SH = 2560 // N_CHIPS
FF_SH = D_FF // N_CHIPS
OUT_SH = D_MODEL // N_CHIPS
C_ROWS = OUT_SH + FF_SH
EPS = 1e-6
HALO = 32

ADAM_LR = 0.001
ADAM_B1 = 0.9
ADAM_B2 = 0.999
ADAM_EPS = 1e-08
ADAM_WD = 0.01
ADAM_STEP = 10

VMEM_LIMIT = 56 * 2 ** 20
VMEM_LIMIT_ATTN_BWD = 60 * 2 ** 20


def _cp(sem=None, vmem=VMEM_LIMIT):
    return pltpu.CompilerParams(dimension_semantics=sem, vmem_limit_bytes=vmem)


def _hbm():
    return pl.BlockSpec(memory_space=pltpu.HBM)


def _const_spec(shape):
    nd = len(shape)
    return pl.BlockSpec(shape, lambda *_: (0,) * nd, pipeline_mode=pl.Buffered(1))


def _dot(a, b):
    return jnp.dot(a, b, preferred_element_type=F32)


def _dot_nt(a, b):
    return lax.dot_general(a, b, (((1,), (1,)), ((), ())), preferred_element_type=F32)


def _dot_tn(a, b):
    return lax.dot_general(a, b, (((0,), (0,)), ((), ())), preferred_element_type=F32)


def _split3(x):
    b0 = x.astype(BF16)
    r1 = x - b0.astype(F32)
    b1 = r1.astype(BF16)
    b2 = (r1 - b1.astype(F32)).astype(BF16)
    return b0, b1, b2


def _split2(x):
    hi = x.astype(BF16)
    lo = (x - hi.astype(F32)).astype(BF16)
    return hi, lo


def _sigmoid(x):
    return 1.0 / (1.0 + jnp.exp(-x))


def _head_mean(x, seg):
    b0, b1, b2 = _split3(x)
    return (_dot(b0, seg) + _dot(b1, seg) + _dot(b2, seg)) * (1.0 / HEAD_DIM)


def _seg_matrix(n):
    r = lax.broadcasted_iota(jnp.int32, (n, n), 0) // HEAD_DIM
    c = lax.broadcasted_iota(jnp.int32, (n, n), 1) // HEAD_DIM
    return (r == c).astype(BF16)


def _rms(x):
    return lax.rsqrt(jnp.mean(x * x, axis=-1, keepdims=True) + EPS)


def _rms_bwd(dy, n, r, g):
    dn = dy * g
    dx = r * (dn - n * jnp.mean(dn * n, axis=-1, keepdims=True))
    return dx, dy * n


def _gather_weights(a_sh, b_sh, c_sh, w_sh):
    ha, hc, hw = a_sh.shape[0] // 2, c_sh.shape[0] // 2, w_sh.shape[0] // 2

    def body(a_ref, b_ref, c_ref, w_ref, ao, bo, co, wo, ssem, rsem):
        x, y, c = lax.axis_index("x"), lax.axis_index("y"), lax.axis_index("c")
        me = 2 * x + y
        sibling = (x, y, 1 - c)
        chips = [(1 - x, y), (x, 1 - y), (1 - x, 1 - y)]

        def src_half(i, h):
            return [a_ref.at[pl.ds(h * ha, ha), :], b_ref.at[h], c_ref.at[pl.ds(h * hc, hc), :],
                    w_ref.at[pl.ds(h * hw, hw), :]][i]

        def out_half(i, j, h):
            return [ao.at[j, pl.ds(h * ha, ha), :], bo.at[j, h], co.at[j, pl.ds(h * hc, hc), :],
                    wo.at[j, pl.ds(h * hw, hw), :]][i]

        def ici(i, k, chip, origin):
            return pltpu.make_async_remote_copy(
                src_ref=src_half(i, c), dst_ref=out_half(i, origin, c),
                send_sem=ssem.at[6 * i + k], recv_sem=rsem.at[6 * i + k],
                device_id=(chip[0], chip[1], c), device_id_type=MESH)

        def d2d(i, k, origin, h):
            return pltpu.make_async_remote_copy(
                src_ref=out_half(i, origin, h), dst_ref=out_half(i, origin, h),
                send_sem=ssem.at[6 * i + 3 + k], recv_sem=rsem.at[6 * i + 3 + k],
                device_id=sibling, device_id_type=MESH)

        sends = []
        for i in range(4):
            for k, chip in enumerate(chips):
                cp = ici(i, k, chip, me)
                cp.start()
                sends.append(cp)
        for i in range(4):
            for k, chip in enumerate(chips):
                origin = 2 * chip[0] + chip[1]
                ici(i, k, chip, origin).wait_recv()
                cp = d2d(i, k, origin, c)
                cp.start()
                sends.append(cp)
        for i in range(4):
            for k, chip in enumerate(chips):
                origin = 2 * chip[0] + chip[1]
                d2d(i, k, origin, 1 - c).wait_recv()
        for cp in sends:
            cp.wait_send()

    shapes = [jax.ShapeDtypeStruct((N_CHIPS,) + s.shape, s.dtype) for s in (a_sh, b_sh, c_sh, w_sh)]
    return pl.pallas_call(
        body, name="gather_weights", out_shape=shapes,
        in_specs=[_hbm()] * 4, out_specs=[_hbm()] * 4,
        scratch_shapes=[pltpu.SemaphoreType.DMA((24,)), pltpu.SemaphoreType.DMA((24,))],
    )(a_sh, b_sh, c_sh, w_sh)


def _in_proj(x2, g1, wa, tm):
    S = x2.shape[0]

    def body(x_ref, g_ref, w_ref, a_ref, uc_ref, qkv_ref):
        x = x_ref[...]
        a = (x * _rms(x) * g_ref[...]).astype(BF16)
        a_ref[...] = a
        uc_ref[...] = _dot(a, w_ref[:, 0:2 * CONV_CH])
        qkv_ref[...] = _dot(a, w_ref[:, 2 * CONV_CH:]).astype(BF16)

    return pl.pallas_call(
        body, name="in_proj", grid=(S // tm,),
        in_specs=[pl.BlockSpec((tm, D_MODEL), lambda i: (i, 0)), _const_spec((1, D_MODEL)),
                  _const_spec(wa.shape)],
        out_specs=[pl.BlockSpec((tm, D_MODEL), lambda i: (i, 0)),
                   pl.BlockSpec((tm, 2 * CONV_CH), lambda i: (i, 0)),
                   pl.BlockSpec((tm, 1536), lambda i: (i, 0))],
        out_shape=[jax.ShapeDtypeStruct((S, D_MODEL), BF16), jax.ShapeDtypeStruct((S, 2 * CONV_CH), F32),
                   jax.ShapeDtypeStruct((S, 1536), BF16)],
        compiler_params=_cp(("parallel",)),
    )(x2, g1, wa)


SUBLANES = 8


def _shift_copies(src_ref, sh_ref):
    rows = sh_ref.shape[1]
    for b in range(1, SUBLANES):
        sh_ref[b - 1] = src_ref[pl.ds(b, rows), :]


def _rows_at(src_ref, sh_ref, off, rows):
    a, b = divmod(off, SUBLANES)
    if b == 0:
        return src_ref[pl.ds(SUBLANES * a, rows), :]
    return sh_ref[b - 1, pl.ds(SUBLANES * a, rows), :]


def _conv_taps(cw_ref, src_ref, sh_ref, offs, rows):
    acc = None
    for w, off in enumerate(offs):
        term = cw_ref[w:w + 1, :] * _rows_at(src_ref, sh_ref, off, rows)
        acc = term if acc is None else acc + term
    return acc


def _glu(uc):
    return uc[:, :CONV_CH] * _sigmoid(uc[:, CONV_CH:])


def _conv_fwd(uc, cwf, cb, lg, lb, tm):
    S = uc.shape[0]
    hb = tm // HALO

    def body(uc_ref, prev_ref, cw_ref, cb_ref, lg_ref, lb_ref, out_ref, glu_ref, sh_ref):
        i = pl.program_id(0)
        glu_ref[0:HALO, :] = jnp.where(i == 0, 0.0, _glu(prev_ref[...]))
        glu_ref[HALO:HALO + tm, :] = _glu(uc_ref[...])
        glu_ref[HALO + tm:HALO + tm + SUBLANES, :] = jnp.zeros((SUBLANES, CONV_CH), F32)
        _shift_copies(glu_ref, sh_ref)
        offs = [HALO - (CONV_WIDTH - 1) + w for w in range(CONV_WIDTH)]
        y = _conv_taps(cw_ref, glu_ref, sh_ref, offs, tm) + cb_ref[...]
        mu = jnp.mean(y, axis=-1, keepdims=True)
        yc = y - mu
        rstd = lax.rsqrt(jnp.mean(yc * yc, axis=-1, keepdims=True) + EPS)
        ln = yc * rstd * lg_ref[...] + lb_ref[...]
        out_ref[...] = (ln * _sigmoid(ln)).astype(BF16)

    return pl.pallas_call(
        body, name="conv_fwd", grid=(S // tm,),
        in_specs=[pl.BlockSpec((tm, 2 * CONV_CH), lambda i: (i, 0)),
                  pl.BlockSpec((HALO, 2 * CONV_CH), lambda i: (jnp.maximum(i * hb - 1, 0), 0)),
                  _const_spec(cwf.shape), _const_spec((1, CONV_CH)), _const_spec((1, CONV_CH)),
                  _const_spec((1, CONV_CH))],
        out_specs=pl.BlockSpec((tm, CONV_CH), lambda i: (i, 0)),
        out_shape=jax.ShapeDtypeStruct((S, CONV_CH), BF16),
        scratch_shapes=[pltpu.VMEM((HALO + tm + SUBLANES, CONV_CH), F32),
                        pltpu.VMEM((SUBLANES - 1, HALO + tm, CONV_CH), F32)],
        compiler_params=_cp(("parallel",)),
    )(uc, uc, cwf, cb, lg, lb)


def _lane_mask(h):
    lane = lax.broadcasted_iota(jnp.int32, (1, PAIR), 1)
    return (lane >= HEAD_DIM * h) & (lane < HEAD_DIM * (h + 1))


def _neg_abs(x):
    bits = lax.bitcast_convert_type(x, jnp.uint32) | jnp.uint32(0x80000000)
    return lax.bitcast_convert_type(bits, F32)


def _tri_dot(x, m):
    return _dot(x.astype(BF16), m)


MASKED = -1e30
KEY_BLOCKS = 4


def _running_sums(x, m, reverse):
    t = m.shape[0]
    order = range(KEY_BLOCKS - 1, -1, -1) if reverse else range(KEY_BLOCKS)
    out = [None] * KEY_BLOCKS
    carry = None
    for b in order:
        xb = x[:, b * t:(b + 1) * t]
        cb = _tri_dot(xb, m)
        out[b] = cb if carry is None else cb + carry
        rs = jnp.sum(xb, axis=1, keepdims=True)
        carry = rs if carry is None else carry + rs
    return jnp.concatenate(out, axis=1), carry


def _sb_tile(z, r, m_suf):
    sp = jnp.maximum(z, 0.0) + jnp.log(1.0 + jnp.exp(_neg_abs(z)))
    c, rs = _running_sums(sp, m_suf, reverse=True)
    ex = z - c
    if r is not None:
        ex = ex - r
    return jnp.exp(ex), sp, rs


def _scores(qm, kt, mask):
    z = _dot_nt(qm, kt)
    return z if mask is None else jnp.where(mask, z, MASKED)


def _causal_mask(i, sb, t):
    row = lax.broadcasted_iota(jnp.int32, (t, KEY_BLOCKS * t), 0) + i * t
    col = lax.broadcasted_iota(jnp.int32, (t, KEY_BLOCKS * t), 1) + sb * (KEY_BLOCKS * t)
    return col < row


def _sweep_plain(first, count, tile):
    def step(n, carry):
        tile(first + n)
        return carry

    lax.fori_loop(0, count + 1, step, 0)


def _sweep(first, count, down, fetch, load, work):
    lo, hi = (first - count, first) if down else (first, first + count)
    tile = lambda j: jnp.clip(first - j if down else first + j, lo, hi)
    fetch(first, 0, True)

    def step(n, carry):
        j = 2 * n
        vals = load(0)
        fetch(tile(j + 1), 1, False)
        work(tile(j), vals)
        vals = load(1)
        fetch(tile(j + 2), 0, False)
        work(tile(j + 1), vals)
        return carry

    lax.fori_loop(0, (count + 1) // 2, step, 0)

    @pl.when(lax.rem(count, 2) == 0)
    def _():
        work(tile(count), load(0))


def _suffix_matrix(t, prefix=False):
    row = lax.broadcasted_iota(jnp.int32, (t, t), 0)
    col = lax.broadcasted_iota(jnp.int32, (t, t), 1)
    return ((row <= col) if prefix else (row >= col)).astype(BF16)


def _attn_fwd(qkv, t):
    S = qkv.shape[0]
    tk = KEY_BLOCKS * t

    def body(q_ref, k_ref, v_ref, o_ref, acc_ref, r_ref, z_buf):
        i = pl.program_id(1)
        last = i // KEY_BLOCKS
        m_suf = _suffix_matrix(t)
        q = q_ref[...]
        hms = [_lane_mask(h) for h in range(2)]
        qms = [jnp.where(hm, q, 0) * 0.125 for hm in hms]
        acc_ref[...] = jnp.zeros_like(acc_ref)
        r_ref[...] = jnp.zeros_like(r_ref)

        def rows(sb):
            return pl.ds(pl.multiple_of(sb * tk, tk), tk)

        def fetch(sb, slot, diagonal):
            kt = k_ref[rows(sb), :]
            mask = _causal_mask(i, sb, t) if diagonal else None
            for h in range(2):
                z_buf[slot, h] = _scores(qms[h], kt, mask)

        def load(slot):
            return [z_buf[slot, h] for h in range(2)]

        def work(sb, zs):
            vt = v_ref[rows(sb), :]
            for h in range(2):
                a_loc, _, rs = _sb_tile(zs[h], None, m_suf)
                r = r_ref[h]
                acc_ref[...] += _dot(a_loc.astype(BF16), jnp.where(hms[h], vt, 0)) * jnp.exp(-r)
                r_ref[h] = r + rs

        _sweep(last, last, True, fetch, load, work)
        o_ref[...] = acc_ref[...]

    return pl.pallas_call(
        body, name="attn_fwd", grid=(N_PAIRS, S // t),
        in_specs=[pl.BlockSpec((t, PAIR), lambda p, i: (i, p)),
                  pl.BlockSpec((S, PAIR), lambda p, i: (0, N_PAIRS + p)),
                  pl.BlockSpec((S, PAIR), lambda p, i: (0, 2 * N_PAIRS + p))],
        out_specs=pl.BlockSpec((t, PAIR), lambda p, i: (i, p)),
        out_shape=jax.ShapeDtypeStruct((S, N_PAIRS * PAIR), F32),
        scratch_shapes=[pltpu.VMEM((t, PAIR), F32), pltpu.VMEM((2, t, 1), F32),
                        pltpu.VMEM((2, 2, t, tk), F32)],
        compiler_params=_cp(("parallel", "arbitrary")),
    )(qkv, qkv, qkv)


def _out_proj(conv_out, o, ag, wc, x2, g2, g3, tm):
    S = o.shape[0]

    def body(co_ref, o_ref, ag_ref, w_ref, x_ref, g2_ref, g3_ref, mix_ref, y_ref, h1_ref, fin_ref):
        seg = _seg_matrix(CONV_CH)
        o = o_ref[...]
        att = (o * lax.rsqrt(_head_mean(o * o, seg) + EPS) * ag_ref[...]).astype(BF16)
        co = co_ref[...]
        mix_ref[:, :CONV_CH] = co
        mix_ref[:, CONV_CH:] = att
        y = _dot(co, w_ref[0:CONV_CH, :]) + _dot(att, w_ref[CONV_CH:, :])
        y_ref[...] = y
        h1 = x_ref[...] + y * _rms(y) * g2_ref[...]
        h1_ref[...] = h1
        fin_ref[...] = (h1 * _rms(h1) * g3_ref[...]).astype(BF16)

    row = lambda w: pl.BlockSpec((tm, w), lambda i: (i, 0))
    return pl.pallas_call(
        body, name="out_proj", grid=(S // tm,),
        in_specs=[row(CONV_CH), row(CONV_CH), _const_spec((1, CONV_CH)), _const_spec(wc.shape),
                  row(D_MODEL), _const_spec((1, D_MODEL)), _const_spec((1, D_MODEL))],
        out_specs=[row(D_MODEL)] * 4,
        out_shape=[jax.ShapeDtypeStruct((S, D_MODEL), BF16), jax.ShapeDtypeStruct((S, D_MODEL), F32),
                   jax.ShapeDtypeStruct((S, D_MODEL), F32), jax.ShapeDtypeStruct((S, D_MODEL), BF16)],
        compiler_params=_cp(("parallel",)),
    )(conv_out, o, ag, wc, x2, g2, g3)


def _ffn_fwd(f_in, h1, tgt, wg, wu, wd, g4, tm):
    S = f_in.shape[0]

    def body(fin_ref, h1_ref, tgt_ref, wg_ref, wu_ref, wd_ref, g4_ref, df_ref, dh2_ref, dg4_ref, loss_ref):
        i = pl.program_id(0)
        fin = fin_ref[...]
        gt = _dot(fin, wg_ref[...])
        up = _dot(fin, wu_ref[...])
        f = _dot((gt * _sigmoid(gt) * up).astype(BF16), wd_ref[...])
        r = _rms(f)
        n = f * r
        g4 = g4_ref[...]
        err = h1_ref[...] + n * g4 - tgt_ref[...]
        dh2 = err * (1.0 / D_MODEL)
        dh2_ref[...] = dh2
        df, dg = _rms_bwd(dh2, n, r, g4)
        df_ref[...] = df.astype(BF16)

        @pl.when(i == 0)
        def _():
            dg4_ref[...] = jnp.zeros_like(dg4_ref)
            loss_ref[...] = jnp.zeros_like(loss_ref)

        dg4_ref[...] += jnp.sum(dg, axis=0, keepdims=True)
        part = jnp.sum(jnp.sum(err * err, axis=1, keepdims=True), axis=0, keepdims=True)
        loss_ref[...] += part * (0.5 / D_MODEL)

    row = lambda w: pl.BlockSpec((tm, w), lambda i: (i, 0))
    return pl.pallas_call(
        body, name="ffn_fwd", grid=(S // tm,),
        in_specs=[row(D_MODEL), row(D_MODEL), row(D_MODEL), _const_spec(wg.shape), _const_spec(wu.shape),
                  _const_spec(wd.shape), _const_spec((1, D_MODEL))],
        out_specs=[row(D_MODEL), row(D_MODEL), pl.BlockSpec((1, D_MODEL), lambda i: (0, 0)),
                   pl.BlockSpec((1, 128), lambda i: (0, 0))],
        out_shape=[jax.ShapeDtypeStruct((S, D_MODEL), BF16), jax.ShapeDtypeStruct((S, D_MODEL), F32),
                   jax.ShapeDtypeStruct((1, D_MODEL), F32), jax.ShapeDtypeStruct((1, 128), F32)],
        compiler_params=_cp(("arbitrary",)),
    )(f_in, h1, tgt, wg, wu, wd, g4)


def _ffn_bwd(f_in, df, dh2, h1, yv, wg, wu, wd, g3, g2, tm):
    S = f_in.shape[0]

    def body(fin_ref, df_ref, dh2_ref, h1_ref, y_ref, wg_ref, wu_ref, wd_ref, g3_ref, g2_ref,
             act_ref, dgt_ref, dup_ref, dh1_ref, dy_ref, dg3_ref, dg2_ref):
        i = pl.program_id(0)
        fin = fin_ref[...]
        df = df_ref[...]
        gt = _dot(fin, wg_ref[...])
        up = _dot(fin, wu_ref[...])
        sg = _sigmoid(gt)
        silu = gt * sg
        act_ref[...] = (silu * up).astype(BF16)
        dact = _dot_nt(df, wd_ref[...])
        dgt = (dact * up * (sg * (1.0 + gt * (1.0 - sg)))).astype(BF16)
        dup = (dact * silu).astype(BF16)
        dgt_ref[...] = dgt
        dup_ref[...] = dup
        dfin = _dot_nt(dgt, wg_ref[...]) + _dot_nt(dup, wu_ref[...])
        h1 = h1_ref[...]
        r3 = _rms(h1)
        dh1_n, dg3 = _rms_bwd(dfin, h1 * r3, r3, g3_ref[...])
        dh1 = dh2_ref[...] + dh1_n
        dh1_ref[...] = dh1
        y = y_ref[...]
        r2 = _rms(y)
        dy, dg2 = _rms_bwd(dh1, y * r2, r2, g2_ref[...])
        dy_ref[...] = dy.astype(BF16)

        @pl.when(i == 0)
        def _():
            dg3_ref[...] = jnp.zeros_like(dg3_ref)
            dg2_ref[...] = jnp.zeros_like(dg2_ref)

        dg3_ref[...] += jnp.sum(dg3, axis=0, keepdims=True)
        dg2_ref[...] += jnp.sum(dg2, axis=0, keepdims=True)

    row = lambda w: pl.BlockSpec((tm, w), lambda i: (i, 0))
    vec = pl.BlockSpec((1, D_MODEL), lambda i: (0, 0))
    return pl.pallas_call(
        body, name="ffn_bwd", grid=(S // tm,),
        in_specs=[row(D_MODEL)] * 5 + [_const_spec(wg.shape), _const_spec(wu.shape), _const_spec(wd.shape),
                                       _const_spec((1, D_MODEL)), _const_spec((1, D_MODEL))],
        out_specs=[row(D_FF), row(D_FF), row(D_FF), row(D_MODEL), row(D_MODEL), vec, vec],
        out_shape=[jax.ShapeDtypeStruct((S, D_FF), BF16)] * 3
        + [jax.ShapeDtypeStruct((S, D_MODEL), F32), jax.ShapeDtypeStruct((S, D_MODEL), BF16),
           jax.ShapeDtypeStruct((1, D_MODEL), F32), jax.ShapeDtypeStruct((1, D_MODEL), F32)],
        compiler_params=_cp(("arbitrary",)),
    )(f_in, df, dh2, h1, yv, wg, wu, wd, g3, g2)


def _out_bwd(dy, o, ag, wc, tm):
    S = o.shape[0]

    def body(dy_ref, o_ref, ag_ref, w_ref, dco_ref, do_ref, dag_ref):
        i = pl.program_id(0)
        seg = _seg_matrix(CONV_CH)
        dy = dy_ref[...]
        dco_ref[...] = _dot_nt(dy, w_ref[0:CONV_CH, :])
        datt = _dot_nt(dy, w_ref[CONV_CH:, :])
        o = o_ref[...]
        r = lax.rsqrt(_head_mean(o * o, seg) + EPS)
        n = o * r
        dn = datt * ag_ref[...]
        do_ref[...] = (r * (dn - n * _head_mean(dn * n, seg))).astype(BF16)

        @pl.when(i == 0)
        def _():
            dag_ref[...] = jnp.zeros_like(dag_ref)

        dag_ref[...] += jnp.sum(datt * n, axis=0, keepdims=True)

    row = lambda w: pl.BlockSpec((tm, w), lambda i: (i, 0))
    return pl.pallas_call(
        body, name="out_bwd", grid=(S // tm,),
        in_specs=[row(D_MODEL), row(CONV_CH), _const_spec((1, CONV_CH)), _const_spec(wc.shape)],
        out_specs=[row(CONV_CH), row(CONV_CH), pl.BlockSpec((1, CONV_CH), lambda i: (0, 0))],
        out_shape=[jax.ShapeDtypeStruct((S, CONV_CH), F32), jax.ShapeDtypeStruct((S, CONV_CH), BF16),
                   jax.ShapeDtypeStruct((1, CONV_CH), F32)],
        compiler_params=_cp(("arbitrary",)),
    )(dy, o, ag, wc)


def _attn_bwd(qkv, do, t):
    S = qkv.shape[0]
    tk = KEY_BLOCKS * t
    nk = S // tk

    def body(q_ref, k_ref, v_ref, do_ref, dq_ref, dk_hbm, dv_hbm, g_buf, s_buf, r_ref, dq_acc, dk_ref, dv_ref,
             z_buf, da_buf):
        p = pl.program_id(0)
        i = pl.program_id(1)
        last = i // KEY_BLOCKS

        @pl.when(i == 0)
        def _():
            dk_ref[...] = jnp.zeros_like(dk_ref)
            dv_ref[...] = jnp.zeros_like(dv_ref)

        m_suf = _suffix_matrix(t)
        m_pre = _suffix_matrix(t, prefix=True)
        q = q_ref[...]
        do = do_ref[...]
        hms = [_lane_mask(h) for h in range(2)]
        qms = [jnp.where(hm, q, 0) * 0.125 for hm in hms]
        doms = [jnp.where(hm, do, 0) for hm in hms]
        dq_acc[...] = jnp.zeros_like(dq_acc)
        r_ref[...] = jnp.zeros_like(r_ref)

        def rows(sb):
            return pl.ds(pl.multiple_of(sb * tk, tk), tk)

        def fetch1(sb, slot, diagonal):
            kt = k_ref[rows(sb), :]
            vt = v_ref[rows(sb), :]
            mask = _causal_mask(i, sb, t) if diagonal else None
            for h in range(2):
                z_buf[slot, h] = _scores(qms[h], kt, mask)
                da_buf[slot, h] = _dot_nt(doms[h], vt)

        def load1(slot):
            return [(z_buf[slot, h], da_buf[slot, h]) for h in range(2)]

        def work1(sb, vals):
            dv = jnp.zeros((tk, PAIR), F32)
            for h in range(2):
                z, da = vals[h]
                A, sp, rs = _sb_tile(z, r_ref[h], m_suf)
                g_buf[h, sb] = A * da
                s_buf[h, sb] = 1.0 - jnp.exp(-sp)
                dv = dv + _dot_tn(A.astype(BF16), doms[h])
                r_ref[h] += rs
            dv_ref[rows(sb), :] += dv

        _sweep(last, last, True, fetch1, load1, work1)
        r_ref[...] = jnp.zeros_like(r_ref)

        def sweep2(sb):
            kt = k_ref[rows(sb), :]
            dk = jnp.zeros((tk, PAIR), F32)
            for h in range(2):
                g = g_buf[h, sb]
                pre, rs = _running_sums(g, m_pre, reverse=False)
                dzb = (g - s_buf[h, sb] * (pre + r_ref[h])).astype(BF16)
                dq_acc[...] += _dot(dzb, jnp.where(hms[h], kt, 0))
                dk = dk + _dot_tn(dzb, qms[h])
                r_ref[h] += rs
            dk_ref[rows(sb), :] += dk

        _sweep_plain(0, last, sweep2)
        dq_ref[...] = dq_acc[...] * 0.125

        @pl.when(i == S // t - 1)
        def _():
            cols = pl.ds(pl.multiple_of(p * PAIR, PAIR), PAIR)
            pltpu.sync_copy(dk_ref, dk_hbm.at[:, cols])
            pltpu.sync_copy(dv_ref, dv_hbm.at[:, cols])

    once = lambda cb: pl.BlockSpec((S, PAIR), cb, pipeline_mode=pl.Buffered(1))
    return pl.pallas_call(
        body, name="attn_bwd", grid=(N_PAIRS, S // t),
        in_specs=[pl.BlockSpec((t, PAIR), lambda p, i: (i, p)),
                  once(lambda p, i: (0, N_PAIRS + p)), once(lambda p, i: (0, 2 * N_PAIRS + p)),
                  pl.BlockSpec((t, PAIR), lambda p, i: (i, p))],
        out_specs=[pl.BlockSpec((t, PAIR), lambda p, i: (i, p)), _hbm(), _hbm()],
        out_shape=[jax.ShapeDtypeStruct((S, N_PAIRS * PAIR), F32)] * 3,
        scratch_shapes=[pltpu.VMEM((2, nk, t, tk), F32), pltpu.VMEM((2, nk, t, tk), F32),
                        pltpu.VMEM((2, t, 1), F32), pltpu.VMEM((t, PAIR), F32),
                        pltpu.VMEM((S, PAIR), F32), pltpu.VMEM((S, PAIR), F32),
                        pltpu.VMEM((2, 2, t, tk), F32), pltpu.VMEM((2, 2, t, tk), F32)],
        compiler_params=_cp(("arbitrary", "arbitrary"), vmem=VMEM_LIMIT_ATTN_BWD),
    )(qkv, qkv, qkv, do)


def _conv_bwd(uc, dco, cwf, cb, lg, lb, tm):
    S = uc.shape[0]
    hb = tm // HALO
    nb = S // tm
    ext = tm + HALO

    def body(uc_ref, prev_ref, next_ref, dco_ref, dnext_ref, cw_ref, cb_ref, lg_ref, lb_ref,
             duc_ref, dcw_ref, dcb_ref, dlg_ref, dlb_ref, glu_ref, dyc_ref, shg_ref, shd_ref):
        i = pl.program_id(0)
        last = i == nb - 1

        @pl.when(i == 0)
        def _():
            for ref in (dcw_ref, dcb_ref, dlg_ref, dlb_ref):
                ref[...] = jnp.zeros_like(ref)

        uc = uc_ref[...]
        glu_ref[0:HALO, :] = jnp.where(i == 0, 0.0, _glu(prev_ref[...]))
        glu_ref[HALO:ext, :] = _glu(uc)
        glu_ref[ext:ext + HALO, :] = _glu(next_ref[...])
        glu_ref[ext + HALO:ext + HALO + SUBLANES, :] = jnp.zeros((SUBLANES, CONV_CH), F32)
        _shift_copies(glu_ref, shg_ref)
        fwd_offs = [HALO - (CONV_WIDTH - 1) + w for w in range(CONV_WIDTH)]
        y = _conv_taps(cw_ref, glu_ref, shg_ref, fwd_offs, ext) + cb_ref[...]
        mu = jnp.mean(y, axis=-1, keepdims=True)
        yc = y - mu
        rstd = lax.rsqrt(jnp.mean(yc * yc, axis=-1, keepdims=True) + EPS)
        yhat = yc * rstd
        lg = lg_ref[...]
        ln = yhat * lg + lb_ref[...]
        sg = _sigmoid(ln)
        dout = jnp.concatenate([dco_ref[...], jnp.where(last, 0.0, dnext_ref[...])], axis=0)
        dln = dout * (sg * (1.0 + ln * (1.0 - sg)))
        dyh = dln * lg
        dyc = rstd * (dyh - jnp.mean(dyh, axis=-1, keepdims=True)
                      - yhat * jnp.mean(dyh * yhat, axis=-1, keepdims=True))
        dyc_ref[0:ext, :] = dyc
        dyc_ref[ext:ext + SUBLANES, :] = jnp.zeros((SUBLANES, CONV_CH), F32)
        _shift_copies(dyc_ref, shd_ref)
        dlg_ref[...] += jnp.sum((dln * yhat)[0:tm], axis=0, keepdims=True)
        dlb_ref[...] += jnp.sum(dln[0:tm], axis=0, keepdims=True)
        dcb_ref[...] += jnp.sum(dyc[0:tm], axis=0, keepdims=True)
        dglu = _conv_taps(cw_ref, dyc_ref, shd_ref, [CONV_WIDTH - 1 - w for w in range(CONV_WIDTH)], tm)
        d0 = dyc[0:tm]
        for w, off in enumerate(fwd_offs):
            dcw_ref[w:w + 1, :] += jnp.sum(d0 * _rows_at(glu_ref, shg_ref, off, tm), axis=0, keepdims=True)
        val, gate = uc[:, :CONV_CH], uc[:, CONV_CH:]
        sgate = _sigmoid(gate)
        duc_ref[:, :CONV_CH] = (dglu * sgate).astype(BF16)
        duc_ref[:, CONV_CH:] = (dglu * val * sgate * (1.0 - sgate)).astype(BF16)

    vec = pl.BlockSpec((1, CONV_CH), lambda i: (0, 0))
    nxt = lambda i: (jnp.minimum((i + 1) * hb, S // HALO - 1), 0)
    return pl.pallas_call(
        body, name="conv_bwd", grid=(nb,),
        in_specs=[pl.BlockSpec((tm, 2 * CONV_CH), lambda i: (i, 0)),
                  pl.BlockSpec((HALO, 2 * CONV_CH), lambda i: (jnp.maximum(i * hb - 1, 0), 0)),
                  pl.BlockSpec((HALO, 2 * CONV_CH), nxt),
                  pl.BlockSpec((tm, CONV_CH), lambda i: (i, 0)),
                  pl.BlockSpec((HALO, CONV_CH), nxt),
                  _const_spec(cwf.shape), _const_spec((1, CONV_CH)), _const_spec((1, CONV_CH)),
                  _const_spec((1, CONV_CH))],
        out_specs=[pl.BlockSpec((tm, 2 * CONV_CH), lambda i: (i, 0)),
                   pl.BlockSpec(cwf.shape, lambda i: (0, 0)), vec, vec, vec],
        out_shape=[jax.ShapeDtypeStruct((S, 2 * CONV_CH), BF16), jax.ShapeDtypeStruct(cwf.shape, F32)]
        + [jax.ShapeDtypeStruct((1, CONV_CH), F32)] * 3,
        scratch_shapes=[pltpu.VMEM((ext + HALO + SUBLANES, CONV_CH), F32), pltpu.VMEM((ext + SUBLANES, CONV_CH), F32),
                        pltpu.VMEM((SUBLANES - 1, ext + HALO, CONV_CH), F32),
                        pltpu.VMEM((SUBLANES - 1, ext, CONV_CH), F32)],
        compiler_params=_cp(("arbitrary",)),
    )(uc, uc, uc, dco, dco, cwf, cb, lg, lb)


def _in_bwd(duc, dq, dk, dv, x2, dh1, g1, wa, tm):
    S = x2.shape[0]

    def body(duc_ref, dq_ref, dk_ref, dv_ref, x_ref, dh1_ref, g_ref, w_ref, gx_ref, du_ref, dg_ref):
        i = pl.program_id(0)
        du = jnp.concatenate([duc_ref[...], dq_ref[...].astype(BF16), dk_ref[...].astype(BF16),
                              dv_ref[...].astype(BF16)], axis=1)
        du_ref[...] = du
        da = _dot_nt(du, w_ref[...])
        x = x_ref[...]
        r = _rms(x)
        dx, dg = _rms_bwd(da, x * r, r, g_ref[...])
        gx_ref[...] = dh1_ref[...] + dx

        @pl.when(i == 0)
        def _():
            dg_ref[...] = jnp.zeros_like(dg_ref)

        dg_ref[...] += jnp.sum(dg, axis=0, keepdims=True)

    row = lambda w: pl.BlockSpec((tm, w), lambda i: (i, 0))
    return pl.pallas_call(
        body, name="in_bwd", grid=(S // tm,),
        in_specs=[row(2 * CONV_CH), row(CONV_CH), row(CONV_CH), row(CONV_CH), row(D_MODEL), row(D_MODEL),
                  _const_spec((1, D_MODEL)), _const_spec(wa.shape)],
        out_specs=[row(D_MODEL), row(2560), pl.BlockSpec((1, D_MODEL), lambda i: (0, 0))],
        out_shape=[jax.ShapeDtypeStruct((S, D_MODEL), F32), jax.ShapeDtypeStruct((S, 2560), BF16),
                   jax.ShapeDtypeStruct((1, D_MODEL), F32)],
        compiler_params=_cp(("arbitrary",)),
    )(duc, dq, dk, dv, x2, dh1, g1, wa)


def _matmul_tn(xm, ym, tm, tn, ts, name):
    S, M = xm.shape
    N = ym.shape[1]

    def body(x_ref, y_ref, o_ref):
        @pl.when(pl.program_id(2) == 0)
        def _():
            o_ref[...] = jnp.zeros_like(o_ref)

        o_ref[...] += _dot_tn(x_ref[...], y_ref[...])

    return pl.pallas_call(
        body, name=name, grid=(M // tm, N // tn, S // ts),
        in_specs=[pl.BlockSpec((ts, tm), lambda m, n, s: (s, m)), pl.BlockSpec((ts, tn), lambda m, n, s: (s, n))],
        out_specs=pl.BlockSpec((tm, tn), lambda m, n, s: (m, n)),
        out_shape=jax.ShapeDtypeStruct((M, N), F32),
        compiler_params=_cp(("parallel", "parallel", "arbitrary")),
    )(xm, ym)


def _sibling_halves(grads):
    n = len(grads)

    def body(*refs):
        ins, outs, ssem, rsem = refs[:n], refs[n:2 * n], refs[2 * n], refs[2 * n + 1]
        x, y, c = lax.axis_index("x"), lax.axis_index("y"), lax.axis_index("c")
        copies = []
        for k in range(n):
            for j in range(N_CHIPS):
                copies.append(pltpu.make_async_remote_copy(
                    src_ref=ins[k].at[j, 1 - c], dst_ref=outs[k].at[j],
                    send_sem=ssem.at[N_CHIPS * k + j], recv_sem=rsem.at[N_CHIPS * k + j],
                    device_id=(x, y, 1 - c), device_id_type=MESH))
        for cp in copies:
            cp.start()
        for cp in copies:
            cp.wait()

    shapes = [jax.ShapeDtypeStruct((g.shape[0],) + g.shape[2:], F32) for g in grads]
    return pl.pallas_call(
        body, name="grad_sibling_halves", out_shape=shapes,
        in_specs=[_hbm()] * n, out_specs=[_hbm()] * n,
        scratch_shapes=[pltpu.SemaphoreType.DMA((N_CHIPS * n,)), pltpu.SemaphoreType.DMA((N_CHIPS * n,))],
    )(*grads)


def _add_half(c_arr, g, landed, name):
    def body(c_ref, g_ref, l_ref, o_ref):
        o_ref[...] = (g_ref[...] + l_ref[...]).astype(BF16)

    rows, n = g.shape[2], g.shape[3]
    grid = (N_CHIPS,)
    g_spec = pl.BlockSpec((None, None, rows, n), lambda j, c: (j, c[0], 0, 0))
    l_spec = pl.BlockSpec((None, rows, n), lambda j, c: (j, 0, 0))
    return pl.pallas_call(
        body, name=name,
        grid_spec=pltpu.PrefetchScalarGridSpec(num_scalar_prefetch=1, grid=grid, in_specs=[g_spec, l_spec],
                                               out_specs=l_spec),
        out_shape=jax.ShapeDtypeStruct(landed.shape, BF16),
        compiler_params=_cp(("parallel",)),
    )(c_arr, g, landed)


def _chip_scatter(parts):
    n = len(parts)

    def piece(ref, j):
        return ref.at[j]

    def body(*refs):
        ins, outs = refs[:n], refs[n:2 * n]
        lsem, ssem, rsem = refs[2 * n:]
        x, y, c = lax.axis_index("x"), lax.axis_index("y"), lax.axis_index("c")
        me = 2 * x + y
        chips = [(1 - x, y), (x, 1 - y), (1 - x, 1 - y)]
        copies = []
        for k in range(n):
            cp = pltpu.make_async_copy(piece(ins[k], me), outs[k].at[me], lsem.at[k])
            cp.start()
            copies.append(cp)
            for r, chip in enumerate(chips):
                cp = pltpu.make_async_remote_copy(
                    src_ref=piece(ins[k], 2 * chip[0] + chip[1]), dst_ref=outs[k].at[me],
                    send_sem=ssem.at[3 * k + r], recv_sem=rsem.at[3 * k + r],
                    device_id=(chip[0], chip[1], c), device_id_type=MESH)
                cp.start()
                copies.append(cp)
        for cp in copies:
            cp.wait()

    shapes = [jax.ShapeDtypeStruct(p.shape, p.dtype) for p in parts]
    return pl.pallas_call(
        body, name="grad_chip_scatter", out_shape=shapes,
        in_specs=[_hbm()] * n, out_specs=[_hbm()] * n,
        scratch_shapes=[pltpu.SemaphoreType.DMA((n,)), pltpu.SemaphoreType.DMA((3 * n,)),
                        pltpu.SemaphoreType.DMA((3 * n,))],
    )(*parts)


def _sum_chips(landed, name):
    _, rows, n = landed.shape
    tr = 256 if rows % 256 == 0 else rows

    def body(a_ref, b_ref, c_ref, d_ref, o_ref):
        f = lambda ref: ref[...].astype(F32)
        o_ref[...] = ((f(a_ref) + f(b_ref)) + f(c_ref)) + f(d_ref)

    specs = [pl.BlockSpec((None, tr, n), functools.partial(lambda i, j: (j, i, 0), j=j)) for j in range(N_CHIPS)]
    return pl.pallas_call(
        body, name=name, grid=(rows // tr,), in_specs=specs,
        out_specs=pl.BlockSpec((tr, n), lambda i: (i, 0)),
        out_shape=jax.ShapeDtypeStruct((rows, n), F32),
        compiler_params=_cp(("parallel",)),
    )(landed, landed, landed, landed)


def _share_halves(halves):
    n = len(halves)

    def body(*refs):
        ins, outs = refs[:n], refs[n:2 * n]
        ssem, rsem = refs[2 * n:]
        x, y, c = lax.axis_index("x"), lax.axis_index("y"), lax.axis_index("c")
        copies = [pltpu.make_async_remote_copy(
            src_ref=ins[k], dst_ref=outs[k], send_sem=ssem.at[k], recv_sem=rsem.at[k],
            device_id=(x, y, 1 - c), device_id_type=MESH) for k in range(n)]
        for cp in copies:
            cp.start()
        for cp in copies:
            cp.wait()

    shapes = [jax.ShapeDtypeStruct(h.shape, F32) for h in halves]
    return pl.pallas_call(
        body, name="grad_share_halves", out_shape=shapes,
        in_specs=[_hbm()] * n, out_specs=[_hbm()] * n,
        scratch_shapes=[pltpu.SemaphoreType.DMA((n,)), pltpu.SemaphoreType.DMA((n,))],
    )(*halves)


def _allreduce_small(packed):
    rows, n = packed.shape

    def body(in_ref, out_ref, land_ref, ssem, rsem):
        x, y, c = lax.axis_index("x"), lax.axis_index("y"), lax.axis_index("c")
        me = 4 * x + 2 * y + c
        land_ref[me] = in_ref[...]
        copies = []
        for r in range(1, 8):
            tx = 1 - x if r & 4 else x
            ty = 1 - y if r & 2 else y
            tc = 1 - c if r & 1 else c
            cp = pltpu.make_async_remote_copy(
                src_ref=in_ref, dst_ref=land_ref.at[me], send_sem=ssem.at[r - 1], recv_sem=rsem.at[r - 1],
                device_id=(tx, ty, tc), device_id_type=MESH)
            cp.start()
            copies.append(cp)
        for cp in copies:
            cp.wait()
        acc = land_ref[0]
        for k in range(1, 8):
            acc = acc + land_ref[k]
        out_ref[...] = acc

    return pl.pallas_call(
        body, name="allreduce_small", out_shape=jax.ShapeDtypeStruct((rows, n), F32),
        in_specs=[pl.BlockSpec(memory_space=pltpu.VMEM)], out_specs=pl.BlockSpec(memory_space=pltpu.VMEM),
        scratch_shapes=[pltpu.VMEM((8, rows, n), F32), pltpu.SemaphoreType.DMA((7,)),
                        pltpu.SemaphoreType.DMA((7,))],
    )(packed)


def _adamw_math(w, g, m, v):
    m = ADAM_B1 * m + (1.0 - ADAM_B1) * g
    v = ADAM_B2 * v + (1.0 - ADAM_B2) * (g * g)
    m_hat = m / (1.0 - ADAM_B1 ** ADAM_STEP)
    v_hat = v / (1.0 - ADAM_B2 ** ADAM_STEP)
    return -ADAM_LR * (m_hat / (jnp.sqrt(v_hat) + ADAM_EPS) + ADAM_WD * w), m, v


def _adamw_halves(c_arr, w, mine, other, m, v, name):
    rows, n = mine.shape
    tr = 256 if rows % 256 == 0 else rows
    nb = rows // tr

    def body(c_ref, w_ref, a_ref, b_ref, m_ref, v_ref, g_ref, d_ref, mo_ref, vo_ref):
        g = jnp.where(pl.program_id(0) == c_ref[0], a_ref[...], b_ref[...])
        g_ref[...] = g
        d_ref[...], mo_ref[...], vo_ref[...] = _adamw_math(w_ref[...], g, m_ref[...], v_ref[...])

    full = pl.BlockSpec((tr, n), lambda h, i, c: (h * nb + i, 0))
    half = pl.BlockSpec((tr, n), lambda h, i, c: (i, 0))
    return pl.pallas_call(
        body, name=name,
        grid_spec=pltpu.PrefetchScalarGridSpec(num_scalar_prefetch=1, grid=(2, nb),
                                               in_specs=[full, half, half, full, full], out_specs=[full] * 4),
        out_shape=[jax.ShapeDtypeStruct((2 * rows, n), F32)] * 4,
        compiler_params=_cp(("parallel", "parallel")),
    )(c_arr, w, mine, other, m, v)


def _adamw(w, g, m, v, name):
    rows, n = w.shape
    tr = 256 if rows % 256 == 0 else rows

    def body(w_ref, g_ref, m_ref, v_ref, d_ref, mo_ref, vo_ref):
        d_ref[...], mo_ref[...], vo_ref[...] = _adamw_math(w_ref[...], g_ref[...], m_ref[...], v_ref[...])

    spec = pl.BlockSpec((tr, n), lambda i: (i, 0))
    return pl.pallas_call(
        body, name=name, grid=(rows // tr,), in_specs=[spec] * 4, out_specs=[spec] * 3,
        out_shape=[jax.ShapeDtypeStruct((rows, n), F32)] * 3,
        compiler_params=_cp(("parallel",)),
    )(w, g, m, v)


def _rows8(a):
    a = a.reshape(-1, 128)
    return jnp.pad(a, ((0, (-a.shape[0]) % 8), (0, 0)))


def kernel(x, g_pre_mix, w_in, conv_w, conv_b, conv_ln_g, conv_ln_b, attn_norm_g, w_out, g_post_mix, g_pre_ffn, w_gate, w_up, w_down, g_post_ffn, loss_target, m_g_pre_mix, m_w_in, m_conv_w, m_conv_b, m_conv_ln_g, m_conv_ln_b, m_attn_norm_g, m_w_out, m_g_post_mix, m_g_pre_ffn, m_w_gate, m_w_up, m_w_down, m_g_post_ffn, v_g_pre_mix, v_w_in, v_conv_w, v_conv_b, v_conv_ln_g, v_conv_ln_b, v_attn_norm_g, v_w_out, v_g_post_mix, v_g_pre_ffn, v_w_gate, v_w_up, v_w_down, v_g_post_ffn):
    S = x.shape[1]
    tm_big = min(512, S)
    tm_ffn = min(256, S)
    t_att = min(256, S // KEY_BLOCKS)
    chip = 2 * lax.axis_index("x") + lax.axis_index("y")
    core = lax.axis_index("c")
    x2 = x.reshape(S, D_MODEL)
    tgt = loss_target.reshape(S, D_MODEL)
    ag = attn_norm_g.reshape(1, CONV_CH)

    a_sh = w_in[0].astype(BF16)
    b_sh = jnp.stack([w_gate[0], w_up[0]]).astype(BF16)
    c_sh = jnp.concatenate([w_out[0], w_down[0]], axis=0).astype(BF16)
    cw_sh = jnp.pad(conv_w[0, :, 0, :], ((0, 1), (0, 0)))
    wa4, wb4, wc4, cw4 = [lax.dynamic_update_index_in_dim(full, own, chip, 0) for full, own in
                          zip(_gather_weights(a_sh, b_sh, c_sh, cw_sh), (a_sh, b_sh, c_sh, cw_sh))]
    cols = lambda w4: jnp.transpose(w4, (1, 0, 2)).reshape(w4.shape[1], N_CHIPS * w4.shape[2])
    wa = cols(wa4)
    wg, wu = cols(wb4[:, 0]), cols(wb4[:, 1])
    wo = wc4[:, :OUT_SH].reshape(D_MODEL, D_MODEL)
    wd = wc4[:, OUT_SH:].reshape(D_FF, D_MODEL)
    cwf = cols(cw4)

    a_bf, uc, qkv = _in_proj(x2, g_pre_mix, wa, tm_big)
    conv_out = _conv_fwd(uc, cwf, conv_b, conv_ln_g, conv_ln_b, tm_big)
    o = _attn_fwd(qkv, t_att)
    mixed, yv, h1, f_in = _out_proj(conv_out, o, ag, wo, x2, g_post_mix, g_pre_ffn, tm_big)
    df, dh2, dg4, loss_part = _ffn_fwd(f_in, h1, tgt, wg, wu, wd, g_post_ffn, tm_ffn)

    act, dgt, dup, dh1, dy, dg3, dg2 = _ffn_bwd(f_in, df, dh2, h1, yv, wg, wu, wd, g_pre_ffn, g_post_mix, tm_ffn)
    dco, do, dag = _out_bwd(dy, o, ag, wo, tm_big)
    dq, dk, dv = _attn_bwd(qkv, do, t_att)
    duc, dcw, dcb, dlg, dlb = _conv_bwd(uc, dco, cwf, conv_b, conv_ln_g, conv_ln_b, tm_big)
    grad_x, du, dg1 = _in_bwd(duc, dq, dk, dv, x2, dh1, g_pre_mix, wa, tm_big)
    ts = min(512, S)
    gw_in = _matmul_tn(a_bf, du, D_MODEL, 1280, ts, "grad_w_in")
    gw_out = _matmul_tn(mixed, dy, D_MODEL, D_MODEL, ts, "grad_w_out")
    gw_gate = _matmul_tn(f_in, dgt, D_MODEL, D_FF // 2, ts, "grad_w_gate")
    gw_up = _matmul_tn(f_in, dup, D_MODEL, D_FF // 2, ts, "grad_w_up")
    gw_down = _matmul_tn(act, df, D_FF // 2, D_MODEL, ts, "grad_w_down")

    by_cols = lambda g: jnp.transpose(g.reshape(2, D_MODEL // 2, N_CHIPS, -1), (2, 0, 1, 3))
    by_rows = lambda g: g.reshape(N_CHIPS, 2, g.shape[0] // (2 * N_CHIPS), g.shape[1])
    views = [by_cols(gw_in), by_cols(gw_gate), by_cols(gw_up), by_rows(gw_out), by_rows(gw_down)]
    names = ["w_in", "w_gate", "w_up", "w_out", "w_down"]
    landed = _sibling_halves(views)
    c_arr = core.reshape(1).astype(jnp.int32)
    parts = [_add_half(c_arr, g, l, "grad_half_" + nm) for g, l, nm in zip(views, landed, names)]
    slots = _chip_scatter(parts)
    halves = [_sum_chips(s, "grad_sum_" + nm) for s, nm in zip(slots, names)]
    others = _share_halves(halves)
    mine = dict(zip(names, halves))
    other = dict(zip(names, others))

    small = [dg1, dcb, dlg, dlb, dag, dg2, dg3, dg4]
    packed = jnp.concatenate([_rows8(s) for s in small] + [_rows8(dcw), _rows8(loss_part)], axis=0)
    red = _allreduce_small(packed)
    sizes = [D_MODEL, CONV_CH, CONV_CH, CONV_CH, CONV_CH, D_MODEL, D_MODEL, D_MODEL]
    g_small = [red[8 * k:8 * k + n // 128].reshape(1, n) for k, n in enumerate(sizes)]
    cw_red = red[64:64 + 128].reshape(HALO, CONV_CH)
    g_cw = lax.dynamic_slice(cw_red, (0, chip * 128), (HALO, 128))
    loss = red[192, 0]

    big = []
    for w, m, v, nm in [(w_in, m_w_in, v_w_in, "w_in"), (w_out, m_w_out, v_w_out, "w_out"),
                        (w_gate, m_w_gate, v_w_gate, "w_gate"), (w_up, m_w_up, v_w_up, "w_up"),
                        (w_down, m_w_down, v_w_down, "w_down")]:
        big.append(_adamw_halves(c_arr, w[0], mine[nm], other[nm], m[0], v[0], "adamw_" + nm))
    sm_w = [g_pre_mix, conv_b, conv_ln_g, conv_ln_b, ag, g_post_mix, g_pre_ffn, g_post_ffn]
    sm_m = [m_g_pre_mix, m_conv_b, m_conv_ln_g, m_conv_ln_b, m_attn_norm_g, m_g_post_mix, m_g_pre_ffn, m_g_post_ffn]
    sm_v = [v_g_pre_mix, v_conv_b, v_conv_ln_g, v_conv_ln_b, v_attn_norm_g, v_g_post_mix, v_g_pre_ffn, v_g_post_ffn]
    pad_cw = lambda a: jnp.pad(a[0, :, 0, :], ((0, 1), (0, 0)))

    def pack(vecs, cw):
        return jnp.concatenate([_rows8(a) for a in vecs] + [cw], axis=0)

    sd, smn, svn = _adamw(pack(sm_w, pad_cw(conv_w)), pack(g_small, g_cw), pack(sm_m, pad_cw(m_conv_w)),
                          pack(sm_v, pad_cw(v_conv_w)), "adamw_small")

    def unpack(p):
        vecs = [p[8 * k:8 * k + n // 128].reshape(1, n) for k, n in enumerate(sizes)]
        return vecs, p[64:64 + CONV_WIDTH].reshape(1, CONV_WIDTH, 1, 128)

    def ordered(vecs, cw, w_in_, w_out_, w_gate_, w_up_, w_down_):
        g1_, cb_, lg_, lb_, ag_, g2_, g3_, g4_ = vecs
        return [g1_, w_in_[None], cw, cb_, lg_, lb_, ag_.reshape(1, 8, HEAD_DIM), w_out_[None], g2_, g3_,
                w_gate_[None], w_up_[None], w_down_[None], g4_]

    grads = ordered(g_small, g_cw[:CONV_WIDTH].reshape(1, CONV_WIDTH, 1, 128), *[b[0] for b in big])
    outs = []
    for idx, p in enumerate((sd, smn, svn)):
        vecs, cw = unpack(p)
        outs += ordered(vecs, cw, *[b[idx + 1] for b in big])
    return (loss, grad_x.reshape(1, S, D_MODEL), *grads, *outs)
```

```python
import functools
import math

import jax
import jax.numpy as jnp
from jax import lax
from jax.experimental import pallas as pl
from jax.experimental.pallas import tpu as pltpu

F32 = jnp.float32
BF16 = jnp.bfloat16
MESH = pl.DeviceIdType.MESH

D_MODEL = 1024
CONV_CH = 512
CONV_WIDTH = 31
HEAD_DIM = 64
PAIR = 2 * HEAD_DIM
N_PAIRS = 4
D_FF = 2816
N_CHIPS = 4
IN_SH = 2560 // N_CHIPS
FF_SH = D_FF // N_CHIPS
OUT_SH = D_MODEL // N_CHIPS
C_ROWS = OUT_SH + FF_SH
EPS = 1e-6
HALO = 32

ADAM_LR = 0.001
ADAM_B1 = 0.9
ADAM_B2 = 0.999
ADAM_EPS = 1e-08
ADAM_WD = 0.01
ADAM_STEP = 10

VMEM_LIMIT = 56 * 2 ** 20
VMEM_LIMIT_ATTN_BWD = 60 * 2 ** 20


def _cp(sem=None, vmem=VMEM_LIMIT):
    return pltpu.CompilerParams(dimension_semantics=sem, vmem_limit_bytes=vmem)


def _hbm():
    return pl.BlockSpec(memory_space=pltpu.HBM)


def _const_spec(shape):
    nd = len(shape)
    return pl.BlockSpec(shape, lambda *_: (0,) * nd, pipeline_mode=pl.Buffered(1))


def _dot(a, b):
    return jnp.dot(a, b, preferred_element_type=F32)


def _dot_nt(a, b):
    return lax.dot_general(a, b, (((1,), (1,)), ((), ())), preferred_element_type=F32)


def _dot_tn(a, b):
    return lax.dot_general(a, b, (((0,), (0,)), ((), ())), preferred_element_type=F32)


def _split3(x):
    b0 = x.astype(BF16)
    r1 = x - b0.astype(F32)
    b1 = r1.astype(BF16)
    b2 = (r1 - b1.astype(F32)).astype(BF16)
    return b0, b1, b2


def _split2(x):
    hi = x.astype(BF16)
    lo = (x - hi.astype(F32)).astype(BF16)
    return hi, lo


def _sigmoid(x):
    return 1.0 / (1.0 + jnp.exp(-x))


def _head_mean(x, seg):
    b0, b1, b2 = _split3(x)
    return (_dot(b0, seg) + _dot(b1, seg) + _dot(b2, seg)) * (1.0 / HEAD_DIM)


def _seg_matrix(n):
    r = lax.broadcasted_iota(jnp.int32, (n, n), 0) // HEAD_DIM
    c = lax.broadcasted_iota(jnp.int32, (n, n), 1) // HEAD_DIM
    return (r == c).astype(BF16)


def _rms(x):
    return lax.rsqrt(jnp.mean(x * x, axis=-1, keepdims=True) + EPS)


def _rms_bwd(dy, n, r, g):
    dn = dy * g
    dx = r * (dn - n * jnp.mean(dn * n, axis=-1, keepdims=True))
    return dx, dy * n


class _GatherPlan:
    def __init__(self, srcs, outs, lead, ssem, rsem):
        self.srcs, self.outs, self.lead, self.ssem, self.rsem = srcs, outs, lead, ssem, rsem
        x, y, self.c = lax.axis_index("x"), lax.axis_index("y"), lax.axis_index("c")
        self.me = 2 * x + y
        self.sibling = (x, y, 1 - self.c)
        self.chips = [(1 - x, y), (x, 1 - y), (1 - x, 1 - y)]

    def _half(self, ref, i, h):
        if self.lead[i]:
            return ref.at[h]
        rows = ref.shape[0] // 2
        return ref.at[pl.ds(h * rows, rows)]

    def _ici(self, i, k, origin):
        return pltpu.make_async_remote_copy(
            src_ref=self._half(self.srcs[i], i, self.c), dst_ref=self._half(self.outs[i].at[origin], i, self.c),
            send_sem=self.ssem.at[6 * i + k], recv_sem=self.rsem.at[6 * i + k],
            device_id=(self.chips[k][0], self.chips[k][1], self.c), device_id_type=MESH)

    def _d2d(self, i, k, h):
        origin = 2 * self.chips[k][0] + self.chips[k][1]
        piece = self._half(self.outs[i].at[origin], i, h)
        return pltpu.make_async_remote_copy(
            src_ref=piece, dst_ref=piece, send_sem=self.ssem.at[6 * i + 3 + k],
            recv_sem=self.rsem.at[6 * i + 3 + k], device_id=self.sibling, device_id_type=MESH)

    def _each(self):
        return [(i, k) for i in range(len(self.srcs)) for k in range(3)]

    def start(self):
        for i, k in self._each():
            self._ici(i, k, self.me).start()

    def forward(self):
        for i, k in self._each():
            self._ici(i, k, 2 * self.chips[k][0] + self.chips[k][1]).wait_recv()
            self._d2d(i, k, self.c).start()

    def finish(self):
        for i, k in self._each():
            self._d2d(i, k, 1 - self.c).wait_recv()
        for i, k in self._each():
            self._ici(i, k, self.me).wait_send()
            self._d2d(i, k, self.c).wait_send()


def _gather_shapes(shards):
    return [jax.ShapeDtypeStruct((N_CHIPS,) + s.shape, s.dtype) for s in shards]


def _gather_weights(shards, lead):
    n = len(shards)

    def body(*refs):
        plan = _GatherPlan(refs[:n], refs[n:2 * n], lead, refs[2 * n], refs[2 * n + 1])
        plan.start()
        plan.forward()
        plan.finish()

    return pl.pallas_call(
        body, name="gather_weights", out_shape=_gather_shapes(shards),
        in_specs=[_hbm()] * n, out_specs=[_hbm()] * n,
        scratch_shapes=[pltpu.SemaphoreType.DMA((6 * n,)), pltpu.SemaphoreType.DMA((6 * n,))],
    )(*shards)


def _in_proj(x2, g1, wa, tm):
    S = x2.shape[0]

    def body(x_ref, g_ref, w_ref, a_ref, uc_ref, qkv_ref):
        x = x_ref[...]
        a = (x * _rms(x) * g_ref[...]).astype(BF16)
        a_ref[...] = a
        uc_ref[...] = _dot(a, w_ref[:, 0:2 * CONV_CH])
        qkv_ref[...] = _dot(a, w_ref[:, 2 * CONV_CH:]).astype(BF16)

    return pl.pallas_call(
        body, name="in_proj", grid=(S // tm,),
        in_specs=[pl.BlockSpec((tm, D_MODEL), lambda i: (i, 0)), _const_spec((1, D_MODEL)),
                  _const_spec(wa.shape)],
        out_specs=[pl.BlockSpec((tm, D_MODEL), lambda i: (i, 0)),
                   pl.BlockSpec((tm, 2 * CONV_CH), lambda i: (i, 0)),
                   pl.BlockSpec((tm, 1536), lambda i: (i, 0))],
        out_shape=[jax.ShapeDtypeStruct((S, D_MODEL), BF16), jax.ShapeDtypeStruct((S, 2 * CONV_CH), F32),
                   jax.ShapeDtypeStruct((S, 1536), BF16)],
        compiler_params=_cp(("parallel",)),
    )(x2, g1, wa)


SUBLANES = 8


def _shift_copies(src_ref, sh_ref):
    rows = sh_ref.shape[1]
    for b in range(1, SUBLANES):
        sh_ref[b - 1] = src_ref[pl.ds(b, rows), :]


def _rows_at(src_ref, sh_ref, off, rows):
    a, b = divmod(off, SUBLANES)
    if b == 0:
        return src_ref[pl.ds(SUBLANES * a, rows), :]
    return sh_ref[b - 1, pl.ds(SUBLANES * a, rows), :]


def _conv_taps(cw_ref, src_ref, sh_ref, offs, rows):
    acc = None
    for w, off in enumerate(offs):
        term = cw_ref[w:w + 1, :] * _rows_at(src_ref, sh_ref, off, rows)
        acc = term if acc is None else acc + term
    return acc


def _glu(uc):
    return uc[:, :CONV_CH] * _sigmoid(uc[:, CONV_CH:])


def _conv_fwd(uc, cwf, cb, lg, lb, tm):
    S = uc.shape[0]
    hb = tm // HALO

    def body(uc_ref, prev_ref, cw_ref, cb_ref, lg_ref, lb_ref, out_ref, glu_ref, sh_ref):
        i = pl.program_id(0)
        glu_ref[0:HALO, :] = jnp.where(i == 0, 0.0, _glu(prev_ref[...]))
        glu_ref[HALO:HALO + tm, :] = _glu(uc_ref[...])
        glu_ref[HALO + tm:HALO + tm + SUBLANES, :] = jnp.zeros((SUBLANES, CONV_CH), F32)
        _shift_copies(glu_ref, sh_ref)
        offs = [HALO - (CONV_WIDTH - 1) + w for w in range(CONV_WIDTH)]
        y = _conv_taps(cw_ref, glu_ref, sh_ref, offs, tm) + cb_ref[...]
        mu = jnp.mean(y, axis=-1, keepdims=True)
        yc = y - mu
        rstd = lax.rsqrt(jnp.mean(yc * yc, axis=-1, keepdims=True) + EPS)
        ln = yc * rstd * lg_ref[...] + lb_ref[...]
        out_ref[...] = (ln * _sigmoid(ln)).astype(BF16)

    return pl.pallas_call(
        body, name="conv_fwd", grid=(S // tm,),
        in_specs=[pl.BlockSpec((tm, 2 * CONV_CH), lambda i: (i, 0)),
                  pl.BlockSpec((HALO, 2 * CONV_CH), lambda i: (jnp.maximum(i * hb - 1, 0), 0)),
                  _const_spec(cwf.shape), _const_spec((1, CONV_CH)), _const_spec((1, CONV_CH)),
                  _const_spec((1, CONV_CH))],
        out_specs=pl.BlockSpec((tm, CONV_CH), lambda i: (i, 0)),
        out_shape=jax.ShapeDtypeStruct((S, CONV_CH), BF16),
        scratch_shapes=[pltpu.VMEM((HALO + tm + SUBLANES, CONV_CH), F32),
                        pltpu.VMEM((SUBLANES - 1, HALO + tm, CONV_CH), F32)],
        compiler_params=_cp(("parallel",)),
    )(uc, uc, cwf, cb, lg, lb)


def _lane_mask(h):
    lane = lax.broadcasted_iota(jnp.int32, (1, PAIR), 1)
    return (lane >= HEAD_DIM * h) & (lane < HEAD_DIM * (h + 1))


def _neg_abs(x):
    bits = lax.bitcast_convert_type(x, jnp.uint32) | jnp.uint32(0x80000000)
    return lax.bitcast_convert_type(bits, F32)


def _tri_dot(x, m):
    return _dot(x.astype(BF16), m)


MASKED = -1e30
KEY_BLOCKS = 4


def _running_sums(x, m, reverse):
    t = m.shape[0]
    order = range(KEY_BLOCKS - 1, -1, -1) if reverse else range(KEY_BLOCKS)
    out = [None] * KEY_BLOCKS
    carry = None
    for b in order:
        xb = x[:, b * t:(b + 1) * t]
        cb = _tri_dot(xb, m)
        out[b] = cb if carry is None else cb + carry
        rs = jnp.sum(xb, axis=1, keepdims=True)
        carry = rs if carry is None else carry + rs
    return jnp.concatenate(out, axis=1), carry


def _sb_tile(z, r, m_suf):
    sp = jnp.maximum(z, 0.0) + jnp.log(1.0 + jnp.exp(_neg_abs(z)))
    c, rs = _running_sums(sp, m_suf, reverse=True)
    ex = z - c
    if r is not None:
        ex = ex - r
    return jnp.exp(ex), sp, rs


def _scores(qm, kt, mask):
    z = _dot_nt(qm, kt)
    return z if mask is None else jnp.where(mask, z, MASKED)


def _causal_mask(i, sb, t):
    row = lax.broadcasted_iota(jnp.int32, (t, KEY_BLOCKS * t), 0) + i * t
    col = lax.broadcasted_iota(jnp.int32, (t, KEY_BLOCKS * t), 1) + sb * (KEY_BLOCKS * t)
    return col < row


def _sweep_plain(first, count, tile):
    def step(n, carry):
        tile(first + n)
        return carry

    lax.fori_loop(0, count + 1, step, 0)


def _sweep(first, count, down, fetch, load, work):
    lo, hi = (first - count, first) if down else (first, first + count)
    tile = lambda j: jnp.clip(first - j if down else first + j, lo, hi)
    fetch(first, 0, True)

    def step(n, carry):
        j = 2 * n
        vals = load(0)
        fetch(tile(j + 1), 1, False)
        work(tile(j), vals)
        vals = load(1)
        fetch(tile(j + 2), 0, False)
        work(tile(j + 1), vals)
        return carry

    lax.fori_loop(0, (count + 1) // 2, step, 0)

    @pl.when(lax.rem(count, 2) == 0)
    def _():
        work(tile(count), load(0))


def _suffix_matrix(t, prefix=False):
    row = lax.broadcasted_iota(jnp.int32, (t, t), 0)
    col = lax.broadcasted_iota(jnp.int32, (t, t), 1)
    return ((row <= col) if prefix else (row >= col)).astype(BF16)


def _attn_fwd(qkv, t, shards, lead):
    S = qkv.shape[0]
    tk = KEY_BLOCKS * t

    ng = len(shards)
    nq = S // t

    def body(*refs):
        q_ref, k_ref, v_ref = refs[:3]
        o_ref = refs[3 + ng]
        acc_ref, r_ref, z_buf, ssem, rsem = refs[4 + 2 * ng:]
        p = pl.program_id(0)
        i = pl.program_id(1)
        plan = _GatherPlan(refs[3:3 + ng], refs[4 + ng:4 + 2 * ng], lead, ssem, rsem)
        pl.when((p == 0) & (i == 0))(plan.start)
        pl.when((p == 1) & (i == 0))(plan.forward)
        last = i // KEY_BLOCKS
        m_suf = _suffix_matrix(t)
        q = q_ref[...]
        hms = [_lane_mask(h) for h in range(2)]
        qms = [jnp.where(hm, q, 0) * 0.125 for hm in hms]
        acc_ref[...] = jnp.zeros_like(acc_ref)
        r_ref[...] = jnp.zeros_like(r_ref)

        def rows(sb):
            return pl.ds(pl.multiple_of(sb * tk, tk), tk)

        def fetch(sb, slot, diagonal):
            kt = k_ref[rows(sb), :]
            mask = _causal_mask(i, sb, t) if diagonal else None
            for h in range(2):
                z_buf[slot, h] = _scores(qms[h], kt, mask)

        def load(slot):
            return [z_buf[slot, h] for h in range(2)]

        def work(sb, zs):
            vt = v_ref[rows(sb), :]
            for h in range(2):
                a_loc, _, rs = _sb_tile(zs[h], None, m_suf)
                r = r_ref[h]
                acc_ref[...] += _dot(a_loc.astype(BF16), jnp.where(hms[h], vt, 0)) * jnp.exp(-r)
                r_ref[h] = r + rs

        _sweep(last, last, True, fetch, load, work)
        o_ref[...] = acc_ref[...]
        pl.when((p == N_PAIRS - 1) & (i == nq - 1))(plan.finish)

    return pl.pallas_call(
        body, name="attn_fwd", grid=(N_PAIRS, nq),
        in_specs=[pl.BlockSpec((t, PAIR), lambda p, i: (i, p)),
                  pl.BlockSpec((S, PAIR), lambda p, i: (0, N_PAIRS + p)),
                  pl.BlockSpec((S, PAIR), lambda p, i: (0, 2 * N_PAIRS + p))] + [_hbm()] * ng,
        out_specs=[pl.BlockSpec((t, PAIR), lambda p, i: (i, p))] + [_hbm()] * ng,
        out_shape=[jax.ShapeDtypeStruct((S, N_PAIRS * PAIR), F32)] + _gather_shapes(shards),
        scratch_shapes=[pltpu.VMEM((t, PAIR), F32), pltpu.VMEM((2, t, 1), F32),
                        pltpu.VMEM((2, 2, t, tk), F32),
                        pltpu.SemaphoreType.DMA((6 * ng,)), pltpu.SemaphoreType.DMA((6 * ng,))],
        compiler_params=_cp(("arbitrary", "arbitrary")),
    )(qkv, qkv, qkv, *shards)


def _out_proj(conv_out, o, ag, wc, x2, g2, g3, tm):
    S = o.shape[0]

    def body(co_ref, o_ref, ag_ref, w_ref, x_ref, g2_ref, g3_ref, mix_ref, y_ref, h1_ref, fin_ref):
        seg = _seg_matrix(CONV_CH)
        o = o_ref[...]
        att = (o * lax.rsqrt(_head_mean(o * o, seg) + EPS) * ag_ref[...]).astype(BF16)
        co = co_ref[...]
        mix_ref[:, :CONV_CH] = co
        mix_ref[:, CONV_CH:] = att
        y = _dot(co, w_ref[0:CONV_CH, :]) + _dot(att, w_ref[CONV_CH:, :])
        y_ref[...] = y
        h1 = x_ref[...] + y * _rms(y) * g2_ref[...]
        h1_ref[...] = h1
        fin_ref[...] = (h1 * _rms(h1) * g3_ref[...]).astype(BF16)

    row = lambda w: pl.BlockSpec((tm, w), lambda i: (i, 0))
    return pl.pallas_call(
        body, name="out_proj", grid=(S // tm,),
        in_specs=[row(CONV_CH), row(CONV_CH), _const_spec((1, CONV_CH)), _const_spec(wc.shape),
                  row(D_MODEL), _const_spec((1, D_MODEL)), _const_spec((1, D_MODEL))],
        out_specs=[row(D_MODEL)] * 4,
        out_shape=[jax.ShapeDtypeStruct((S, D_MODEL), BF16), jax.ShapeDtypeStruct((S, D_MODEL), F32),
                   jax.ShapeDtypeStruct((S, D_MODEL), F32), jax.ShapeDtypeStruct((S, D_MODEL), BF16)],
        compiler_params=_cp(("parallel",)),
    )(conv_out, o, ag, wc, x2, g2, g3)


def _ffn_fwd(f_in, h1, tgt, wg, wu, wd, g4, tm):
    S = f_in.shape[0]

    def body(fin_ref, h1_ref, tgt_ref, wg_ref, wu_ref, wd_ref, g4_ref, df_ref, dh2_ref, dg4_ref, loss_ref):
        i = pl.program_id(0)
        fin = fin_ref[...]
        gt = _dot(fin, wg_ref[...])
        up = _dot(fin, wu_ref[...])
        f = _dot((gt * _sigmoid(gt) * up).astype(BF16), wd_ref[...])
        r = _rms(f)
        n = f * r
        g4 = g4_ref[...]
        err = h1_ref[...] + n * g4 - tgt_ref[...]
        dh2 = err * (1.0 / D_MODEL)
        dh2_ref[...] = dh2
        df, dg = _rms_bwd(dh2, n, r, g4)
        df_ref[...] = df.astype(BF16)

        @pl.when(i == 0)
        def _():
            dg4_ref[...] = jnp.zeros_like(dg4_ref)
            loss_ref[...] = jnp.zeros_like(loss_ref)

        dg4_ref[...] += jnp.sum(dg, axis=0, keepdims=True)
        part = jnp.sum(jnp.sum(err * err, axis=1, keepdims=True), axis=0, keepdims=True)
        loss_ref[...] += part * (0.5 / D_MODEL)

    row = lambda w: pl.BlockSpec((tm, w), lambda i: (i, 0))
    return pl.pallas_call(
        body, name="ffn_fwd", grid=(S // tm,),
        in_specs=[row(D_MODEL), row(D_MODEL), row(D_MODEL), _const_spec(wg.shape), _const_spec(wu.shape),
                  _const_spec(wd.shape), _const_spec((1, D_MODEL))],
        out_specs=[row(D_MODEL), row(D_MODEL), pl.BlockSpec((1, D_MODEL), lambda i: (0, 0)),
                   pl.BlockSpec((1, 128), lambda i: (0, 0))],
        out_shape=[jax.ShapeDtypeStruct((S, D_MODEL), BF16), jax.ShapeDtypeStruct((S, D_MODEL), F32),
                   jax.ShapeDtypeStruct((1, D_MODEL), F32), jax.ShapeDtypeStruct((1, 128), F32)],
        compiler_params=_cp(("arbitrary",)),
    )(f_in, h1, tgt, wg, wu, wd, g4)


def _ffn_bwd(f_in, df, dh2, h1, yv, wg, wu, wd, g3, g2, tm):
    S = f_in.shape[0]

    def body(fin_ref, df_ref, dh2_ref, h1_ref, y_ref, wg_ref, wu_ref, wd_ref, g3_ref, g2_ref,
             act_ref, dgt_ref, dup_ref, dh1_ref, dy_ref, dg3_ref, dg2_ref):
        i = pl.program_id(0)
        fin = fin_ref[...]
        df = df_ref[...]
        gt = _dot(fin, wg_ref[...])
        up = _dot(fin, wu_ref[...])
        sg = _sigmoid(gt)
        silu = gt * sg
        act_ref[...] = (silu * up).astype(BF16)
        dact = _dot_nt(df, wd_ref[...])
        dgt = (dact * up * (sg * (1.0 + gt * (1.0 - sg)))).astype(BF16)
        dup = (dact * silu).astype(BF16)
        dgt_ref[...] = dgt
        dup_ref[...] = dup
        dfin = _dot_nt(dgt, wg_ref[...]) + _dot_nt(dup, wu_ref[...])
        h1 = h1_ref[...]
        r3 = _rms(h1)
        dh1_n, dg3 = _rms_bwd(dfin, h1 * r3, r3, g3_ref[...])
        dh1 = dh2_ref[...] + dh1_n
        dh1_ref[...] = dh1
        y = y_ref[...]
        r2 = _rms(y)
        dy, dg2 = _rms_bwd(dh1, y * r2, r2, g2_ref[...])
        dy_ref[...] = dy.astype(BF16)

        @pl.when(i == 0)
        def _():
            dg3_ref[...] = jnp.zeros_like(dg3_ref)
            dg2_ref[...] = jnp.zeros_like(dg2_ref)

        dg3_ref[...] += jnp.sum(dg3, axis=0, keepdims=True)
        dg2_ref[...] += jnp.sum(dg2, axis=0, keepdims=True)

    row = lambda w: pl.BlockSpec((tm, w), lambda i: (i, 0))
    vec = pl.BlockSpec((1, D_MODEL), lambda i: (0, 0))
    return pl.pallas_call(
        body, name="ffn_bwd", grid=(S // tm,),
        in_specs=[row(D_MODEL)] * 5 + [_const_spec(wg.shape), _const_spec(wu.shape), _const_spec(wd.shape),
                                       _const_spec((1, D_MODEL)), _const_spec((1, D_MODEL))],
        out_specs=[row(D_FF), row(D_FF), row(D_FF), row(D_MODEL), row(D_MODEL), vec, vec],
        out_shape=[jax.ShapeDtypeStruct((S, D_FF), BF16)] * 3
        + [jax.ShapeDtypeStruct((S, D_MODEL), F32), jax.ShapeDtypeStruct((S, D_MODEL), BF16),
           jax.ShapeDtypeStruct((1, D_MODEL), F32), jax.ShapeDtypeStruct((1, D_MODEL), F32)],
        compiler_params=_cp(("arbitrary",)),
    )(f_in, df, dh2, h1, yv, wg, wu, wd, g3, g2)


def _out_bwd(dy, o, ag, wc, tm):
    S = o.shape[0]

    def body(dy_ref, o_ref, ag_ref, w_ref, dco_ref, do_ref, dag_ref):
        i = pl.program_id(0)
        seg = _seg_matrix(CONV_CH)
        dy = dy_ref[...]
        dco_ref[...] = _dot_nt(dy, w_ref[0:CONV_CH, :])
        datt = _dot_nt(dy, w_ref[CONV_CH:, :])
        o = o_ref[...]
        r = lax.rsqrt(_head_mean(o * o, seg) + EPS)
        n = o * r
        dn = datt * ag_ref[...]
        do_ref[...] = (r * (dn - n * _head_mean(dn * n, seg))).astype(BF16)

        @pl.when(i == 0)
        def _():
            dag_ref[...] = jnp.zeros_like(dag_ref)

        dag_ref[...] += jnp.sum(datt * n, axis=0, keepdims=True)

    row = lambda w: pl.BlockSpec((tm, w), lambda i: (i, 0))
    return pl.pallas_call(
        body, name="out_bwd", grid=(S // tm,),
        in_specs=[row(D_MODEL), row(CONV_CH), _const_spec((1, CONV_CH)), _const_spec(wc.shape)],
        out_specs=[row(CONV_CH), row(CONV_CH), pl.BlockSpec((1, CONV_CH), lambda i: (0, 0))],
        out_shape=[jax.ShapeDtypeStruct((S, CONV_CH), F32), jax.ShapeDtypeStruct((S, CONV_CH), BF16),
                   jax.ShapeDtypeStruct((1, CONV_CH), F32)],
        compiler_params=_cp(("arbitrary",)),
    )(dy, o, ag, wc)


def _attn_bwd(qkv, do, t, parts):
    S = qkv.shape[0]
    tk = KEY_BLOCKS * t
    nk = S // tk
    ns = len(parts)

    def body(*refs):
        q_ref, k_ref, v_ref, do_ref = refs[:4]
        dq_ref, dk_hbm, dv_hbm = refs[4 + ns:7 + ns]
        g_buf, s_buf, r_ref, dq_acc, dk_ref, dv_ref, z_buf, da_buf = refs[7 + 2 * ns:15 + 2 * ns]
        p = pl.program_id(0)
        i = pl.program_id(1)
        plan = _ScatterPlan(refs[4:4 + ns], refs[7 + ns:7 + 2 * ns], *refs[15 + 2 * ns:])
        pl.when((p == 0) & (i == 0))(plan.start)
        last = i // KEY_BLOCKS

        @pl.when(i == 0)
        def _():
            dk_ref[...] = jnp.zeros_like(dk_ref)
            dv_ref[...] = jnp.zeros_like(dv_ref)

        m_suf = _suffix_matrix(t)
        m_pre = _suffix_matrix(t, prefix=True)
        q = q_ref[...]
        do = do_ref[...]
        hms = [_lane_mask(h) for h in range(2)]
        qms = [jnp.where(hm, q, 0) * 0.125 for hm in hms]
        doms = [jnp.where(hm, do, 0) for hm in hms]
        dq_acc[...] = jnp.zeros_like(dq_acc)
        r_ref[...] = jnp.zeros_like(r_ref)

        def rows(sb):
            return pl.ds(pl.multiple_of(sb * tk, tk), tk)

        def fetch1(sb, slot, diagonal):
            kt = k_ref[rows(sb), :]
            vt = v_ref[rows(sb), :]
            mask = _causal_mask(i, sb, t) if diagonal else None
            for h in range(2):
                z_buf[slot, h] = _scores(qms[h], kt, mask)
                da_buf[slot, h] = _dot_nt(doms[h], vt)

        def load1(slot):
            return [(z_buf[slot, h], da_buf[slot, h]) for h in range(2)]

        def work1(sb, vals):
            dv = jnp.zeros((tk, PAIR), F32)
            for h in range(2):
                z, da = vals[h]
                A, sp, rs = _sb_tile(z, r_ref[h], m_suf)
                g_buf[h, sb] = A * da
                s_buf[h, sb] = 1.0 - jnp.exp(-sp)
                dv = dv + _dot_tn(A.astype(BF16), doms[h])
                r_ref[h] += rs
            dv_ref[rows(sb), :] += dv

        _sweep(last, last, True, fetch1, load1, work1)
        r_ref[...] = jnp.zeros_like(r_ref)

        def sweep2(sb):
            kt = k_ref[rows(sb), :]
            dk = jnp.zeros((tk, PAIR), F32)
            for h in range(2):
                g = g_buf[h, sb]
                pre, rs = _running_sums(g, m_pre, reverse=False)
                dzb = (g - s_buf[h, sb] * (pre + r_ref[h])).astype(BF16)
                dq_acc[...] += _dot(dzb, jnp.where(hms[h], kt, 0))
                dk = dk + _dot_tn(dzb, qms[h])
                r_ref[h] += rs
            dk_ref[rows(sb), :] += dk

        _sweep_plain(0, last, sweep2)
        dq_ref[...] = dq_acc[...] * 0.125

        @pl.when(i == S // t - 1)
        def _():
            cols = pl.ds(pl.multiple_of(p * PAIR, PAIR), PAIR)
            pltpu.sync_copy(dk_ref, dk_hbm.at[:, cols])
            pltpu.sync_copy(dv_ref, dv_hbm.at[:, cols])

        pl.when((p == N_PAIRS - 1) & (i == S // t - 1))(plan.finish)

    once = lambda cb: pl.BlockSpec((S, PAIR), cb, pipeline_mode=pl.Buffered(1))
    return pl.pallas_call(
        body, name="attn_bwd", grid=(N_PAIRS, S // t),
        in_specs=[pl.BlockSpec((t, PAIR), lambda p, i: (i, p)),
                  once(lambda p, i: (0, N_PAIRS + p)), once(lambda p, i: (0, 2 * N_PAIRS + p)),
                  pl.BlockSpec((t, PAIR), lambda p, i: (i, p))] + [_hbm()] * ns,
        out_specs=[pl.BlockSpec((t, PAIR), lambda p, i: (i, p)), _hbm(), _hbm()] + [_hbm()] * ns,
        out_shape=[jax.ShapeDtypeStruct((S, N_PAIRS * PAIR), F32)] * 3
        + [jax.ShapeDtypeStruct(pt.shape, pt.dtype) for pt in parts],
        scratch_shapes=[pltpu.VMEM((2, nk, t, tk), F32), pltpu.VMEM((2, nk, t, tk), F32),
                        pltpu.VMEM((2, t, 1), F32), pltpu.VMEM((t, PAIR), F32),
                        pltpu.VMEM((S, PAIR), F32), pltpu.VMEM((S, PAIR), F32),
                        pltpu.VMEM((2, 2, t, tk), F32), pltpu.VMEM((2, 2, t, tk), F32)] + _scatter_sems(ns),
        compiler_params=_cp(("arbitrary", "arbitrary"), vmem=VMEM_LIMIT_ATTN_BWD),
    )(qkv, qkv, qkv, do, *parts)


def _conv_bwd(uc, dco, cwf, cb, lg, lb, tm):
    S = uc.shape[0]
    hb = tm // HALO
    nb = S // tm
    ext = tm + HALO

    def body(uc_ref, prev_ref, next_ref, dco_ref, dnext_ref, cw_ref, cb_ref, lg_ref, lb_ref,
             duc_ref, dcw_ref, dcb_ref, dlg_ref, dlb_ref, glu_ref, dyc_ref, shg_ref, shd_ref):
        i = pl.program_id(0)
        last = i == nb - 1

        @pl.when(i == 0)
        def _():
            for ref in (dcw_ref, dcb_ref, dlg_ref, dlb_ref):
                ref[...] = jnp.zeros_like(ref)

        uc = uc_ref[...]
        glu_ref[0:HALO, :] = jnp.where(i == 0, 0.0, _glu(prev_ref[...]))
        glu_ref[HALO:ext, :] = _glu(uc)
        glu_ref[ext:ext + HALO, :] = _glu(next_ref[...])
        glu_ref[ext + HALO:ext + HALO + SUBLANES, :] = jnp.zeros((SUBLANES, CONV_CH), F32)
        _shift_copies(glu_ref, shg_ref)
        fwd_offs = [HALO - (CONV_WIDTH - 1) + w for w in range(CONV_WIDTH)]
        y = _conv_taps(cw_ref, glu_ref, shg_ref, fwd_offs, ext) + cb_ref[...]
        mu = jnp.mean(y, axis=-1, keepdims=True)
        yc = y - mu
        rstd = lax.rsqrt(jnp.mean(yc * yc, axis=-1, keepdims=True) + EPS)
        yhat = yc * rstd
        lg = lg_ref[...]
        ln = yhat * lg + lb_ref[...]
        sg = _sigmoid(ln)
        dout = jnp.concatenate([dco_ref[...], jnp.where(last, 0.0, dnext_ref[...])], axis=0)
        dln = dout * (sg * (1.0 + ln * (1.0 - sg)))
        dyh = dln * lg
        dyc = rstd * (dyh - jnp.mean(dyh, axis=-1, keepdims=True)
                      - yhat * jnp.mean(dyh * yhat, axis=-1, keepdims=True))
        dyc_ref[0:ext, :] = dyc
        dyc_ref[ext:ext + SUBLANES, :] = jnp.zeros((SUBLANES, CONV_CH), F32)
        _shift_copies(dyc_ref, shd_ref)
        dlg_ref[...] += jnp.sum((dln * yhat)[0:tm], axis=0, keepdims=True)
        dlb_ref[...] += jnp.sum(dln[0:tm], axis=0, keepdims=True)
        dcb_ref[...] += jnp.sum(dyc[0:tm], axis=0, keepdims=True)
        dglu = _conv_taps(cw_ref, dyc_ref, shd_ref, [CONV_WIDTH - 1 - w for w in range(CONV_WIDTH)], tm)
        d0 = dyc[0:tm]
        for w, off in enumerate(fwd_offs):
            dcw_ref[w:w + 1, :] += jnp.sum(d0 * _rows_at(glu_ref, shg_ref, off, tm), axis=0, keepdims=True)
        val, gate = uc[:, :CONV_CH], uc[:, CONV_CH:]
        sgate = _sigmoid(gate)
        duc_ref[:, :CONV_CH] = (dglu * sgate).astype(BF16)
        duc_ref[:, CONV_CH:] = (dglu * val * sgate * (1.0 - sgate)).astype(BF16)

    vec = pl.BlockSpec((1, CONV_CH), lambda i: (0, 0))
    nxt = lambda i: (jnp.minimum((i + 1) * hb, S // HALO - 1), 0)
    return pl.pallas_call(
        body, name="conv_bwd", grid=(nb,),
        in_specs=[pl.BlockSpec((tm, 2 * CONV_CH), lambda i: (i, 0)),
                  pl.BlockSpec((HALO, 2 * CONV_CH), lambda i: (jnp.maximum(i * hb - 1, 0), 0)),
                  pl.BlockSpec((HALO, 2 * CONV_CH), nxt),
                  pl.BlockSpec((tm, CONV_CH), lambda i: (i, 0)),
                  pl.BlockSpec((HALO, CONV_CH), nxt),
                  _const_spec(cwf.shape), _const_spec((1, CONV_CH)), _const_spec((1, CONV_CH)),
                  _const_spec((1, CONV_CH))],
        out_specs=[pl.BlockSpec((tm, 2 * CONV_CH), lambda i: (i, 0)),
                   pl.BlockSpec(cwf.shape, lambda i: (0, 0)), vec, vec, vec],
        out_shape=[jax.ShapeDtypeStruct((S, 2 * CONV_CH), BF16), jax.ShapeDtypeStruct(cwf.shape, F32)]
        + [jax.ShapeDtypeStruct((1, CONV_CH), F32)] * 3,
        scratch_shapes=[pltpu.VMEM((ext + HALO + SUBLANES, CONV_CH), F32), pltpu.VMEM((ext + SUBLANES, CONV_CH), F32),
                        pltpu.VMEM((SUBLANES - 1, ext + HALO, CONV_CH), F32),
                        pltpu.VMEM((SUBLANES - 1, ext, CONV_CH), F32)],
        compiler_params=_cp(("arbitrary",)),
    )(uc, uc, uc, dco, dco, cwf, cb, lg, lb)


def _in_bwd(duc, dq, dk, dv, x2, dh1, g1, wa, tm):
    S = x2.shape[0]

    def body(duc_ref, dq_ref, dk_ref, dv_ref, x_ref, dh1_ref, g_ref, w_ref, gx_ref, du_ref, dg_ref):
        i = pl.program_id(0)
        du = jnp.concatenate([duc_ref[...], dq_ref[...].astype(BF16), dk_ref[...].astype(BF16),
                              dv_ref[...].astype(BF16)], axis=1)
        du_ref[...] = du
        da = _dot_nt(du, w_ref[...])
        x = x_ref[...]
        r = _rms(x)
        dx, dg = _rms_bwd(da, x * r, r, g_ref[...])
        gx_ref[...] = dh1_ref[...] + dx

        @pl.when(i == 0)
        def _():
            dg_ref[...] = jnp.zeros_like(dg_ref)

        dg_ref[...] += jnp.sum(dg, axis=0, keepdims=True)

    row = lambda w: pl.BlockSpec((tm, w), lambda i: (i, 0))
    return pl.pallas_call(
        body, name="in_bwd", grid=(S // tm,),
        in_specs=[row(2 * CONV_CH), row(CONV_CH), row(CONV_CH), row(CONV_CH), row(D_MODEL), row(D_MODEL),
                  _const_spec((1, D_MODEL)), _const_spec(wa.shape)],
        out_specs=[row(D_MODEL), row(2560), pl.BlockSpec((1, D_MODEL), lambda i: (0, 0))],
        out_shape=[jax.ShapeDtypeStruct((S, D_MODEL), F32), jax.ShapeDtypeStruct((S, 2560), BF16),
                   jax.ShapeDtypeStruct((1, D_MODEL), F32)],
        compiler_params=_cp(("arbitrary",)),
    )(duc, dq, dk, dv, x2, dh1, g1, wa)


def _matmul_tn(xm, ym, tm, tn, ts, name):
    S, M = xm.shape
    N = ym.shape[1]

    def body(x_ref, y_ref, o_ref):
        @pl.when(pl.program_id(2) == 0)
        def _():
            o_ref[...] = jnp.zeros_like(o_ref)

        o_ref[...] += _dot_tn(x_ref[...], y_ref[...])

    return pl.pallas_call(
        body, name=name, grid=(M // tm, N // tn, S // ts),
        in_specs=[pl.BlockSpec((ts, tm), lambda m, n, s: (s, m)), pl.BlockSpec((ts, tn), lambda m, n, s: (s, n))],
        out_specs=pl.BlockSpec((tm, tn), lambda m, n, s: (m, n)),
        out_shape=jax.ShapeDtypeStruct((M, N), F32),
        compiler_params=_cp(("parallel", "parallel", "arbitrary")),
    )(xm, ym)


def _sibling_halves(grads, name):
    n = len(grads)

    def body(*refs):
        ins, outs, ssem, rsem = refs[:n], refs[n:2 * n], refs[2 * n], refs[2 * n + 1]
        x, y, c = lax.axis_index("x"), lax.axis_index("y"), lax.axis_index("c")
        copies = []
        for k in range(n):
            for j in range(N_CHIPS):
                copies.append(pltpu.make_async_remote_copy(
                    src_ref=ins[k].at[j, 1 - c], dst_ref=outs[k].at[j],
                    send_sem=ssem.at[N_CHIPS * k + j], recv_sem=rsem.at[N_CHIPS * k + j],
                    device_id=(x, y, 1 - c), device_id_type=MESH))
        for cp in copies:
            cp.start()
        for cp in copies:
            cp.wait()

    shapes = [jax.ShapeDtypeStruct((g.shape[0],) + g.shape[2:], F32) for g in grads]
    return pl.pallas_call(
        body, name=name, out_shape=shapes,
        in_specs=[_hbm()] * n, out_specs=[_hbm()] * n,
        scratch_shapes=[pltpu.SemaphoreType.DMA((N_CHIPS * n,)), pltpu.SemaphoreType.DMA((N_CHIPS * n,))],
    )(*grads)


def _add_half(c_arr, g, landed, name):
    def body(c_ref, g_ref, l_ref, o_ref):
        o_ref[...] = (g_ref[...] + l_ref[...]).astype(BF16)

    rows, n = g.shape[2], g.shape[3]
    grid = (N_CHIPS,)
    g_spec = pl.BlockSpec((None, None, rows, n), lambda j, c: (j, c[0], 0, 0))
    l_spec = pl.BlockSpec((None, rows, n), lambda j, c: (j, 0, 0))
    return pl.pallas_call(
        body, name=name,
        grid_spec=pltpu.PrefetchScalarGridSpec(num_scalar_prefetch=1, grid=grid, in_specs=[g_spec, l_spec],
                                               out_specs=l_spec),
        out_shape=jax.ShapeDtypeStruct(landed.shape, BF16),
        compiler_params=_cp(("parallel",)),
    )(c_arr, g, landed)


class _ScatterPlan:
    def __init__(self, ins, outs, lsem, ssem, rsem):
        x, y, c = lax.axis_index("x"), lax.axis_index("y"), lax.axis_index("c")
        me = 2 * x + y
        self.copies = []
        for k in range(len(ins)):
            self.copies.append(pltpu.make_async_copy(ins[k].at[me], outs[k].at[me], lsem.at[k]))
            for r, chip in enumerate([(1 - x, y), (x, 1 - y), (1 - x, 1 - y)]):
                self.copies.append(pltpu.make_async_remote_copy(
                    src_ref=ins[k].at[2 * chip[0] + chip[1]], dst_ref=outs[k].at[me],
                    send_sem=ssem.at[3 * k + r], recv_sem=rsem.at[3 * k + r],
                    device_id=(chip[0], chip[1], c), device_id_type=MESH))

    def start(self):
        for cp in self.copies:
            cp.start()

    def finish(self):
        for cp in self.copies:
            cp.wait()


def _scatter_sems(n):
    return [pltpu.SemaphoreType.DMA((n,)), pltpu.SemaphoreType.DMA((3 * n,)), pltpu.SemaphoreType.DMA((3 * n,))]


def _chip_scatter(parts):
    n = len(parts)

    def body(*refs):
        plan = _ScatterPlan(refs[:n], refs[n:2 * n], *refs[2 * n:])
        plan.start()
        plan.finish()

    shapes = [jax.ShapeDtypeStruct(p.shape, p.dtype) for p in parts]
    return pl.pallas_call(
        body, name="grad_chip_scatter", out_shape=shapes,
        in_specs=[_hbm()] * n, out_specs=[_hbm()] * n, scratch_shapes=_scatter_sems(n),
    )(*parts)


def _sum_chips(landed, name):
    _, rows, n = landed.shape
    tr = 256 if rows % 256 == 0 else rows

    def body(a_ref, b_ref, c_ref, d_ref, o_ref):
        f = lambda ref: ref[...].astype(F32)
        o_ref[...] = ((f(a_ref) + f(b_ref)) + f(c_ref)) + f(d_ref)

    specs = [pl.BlockSpec((None, tr, n), functools.partial(lambda i, j: (j, i, 0), j=j)) for j in range(N_CHIPS)]
    return pl.pallas_call(
        body, name=name, grid=(rows // tr,), in_specs=specs,
        out_specs=pl.BlockSpec((tr, n), lambda i: (i, 0)),
        out_shape=jax.ShapeDtypeStruct((rows, n), F32),
        compiler_params=_cp(("parallel",)),
    )(landed, landed, landed, landed)


def _share_halves(halves):
    n = len(halves)

    def body(*refs):
        ins, outs = refs[:n], refs[n:2 * n]
        ssem, rsem = refs[2 * n:]
        x, y, c = lax.axis_index("x"), lax.axis_index("y"), lax.axis_index("c")
        copies = [pltpu.make_async_remote_copy(
            src_ref=ins[k], dst_ref=outs[k], send_sem=ssem.at[k], recv_sem=rsem.at[k],
            device_id=(x, y, 1 - c), device_id_type=MESH) for k in range(n)]
        for cp in copies:
            cp.start()
        for cp in copies:
            cp.wait()

    shapes = [jax.ShapeDtypeStruct(h.shape, F32) for h in halves]
    return pl.pallas_call(
        body, name="grad_share_halves", out_shape=shapes,
        in_specs=[_hbm()] * n, out_specs=[_hbm()] * n,
        scratch_shapes=[pltpu.SemaphoreType.DMA((n,)), pltpu.SemaphoreType.DMA((n,))],
    )(*halves)


def _allreduce_small(packed):
    rows, n = packed.shape

    def body(in_ref, out_ref, land_ref, ssem, rsem):
        x, y, c = lax.axis_index("x"), lax.axis_index("y"), lax.axis_index("c")
        me = 4 * x + 2 * y + c
        land_ref[me] = in_ref[...]
        copies = []
        for r in range(1, 8):
            tx = 1 - x if r & 4 else x
            ty = 1 - y if r & 2 else y
            tc = 1 - c if r & 1 else c
            cp = pltpu.make_async_remote_copy(
                src_ref=in_ref, dst_ref=land_ref.at[me], send_sem=ssem.at[r - 1], recv_sem=rsem.at[r - 1],
                device_id=(tx, ty, tc), device_id_type=MESH)
            cp.start()
            copies.append(cp)
        for cp in copies:
            cp.wait()
        acc = land_ref[0]
        for k in range(1, 8):
            acc = acc + land_ref[k]
        out_ref[...] = acc

    return pl.pallas_call(
        body, name="allreduce_small", out_shape=jax.ShapeDtypeStruct((rows, n), F32),
        in_specs=[pl.BlockSpec(memory_space=pltpu.VMEM)], out_specs=pl.BlockSpec(memory_space=pltpu.VMEM),
        scratch_shapes=[pltpu.VMEM((8, rows, n), F32), pltpu.SemaphoreType.DMA((7,)),
                        pltpu.SemaphoreType.DMA((7,))],
    )(packed)


def _adamw_math(w, g, m, v):
    m = ADAM_B1 * m + (1.0 - ADAM_B1) * g
    v = ADAM_B2 * v + (1.0 - ADAM_B2) * (g * g)
    m_hat = m / (1.0 - ADAM_B1 ** ADAM_STEP)
    v_hat = v / (1.0 - ADAM_B2 ** ADAM_STEP)
    return -ADAM_LR * (m_hat / (jnp.sqrt(v_hat) + ADAM_EPS) + ADAM_WD * w), m, v


def _adamw_halves(c_arr, w, mine, other, m, v, name):
    rows, n = mine.shape
    tr = 256 if rows % 256 == 0 else rows
    nb = rows // tr

    def body(c_ref, w_ref, a_ref, b_ref, m_ref, v_ref, g_ref, d_ref, mo_ref, vo_ref):
        g = jnp.where(pl.program_id(0) == c_ref[0], a_ref[...], b_ref[...])
        g_ref[...] = g
        d_ref[...], mo_ref[...], vo_ref[...] = _adamw_math(w_ref[...], g, m_ref[...], v_ref[...])

    full = pl.BlockSpec((tr, n), lambda h, i, c: (h * nb + i, 0))
    half = pl.BlockSpec((tr, n), lambda h, i, c: (i, 0))
    return pl.pallas_call(
        body, name=name,
        grid_spec=pltpu.PrefetchScalarGridSpec(num_scalar_prefetch=1, grid=(2, nb),
                                               in_specs=[full, half, half, full, full], out_specs=[full] * 4),
        out_shape=[jax.ShapeDtypeStruct((2 * rows, n), F32)] * 4,
        compiler_params=_cp(("parallel", "parallel")),
    )(c_arr, w, mine, other, m, v)


def _adamw(w, g, m, v, name):
    rows, n = w.shape
    tr = 256 if rows % 256 == 0 else rows

    def body(w_ref, g_ref, m_ref, v_ref, d_ref, mo_ref, vo_ref):
        d_ref[...], mo_ref[...], vo_ref[...] = _adamw_math(w_ref[...], g_ref[...], m_ref[...], v_ref[...])

    spec = pl.BlockSpec((tr, n), lambda i: (i, 0))
    return pl.pallas_call(
        body, name=name, grid=(rows // tr,), in_specs=[spec] * 4, out_specs=[spec] * 3,
        out_shape=[jax.ShapeDtypeStruct((rows, n), F32)] * 3,
        compiler_params=_cp(("parallel",)),
    )(w, g, m, v)


def _rows8(a):
    a = a.reshape(-1, 128)
    return jnp.pad(a, ((0, (-a.shape[0]) % 8), (0, 0)))


def kernel(x, g_pre_mix, w_in, conv_w, conv_b, conv_ln_g, conv_ln_b, attn_norm_g, w_out, g_post_mix, g_pre_ffn, w_gate, w_up, w_down, g_post_ffn, loss_target, m_g_pre_mix, m_w_in, m_conv_w, m_conv_b, m_conv_ln_g, m_conv_ln_b, m_attn_norm_g, m_w_out, m_g_post_mix, m_g_pre_ffn, m_w_gate, m_w_up, m_w_down, m_g_post_ffn, v_g_pre_mix, v_w_in, v_conv_w, v_conv_b, v_conv_ln_g, v_conv_ln_b, v_attn_norm_g, v_w_out, v_g_post_mix, v_g_pre_ffn, v_w_gate, v_w_up, v_w_down, v_g_post_ffn):
    S = x.shape[1]
    tm_big = min(512, S)
    tm_ffn = min(256, S)
    t_att = min(256, S // KEY_BLOCKS)
    chip = 2 * lax.axis_index("x") + lax.axis_index("y")
    core = lax.axis_index("c")
    x2 = x.reshape(S, D_MODEL)
    tgt = loss_target.reshape(S, D_MODEL)
    ag = attn_norm_g.reshape(1, CONV_CH)

    a_sh = w_in[0].astype(BF16)
    b_sh = jnp.stack([w_gate[0], w_up[0]]).astype(BF16)
    c_sh = jnp.concatenate([w_out[0], w_down[0]], axis=0).astype(BF16)
    cw_sh = jnp.pad(conv_w[0, :, 0, :], ((0, 1), (0, 0)))
    own = lambda full, shard: lax.dynamic_update_index_in_dim(full, shard, chip, 0)
    cols = lambda w4: jnp.transpose(w4, (1, 0, 2)).reshape(w4.shape[1], N_CHIPS * w4.shape[2])
    wa4, cw4 = _gather_weights([a_sh, cw_sh], [False, False])
    wa = cols(own(wa4, a_sh))
    cwf = cols(own(cw4, cw_sh))

    a_bf, uc, qkv = _in_proj(x2, g_pre_mix, wa, tm_big)
    conv_out = _conv_fwd(uc, cwf, conv_b, conv_ln_g, conv_ln_b, tm_big)
    o, wb4, wc4 = _attn_fwd(qkv, t_att, [b_sh, c_sh], [True, False])
    wb4, wc4 = own(wb4, b_sh), own(wc4, c_sh)
    wg, wu = cols(wb4[:, 0]), cols(wb4[:, 1])
    wo = wc4[:, :OUT_SH].reshape(D_MODEL, D_MODEL)
    wd = wc4[:, OUT_SH:].reshape(D_FF, D_MODEL)
    mixed, yv, h1, f_in = _out_proj(conv_out, o, ag, wo, x2, g_post_mix, g_pre_ffn, tm_big)
    df, dh2, dg4, loss_part = _ffn_fwd(f_in, h1, tgt, wg, wu, wd, g_post_ffn, tm_ffn)

    act, dgt, dup, dh1, dy, dg3, dg2 = _ffn_bwd(f_in, df, dh2, h1, yv, wg, wu, wd, g_pre_ffn, g_post_mix, tm_ffn)
    dco, do, dag = _out_bwd(dy, o, ag, wo, tm_big)
    ts = min(512, S)
    gw_out = _matmul_tn(mixed, dy, D_MODEL, D_MODEL, ts, "grad_w_out")
    gw_gate = _matmul_tn(f_in, dgt, D_MODEL, D_FF // 2, ts, "grad_w_gate")
    gw_up = _matmul_tn(f_in, dup, D_MODEL, D_FF // 2, ts, "grad_w_up")
    gw_down = _matmul_tn(act, df, D_FF // 2, D_MODEL, ts, "grad_w_down")

    by_cols = lambda g: jnp.transpose(g.reshape(2, D_MODEL // 2, N_CHIPS, -1), (2, 0, 1, 3))
    by_rows = lambda g: g.reshape(N_CHIPS, 2, g.shape[0] // (2 * N_CHIPS), g.shape[1])
    c_arr = core.reshape(1).astype(jnp.int32)

    def chip_partials(views, nms):
        landed = _sibling_halves(views, "grad_sibling_halves_" + nms[0])
        return [_add_half(c_arr, g, l, "grad_half_" + nm) for g, l, nm in zip(views, landed, nms)]

    early = ["w_gate", "w_up", "w_out", "w_down"]
    parts = chip_partials([by_cols(gw_gate), by_cols(gw_up), by_rows(gw_out), by_rows(gw_down)], early)
    dq, dk, dv, *slots = _attn_bwd(qkv, do, t_att, parts)
    duc, dcw, dcb, dlg, dlb = _conv_bwd(uc, dco, cwf, conv_b, conv_ln_g, conv_ln_b, tm_big)
    grad_x, du, dg1 = _in_bwd(duc, dq, dk, dv, x2, dh1, g_pre_mix, wa, tm_big)
    gw_in = _matmul_tn(a_bf, du, D_MODEL, 1280, ts, "grad_w_in")
    slots += _chip_scatter(chip_partials([by_cols(gw_in)], ["w_in"]))
    names = early + ["w_in"]
    halves = [_sum_chips(s, "grad_sum_" + nm) for s, nm in zip(slots, names)]
    others = _share_halves(halves)
    mine = dict(zip(names, halves))
    other = dict(zip(names, others))

    small = [dg1, dcb, dlg, dlb, dag, dg2, dg3, dg4]
    packed = jnp.concatenate([_rows8(s) for s in small] + [_rows8(dcw), _rows8(loss_part)], axis=0)
    red = _allreduce_small(packed)
    sizes = [D_MODEL, CONV_CH, CONV_CH, CONV_CH, CONV_CH, D_MODEL, D_MODEL, D_MODEL]
    g_small = [red[8 * k:8 * k + n // 128].reshape(1, n) for k, n in enumerate(sizes)]
    cw_red = red[64:64 + 128].reshape(HALO, CONV_CH)
    g_cw = lax.dynamic_slice(cw_red, (0, chip * 128), (HALO, 128))
    loss = red[192, 0]

    big = []
    for w, m, v, nm in [(w_in, m_w_in, v_w_in, "w_in"), (w_out, m_w_out, v_w_out, "w_out"),
                        (w_gate, m_w_gate, v_w_gate, "w_gate"), (w_up, m_w_up, v_w_up, "w_up"),
                        (w_down, m_w_down, v_w_down, "w_down")]:
        big.append(_adamw_halves(c_arr, w[0], mine[nm], other[nm], m[0], v[0], "adamw_" + nm))
    sm_w = [g_pre_mix, conv_b, conv_ln_g, conv_ln_b, ag, g_post_mix, g_pre_ffn, g_post_ffn]
    sm_m = [m_g_pre_mix, m_conv_b, m_conv_ln_g, m_conv_ln_b, m_attn_norm_g, m_g_post_mix, m_g_pre_ffn, m_g_post_ffn]
    sm_v = [v_g_pre_mix, v_conv_b, v_conv_ln_g, v_conv_ln_b, v_attn_norm_g, v_g_post_mix, v_g_pre_ffn, v_g_post_ffn]
    pad_cw = lambda a: jnp.pad(a[0, :, 0, :], ((0, 1), (0, 0)))

    def pack(vecs, cw):
        return jnp.concatenate([_rows8(a) for a in vecs] + [cw], axis=0)

    sd, smn, svn = _adamw(pack(sm_w, pad_cw(conv_w)), pack(g_small, g_cw), pack(sm_m, pad_cw(m_conv_w)),
                          pack(sm_v, pad_cw(v_conv_w)), "adamw_small")

    def unpack(p):
        vecs = [p[8 * k:8 * k + n // 128].reshape(1, n) for k, n in enumerate(sizes)]
        return vecs, p[64:64 + CONV_WIDTH].reshape(1, CONV_WIDTH, 1, 128)

    def ordered(vecs, cw, w_in_, w_out_, w_gate_, w_up_, w_down_):
        g1_, cb_, lg_, lb_, ag_, g2_, g3_, g4_ = vecs
        return [g1_, w_in_[None], cw, cb_, lg_, lb_, ag_.reshape(1, 8, HEAD_DIM), w_out_[None], g2_, g3_,
                w_gate_[None], w_up_[None], w_down_[None], g4_]

    grads = ordered(g_small, g_cw[:CONV_WIDTH].reshape(1, CONV_WIDTH, 1, 128), *[b[0] for b in big])
    outs = []
    for idx, p in enumerate((sd, smn, svn)):
        vecs, cw = unpack(p)
        outs += ordered(vecs, cw, *[b[idx + 1] for b in big])
    return (loss, grad_x.reshape(1, S, D_MODEL), *grads, *outs)
```

```python
import functools
import math

import jax
import jax.numpy as jnp
from jax import lax
from jax.experimental import pallas as pl
from jax.experimental.pallas import tpu as pltpu

F32 = jnp.float32
BF16 = jnp.bfloat16
MESH = pl.DeviceIdType.MESH

D_MODEL = 1024
CONV_CH = 512
CONV_WIDTH = 31
HEAD_DIM = 64
PAIR = 2 * HEAD_DIM
N_PAIRS = 4
D_FF = 2816
N_CHIPS = 4
IN_SH = 2560 // N_CHIPS
FF_SH = D_FF // N_CHIPS
OUT_SH = D_MODEL // N_CHIPS
C_ROWS = OUT_SH + FF_SH
EPS = 1e-6
HALO = 32

ADAM_LR = 0.001
ADAM_B1 = 0.9
ADAM_B2 = 0.999
ADAM_EPS = 1e-08
ADAM_WD = 0.01
ADAM_STEP = 10

VMEM_LIMIT = 56 * 2 ** 20
VMEM_LIMIT_ATTN_BWD = 60 * 2 ** 20


def _cp(sem=None, vmem=VMEM_LIMIT):
    return pltpu.CompilerParams(dimension_semantics=sem, vmem_limit_bytes=vmem)


def _hbm():
    return pl.BlockSpec(memory_space=pltpu.HBM)


def _const_spec(shape):
    nd = len(shape)
    return pl.BlockSpec(shape, lambda *_: (0,) * nd, pipeline_mode=pl.Buffered(1))


def _dot(a, b):
    return jnp.dot(a, b, preferred_element_type=F32)


def _dot_nt(a, b):
    return lax.dot_general(a, b, (((1,), (1,)), ((), ())), preferred_element_type=F32)


def _dot_tn(a, b):
    return lax.dot_general(a, b, (((0,), (0,)), ((), ())), preferred_element_type=F32)


def _split3(x):
    b0 = x.astype(BF16)
    r1 = x - b0.astype(F32)
    b1 = r1.astype(BF16)
    b2 = (r1 - b1.astype(F32)).astype(BF16)
    return b0, b1, b2


def _split2(x):
    hi = x.astype(BF16)
    lo = (x - hi.astype(F32)).astype(BF16)
    return hi, lo


def _sigmoid(x):
    return 1.0 / (1.0 + jnp.exp(-x))


def _head_mean(x, seg):
    b0, b1, b2 = _split3(x)
    return (_dot(b0, seg) + _dot(b1, seg) + _dot(b2, seg)) * (1.0 / HEAD_DIM)


def _seg_matrix(n):
    r = lax.broadcasted_iota(jnp.int32, (n, n), 0) // HEAD_DIM
    c = lax.broadcasted_iota(jnp.int32, (n, n), 1) // HEAD_DIM
    return (r == c).astype(BF16)


def _rms(x):
    return lax.rsqrt(jnp.mean(x * x, axis=-1, keepdims=True) + EPS)


def _rms_bwd(dy, n, r, g):
    dn = dy * g
    dx = r * (dn - n * jnp.mean(dn * n, axis=-1, keepdims=True))
    return dx, dy * n


class _GatherPlan:
    def __init__(self, srcs, outs, lead, ssem, rsem):
        self.srcs, self.outs, self.lead, self.ssem, self.rsem = srcs, outs, lead, ssem, rsem
        x, y, self.c = lax.axis_index("x"), lax.axis_index("y"), lax.axis_index("c")
        self.me = 2 * x + y
        self.sibling = (x, y, 1 - self.c)
        self.chips = [(1 - x, y), (x, 1 - y), (1 - x, 1 - y)]

    def _half(self, ref, i, h):
        if self.lead[i]:
            return ref.at[h]
        rows = ref.shape[0] // 2
        return ref.at[pl.ds(h * rows, rows)]

    def _ici(self, i, k, origin):
        return pltpu.make_async_remote_copy(
            src_ref=self._half(self.srcs[i], i, self.c), dst_ref=self._half(self.outs[i].at[origin], i, self.c),
            send_sem=self.ssem.at[6 * i + k], recv_sem=self.rsem.at[6 * i + k],
            device_id=(self.chips[k][0], self.chips[k][1], self.c), device_id_type=MESH)

    def _d2d(self, i, k, h):
        origin = 2 * self.chips[k][0] + self.chips[k][1]
        piece = self._half(self.outs[i].at[origin], i, h)
        return pltpu.make_async_remote_copy(
            src_ref=piece, dst_ref=piece, send_sem=self.ssem.at[6 * i + 3 + k],
            recv_sem=self.rsem.at[6 * i + 3 + k], device_id=self.sibling, device_id_type=MESH)

    def _each(self):
        return [(i, k) for i in range(len(self.srcs)) for k in range(3)]

    def start(self):
        for i, k in self._each():
            self._ici(i, k, self.me).start()

    def forward(self):
        for i, k in self._each():
            self._ici(i, k, 2 * self.chips[k][0] + self.chips[k][1]).wait_recv()
            self._d2d(i, k, self.c).start()

    def finish(self):
        for i, k in self._each():
            self._d2d(i, k, 1 - self.c).wait_recv()
        for i, k in self._each():
            self._ici(i, k, self.me).wait_send()
            self._d2d(i, k, self.c).wait_send()


def _gather_shapes(shards):
    return [jax.ShapeDtypeStruct((N_CHIPS,) + s.shape, s.dtype) for s in shards]


def _gather_weights(shards, lead):
    n = len(shards)

    def body(*refs):
        plan = _GatherPlan(refs[:n], refs[n:2 * n], lead, refs[2 * n], refs[2 * n + 1])
        plan.start()
        plan.forward()
        plan.finish()

    return pl.pallas_call(
        body, name="gather_weights", out_shape=_gather_shapes(shards),
        in_specs=[_hbm()] * n, out_specs=[_hbm()] * n,
        scratch_shapes=[pltpu.SemaphoreType.DMA((6 * n,)), pltpu.SemaphoreType.DMA((6 * n,))],
    )(*shards)


def _in_proj(x2, g1, wa, tm):
    S = x2.shape[0]

    def body(x_ref, g_ref, w_ref, a_ref, uc_ref, qkv_ref):
        x = x_ref[...]
        a = (x * _rms(x) * g_ref[...]).astype(BF16)
        a_ref[...] = a
        u = [_dot(a, w_ref[j]) for j in range(N_CHIPS)]
        uc_ref[:, 0:640] = u[0]
        uc_ref[:, 640:1024] = u[1][:, 0:384]
        qkv_ref[:, 0:256] = u[1][:, 384:640].astype(BF16)
        qkv_ref[:, 256:896] = u[2].astype(BF16)
        qkv_ref[:, 896:1536] = u[3].astype(BF16)

    return pl.pallas_call(
        body, name="in_proj", grid=(S // tm,),
        in_specs=[pl.BlockSpec((tm, D_MODEL), lambda i: (i, 0)), _const_spec((1, D_MODEL)),
                  _const_spec(wa.shape)],
        out_specs=[pl.BlockSpec((tm, D_MODEL), lambda i: (i, 0)),
                   pl.BlockSpec((tm, 2 * CONV_CH), lambda i: (i, 0)),
                   pl.BlockSpec((tm, 1536), lambda i: (i, 0))],
        out_shape=[jax.ShapeDtypeStruct((S, D_MODEL), BF16), jax.ShapeDtypeStruct((S, 2 * CONV_CH), F32),
                   jax.ShapeDtypeStruct((S, 1536), BF16)],
        compiler_params=_cp(("parallel",)),
    )(x2, g1, wa)


SUBLANES = 8


def _shift_copies(src_ref, sh_ref):
    rows = sh_ref.shape[1]
    for b in range(1, SUBLANES):
        sh_ref[b - 1] = src_ref[pl.ds(b, rows), :]


def _rows_at(src_ref, sh_ref, off, rows):
    a, b = divmod(off, SUBLANES)
    if b == 0:
        return src_ref[pl.ds(SUBLANES * a, rows), :]
    return sh_ref[b - 1, pl.ds(SUBLANES * a, rows), :]


def _conv_taps(cw_ref, src_ref, sh_ref, offs, rows):
    acc = None
    for w, off in enumerate(offs):
        term = cw_ref[w:w + 1, :] * _rows_at(src_ref, sh_ref, off, rows)
        acc = term if acc is None else acc + term
    return acc


def _glu(uc):
    return uc[:, :CONV_CH] * _sigmoid(uc[:, CONV_CH:])


def _conv_fwd(uc, cwf, cb, lg, lb, tm):
    S = uc.shape[0]
    hb = tm // HALO

    def body(uc_ref, prev_ref, cw_ref, cb_ref, lg_ref, lb_ref, out_ref, y_ref, glu_ref, sh_ref):
        i = pl.program_id(0)
        glu_ref[0:HALO, :] = jnp.where(i == 0, 0.0, _glu(prev_ref[...]))
        glu_ref[HALO:HALO + tm, :] = _glu(uc_ref[...])
        glu_ref[HALO + tm:HALO + tm + SUBLANES, :] = jnp.zeros((SUBLANES, CONV_CH), F32)
        _shift_copies(glu_ref, sh_ref)
        offs = [HALO - (CONV_WIDTH - 1) + w for w in range(CONV_WIDTH)]
        y = _conv_taps(cw_ref, glu_ref, sh_ref, offs, tm) + cb_ref[...]
        y_ref[...] = y
        mu = jnp.mean(y, axis=-1, keepdims=True)
        yc = y - mu
        rstd = lax.rsqrt(jnp.mean(yc * yc, axis=-1, keepdims=True) + EPS)
        ln = yc * rstd * lg_ref[...] + lb_ref[...]
        out_ref[...] = (ln * _sigmoid(ln)).astype(BF16)

    return pl.pallas_call(
        body, name="conv_fwd", grid=(S // tm,),
        in_specs=[pl.BlockSpec((tm, 2 * CONV_CH), lambda i: (i, 0)),
                  pl.BlockSpec((HALO, 2 * CONV_CH), lambda i: (jnp.maximum(i * hb - 1, 0), 0)),
                  _const_spec(cwf.shape), _const_spec((1, CONV_CH)), _const_spec((1, CONV_CH)),
                  _const_spec((1, CONV_CH))],
        out_specs=[pl.BlockSpec((tm, CONV_CH), lambda i: (i, 0))] * 2,
        out_shape=[jax.ShapeDtypeStruct((S, CONV_CH), BF16), jax.ShapeDtypeStruct((S, CONV_CH), F32)],
        scratch_shapes=[pltpu.VMEM((HALO + tm + SUBLANES, CONV_CH), F32),
                        pltpu.VMEM((SUBLANES - 1, HALO + tm, CONV_CH), F32)],
        compiler_params=_cp(("parallel",)),
    )(uc, uc, cwf, cb, lg, lb)


def _lane_mask(h):
    lane = lax.broadcasted_iota(jnp.int32, (1, PAIR), 1)
    return (lane >= HEAD_DIM * h) & (lane < HEAD_DIM * (h + 1))


def _neg_abs(x):
    bits = lax.bitcast_convert_type(x, jnp.uint32) | jnp.uint32(0x80000000)
    return lax.bitcast_convert_type(bits, F32)


def _tri_dot(x, m):
    return _dot(x.astype(BF16), m)


MASKED = -1e30
KEY_BLOCKS = 4


def _running_sums(x, m, reverse):
    t = m.shape[0]
    order = range(KEY_BLOCKS - 1, -1, -1) if reverse else range(KEY_BLOCKS)
    out = [None] * KEY_BLOCKS
    carry = None
    for b in order:
        xb = x[:, b * t:(b + 1) * t]
        cb = _tri_dot(xb, m)
        out[b] = cb if carry is None else cb + carry
        rs = jnp.sum(xb, axis=1, keepdims=True)
        carry = rs if carry is None else carry + rs
    return jnp.concatenate(out, axis=1), carry


def _sb_tile(z, r, m_suf):
    sp = jnp.maximum(z, 0.0) + jnp.log(1.0 + jnp.exp(_neg_abs(z)))
    c, rs = _running_sums(sp, m_suf, reverse=True)
    ex = z - c
    if r is not None:
        ex = ex - r
    return jnp.exp(ex), sp, rs


def _scores(qm, kt, mask):
    z = _dot_nt(qm, kt)
    return z if mask is None else jnp.where(mask, z, MASKED)


def _causal_mask(i, sb, t):
    row = lax.broadcasted_iota(jnp.int32, (t, KEY_BLOCKS * t), 0) + i * t
    col = lax.broadcasted_iota(jnp.int32, (t, KEY_BLOCKS * t), 1) + sb * (KEY_BLOCKS * t)
    return col < row


def _sweep_plain(first, count, tile):
    def step(n, carry):
        tile(first + n)
        return carry

    lax.fori_loop(0, count + 1, step, 0)


def _sweep(first, count, down, fetch, load, work):
    lo, hi = (first - count, first) if down else (first, first + count)
    tile = lambda j: jnp.clip(first - j if down else first + j, lo, hi)
    fetch(first, 0, True)

    def step(n, carry):
        j = 2 * n
        vals = load(0)
        fetch(tile(j + 1), 1, False)
        work(tile(j), vals)
        vals = load(1)
        fetch(tile(j + 2), 0, False)
        work(tile(j + 1), vals)
        return carry

    lax.fori_loop(0, (count + 1) // 2, step, 0)

    @pl.when(lax.rem(count, 2) == 0)
    def _():
        work(tile(count), load(0))


def _suffix_matrix(t, prefix=False):
    row = lax.broadcasted_iota(jnp.int32, (t, t), 0)
    col = lax.broadcasted_iota(jnp.int32, (t, t), 1)
    return ((row <= col) if prefix else (row >= col)).astype(BF16)


def _attn_fwd(qkv, t, shards, lead):
    S = qkv.shape[0]
    tk = KEY_BLOCKS * t

    ng = len(shards)
    nq = S // t

    def body(*refs):
        q_ref, k_ref, v_ref = refs[:3]
        o_ref = refs[3 + ng]
        acc_ref, r_ref, z_buf, ssem, rsem = refs[4 + 2 * ng:]
        p = pl.program_id(0)
        i = pl.program_id(1)
        plan = _GatherPlan(refs[3:3 + ng], refs[4 + ng:4 + 2 * ng], lead, ssem, rsem)
        pl.when((p == 0) & (i == 0))(plan.start)
        pl.when((p == 1) & (i == 0))(plan.forward)
        last = i // KEY_BLOCKS
        m_suf = _suffix_matrix(t)
        q = q_ref[...]
        hms = [_lane_mask(h) for h in range(2)]
        qms = [jnp.where(hm, q, 0) * 0.125 for hm in hms]
        acc_ref[...] = jnp.zeros_like(acc_ref)
        r_ref[...] = jnp.zeros_like(r_ref)

        def rows(sb):
            return pl.ds(pl.multiple_of(sb * tk, tk), tk)

        def fetch(sb, slot, diagonal):
            kt = k_ref[rows(sb), :]
            mask = _causal_mask(i, sb, t) if diagonal else None
            for h in range(2):
                z_buf[slot, h] = _scores(qms[h], kt, mask)

        def load(slot):
            return [z_buf[slot, h] for h in range(2)]

        def work(sb, zs):
            vt = v_ref[rows(sb), :]
            for h in range(2):
                a_loc, _, rs = _sb_tile(zs[h], None, m_suf)
                r = r_ref[h]
                acc_ref[...] += _dot(a_loc.astype(BF16), jnp.where(hms[h], vt, 0)) * jnp.exp(-r)
                r_ref[h] = r + rs

        _sweep(last, last, True, fetch, load, work)
        o_ref[...] = acc_ref[...]
        pl.when((p == N_PAIRS - 1) & (i == nq - 1))(plan.finish)

    return pl.pallas_call(
        body, name="attn_fwd", grid=(N_PAIRS, nq),
        in_specs=[pl.BlockSpec((t, PAIR), lambda p, i: (i, p)),
                  pl.BlockSpec((S, PAIR), lambda p, i: (0, N_PAIRS + p)),
                  pl.BlockSpec((S, PAIR), lambda p, i: (0, 2 * N_PAIRS + p))] + [_hbm()] * ng,
        out_specs=[pl.BlockSpec((t, PAIR), lambda p, i: (i, p))] + [_hbm()] * ng,
        out_shape=[jax.ShapeDtypeStruct((S, N_PAIRS * PAIR), F32)] + _gather_shapes(shards),
        scratch_shapes=[pltpu.VMEM((t, PAIR), F32), pltpu.VMEM((2, t, 1), F32),
                        pltpu.VMEM((2, 2, t, tk), F32),
                        pltpu.SemaphoreType.DMA((6 * ng,)), pltpu.SemaphoreType.DMA((6 * ng,))],
        compiler_params=_cp(("arbitrary", "arbitrary")),
    )(qkv, qkv, qkv, *shards)


def _out_proj(conv_out, o, ag, wc, x2, g2, g3, tm):
    S = o.shape[0]

    def body(co_ref, o_ref, ag_ref, w_ref, x_ref, g2_ref, g3_ref, mix_ref, y_ref, h1_ref, fin_ref):
        seg = _seg_matrix(CONV_CH)
        o = o_ref[...]
        att = (o * lax.rsqrt(_head_mean(o * o, seg) + EPS) * ag_ref[...]).astype(BF16)
        co = co_ref[...]
        mix_ref[:, :CONV_CH] = co
        mix_ref[:, CONV_CH:] = att
        y = _dot(co, w_ref[0:CONV_CH, :]) + _dot(att, w_ref[CONV_CH:, :])
        y_ref[...] = y
        h1 = x_ref[...] + y * _rms(y) * g2_ref[...]
        h1_ref[...] = h1
        fin_ref[...] = (h1 * _rms(h1) * g3_ref[...]).astype(BF16)

    row = lambda w: pl.BlockSpec((tm, w), lambda i: (i, 0))
    return pl.pallas_call(
        body, name="out_proj", grid=(S // tm,),
        in_specs=[row(CONV_CH), row(CONV_CH), _const_spec((1, CONV_CH)), _const_spec(wc.shape),
                  row(D_MODEL), _const_spec((1, D_MODEL)), _const_spec((1, D_MODEL))],
        out_specs=[row(D_MODEL)] * 4,
        out_shape=[jax.ShapeDtypeStruct((S, D_MODEL), BF16), jax.ShapeDtypeStruct((S, D_MODEL), F32),
                   jax.ShapeDtypeStruct((S, D_MODEL), F32), jax.ShapeDtypeStruct((S, D_MODEL), BF16)],
        compiler_params=_cp(("parallel",)),
    )(conv_out, o, ag, wc, x2, g2, g3)


def _ffn_fwd(f_in, h1, tgt, wg, wu, wd, g4, tm):
    S = f_in.shape[0]

    def body(fin_ref, h1_ref, tgt_ref, wg_ref, wu_ref, wd_ref, g4_ref, df_ref, dh2_ref, dg4_ref, loss_ref):
        i = pl.program_id(0)
        fin = fin_ref[...]
        gt = _dot(fin, wg_ref[...])
        up = _dot(fin, wu_ref[...])
        f = _dot((gt * _sigmoid(gt) * up).astype(BF16), wd_ref[...])
        r = _rms(f)
        n = f * r
        g4 = g4_ref[...]
        err = h1_ref[...] + n * g4 - tgt_ref[...]
        dh2 = err * (1.0 / D_MODEL)
        dh2_ref[...] = dh2
        df, dg = _rms_bwd(dh2, n, r, g4)
        df_ref[...] = df.astype(BF16)

        @pl.when(i == 0)
        def _():
            dg4_ref[...] = jnp.zeros_like(dg4_ref)
            loss_ref[...] = jnp.zeros_like(loss_ref)

        dg4_ref[...] += jnp.sum(dg, axis=0, keepdims=True)
        part = jnp.sum(jnp.sum(err * err, axis=1, keepdims=True), axis=0, keepdims=True)
        loss_ref[...] += part * (0.5 / D_MODEL)

    row = lambda w: pl.BlockSpec((tm, w), lambda i: (i, 0))
    return pl.pallas_call(
        body, name="ffn_fwd", grid=(S // tm,),
        in_specs=[row(D_MODEL), row(D_MODEL), row(D_MODEL), _const_spec(wg.shape), _const_spec(wu.shape),
                  _const_spec(wd.shape), _const_spec((1, D_MODEL))],
        out_specs=[row(D_MODEL), row(D_MODEL), pl.BlockSpec((1, D_MODEL), lambda i: (0, 0)),
                   pl.BlockSpec((1, 128), lambda i: (0, 0))],
        out_shape=[jax.ShapeDtypeStruct((S, D_MODEL), BF16), jax.ShapeDtypeStruct((S, D_MODEL), F32),
                   jax.ShapeDtypeStruct((1, D_MODEL), F32), jax.ShapeDtypeStruct((1, 128), F32)],
        compiler_params=_cp(("arbitrary",)),
    )(f_in, h1, tgt, wg, wu, wd, g4)


def _ffn_bwd(f_in, df, dh2, h1, yv, wg, wu, wd, g3, g2, tm):
    S = f_in.shape[0]

    def body(fin_ref, df_ref, dh2_ref, h1_ref, y_ref, wg_ref, wu_ref, wd_ref, g3_ref, g2_ref,
             act_ref, dgt_ref, dup_ref, dh1_ref, dy_ref, dg3_ref, dg2_ref):
        i = pl.program_id(0)
        fin = fin_ref[...]
        df = df_ref[...]
        gt = _dot(fin, wg_ref[...])
        up = _dot(fin, wu_ref[...])
        sg = _sigmoid(gt)
        silu = gt * sg
        act_ref[...] = (silu * up).astype(BF16)
        dact = _dot_nt(df, wd_ref[...])
        dgt = (dact * up * (sg * (1.0 + gt * (1.0 - sg)))).astype(BF16)
        dup = (dact * silu).astype(BF16)
        dgt_ref[...] = dgt
        dup_ref[...] = dup
        dfin = _dot_nt(dgt, wg_ref[...]) + _dot_nt(dup, wu_ref[...])
        h1 = h1_ref[...]
        r3 = _rms(h1)
        dh1_n, dg3 = _rms_bwd(dfin, h1 * r3, r3, g3_ref[...])
        dh1 = dh2_ref[...] + dh1_n
        dh1_ref[...] = dh1
        y = y_ref[...]
        r2 = _rms(y)
        dy, dg2 = _rms_bwd(dh1, y * r2, r2, g2_ref[...])
        dy_ref[...] = dy.astype(BF16)

        @pl.when(i == 0)
        def _():
            dg3_ref[...] = jnp.zeros_like(dg3_ref)
            dg2_ref[...] = jnp.zeros_like(dg2_ref)

        dg3_ref[...] += jnp.sum(dg3, axis=0, keepdims=True)
        dg2_ref[...] += jnp.sum(dg2, axis=0, keepdims=True)

    row = lambda w: pl.BlockSpec((tm, w), lambda i: (i, 0))
    vec = pl.BlockSpec((1, D_MODEL), lambda i: (0, 0))
    return pl.pallas_call(
        body, name="ffn_bwd", grid=(S // tm,),
        in_specs=[row(D_MODEL)] * 5 + [_const_spec(wg.shape), _const_spec(wu.shape), _const_spec(wd.shape),
                                       _const_spec((1, D_MODEL)), _const_spec((1, D_MODEL))],
        out_specs=[row(D_FF), row(D_FF), row(D_FF), row(D_MODEL), row(D_MODEL), vec, vec],
        out_shape=[jax.ShapeDtypeStruct((S, D_FF), BF16)] * 3
        + [jax.ShapeDtypeStruct((S, D_MODEL), F32), jax.ShapeDtypeStruct((S, D_MODEL), BF16),
           jax.ShapeDtypeStruct((1, D_MODEL), F32), jax.ShapeDtypeStruct((1, D_MODEL), F32)],
        compiler_params=_cp(("arbitrary",)),
    )(f_in, df, dh2, h1, yv, wg, wu, wd, g3, g2)


def _out_bwd(dy, o, ag, wc, tm):
    S = o.shape[0]

    def body(dy_ref, o_ref, ag_ref, w_ref, dco_ref, do_ref, dag_ref):
        i = pl.program_id(0)
        seg = _seg_matrix(CONV_CH)
        dy = dy_ref[...]
        dco_ref[...] = _dot_nt(dy, w_ref[0:CONV_CH, :])
        datt = _dot_nt(dy, w_ref[CONV_CH:, :])
        o = o_ref[...]
        r = lax.rsqrt(_head_mean(o * o, seg) + EPS)
        n = o * r
        dn = datt * ag_ref[...]
        do_ref[...] = (r * (dn - n * _head_mean(dn * n, seg))).astype(BF16)

        @pl.when(i == 0)
        def _():
            dag_ref[...] = jnp.zeros_like(dag_ref)

        dag_ref[...] += jnp.sum(datt * n, axis=0, keepdims=True)

    row = lambda w: pl.BlockSpec((tm, w), lambda i: (i, 0))
    return pl.pallas_call(
        body, name="out_bwd", grid=(S // tm,),
        in_specs=[row(D_MODEL), row(CONV_CH), _const_spec((1, CONV_CH)), _const_spec(wc.shape)],
        out_specs=[row(CONV_CH), row(CONV_CH), pl.BlockSpec((1, CONV_CH), lambda i: (0, 0))],
        out_shape=[jax.ShapeDtypeStruct((S, CONV_CH), F32), jax.ShapeDtypeStruct((S, CONV_CH), BF16),
                   jax.ShapeDtypeStruct((1, CONV_CH), F32)],
        compiler_params=_cp(("arbitrary",)),
    )(dy, o, ag, wc)


def _attn_bwd(qkv, do, t, parts):
    S = qkv.shape[0]
    tk = KEY_BLOCKS * t
    nk = S // tk
    ns = len(parts)

    def body(*refs):
        q_ref, k_ref, v_ref, do_ref = refs[:4]
        dq_ref, dk_hbm, dv_hbm = refs[4 + ns:7 + ns]
        g_buf, s_buf, r_ref, dq_acc, dk_ref, dv_ref, z_buf, da_buf = refs[7 + 2 * ns:15 + 2 * ns]
        p = pl.program_id(0)
        i = pl.program_id(1)
        plan = _ScatterPlan(refs[4:4 + ns], refs[7 + ns:7 + 2 * ns], *refs[15 + 2 * ns:])
        pl.when((p == 0) & (i == 0))(plan.start)
        last = i // KEY_BLOCKS

        @pl.when(i == 0)
        def _():
            dk_ref[...] = jnp.zeros_like(dk_ref)
            dv_ref[...] = jnp.zeros_like(dv_ref)

        m_suf = _suffix_matrix(t)
        m_pre = _suffix_matrix(t, prefix=True)
        q = q_ref[...]
        do = do_ref[...]
        hms = [_lane_mask(h) for h in range(2)]
        qms = [jnp.where(hm, q, 0) * 0.125 for hm in hms]
        doms = [jnp.where(hm, do, 0) for hm in hms]
        dq_acc[...] = jnp.zeros_like(dq_acc)
        r_ref[...] = jnp.zeros_like(r_ref)

        def rows(sb):
            return pl.ds(pl.multiple_of(sb * tk, tk), tk)

        def fetch1(sb, slot, diagonal):
            kt = k_ref[rows(sb), :]
            vt = v_ref[rows(sb), :]
            mask = _causal_mask(i, sb, t) if diagonal else None
            for h in range(2):
                z_buf[slot, h] = _scores(qms[h], kt, mask)
                da_buf[slot, h] = _dot_nt(doms[h], vt)

        def load1(slot):
            return [(z_buf[slot, h], da_buf[slot, h]) for h in range(2)]

        def work1(sb, vals):
            dv = jnp.zeros((tk, PAIR), F32)
            for h in range(2):
                z, da = vals[h]
                A, sp, rs = _sb_tile(z, r_ref[h], m_suf)
                g_buf[h, sb] = A * da
                s_buf[h, sb] = 1.0 - jnp.exp(-sp)
                dv = dv + _dot_tn(A.astype(BF16), doms[h])
                r_ref[h] += rs
            dv_ref[rows(sb), :] += dv

        _sweep(last, last, True, fetch1, load1, work1)
        r_ref[...] = jnp.zeros_like(r_ref)

        def sweep2(sb):
            kt = k_ref[rows(sb), :]
            dk = jnp.zeros((tk, PAIR), F32)
            for h in range(2):
                g = g_buf[h, sb]
                pre, rs = _running_sums(g, m_pre, reverse=False)
                dzb = (g - s_buf[h, sb] * (pre + r_ref[h])).astype(BF16)
                dq_acc[...] += _dot(dzb, jnp.where(hms[h], kt, 0))
                dk = dk + _dot_tn(dzb, qms[h])
                r_ref[h] += rs
            dk_ref[rows(sb), :] += dk

        _sweep_plain(0, last, sweep2)
        dq_ref[...] = dq_acc[...] * 0.125

        @pl.when(i == S // t - 1)
        def _():
            cols = pl.ds(pl.multiple_of(p * PAIR, PAIR), PAIR)
            pltpu.sync_copy(dk_ref, dk_hbm.at[:, cols])
            pltpu.sync_copy(dv_ref, dv_hbm.at[:, cols])

        pl.when((p == N_PAIRS - 1) & (i == S // t - 1))(plan.finish)

    once = lambda cb: pl.BlockSpec((S, PAIR), cb, pipeline_mode=pl.Buffered(1))
    return pl.pallas_call(
        body, name="attn_bwd", grid=(N_PAIRS, S // t),
        in_specs=[pl.BlockSpec((t, PAIR), lambda p, i: (i, p)),
                  once(lambda p, i: (0, N_PAIRS + p)), once(lambda p, i: (0, 2 * N_PAIRS + p)),
                  pl.BlockSpec((t, PAIR), lambda p, i: (i, p))] + [_hbm()] * ns,
        out_specs=[pl.BlockSpec((t, PAIR), lambda p, i: (i, p)), _hbm(), _hbm()] + [_hbm()] * ns,
        out_shape=[jax.ShapeDtypeStruct((S, N_PAIRS * PAIR), F32)] * 3
        + [jax.ShapeDtypeStruct(pt.shape, pt.dtype) for pt in parts],
        scratch_shapes=[pltpu.VMEM((2, nk, t, tk), F32), pltpu.VMEM((2, nk, t, tk), F32),
                        pltpu.VMEM((2, t, 1), F32), pltpu.VMEM((t, PAIR), F32),
                        pltpu.VMEM((S, PAIR), F32), pltpu.VMEM((S, PAIR), F32),
                        pltpu.VMEM((2, 2, t, tk), F32), pltpu.VMEM((2, 2, t, tk), F32)] + _scatter_sems(ns),
        compiler_params=_cp(("arbitrary", "arbitrary"), vmem=VMEM_LIMIT_ATTN_BWD),
    )(qkv, qkv, qkv, do, *parts)


def _conv_bwd(uc, yconv, dco, cwf, lg, lb, tm):
    S = uc.shape[0]
    hb = tm // HALO
    nb = S // tm
    ext = tm + HALO

    def body(uc_ref, prev_ref, y_ref, ynext_ref, dco_ref, dnext_ref, cw_ref, lg_ref, lb_ref,
             duc_ref, dcw_ref, dcb_ref, dlg_ref, dlb_ref, glu_ref, dyc_ref, shg_ref, shd_ref):
        i = pl.program_id(0)
        last = i == nb - 1

        @pl.when(i == 0)
        def _():
            for ref in (dcw_ref, dcb_ref, dlg_ref, dlb_ref):
                ref[...] = jnp.zeros_like(ref)

        uc = uc_ref[...]
        glu_ref[0:HALO, :] = jnp.where(i == 0, 0.0, _glu(prev_ref[...]))
        glu_ref[HALO:ext, :] = _glu(uc)
        glu_ref[ext:ext + SUBLANES, :] = jnp.zeros((SUBLANES, CONV_CH), F32)
        _shift_copies(glu_ref, shg_ref)
        fwd_offs = [HALO - (CONV_WIDTH - 1) + w for w in range(CONV_WIDTH)]
        y = jnp.concatenate([y_ref[...], ynext_ref[...]], axis=0)
        mu = jnp.mean(y, axis=-1, keepdims=True)
        yc = y - mu
        rstd = lax.rsqrt(jnp.mean(yc * yc, axis=-1, keepdims=True) + EPS)
        yhat = yc * rstd
        lg = lg_ref[...]
        ln = yhat * lg + lb_ref[...]
        sg = _sigmoid(ln)
        dout = jnp.concatenate([dco_ref[...], jnp.where(last, 0.0, dnext_ref[...])], axis=0)
        dln = dout * (sg * (1.0 + ln * (1.0 - sg)))
        dyh = dln * lg
        dyc = rstd * (dyh - jnp.mean(dyh, axis=-1, keepdims=True)
                      - yhat * jnp.mean(dyh * yhat, axis=-1, keepdims=True))
        dyc_ref[0:ext, :] = dyc
        dyc_ref[ext:ext + SUBLANES, :] = jnp.zeros((SUBLANES, CONV_CH), F32)
        _shift_copies(dyc_ref, shd_ref)
        dlg_ref[...] += jnp.sum((dln * yhat)[0:tm], axis=0, keepdims=True)
        dlb_ref[...] += jnp.sum(dln[0:tm], axis=0, keepdims=True)
        dcb_ref[...] += jnp.sum(dyc[0:tm], axis=0, keepdims=True)
        dglu = _conv_taps(cw_ref, dyc_ref, shd_ref, [CONV_WIDTH - 1 - w for w in range(CONV_WIDTH)], tm)
        d0 = dyc[0:tm]
        for w, off in enumerate(fwd_offs):
            dcw_ref[w:w + 1, :] += jnp.sum(d0 * _rows_at(glu_ref, shg_ref, off, tm), axis=0, keepdims=True)
        val, gate = uc[:, :CONV_CH], uc[:, CONV_CH:]
        sgate = _sigmoid(gate)
        duc_ref[:, :CONV_CH] = (dglu * sgate).astype(BF16)
        duc_ref[:, CONV_CH:] = (dglu * val * sgate * (1.0 - sgate)).astype(BF16)

    vec = pl.BlockSpec((1, CONV_CH), lambda i: (0, 0))
    nxt = lambda i: (jnp.minimum((i + 1) * hb, S // HALO - 1), 0)
    return pl.pallas_call(
        body, name="conv_bwd", grid=(nb,),
        in_specs=[pl.BlockSpec((tm, 2 * CONV_CH), lambda i: (i, 0)),
                  pl.BlockSpec((HALO, 2 * CONV_CH), lambda i: (jnp.maximum(i * hb - 1, 0), 0)),
                  pl.BlockSpec((tm, CONV_CH), lambda i: (i, 0)), pl.BlockSpec((HALO, CONV_CH), nxt),
                  pl.BlockSpec((tm, CONV_CH), lambda i: (i, 0)), pl.BlockSpec((HALO, CONV_CH), nxt),
                  _const_spec(cwf.shape), _const_spec((1, CONV_CH)), _const_spec((1, CONV_CH))],
        out_specs=[pl.BlockSpec((tm, 2 * CONV_CH), lambda i: (i, 0)),
                   pl.BlockSpec(cwf.shape, lambda i: (0, 0)), vec, vec, vec],
        out_shape=[jax.ShapeDtypeStruct((S, 2 * CONV_CH), BF16), jax.ShapeDtypeStruct(cwf.shape, F32)]
        + [jax.ShapeDtypeStruct((1, CONV_CH), F32)] * 3,
        scratch_shapes=[pltpu.VMEM((ext + SUBLANES, CONV_CH), F32), pltpu.VMEM((ext + SUBLANES, CONV_CH), F32),
                        pltpu.VMEM((SUBLANES - 1, ext, CONV_CH), F32),
                        pltpu.VMEM((SUBLANES - 1, ext, CONV_CH), F32)],
        compiler_params=_cp(("arbitrary",)),
    )(uc, uc, yconv, yconv, dco, dco, cwf, lg, lb)


def _in_bwd(duc, dq, dk, dv, x2, dh1, g1, wa, tm):
    S = x2.shape[0]

    def body(duc_ref, dq_ref, dk_ref, dv_ref, x_ref, dh1_ref, g_ref, w_ref, gx_ref, du_ref, dg_ref):
        i = pl.program_id(0)
        du = jnp.concatenate([duc_ref[...], dq_ref[...].astype(BF16), dk_ref[...].astype(BF16),
                              dv_ref[...].astype(BF16)], axis=1)
        du_ref[...] = du
        da = _dot_nt(du[:, 0:IN_SH], w_ref[0])
        for j in range(1, N_CHIPS):
            da = da + _dot_nt(du[:, IN_SH * j:IN_SH * (j + 1)], w_ref[j])
        x = x_ref[...]
        r = _rms(x)
        dx, dg = _rms_bwd(da, x * r, r, g_ref[...])
        gx_ref[...] = dh1_ref[...] + dx

        @pl.when(i == 0)
        def _():
            dg_ref[...] = jnp.zeros_like(dg_ref)

        dg_ref[...] += jnp.sum(dg, axis=0, keepdims=True)

    row = lambda w: pl.BlockSpec((tm, w), lambda i: (i, 0))
    return pl.pallas_call(
        body, name="in_bwd", grid=(S // tm,),
        in_specs=[row(2 * CONV_CH), row(CONV_CH), row(CONV_CH), row(CONV_CH), row(D_MODEL), row(D_MODEL),
                  _const_spec((1, D_MODEL)), _const_spec(wa.shape)],
        out_specs=[pl.BlockSpec((None, tm, D_MODEL), lambda i: (0, i, 0)), row(2560),
                   pl.BlockSpec((1, D_MODEL), lambda i: (0, 0))],
        out_shape=[jax.ShapeDtypeStruct((1, S, D_MODEL), F32), jax.ShapeDtypeStruct((S, 2560), BF16),
                   jax.ShapeDtypeStruct((1, D_MODEL), F32)],
        compiler_params=_cp(("arbitrary",)),
    )(duc, dq, dk, dv, x2, dh1, g1, wa)


def _matmul_tn(xm, ym, tm, tn, ts, name, by_column_block=False):
    S, M = xm.shape
    N = ym.shape[1]

    def body(x_ref, y_ref, o_ref):
        @pl.when(pl.program_id(2) == 0)
        def _():
            o_ref[...] = jnp.zeros_like(o_ref)

        o_ref[...] += _dot_tn(x_ref[...], y_ref[...])

    if by_column_block:
        out_spec = pl.BlockSpec((None, tm, tn), lambda m, n, s: (n, m, 0))
        out_shape = jax.ShapeDtypeStruct((N // tn, M, tn), F32)
    else:
        out_spec = pl.BlockSpec((tm, tn), lambda m, n, s: (m, n))
        out_shape = jax.ShapeDtypeStruct((M, N), F32)
    return pl.pallas_call(
        body, name=name, grid=(M // tm, N // tn, S // ts),
        in_specs=[pl.BlockSpec((ts, tm), lambda m, n, s: (s, m)), pl.BlockSpec((ts, tn), lambda m, n, s: (s, n))],
        out_specs=out_spec, out_shape=out_shape,
        compiler_params=_cp(("parallel", "parallel", "arbitrary")),
    )(xm, ym)


def _sibling_halves(grads, name):
    n = len(grads)

    def body(*refs):
        ins, outs, ssem, rsem = refs[:n], refs[n:2 * n], refs[2 * n], refs[2 * n + 1]
        x, y, c = lax.axis_index("x"), lax.axis_index("y"), lax.axis_index("c")
        copies = []
        for k in range(n):
            for j in range(N_CHIPS):
                copies.append(pltpu.make_async_remote_copy(
                    src_ref=ins[k].at[j, 1 - c], dst_ref=outs[k].at[j],
                    send_sem=ssem.at[N_CHIPS * k + j], recv_sem=rsem.at[N_CHIPS * k + j],
                    device_id=(x, y, 1 - c), device_id_type=MESH))
        for cp in copies:
            cp.start()
        for cp in copies:
            cp.wait()

    shapes = [jax.ShapeDtypeStruct((g.shape[0],) + g.shape[2:], F32) for g in grads]
    return pl.pallas_call(
        body, name=name, out_shape=shapes,
        in_specs=[_hbm()] * n, out_specs=[_hbm()] * n,
        scratch_shapes=[pltpu.SemaphoreType.DMA((N_CHIPS * n,)), pltpu.SemaphoreType.DMA((N_CHIPS * n,))],
    )(*grads)


def _add_half(c_arr, g, landed, name):
    def body(c_ref, g_ref, l_ref, o_ref):
        o_ref[...] = (g_ref[...] + l_ref[...]).astype(BF16)

    rows, n = g.shape[2], g.shape[3]
    grid = (N_CHIPS,)
    g_spec = pl.BlockSpec((None, None, rows, n), lambda j, c: (j, c[0], 0, 0))
    l_spec = pl.BlockSpec((None, rows, n), lambda j, c: (j, 0, 0))
    return pl.pallas_call(
        body, name=name,
        grid_spec=pltpu.PrefetchScalarGridSpec(num_scalar_prefetch=1, grid=grid, in_specs=[g_spec, l_spec],
                                               out_specs=l_spec),
        out_shape=jax.ShapeDtypeStruct(landed.shape, BF16),
        compiler_params=_cp(("parallel",)),
    )(c_arr, g, landed)


class _ScatterPlan:
    def __init__(self, ins, outs, lsem, ssem, rsem):
        x, y, c = lax.axis_index("x"), lax.axis_index("y"), lax.axis_index("c")
        me = 2 * x + y
        self.copies = []
        for k in range(len(ins)):
            self.copies.append(pltpu.make_async_copy(ins[k].at[me], outs[k].at[me], lsem.at[k]))
            for r, chip in enumerate([(1 - x, y), (x, 1 - y), (1 - x, 1 - y)]):
                self.copies.append(pltpu.make_async_remote_copy(
                    src_ref=ins[k].at[2 * chip[0] + chip[1]], dst_ref=outs[k].at[me],
                    send_sem=ssem.at[3 * k + r], recv_sem=rsem.at[3 * k + r],
                    device_id=(chip[0], chip[1], c), device_id_type=MESH))

    def start(self):
        for cp in self.copies:
            cp.start()

    def finish(self):
        for cp in self.copies:
            cp.wait()


def _scatter_sems(n):
    return [pltpu.SemaphoreType.DMA((n,)), pltpu.SemaphoreType.DMA((3 * n,)), pltpu.SemaphoreType.DMA((3 * n,))]


def _chip_scatter(parts):
    n = len(parts)

    def body(*refs):
        plan = _ScatterPlan(refs[:n], refs[n:2 * n], *refs[2 * n:])
        plan.start()
        plan.finish()

    shapes = [jax.ShapeDtypeStruct(p.shape, p.dtype) for p in parts]
    return pl.pallas_call(
        body, name="grad_chip_scatter", out_shape=shapes,
        in_specs=[_hbm()] * n, out_specs=[_hbm()] * n, scratch_shapes=_scatter_sems(n),
    )(*parts)


def _sum_chips(landed, name):
    _, rows, n = landed.shape
    tr = 256 if rows % 256 == 0 else rows

    def body(a_ref, b_ref, c_ref, d_ref, o_ref):
        f = lambda ref: ref[...].astype(F32)
        o_ref[...] = ((f(a_ref) + f(b_ref)) + f(c_ref)) + f(d_ref)

    specs = [pl.BlockSpec((None, tr, n), functools.partial(lambda i, j: (j, i, 0), j=j)) for j in range(N_CHIPS)]
    return pl.pallas_call(
        body, name=name, grid=(rows // tr,), in_specs=specs,
        out_specs=pl.BlockSpec((tr, n), lambda i: (i, 0)),
        out_shape=jax.ShapeDtypeStruct((rows, n), F32),
        compiler_params=_cp(("parallel",)),
    )(landed, landed, landed, landed)


def _share_halves(halves):
    n = len(halves)

    def body(*refs):
        ins, outs = refs[:n], refs[n:2 * n]
        ssem, rsem = refs[2 * n:]
        x, y, c = lax.axis_index("x"), lax.axis_index("y"), lax.axis_index("c")
        copies = [pltpu.make_async_remote_copy(
            src_ref=ins[k], dst_ref=outs[k], send_sem=ssem.at[k], recv_sem=rsem.at[k],
            device_id=(x, y, 1 - c), device_id_type=MESH) for k in range(n)]
        for cp in copies:
            cp.start()
        for cp in copies:
            cp.wait()

    shapes = [jax.ShapeDtypeStruct(h.shape, F32) for h in halves]
    return pl.pallas_call(
        body, name="grad_share_halves", out_shape=shapes,
        in_specs=[_hbm()] * n, out_specs=[_hbm()] * n,
        scratch_shapes=[pltpu.SemaphoreType.DMA((n,)), pltpu.SemaphoreType.DMA((n,))],
    )(*halves)


def _allreduce_small(packed):
    rows, n = packed.shape

    def body(in_ref, out_ref, land_ref, ssem, rsem):
        x, y, c = lax.axis_index("x"), lax.axis_index("y"), lax.axis_index("c")
        me = 4 * x + 2 * y + c
        land_ref[me] = in_ref[...]
        copies = []
        for r in range(1, 8):
            tx = 1 - x if r & 4 else x
            ty = 1 - y if r & 2 else y
            tc = 1 - c if r & 1 else c
            cp = pltpu.make_async_remote_copy(
                src_ref=in_ref, dst_ref=land_ref.at[me], send_sem=ssem.at[r - 1], recv_sem=rsem.at[r - 1],
                device_id=(tx, ty, tc), device_id_type=MESH)
            cp.start()
            copies.append(cp)
        for cp in copies:
            cp.wait()
        acc = land_ref[0]
        for k in range(1, 8):
            acc = acc + land_ref[k]
        out_ref[...] = acc

    return pl.pallas_call(
        body, name="allreduce_small", out_shape=jax.ShapeDtypeStruct((rows, n), F32),
        in_specs=[pl.BlockSpec(memory_space=pltpu.VMEM)], out_specs=pl.BlockSpec(memory_space=pltpu.VMEM),
        scratch_shapes=[pltpu.VMEM((8, rows, n), F32), pltpu.SemaphoreType.DMA((7,)),
                        pltpu.SemaphoreType.DMA((7,))],
    )(packed)


def _adamw_math(w, g, m, v):
    m = ADAM_B1 * m + (1.0 - ADAM_B1) * g
    v = ADAM_B2 * v + (1.0 - ADAM_B2) * (g * g)
    m_hat = m / (1.0 - ADAM_B1 ** ADAM_STEP)
    v_hat = v / (1.0 - ADAM_B2 ** ADAM_STEP)
    return -ADAM_LR * (m_hat / (jnp.sqrt(v_hat) + ADAM_EPS) + ADAM_WD * w), m, v


def _adamw_halves(c_arr, w, mine, other, m, v, name):
    rows, n = mine.shape
    tr = 256 if rows % 256 == 0 else rows
    nb = rows // tr

    def body(c_ref, w_ref, a_ref, b_ref, m_ref, v_ref, g_ref, d_ref, mo_ref, vo_ref):
        g = jnp.where(pl.program_id(0) == c_ref[0], a_ref[...], b_ref[...])
        g_ref[...] = g
        d_ref[...], mo_ref[...], vo_ref[...] = _adamw_math(w_ref[...], g, m_ref[...], v_ref[...])

    full = pl.BlockSpec((None, tr, n), lambda h, i, c: (0, h * nb + i, 0))
    half = pl.BlockSpec((tr, n), lambda h, i, c: (i, 0))
    return pl.pallas_call(
        body, name=name,
        grid_spec=pltpu.PrefetchScalarGridSpec(num_scalar_prefetch=1, grid=(2, nb),
                                               in_specs=[full, half, half, full, full], out_specs=[full] * 4),
        out_shape=[jax.ShapeDtypeStruct((1, 2 * rows, n), F32)] * 4,
        compiler_params=_cp(("parallel", "parallel")),
    )(c_arr, w, mine, other, m, v)


def _adamw(w, g, m, v, name):
    rows, n = w.shape
    tr = 256 if rows % 256 == 0 else rows

    def body(w_ref, g_ref, m_ref, v_ref, d_ref, mo_ref, vo_ref):
        d_ref[...], mo_ref[...], vo_ref[...] = _adamw_math(w_ref[...], g_ref[...], m_ref[...], v_ref[...])

    spec = pl.BlockSpec((tr, n), lambda i: (i, 0))
    return pl.pallas_call(
        body, name=name, grid=(rows // tr,), in_specs=[spec] * 4, out_specs=[spec] * 3,
        out_shape=[jax.ShapeDtypeStruct((rows, n), F32)] * 3,
        compiler_params=_cp(("parallel",)),
    )(w, g, m, v)


def _rows8(a):
    a = a.reshape(-1, 128)
    return jnp.pad(a, ((0, (-a.shape[0]) % 8), (0, 0)))


def kernel(x, g_pre_mix, w_in, conv_w, conv_b, conv_ln_g, conv_ln_b, attn_norm_g, w_out, g_post_mix, g_pre_ffn, w_gate, w_up, w_down, g_post_ffn, loss_target, m_g_pre_mix, m_w_in, m_conv_w, m_conv_b, m_conv_ln_g, m_conv_ln_b, m_attn_norm_g, m_w_out, m_g_post_mix, m_g_pre_ffn, m_w_gate, m_w_up, m_w_down, m_g_post_ffn, v_g_pre_mix, v_w_in, v_conv_w, v_conv_b, v_conv_ln_g, v_conv_ln_b, v_attn_norm_g, v_w_out, v_g_post_mix, v_g_pre_ffn, v_w_gate, v_w_up, v_w_down, v_g_post_ffn):
    S = x.shape[1]
    tm_big = min(512, S)
    tm_ffn = min(256, S)
    t_att = min(256, S // KEY_BLOCKS)
    chip = 2 * lax.axis_index("x") + lax.axis_index("y")
    core = lax.axis_index("c")
    x2 = x.reshape(S, D_MODEL)
    tgt = loss_target.reshape(S, D_MODEL)
    ag = attn_norm_g.reshape(1, CONV_CH)

    a_sh = w_in[0].astype(BF16)
    b_sh = jnp.stack([w_gate[0], w_up[0]]).astype(BF16)
    c_sh = jnp.concatenate([w_out[0], w_down[0]], axis=0).astype(BF16)
    cw_sh = jnp.pad(conv_w[0, :, 0, :], ((0, 1), (0, 0)))
    own = lambda full, shard: lax.dynamic_update_index_in_dim(full, shard, chip, 0)
    cols = lambda w4: jnp.transpose(w4, (1, 0, 2)).reshape(w4.shape[1], N_CHIPS * w4.shape[2])
    wa4, cw4 = _gather_weights([a_sh, cw_sh], [False, False])
    wa = own(wa4, a_sh)
    cwf = cols(own(cw4, cw_sh))

    a_bf, uc, qkv = _in_proj(x2, g_pre_mix, wa, tm_big)
    conv_out, yconv = _conv_fwd(uc, cwf, conv_b, conv_ln_g, conv_ln_b, tm_big)
    o, wb4, wc4 = _attn_fwd(qkv, t_att, [b_sh, c_sh], [True, False])
    wb4, wc4 = own(wb4, b_sh), own(wc4, c_sh)
    wg, wu = cols(wb4[:, 0]), cols(wb4[:, 1])
    wo = wc4[:, :OUT_SH].reshape(D_MODEL, D_MODEL)
    wd = wc4[:, OUT_SH:].reshape(D_FF, D_MODEL)
    mixed, yv, h1, f_in = _out_proj(conv_out, o, ag, wo, x2, g_post_mix, g_pre_ffn, tm_big)
    df, dh2, dg4, loss_part = _ffn_fwd(f_in, h1, tgt, wg, wu, wd, g_post_ffn, tm_ffn)

    act, dgt, dup, dh1, dy, dg3, dg2 = _ffn_bwd(f_in, df, dh2, h1, yv, wg, wu, wd, g_pre_ffn, g_post_mix, tm_ffn)
    dco, do, dag = _out_bwd(dy, o, ag, wo, tm_big)
    ts = min(512, S)
    gw_out = _matmul_tn(mixed, dy, D_MODEL, D_MODEL, ts, "grad_w_out")
    gw_gate = _matmul_tn(f_in, dgt, D_MODEL, D_FF // 2, ts, "grad_w_gate")
    gw_up = _matmul_tn(f_in, dup, D_MODEL, D_FF // 2, ts, "grad_w_up")
    gw_down = _matmul_tn(act, df, D_FF // 2, D_MODEL, ts, "grad_w_down")

    by_cols = lambda g: jnp.transpose(g.reshape(2, D_MODEL // 2, N_CHIPS, -1), (2, 0, 1, 3))
    by_rows = lambda g: g.reshape(N_CHIPS, 2, g.shape[0] // (2 * N_CHIPS), g.shape[1])
    c_arr = core.reshape(1).astype(jnp.int32)

    def chip_partials(views, nms):
        landed = _sibling_halves(views, "grad_sibling_halves_" + nms[0])
        return [_add_half(c_arr, g, l, "grad_half_" + nm) for g, l, nm in zip(views, landed, nms)]

    early = ["w_gate", "w_up", "w_out", "w_down"]
    parts = chip_partials([by_cols(gw_gate), by_cols(gw_up), by_rows(gw_out), by_rows(gw_down)], early)
    dq, dk, dv, *slots = _attn_bwd(qkv, do, t_att, parts)
    duc, dcw, dcb, dlg, dlb = _conv_bwd(uc, yconv, dco, cwf, conv_ln_g, conv_ln_b, tm_big)
    grad_x, du, dg1 = _in_bwd(duc, dq, dk, dv, x2, dh1, g_pre_mix, wa, tm_big)
    gw_in = _matmul_tn(a_bf, du, D_MODEL, IN_SH, ts, "grad_w_in", by_column_block=True)
    slots += _chip_scatter(chip_partials([gw_in.reshape(N_CHIPS, 2, D_MODEL // 2, IN_SH)], ["w_in"]))
    names = early + ["w_in"]
    halves = [_sum_chips(s, "grad_sum_" + nm) for s, nm in zip(slots, names)]
    others = _share_halves(halves)
    mine = dict(zip(names, halves))
    other = dict(zip(names, others))

    small = [dg1, dcb, dlg, dlb, dag, dg2, dg3, dg4]
    packed = jnp.concatenate([_rows8(s) for s in small] + [_rows8(dcw), _rows8(loss_part)], axis=0)
    red = _allreduce_small(packed)
    sizes = [D_MODEL, CONV_CH, CONV_CH, CONV_CH, CONV_CH, D_MODEL, D_MODEL, D_MODEL]
    g_small = [red[8 * k:8 * k + n // 128].reshape(1, n) for k, n in enumerate(sizes)]
    cw_red = red[64:64 + 128].reshape(HALO, CONV_CH)
    g_cw = lax.dynamic_slice(cw_red, (0, chip * 128), (HALO, 128))
    loss = red[192, 0]

    big = []
    for w, m, v, nm in [(w_in, m_w_in, v_w_in, "w_in"), (w_out, m_w_out, v_w_out, "w_out"),
                        (w_gate, m_w_gate, v_w_gate, "w_gate"), (w_up, m_w_up, v_w_up, "w_up"),
                        (w_down, m_w_down, v_w_down, "w_down")]:
        big.append(_adamw_halves(c_arr, w, mine[nm], other[nm], m, v, "adamw_" + nm))
    sm_w = [g_pre_mix, conv_b, conv_ln_g, conv_ln_b, ag, g_post_mix, g_pre_ffn, g_post_ffn]
    sm_m = [m_g_pre_mix, m_conv_b, m_conv_ln_g, m_conv_ln_b, m_attn_norm_g, m_g_post_mix, m_g_pre_ffn, m_g_post_ffn]
    sm_v = [v_g_pre_mix, v_conv_b, v_conv_ln_g, v_conv_ln_b, v_attn_norm_g, v_g_post_mix, v_g_pre_ffn, v_g_post_ffn]
    pad_cw = lambda a: jnp.pad(a[0, :, 0, :], ((0, 1), (0, 0)))

    def pack(vecs, cw):
        return jnp.concatenate([_rows8(a) for a in vecs] + [cw], axis=0)

    sd, smn, svn = _adamw(pack(sm_w, pad_cw(conv_w)), pack(g_small, g_cw), pack(sm_m, pad_cw(m_conv_w)),
                          pack(sm_v, pad_cw(v_conv_w)), "adamw_small")

    def unpack(p):
        vecs = [p[8 * k:8 * k + n // 128].reshape(1, n) for k, n in enumerate(sizes)]
        return vecs, p[64:64 + CONV_WIDTH].reshape(1, CONV_WIDTH, 1, 128)

    def ordered(vecs, cw, w_in_, w_out_, w_gate_, w_up_, w_down_):
        g1_, cb_, lg_, lb_, ag_, g2_, g3_, g4_ = vecs
        return [g1_, w_in_, cw, cb_, lg_, lb_, ag_.reshape(1, 8, HEAD_DIM), w_out_, g2_, g3_,
                w_gate_, w_up_, w_down_, g4_]

    grads = ordered(g_small, g_cw[:CONV_WIDTH].reshape(1, CONV_WIDTH, 1, 128), *[b[0] for b in big])
    outs = []
    for idx, p in enumerate((sd, smn, svn)):
        vecs, cw = unpack(p)
        outs += ordered(vecs, cw, *[b[idx + 1] for b in big])
    return (loss, grad_x, *grads, *outs)
```

```python
import functools
import math

import jax
import jax.numpy as jnp
from jax import lax
from jax.experimental import pallas as pl
from jax.experimental.pallas import tpu as pltpu

F32 = jnp.float32
BF16 = jnp.bfloat16
MESH = pl.DeviceIdType.MESH

D_MODEL = 1024
CONV_CH = 512
CONV_WIDTH = 31
HEAD_DIM = 64
PAIR = 2 * HEAD_DIM
N_PAIRS = 4
D_FF = 2816
N_CHIPS = 4
IN_SH = 2560 // N_CHIPS
FF_SH = D_FF // N_CHIPS
OUT_SH = D_MODEL // N_CHIPS
C_ROWS = OUT_SH + FF_SH
EPS = 1e-6
HALO = 32

ADAM_LR = 0.001
ADAM_B1 = 0.9
ADAM_B2 = 0.999
ADAM_EPS = 1e-08
ADAM_WD = 0.01
ADAM_STEP = 10

VMEM_LIMIT = 56 * 2 ** 20
VMEM_LIMIT_ATTN_BWD = 60 * 2 ** 20


def _cp(sem=None, vmem=VMEM_LIMIT):
    return pltpu.CompilerParams(dimension_semantics=sem, vmem_limit_bytes=vmem)


def _hbm():
    return pl.BlockSpec(memory_space=pltpu.HBM)


def _const_spec(shape):
    nd = len(shape)
    return pl.BlockSpec(shape, lambda *_: (0,) * nd, pipeline_mode=pl.Buffered(1))


def _dot(a, b):
    return jnp.dot(a, b, preferred_element_type=F32)


def _dot_nt(a, b):
    return lax.dot_general(a, b, (((1,), (1,)), ((), ())), preferred_element_type=F32)


def _dot_tn(a, b):
    return lax.dot_general(a, b, (((0,), (0,)), ((), ())), preferred_element_type=F32)


def _split3(x):
    b0 = x.astype(BF16)
    r1 = x - b0.astype(F32)
    b1 = r1.astype(BF16)
    b2 = (r1 - b1.astype(F32)).astype(BF16)
    return b0, b1, b2


def _split2(x):
    hi = x.astype(BF16)
    lo = (x - hi.astype(F32)).astype(BF16)
    return hi, lo


def _sigmoid(x):
    return 1.0 / (1.0 + jnp.exp(-x))


def _head_mean(x, seg):
    b0, b1, b2 = _split3(x)
    return (_dot(b0, seg) + _dot(b1, seg) + _dot(b2, seg)) * (1.0 / HEAD_DIM)


def _seg_matrix(n):
    r = lax.broadcasted_iota(jnp.int32, (n, n), 0) // HEAD_DIM
    c = lax.broadcasted_iota(jnp.int32, (n, n), 1) // HEAD_DIM
    return (r == c).astype(BF16)


def _rms(x):
    return lax.rsqrt(jnp.mean(x * x, axis=-1, keepdims=True) + EPS)


def _rms_bwd(dy, n, r, g):
    dn = dy * g
    dx = r * (dn - n * jnp.mean(dn * n, axis=-1, keepdims=True))
    return dx, dy * n


class _GatherPlan:
    def __init__(self, srcs, outs, lead, ssem, rsem):
        self.srcs, self.outs, self.lead, self.ssem, self.rsem = srcs, outs, lead, ssem, rsem
        x, y, self.c = lax.axis_index("x"), lax.axis_index("y"), lax.axis_index("c")
        self.me = 2 * x + y
        self.sibling = (x, y, 1 - self.c)
        self.chips = [(1 - x, y), (x, 1 - y), (1 - x, 1 - y)]

    def _half(self, ref, i, h):
        if self.lead[i]:
            return ref.at[h]
        rows = ref.shape[0] // 2
        return ref.at[pl.ds(h * rows, rows)]

    def _ici(self, i, k, origin):
        return pltpu.make_async_remote_copy(
            src_ref=self._half(self.srcs[i], i, self.c), dst_ref=self._half(self.outs[i].at[origin], i, self.c),
            send_sem=self.ssem.at[6 * i + k], recv_sem=self.rsem.at[6 * i + k],
            device_id=(self.chips[k][0], self.chips[k][1], self.c), device_id_type=MESH)

    def _d2d(self, i, k, h):
        origin = 2 * self.chips[k][0] + self.chips[k][1]
        piece = self._half(self.outs[i].at[origin], i, h)
        return pltpu.make_async_remote_copy(
            src_ref=piece, dst_ref=piece, send_sem=self.ssem.at[6 * i + 3 + k],
            recv_sem=self.rsem.at[6 * i + 3 + k], device_id=self.sibling, device_id_type=MESH)

    def _each(self):
        return [(i, k) for i in range(len(self.srcs)) for k in range(3)]

    def start(self):
        for i, k in self._each():
            self._ici(i, k, self.me).start()

    def forward(self):
        for i, k in self._each():
            self._ici(i, k, 2 * self.chips[k][0] + self.chips[k][1]).wait_recv()
            self._d2d(i, k, self.c).start()

    def finish(self):
        for i, k in self._each():
            self._d2d(i, k, 1 - self.c).wait_recv()
        for i, k in self._each():
            self._ici(i, k, self.me).wait_send()
            self._d2d(i, k, self.c).wait_send()


def _gather_shapes(shards):
    return [jax.ShapeDtypeStruct((N_CHIPS,) + s.shape, s.dtype) for s in shards]


def _gather_weights(shards, lead):
    n = len(shards)

    def body(*refs):
        plan = _GatherPlan(refs[:n], refs[n:2 * n], lead, refs[2 * n], refs[2 * n + 1])
        plan.start()
        plan.forward()
        plan.finish()

    return pl.pallas_call(
        body, name="gather_weights", out_shape=_gather_shapes(shards),
        in_specs=[_hbm()] * n, out_specs=[_hbm()] * n,
        scratch_shapes=[pltpu.SemaphoreType.DMA((6 * n,)), pltpu.SemaphoreType.DMA((6 * n,))],
    )(*shards)


def _in_proj(x2, g1, wa, tm):
    S = x2.shape[0]

    def body(x_ref, g_ref, w_ref, a_ref, uc_ref, qkv_ref):
        x = x_ref[...]
        a = (x * _rms(x) * g_ref[...]).astype(BF16)
        a_ref[...] = a
        u = [_dot(a, w_ref[j]) for j in range(N_CHIPS)]
        uc_ref[:, 0:640] = u[0]
        uc_ref[:, 640:1024] = u[1][:, 0:384]
        qkv_ref[:, 0:256] = u[1][:, 384:640].astype(BF16)
        qkv_ref[:, 256:896] = u[2].astype(BF16)
        qkv_ref[:, 896:1536] = u[3].astype(BF16)

    return pl.pallas_call(
        body, name="in_proj", grid=(S // tm,),
        in_specs=[pl.BlockSpec((tm, D_MODEL), lambda i: (i, 0)), _const_spec((1, D_MODEL)),
                  _const_spec(wa.shape)],
        out_specs=[pl.BlockSpec((tm, D_MODEL), lambda i: (i, 0)),
                   pl.BlockSpec((tm, 2 * CONV_CH), lambda i: (i, 0)),
                   pl.BlockSpec((tm, 1536), lambda i: (i, 0))],
        out_shape=[jax.ShapeDtypeStruct((S, D_MODEL), BF16), jax.ShapeDtypeStruct((S, 2 * CONV_CH), F32),
                   jax.ShapeDtypeStruct((S, 1536), BF16)],
        compiler_params=_cp(("parallel",)),
    )(x2, g1, wa)


SUBLANES = 8


def _shift_copies(src_ref, sh_ref):
    rows = sh_ref.shape[1]
    for b in range(1, SUBLANES):
        sh_ref[b - 1] = src_ref[pl.ds(b, rows), :]


def _rows_at(src_ref, sh_ref, off, rows):
    a, b = divmod(off, SUBLANES)
    if b == 0:
        return src_ref[pl.ds(SUBLANES * a, rows), :]
    return sh_ref[b - 1, pl.ds(SUBLANES * a, rows), :]


def _conv_taps(cw_ref, src_ref, sh_ref, offs, rows):
    acc = None
    for w, off in enumerate(offs):
        term = cw_ref[w:w + 1, :] * _rows_at(src_ref, sh_ref, off, rows)
        acc = term if acc is None else acc + term
    return acc


def _glu(uc):
    return uc[:, :CONV_CH] * _sigmoid(uc[:, CONV_CH:])


def _conv_fwd(uc, cwf, cb, lg, lb, tm):
    S = uc.shape[0]
    hb = tm // HALO

    def body(uc_ref, prev_ref, cw_ref, cb_ref, lg_ref, lb_ref, out_ref, y_ref, glu_ref, sh_ref):
        i = pl.program_id(0)
        glu_ref[0:HALO, :] = jnp.where(i == 0, 0.0, _glu(prev_ref[...]))
        glu_ref[HALO:HALO + tm, :] = _glu(uc_ref[...])
        glu_ref[HALO + tm:HALO + tm + SUBLANES, :] = jnp.zeros((SUBLANES, CONV_CH), F32)
        _shift_copies(glu_ref, sh_ref)
        offs = [HALO - (CONV_WIDTH - 1) + w for w in range(CONV_WIDTH)]
        y = _conv_taps(cw_ref, glu_ref, sh_ref, offs, tm) + cb_ref[...]
        y_ref[...] = y
        mu = jnp.mean(y, axis=-1, keepdims=True)
        yc = y - mu
        rstd = lax.rsqrt(jnp.mean(yc * yc, axis=-1, keepdims=True) + EPS)
        ln = yc * rstd * lg_ref[...] + lb_ref[...]
        out_ref[...] = (ln * _sigmoid(ln)).astype(BF16)

    return pl.pallas_call(
        body, name="conv_fwd", grid=(S // tm,),
        in_specs=[pl.BlockSpec((tm, 2 * CONV_CH), lambda i: (i, 0)),
                  pl.BlockSpec((HALO, 2 * CONV_CH), lambda i: (jnp.maximum(i * hb - 1, 0), 0)),
                  _const_spec(cwf.shape), _const_spec((1, CONV_CH)), _const_spec((1, CONV_CH)),
                  _const_spec((1, CONV_CH))],
        out_specs=[pl.BlockSpec((tm, CONV_CH), lambda i: (i, 0))] * 2,
        out_shape=[jax.ShapeDtypeStruct((S, CONV_CH), BF16), jax.ShapeDtypeStruct((S, CONV_CH), F32)],
        scratch_shapes=[pltpu.VMEM((HALO + tm + SUBLANES, CONV_CH), F32),
                        pltpu.VMEM((SUBLANES - 1, HALO + tm, CONV_CH), F32)],
        compiler_params=_cp(("parallel",)),
    )(uc, uc, cwf, cb, lg, lb)


def _lane_mask(h):
    lane = lax.broadcasted_iota(jnp.int32, (1, PAIR), 1)
    return (lane >= HEAD_DIM * h) & (lane < HEAD_DIM * (h + 1))


def _neg_abs(x):
    bits = lax.bitcast_convert_type(x, jnp.uint32) | jnp.uint32(0x80000000)
    return lax.bitcast_convert_type(bits, F32)


def _tri_dot(x, m):
    return _dot(x.astype(BF16), m)


MASKED = -1e30
KEY_BLOCKS = 4


def _running_sums(x, m, reverse):
    t = m.shape[0]
    order = range(KEY_BLOCKS - 1, -1, -1) if reverse else range(KEY_BLOCKS)
    out = [None] * KEY_BLOCKS
    carry = None
    for b in order:
        xb = x[:, b * t:(b + 1) * t]
        cb = _tri_dot(xb, m)
        out[b] = cb if carry is None else cb + carry
        rs = jnp.sum(xb, axis=1, keepdims=True)
        carry = rs if carry is None else carry + rs
    return jnp.concatenate(out, axis=1), carry


def _sb_tile(z, r, m_suf):
    sp = jnp.maximum(z, 0.0) + jnp.log(1.0 + jnp.exp(_neg_abs(z)))
    c, rs = _running_sums(sp, m_suf, reverse=True)
    ex = z - c
    if r is not None:
        ex = ex - r
    return jnp.exp(ex), sp, rs


def _scores(qm, kt, mask):
    z = _dot_nt(qm, kt)
    return z if mask is None else jnp.where(mask, z, MASKED)


def _causal_mask(i, sb, t):
    row = lax.broadcasted_iota(jnp.int32, (t, KEY_BLOCKS * t), 0) + i * t
    col = lax.broadcasted_iota(jnp.int32, (t, KEY_BLOCKS * t), 1) + sb * (KEY_BLOCKS * t)
    return col < row


def _sweep_plain(first, count, tile):
    def step(n, carry):
        tile(first + n)
        return carry

    lax.fori_loop(0, count + 1, step, 0)


def _sweep(first, count, down, fetch, load, work):
    lo, hi = (first - count, first) if down else (first, first + count)
    tile = lambda j: jnp.clip(first - j if down else first + j, lo, hi)
    fetch(first, 0, True)

    def step(n, carry):
        j = 2 * n
        vals = load(0)
        fetch(tile(j + 1), 1, False)
        work(tile(j), vals)
        vals = load(1)
        fetch(tile(j + 2), 0, False)
        work(tile(j + 1), vals)
        return carry

    lax.fori_loop(0, (count + 1) // 2, step, 0)

    @pl.when(lax.rem(count, 2) == 0)
    def _():
        work(tile(count), load(0))


def _suffix_matrix(t, prefix=False):
    row = lax.broadcasted_iota(jnp.int32, (t, t), 0)
    col = lax.broadcasted_iota(jnp.int32, (t, t), 1)
    return ((row <= col) if prefix else (row >= col)).astype(BF16)


def _attn_fwd(qkv, t, shards, lead):
    S = qkv.shape[0]
    tk = KEY_BLOCKS * t

    ng = len(shards)
    nq = S // t

    def body(*refs):
        q_ref, k_ref, v_ref = refs[:3]
        o_ref = refs[3 + ng]
        acc_ref, r_ref, z_buf, ssem, rsem = refs[4 + 2 * ng:]
        p = pl.program_id(0)
        i = pl.program_id(1)
        plan = _GatherPlan(refs[3:3 + ng], refs[4 + ng:4 + 2 * ng], lead, ssem, rsem)
        pl.when((p == 0) & (i == 0))(plan.start)
        pl.when((p == 1) & (i == 0))(plan.forward)
        last = i // KEY_BLOCKS
        m_suf = _suffix_matrix(t)
        q = q_ref[...]
        hms = [_lane_mask(h) for h in range(2)]
        qms = [jnp.where(hm, q, 0) * 0.125 for hm in hms]
        acc_ref[...] = jnp.zeros_like(acc_ref)
        r_ref[...] = jnp.zeros_like(r_ref)

        def rows(sb):
            return pl.ds(pl.multiple_of(sb * tk, tk), tk)

        def fetch(sb, slot, diagonal):
            kt = k_ref[rows(sb), :]
            mask = _causal_mask(i, sb, t) if diagonal else None
            for h in range(2):
                z_buf[slot, h] = _scores(qms[h], kt, mask)

        def load(slot):
            return [z_buf[slot, h] for h in range(2)]

        def work(sb, zs):
            vt = v_ref[rows(sb), :]
            for h in range(2):
                a_loc, _, rs = _sb_tile(zs[h], None, m_suf)
                r = r_ref[h]
                acc_ref[...] += _dot(a_loc.astype(BF16), jnp.where(hms[h], vt, 0)) * jnp.exp(-r)
                r_ref[h] = r + rs

        _sweep(last, last, True, fetch, load, work)
        o_ref[...] = acc_ref[...]
        pl.when((p == N_PAIRS - 1) & (i == nq - 1))(plan.finish)

    return pl.pallas_call(
        body, name="attn_fwd", grid=(N_PAIRS, nq),
        in_specs=[pl.BlockSpec((t, PAIR), lambda p, i: (i, p)),
                  pl.BlockSpec((S, PAIR), lambda p, i: (0, N_PAIRS + p)),
                  pl.BlockSpec((S, PAIR), lambda p, i: (0, 2 * N_PAIRS + p))] + [_hbm()] * ng,
        out_specs=[pl.BlockSpec((t, PAIR), lambda p, i: (i, p))] + [_hbm()] * ng,
        out_shape=[jax.ShapeDtypeStruct((S, N_PAIRS * PAIR), F32)] + _gather_shapes(shards),
        scratch_shapes=[pltpu.VMEM((t, PAIR), F32), pltpu.VMEM((2, t, 1), F32),
                        pltpu.VMEM((2, 2, t, tk), F32),
                        pltpu.SemaphoreType.DMA((6 * ng,)), pltpu.SemaphoreType.DMA((6 * ng,))],
        compiler_params=_cp(("arbitrary", "arbitrary")),
    )(qkv, qkv, qkv, *shards)


def _out_proj(conv_out, o, ag, wc, x2, g2, g3, tm):
    S = o.shape[0]

    def body(co_ref, o_ref, ag_ref, w_ref, x_ref, g2_ref, g3_ref, mix_ref, y_ref, h1_ref, fin_ref):
        seg = _seg_matrix(CONV_CH)
        o = o_ref[...]
        att = (o * lax.rsqrt(_head_mean(o * o, seg) + EPS) * ag_ref[...]).astype(BF16)
        co = co_ref[...]
        mix_ref[:, :CONV_CH] = co
        mix_ref[:, CONV_CH:] = att
        y = _dot(co, w_ref[0:CONV_CH, :]) + _dot(att, w_ref[CONV_CH:, :])
        y_ref[...] = y
        h1 = x_ref[...] + y * _rms(y) * g2_ref[...]
        h1_ref[...] = h1
        fin_ref[...] = (h1 * _rms(h1) * g3_ref[...]).astype(BF16)

    row = lambda w: pl.BlockSpec((tm, w), lambda i: (i, 0))
    return pl.pallas_call(
        body, name="out_proj", grid=(S // tm,),
        in_specs=[row(CONV_CH), row(CONV_CH), _const_spec((1, CONV_CH)), _const_spec(wc.shape),
                  row(D_MODEL), _const_spec((1, D_MODEL)), _const_spec((1, D_MODEL))],
        out_specs=[row(D_MODEL)] * 4,
        out_shape=[jax.ShapeDtypeStruct((S, D_MODEL), BF16), jax.ShapeDtypeStruct((S, D_MODEL), F32),
                   jax.ShapeDtypeStruct((S, D_MODEL), F32), jax.ShapeDtypeStruct((S, D_MODEL), BF16)],
        compiler_params=_cp(("parallel",)),
    )(conv_out, o, ag, wc, x2, g2, g3)


def _ffn_fwd(f_in, h1, tgt, wg, wu, wd, g4, tm):
    S = f_in.shape[0]

    def body(fin_ref, h1_ref, tgt_ref, wg_ref, wu_ref, wd_ref, g4_ref, df_ref, dh2_ref, dg4_ref, loss_ref,
             gt_ref, up_ref, act_ref):
        i = pl.program_id(0)
        fin = fin_ref[...]
        gt = _dot(fin, wg_ref[...])
        up = _dot(fin, wu_ref[...])
        act = (gt * _sigmoid(gt) * up).astype(BF16)
        gt_ref[...] = gt.astype(BF16)
        up_ref[...] = up.astype(BF16)
        act_ref[...] = act
        f = _dot(act, wd_ref[...])
        r = _rms(f)
        n = f * r
        g4 = g4_ref[...]
        err = h1_ref[...] + n * g4 - tgt_ref[...]
        dh2 = err * (1.0 / D_MODEL)
        dh2_ref[...] = dh2
        df, dg = _rms_bwd(dh2, n, r, g4)
        df_ref[...] = df.astype(BF16)

        @pl.when(i == 0)
        def _():
            dg4_ref[...] = jnp.zeros_like(dg4_ref)
            loss_ref[...] = jnp.zeros_like(loss_ref)

        dg4_ref[...] += jnp.sum(dg, axis=0, keepdims=True)
        part = jnp.sum(jnp.sum(err * err, axis=1, keepdims=True), axis=0, keepdims=True)
        loss_ref[...] += part * (0.5 / D_MODEL)

    row = lambda w: pl.BlockSpec((tm, w), lambda i: (i, 0))
    return pl.pallas_call(
        body, name="ffn_fwd", grid=(S // tm,),
        in_specs=[row(D_MODEL), row(D_MODEL), row(D_MODEL), _const_spec(wg.shape), _const_spec(wu.shape),
                  _const_spec(wd.shape), _const_spec((1, D_MODEL))],
        out_specs=[row(D_MODEL), row(D_MODEL), pl.BlockSpec((1, D_MODEL), lambda i: (0, 0)),
                   pl.BlockSpec((1, 128), lambda i: (0, 0)), row(D_FF), row(D_FF), row(D_FF)],
        out_shape=[jax.ShapeDtypeStruct((S, D_MODEL), BF16), jax.ShapeDtypeStruct((S, D_MODEL), F32),
                   jax.ShapeDtypeStruct((1, D_MODEL), F32), jax.ShapeDtypeStruct((1, 128), F32)]
        + [jax.ShapeDtypeStruct((S, D_FF), BF16)] * 3,
        compiler_params=_cp(("arbitrary",)),
    )(f_in, h1, tgt, wg, wu, wd, g4)


def _ffn_bwd(gt_bf, up_bf, df, dh2, h1, yv, wg, wu, wd, g3, g2, tm):
    S = df.shape[0]

    def body(gt_ref, up_ref, df_ref, dh2_ref, h1_ref, y_ref, wg_ref, wu_ref, wd_ref, g3_ref, g2_ref,
             dgt_ref, dup_ref, dh1_ref, dy_ref, dg3_ref, dg2_ref):
        i = pl.program_id(0)
        df = df_ref[...]
        gt = gt_ref[...].astype(F32)
        up = up_ref[...].astype(F32)
        sg = _sigmoid(gt)
        silu = gt * sg
        dact = _dot_nt(df, wd_ref[...])
        dgt = (dact * up * (sg * (1.0 + gt * (1.0 - sg)))).astype(BF16)
        dup = (dact * silu).astype(BF16)
        dgt_ref[...] = dgt
        dup_ref[...] = dup
        dfin = _dot_nt(dgt, wg_ref[...]) + _dot_nt(dup, wu_ref[...])
        h1 = h1_ref[...]
        r3 = _rms(h1)
        dh1_n, dg3 = _rms_bwd(dfin, h1 * r3, r3, g3_ref[...])
        dh1 = dh2_ref[...] + dh1_n
        dh1_ref[...] = dh1
        y = y_ref[...]
        r2 = _rms(y)
        dy, dg2 = _rms_bwd(dh1, y * r2, r2, g2_ref[...])
        dy_ref[...] = dy.astype(BF16)

        @pl.when(i == 0)
        def _():
            dg3_ref[...] = jnp.zeros_like(dg3_ref)
            dg2_ref[...] = jnp.zeros_like(dg2_ref)

        dg3_ref[...] += jnp.sum(dg3, axis=0, keepdims=True)
        dg2_ref[...] += jnp.sum(dg2, axis=0, keepdims=True)

    row = lambda w: pl.BlockSpec((tm, w), lambda i: (i, 0))
    vec = pl.BlockSpec((1, D_MODEL), lambda i: (0, 0))
    return pl.pallas_call(
        body, name="ffn_bwd", grid=(S // tm,),
        in_specs=[row(D_FF), row(D_FF)] + [row(D_MODEL)] * 4
        + [_const_spec(wg.shape), _const_spec(wu.shape), _const_spec(wd.shape),
           _const_spec((1, D_MODEL)), _const_spec((1, D_MODEL))],
        out_specs=[row(D_FF), row(D_FF), row(D_MODEL), row(D_MODEL), vec, vec],
        out_shape=[jax.ShapeDtypeStruct((S, D_FF), BF16)] * 2
        + [jax.ShapeDtypeStruct((S, D_MODEL), F32), jax.ShapeDtypeStruct((S, D_MODEL), BF16),
           jax.ShapeDtypeStruct((1, D_MODEL), F32), jax.ShapeDtypeStruct((1, D_MODEL), F32)],
        compiler_params=_cp(("arbitrary",)),
    )(gt_bf, up_bf, df, dh2, h1, yv, wg, wu, wd, g3, g2)


def _out_bwd(dy, o, ag, wc, tm):
    S = o.shape[0]

    def body(dy_ref, o_ref, ag_ref, w_ref, dco_ref, do_ref, dag_ref):
        i = pl.program_id(0)
        seg = _seg_matrix(CONV_CH)
        dy = dy_ref[...]
        dco_ref[...] = _dot_nt(dy, w_ref[0:CONV_CH, :])
        datt = _dot_nt(dy, w_ref[CONV_CH:, :])
        o = o_ref[...]
        r = lax.rsqrt(_head_mean(o * o, seg) + EPS)
        n = o * r
        dn = datt * ag_ref[...]
        do_ref[...] = (r * (dn - n * _head_mean(dn * n, seg))).astype(BF16)

        @pl.when(i == 0)
        def _():
            dag_ref[...] = jnp.zeros_like(dag_ref)

        dag_ref[...] += jnp.sum(datt * n, axis=0, keepdims=True)

    row = lambda w: pl.BlockSpec((tm, w), lambda i: (i, 0))
    return pl.pallas_call(
        body, name="out_bwd", grid=(S // tm,),
        in_specs=[row(D_MODEL), row(CONV_CH), _const_spec((1, CONV_CH)), _const_spec(wc.shape)],
        out_specs=[row(CONV_CH), row(CONV_CH), pl.BlockSpec((1, CONV_CH), lambda i: (0, 0))],
        out_shape=[jax.ShapeDtypeStruct((S, CONV_CH), F32), jax.ShapeDtypeStruct((S, CONV_CH), BF16),
                   jax.ShapeDtypeStruct((1, CONV_CH), F32)],
        compiler_params=_cp(("arbitrary",)),
    )(dy, o, ag, wc)


def _attn_bwd(qkv, do, t, parts):
    S = qkv.shape[0]
    tk = KEY_BLOCKS * t
    nk = S // tk
    ns = len(parts)

    def body(*refs):
        q_ref, k_ref, v_ref, do_ref = refs[:4]
        dq_ref, dk_hbm, dv_hbm = refs[4 + ns:7 + ns]
        g_buf, s_buf, r_ref, dq_acc, dk_ref, dv_ref, z_buf, da_buf = refs[7 + 2 * ns:15 + 2 * ns]
        p = pl.program_id(0)
        i = pl.program_id(1)
        plan = _ScatterPlan(refs[4:4 + ns], refs[7 + ns:7 + 2 * ns], *refs[15 + 2 * ns:])
        pl.when((p == 0) & (i == 0))(plan.start)
        last = i // KEY_BLOCKS

        @pl.when(i == 0)
        def _():
            dk_ref[...] = jnp.zeros_like(dk_ref)
            dv_ref[...] = jnp.zeros_like(dv_ref)

        m_suf = _suffix_matrix(t)
        m_pre = _suffix_matrix(t, prefix=True)
        q = q_ref[...]
        do = do_ref[...]
        hms = [_lane_mask(h) for h in range(2)]
        qms = [jnp.where(hm, q, 0) * 0.125 for hm in hms]
        doms = [jnp.where(hm, do, 0) for hm in hms]
        dq_acc[...] = jnp.zeros_like(dq_acc)
        r_ref[...] = jnp.zeros_like(r_ref)

        def rows(sb):
            return pl.ds(pl.multiple_of(sb * tk, tk), tk)

        def fetch1(sb, slot, diagonal):
            kt = k_ref[rows(sb), :]
            vt = v_ref[rows(sb), :]
            mask = _causal_mask(i, sb, t) if diagonal else None
            for h in range(2):
                z_buf[slot, h] = _scores(qms[h], kt, mask)
                da_buf[slot, h] = _dot_nt(doms[h], vt)

        def load1(slot):
            return [(z_buf[slot, h], da_buf[slot, h]) for h in range(2)]

        def work1(sb, vals):
            dv = jnp.zeros((tk, PAIR), F32)
            for h in range(2):
                z, da = vals[h]
                A, sp, rs = _sb_tile(z, r_ref[h], m_suf)
                g_buf[h, sb] = A * da
                s_buf[h, sb] = 1.0 - jnp.exp(-sp)
                dv = dv + _dot_tn(A.astype(BF16), doms[h])
                r_ref[h] += rs
            dv_ref[rows(sb), :] += dv

        _sweep(last, last, True, fetch1, load1, work1)
        r_ref[...] = jnp.zeros_like(r_ref)

        def sweep2(sb):
            kt = k_ref[rows(sb), :]
            dk = jnp.zeros((tk, PAIR), F32)
            for h in range(2):
                g = g_buf[h, sb]
                pre, rs = _running_sums(g, m_pre, reverse=False)
                dzb = (g - s_buf[h, sb] * (pre + r_ref[h])).astype(BF16)
                dq_acc[...] += _dot(dzb, jnp.where(hms[h], kt, 0))
                dk = dk + _dot_tn(dzb, qms[h])
                r_ref[h] += rs
            dk_ref[rows(sb), :] += dk

        _sweep_plain(0, last, sweep2)
        dq_ref[...] = dq_acc[...] * 0.125

        @pl.when(i == S // t - 1)
        def _():
            cols = pl.ds(pl.multiple_of(p * PAIR, PAIR), PAIR)
            pltpu.sync_copy(dk_ref, dk_hbm.at[:, cols])
            pltpu.sync_copy(dv_ref, dv_hbm.at[:, cols])

        pl.when((p == N_PAIRS - 1) & (i == S // t - 1))(plan.finish)

    once = lambda cb: pl.BlockSpec((S, PAIR), cb, pipeline_mode=pl.Buffered(1))
    return pl.pallas_call(
        body, name="attn_bwd", grid=(N_PAIRS, S // t),
        in_specs=[pl.BlockSpec((t, PAIR), lambda p, i: (i, p)),
                  once(lambda p, i: (0, N_PAIRS + p)), once(lambda p, i: (0, 2 * N_PAIRS + p)),
                  pl.BlockSpec((t, PAIR), lambda p, i: (i, p))] + [_hbm()] * ns,
        out_specs=[pl.BlockSpec((t, PAIR), lambda p, i: (i, p)), _hbm(), _hbm()] + [_hbm()] * ns,
        out_shape=[jax.ShapeDtypeStruct((S, N_PAIRS * PAIR), F32)] * 3
        + [jax.ShapeDtypeStruct(pt.shape, pt.dtype) for pt in parts],
        scratch_shapes=[pltpu.VMEM((2, nk, t, tk), F32), pltpu.VMEM((2, nk, t, tk), F32),
                        pltpu.VMEM((2, t, 1), F32), pltpu.VMEM((t, PAIR), F32),
                        pltpu.VMEM((S, PAIR), F32), pltpu.VMEM((S, PAIR), F32),
                        pltpu.VMEM((2, 2, t, tk), F32), pltpu.VMEM((2, 2, t, tk), F32)] + _scatter_sems(ns),
        compiler_params=_cp(("arbitrary", "arbitrary"), vmem=VMEM_LIMIT_ATTN_BWD),
    )(qkv, qkv, qkv, do, *parts)


def _conv_bwd(uc, yconv, dco, cwf, lg, lb, tm):
    S = uc.shape[0]
    hb = tm // HALO
    nb = S // tm
    ext = tm + HALO

    def body(uc_ref, prev_ref, y_ref, ynext_ref, dco_ref, dnext_ref, cw_ref, lg_ref, lb_ref,
             duc_ref, dcw_ref, dcb_ref, dlg_ref, dlb_ref, glu_ref, dyc_ref, shg_ref, shd_ref):
        i = pl.program_id(0)
        last = i == nb - 1

        @pl.when(i == 0)
        def _():
            for ref in (dcw_ref, dcb_ref, dlg_ref, dlb_ref):
                ref[...] = jnp.zeros_like(ref)

        uc = uc_ref[...]
        glu_ref[0:HALO, :] = jnp.where(i == 0, 0.0, _glu(prev_ref[...]))
        glu_ref[HALO:ext, :] = _glu(uc)
        glu_ref[ext:ext + SUBLANES, :] = jnp.zeros((SUBLANES, CONV_CH), F32)
        _shift_copies(glu_ref, shg_ref)
        fwd_offs = [HALO - (CONV_WIDTH - 1) + w for w in range(CONV_WIDTH)]
        y = jnp.concatenate([y_ref[...], ynext_ref[...]], axis=0)
        mu = jnp.mean(y, axis=-1, keepdims=True)
        yc = y - mu
        rstd = lax.rsqrt(jnp.mean(yc * yc, axis=-1, keepdims=True) + EPS)
        yhat = yc * rstd
        lg = lg_ref[...]
        ln = yhat * lg + lb_ref[...]
        sg = _sigmoid(ln)
        dout = jnp.concatenate([dco_ref[...], jnp.where(last, 0.0, dnext_ref[...])], axis=0)
        dln = dout * (sg * (1.0 + ln * (1.0 - sg)))
        dyh = dln * lg
        dyc = rstd * (dyh - jnp.mean(dyh, axis=-1, keepdims=True)
                      - yhat * jnp.mean(dyh * yhat, axis=-1, keepdims=True))
        dyc_ref[0:ext, :] = dyc
        dyc_ref[ext:ext + SUBLANES, :] = jnp.zeros((SUBLANES, CONV_CH), F32)
        _shift_copies(dyc_ref, shd_ref)
        dlg_ref[...] += jnp.sum((dln * yhat)[0:tm], axis=0, keepdims=True)
        dlb_ref[...] += jnp.sum(dln[0:tm], axis=0, keepdims=True)
        dcb_ref[...] += jnp.sum(dyc[0:tm], axis=0, keepdims=True)
        dglu = _conv_taps(cw_ref, dyc_ref, shd_ref, [CONV_WIDTH - 1 - w for w in range(CONV_WIDTH)], tm)
        d0 = dyc[0:tm]
        for w, off in enumerate(fwd_offs):
            dcw_ref[w:w + 1, :] += jnp.sum(d0 * _rows_at(glu_ref, shg_ref, off, tm), axis=0, keepdims=True)
        val, gate = uc[:, :CONV_CH], uc[:, CONV_CH:]
        sgate = _sigmoid(gate)
        duc_ref[:, :CONV_CH] = (dglu * sgate).astype(BF16)
        duc_ref[:, CONV_CH:] = (dglu * val * sgate * (1.0 - sgate)).astype(BF16)

    vec = pl.BlockSpec((1, CONV_CH), lambda i: (0, 0))
    nxt = lambda i: (jnp.minimum((i + 1) * hb, S // HALO - 1), 0)
    return pl.pallas_call(
        body, name="conv_bwd", grid=(nb,),
        in_specs=[pl.BlockSpec((tm, 2 * CONV_CH), lambda i: (i, 0)),
                  pl.BlockSpec((HALO, 2 * CONV_CH), lambda i: (jnp.maximum(i * hb - 1, 0), 0)),
                  pl.BlockSpec((tm, CONV_CH), lambda i: (i, 0)), pl.BlockSpec((HALO, CONV_CH), nxt),
                  pl.BlockSpec((tm, CONV_CH), lambda i: (i, 0)), pl.BlockSpec((HALO, CONV_CH), nxt),
                  _const_spec(cwf.shape), _const_spec((1, CONV_CH)), _const_spec((1, CONV_CH))],
        out_specs=[pl.BlockSpec((tm, 2 * CONV_CH), lambda i: (i, 0)),
                   pl.BlockSpec(cwf.shape, lambda i: (0, 0)), vec, vec, vec],
        out_shape=[jax.ShapeDtypeStruct((S, 2 * CONV_CH), BF16), jax.ShapeDtypeStruct(cwf.shape, F32)]
        + [jax.ShapeDtypeStruct((1, CONV_CH), F32)] * 3,
        scratch_shapes=[pltpu.VMEM((ext + SUBLANES, CONV_CH), F32), pltpu.VMEM((ext + SUBLANES, CONV_CH), F32),
                        pltpu.VMEM((SUBLANES - 1, ext, CONV_CH), F32),
                        pltpu.VMEM((SUBLANES - 1, ext, CONV_CH), F32)],
        compiler_params=_cp(("arbitrary",)),
    )(uc, uc, yconv, yconv, dco, dco, cwf, lg, lb)


def _in_bwd(duc, dq, dk, dv, x2, dh1, g1, wa, tm):
    S = x2.shape[0]

    def body(duc_ref, dq_ref, dk_ref, dv_ref, x_ref, dh1_ref, g_ref, w_ref, gx_ref, du_ref, dg_ref):
        i = pl.program_id(0)
        du = jnp.concatenate([duc_ref[...], dq_ref[...].astype(BF16), dk_ref[...].astype(BF16),
                              dv_ref[...].astype(BF16)], axis=1)
        du_ref[...] = du
        da = _dot_nt(du[:, 0:IN_SH], w_ref[0])
        for j in range(1, N_CHIPS):
            da = da + _dot_nt(du[:, IN_SH * j:IN_SH * (j + 1)], w_ref[j])
        x = x_ref[...]
        r = _rms(x)
        dx, dg = _rms_bwd(da, x * r, r, g_ref[...])
        gx_ref[...] = dh1_ref[...] + dx

        @pl.when(i == 0)
        def _():
            dg_ref[...] = jnp.zeros_like(dg_ref)

        dg_ref[...] += jnp.sum(dg, axis=0, keepdims=True)

    row = lambda w: pl.BlockSpec((tm, w), lambda i: (i, 0))
    return pl.pallas_call(
        body, name="in_bwd", grid=(S // tm,),
        in_specs=[row(2 * CONV_CH), row(CONV_CH), row(CONV_CH), row(CONV_CH), row(D_MODEL), row(D_MODEL),
                  _const_spec((1, D_MODEL)), _const_spec(wa.shape)],
        out_specs=[pl.BlockSpec((None, tm, D_MODEL), lambda i: (0, i, 0)), row(2560),
                   pl.BlockSpec((1, D_MODEL), lambda i: (0, 0))],
        out_shape=[jax.ShapeDtypeStruct((1, S, D_MODEL), F32), jax.ShapeDtypeStruct((S, 2560), BF16),
                   jax.ShapeDtypeStruct((1, D_MODEL), F32)],
        compiler_params=_cp(("arbitrary",)),
    )(duc, dq, dk, dv, x2, dh1, g1, wa)


def _matmul_tn(xm, ym, tm, ts, name, column_block=None):
    S, M = xm.shape
    N = ym.shape[1]

    def body(x_ref, y_ref, o_ref):
        @pl.when(pl.program_id(1) == 0)
        def _():
            o_ref[...] = jnp.zeros_like(o_ref)

        xt = x_ref[...].T
        if column_block is None:
            o_ref[...] += _dot(xt, y_ref[...])
        else:
            for j in range(N // column_block):
                o_ref[j] += _dot(xt, y_ref[:, column_block * j:column_block * (j + 1)])

    if column_block is None:
        out_spec = pl.BlockSpec((tm, N), lambda m, s: (m, 0))
        out_shape = jax.ShapeDtypeStruct((M, N), F32)
    else:
        out_spec = pl.BlockSpec((N // column_block, tm, column_block), lambda m, s: (0, m, 0))
        out_shape = jax.ShapeDtypeStruct((N // column_block, M, column_block), F32)
    return pl.pallas_call(
        body, name=name, grid=(M // tm, S // ts),
        in_specs=[pl.BlockSpec((ts, tm), lambda m, s: (s, m)), pl.BlockSpec((ts, N), lambda m, s: (s, 0))],
        out_specs=out_spec, out_shape=out_shape,
        compiler_params=_cp(("parallel", "arbitrary")),
    )(xm, ym)


def _sibling_halves(grads, name):
    n = len(grads)

    def body(*refs):
        ins, outs, ssem, rsem = refs[:n], refs[n:2 * n], refs[2 * n], refs[2 * n + 1]
        x, y, c = lax.axis_index("x"), lax.axis_index("y"), lax.axis_index("c")
        copies = []
        for k in range(n):
            for j in range(N_CHIPS):
                copies.append(pltpu.make_async_remote_copy(
                    src_ref=ins[k].at[j, 1 - c], dst_ref=outs[k].at[j],
                    send_sem=ssem.at[N_CHIPS * k + j], recv_sem=rsem.at[N_CHIPS * k + j],
                    device_id=(x, y, 1 - c), device_id_type=MESH))
        for cp in copies:
            cp.start()
        for cp in copies:
            cp.wait()

    shapes = [jax.ShapeDtypeStruct((g.shape[0],) + g.shape[2:], F32) for g in grads]
    return pl.pallas_call(
        body, name=name, out_shape=shapes,
        in_specs=[_hbm()] * n, out_specs=[_hbm()] * n,
        scratch_shapes=[pltpu.SemaphoreType.DMA((N_CHIPS * n,)), pltpu.SemaphoreType.DMA((N_CHIPS * n,))],
    )(*grads)


def _add_half(c_arr, g, landed, name):
    def body(c_ref, g_ref, l_ref, o_ref):
        o_ref[...] = (g_ref[...] + l_ref[...]).astype(BF16)

    rows, n = g.shape[2], g.shape[3]
    grid = (N_CHIPS,)
    g_spec = pl.BlockSpec((None, None, rows, n), lambda j, c: (j, c[0], 0, 0))
    l_spec = pl.BlockSpec((None, rows, n), lambda j, c: (j, 0, 0))
    return pl.pallas_call(
        body, name=name,
        grid_spec=pltpu.PrefetchScalarGridSpec(num_scalar_prefetch=1, grid=grid, in_specs=[g_spec, l_spec],
                                               out_specs=l_spec),
        out_shape=jax.ShapeDtypeStruct(landed.shape, BF16),
        compiler_params=_cp(("parallel",)),
    )(c_arr, g, landed)


class _ScatterPlan:
    def __init__(self, ins, outs, lsem, ssem, rsem):
        x, y, c = lax.axis_index("x"), lax.axis_index("y"), lax.axis_index("c")
        me = 2 * x + y
        self.copies = []
        for k in range(len(ins)):
            self.copies.append(pltpu.make_async_copy(ins[k].at[me], outs[k].at[me], lsem.at[k]))
            for r, chip in enumerate([(1 - x, y), (x, 1 - y), (1 - x, 1 - y)]):
                self.copies.append(pltpu.make_async_remote_copy(
                    src_ref=ins[k].at[2 * chip[0] + chip[1]], dst_ref=outs[k].at[me],
                    send_sem=ssem.at[3 * k + r], recv_sem=rsem.at[3 * k + r],
                    device_id=(chip[0], chip[1], c), device_id_type=MESH))

    def start(self):
        for cp in self.copies:
            cp.start()

    def finish(self):
        for cp in self.copies:
            cp.wait()


def _scatter_sems(n):
    return [pltpu.SemaphoreType.DMA((n,)), pltpu.SemaphoreType.DMA((3 * n,)), pltpu.SemaphoreType.DMA((3 * n,))]


def _chip_scatter(parts):
    n = len(parts)

    def body(*refs):
        plan = _ScatterPlan(refs[:n], refs[n:2 * n], *refs[2 * n:])
        plan.start()
        plan.finish()

    shapes = [jax.ShapeDtypeStruct(p.shape, p.dtype) for p in parts]
    return pl.pallas_call(
        body, name="grad_chip_scatter", out_shape=shapes,
        in_specs=[_hbm()] * n, out_specs=[_hbm()] * n, scratch_shapes=_scatter_sems(n),
    )(*parts)


def _sum_chips(landed, name):
    _, rows, n = landed.shape
    tr = 256 if rows % 256 == 0 else rows

    def body(a_ref, b_ref, c_ref, d_ref, o_ref):
        f = lambda ref: ref[...].astype(F32)
        o_ref[...] = ((f(a_ref) + f(b_ref)) + f(c_ref)) + f(d_ref)

    specs = [pl.BlockSpec((None, tr, n), functools.partial(lambda i, j: (j, i, 0), j=j)) for j in range(N_CHIPS)]
    return pl.pallas_call(
        body, name=name, grid=(rows // tr,), in_specs=specs,
        out_specs=pl.BlockSpec((tr, n), lambda i: (i, 0)),
        out_shape=jax.ShapeDtypeStruct((rows, n), F32),
        compiler_params=_cp(("parallel",)),
    )(landed, landed, landed, landed)


def _share_halves(halves):
    n = len(halves)

    def body(*refs):
        ins, outs = refs[:n], refs[n:2 * n]
        ssem, rsem = refs[2 * n:]
        x, y, c = lax.axis_index("x"), lax.axis_index("y"), lax.axis_index("c")
        copies = [pltpu.make_async_remote_copy(
            src_ref=ins[k], dst_ref=outs[k], send_sem=ssem.at[k], recv_sem=rsem.at[k],
            device_id=(x, y, 1 - c), device_id_type=MESH) for k in range(n)]
        for cp in copies:
            cp.start()
        for cp in copies:
            cp.wait()

    shapes = [jax.ShapeDtypeStruct(h.shape, F32) for h in halves]
    return pl.pallas_call(
        body, name="grad_share_halves", out_shape=shapes,
        in_specs=[_hbm()] * n, out_specs=[_hbm()] * n,
        scratch_shapes=[pltpu.SemaphoreType.DMA((n,)), pltpu.SemaphoreType.DMA((n,))],
    )(*halves)


def _allreduce_small(packed):
    rows, n = packed.shape

    def body(in_ref, out_ref, land_ref, ssem, rsem):
        x, y, c = lax.axis_index("x"), lax.axis_index("y"), lax.axis_index("c")
        me = 4 * x + 2 * y + c
        land_ref[me] = in_ref[...]
        copies = []
        for r in range(1, 8):
            tx = 1 - x if r & 4 else x
            ty = 1 - y if r & 2 else y
            tc = 1 - c if r & 1 else c
            cp = pltpu.make_async_remote_copy(
                src_ref=in_ref, dst_ref=land_ref.at[me], send_sem=ssem.at[r - 1], recv_sem=rsem.at[r - 1],
                device_id=(tx, ty, tc), device_id_type=MESH)
            cp.start()
            copies.append(cp)
        for cp in copies:
            cp.wait()
        acc = land_ref[0]
        for k in range(1, 8):
            acc = acc + land_ref[k]
        out_ref[...] = acc

    return pl.pallas_call(
        body, name="allreduce_small", out_shape=jax.ShapeDtypeStruct((rows, n), F32),
        in_specs=[pl.BlockSpec(memory_space=pltpu.VMEM)], out_specs=pl.BlockSpec(memory_space=pltpu.VMEM),
        scratch_shapes=[pltpu.VMEM((8, rows, n), F32), pltpu.SemaphoreType.DMA((7,)),
                        pltpu.SemaphoreType.DMA((7,))],
    )(packed)


def _adamw_math(w, g, m, v):
    m = ADAM_B1 * m + (1.0 - ADAM_B1) * g
    v = ADAM_B2 * v + (1.0 - ADAM_B2) * (g * g)
    m_hat = m / (1.0 - ADAM_B1 ** ADAM_STEP)
    v_hat = v / (1.0 - ADAM_B2 ** ADAM_STEP)
    return -ADAM_LR * (m_hat / (jnp.sqrt(v_hat) + ADAM_EPS) + ADAM_WD * w), m, v


def _adamw_halves(c_arr, w, mine, other, m, v, name):
    rows, n = mine.shape
    tr = 256 if rows % 256 == 0 else rows
    nb = rows // tr

    def body(c_ref, w_ref, a_ref, b_ref, m_ref, v_ref, g_ref, d_ref, mo_ref, vo_ref):
        g = jnp.where(pl.program_id(0) == c_ref[0], a_ref[...], b_ref[...])
        g_ref[...] = g
        d_ref[...], mo_ref[...], vo_ref[...] = _adamw_math(w_ref[...], g, m_ref[...], v_ref[...])

    full = pl.BlockSpec((None, tr, n), lambda h, i, c: (0, h * nb + i, 0))
    half = pl.BlockSpec((tr, n), lambda h, i, c: (i, 0))
    return pl.pallas_call(
        body, name=name,
        grid_spec=pltpu.PrefetchScalarGridSpec(num_scalar_prefetch=1, grid=(2, nb),
                                               in_specs=[full, half, half, full, full], out_specs=[full] * 4),
        out_shape=[jax.ShapeDtypeStruct((1, 2 * rows, n), F32)] * 4,
        compiler_params=_cp(("parallel", "parallel")),
    )(c_arr, w, mine, other, m, v)


def _adamw(w, g, m, v, name):
    rows, n = w.shape
    tr = 256 if rows % 256 == 0 else rows

    def body(w_ref, g_ref, m_ref, v_ref, d_ref, mo_ref, vo_ref):
        d_ref[...], mo_ref[...], vo_ref[...] = _adamw_math(w_ref[...], g_ref[...], m_ref[...], v_ref[...])

    spec = pl.BlockSpec((tr, n), lambda i: (i, 0))
    return pl.pallas_call(
        body, name=name, grid=(rows // tr,), in_specs=[spec] * 4, out_specs=[spec] * 3,
        out_shape=[jax.ShapeDtypeStruct((rows, n), F32)] * 3,
        compiler_params=_cp(("parallel",)),
    )(w, g, m, v)


def _rows8(a):
    a = a.reshape(-1, 128)
    return jnp.pad(a, ((0, (-a.shape[0]) % 8), (0, 0)))


def kernel(x, g_pre_mix, w_in, conv_w, conv_b, conv_ln_g, conv_ln_b, attn_norm_g, w_out, g_post_mix, g_pre_ffn, w_gate, w_up, w_down, g_post_ffn, loss_target, m_g_pre_mix, m_w_in, m_conv_w, m_conv_b, m_conv_ln_g, m_conv_ln_b, m_attn_norm_g, m_w_out, m_g_post_mix, m_g_pre_ffn, m_w_gate, m_w_up, m_w_down, m_g_post_ffn, v_g_pre_mix, v_w_in, v_conv_w, v_conv_b, v_conv_ln_g, v_conv_ln_b, v_attn_norm_g, v_w_out, v_g_post_mix, v_g_pre_ffn, v_w_gate, v_w_up, v_w_down, v_g_post_ffn):
    S = x.shape[1]
    tm_big = min(512, S)
    tm_ffn = min(256, S)
    t_att = min(256, S // KEY_BLOCKS)
    chip = 2 * lax.axis_index("x") + lax.axis_index("y")
    core = lax.axis_index("c")
    x2 = x.reshape(S, D_MODEL)
    tgt = loss_target.reshape(S, D_MODEL)
    ag = attn_norm_g.reshape(1, CONV_CH)

    a_sh = w_in[0].astype(BF16)
    b_sh = jnp.stack([w_gate[0], w_up[0]]).astype(BF16)
    c_sh = jnp.concatenate([w_out[0], w_down[0]], axis=0).astype(BF16)
    cw_sh = jnp.pad(conv_w[0, :, 0, :], ((0, 1), (0, 0)))
    own = lambda full, shard: lax.dynamic_update_index_in_dim(full, shard, chip, 0)
    cols = lambda w4: jnp.transpose(w4, (1, 0, 2)).reshape(w4.shape[1], N_CHIPS * w4.shape[2])
    wa4, cw4 = _gather_weights([a_sh, cw_sh], [False, False])
    wa = own(wa4, a_sh)
    cwf = cols(own(cw4, cw_sh))

    a_bf, uc, qkv = _in_proj(x2, g_pre_mix, wa, tm_big)
    conv_out, yconv = _conv_fwd(uc, cwf, conv_b, conv_ln_g, conv_ln_b, tm_big)
    o, wb4, wc4 = _attn_fwd(qkv, t_att, [b_sh, c_sh], [True, False])
    wb4, wc4 = own(wb4, b_sh), own(wc4, c_sh)
    wg, wu = cols(wb4[:, 0]), cols(wb4[:, 1])
    wo = wc4[:, :OUT_SH].reshape(D_MODEL, D_MODEL)
    wd = wc4[:, OUT_SH:].reshape(D_FF, D_MODEL)
    mixed, yv, h1, f_in = _out_proj(conv_out, o, ag, wo, x2, g_post_mix, g_pre_ffn, tm_big)
    df, dh2, dg4, loss_part, gt_bf, up_bf, act = _ffn_fwd(f_in, h1, tgt, wg, wu, wd, g_post_ffn, tm_ffn)

    dgt, dup, dh1, dy, dg3, dg2 = _ffn_bwd(gt_bf, up_bf, df, dh2, h1, yv, wg, wu, wd, g_pre_ffn, g_post_mix, tm_ffn)
    dco, do, dag = _out_bwd(dy, o, ag, wo, tm_big)
    ts = min(512, S)
    gw_out = _matmul_tn(mixed, dy, D_MODEL, ts, "grad_w_out")
    gw_gate = _matmul_tn(f_in, dgt, D_MODEL, ts, "grad_w_gate")
    gw_up = _matmul_tn(f_in, dup, D_MODEL, ts, "grad_w_up")
    gw_down = _matmul_tn(act, df, D_FF // 2, ts, "grad_w_down")

    by_cols = lambda g: jnp.transpose(g.reshape(2, D_MODEL // 2, N_CHIPS, -1), (2, 0, 1, 3))
    by_rows = lambda g: g.reshape(N_CHIPS, 2, g.shape[0] // (2 * N_CHIPS), g.shape[1])
    c_arr = core.reshape(1).astype(jnp.int32)

    def chip_partials(views, nms):
        landed = _sibling_halves(views, "grad_sibling_halves_" + nms[0])
        return [_add_half(c_arr, g, l, "grad_half_" + nm) for g, l, nm in zip(views, landed, nms)]

    early = ["w_gate", "w_up", "w_out", "w_down"]
    parts = chip_partials([by_cols(gw_gate), by_cols(gw_up), by_rows(gw_out), by_rows(gw_down)], early)
    dq, dk, dv, *slots = _attn_bwd(qkv, do, t_att, parts)
    duc, dcw, dcb, dlg, dlb = _conv_bwd(uc, yconv, dco, cwf, conv_ln_g, conv_ln_b, tm_big)
    grad_x, du, dg1 = _in_bwd(duc, dq, dk, dv, x2, dh1, g_pre_mix, wa, tm_big)
    gw_in = _matmul_tn(a_bf, du, D_MODEL, ts, "grad_w_in", column_block=IN_SH)
    slots += _chip_scatter(chip_partials([gw_in.reshape(N_CHIPS, 2, D_MODEL // 2, IN_SH)], ["w_in"]))
    names = early + ["w_in"]
    halves = [_sum_chips(s, "grad_sum_" + nm) for s, nm in zip(slots, names)]
    others = _share_halves(halves)
    mine = dict(zip(names, halves))
    other = dict(zip(names, others))

    small = [dg1, dcb, dlg, dlb, dag, dg2, dg3, dg4]
    packed = jnp.concatenate([_rows8(s) for s in small] + [_rows8(dcw), _rows8(loss_part)], axis=0)
    red = _allreduce_small(packed)
    sizes = [D_MODEL, CONV_CH, CONV_CH, CONV_CH, CONV_CH, D_MODEL, D_MODEL, D_MODEL]
    g_small = [red[8 * k:8 * k + n // 128].reshape(1, n) for k, n in enumerate(sizes)]
    cw_red = red[64:64 + 128].reshape(HALO, CONV_CH)
    g_cw = lax.dynamic_slice(cw_red, (0, chip * 128), (HALO, 128))
    loss = red[192, 0]

    big = []
    for w, m, v, nm in [(w_in, m_w_in, v_w_in, "w_in"), (w_out, m_w_out, v_w_out, "w_out"),
                        (w_gate, m_w_gate, v_w_gate, "w_gate"), (w_up, m_w_up, v_w_up, "w_up"),
                        (w_down, m_w_down, v_w_down, "w_down")]:
        big.append(_adamw_halves(c_arr, w, mine[nm], other[nm], m, v, "adamw_" + nm))
    sm_w = [g_pre_mix, conv_b, conv_ln_g, conv_ln_b, ag, g_post_mix, g_pre_ffn, g_post_ffn]
    sm_m = [m_g_pre_mix, m_conv_b, m_conv_ln_g, m_conv_ln_b, m_attn_norm_g, m_g_post_mix, m_g_pre_ffn, m_g_post_ffn]
    sm_v = [v_g_pre_mix, v_conv_b, v_conv_ln_g, v_conv_ln_b, v_attn_norm_g, v_g_post_mix, v_g_pre_ffn, v_g_post_ffn]
    pad_cw = lambda a: jnp.pad(a[0, :, 0, :], ((0, 1), (0, 0)))

    def pack(vecs, cw):
        return jnp.concatenate([_rows8(a) for a in vecs] + [cw], axis=0)

    sd, smn, svn = _adamw(pack(sm_w, pad_cw(conv_w)), pack(g_small, g_cw), pack(sm_m, pad_cw(m_conv_w)),
                          pack(sm_v, pad_cw(v_conv_w)), "adamw_small")

    def unpack(p):
        vecs = [p[8 * k:8 * k + n // 128].reshape(1, n) for k, n in enumerate(sizes)]
        return vecs, p[64:64 + CONV_WIDTH].reshape(1, CONV_WIDTH, 1, 128)

    def ordered(vecs, cw, w_in_, w_out_, w_gate_, w_up_, w_down_):
        g1_, cb_, lg_, lb_, ag_, g2_, g3_, g4_ = vecs
        return [g1_, w_in_, cw, cb_, lg_, lb_, ag_.reshape(1, 8, HEAD_DIM), w_out_, g2_, g3_,
                w_gate_, w_up_, w_down_, g4_]

    grads = ordered(g_small, g_cw[:CONV_WIDTH].reshape(1, CONV_WIDTH, 1, 128), *[b[0] for b in big])
    outs = []
    for idx, p in enumerate((sd, smn, svn)):
        vecs, cw = unpack(p)
        outs += ordered(vecs, cw, *[b[idx + 1] for b in big])
    return (loss, grad_x, *grads, *outs)
```

```python
import functools
import math

import jax
import jax.numpy as jnp
from jax import lax
from jax.experimental import pallas as pl
from jax.experimental.pallas import tpu as pltpu

F32 = jnp.float32
BF16 = jnp.bfloat16
MESH = pl.DeviceIdType.MESH

D_MODEL = 1024
CONV_CH = 512
CONV_WIDTH = 31
HEAD_DIM = 64
PAIR = 2 * HEAD_DIM
N_PAIRS = 4
D_FF = 2816
N_CHIPS = 4
IN_SH = 2560 // N_CHIPS
FF_SH = D_FF // N_CHIPS
OUT_SH = D_MODEL // N_CHIPS
C_ROWS = OUT_SH + FF_SH
EPS = 1e-6
HALO = 32

ADAM_LR = 0.001
ADAM_B1 = 0.9
ADAM_B2 = 0.999
ADAM_EPS = 1e-08
ADAM_WD = 0.01
ADAM_STEP = 10

VMEM_LIMIT = 56 * 2 ** 20
VMEM_LIMIT_ATTN_BWD = 60 * 2 ** 20


def _cp(sem=None, vmem=VMEM_LIMIT):
    return pltpu.CompilerParams(dimension_semantics=sem, vmem_limit_bytes=vmem)


def _hbm():
    return pl.BlockSpec(memory_space=pltpu.HBM)


def _const_spec(shape):
    nd = len(shape)
    return pl.BlockSpec(shape, lambda *_: (0,) * nd, pipeline_mode=pl.Buffered(1))


def _dot(a, b):
    return jnp.dot(a, b, preferred_element_type=F32)


def _dot_nt(a, b):
    return lax.dot_general(a, b, (((1,), (1,)), ((), ())), preferred_element_type=F32)


def _dot_tn(a, b):
    return lax.dot_general(a, b, (((0,), (0,)), ((), ())), preferred_element_type=F32)


def _split3(x):
    b0 = x.astype(BF16)
    r1 = x - b0.astype(F32)
    b1 = r1.astype(BF16)
    b2 = (r1 - b1.astype(F32)).astype(BF16)
    return b0, b1, b2


def _split2(x):
    hi = x.astype(BF16)
    lo = (x - hi.astype(F32)).astype(BF16)
    return hi, lo


def _sigmoid(x):
    return 1.0 / (1.0 + jnp.exp(-x))


def _head_mean(x, seg):
    b0, b1, b2 = _split3(x)
    return (_dot(b0, seg) + _dot(b1, seg) + _dot(b2, seg)) * (1.0 / HEAD_DIM)


def _seg_matrix(n):
    r = lax.broadcasted_iota(jnp.int32, (n, n), 0) // HEAD_DIM
    c = lax.broadcasted_iota(jnp.int32, (n, n), 1) // HEAD_DIM
    return (r == c).astype(BF16)


def _rms(x):
    return lax.rsqrt(jnp.mean(x * x, axis=-1, keepdims=True) + EPS)


def _rms_bwd(dy, n, r, g):
    dn = dy * g
    dx = r * (dn - n * jnp.mean(dn * n, axis=-1, keepdims=True))
    return dx, dy * n


class _GatherPlan:
    def __init__(self, srcs, outs, lead, ssem, rsem):
        self.srcs, self.outs, self.lead, self.ssem, self.rsem = srcs, outs, lead, ssem, rsem
        x, y, self.c = lax.axis_index("x"), lax.axis_index("y"), lax.axis_index("c")
        self.me = 2 * x + y
        self.sibling = (x, y, 1 - self.c)
        self.chips = [(1 - x, y), (x, 1 - y), (1 - x, 1 - y)]

    def _half(self, ref, i, h):
        if self.lead[i]:
            return ref.at[h]
        rows = ref.shape[0] // 2
        return ref.at[pl.ds(h * rows, rows)]

    def _ici(self, i, k, origin):
        return pltpu.make_async_remote_copy(
            src_ref=self._half(self.srcs[i], i, self.c), dst_ref=self._half(self.outs[i].at[origin], i, self.c),
            send_sem=self.ssem.at[6 * i + k], recv_sem=self.rsem.at[6 * i + k],
            device_id=(self.chips[k][0], self.chips[k][1], self.c), device_id_type=MESH)

    def _d2d(self, i, k, h):
        origin = 2 * self.chips[k][0] + self.chips[k][1]
        piece = self._half(self.outs[i].at[origin], i, h)
        return pltpu.make_async_remote_copy(
            src_ref=piece, dst_ref=piece, send_sem=self.ssem.at[6 * i + 3 + k],
            recv_sem=self.rsem.at[6 * i + 3 + k], device_id=self.sibling, device_id_type=MESH)

    def _each(self):
        return [(i, k) for i in range(len(self.srcs)) for k in range(3)]

    def start(self):
        for i, k in self._each():
            self._ici(i, k, self.me).start()

    def forward(self):
        for i, k in self._each():
            self._ici(i, k, 2 * self.chips[k][0] + self.chips[k][1]).wait_recv()
            self._d2d(i, k, self.c).start()

    def finish(self):
        for i, k in self._each():
            self._d2d(i, k, 1 - self.c).wait_recv()
        for i, k in self._each():
            self._ici(i, k, self.me).wait_send()
            self._d2d(i, k, self.c).wait_send()


def _gather_shapes(shards):
    return [jax.ShapeDtypeStruct((N_CHIPS,) + s.shape, s.dtype) for s in shards]


def _gather_weights(shards, lead):
    n = len(shards)

    def body(*refs):
        plan = _GatherPlan(refs[:n], refs[n:2 * n], lead, refs[2 * n], refs[2 * n + 1])
        plan.start()
        plan.forward()
        plan.finish()

    return pl.pallas_call(
        body, name="gather_weights", out_shape=_gather_shapes(shards),
        in_specs=[_hbm()] * n, out_specs=[_hbm()] * n,
        scratch_shapes=[pltpu.SemaphoreType.DMA((6 * n,)), pltpu.SemaphoreType.DMA((6 * n,))],
    )(*shards)


def _in_proj(x2, g1, wa, tm):
    S = x2.shape[0]

    def body(x_ref, g_ref, w_ref, a_ref, uc_ref, qkv_ref):
        x = x_ref[...]
        a = (x * _rms(x) * g_ref[...]).astype(BF16)
        a_ref[...] = a
        u = [_dot(a, w_ref[j]) for j in range(N_CHIPS)]
        uc_ref[:, 0:640] = u[0]
        uc_ref[:, 640:1024] = u[1][:, 0:384]
        qkv_ref[:, 0:256] = u[1][:, 384:640].astype(BF16)
        qkv_ref[:, 256:896] = u[2].astype(BF16)
        qkv_ref[:, 896:1536] = u[3].astype(BF16)

    return pl.pallas_call(
        body, name="in_proj", grid=(S // tm,),
        in_specs=[pl.BlockSpec((tm, D_MODEL), lambda i: (i, 0)), _const_spec((1, D_MODEL)),
                  _const_spec(wa.shape)],
        out_specs=[pl.BlockSpec((tm, D_MODEL), lambda i: (i, 0)),
                   pl.BlockSpec((tm, 2 * CONV_CH), lambda i: (i, 0)),
                   pl.BlockSpec((tm, 1536), lambda i: (i, 0))],
        out_shape=[jax.ShapeDtypeStruct((S, D_MODEL), BF16), jax.ShapeDtypeStruct((S, 2 * CONV_CH), F32),
                   jax.ShapeDtypeStruct((S, 1536), BF16)],
        compiler_params=_cp(("parallel",)),
    )(x2, g1, wa)


SUBLANES = 8


def _shift_copies(src_ref, sh_ref):
    rows = sh_ref.shape[1]
    for b in range(1, SUBLANES):
        sh_ref[b - 1] = src_ref[pl.ds(b, rows), :]


def _rows_at(src_ref, sh_ref, off, rows):
    a, b = divmod(off, SUBLANES)
    if b == 0:
        return src_ref[pl.ds(SUBLANES * a, rows), :]
    return sh_ref[b - 1, pl.ds(SUBLANES * a, rows), :]


def _conv_taps(cw_ref, src_ref, sh_ref, offs, rows):
    acc = None
    for w, off in enumerate(offs):
        term = cw_ref[w:w + 1, :] * _rows_at(src_ref, sh_ref, off, rows)
        acc = term if acc is None else acc + term
    return acc


def _glu(uc):
    return uc[:, :CONV_CH] * _sigmoid(uc[:, CONV_CH:])


def _conv_fwd(uc, cwf, cb, lg, lb, tm):
    S = uc.shape[0]
    hb = tm // HALO

    def body(uc_ref, prev_ref, cw_ref, cb_ref, lg_ref, lb_ref, out_ref, y_ref, glu_ref, sh_ref):
        i = pl.program_id(0)
        glu_ref[0:HALO, :] = jnp.where(i == 0, 0.0, _glu(prev_ref[...]))
        glu_ref[HALO:HALO + tm, :] = _glu(uc_ref[...])
        glu_ref[HALO + tm:HALO + tm + SUBLANES, :] = jnp.zeros((SUBLANES, CONV_CH), F32)
        _shift_copies(glu_ref, sh_ref)
        offs = [HALO - (CONV_WIDTH - 1) + w for w in range(CONV_WIDTH)]
        y = _conv_taps(cw_ref, glu_ref, sh_ref, offs, tm) + cb_ref[...]
        y_ref[...] = y
        mu = jnp.mean(y, axis=-1, keepdims=True)
        yc = y - mu
        rstd = lax.rsqrt(jnp.mean(yc * yc, axis=-1, keepdims=True) + EPS)
        ln = yc * rstd * lg_ref[...] + lb_ref[...]
        out_ref[...] = (ln * _sigmoid(ln)).astype(BF16)

    return pl.pallas_call(
        body, name="conv_fwd", grid=(S // tm,),
        in_specs=[pl.BlockSpec((tm, 2 * CONV_CH), lambda i: (i, 0)),
                  pl.BlockSpec((HALO, 2 * CONV_CH), lambda i: (jnp.maximum(i * hb - 1, 0), 0)),
                  _const_spec(cwf.shape), _const_spec((1, CONV_CH)), _const_spec((1, CONV_CH)),
                  _const_spec((1, CONV_CH))],
        out_specs=[pl.BlockSpec((tm, CONV_CH), lambda i: (i, 0))] * 2,
        out_shape=[jax.ShapeDtypeStruct((S, CONV_CH), BF16), jax.ShapeDtypeStruct((S, CONV_CH), F32)],
        scratch_shapes=[pltpu.VMEM((HALO + tm + SUBLANES, CONV_CH), F32),
                        pltpu.VMEM((SUBLANES - 1, HALO + tm, CONV_CH), F32)],
        compiler_params=_cp(("parallel",)),
    )(uc, uc, cwf, cb, lg, lb)


def _lane_mask(h):
    lane = lax.broadcasted_iota(jnp.int32, (1, PAIR), 1)
    return (lane >= HEAD_DIM * h) & (lane < HEAD_DIM * (h + 1))


def _neg_abs(x):
    bits = lax.bitcast_convert_type(x, jnp.uint32) | jnp.uint32(0x80000000)
    return lax.bitcast_convert_type(bits, F32)


def _tri_dot(x, m):
    return _dot(x.astype(BF16), m)


MASKED = -1e30
KEY_BLOCKS = 4


def _running_sums(x, m, reverse, start=None):
    t = m.shape[0]
    blocks = x.shape[1] // t
    order = range(blocks - 1, -1, -1) if reverse else range(blocks)
    out = [None] * blocks
    carry = start
    for b in order:
        xb = x[:, b * t:(b + 1) * t]
        cb = _tri_dot(xb, m)
        out[b] = cb if carry is None else cb + carry
        rs = jnp.sum(xb, axis=1, keepdims=True)
        carry = rs if carry is None else carry + rs
    return jnp.concatenate(out, axis=1), carry


def _sb_tile(z, r, m_suf):
    sp = jnp.maximum(z, 0.0) + jnp.log(1.0 + jnp.exp(_neg_abs(z)))
    c, rs = _running_sums(sp, m_suf, reverse=True, start=r)
    return jnp.exp(z - c), sp, rs


def _scores(qm, kt, mask):
    z = _dot_nt(qm, kt)
    return z if mask is None else jnp.where(mask, z, MASKED)


def _causal_mask(i, sb, t, tk):
    row = lax.broadcasted_iota(jnp.int32, (t, tk), 0) + i * t
    col = lax.broadcasted_iota(jnp.int32, (t, tk), 1) + sb * tk
    return col < row


def _sweep_plain(first, count, tile):
    def step(n, carry):
        tile(first + n)
        return carry

    lax.fori_loop(0, count + 1, step, 0)


def _sweep(first, count, down, fetch, load, work):
    lo, hi = (first - count, first) if down else (first, first + count)
    tile = lambda j: jnp.clip(first - j if down else first + j, lo, hi)
    fetch(first, 0, True)

    def step(n, carry):
        j = 2 * n
        vals = load(0)
        fetch(tile(j + 1), 1, False)
        work(tile(j), vals)
        vals = load(1)
        fetch(tile(j + 2), 0, False)
        work(tile(j + 1), vals)
        return carry

    lax.fori_loop(0, (count + 1) // 2, step, 0)

    @pl.when(lax.rem(count, 2) == 0)
    def _():
        work(tile(count), load(0))


def _suffix_matrix(t, prefix=False):
    row = lax.broadcasted_iota(jnp.int32, (t, t), 0)
    col = lax.broadcasted_iota(jnp.int32, (t, t), 1)
    return ((row <= col) if prefix else (row >= col)).astype(BF16)


def _attn_fwd(qkv, t, tk, shards, lead):
    S = qkv.shape[0]

    ng = len(shards)
    nq = S // t

    def body(*refs):
        q_ref, k_ref, v_ref = refs[:3]
        o_ref = refs[3 + ng]
        acc_ref, r_ref, z_buf, ssem, rsem = refs[4 + 2 * ng:]
        p = pl.program_id(0)
        i = pl.program_id(1)
        plan = _GatherPlan(refs[3:3 + ng], refs[4 + ng:4 + 2 * ng], lead, ssem, rsem)
        pl.when((p == 0) & (i == 0))(plan.start)
        pl.when((p == 1) & (i == 0))(plan.forward)
        last = (i * t + t - 1) // tk
        m_suf = _suffix_matrix(tk // KEY_BLOCKS)
        q = q_ref[...]
        hms = [_lane_mask(h) for h in range(2)]
        qms = [jnp.where(hm, q, 0) * 0.125 for hm in hms]
        acc_ref[...] = jnp.zeros_like(acc_ref)
        r_ref[...] = jnp.zeros_like(r_ref)

        def rows(sb):
            return pl.ds(pl.multiple_of(sb * tk, tk), tk)

        def fetch(sb, slot, diagonal):
            kt = k_ref[rows(sb), :]
            mask = _causal_mask(i, sb, t, tk) if diagonal else None
            for h in range(2):
                z_buf[slot, h] = _scores(qms[h], kt, mask)

        def load(slot):
            return [z_buf[slot, h] for h in range(2)]

        def work(sb, zs):
            vt = v_ref[rows(sb), :]
            for h in range(2):
                a_loc, _, rs = _sb_tile(zs[h], None, m_suf)
                r = r_ref[h]
                acc_ref[...] += _dot(a_loc.astype(BF16), jnp.where(hms[h], vt, 0)) * jnp.exp(-r)
                r_ref[h] = r + rs

        _sweep(last, last, True, fetch, load, work)
        o_ref[...] = acc_ref[...]
        pl.when((p == N_PAIRS - 1) & (i == nq - 1))(plan.finish)

    return pl.pallas_call(
        body, name="attn_fwd", grid=(N_PAIRS, nq),
        in_specs=[pl.BlockSpec((t, PAIR), lambda p, i: (i, p)),
                  pl.BlockSpec((S, PAIR), lambda p, i: (0, N_PAIRS + p)),
                  pl.BlockSpec((S, PAIR), lambda p, i: (0, 2 * N_PAIRS + p))] + [_hbm()] * ng,
        out_specs=[pl.BlockSpec((t, PAIR), lambda p, i: (i, p))] + [_hbm()] * ng,
        out_shape=[jax.ShapeDtypeStruct((S, N_PAIRS * PAIR), F32)] + _gather_shapes(shards),
        scratch_shapes=[pltpu.VMEM((t, PAIR), F32), pltpu.VMEM((2, t, 1), F32),
                        pltpu.VMEM((2, 2, t, tk), F32),
                        pltpu.SemaphoreType.DMA((6 * ng,)), pltpu.SemaphoreType.DMA((6 * ng,))],
        compiler_params=_cp(("arbitrary", "arbitrary")),
    )(qkv, qkv, qkv, *shards)


def _out_proj(conv_out, o, ag, wc, x2, g2, g3, tm):
    S = o.shape[0]

    def body(co_ref, o_ref, ag_ref, w_ref, x_ref, g2_ref, g3_ref, mix_ref, y_ref, h1_ref, fin_ref):
        seg = _seg_matrix(CONV_CH)
        o = o_ref[...]
        att = (o * lax.rsqrt(_head_mean(o * o, seg) + EPS) * ag_ref[...]).astype(BF16)
        co = co_ref[...]
        mix_ref[:, :CONV_CH] = co
        mix_ref[:, CONV_CH:] = att
        y = _dot(co, w_ref[0:CONV_CH, :]) + _dot(att, w_ref[CONV_CH:, :])
        y_ref[...] = y
        h1 = x_ref[...] + y * _rms(y) * g2_ref[...]
        h1_ref[...] = h1
        fin_ref[...] = (h1 * _rms(h1) * g3_ref[...]).astype(BF16)

    row = lambda w: pl.BlockSpec((tm, w), lambda i: (i, 0))
    return pl.pallas_call(
        body, name="out_proj", grid=(S // tm,),
        in_specs=[row(CONV_CH), row(CONV_CH), _const_spec((1, CONV_CH)), _const_spec(wc.shape),
                  row(D_MODEL), _const_spec((1, D_MODEL)), _const_spec((1, D_MODEL))],
        out_specs=[row(D_MODEL)] * 4,
        out_shape=[jax.ShapeDtypeStruct((S, D_MODEL), BF16), jax.ShapeDtypeStruct((S, D_MODEL), F32),
                   jax.ShapeDtypeStruct((S, D_MODEL), F32), jax.ShapeDtypeStruct((S, D_MODEL), BF16)],
        compiler_params=_cp(("parallel",)),
    )(conv_out, o, ag, wc, x2, g2, g3)


def _ffn_fwd(f_in, h1, tgt, wg, wu, wd, g4, tm):
    S = f_in.shape[0]

    def body(fin_ref, h1_ref, tgt_ref, wg_ref, wu_ref, wd_ref, g4_ref, df_ref, dh2_ref, dg4_ref, loss_ref,
             gt_ref, up_ref, act_ref):
        i = pl.program_id(0)
        fin = fin_ref[...]
        gt = _dot(fin, wg_ref[...])
        up = _dot(fin, wu_ref[...])
        act = (gt * _sigmoid(gt) * up).astype(BF16)
        gt_ref[...] = gt.astype(BF16)
        up_ref[...] = up.astype(BF16)
        act_ref[...] = act
        f = _dot(act, wd_ref[...])
        r = _rms(f)
        n = f * r
        g4 = g4_ref[...]
        err = h1_ref[...] + n * g4 - tgt_ref[...]
        dh2 = err * (1.0 / D_MODEL)
        dh2_ref[...] = dh2
        df, dg = _rms_bwd(dh2, n, r, g4)
        df_ref[...] = df.astype(BF16)

        @pl.when(i == 0)
        def _():
            dg4_ref[...] = jnp.zeros_like(dg4_ref)
            loss_ref[...] = jnp.zeros_like(loss_ref)

        dg4_ref[...] += jnp.sum(dg, axis=0, keepdims=True)
        part = jnp.sum(jnp.sum(err * err, axis=1, keepdims=True), axis=0, keepdims=True)
        loss_ref[...] += part * (0.5 / D_MODEL)

    row = lambda w: pl.BlockSpec((tm, w), lambda i: (i, 0))
    return pl.pallas_call(
        body, name="ffn_fwd", grid=(S // tm,),
        in_specs=[row(D_MODEL), row(D_MODEL), row(D_MODEL), _const_spec(wg.shape), _const_spec(wu.shape),
                  _const_spec(wd.shape), _const_spec((1, D_MODEL))],
        out_specs=[row(D_MODEL), row(D_MODEL), pl.BlockSpec((1, D_MODEL), lambda i: (0, 0)),
                   pl.BlockSpec((1, 128), lambda i: (0, 0)), row(D_FF), row(D_FF), row(D_FF)],
        out_shape=[jax.ShapeDtypeStruct((S, D_MODEL), BF16), jax.ShapeDtypeStruct((S, D_MODEL), F32),
                   jax.ShapeDtypeStruct((1, D_MODEL), F32), jax.ShapeDtypeStruct((1, 128), F32)]
        + [jax.ShapeDtypeStruct((S, D_FF), BF16)] * 3,
        compiler_params=_cp(("arbitrary",)),
    )(f_in, h1, tgt, wg, wu, wd, g4)


def _ffn_bwd(gt_bf, up_bf, df, dh2, h1, yv, wg, wu, wd, g3, g2, tm):
    S = df.shape[0]

    def body(gt_ref, up_ref, df_ref, dh2_ref, h1_ref, y_ref, wg_ref, wu_ref, wd_ref, g3_ref, g2_ref,
             dgt_ref, dup_ref, dh1_ref, dy_ref, dg3_ref, dg2_ref):
        i = pl.program_id(0)
        df = df_ref[...]
        gt = gt_ref[...].astype(F32)
        up = up_ref[...].astype(F32)
        sg = _sigmoid(gt)
        silu = gt * sg
        dact = _dot_nt(df, wd_ref[...])
        dgt = (dact * up * (sg * (1.0 + gt * (1.0 - sg)))).astype(BF16)
        dup = (dact * silu).astype(BF16)
        dgt_ref[...] = dgt
        dup_ref[...] = dup
        dfin = _dot_nt(dgt, wg_ref[...]) + _dot_nt(dup, wu_ref[...])
        h1 = h1_ref[...]
        r3 = _rms(h1)
        dh1_n, dg3 = _rms_bwd(dfin, h1 * r3, r3, g3_ref[...])
        dh1 = dh2_ref[...] + dh1_n
        dh1_ref[...] = dh1
        y = y_ref[...]
        r2 = _rms(y)
        dy, dg2 = _rms_bwd(dh1, y * r2, r2, g2_ref[...])
        dy_ref[...] = dy.astype(BF16)

        @pl.when(i == 0)
        def _():
            dg3_ref[...] = jnp.zeros_like(dg3_ref)
            dg2_ref[...] = jnp.zeros_like(dg2_ref)

        dg3_ref[...] += jnp.sum(dg3, axis=0, keepdims=True)
        dg2_ref[...] += jnp.sum(dg2, axis=0, keepdims=True)

    row = lambda w: pl.BlockSpec((tm, w), lambda i: (i, 0))
    vec = pl.BlockSpec((1, D_MODEL), lambda i: (0, 0))
    return pl.pallas_call(
        body, name="ffn_bwd", grid=(S // tm,),
        in_specs=[row(D_FF), row(D_FF)] + [row(D_MODEL)] * 4
        + [_const_spec(wg.shape), _const_spec(wu.shape), _const_spec(wd.shape),
           _const_spec((1, D_MODEL)), _const_spec((1, D_MODEL))],
        out_specs=[row(D_FF), row(D_FF), row(D_MODEL), row(D_MODEL), vec, vec],
        out_shape=[jax.ShapeDtypeStruct((S, D_FF), BF16)] * 2
        + [jax.ShapeDtypeStruct((S, D_MODEL), F32), jax.ShapeDtypeStruct((S, D_MODEL), BF16),
           jax.ShapeDtypeStruct((1, D_MODEL), F32), jax.ShapeDtypeStruct((1, D_MODEL), F32)],
        compiler_params=_cp(("arbitrary",)),
    )(gt_bf, up_bf, df, dh2, h1, yv, wg, wu, wd, g3, g2)


def _out_bwd(dy, o, ag, wc, tm):
    S = o.shape[0]

    def body(dy_ref, o_ref, ag_ref, w_ref, dco_ref, do_ref, dag_ref):
        i = pl.program_id(0)
        seg = _seg_matrix(CONV_CH)
        dy = dy_ref[...]
        dco_ref[...] = _dot_nt(dy, w_ref[0:CONV_CH, :])
        datt = _dot_nt(dy, w_ref[CONV_CH:, :])
        o = o_ref[...]
        r = lax.rsqrt(_head_mean(o * o, seg) + EPS)
        n = o * r
        dn = datt * ag_ref[...]
        do_ref[...] = (r * (dn - n * _head_mean(dn * n, seg))).astype(BF16)

        @pl.when(i == 0)
        def _():
            dag_ref[...] = jnp.zeros_like(dag_ref)

        dag_ref[...] += jnp.sum(datt * n, axis=0, keepdims=True)

    row = lambda w: pl.BlockSpec((tm, w), lambda i: (i, 0))
    return pl.pallas_call(
        body, name="out_bwd", grid=(S // tm,),
        in_specs=[row(D_MODEL), row(CONV_CH), _const_spec((1, CONV_CH)), _const_spec(wc.shape)],
        out_specs=[row(CONV_CH), row(CONV_CH), pl.BlockSpec((1, CONV_CH), lambda i: (0, 0))],
        out_shape=[jax.ShapeDtypeStruct((S, CONV_CH), F32), jax.ShapeDtypeStruct((S, CONV_CH), BF16),
                   jax.ShapeDtypeStruct((1, CONV_CH), F32)],
        compiler_params=_cp(("arbitrary",)),
    )(dy, o, ag, wc)


def _attn_bwd(qkv, do, t, parts):
    S = qkv.shape[0]
    tk = KEY_BLOCKS * t
    nk = S // tk
    ns = len(parts)

    def body(*refs):
        q_ref, k_ref, v_ref, do_ref = refs[:4]
        dq_ref, dk_hbm, dv_hbm = refs[4 + ns:7 + ns]
        g_buf, s_buf, r_ref, dq_acc, dk_ref, dv_ref, z_buf, da_buf = refs[7 + 2 * ns:15 + 2 * ns]
        p = pl.program_id(0)
        i = pl.program_id(1)
        plan = _ScatterPlan(refs[4:4 + ns], refs[7 + ns:7 + 2 * ns], *refs[15 + 2 * ns:])
        pl.when((p == 0) & (i == 0))(plan.start)
        last = (i * t + t - 1) // tk

        @pl.when(i == 0)
        def _():
            dk_ref[...] = jnp.zeros_like(dk_ref)
            dv_ref[...] = jnp.zeros_like(dv_ref)

        m_suf = _suffix_matrix(t)
        m_pre = _suffix_matrix(t, prefix=True)
        q = q_ref[...]
        do = do_ref[...]
        hms = [_lane_mask(h) for h in range(2)]
        qms = [jnp.where(hm, q, 0) * 0.125 for hm in hms]
        doms = [jnp.where(hm, do, 0) for hm in hms]
        dq_acc[...] = jnp.zeros_like(dq_acc)
        r_ref[...] = jnp.zeros_like(r_ref)

        def rows(sb):
            return pl.ds(pl.multiple_of(sb * tk, tk), tk)

        def fetch1(sb, slot, diagonal):
            kt = k_ref[rows(sb), :]
            vt = v_ref[rows(sb), :]
            mask = _causal_mask(i, sb, t, tk) if diagonal else None
            for h in range(2):
                z_buf[slot, h] = _scores(qms[h], kt, mask)
                da_buf[slot, h] = _dot_nt(doms[h], vt)

        def load1(slot):
            return [(z_buf[slot, h], da_buf[slot, h]) for h in range(2)]

        def work1(sb, vals):
            dv = jnp.zeros((tk, PAIR), F32)
            for h in range(2):
                z, da = vals[h]
                A, sp, r_ref[h] = _sb_tile(z, r_ref[h], m_suf)
                g_buf[h, sb] = A * da
                s_buf[h, sb] = sp
                dv = dv + _dot_tn(A.astype(BF16), doms[h])
            dv_ref[rows(sb), :] += dv

        _sweep(last, last, True, fetch1, load1, work1)
        r_ref[...] = jnp.zeros_like(r_ref)

        def sweep2(sb):
            kt = k_ref[rows(sb), :]
            dk = jnp.zeros((tk, PAIR), F32)
            for h in range(2):
                g = g_buf[h, sb]
                pre, r_ref[h] = _running_sums(g, m_pre, reverse=False, start=r_ref[h])
                sig = 1.0 - jnp.exp(-s_buf[h, sb])
                dzb = (g - sig * pre).astype(BF16)
                dq_acc[...] += _dot(dzb, jnp.where(hms[h], kt, 0))
                dk = dk + _dot_tn(dzb, qms[h])
            dk_ref[rows(sb), :] += dk

        _sweep_plain(0, last, sweep2)
        dq_ref[...] = dq_acc[...] * 0.125

        @pl.when(i == S // t - 1)
        def _():
            cols = pl.ds(pl.multiple_of(p * PAIR, PAIR), PAIR)
            pltpu.sync_copy(dk_ref, dk_hbm.at[:, cols])
            pltpu.sync_copy(dv_ref, dv_hbm.at[:, cols])

        pl.when((p == N_PAIRS - 1) & (i == S // t - 1))(plan.finish)

    once = lambda cb: pl.BlockSpec((S, PAIR), cb, pipeline_mode=pl.Buffered(1))
    return pl.pallas_call(
        body, name="attn_bwd", grid=(N_PAIRS, S // t),
        in_specs=[pl.BlockSpec((t, PAIR), lambda p, i: (i, p)),
                  once(lambda p, i: (0, N_PAIRS + p)), once(lambda p, i: (0, 2 * N_PAIRS + p)),
                  pl.BlockSpec((t, PAIR), lambda p, i: (i, p))] + [_hbm()] * ns,
        out_specs=[pl.BlockSpec((t, PAIR), lambda p, i: (i, p)), _hbm(), _hbm()] + [_hbm()] * ns,
        out_shape=[jax.ShapeDtypeStruct((S, N_PAIRS * PAIR), F32)] * 3
        + [jax.ShapeDtypeStruct(pt.shape, pt.dtype) for pt in parts],
        scratch_shapes=[pltpu.VMEM((2, nk, t, tk), F32), pltpu.VMEM((2, nk, t, tk), F32),
                        pltpu.VMEM((2, t, 1), F32), pltpu.VMEM((t, PAIR), F32),
                        pltpu.VMEM((S, PAIR), F32), pltpu.VMEM((S, PAIR), F32),
                        pltpu.VMEM((2, 2, t, tk), F32), pltpu.VMEM((2, 2, t, tk), F32)] + _scatter_sems(ns),
        compiler_params=_cp(("arbitrary", "arbitrary"), vmem=VMEM_LIMIT_ATTN_BWD),
    )(qkv, qkv, qkv, do, *parts)


def _conv_bwd(uc, yconv, dco, cwf, lg, lb, tm):
    S = uc.shape[0]
    hb = tm // HALO
    nb = S // tm
    ext = tm + HALO

    def body(uc_ref, prev_ref, y_ref, ynext_ref, dco_ref, dnext_ref, cw_ref, lg_ref, lb_ref,
             duc_ref, dcw_ref, dcb_ref, dlg_ref, dlb_ref, glu_ref, dyc_ref, shg_ref, shd_ref):
        i = pl.program_id(0)
        last = i == nb - 1

        @pl.when(i == 0)
        def _():
            for ref in (dcw_ref, dcb_ref, dlg_ref, dlb_ref):
                ref[...] = jnp.zeros_like(ref)

        uc = uc_ref[...]
        glu_ref[0:HALO, :] = jnp.where(i == 0, 0.0, _glu(prev_ref[...]))
        glu_ref[HALO:ext, :] = _glu(uc)
        glu_ref[ext:ext + SUBLANES, :] = jnp.zeros((SUBLANES, CONV_CH), F32)
        _shift_copies(glu_ref, shg_ref)
        fwd_offs = [HALO - (CONV_WIDTH - 1) + w for w in range(CONV_WIDTH)]
        y = jnp.concatenate([y_ref[...], ynext_ref[...]], axis=0)
        mu = jnp.mean(y, axis=-1, keepdims=True)
        yc = y - mu
        rstd = lax.rsqrt(jnp.mean(yc * yc, axis=-1, keepdims=True) + EPS)
        yhat = yc * rstd
        lg = lg_ref[...]
        ln = yhat * lg + lb_ref[...]
        sg = _sigmoid(ln)
        dout = jnp.concatenate([dco_ref[...], jnp.where(last, 0.0, dnext_ref[...])], axis=0)
        dln = dout * (sg * (1.0 + ln * (1.0 - sg)))
        dyh = dln * lg
        dyc = rstd * (dyh - jnp.mean(dyh, axis=-1, keepdims=True)
                      - yhat * jnp.mean(dyh * yhat, axis=-1, keepdims=True))
        dyc_ref[0:ext, :] = dyc
        dyc_ref[ext:ext + SUBLANES, :] = jnp.zeros((SUBLANES, CONV_CH), F32)
        _shift_copies(dyc_ref, shd_ref)
        dlg_ref[...] += jnp.sum((dln * yhat)[0:tm], axis=0, keepdims=True)
        dlb_ref[...] += jnp.sum(dln[0:tm], axis=0, keepdims=True)
        dcb_ref[...] += jnp.sum(dyc[0:tm], axis=0, keepdims=True)
        dglu = _conv_taps(cw_ref, dyc_ref, shd_ref, [CONV_WIDTH - 1 - w for w in range(CONV_WIDTH)], tm)
        d0 = dyc[0:tm]
        for w, off in enumerate(fwd_offs):
            dcw_ref[w:w + 1, :] += jnp.sum(d0 * _rows_at(glu_ref, shg_ref, off, tm), axis=0, keepdims=True)
        val, gate = uc[:, :CONV_CH], uc[:, CONV_CH:]
        sgate = _sigmoid(gate)
        duc_ref[:, :CONV_CH] = (dglu * sgate).astype(BF16)
        duc_ref[:, CONV_CH:] = (dglu * val * sgate * (1.0 - sgate)).astype(BF16)

    vec = pl.BlockSpec((1, CONV_CH), lambda i: (0, 0))
    nxt = lambda i: (jnp.minimum((i + 1) * hb, S // HALO - 1), 0)
    return pl.pallas_call(
        body, name="conv_bwd", grid=(nb,),
        in_specs=[pl.BlockSpec((tm, 2 * CONV_CH), lambda i: (i, 0)),
                  pl.BlockSpec((HALO, 2 * CONV_CH), lambda i: (jnp.maximum(i * hb - 1, 0), 0)),
                  pl.BlockSpec((tm, CONV_CH), lambda i: (i, 0)), pl.BlockSpec((HALO, CONV_CH), nxt),
                  pl.BlockSpec((tm, CONV_CH), lambda i: (i, 0)), pl.BlockSpec((HALO, CONV_CH), nxt),
                  _const_spec(cwf.shape), _const_spec((1, CONV_CH)), _const_spec((1, CONV_CH))],
        out_specs=[pl.BlockSpec((tm, 2 * CONV_CH), lambda i: (i, 0)),
                   pl.BlockSpec(cwf.shape, lambda i: (0, 0)), vec, vec, vec],
        out_shape=[jax.ShapeDtypeStruct((S, 2 * CONV_CH), BF16), jax.ShapeDtypeStruct(cwf.shape, F32)]
        + [jax.ShapeDtypeStruct((1, CONV_CH), F32)] * 3,
        scratch_shapes=[pltpu.VMEM((ext + SUBLANES, CONV_CH), F32), pltpu.VMEM((ext + SUBLANES, CONV_CH), F32),
                        pltpu.VMEM((SUBLANES - 1, ext, CONV_CH), F32),
                        pltpu.VMEM((SUBLANES - 1, ext, CONV_CH), F32)],
        compiler_params=_cp(("arbitrary",)),
    )(uc, uc, yconv, yconv, dco, dco, cwf, lg, lb)


def _in_bwd(duc, dq, dk, dv, x2, dh1, g1, wa, tm):
    S = x2.shape[0]

    def body(duc_ref, dq_ref, dk_ref, dv_ref, x_ref, dh1_ref, g_ref, w_ref, gx_ref, du_ref, dg_ref):
        i = pl.program_id(0)
        du = jnp.concatenate([duc_ref[...], dq_ref[...].astype(BF16), dk_ref[...].astype(BF16),
                              dv_ref[...].astype(BF16)], axis=1)
        du_ref[...] = du
        da = _dot_nt(du[:, 0:IN_SH], w_ref[0])
        for j in range(1, N_CHIPS):
            da = da + _dot_nt(du[:, IN_SH * j:IN_SH * (j + 1)], w_ref[j])
        x = x_ref[...]
        r = _rms(x)
        dx, dg = _rms_bwd(da, x * r, r, g_ref[...])
        gx_ref[...] = dh1_ref[...] + dx

        @pl.when(i == 0)
        def _():
            dg_ref[...] = jnp.zeros_like(dg_ref)

        dg_ref[...] += jnp.sum(dg, axis=0, keepdims=True)

    row = lambda w: pl.BlockSpec((tm, w), lambda i: (i, 0))
    return pl.pallas_call(
        body, name="in_bwd", grid=(S // tm,),
        in_specs=[row(2 * CONV_CH), row(CONV_CH), row(CONV_CH), row(CONV_CH), row(D_MODEL), row(D_MODEL),
                  _const_spec((1, D_MODEL)), _const_spec(wa.shape)],
        out_specs=[pl.BlockSpec((None, tm, D_MODEL), lambda i: (0, i, 0)), row(2560),
                   pl.BlockSpec((1, D_MODEL), lambda i: (0, 0))],
        out_shape=[jax.ShapeDtypeStruct((1, S, D_MODEL), F32), jax.ShapeDtypeStruct((S, 2560), BF16),
                   jax.ShapeDtypeStruct((1, D_MODEL), F32)],
        compiler_params=_cp(("arbitrary",)),
    )(duc, dq, dk, dv, x2, dh1, g1, wa)


def _matmul_tn(xm, ym, tm, ts, name, column_block=None):
    S, M = xm.shape
    N = ym.shape[1]

    def body(x_ref, y_ref, o_ref):
        @pl.when(pl.program_id(1) == 0)
        def _():
            o_ref[...] = jnp.zeros_like(o_ref)

        xt = x_ref[...].T
        if column_block is None:
            o_ref[...] += _dot(xt, y_ref[...])
        else:
            for j in range(N // column_block):
                o_ref[j] += _dot(xt, y_ref[:, column_block * j:column_block * (j + 1)])

    if column_block is None:
        out_spec = pl.BlockSpec((tm, N), lambda m, s: (m, 0))
        out_shape = jax.ShapeDtypeStruct((M, N), F32)
    else:
        out_spec = pl.BlockSpec((N // column_block, tm, column_block), lambda m, s: (0, m, 0))
        out_shape = jax.ShapeDtypeStruct((N // column_block, M, column_block), F32)
    return pl.pallas_call(
        body, name=name, grid=(M // tm, S // ts),
        in_specs=[pl.BlockSpec((ts, tm), lambda m, s: (s, m)), pl.BlockSpec((ts, N), lambda m, s: (s, 0))],
        out_specs=out_spec, out_shape=out_shape,
        compiler_params=_cp(("parallel", "arbitrary")),
    )(xm, ym)


def _sibling_halves(grads, name):
    n = len(grads)

    def body(*refs):
        ins, outs, ssem, rsem = refs[:n], refs[n:2 * n], refs[2 * n], refs[2 * n + 1]
        x, y, c = lax.axis_index("x"), lax.axis_index("y"), lax.axis_index("c")
        copies = []
        for k in range(n):
            for j in range(N_CHIPS):
                copies.append(pltpu.make_async_remote_copy(
                    src_ref=ins[k].at[j, 1 - c], dst_ref=outs[k].at[j],
                    send_sem=ssem.at[N_CHIPS * k + j], recv_sem=rsem.at[N_CHIPS * k + j],
                    device_id=(x, y, 1 - c), device_id_type=MESH))
        for cp in copies:
            cp.start()
        for cp in copies:
            cp.wait()

    shapes = [jax.ShapeDtypeStruct((g.shape[0],) + g.shape[2:], F32) for g in grads]
    return pl.pallas_call(
        body, name=name, out_shape=shapes,
        in_specs=[_hbm()] * n, out_specs=[_hbm()] * n,
        scratch_shapes=[pltpu.SemaphoreType.DMA((N_CHIPS * n,)), pltpu.SemaphoreType.DMA((N_CHIPS * n,))],
    )(*grads)


def _add_half(c_arr, g, landed, name):
    def body(c_ref, g_ref, l_ref, o_ref):
        o_ref[...] = (g_ref[...] + l_ref[...]).astype(BF16)

    rows, n = g.shape[2], g.shape[3]
    grid = (N_CHIPS,)
    g_spec = pl.BlockSpec((None, None, rows, n), lambda j, c: (j, c[0], 0, 0))
    l_spec = pl.BlockSpec((None, rows, n), lambda j, c: (j, 0, 0))
    return pl.pallas_call(
        body, name=name,
        grid_spec=pltpu.PrefetchScalarGridSpec(num_scalar_prefetch=1, grid=grid, in_specs=[g_spec, l_spec],
                                               out_specs=l_spec),
        out_shape=jax.ShapeDtypeStruct(landed.shape, BF16),
        compiler_params=_cp(("parallel",)),
    )(c_arr, g, landed)


class _ScatterPlan:
    def __init__(self, ins, outs, lsem, ssem, rsem):
        x, y, c = lax.axis_index("x"), lax.axis_index("y"), lax.axis_index("c")
        me = 2 * x + y
        self.copies = []
        for k in range(len(ins)):
            self.copies.append(pltpu.make_async_copy(ins[k].at[me], outs[k].at[me], lsem.at[k]))
            for r, chip in enumerate([(1 - x, y), (x, 1 - y), (1 - x, 1 - y)]):
                self.copies.append(pltpu.make_async_remote_copy(
                    src_ref=ins[k].at[2 * chip[0] + chip[1]], dst_ref=outs[k].at[me],
                    send_sem=ssem.at[3 * k + r], recv_sem=rsem.at[3 * k + r],
                    device_id=(chip[0], chip[1], c), device_id_type=MESH))

    def start(self):
        for cp in self.copies:
            cp.start()

    def finish(self):
        for cp in self.copies:
            cp.wait()


def _scatter_sems(n):
    return [pltpu.SemaphoreType.DMA((n,)), pltpu.SemaphoreType.DMA((3 * n,)), pltpu.SemaphoreType.DMA((3 * n,))]


def _chip_scatter(parts):
    n = len(parts)

    def body(*refs):
        plan = _ScatterPlan(refs[:n], refs[n:2 * n], *refs[2 * n:])
        plan.start()
        plan.finish()

    shapes = [jax.ShapeDtypeStruct(p.shape, p.dtype) for p in parts]
    return pl.pallas_call(
        body, name="grad_chip_scatter", out_shape=shapes,
        in_specs=[_hbm()] * n, out_specs=[_hbm()] * n, scratch_shapes=_scatter_sems(n),
    )(*parts)


def _sum_chips(landed, name):
    _, rows, n = landed.shape
    tr = 256 if rows % 256 == 0 else rows

    def body(a_ref, b_ref, c_ref, d_ref, o_ref):
        f = lambda ref: ref[...].astype(F32)
        o_ref[...] = ((f(a_ref) + f(b_ref)) + f(c_ref)) + f(d_ref)

    specs = [pl.BlockSpec((None, tr, n), functools.partial(lambda i, j: (j, i, 0), j=j)) for j in range(N_CHIPS)]
    return pl.pallas_call(
        body, name=name, grid=(rows // tr,), in_specs=specs,
        out_specs=pl.BlockSpec((tr, n), lambda i: (i, 0)),
        out_shape=jax.ShapeDtypeStruct((rows, n), F32),
        compiler_params=_cp(("parallel",)),
    )(landed, landed, landed, landed)


def _share_halves(halves):
    n = len(halves)

    def body(*refs):
        ins, outs = refs[:n], refs[n:2 * n]
        ssem, rsem = refs[2 * n:]
        x, y, c = lax.axis_index("x"), lax.axis_index("y"), lax.axis_index("c")
        copies = [pltpu.make_async_remote_copy(
            src_ref=ins[k], dst_ref=outs[k], send_sem=ssem.at[k], recv_sem=rsem.at[k],
            device_id=(x, y, 1 - c), device_id_type=MESH) for k in range(n)]
        for cp in copies:
            cp.start()
        for cp in copies:
            cp.wait()

    shapes = [jax.ShapeDtypeStruct(h.shape, F32) for h in halves]
    return pl.pallas_call(
        body, name="grad_share_halves", out_shape=shapes,
        in_specs=[_hbm()] * n, out_specs=[_hbm()] * n,
        scratch_shapes=[pltpu.SemaphoreType.DMA((n,)), pltpu.SemaphoreType.DMA((n,))],
    )(*halves)


def _allreduce_small(packed):
    rows, n = packed.shape

    def body(in_ref, out_ref, land_ref, ssem, rsem):
        x, y, c = lax.axis_index("x"), lax.axis_index("y"), lax.axis_index("c")
        me = 4 * x + 2 * y + c
        land_ref[me] = in_ref[...]
        copies = []
        for r in range(1, 8):
            tx = 1 - x if r & 4 else x
            ty = 1 - y if r & 2 else y
            tc = 1 - c if r & 1 else c
            cp = pltpu.make_async_remote_copy(
                src_ref=in_ref, dst_ref=land_ref.at[me], send_sem=ssem.at[r - 1], recv_sem=rsem.at[r - 1],
                device_id=(tx, ty, tc), device_id_type=MESH)
            cp.start()
            copies.append(cp)
        for cp in copies:
            cp.wait()
        acc = land_ref[0]
        for k in range(1, 8):
            acc = acc + land_ref[k]
        out_ref[...] = acc

    return pl.pallas_call(
        body, name="allreduce_small", out_shape=jax.ShapeDtypeStruct((rows, n), F32),
        in_specs=[pl.BlockSpec(memory_space=pltpu.VMEM)], out_specs=pl.BlockSpec(memory_space=pltpu.VMEM),
        scratch_shapes=[pltpu.VMEM((8, rows, n), F32), pltpu.SemaphoreType.DMA((7,)),
                        pltpu.SemaphoreType.DMA((7,))],
    )(packed)


def _adamw_math(w, g, m, v):
    m = ADAM_B1 * m + (1.0 - ADAM_B1) * g
    v = ADAM_B2 * v + (1.0 - ADAM_B2) * (g * g)
    m_hat = m / (1.0 - ADAM_B1 ** ADAM_STEP)
    v_hat = v / (1.0 - ADAM_B2 ** ADAM_STEP)
    return -ADAM_LR * (m_hat / (jnp.sqrt(v_hat) + ADAM_EPS) + ADAM_WD * w), m, v


def _adamw_halves(c_arr, w, mine, other, m, v, name):
    rows, n = mine.shape
    tr = 256 if rows % 256 == 0 else rows
    nb = rows // tr

    def body(c_ref, w_ref, a_ref, b_ref, m_ref, v_ref, g_ref, d_ref, mo_ref, vo_ref):
        g = jnp.where(pl.program_id(0) == c_ref[0], a_ref[...], b_ref[...])
        g_ref[...] = g
        d_ref[...], mo_ref[...], vo_ref[...] = _adamw_math(w_ref[...], g, m_ref[...], v_ref[...])

    full = pl.BlockSpec((None, tr, n), lambda h, i, c: (0, h * nb + i, 0))
    half = pl.BlockSpec((tr, n), lambda h, i, c: (i, 0))
    return pl.pallas_call(
        body, name=name,
        grid_spec=pltpu.PrefetchScalarGridSpec(num_scalar_prefetch=1, grid=(2, nb),
                                               in_specs=[full, half, half, full, full], out_specs=[full] * 4),
        out_shape=[jax.ShapeDtypeStruct((1, 2 * rows, n), F32)] * 4,
        compiler_params=_cp(("parallel", "parallel")),
    )(c_arr, w, mine, other, m, v)


def _adamw(w, g, m, v, name):
    rows, n = w.shape
    tr = 256 if rows % 256 == 0 else rows

    def body(w_ref, g_ref, m_ref, v_ref, d_ref, mo_ref, vo_ref):
        d_ref[...], mo_ref[...], vo_ref[...] = _adamw_math(w_ref[...], g_ref[...], m_ref[...], v_ref[...])

    spec = pl.BlockSpec((tr, n), lambda i: (i, 0))
    return pl.pallas_call(
        body, name=name, grid=(rows // tr,), in_specs=[spec] * 4, out_specs=[spec] * 3,
        out_shape=[jax.ShapeDtypeStruct((rows, n), F32)] * 3,
        compiler_params=_cp(("parallel",)),
    )(w, g, m, v)


def _rows8(a):
    a = a.reshape(-1, 128)
    return jnp.pad(a, ((0, (-a.shape[0]) % 8), (0, 0)))


def kernel(x, g_pre_mix, w_in, conv_w, conv_b, conv_ln_g, conv_ln_b, attn_norm_g, w_out, g_post_mix, g_pre_ffn, w_gate, w_up, w_down, g_post_ffn, loss_target, m_g_pre_mix, m_w_in, m_conv_w, m_conv_b, m_conv_ln_g, m_conv_ln_b, m_attn_norm_g, m_w_out, m_g_post_mix, m_g_pre_ffn, m_w_gate, m_w_up, m_w_down, m_g_post_ffn, v_g_pre_mix, v_w_in, v_conv_w, v_conv_b, v_conv_ln_g, v_conv_ln_b, v_attn_norm_g, v_w_out, v_g_post_mix, v_g_pre_ffn, v_w_gate, v_w_up, v_w_down, v_g_post_ffn):
    S = x.shape[1]
    tm_big = min(512, S)
    tm_ffn = min(256, S)
    tk_att = min(1024, S)
    t_att = tk_att // KEY_BLOCKS
    t_att_fwd = min(2 * t_att, S)
    chip = 2 * lax.axis_index("x") + lax.axis_index("y")
    core = lax.axis_index("c")
    x2 = x.reshape(S, D_MODEL)
    tgt = loss_target.reshape(S, D_MODEL)
    ag = attn_norm_g.reshape(1, CONV_CH)

    a_sh = w_in[0].astype(BF16)
    b_sh = jnp.stack([w_gate[0], w_up[0]]).astype(BF16)
    c_sh = jnp.concatenate([w_out[0], w_down[0]], axis=0).astype(BF16)
    cw_sh = jnp.pad(conv_w[0, :, 0, :], ((0, 1), (0, 0)))
    own = lambda full, shard: lax.dynamic_update_index_in_dim(full, shard, chip, 0)
    cols = lambda w4: jnp.transpose(w4, (1, 0, 2)).reshape(w4.shape[1], N_CHIPS * w4.shape[2])
    wa4, cw4 = _gather_weights([a_sh, cw_sh], [False, False])
    wa = own(wa4, a_sh)
    cwf = cols(own(cw4, cw_sh))

    a_bf, uc, qkv = _in_proj(x2, g_pre_mix, wa, tm_big)
    conv_out, yconv = _conv_fwd(uc, cwf, conv_b, conv_ln_g, conv_ln_b, tm_big)
    o, wb4, wc4 = _attn_fwd(qkv, t_att_fwd, tk_att, [b_sh, c_sh], [True, False])
    wb4, wc4 = own(wb4, b_sh), own(wc4, c_sh)
    wg, wu = cols(wb4[:, 0]), cols(wb4[:, 1])
    wo = wc4[:, :OUT_SH].reshape(D_MODEL, D_MODEL)
    wd = wc4[:, OUT_SH:].reshape(D_FF, D_MODEL)
    mixed, yv, h1, f_in = _out_proj(conv_out, o, ag, wo, x2, g_post_mix, g_pre_ffn, tm_big)
    df, dh2, dg4, loss_part, gt_bf, up_bf, act = _ffn_fwd(f_in, h1, tgt, wg, wu, wd, g_post_ffn, tm_ffn)

    dgt, dup, dh1, dy, dg3, dg2 = _ffn_bwd(gt_bf, up_bf, df, dh2, h1, yv, wg, wu, wd, g_pre_ffn, g_post_mix, tm_ffn)
    dco, do, dag = _out_bwd(dy, o, ag, wo, tm_big)
    ts = min(512, S)
    gw_out = _matmul_tn(mixed, dy, D_MODEL, ts, "grad_w_out")
    gw_gate = _matmul_tn(f_in, dgt, D_MODEL, ts, "grad_w_gate")
    gw_up = _matmul_tn(f_in, dup, D_MODEL, ts, "grad_w_up")
    gw_down = _matmul_tn(act, df, D_FF // 2, ts, "grad_w_down")

    by_cols = lambda g: jnp.transpose(g.reshape(2, D_MODEL // 2, N_CHIPS, -1), (2, 0, 1, 3))
    by_rows = lambda g: g.reshape(N_CHIPS, 2, g.shape[0] // (2 * N_CHIPS), g.shape[1])
    c_arr = core.reshape(1).astype(jnp.int32)

    def chip_partials(views, nms):
        landed = _sibling_halves(views, "grad_sibling_halves_" + nms[0])
        return [_add_half(c_arr, g, l, "grad_half_" + nm) for g, l, nm in zip(views, landed, nms)]

    early = ["w_gate", "w_up", "w_out", "w_down"]
    parts = chip_partials([by_cols(gw_gate), by_cols(gw_up), by_rows(gw_out), by_rows(gw_down)], early)
    dq, dk, dv, *slots = _attn_bwd(qkv, do, t_att, parts)
    duc, dcw, dcb, dlg, dlb = _conv_bwd(uc, yconv, dco, cwf, conv_ln_g, conv_ln_b, tm_big)
    grad_x, du, dg1 = _in_bwd(duc, dq, dk, dv, x2, dh1, g_pre_mix, wa, tm_big)
    gw_in = _matmul_tn(a_bf, du, D_MODEL, ts, "grad_w_in", column_block=IN_SH)
    slots += _chip_scatter(chip_partials([gw_in.reshape(N_CHIPS, 2, D_MODEL // 2, IN_SH)], ["w_in"]))
    names = early + ["w_in"]
    halves = [_sum_chips(s, "grad_sum_" + nm) for s, nm in zip(slots, names)]
    others = _share_halves(halves)
    mine = dict(zip(names, halves))
    other = dict(zip(names, others))

    small = [dg1, dcb, dlg, dlb, dag, dg2, dg3, dg4]
    packed = jnp.concatenate([_rows8(s) for s in small] + [_rows8(dcw), _rows8(loss_part)], axis=0)
    red = _allreduce_small(packed)
    sizes = [D_MODEL, CONV_CH, CONV_CH, CONV_CH, CONV_CH, D_MODEL, D_MODEL, D_MODEL]
    g_small = [red[8 * k:8 * k + n // 128].reshape(1, n) for k, n in enumerate(sizes)]
    cw_red = red[64:64 + 128].reshape(HALO, CONV_CH)
    g_cw = lax.dynamic_slice(cw_red, (0, chip * 128), (HALO, 128))
    loss = red[192, 0]

    big = []
    for w, m, v, nm in [(w_in, m_w_in, v_w_in, "w_in"), (w_out, m_w_out, v_w_out, "w_out"),
                        (w_gate, m_w_gate, v_w_gate, "w_gate"), (w_up, m_w_up, v_w_up, "w_up"),
                        (w_down, m_w_down, v_w_down, "w_down")]:
        big.append(_adamw_halves(c_arr, w, mine[nm], other[nm], m, v, "adamw_" + nm))
    sm_w = [g_pre_mix, conv_b, conv_ln_g, conv_ln_b, ag, g_post_mix, g_pre_ffn, g_post_ffn]
    sm_m = [m_g_pre_mix, m_conv_b, m_conv_ln_g, m_conv_ln_b, m_attn_norm_g, m_g_post_mix, m_g_pre_ffn, m_g_post_ffn]
    sm_v = [v_g_pre_mix, v_conv_b, v_conv_ln_g, v_conv_ln_b, v_attn_norm_g, v_g_post_mix, v_g_pre_ffn, v_g_post_ffn]
    pad_cw = lambda a: jnp.pad(a[0, :, 0, :], ((0, 1), (0, 0)))

    def pack(vecs, cw):
        return jnp.concatenate([_rows8(a) for a in vecs] + [cw], axis=0)

    sd, smn, svn = _adamw(pack(sm_w, pad_cw(conv_w)), pack(g_small, g_cw), pack(sm_m, pad_cw(m_conv_w)),
                          pack(sm_v, pad_cw(v_conv_w)), "adamw_small")

    def unpack(p):
        vecs = [p[8 * k:8 * k + n // 128].reshape(1, n) for k, n in enumerate(sizes)]
        return vecs, p[64:64 + CONV_WIDTH].reshape(1, CONV_WIDTH, 1, 128)

    def ordered(vecs, cw, w_in_, w_out_, w_gate_, w_up_, w_down_):
        g1_, cb_, lg_, lb_, ag_, g2_, g3_, g4_ = vecs
        return [g1_, w_in_, cw, cb_, lg_, lb_, ag_.reshape(1, 8, HEAD_DIM), w_out_, g2_, g3_,
                w_gate_, w_up_, w_down_, g4_]

    grads = ordered(g_small, g_cw[:CONV_WIDTH].reshape(1, CONV_WIDTH, 1, 128), *[b[0] for b in big])
    outs = []
    for idx, p in enumerate((sd, smn, svn)):
        vecs, cw = unpack(p)
        outs += ordered(vecs, cw, *[b[idx + 1] for b in big])
    return (loss, grad_x, *grads, *outs)
```

```python
import functools
import math

import jax
import jax.numpy as jnp
from jax import lax
from jax.experimental import pallas as pl
from jax.experimental.pallas import tpu as pltpu

F32 = jnp.float32
BF16 = jnp.bfloat16
MESH = pl.DeviceIdType.MESH

D_MODEL = 1024
CONV_CH = 512
CONV_WIDTH = 31
HEAD_DIM = 64
PAIR = 2 * HEAD_DIM
N_PAIRS = 4
D_FF = 2816
N_CHIPS = 4
IN_SH = 2560 // N_CHIPS
FF_SH = D_FF // N_CHIPS
OUT_SH = D_MODEL // N_CHIPS
C_ROWS = OUT_SH + FF_SH
EPS = 1e-6
HALO = 32

ADAM_LR = 0.001
ADAM_B1 = 0.9
ADAM_B2 = 0.999
ADAM_EPS = 1e-08
ADAM_WD = 0.01
ADAM_STEP = 10

VMEM_LIMIT = 56 * 2 ** 20
VMEM_LIMIT_ATTN_BWD = 60 * 2 ** 20


def _cp(sem=None, vmem=VMEM_LIMIT):
    return pltpu.CompilerParams(dimension_semantics=sem, vmem_limit_bytes=vmem)


def _hbm():
    return pl.BlockSpec(memory_space=pltpu.HBM)


def _const_spec(shape):
    nd = len(shape)
    return pl.BlockSpec(shape, lambda *_: (0,) * nd, pipeline_mode=pl.Buffered(1))


def _dot(a, b):
    return jnp.dot(a, b, preferred_element_type=F32)


def _dot_nt(a, b):
    return lax.dot_general(a, b, (((1,), (1,)), ((), ())), preferred_element_type=F32)


def _dot_tn(a, b):
    return lax.dot_general(a, b, (((0,), (0,)), ((), ())), preferred_element_type=F32)


def _split3(x):
    b0 = x.astype(BF16)
    r1 = x - b0.astype(F32)
    b1 = r1.astype(BF16)
    b2 = (r1 - b1.astype(F32)).astype(BF16)
    return b0, b1, b2


def _split2(x):
    hi = x.astype(BF16)
    lo = (x - hi.astype(F32)).astype(BF16)
    return hi, lo


def _sigmoid(x):
    return 1.0 / (1.0 + jnp.exp(-x))


def _head_mean(x, seg):
    b0, b1, b2 = _split3(x)
    return (_dot(b0, seg) + _dot(b1, seg) + _dot(b2, seg)) * (1.0 / HEAD_DIM)


def _seg_matrix(n):
    r = lax.broadcasted_iota(jnp.int32, (n, n), 0) // HEAD_DIM
    c = lax.broadcasted_iota(jnp.int32, (n, n), 1) // HEAD_DIM
    return (r == c).astype(BF16)


def _rms(x):
    return lax.rsqrt(jnp.mean(x * x, axis=-1, keepdims=True) + EPS)


def _rms_bwd(dy, n, r, g):
    dn = dy * g
    dx = r * (dn - n * jnp.mean(dn * n, axis=-1, keepdims=True))
    return dx, dy * n


class _GatherPlan:
    def __init__(self, srcs, outs, lead, ssem, rsem):
        self.srcs, self.outs, self.lead, self.ssem, self.rsem = srcs, outs, lead, ssem, rsem
        x, y, self.c = lax.axis_index("x"), lax.axis_index("y"), lax.axis_index("c")
        self.me = 2 * x + y
        self.sibling = (x, y, 1 - self.c)
        self.chips = [(1 - x, y), (x, 1 - y), (1 - x, 1 - y)]

    def _half(self, ref, i, h):
        if self.lead[i]:
            return ref.at[h]
        rows = ref.shape[0] // 2
        return ref.at[pl.ds(h * rows, rows)]

    def _ici(self, i, k, origin):
        return pltpu.make_async_remote_copy(
            src_ref=self._half(self.srcs[i], i, self.c), dst_ref=self._half(self.outs[i].at[origin], i, self.c),
            send_sem=self.ssem.at[6 * i + k], recv_sem=self.rsem.at[6 * i + k],
            device_id=(self.chips[k][0], self.chips[k][1], self.c), device_id_type=MESH)

    def _d2d(self, i, k, h):
        origin = 2 * self.chips[k][0] + self.chips[k][1]
        piece = self._half(self.outs[i].at[origin], i, h)
        return pltpu.make_async_remote_copy(
            src_ref=piece, dst_ref=piece, send_sem=self.ssem.at[6 * i + 3 + k],
            recv_sem=self.rsem.at[6 * i + 3 + k], device_id=self.sibling, device_id_type=MESH)

    def _each(self):
        return [(i, k) for i in range(len(self.srcs)) for k in range(3)]

    def start(self):
        for i, k in self._each():
            self._ici(i, k, self.me).start()

    def forward(self):
        for i, k in self._each():
            self._ici(i, k, 2 * self.chips[k][0] + self.chips[k][1]).wait_recv()
            self._d2d(i, k, self.c).start()

    def finish(self):
        for i, k in self._each():
            self._d2d(i, k, 1 - self.c).wait_recv()
        for i, k in self._each():
            self._ici(i, k, self.me).wait_send()
            self._d2d(i, k, self.c).wait_send()


def _gather_shapes(shards):
    return [jax.ShapeDtypeStruct((N_CHIPS,) + s.shape, s.dtype) for s in shards]


def _gather_weights(shards, lead):
    n = len(shards)

    def body(*refs):
        plan = _GatherPlan(refs[:n], refs[n:2 * n], lead, refs[2 * n], refs[2 * n + 1])
        plan.start()
        plan.forward()
        plan.finish()

    return pl.pallas_call(
        body, name="gather_weights", out_shape=_gather_shapes(shards),
        in_specs=[_hbm()] * n, out_specs=[_hbm()] * n,
        scratch_shapes=[pltpu.SemaphoreType.DMA((6 * n,)), pltpu.SemaphoreType.DMA((6 * n,))],
    )(*shards)


def _in_proj(x2, g1, wa, tm):
    S = x2.shape[0]

    def body(x_ref, g_ref, w_ref, a_ref, uc_ref, qkv_ref):
        x = x_ref[...]
        a = (x * _rms(x) * g_ref[...]).astype(BF16)
        a_ref[...] = a
        u = [_dot(a, w_ref[j]) for j in range(N_CHIPS)]
        uc_ref[:, 0:640] = u[0]
        uc_ref[:, 640:1024] = u[1][:, 0:384]
        qkv_ref[:, 0:256] = u[1][:, 384:640].astype(BF16)
        qkv_ref[:, 256:896] = u[2].astype(BF16)
        qkv_ref[:, 896:1536] = u[3].astype(BF16)

    return pl.pallas_call(
        body, name="in_proj", grid=(S // tm,),
        in_specs=[pl.BlockSpec((tm, D_MODEL), lambda i: (i, 0)), _const_spec((1, D_MODEL)),
                  _const_spec(wa.shape)],
        out_specs=[pl.BlockSpec((tm, D_MODEL), lambda i: (i, 0)),
                   pl.BlockSpec((tm, 2 * CONV_CH), lambda i: (i, 0)),
                   pl.BlockSpec((tm, 1536), lambda i: (i, 0))],
        out_shape=[jax.ShapeDtypeStruct((S, D_MODEL), BF16), jax.ShapeDtypeStruct((S, 2 * CONV_CH), F32),
                   jax.ShapeDtypeStruct((S, 1536), BF16)],
        compiler_params=_cp(("parallel",)),
    )(x2, g1, wa)


SUBLANES = 8


def _shift_copies(src_ref, sh_ref):
    rows = sh_ref.shape[1]
    for b in range(1, SUBLANES):
        sh_ref[b - 1] = src_ref[pl.ds(b, rows), :]


def _rows_at(src_ref, sh_ref, off, rows):
    a, b = divmod(off, SUBLANES)
    if b == 0:
        return src_ref[pl.ds(SUBLANES * a, rows), :]
    return sh_ref[b - 1, pl.ds(SUBLANES * a, rows), :]


def _conv_taps(cw_ref, src_ref, sh_ref, offs, rows):
    acc = None
    for w, off in enumerate(offs):
        term = cw_ref[w:w + 1, :] * _rows_at(src_ref, sh_ref, off, rows)
        acc = term if acc is None else acc + term
    return acc


def _glu(uc):
    return uc[:, :CONV_CH] * _sigmoid(uc[:, CONV_CH:])


def _conv_fwd(uc, cwf, cb, lg, lb, tm):
    S = uc.shape[0]
    hb = tm // HALO

    def body(uc_ref, prev_ref, cw_ref, cb_ref, lg_ref, lb_ref, out_ref, y_ref, glu_ref, sh_ref):
        i = pl.program_id(0)
        glu_ref[0:HALO, :] = jnp.where(i == 0, 0.0, _glu(prev_ref[...]))
        glu_ref[HALO:HALO + tm, :] = _glu(uc_ref[...])
        glu_ref[HALO + tm:HALO + tm + SUBLANES, :] = jnp.zeros((SUBLANES, CONV_CH), F32)
        _shift_copies(glu_ref, sh_ref)
        offs = [HALO - (CONV_WIDTH - 1) + w for w in range(CONV_WIDTH)]
        y = _conv_taps(cw_ref, glu_ref, sh_ref, offs, tm) + cb_ref[...]
        y_ref[...] = y
        mu = jnp.mean(y, axis=-1, keepdims=True)
        yc = y - mu
        rstd = lax.rsqrt(jnp.mean(yc * yc, axis=-1, keepdims=True) + EPS)
        ln = yc * rstd * lg_ref[...] + lb_ref[...]
        out_ref[...] = (ln * _sigmoid(ln)).astype(BF16)

    return pl.pallas_call(
        body, name="conv_fwd", grid=(S // tm,),
        in_specs=[pl.BlockSpec((tm, 2 * CONV_CH), lambda i: (i, 0)),
                  pl.BlockSpec((HALO, 2 * CONV_CH), lambda i: (jnp.maximum(i * hb - 1, 0), 0)),
                  _const_spec(cwf.shape), _const_spec((1, CONV_CH)), _const_spec((1, CONV_CH)),
                  _const_spec((1, CONV_CH))],
        out_specs=[pl.BlockSpec((tm, CONV_CH), lambda i: (i, 0))] * 2,
        out_shape=[jax.ShapeDtypeStruct((S, CONV_CH), BF16), jax.ShapeDtypeStruct((S, CONV_CH), F32)],
        scratch_shapes=[pltpu.VMEM((HALO + tm + SUBLANES, CONV_CH), F32),
                        pltpu.VMEM((SUBLANES - 1, HALO + tm, CONV_CH), F32)],
        compiler_params=_cp(("parallel",)),
    )(uc, uc, cwf, cb, lg, lb)


def _lane_mask(h):
    lane = lax.broadcasted_iota(jnp.int32, (1, PAIR), 1)
    return (lane >= HEAD_DIM * h) & (lane < HEAD_DIM * (h + 1))


def _neg_abs(x):
    bits = lax.bitcast_convert_type(x, jnp.uint32) | jnp.uint32(0x80000000)
    return lax.bitcast_convert_type(bits, F32)


def _tri_dot(x, m):
    return _dot(x.astype(BF16), m)


MASKED = -1e30
KEY_BLOCKS = 4


def _running_sums(x, m, reverse, start=None):
    t = m.shape[0]
    blocks = x.shape[1] // t
    order = range(blocks - 1, -1, -1) if reverse else range(blocks)
    out = [None] * blocks
    carry = start
    for b in order:
        xb = x[:, b * t:(b + 1) * t]
        cb = _tri_dot(xb, m)
        out[b] = cb if carry is None else cb + carry
        rs = jnp.sum(xb, axis=1, keepdims=True)
        carry = rs if carry is None else carry + rs
    return jnp.concatenate(out, axis=1), carry


def _sb_tile(z, r, m_suf):
    sp = jnp.maximum(z, 0.0) + jnp.log(1.0 + jnp.exp(_neg_abs(z)))
    c, rs = _running_sums(sp, m_suf, reverse=True, start=r)
    return jnp.exp(z - c), sp, rs


def _scores(qm, kt, mask):
    z = _dot_nt(qm, kt)
    return z if mask is None else jnp.where(mask, z, MASKED)


def _causal_mask(i, sb, t, tk):
    row = lax.broadcasted_iota(jnp.int32, (t, tk), 0) + i * t
    col = lax.broadcasted_iota(jnp.int32, (t, tk), 1) + sb * tk
    return col < row


def _sweep_plain(first, count, tile):
    def step(n, carry):
        tile(first + n)
        return carry

    lax.fori_loop(0, count + 1, step, 0)


def _sweep(first, count, down, fetch, load, work):
    lo, hi = (first - count, first) if down else (first, first + count)
    tile = lambda j: jnp.clip(first - j if down else first + j, lo, hi)
    fetch(first, 0, True)

    def step(n, carry):
        j = 2 * n
        vals = load(0)
        fetch(tile(j + 1), 1, False)
        work(tile(j), vals)
        vals = load(1)
        fetch(tile(j + 2), 0, False)
        work(tile(j + 1), vals)
        return carry

    lax.fori_loop(0, (count + 1) // 2, step, 0)

    @pl.when(lax.rem(count, 2) == 0)
    def _():
        work(tile(count), load(0))


def _suffix_matrix(t, prefix=False):
    row = lax.broadcasted_iota(jnp.int32, (t, t), 0)
    col = lax.broadcasted_iota(jnp.int32, (t, t), 1)
    return ((row <= col) if prefix else (row >= col)).astype(BF16)


def _attn_fwd(qkv, t, tk, shards, lead):
    S = qkv.shape[0]

    ng = len(shards)
    nq = S // t

    def body(*refs):
        q_ref, k_ref, v_ref = refs[:3]
        o_ref = refs[3 + ng]
        acc_ref, r_ref, z_buf, ssem, rsem = refs[4 + 2 * ng:]
        p = pl.program_id(0)
        i = pl.program_id(1)
        plan = _GatherPlan(refs[3:3 + ng], refs[4 + ng:4 + 2 * ng], lead, ssem, rsem)
        pl.when((p == 0) & (i == 0))(plan.start)
        pl.when((p == 1) & (i == 0))(plan.forward)
        last = (i * t + t - 1) // tk
        m_suf = _suffix_matrix(tk // KEY_BLOCKS)
        q = q_ref[...]
        hms = [_lane_mask(h) for h in range(2)]
        qms = [jnp.where(hm, q, 0) * 0.125 for hm in hms]
        acc_ref[...] = jnp.zeros_like(acc_ref)
        r_ref[...] = jnp.zeros_like(r_ref)

        def rows(sb):
            return pl.ds(pl.multiple_of(sb * tk, tk), tk)

        def fetch(sb, slot, diagonal):
            kt = k_ref[rows(sb), :]
            mask = _causal_mask(i, sb, t, tk) if diagonal else None
            for h in range(2):
                z_buf[slot, h] = _scores(qms[h], kt, mask)

        def load(slot):
            return [z_buf[slot, h] for h in range(2)]

        def work(sb, zs):
            vt = v_ref[rows(sb), :]
            for h in range(2):
                a_loc, _, rs = _sb_tile(zs[h], None, m_suf)
                r = r_ref[h]
                acc_ref[...] += _dot(a_loc.astype(BF16), jnp.where(hms[h], vt, 0)) * jnp.exp(-r)
                r_ref[h] = r + rs

        _sweep(last, last, True, fetch, load, work)
        o_ref[...] = acc_ref[...]
        pl.when((p == N_PAIRS - 1) & (i == nq - 1))(plan.finish)

    return pl.pallas_call(
        body, name="attn_fwd", grid=(N_PAIRS, nq),
        in_specs=[pl.BlockSpec((t, PAIR), lambda p, i: (i, p)),
                  pl.BlockSpec((S, PAIR), lambda p, i: (0, N_PAIRS + p)),
                  pl.BlockSpec((S, PAIR), lambda p, i: (0, 2 * N_PAIRS + p))] + [_hbm()] * ng,
        out_specs=[pl.BlockSpec((t, PAIR), lambda p, i: (i, p))] + [_hbm()] * ng,
        out_shape=[jax.ShapeDtypeStruct((S, N_PAIRS * PAIR), F32)] + _gather_shapes(shards),
        scratch_shapes=[pltpu.VMEM((t, PAIR), F32), pltpu.VMEM((2, t, 1), F32),
                        pltpu.VMEM((2, 2, t, tk), F32),
                        pltpu.SemaphoreType.DMA((6 * ng,)), pltpu.SemaphoreType.DMA((6 * ng,))],
        compiler_params=_cp(("arbitrary", "arbitrary")),
    )(qkv, qkv, qkv, *shards)


def _out_proj(conv_out, o, ag, wc, x2, g2, g3, tm):
    S = o.shape[0]

    def body(co_ref, o_ref, ag_ref, w_ref, x_ref, g2_ref, g3_ref, mix_ref, y_ref, h1_ref, fin_ref):
        seg = _seg_matrix(CONV_CH)
        o = o_ref[...]
        att = (o * lax.rsqrt(_head_mean(o * o, seg) + EPS) * ag_ref[...]).astype(BF16)
        co = co_ref[...]
        mix_ref[:, :CONV_CH] = co
        mix_ref[:, CONV_CH:] = att
        y = _dot(co, w_ref[0:CONV_CH, :]) + _dot(att, w_ref[CONV_CH:, :])
        y_ref[...] = y
        h1 = x_ref[...] + y * _rms(y) * g2_ref[...]
        h1_ref[...] = h1
        fin_ref[...] = (h1 * _rms(h1) * g3_ref[...]).astype(BF16)

    row = lambda w: pl.BlockSpec((tm, w), lambda i: (i, 0))
    return pl.pallas_call(
        body, name="out_proj", grid=(S // tm,),
        in_specs=[row(CONV_CH), row(CONV_CH), _const_spec((1, CONV_CH)), _const_spec(wc.shape),
                  row(D_MODEL), _const_spec((1, D_MODEL)), _const_spec((1, D_MODEL))],
        out_specs=[row(D_MODEL)] * 4,
        out_shape=[jax.ShapeDtypeStruct((S, D_MODEL), BF16), jax.ShapeDtypeStruct((S, D_MODEL), F32),
                   jax.ShapeDtypeStruct((S, D_MODEL), F32), jax.ShapeDtypeStruct((S, D_MODEL), BF16)],
        compiler_params=_cp(("parallel",)),
    )(conv_out, o, ag, wc, x2, g2, g3)


def _ffn_fwd(f_in, h1, tgt, wg, wu, wd, g4, tm):
    S = f_in.shape[0]

    def body(fin_ref, h1_ref, tgt_ref, wg_ref, wu_ref, wd_ref, g4_ref, df_ref, dh2_ref, dg4_ref, loss_ref,
             gt_ref, up_ref, act_ref):
        i = pl.program_id(0)
        fin = fin_ref[...]
        gt = _dot(fin, wg_ref[...])
        up = _dot(fin, wu_ref[...])
        act = (gt * _sigmoid(gt) * up).astype(BF16)
        gt_ref[...] = gt.astype(BF16)
        up_ref[...] = up.astype(BF16)
        act_ref[...] = act
        f = _dot(act, wd_ref[...])
        r = _rms(f)
        n = f * r
        g4 = g4_ref[...]
        err = h1_ref[...] + n * g4 - tgt_ref[...]
        dh2 = err * (1.0 / D_MODEL)
        dh2_ref[...] = dh2
        df, dg = _rms_bwd(dh2, n, r, g4)
        df_ref[...] = df.astype(BF16)

        @pl.when(i == 0)
        def _():
            dg4_ref[...] = jnp.zeros_like(dg4_ref)
            loss_ref[...] = jnp.zeros_like(loss_ref)

        dg4_ref[...] += jnp.sum(dg, axis=0, keepdims=True)
        part = jnp.sum(jnp.sum(err * err, axis=1, keepdims=True), axis=0, keepdims=True)
        loss_ref[...] += part * (0.5 / D_MODEL)

    row = lambda w: pl.BlockSpec((tm, w), lambda i: (i, 0))
    return pl.pallas_call(
        body, name="ffn_fwd", grid=(S // tm,),
        in_specs=[row(D_MODEL), row(D_MODEL), row(D_MODEL), _const_spec(wg.shape), _const_spec(wu.shape),
                  _const_spec(wd.shape), _const_spec((1, D_MODEL))],
        out_specs=[row(D_MODEL), row(D_MODEL), pl.BlockSpec((1, D_MODEL), lambda i: (0, 0)),
                   pl.BlockSpec((1, 128), lambda i: (0, 0)), row(D_FF), row(D_FF), row(D_FF)],
        out_shape=[jax.ShapeDtypeStruct((S, D_MODEL), BF16), jax.ShapeDtypeStruct((S, D_MODEL), F32),
                   jax.ShapeDtypeStruct((1, D_MODEL), F32), jax.ShapeDtypeStruct((1, 128), F32)]
        + [jax.ShapeDtypeStruct((S, D_FF), BF16)] * 3,
        compiler_params=_cp(("arbitrary",)),
    )(f_in, h1, tgt, wg, wu, wd, g4)


def _ffn_bwd(gt_bf, up_bf, df, dh2, h1, yv, wg, wu, wd, g3, g2, tm):
    S = df.shape[0]

    def body(gt_ref, up_ref, df_ref, dh2_ref, h1_ref, y_ref, wg_ref, wu_ref, wd_ref, g3_ref, g2_ref,
             dgt_ref, dup_ref, dh1_ref, dy_ref, dg3_ref, dg2_ref):
        i = pl.program_id(0)
        df = df_ref[...]
        gt = gt_ref[...].astype(F32)
        up = up_ref[...].astype(F32)
        sg = _sigmoid(gt)
        silu = gt * sg
        dact = _dot_nt(df, wd_ref[...])
        dgt = (dact * up * (sg * (1.0 + gt * (1.0 - sg)))).astype(BF16)
        dup = (dact * silu).astype(BF16)
        dgt_ref[...] = dgt
        dup_ref[...] = dup
        dfin = _dot_nt(dgt, wg_ref[...]) + _dot_nt(dup, wu_ref[...])
        h1 = h1_ref[...]
        r3 = _rms(h1)
        dh1_n, dg3 = _rms_bwd(dfin, h1 * r3, r3, g3_ref[...])
        dh1 = dh2_ref[...] + dh1_n
        dh1_ref[...] = dh1
        y = y_ref[...]
        r2 = _rms(y)
        dy, dg2 = _rms_bwd(dh1, y * r2, r2, g2_ref[...])
        dy_ref[...] = dy.astype(BF16)

        @pl.when(i == 0)
        def _():
            dg3_ref[...] = jnp.zeros_like(dg3_ref)
            dg2_ref[...] = jnp.zeros_like(dg2_ref)

        dg3_ref[...] += jnp.sum(dg3, axis=0, keepdims=True)
        dg2_ref[...] += jnp.sum(dg2, axis=0, keepdims=True)

    row = lambda w: pl.BlockSpec((tm, w), lambda i: (i, 0))
    vec = pl.BlockSpec((1, D_MODEL), lambda i: (0, 0))
    return pl.pallas_call(
        body, name="ffn_bwd", grid=(S // tm,),
        in_specs=[row(D_FF), row(D_FF)] + [row(D_MODEL)] * 4
        + [_const_spec(wg.shape), _const_spec(wu.shape), _const_spec(wd.shape),
           _const_spec((1, D_MODEL)), _const_spec((1, D_MODEL))],
        out_specs=[row(D_FF), row(D_FF), row(D_MODEL), row(D_MODEL), vec, vec],
        out_shape=[jax.ShapeDtypeStruct((S, D_FF), BF16)] * 2
        + [jax.ShapeDtypeStruct((S, D_MODEL), F32), jax.ShapeDtypeStruct((S, D_MODEL), BF16),
           jax.ShapeDtypeStruct((1, D_MODEL), F32), jax.ShapeDtypeStruct((1, D_MODEL), F32)],
        compiler_params=_cp(("arbitrary",)),
    )(gt_bf, up_bf, df, dh2, h1, yv, wg, wu, wd, g3, g2)


def _out_bwd(dy, o, ag, wc, tm):
    S = o.shape[0]

    def body(dy_ref, o_ref, ag_ref, w_ref, dco_ref, do_ref, dag_ref):
        i = pl.program_id(0)
        seg = _seg_matrix(CONV_CH)
        dy = dy_ref[...]
        dco_ref[...] = _dot_nt(dy, w_ref[0:CONV_CH, :])
        datt = _dot_nt(dy, w_ref[CONV_CH:, :])
        o = o_ref[...]
        r = lax.rsqrt(_head_mean(o * o, seg) + EPS)
        n = o * r
        dn = datt * ag_ref[...]
        do_ref[...] = (r * (dn - n * _head_mean(dn * n, seg))).astype(BF16)

        @pl.when(i == 0)
        def _():
            dag_ref[...] = jnp.zeros_like(dag_ref)

        dag_ref[...] += jnp.sum(datt * n, axis=0, keepdims=True)

    row = lambda w: pl.BlockSpec((tm, w), lambda i: (i, 0))
    return pl.pallas_call(
        body, name="out_bwd", grid=(S // tm,),
        in_specs=[row(D_MODEL), row(CONV_CH), _const_spec((1, CONV_CH)), _const_spec(wc.shape)],
        out_specs=[row(CONV_CH), row(CONV_CH), pl.BlockSpec((1, CONV_CH), lambda i: (0, 0))],
        out_shape=[jax.ShapeDtypeStruct((S, CONV_CH), F32), jax.ShapeDtypeStruct((S, CONV_CH), BF16),
                   jax.ShapeDtypeStruct((1, CONV_CH), F32)],
        compiler_params=_cp(("arbitrary",)),
    )(dy, o, ag, wc)


def _attn_bwd(qkv, do, t, parts):
    S = qkv.shape[0]
    tk = KEY_BLOCKS * t
    nk = S // tk
    ns = len(parts)

    def body(*refs):
        q_ref, k_ref, v_ref, do_ref = refs[:4]
        dq_ref, dk_hbm, dv_hbm = refs[4 + ns:7 + ns]
        g_buf, s_buf, r_ref, dq_acc, dk_ref, dv_ref, z_buf, da_buf = refs[7 + 2 * ns:15 + 2 * ns]
        p = pl.program_id(0)
        i = pl.program_id(1)
        plan = _ScatterPlan(refs[4:4 + ns], refs[7 + ns:7 + 2 * ns], *refs[15 + 2 * ns:])
        pl.when((p == 0) & (i == 0))(plan.start)
        last = (i * t + t - 1) // tk

        @pl.when(i == 0)
        def _():
            dk_ref[...] = jnp.zeros_like(dk_ref)
            dv_ref[...] = jnp.zeros_like(dv_ref)

        m_suf = _suffix_matrix(t)
        m_pre = _suffix_matrix(t, prefix=True)
        q = q_ref[...]
        do = do_ref[...]
        hms = [_lane_mask(h) for h in range(2)]
        qms = [jnp.where(hm, q, 0) * 0.125 for hm in hms]
        doms = [jnp.where(hm, do, 0) for hm in hms]
        dq_acc[...] = jnp.zeros_like(dq_acc)
        r_ref[...] = jnp.zeros_like(r_ref)

        def rows(sb):
            return pl.ds(pl.multiple_of(sb * tk, tk), tk)

        def fetch1(sb, slot, diagonal):
            kt = k_ref[rows(sb), :]
            vt = v_ref[rows(sb), :]
            mask = _causal_mask(i, sb, t, tk) if diagonal else None
            for h in range(2):
                z_buf[slot, h] = _scores(qms[h], kt, mask)
                da_buf[slot, h] = _dot_nt(doms[h], vt)

        def load1(slot):
            return [(z_buf[slot, h], da_buf[slot, h]) for h in range(2)]

        def work1(sb, vals):
            dv = jnp.zeros((tk, PAIR), F32)
            for h in range(2):
                z, da = vals[h]
                A, sp, r_ref[h] = _sb_tile(z, r_ref[h], m_suf)
                g_buf[h, sb] = A * da
                s_buf[h, sb] = 1.0 - jnp.exp(-sp)
                dv = dv + _dot_tn(A.astype(BF16), doms[h])
            dv_ref[rows(sb), :] += dv

        _sweep(last, last, True, fetch1, load1, work1)
        r_ref[...] = jnp.zeros_like(r_ref)

        def sweep2(sb):
            kt = k_ref[rows(sb), :]
            dk = jnp.zeros((tk, PAIR), F32)
            for h in range(2):
                g = g_buf[h, sb]
                pre, r_ref[h] = _running_sums(g, m_pre, reverse=False, start=r_ref[h])
                dzb = (g - s_buf[h, sb] * pre).astype(BF16)
                dq_acc[...] += _dot(dzb, jnp.where(hms[h], kt, 0))
                dk = dk + _dot_tn(dzb, qms[h])
            dk_ref[rows(sb), :] += dk

        _sweep_plain(0, last, sweep2)
        dq_ref[...] = dq_acc[...] * 0.125

        @pl.when(i == S // t - 1)
        def _():
            cols = pl.ds(pl.multiple_of(p * PAIR, PAIR), PAIR)
            pltpu.sync_copy(dk_ref, dk_hbm.at[:, cols])
            pltpu.sync_copy(dv_ref, dv_hbm.at[:, cols])

        pl.when((p == N_PAIRS - 1) & (i == S // t - 1))(plan.finish)

    once = lambda cb: pl.BlockSpec((S, PAIR), cb, pipeline_mode=pl.Buffered(1))
    return pl.pallas_call(
        body, name="attn_bwd", grid=(N_PAIRS, S // t),
        in_specs=[pl.BlockSpec((t, PAIR), lambda p, i: (i, p)),
                  once(lambda p, i: (0, N_PAIRS + p)), once(lambda p, i: (0, 2 * N_PAIRS + p)),
                  pl.BlockSpec((t, PAIR), lambda p, i: (i, p))] + [_hbm()] * ns,
        out_specs=[pl.BlockSpec((t, PAIR), lambda p, i: (i, p)), _hbm(), _hbm()] + [_hbm()] * ns,
        out_shape=[jax.ShapeDtypeStruct((S, N_PAIRS * PAIR), F32)] * 3
        + [jax.ShapeDtypeStruct(pt.shape, pt.dtype) for pt in parts],
        scratch_shapes=[pltpu.VMEM((2, nk, t, tk), F32), pltpu.VMEM((2, nk, t, tk), F32),
                        pltpu.VMEM((2, t, 1), F32), pltpu.VMEM((t, PAIR), F32),
                        pltpu.VMEM((S, PAIR), F32), pltpu.VMEM((S, PAIR), F32),
                        pltpu.VMEM((2, 2, t, tk), F32), pltpu.VMEM((2, 2, t, tk), F32)] + _scatter_sems(ns),
        compiler_params=_cp(("arbitrary", "arbitrary"), vmem=VMEM_LIMIT_ATTN_BWD),
    )(qkv, qkv, qkv, do, *parts)


def _conv_bwd(uc, yconv, dco, cwf, lg, lb, tm):
    S = uc.shape[0]
    hb = tm // HALO
    nb = S // tm
    ext = tm + HALO

    def body(uc_ref, prev_ref, y_ref, ynext_ref, dco_ref, dnext_ref, cw_ref, lg_ref, lb_ref,
             duc_ref, dcw_ref, dcb_ref, dlg_ref, dlb_ref, glu_ref, dyc_ref, shg_ref, shd_ref):
        i = pl.program_id(0)
        last = i == nb - 1

        @pl.when(i == 0)
        def _():
            for ref in (dcw_ref, dcb_ref, dlg_ref, dlb_ref):
                ref[...] = jnp.zeros_like(ref)

        uc = uc_ref[...]
        glu_ref[0:HALO, :] = jnp.where(i == 0, 0.0, _glu(prev_ref[...]))
        glu_ref[HALO:ext, :] = _glu(uc)
        glu_ref[ext:ext + SUBLANES, :] = jnp.zeros((SUBLANES, CONV_CH), F32)
        _shift_copies(glu_ref, shg_ref)
        fwd_offs = [HALO - (CONV_WIDTH - 1) + w for w in range(CONV_WIDTH)]
        y = jnp.concatenate([y_ref[...], ynext_ref[...]], axis=0)
        mu = jnp.mean(y, axis=-1, keepdims=True)
        yc = y - mu
        rstd = lax.rsqrt(jnp.mean(yc * yc, axis=-1, keepdims=True) + EPS)
        yhat = yc * rstd
        lg = lg_ref[...]
        ln = yhat * lg + lb_ref[...]
        sg = _sigmoid(ln)
        dout = jnp.concatenate([dco_ref[...], jnp.where(last, 0.0, dnext_ref[...])], axis=0)
        dln = dout * (sg * (1.0 + ln * (1.0 - sg)))
        dyh = dln * lg
        dyc = rstd * (dyh - jnp.mean(dyh, axis=-1, keepdims=True)
                      - yhat * jnp.mean(dyh * yhat, axis=-1, keepdims=True))
        dyc_ref[0:ext, :] = dyc
        dyc_ref[ext:ext + SUBLANES, :] = jnp.zeros((SUBLANES, CONV_CH), F32)
        _shift_copies(dyc_ref, shd_ref)
        dlg_ref[...] += jnp.sum((dln * yhat)[0:tm], axis=0, keepdims=True)
        dlb_ref[...] += jnp.sum(dln[0:tm], axis=0, keepdims=True)
        dcb_ref[...] += jnp.sum(dyc[0:tm], axis=0, keepdims=True)
        dglu = _conv_taps(cw_ref, dyc_ref, shd_ref, [CONV_WIDTH - 1 - w for w in range(CONV_WIDTH)], tm)
        d0 = dyc[0:tm]
        for w, off in enumerate(fwd_offs):
            dcw_ref[w:w + 1, :] += jnp.sum(d0 * _rows_at(glu_ref, shg_ref, off, tm), axis=0, keepdims=True)
        val, gate = uc[:, :CONV_CH], uc[:, CONV_CH:]
        sgate = _sigmoid(gate)
        duc_ref[:, :CONV_CH] = (dglu * sgate).astype(BF16)
        duc_ref[:, CONV_CH:] = (dglu * val * sgate * (1.0 - sgate)).astype(BF16)

    vec = pl.BlockSpec((1, CONV_CH), lambda i: (0, 0))
    nxt = lambda i: (jnp.minimum((i + 1) * hb, S // HALO - 1), 0)
    return pl.pallas_call(
        body, name="conv_bwd", grid=(nb,),
        in_specs=[pl.BlockSpec((tm, 2 * CONV_CH), lambda i: (i, 0)),
                  pl.BlockSpec((HALO, 2 * CONV_CH), lambda i: (jnp.maximum(i * hb - 1, 0), 0)),
                  pl.BlockSpec((tm, CONV_CH), lambda i: (i, 0)), pl.BlockSpec((HALO, CONV_CH), nxt),
                  pl.BlockSpec((tm, CONV_CH), lambda i: (i, 0)), pl.BlockSpec((HALO, CONV_CH), nxt),
                  _const_spec(cwf.shape), _const_spec((1, CONV_CH)), _const_spec((1, CONV_CH))],
        out_specs=[pl.BlockSpec((tm, 2 * CONV_CH), lambda i: (i, 0)),
                   pl.BlockSpec(cwf.shape, lambda i: (0, 0)), vec, vec, vec],
        out_shape=[jax.ShapeDtypeStruct((S, 2 * CONV_CH), BF16), jax.ShapeDtypeStruct(cwf.shape, F32)]
        + [jax.ShapeDtypeStruct((1, CONV_CH), F32)] * 3,
        scratch_shapes=[pltpu.VMEM((ext + SUBLANES, CONV_CH), F32), pltpu.VMEM((ext + SUBLANES, CONV_CH), F32),
                        pltpu.VMEM((SUBLANES - 1, ext, CONV_CH), F32),
                        pltpu.VMEM((SUBLANES - 1, ext, CONV_CH), F32)],
        compiler_params=_cp(("arbitrary",)),
    )(uc, uc, yconv, yconv, dco, dco, cwf, lg, lb)


def _in_bwd(duc, dq, dk, dv, x2, dh1, g1, wa, tm):
    S = x2.shape[0]

    def body(duc_ref, dq_ref, dk_ref, dv_ref, x_ref, dh1_ref, g_ref, w_ref, gx_ref, du_ref, dg_ref):
        i = pl.program_id(0)
        du = jnp.concatenate([duc_ref[...], dq_ref[...].astype(BF16), dk_ref[...].astype(BF16),
                              dv_ref[...].astype(BF16)], axis=1)
        du_ref[...] = du
        da = _dot_nt(du[:, 0:IN_SH], w_ref[0])
        for j in range(1, N_CHIPS):
            da = da + _dot_nt(du[:, IN_SH * j:IN_SH * (j + 1)], w_ref[j])
        x = x_ref[...]
        r = _rms(x)
        dx, dg = _rms_bwd(da, x * r, r, g_ref[...])
        gx_ref[...] = dh1_ref[...] + dx

        @pl.when(i == 0)
        def _():
            dg_ref[...] = jnp.zeros_like(dg_ref)

        dg_ref[...] += jnp.sum(dg, axis=0, keepdims=True)

    row = lambda w: pl.BlockSpec((tm, w), lambda i: (i, 0))
    return pl.pallas_call(
        body, name="in_bwd", grid=(S // tm,),
        in_specs=[row(2 * CONV_CH), row(CONV_CH), row(CONV_CH), row(CONV_CH), row(D_MODEL), row(D_MODEL),
                  _const_spec((1, D_MODEL)), _const_spec(wa.shape)],
        out_specs=[pl.BlockSpec((None, tm, D_MODEL), lambda i: (0, i, 0)), row(2560),
                   pl.BlockSpec((1, D_MODEL), lambda i: (0, 0))],
        out_shape=[jax.ShapeDtypeStruct((1, S, D_MODEL), F32), jax.ShapeDtypeStruct((S, 2560), BF16),
                   jax.ShapeDtypeStruct((1, D_MODEL), F32)],
        compiler_params=_cp(("arbitrary",)),
    )(duc, dq, dk, dv, x2, dh1, g1, wa)


def _matmul_tn(xm, ym, tm, ts, name, column_block=None):
    S, M = xm.shape
    N = ym.shape[1]

    def body(x_ref, y_ref, o_ref):
        @pl.when(pl.program_id(1) == 0)
        def _():
            o_ref[...] = jnp.zeros_like(o_ref)

        xt = x_ref[...].T
        if column_block is None:
            o_ref[...] += _dot(xt, y_ref[...])
        else:
            for j in range(N // column_block):
                o_ref[j] += _dot(xt, y_ref[:, column_block * j:column_block * (j + 1)])

    if column_block is None:
        out_spec = pl.BlockSpec((tm, N), lambda m, s: (m, 0))
        out_shape = jax.ShapeDtypeStruct((M, N), F32)
    else:
        out_spec = pl.BlockSpec((N // column_block, tm, column_block), lambda m, s: (0, m, 0))
        out_shape = jax.ShapeDtypeStruct((N // column_block, M, column_block), F32)
    return pl.pallas_call(
        body, name=name, grid=(M // tm, S // ts),
        in_specs=[pl.BlockSpec((ts, tm), lambda m, s: (s, m)), pl.BlockSpec((ts, N), lambda m, s: (s, 0))],
        out_specs=out_spec, out_shape=out_shape,
        compiler_params=_cp(("parallel", "arbitrary")),
    )(xm, ym)


def _sibling_halves(grads, name):
    n = len(grads)

    def body(*refs):
        ins, outs, ssem, rsem = refs[:n], refs[n:2 * n], refs[2 * n], refs[2 * n + 1]
        x, y, c = lax.axis_index("x"), lax.axis_index("y"), lax.axis_index("c")
        copies = []
        for k in range(n):
            for j in range(N_CHIPS):
                copies.append(pltpu.make_async_remote_copy(
                    src_ref=ins[k].at[j, 1 - c], dst_ref=outs[k].at[j],
                    send_sem=ssem.at[N_CHIPS * k + j], recv_sem=rsem.at[N_CHIPS * k + j],
                    device_id=(x, y, 1 - c), device_id_type=MESH))
        for cp in copies:
            cp.start()
        for cp in copies:
            cp.wait()

    shapes = [jax.ShapeDtypeStruct((g.shape[0],) + g.shape[2:], F32) for g in grads]
    return pl.pallas_call(
        body, name=name, out_shape=shapes,
        in_specs=[_hbm()] * n, out_specs=[_hbm()] * n,
        scratch_shapes=[pltpu.SemaphoreType.DMA((N_CHIPS * n,)), pltpu.SemaphoreType.DMA((N_CHIPS * n,))],
    )(*grads)


def _add_half(c_arr, g, landed, name):
    def body(c_ref, g_ref, l_ref, o_ref):
        o_ref[...] = (g_ref[...] + l_ref[...]).astype(BF16)

    rows, n = g.shape[2], g.shape[3]
    grid = (N_CHIPS,)
    g_spec = pl.BlockSpec((None, None, rows, n), lambda j, c: (j, c[0], 0, 0))
    l_spec = pl.BlockSpec((None, rows, n), lambda j, c: (j, 0, 0))
    return pl.pallas_call(
        body, name=name,
        grid_spec=pltpu.PrefetchScalarGridSpec(num_scalar_prefetch=1, grid=grid, in_specs=[g_spec, l_spec],
                                               out_specs=l_spec),
        out_shape=jax.ShapeDtypeStruct(landed.shape, BF16),
        compiler_params=_cp(("parallel",)),
    )(c_arr, g, landed)


class _ScatterPlan:
    def __init__(self, ins, outs, lsem, ssem, rsem):
        x, y, c = lax.axis_index("x"), lax.axis_index("y"), lax.axis_index("c")
        me = 2 * x + y
        self.copies = []
        for k in range(len(ins)):
            self.copies.append(pltpu.make_async_copy(ins[k].at[me], outs[k].at[me], lsem.at[k]))
            for r, chip in enumerate([(1 - x, y), (x, 1 - y), (1 - x, 1 - y)]):
                self.copies.append(pltpu.make_async_remote_copy(
                    src_ref=ins[k].at[2 * chip[0] + chip[1]], dst_ref=outs[k].at[me],
                    send_sem=ssem.at[3 * k + r], recv_sem=rsem.at[3 * k + r],
                    device_id=(chip[0], chip[1], c), device_id_type=MESH))

    def start(self):
        for cp in self.copies:
            cp.start()

    def finish(self):
        for cp in self.copies:
            cp.wait()


def _scatter_sems(n):
    return [pltpu.SemaphoreType.DMA((n,)), pltpu.SemaphoreType.DMA((3 * n,)), pltpu.SemaphoreType.DMA((3 * n,))]


def _chip_scatter(parts):
    n = len(parts)

    def body(*refs):
        plan = _ScatterPlan(refs[:n], refs[n:2 * n], *refs[2 * n:])
        plan.start()
        plan.finish()

    shapes = [jax.ShapeDtypeStruct(p.shape, p.dtype) for p in parts]
    return pl.pallas_call(
        body, name="grad_chip_scatter", out_shape=shapes,
        in_specs=[_hbm()] * n, out_specs=[_hbm()] * n, scratch_shapes=_scatter_sems(n),
    )(*parts)


def _sum_chips(landed, name):
    _, rows, n = landed.shape
    tr = 256 if rows % 256 == 0 else rows

    def body(a_ref, b_ref, c_ref, d_ref, o_ref):
        f = lambda ref: ref[...].astype(F32)
        o_ref[...] = ((f(a_ref) + f(b_ref)) + f(c_ref)) + f(d_ref)

    specs = [pl.BlockSpec((None, tr, n), functools.partial(lambda i, j: (j, i, 0), j=j)) for j in range(N_CHIPS)]
    return pl.pallas_call(
        body, name=name, grid=(rows // tr,), in_specs=specs,
        out_specs=pl.BlockSpec((tr, n), lambda i: (i, 0)),
        out_shape=jax.ShapeDtypeStruct((rows, n), F32),
        compiler_params=_cp(("parallel",)),
    )(landed, landed, landed, landed)


def _share_halves(halves):
    n = len(halves)

    def body(*refs):
        ins, outs = refs[:n], refs[n:2 * n]
        ssem, rsem = refs[2 * n:]
        x, y, c = lax.axis_index("x"), lax.axis_index("y"), lax.axis_index("c")
        copies = [pltpu.make_async_remote_copy(
            src_ref=ins[k], dst_ref=outs[k], send_sem=ssem.at[k], recv_sem=rsem.at[k],
            device_id=(x, y, 1 - c), device_id_type=MESH) for k in range(n)]
        for cp in copies:
            cp.start()
        for cp in copies:
            cp.wait()

    shapes = [jax.ShapeDtypeStruct(h.shape, F32) for h in halves]
    return pl.pallas_call(
        body, name="grad_share_halves", out_shape=shapes,
        in_specs=[_hbm()] * n, out_specs=[_hbm()] * n,
        scratch_shapes=[pltpu.SemaphoreType.DMA((n,)), pltpu.SemaphoreType.DMA((n,))],
    )(*halves)


def _allreduce_small(packed):
    rows, n = packed.shape

    def body(in_ref, out_ref, land_ref, ssem, rsem):
        x, y, c = lax.axis_index("x"), lax.axis_index("y"), lax.axis_index("c")
        me = 4 * x + 2 * y + c
        land_ref[me] = in_ref[...]
        copies = []
        for r in range(1, 8):
            tx = 1 - x if r & 4 else x
            ty = 1 - y if r & 2 else y
            tc = 1 - c if r & 1 else c
            cp = pltpu.make_async_remote_copy(
                src_ref=in_ref, dst_ref=land_ref.at[me], send_sem=ssem.at[r - 1], recv_sem=rsem.at[r - 1],
                device_id=(tx, ty, tc), device_id_type=MESH)
            cp.start()
            copies.append(cp)
        for cp in copies:
            cp.wait()
        acc = land_ref[0]
        for k in range(1, 8):
            acc = acc + land_ref[k]
        out_ref[...] = acc

    return pl.pallas_call(
        body, name="allreduce_small", out_shape=jax.ShapeDtypeStruct((rows, n), F32),
        in_specs=[pl.BlockSpec(memory_space=pltpu.VMEM)], out_specs=pl.BlockSpec(memory_space=pltpu.VMEM),
        scratch_shapes=[pltpu.VMEM((8, rows, n), F32), pltpu.SemaphoreType.DMA((7,)),
                        pltpu.SemaphoreType.DMA((7,))],
    )(packed)


def _adamw_math(w, g, m, v):
    m = ADAM_B1 * m + (1.0 - ADAM_B1) * g
    v = ADAM_B2 * v + (1.0 - ADAM_B2) * (g * g)
    m_hat = m / (1.0 - ADAM_B1 ** ADAM_STEP)
    v_hat = v / (1.0 - ADAM_B2 ** ADAM_STEP)
    return -ADAM_LR * (m_hat / (jnp.sqrt(v_hat) + ADAM_EPS) + ADAM_WD * w), m, v


def _adamw_halves(c_arr, w, mine, other, m, v, name):
    rows, n = mine.shape
    tr = 256 if rows % 256 == 0 else rows
    nb = rows // tr

    def body(c_ref, w_ref, a_ref, b_ref, m_ref, v_ref, g_ref, d_ref, mo_ref, vo_ref):
        g = jnp.where(pl.program_id(0) == c_ref[0], a_ref[...], b_ref[...])
        g_ref[...] = g
        d_ref[...], mo_ref[...], vo_ref[...] = _adamw_math(w_ref[...], g, m_ref[...], v_ref[...])

    full = pl.BlockSpec((None, tr, n), lambda h, i, c: (0, h * nb + i, 0))
    half = pl.BlockSpec((tr, n), lambda h, i, c: (i, 0))
    return pl.pallas_call(
        body, name=name,
        grid_spec=pltpu.PrefetchScalarGridSpec(num_scalar_prefetch=1, grid=(2, nb),
                                               in_specs=[full, half, half, full, full], out_specs=[full] * 4),
        out_shape=[jax.ShapeDtypeStruct((1, 2 * rows, n), F32)] * 4,
        compiler_params=_cp(("parallel", "parallel")),
    )(c_arr, w, mine, other, m, v)


def _adamw(w, g, m, v, name):
    rows, n = w.shape
    tr = 256 if rows % 256 == 0 else rows

    def body(w_ref, g_ref, m_ref, v_ref, d_ref, mo_ref, vo_ref):
        d_ref[...], mo_ref[...], vo_ref[...] = _adamw_math(w_ref[...], g_ref[...], m_ref[...], v_ref[...])

    spec = pl.BlockSpec((tr, n), lambda i: (i, 0))
    return pl.pallas_call(
        body, name=name, grid=(rows // tr,), in_specs=[spec] * 4, out_specs=[spec] * 3,
        out_shape=[jax.ShapeDtypeStruct((rows, n), F32)] * 3,
        compiler_params=_cp(("parallel",)),
    )(w, g, m, v)


def _rows8(a):
    a = a.reshape(-1, 128)
    return jnp.pad(a, ((0, (-a.shape[0]) % 8), (0, 0)))


def kernel(x, g_pre_mix, w_in, conv_w, conv_b, conv_ln_g, conv_ln_b, attn_norm_g, w_out, g_post_mix, g_pre_ffn, w_gate, w_up, w_down, g_post_ffn, loss_target, m_g_pre_mix, m_w_in, m_conv_w, m_conv_b, m_conv_ln_g, m_conv_ln_b, m_attn_norm_g, m_w_out, m_g_post_mix, m_g_pre_ffn, m_w_gate, m_w_up, m_w_down, m_g_post_ffn, v_g_pre_mix, v_w_in, v_conv_w, v_conv_b, v_conv_ln_g, v_conv_ln_b, v_attn_norm_g, v_w_out, v_g_post_mix, v_g_pre_ffn, v_w_gate, v_w_up, v_w_down, v_g_post_ffn):
    S = x.shape[1]
    tm_big = min(512, S)
    tm_ffn = min(256, S)
    tk_att = min(1024, S)
    t_att = tk_att // KEY_BLOCKS
    t_att_fwd = min(4 * t_att, S)
    chip = 2 * lax.axis_index("x") + lax.axis_index("y")
    core = lax.axis_index("c")
    x2 = x.reshape(S, D_MODEL)
    tgt = loss_target.reshape(S, D_MODEL)
    ag = attn_norm_g.reshape(1, CONV_CH)

    a_sh = w_in[0].astype(BF16)
    b_sh = jnp.stack([w_gate[0], w_up[0]]).astype(BF16)
    c_sh = jnp.concatenate([w_out[0], w_down[0]], axis=0).astype(BF16)
    cw_sh = jnp.pad(conv_w[0, :, 0, :], ((0, 1), (0, 0)))
    own = lambda full, shard: lax.dynamic_update_index_in_dim(full, shard, chip, 0)
    cols = lambda w4: jnp.transpose(w4, (1, 0, 2)).reshape(w4.shape[1], N_CHIPS * w4.shape[2])
    wa4, cw4 = _gather_weights([a_sh, cw_sh], [False, False])
    wa = own(wa4, a_sh)
    cwf = cols(own(cw4, cw_sh))

    a_bf, uc, qkv = _in_proj(x2, g_pre_mix, wa, tm_big)
    conv_out, yconv = _conv_fwd(uc, cwf, conv_b, conv_ln_g, conv_ln_b, tm_big)
    o, wb4, wc4 = _attn_fwd(qkv, t_att_fwd, tk_att, [b_sh, c_sh], [True, False])
    wb4, wc4 = own(wb4, b_sh), own(wc4, c_sh)
    wg, wu = cols(wb4[:, 0]), cols(wb4[:, 1])
    wo = wc4[:, :OUT_SH].reshape(D_MODEL, D_MODEL)
    wd = wc4[:, OUT_SH:].reshape(D_FF, D_MODEL)
    mixed, yv, h1, f_in = _out_proj(conv_out, o, ag, wo, x2, g_post_mix, g_pre_ffn, tm_big)
    df, dh2, dg4, loss_part, gt_bf, up_bf, act = _ffn_fwd(f_in, h1, tgt, wg, wu, wd, g_post_ffn, tm_ffn)

    dgt, dup, dh1, dy, dg3, dg2 = _ffn_bwd(gt_bf, up_bf, df, dh2, h1, yv, wg, wu, wd, g_pre_ffn, g_post_mix, tm_ffn)
    dco, do, dag = _out_bwd(dy, o, ag, wo, tm_big)
    ts = min(512, S)
    gw_out = _matmul_tn(mixed, dy, D_MODEL, ts, "grad_w_out")
    gw_gate = _matmul_tn(f_in, dgt, D_MODEL, ts, "grad_w_gate")
    gw_up = _matmul_tn(f_in, dup, D_MODEL, ts, "grad_w_up")
    gw_down = _matmul_tn(act, df, D_FF // 2, ts, "grad_w_down")

    by_cols = lambda g: jnp.transpose(g.reshape(2, D_MODEL // 2, N_CHIPS, -1), (2, 0, 1, 3))
    by_rows = lambda g: g.reshape(N_CHIPS, 2, g.shape[0] // (2 * N_CHIPS), g.shape[1])
    c_arr = core.reshape(1).astype(jnp.int32)

    def chip_partials(views, nms):
        landed = _sibling_halves(views, "grad_sibling_halves_" + nms[0])
        return [_add_half(c_arr, g, l, "grad_half_" + nm) for g, l, nm in zip(views, landed, nms)]

    early = ["w_gate", "w_up", "w_out", "w_down"]
    parts = chip_partials([by_cols(gw_gate), by_cols(gw_up), by_rows(gw_out), by_rows(gw_down)], early)
    dq, dk, dv, *slots = _attn_bwd(qkv, do, t_att, parts)
    duc, dcw, dcb, dlg, dlb = _conv_bwd(uc, yconv, dco, cwf, conv_ln_g, conv_ln_b, tm_big)
    grad_x, du, dg1 = _in_bwd(duc, dq, dk, dv, x2, dh1, g_pre_mix, wa, tm_big)
    gw_in = _matmul_tn(a_bf, du, D_MODEL, ts, "grad_w_in", column_block=IN_SH)
    slots += _chip_scatter(chip_partials([gw_in.reshape(N_CHIPS, 2, D_MODEL // 2, IN_SH)], ["w_in"]))
    names = early + ["w_in"]
    halves = [_sum_chips(s, "grad_sum_" + nm) for s, nm in zip(slots, names)]
    others = _share_halves(halves)
    mine = dict(zip(names, halves))
    other = dict(zip(names, others))

    small = [dg1, dcb, dlg, dlb, dag, dg2, dg3, dg4]
    packed = jnp.concatenate([_rows8(s) for s in small] + [_rows8(dcw), _rows8(loss_part)], axis=0)
    red = _allreduce_small(packed)
    sizes = [D_MODEL, CONV_CH, CONV_CH, CONV_CH, CONV_CH, D_MODEL, D_MODEL, D_MODEL]
    g_small = [red[8 * k:8 * k + n // 128].reshape(1, n) for k, n in enumerate(sizes)]
    cw_red = red[64:64 + 128].reshape(HALO, CONV_CH)
    g_cw = lax.dynamic_slice(cw_red, (0, chip * 128), (HALO, 128))
    loss = red[192, 0]

    big = []
    for w, m, v, nm in [(w_in, m_w_in, v_w_in, "w_in"), (w_out, m_w_out, v_w_out, "w_out"),
                        (w_gate, m_w_gate, v_w_gate, "w_gate"), (w_up, m_w_up, v_w_up, "w_up"),
                        (w_down, m_w_down, v_w_down, "w_down")]:
        big.append(_adamw_halves(c_arr, w, mine[nm], other[nm], m, v, "adamw_" + nm))
    sm_w = [g_pre_mix, conv_b, conv_ln_g, conv_ln_b, ag, g_post_mix, g_pre_ffn, g_post_ffn]
    sm_m = [m_g_pre_mix, m_conv_b, m_conv_ln_g, m_conv_ln_b, m_attn_norm_g, m_g_post_mix, m_g_pre_ffn, m_g_post_ffn]
    sm_v = [v_g_pre_mix, v_conv_b, v_conv_ln_g, v_conv_ln_b, v_attn_norm_g, v_g_post_mix, v_g_pre_ffn, v_g_post_ffn]
    pad_cw = lambda a: jnp.pad(a[0, :, 0, :], ((0, 1), (0, 0)))

    def pack(vecs, cw):
        return jnp.concatenate([_rows8(a) for a in vecs] + [cw], axis=0)

    sd, smn, svn = _adamw(pack(sm_w, pad_cw(conv_w)), pack(g_small, g_cw), pack(sm_m, pad_cw(m_conv_w)),
                          pack(sm_v, pad_cw(v_conv_w)), "adamw_small")

    def unpack(p):
        vecs = [p[8 * k:8 * k + n // 128].reshape(1, n) for k, n in enumerate(sizes)]
        return vecs, p[64:64 + CONV_WIDTH].reshape(1, CONV_WIDTH, 1, 128)

    def ordered(vecs, cw, w_in_, w_out_, w_gate_, w_up_, w_down_):
        g1_, cb_, lg_, lb_, ag_, g2_, g3_, g4_ = vecs
        return [g1_, w_in_, cw, cb_, lg_, lb_, ag_.reshape(1, 8, HEAD_DIM), w_out_, g2_, g3_,
                w_gate_, w_up_, w_down_, g4_]

    grads = ordered(g_small, g_cw[:CONV_WIDTH].reshape(1, CONV_WIDTH, 1, 128), *[b[0] for b in big])
    outs = []
    for idx, p in enumerate((sd, smn, svn)):
        vecs, cw = unpack(p)
        outs += ordered(vecs, cw, *[b[idx + 1] for b in big])
    return (loss, grad_x, *grads, *outs)
```

```python
import functools
import math

import jax
import jax.numpy as jnp
from jax import lax
from jax.experimental import pallas as pl
from jax.experimental.pallas import tpu as pltpu

F32 = jnp.float32
BF16 = jnp.bfloat16
MESH = pl.DeviceIdType.MESH

D_MODEL = 1024
CONV_CH = 512
CONV_WIDTH = 31
HEAD_DIM = 64
PAIR = 2 * HEAD_DIM
N_PAIRS = 4
D_FF = 2816
N_CHIPS = 4
IN_SH = 2560 // N_CHIPS
FF_SH = D_FF // N_CHIPS
OUT_SH = D_MODEL // N_CHIPS
C_ROWS = OUT_SH + FF_SH
EPS = 1e-6
HALO = 32

ADAM_LR = 0.001
ADAM_B1 = 0.9
ADAM_B2 = 0.999
ADAM_EPS = 1e-08
ADAM_WD = 0.01
ADAM_STEP = 10

VMEM_LIMIT = 56 * 2 ** 20
VMEM_LIMIT_ATTN_BWD = 60 * 2 ** 20


def _cp(sem=None, vmem=VMEM_LIMIT):
    return pltpu.CompilerParams(dimension_semantics=sem, vmem_limit_bytes=vmem)


def _hbm():
    return pl.BlockSpec(memory_space=pltpu.HBM)


def _const_spec(shape):
    nd = len(shape)
    return pl.BlockSpec(shape, lambda *_: (0,) * nd, pipeline_mode=pl.Buffered(1))


def _dot(a, b):
    return jnp.dot(a, b, preferred_element_type=F32)


def _dot_nt(a, b):
    return lax.dot_general(a, b, (((1,), (1,)), ((), ())), preferred_element_type=F32)


def _dot_tn(a, b):
    return lax.dot_general(a, b, (((0,), (0,)), ((), ())), preferred_element_type=F32)


def _split3(x):
    b0 = x.astype(BF16)
    r1 = x - b0.astype(F32)
    b1 = r1.astype(BF16)
    b2 = (r1 - b1.astype(F32)).astype(BF16)
    return b0, b1, b2


def _split2(x):
    hi = x.astype(BF16)
    lo = (x - hi.astype(F32)).astype(BF16)
    return hi, lo


def _sigmoid(x):
    return 1.0 / (1.0 + jnp.exp(-x))


def _head_mean(x, seg):
    b0, b1, b2 = _split3(x)
    return (_dot(b0, seg) + _dot(b1, seg) + _dot(b2, seg)) * (1.0 / HEAD_DIM)


def _seg_matrix(n):
    r = lax.broadcasted_iota(jnp.int32, (n, n), 0) // HEAD_DIM
    c = lax.broadcasted_iota(jnp.int32, (n, n), 1) // HEAD_DIM
    return (r == c).astype(BF16)


def _rms(x):
    return lax.rsqrt(jnp.mean(x * x, axis=-1, keepdims=True) + EPS)


def _rms_bwd(dy, n, r, g):
    dn = dy * g
    dx = r * (dn - n * jnp.mean(dn * n, axis=-1, keepdims=True))
    return dx, dy * n


class _GatherPlan:
    def __init__(self, srcs, outs, lead, ssem, rsem):
        self.srcs, self.outs, self.lead, self.ssem, self.rsem = srcs, outs, lead, ssem, rsem
        x, y, self.c = lax.axis_index("x"), lax.axis_index("y"), lax.axis_index("c")
        self.me = 2 * x + y
        self.sibling = (x, y, 1 - self.c)
        self.chips = [(1 - x, y), (x, 1 - y), (1 - x, 1 - y)]

    def _half(self, ref, i, h):
        if self.lead[i]:
            return ref.at[h]
        rows = ref.shape[0] // 2
        return ref.at[pl.ds(h * rows, rows)]

    def _ici(self, i, k, origin):
        return pltpu.make_async_remote_copy(
            src_ref=self._half(self.srcs[i], i, self.c), dst_ref=self._half(self.outs[i].at[origin], i, self.c),
            send_sem=self.ssem.at[6 * i + k], recv_sem=self.rsem.at[6 * i + k],
            device_id=(self.chips[k][0], self.chips[k][1], self.c), device_id_type=MESH)

    def _d2d(self, i, k, h):
        origin = 2 * self.chips[k][0] + self.chips[k][1]
        piece = self._half(self.outs[i].at[origin], i, h)
        return pltpu.make_async_remote_copy(
            src_ref=piece, dst_ref=piece, send_sem=self.ssem.at[6 * i + 3 + k],
            recv_sem=self.rsem.at[6 * i + 3 + k], device_id=self.sibling, device_id_type=MESH)

    def _each(self):
        return [(i, k) for i in range(len(self.srcs)) for k in range(3)]

    def start(self):
        for i, k in self._each():
            self._ici(i, k, self.me).start()

    def forward(self):
        for i, k in self._each():
            self._ici(i, k, 2 * self.chips[k][0] + self.chips[k][1]).wait_recv()
            self._d2d(i, k, self.c).start()

    def finish(self):
        for i, k in self._each():
            self._d2d(i, k, 1 - self.c).wait_recv()
        for i, k in self._each():
            self._ici(i, k, self.me).wait_send()
            self._d2d(i, k, self.c).wait_send()


def _gather_shapes(shards):
    return [jax.ShapeDtypeStruct((N_CHIPS,) + s.shape, s.dtype) for s in shards]


def _gather_weights(shards, lead):
    n = len(shards)

    def body(*refs):
        plan = _GatherPlan(refs[:n], refs[n:2 * n], lead, refs[2 * n], refs[2 * n + 1])
        plan.start()
        plan.forward()
        plan.finish()

    return pl.pallas_call(
        body, name="gather_weights", out_shape=_gather_shapes(shards),
        in_specs=[_hbm()] * n, out_specs=[_hbm()] * n,
        scratch_shapes=[pltpu.SemaphoreType.DMA((6 * n,)), pltpu.SemaphoreType.DMA((6 * n,))],
    )(*shards)


def _in_proj(x2, g1, wa, tm):
    S = x2.shape[0]

    def body(x_ref, g_ref, w_ref, a_ref, uc_ref, qkv_ref):
        x = x_ref[...]
        a = (x * _rms(x) * g_ref[...]).astype(BF16)
        a_ref[...] = a
        u = [_dot(a, w_ref[j]) for j in range(N_CHIPS)]
        uc_ref[:, 0:640] = u[0]
        uc_ref[:, 640:1024] = u[1][:, 0:384]
        qkv_ref[:, 0:256] = u[1][:, 384:640].astype(BF16)
        qkv_ref[:, 256:896] = u[2].astype(BF16)
        qkv_ref[:, 896:1536] = u[3].astype(BF16)

    return pl.pallas_call(
        body, name="in_proj", grid=(S // tm,),
        in_specs=[pl.BlockSpec((tm, D_MODEL), lambda i: (i, 0)), _const_spec((1, D_MODEL)),
                  _const_spec(wa.shape)],
        out_specs=[pl.BlockSpec((tm, D_MODEL), lambda i: (i, 0)),
                   pl.BlockSpec((tm, 2 * CONV_CH), lambda i: (i, 0)),
                   pl.BlockSpec((tm, 1536), lambda i: (i, 0))],
        out_shape=[jax.ShapeDtypeStruct((S, D_MODEL), BF16), jax.ShapeDtypeStruct((S, 2 * CONV_CH), F32),
                   jax.ShapeDtypeStruct((S, 1536), BF16)],
        compiler_params=_cp(("parallel",)),
    )(x2, g1, wa)


SUBLANES = 8


def _shift_copies(src_ref, sh_ref):
    rows = sh_ref.shape[1]
    for b in range(1, SUBLANES):
        sh_ref[b - 1] = src_ref[pl.ds(b, rows), :]


def _rows_at(src_ref, sh_ref, off, rows):
    a, b = divmod(off, SUBLANES)
    if b == 0:
        return src_ref[pl.ds(SUBLANES * a, rows), :]
    return sh_ref[b - 1, pl.ds(SUBLANES * a, rows), :]


def _conv_taps(cw_ref, src_ref, sh_ref, offs, rows):
    acc = None
    for w, off in enumerate(offs):
        term = cw_ref[w:w + 1, :] * _rows_at(src_ref, sh_ref, off, rows)
        acc = term if acc is None else acc + term
    return acc


def _glu(uc):
    return uc[:, :CONV_CH] * _sigmoid(uc[:, CONV_CH:])


def _conv_fwd(uc, cwf, cb, lg, lb, tm):
    S = uc.shape[0]
    hb = tm // HALO

    def body(uc_ref, prev_ref, cw_ref, cb_ref, lg_ref, lb_ref, out_ref, y_ref, glu_ref, sh_ref):
        i = pl.program_id(0)
        glu_ref[0:HALO, :] = jnp.where(i == 0, 0.0, _glu(prev_ref[...]))
        glu_ref[HALO:HALO + tm, :] = _glu(uc_ref[...])
        glu_ref[HALO + tm:HALO + tm + SUBLANES, :] = jnp.zeros((SUBLANES, CONV_CH), F32)
        _shift_copies(glu_ref, sh_ref)
        offs = [HALO - (CONV_WIDTH - 1) + w for w in range(CONV_WIDTH)]
        y = _conv_taps(cw_ref, glu_ref, sh_ref, offs, tm) + cb_ref[...]
        y_ref[...] = y
        mu = jnp.mean(y, axis=-1, keepdims=True)
        yc = y - mu
        rstd = lax.rsqrt(jnp.mean(yc * yc, axis=-1, keepdims=True) + EPS)
        ln = yc * rstd * lg_ref[...] + lb_ref[...]
        out_ref[...] = (ln * _sigmoid(ln)).astype(BF16)

    return pl.pallas_call(
        body, name="conv_fwd", grid=(S // tm,),
        in_specs=[pl.BlockSpec((tm, 2 * CONV_CH), lambda i: (i, 0)),
                  pl.BlockSpec((HALO, 2 * CONV_CH), lambda i: (jnp.maximum(i * hb - 1, 0), 0)),
                  _const_spec(cwf.shape), _const_spec((1, CONV_CH)), _const_spec((1, CONV_CH)),
                  _const_spec((1, CONV_CH))],
        out_specs=[pl.BlockSpec((tm, CONV_CH), lambda i: (i, 0))] * 2,
        out_shape=[jax.ShapeDtypeStruct((S, CONV_CH), BF16), jax.ShapeDtypeStruct((S, CONV_CH), F32)],
        scratch_shapes=[pltpu.VMEM((HALO + tm + SUBLANES, CONV_CH), F32),
                        pltpu.VMEM((SUBLANES - 1, HALO + tm, CONV_CH), F32)],
        compiler_params=_cp(("parallel",)),
    )(uc, uc, cwf, cb, lg, lb)


def _lane_mask(h):
    lane = lax.broadcasted_iota(jnp.int32, (1, PAIR), 1)
    return (lane >= HEAD_DIM * h) & (lane < HEAD_DIM * (h + 1))


def _neg_abs(x):
    bits = lax.bitcast_convert_type(x, jnp.uint32) | jnp.uint32(0x80000000)
    return lax.bitcast_convert_type(bits, F32)


def _tri_dot(x, m):
    return _dot(x.astype(BF16), m)


MASKED = -1e30
KEY_BLOCKS = 4


def _running_sums(x, m, reverse, start=None):
    t = m.shape[0]
    blocks = x.shape[1] // t
    order = range(blocks - 1, -1, -1) if reverse else range(blocks)
    out = [None] * blocks
    carry = start
    for b in order:
        xb = x[:, b * t:(b + 1) * t]
        cb = _tri_dot(xb, m)
        out[b] = cb if carry is None else cb + carry
        rs = jnp.sum(xb, axis=1, keepdims=True)
        carry = rs if carry is None else carry + rs
    return jnp.concatenate(out, axis=1), carry


def _sb_tile(z, r, m_suf):
    sp = jnp.maximum(z, 0.0) + jnp.log(1.0 + jnp.exp(_neg_abs(z)))
    c, rs = _running_sums(sp, m_suf, reverse=True, start=r)
    return jnp.exp(z - c), sp, rs


def _scores(qm, kt, mask):
    z = _dot_nt(qm, kt)
    return z if mask is None else jnp.where(mask, z, MASKED)


def _causal_mask(i, sb, t, tk):
    row = lax.broadcasted_iota(jnp.int32, (t, tk), 0) + i * t
    col = lax.broadcasted_iota(jnp.int32, (t, tk), 1) + sb * tk
    return col < row


def _sweep_plain(first, count, tile):
    def step(n, carry):
        tile(first + n)
        return carry

    lax.fori_loop(0, count + 1, step, 0)


def _sweep(first, count, down, fetch, load, work):
    lo, hi = (first - count, first) if down else (first, first + count)
    tile = lambda j: jnp.clip(first - j if down else first + j, lo, hi)
    fetch(first, 0, True)

    def step(n, carry):
        j = 2 * n
        vals = load(0)
        fetch(tile(j + 1), 1, False)
        work(tile(j), vals)
        vals = load(1)
        fetch(tile(j + 2), 0, False)
        work(tile(j + 1), vals)
        return carry

    lax.fori_loop(0, (count + 1) // 2, step, 0)

    @pl.when(lax.rem(count, 2) == 0)
    def _():
        work(tile(count), load(0))


def _suffix_matrix(t, prefix=False):
    row = lax.broadcasted_iota(jnp.int32, (t, t), 0)
    col = lax.broadcasted_iota(jnp.int32, (t, t), 1)
    return ((row <= col) if prefix else (row >= col)).astype(BF16)


def _attn_fwd(qkv, t, tk, shards, lead):
    S = qkv.shape[0]

    ng = len(shards)
    nq = S // t

    def body(*refs):
        q_ref, k_ref, v_ref = refs[:3]
        o_ref = refs[3 + ng]
        acc_ref, r_ref, z_buf, ssem, rsem = refs[4 + 2 * ng:]
        p = pl.program_id(0)
        i = pl.program_id(1)
        plan = _GatherPlan(refs[3:3 + ng], refs[4 + ng:4 + 2 * ng], lead, ssem, rsem)
        pl.when((p == 0) & (i == 0))(plan.start)
        pl.when((p == 1) & (i == 0))(plan.forward)
        last = (i * t + t - 1) // tk
        m_suf = _suffix_matrix(tk // KEY_BLOCKS)
        q = q_ref[...]
        hms = [_lane_mask(h) for h in range(2)]
        qms = [jnp.where(hm, q, 0) * 0.125 for hm in hms]
        acc_ref[...] = jnp.zeros_like(acc_ref)
        r_ref[...] = jnp.zeros_like(r_ref)

        def rows(sb):
            return pl.ds(pl.multiple_of(sb * tk, tk), tk)

        def fetch(sb, slot, diagonal):
            kt = k_ref[rows(sb), :]
            mask = _causal_mask(i, sb, t, tk) if diagonal else None
            for h in range(2):
                z_buf[slot, h] = _scores(qms[h], kt, mask)

        def load(slot):
            return [z_buf[slot, h] for h in range(2)]

        def work(sb, zs):
            vt = v_ref[rows(sb), :]
            for h in range(2):
                a_loc, _, rs = _sb_tile(zs[h], None, m_suf)
                r = r_ref[h]
                acc_ref[...] += _dot(a_loc.astype(BF16), jnp.where(hms[h], vt, 0)) * jnp.exp(-r)
                r_ref[h] = r + rs

        _sweep(last, last, True, fetch, load, work)
        o_ref[...] = acc_ref[...]
        pl.when((p == N_PAIRS - 1) & (i == nq - 1))(plan.finish)

    return pl.pallas_call(
        body, name="attn_fwd", grid=(N_PAIRS, nq),
        in_specs=[pl.BlockSpec((t, PAIR), lambda p, i: (i, p)),
                  pl.BlockSpec((S, PAIR), lambda p, i: (0, N_PAIRS + p)),
                  pl.BlockSpec((S, PAIR), lambda p, i: (0, 2 * N_PAIRS + p))] + [_hbm()] * ng,
        out_specs=[pl.BlockSpec((t, PAIR), lambda p, i: (i, p))] + [_hbm()] * ng,
        out_shape=[jax.ShapeDtypeStruct((S, N_PAIRS * PAIR), F32)] + _gather_shapes(shards),
        scratch_shapes=[pltpu.VMEM((t, PAIR), F32), pltpu.VMEM((2, t, 1), F32),
                        pltpu.VMEM((2, 2, t, tk), F32),
                        pltpu.SemaphoreType.DMA((6 * ng,)), pltpu.SemaphoreType.DMA((6 * ng,))],
        compiler_params=_cp(("arbitrary", "arbitrary")),
    )(qkv, qkv, qkv, *shards)


def _out_proj(conv_out, o, ag, wc, x2, g2, g3, tm):
    S = o.shape[0]

    def body(co_ref, o_ref, ag_ref, w_ref, x_ref, g2_ref, g3_ref, mix_ref, y_ref, h1_ref, fin_ref):
        seg = _seg_matrix(CONV_CH)
        o = o_ref[...]
        att = (o * lax.rsqrt(_head_mean(o * o, seg) + EPS) * ag_ref[...]).astype(BF16)
        co = co_ref[...]
        mix_ref[:, :CONV_CH] = co
        mix_ref[:, CONV_CH:] = att
        y = _dot(co, w_ref[0:CONV_CH, :]) + _dot(att, w_ref[CONV_CH:, :])
        y_ref[...] = y
        h1 = x_ref[...] + y * _rms(y) * g2_ref[...]
        h1_ref[...] = h1
        fin_ref[...] = (h1 * _rms(h1) * g3_ref[...]).astype(BF16)

    row = lambda w: pl.BlockSpec((tm, w), lambda i: (i, 0))
    return pl.pallas_call(
        body, name="out_proj", grid=(S // tm,),
        in_specs=[row(CONV_CH), row(CONV_CH), _const_spec((1, CONV_CH)), _const_spec(wc.shape),
                  row(D_MODEL), _const_spec((1, D_MODEL)), _const_spec((1, D_MODEL))],
        out_specs=[row(D_MODEL)] * 4,
        out_shape=[jax.ShapeDtypeStruct((S, D_MODEL), BF16), jax.ShapeDtypeStruct((S, D_MODEL), F32),
                   jax.ShapeDtypeStruct((S, D_MODEL), F32), jax.ShapeDtypeStruct((S, D_MODEL), BF16)],
        compiler_params=_cp(("parallel",)),
    )(conv_out, o, ag, wc, x2, g2, g3)


def _ffn_fwd(f_in, h1, tgt, wg, wu, wd, g4, tm):
    S = f_in.shape[0]

    def body(fin_ref, h1_ref, tgt_ref, wg_ref, wu_ref, wd_ref, g4_ref, df_ref, dh2_ref, dg4_ref, loss_ref,
             gt_ref, up_ref, act_ref):
        i = pl.program_id(0)
        fin = fin_ref[...]
        gt = _dot(fin, wg_ref[...])
        up = _dot(fin, wu_ref[...])
        act = (gt * _sigmoid(gt) * up).astype(BF16)
        gt_ref[...] = gt.astype(BF16)
        up_ref[...] = up.astype(BF16)
        act_ref[...] = act
        f = _dot(act, wd_ref[...])
        r = _rms(f)
        n = f * r
        g4 = g4_ref[...]
        err = h1_ref[...] + n * g4 - tgt_ref[...]
        dh2 = err * (1.0 / D_MODEL)
        dh2_ref[...] = dh2
        df, dg = _rms_bwd(dh2, n, r, g4)
        df_ref[...] = df.astype(BF16)

        @pl.when(i == 0)
        def _():
            dg4_ref[...] = jnp.zeros_like(dg4_ref)
            loss_ref[...] = jnp.zeros_like(loss_ref)

        dg4_ref[...] += jnp.sum(dg, axis=0, keepdims=True)
        part = jnp.sum(jnp.sum(err * err, axis=1, keepdims=True), axis=0, keepdims=True)
        loss_ref[...] += part * (0.5 / D_MODEL)

    row = lambda w: pl.BlockSpec((tm, w), lambda i: (i, 0))
    return pl.pallas_call(
        body, name="ffn_fwd", grid=(S // tm,),
        in_specs=[row(D_MODEL), row(D_MODEL), row(D_MODEL), _const_spec(wg.shape), _const_spec(wu.shape),
                  _const_spec(wd.shape), _const_spec((1, D_MODEL))],
        out_specs=[row(D_MODEL), row(D_MODEL), pl.BlockSpec((1, D_MODEL), lambda i: (0, 0)),
                   pl.BlockSpec((1, 128), lambda i: (0, 0)), row(D_FF), row(D_FF), row(D_FF)],
        out_shape=[jax.ShapeDtypeStruct((S, D_MODEL), BF16), jax.ShapeDtypeStruct((S, D_MODEL), F32),
                   jax.ShapeDtypeStruct((1, D_MODEL), F32), jax.ShapeDtypeStruct((1, 128), F32)]
        + [jax.ShapeDtypeStruct((S, D_FF), BF16)] * 3,
        compiler_params=_cp(("arbitrary",)),
    )(f_in, h1, tgt, wg, wu, wd, g4)


def _ffn_bwd(gt_bf, up_bf, df, dh2, h1, yv, wg, wu, wd, g3, g2, tm):
    S = df.shape[0]

    def body(gt_ref, up_ref, df_ref, dh2_ref, h1_ref, y_ref, wg_ref, wu_ref, wd_ref, g3_ref, g2_ref,
             dgt_ref, dup_ref, dh1_ref, dy_ref, dg3_ref, dg2_ref):
        i = pl.program_id(0)
        df = df_ref[...]
        gt = gt_ref[...].astype(F32)
        up = up_ref[...].astype(F32)
        sg = _sigmoid(gt)
        silu = gt * sg
        dact = _dot_nt(df, wd_ref[...])
        dgt = (dact * up * (sg * (1.0 + gt * (1.0 - sg)))).astype(BF16)
        dup = (dact * silu).astype(BF16)
        dgt_ref[...] = dgt
        dup_ref[...] = dup
        dfin = _dot_nt(dgt, wg_ref[...]) + _dot_nt(dup, wu_ref[...])
        h1 = h1_ref[...]
        r3 = _rms(h1)
        dh1_n, dg3 = _rms_bwd(dfin, h1 * r3, r3, g3_ref[...])
        dh1 = dh2_ref[...] + dh1_n
        dh1_ref[...] = dh1
        y = y_ref[...]
        r2 = _rms(y)
        dy, dg2 = _rms_bwd(dh1, y * r2, r2, g2_ref[...])
        dy_ref[...] = dy.astype(BF16)

        @pl.when(i == 0)
        def _():
            dg3_ref[...] = jnp.zeros_like(dg3_ref)
            dg2_ref[...] = jnp.zeros_like(dg2_ref)

        dg3_ref[...] += jnp.sum(dg3, axis=0, keepdims=True)
        dg2_ref[...] += jnp.sum(dg2, axis=0, keepdims=True)

    row = lambda w: pl.BlockSpec((tm, w), lambda i: (i, 0))
    vec = pl.BlockSpec((1, D_MODEL), lambda i: (0, 0))
    return pl.pallas_call(
        body, name="ffn_bwd", grid=(S // tm,),
        in_specs=[row(D_FF), row(D_FF)] + [row(D_MODEL)] * 4
        + [_const_spec(wg.shape), _const_spec(wu.shape), _const_spec(wd.shape),
           _const_spec((1, D_MODEL)), _const_spec((1, D_MODEL))],
        out_specs=[row(D_FF), row(D_FF), row(D_MODEL), row(D_MODEL), vec, vec],
        out_shape=[jax.ShapeDtypeStruct((S, D_FF), BF16)] * 2
        + [jax.ShapeDtypeStruct((S, D_MODEL), F32), jax.ShapeDtypeStruct((S, D_MODEL), BF16),
           jax.ShapeDtypeStruct((1, D_MODEL), F32), jax.ShapeDtypeStruct((1, D_MODEL), F32)],
        compiler_params=_cp(("arbitrary",)),
    )(gt_bf, up_bf, df, dh2, h1, yv, wg, wu, wd, g3, g2)


def _out_bwd(dy, o, ag, wc, tm):
    S = o.shape[0]

    def body(dy_ref, o_ref, ag_ref, w_ref, dco_ref, do_ref, dag_ref):
        i = pl.program_id(0)
        seg = _seg_matrix(CONV_CH)
        dy = dy_ref[...]
        dco_ref[...] = _dot_nt(dy, w_ref[0:CONV_CH, :])
        datt = _dot_nt(dy, w_ref[CONV_CH:, :])
        o = o_ref[...]
        r = lax.rsqrt(_head_mean(o * o, seg) + EPS)
        n = o * r
        dn = datt * ag_ref[...]
        do_ref[...] = (r * (dn - n * _head_mean(dn * n, seg))).astype(BF16)

        @pl.when(i == 0)
        def _():
            dag_ref[...] = jnp.zeros_like(dag_ref)

        dag_ref[...] += jnp.sum(datt * n, axis=0, keepdims=True)

    row = lambda w: pl.BlockSpec((tm, w), lambda i: (i, 0))
    return pl.pallas_call(
        body, name="out_bwd", grid=(S // tm,),
        in_specs=[row(D_MODEL), row(CONV_CH), _const_spec((1, CONV_CH)), _const_spec(wc.shape)],
        out_specs=[row(CONV_CH), row(CONV_CH), pl.BlockSpec((1, CONV_CH), lambda i: (0, 0))],
        out_shape=[jax.ShapeDtypeStruct((S, CONV_CH), F32), jax.ShapeDtypeStruct((S, CONV_CH), BF16),
                   jax.ShapeDtypeStruct((1, CONV_CH), F32)],
        compiler_params=_cp(("arbitrary",)),
    )(dy, o, ag, wc)


def _attn_bwd(qkv, do, t, tk, parts):
    S = qkv.shape[0]
    nk = S // tk
    ns = len(parts)

    def body(*refs):
        q_ref, k_ref, v_ref, do_ref = refs[:4]
        dq_ref, dk_hbm, dv_hbm = refs[4 + ns:7 + ns]
        g_buf, s_buf, r_ref, dq_acc, dk_ref, dv_ref, z_buf, da_buf = refs[7 + 2 * ns:15 + 2 * ns]
        p = pl.program_id(0)
        i = pl.program_id(1)
        plan = _ScatterPlan(refs[4:4 + ns], refs[7 + ns:7 + 2 * ns], *refs[15 + 2 * ns:])
        pl.when((p == 0) & (i == 0))(plan.start)
        last = (i * t + t - 1) // tk

        @pl.when(i == 0)
        def _():
            dk_ref[...] = jnp.zeros_like(dk_ref)
            dv_ref[...] = jnp.zeros_like(dv_ref)

        m_suf = _suffix_matrix(tk // KEY_BLOCKS)
        m_pre = _suffix_matrix(tk // KEY_BLOCKS, prefix=True)
        q = q_ref[...]
        do = do_ref[...]
        dq_acc[...] = jnp.zeros_like(dq_acc)

        def rows(sb):
            return pl.ds(pl.multiple_of(sb * tk, tk), tk)

        for h in range(2):
            hm = _lane_mask(h)
            qm = jnp.where(hm, q, 0) * 0.125
            dom = jnp.where(hm, do, 0)
            r_ref[...] = jnp.zeros_like(r_ref)

            def fetch1(sb, slot, diagonal, qm=qm, dom=dom):
                mask = _causal_mask(i, sb, t, tk) if diagonal else None
                z_buf[slot] = _scores(qm, k_ref[rows(sb), :], mask)
                da_buf[slot] = _dot_nt(dom, v_ref[rows(sb), :])

            def load1(slot):
                return z_buf[slot], da_buf[slot]

            def work1(sb, vals, dom=dom):
                z, da = vals
                A, sp, r_ref[...] = _sb_tile(z, r_ref[...], m_suf)
                g_buf[sb] = A * da
                s_buf[sb] = 1.0 - jnp.exp(-sp)
                dv_ref[rows(sb), :] += _dot_tn(A.astype(BF16), dom)

            _sweep(last, last, True, fetch1, load1, work1)
            r_ref[...] = jnp.zeros_like(r_ref)

            def sweep2(sb, hm=hm, qm=qm):
                g = g_buf[sb]
                pre, r_ref[...] = _running_sums(g, m_pre, reverse=False, start=r_ref[...])
                dzb = (g - s_buf[sb] * pre).astype(BF16)
                dq_acc[...] += _dot(dzb, jnp.where(hm, k_ref[rows(sb), :], 0))
                dk_ref[rows(sb), :] += _dot_tn(dzb, qm)

            _sweep_plain(0, last, sweep2)
        dq_ref[...] = dq_acc[...] * 0.125

        @pl.when(i == S // t - 1)
        def _():
            cols = pl.ds(pl.multiple_of(p * PAIR, PAIR), PAIR)
            pltpu.sync_copy(dk_ref, dk_hbm.at[:, cols])
            pltpu.sync_copy(dv_ref, dv_hbm.at[:, cols])

        pl.when((p == N_PAIRS - 1) & (i == S // t - 1))(plan.finish)

    once = lambda cb: pl.BlockSpec((S, PAIR), cb, pipeline_mode=pl.Buffered(1))
    return pl.pallas_call(
        body, name="attn_bwd", grid=(N_PAIRS, S // t),
        in_specs=[pl.BlockSpec((t, PAIR), lambda p, i: (i, p)),
                  once(lambda p, i: (0, N_PAIRS + p)), once(lambda p, i: (0, 2 * N_PAIRS + p)),
                  pl.BlockSpec((t, PAIR), lambda p, i: (i, p))] + [_hbm()] * ns,
        out_specs=[pl.BlockSpec((t, PAIR), lambda p, i: (i, p)), _hbm(), _hbm()] + [_hbm()] * ns,
        out_shape=[jax.ShapeDtypeStruct((S, N_PAIRS * PAIR), F32)] * 3
        + [jax.ShapeDtypeStruct(pt.shape, pt.dtype) for pt in parts],
        scratch_shapes=[pltpu.VMEM((nk, t, tk), F32), pltpu.VMEM((nk, t, tk), F32),
                        pltpu.VMEM((t, 1), F32), pltpu.VMEM((t, PAIR), F32),
                        pltpu.VMEM((S, PAIR), F32), pltpu.VMEM((S, PAIR), F32),
                        pltpu.VMEM((2, t, tk), F32), pltpu.VMEM((2, t, tk), F32)] + _scatter_sems(ns),
        compiler_params=_cp(("arbitrary", "arbitrary"), vmem=VMEM_LIMIT_ATTN_BWD),
    )(qkv, qkv, qkv, do, *parts)


def _conv_bwd(uc, yconv, dco, cwf, lg, lb, tm):
    S = uc.shape[0]
    hb = tm // HALO
    nb = S // tm
    ext = tm + HALO

    def body(uc_ref, prev_ref, y_ref, ynext_ref, dco_ref, dnext_ref, cw_ref, lg_ref, lb_ref,
             duc_ref, dcw_ref, dcb_ref, dlg_ref, dlb_ref, glu_ref, dyc_ref, shg_ref, shd_ref):
        i = pl.program_id(0)
        last = i == nb - 1

        @pl.when(i == 0)
        def _():
            for ref in (dcw_ref, dcb_ref, dlg_ref, dlb_ref):
                ref[...] = jnp.zeros_like(ref)

        uc = uc_ref[...]
        glu_ref[0:HALO, :] = jnp.where(i == 0, 0.0, _glu(prev_ref[...]))
        glu_ref[HALO:ext, :] = _glu(uc)
        glu_ref[ext:ext + SUBLANES, :] = jnp.zeros((SUBLANES, CONV_CH), F32)
        _shift_copies(glu_ref, shg_ref)
        fwd_offs = [HALO - (CONV_WIDTH - 1) + w for w in range(CONV_WIDTH)]
        y = jnp.concatenate([y_ref[...], ynext_ref[...]], axis=0)
        mu = jnp.mean(y, axis=-1, keepdims=True)
        yc = y - mu
        rstd = lax.rsqrt(jnp.mean(yc * yc, axis=-1, keepdims=True) + EPS)
        yhat = yc * rstd
        lg = lg_ref[...]
        ln = yhat * lg + lb_ref[...]
        sg = _sigmoid(ln)
        dout = jnp.concatenate([dco_ref[...], jnp.where(last, 0.0, dnext_ref[...])], axis=0)
        dln = dout * (sg * (1.0 + ln * (1.0 - sg)))
        dyh = dln * lg
        dyc = rstd * (dyh - jnp.mean(dyh, axis=-1, keepdims=True)
                      - yhat * jnp.mean(dyh * yhat, axis=-1, keepdims=True))
        dyc_ref[0:ext, :] = dyc
        dyc_ref[ext:ext + SUBLANES, :] = jnp.zeros((SUBLANES, CONV_CH), F32)
        _shift_copies(dyc_ref, shd_ref)
        dlg_ref[...] += jnp.sum((dln * yhat)[0:tm], axis=0, keepdims=True)
        dlb_ref[...] += jnp.sum(dln[0:tm], axis=0, keepdims=True)
        dcb_ref[...] += jnp.sum(dyc[0:tm], axis=0, keepdims=True)
        dglu = _conv_taps(cw_ref, dyc_ref, shd_ref, [CONV_WIDTH - 1 - w for w in range(CONV_WIDTH)], tm)
        d0 = dyc[0:tm]
        for w, off in enumerate(fwd_offs):
            dcw_ref[w:w + 1, :] += jnp.sum(d0 * _rows_at(glu_ref, shg_ref, off, tm), axis=0, keepdims=True)
        val, gate = uc[:, :CONV_CH], uc[:, CONV_CH:]
        sgate = _sigmoid(gate)
        duc_ref[:, :CONV_CH] = (dglu * sgate).astype(BF16)
        duc_ref[:, CONV_CH:] = (dglu * val * sgate * (1.0 - sgate)).astype(BF16)

    vec = pl.BlockSpec((1, CONV_CH), lambda i: (0, 0))
    nxt = lambda i: (jnp.minimum((i + 1) * hb, S // HALO - 1), 0)
    return pl.pallas_call(
        body, name="conv_bwd", grid=(nb,),
        in_specs=[pl.BlockSpec((tm, 2 * CONV_CH), lambda i: (i, 0)),
                  pl.BlockSpec((HALO, 2 * CONV_CH), lambda i: (jnp.maximum(i * hb - 1, 0), 0)),
                  pl.BlockSpec((tm, CONV_CH), lambda i: (i, 0)), pl.BlockSpec((HALO, CONV_CH), nxt),
                  pl.BlockSpec((tm, CONV_CH), lambda i: (i, 0)), pl.BlockSpec((HALO, CONV_CH), nxt),
                  _const_spec(cwf.shape), _const_spec((1, CONV_CH)), _const_spec((1, CONV_CH))],
        out_specs=[pl.BlockSpec((tm, 2 * CONV_CH), lambda i: (i, 0)),
                   pl.BlockSpec(cwf.shape, lambda i: (0, 0)), vec, vec, vec],
        out_shape=[jax.ShapeDtypeStruct((S, 2 * CONV_CH), BF16), jax.ShapeDtypeStruct(cwf.shape, F32)]
        + [jax.ShapeDtypeStruct((1, CONV_CH), F32)] * 3,
        scratch_shapes=[pltpu.VMEM((ext + SUBLANES, CONV_CH), F32), pltpu.VMEM((ext + SUBLANES, CONV_CH), F32),
                        pltpu.VMEM((SUBLANES - 1, ext, CONV_CH), F32),
                        pltpu.VMEM((SUBLANES - 1, ext, CONV_CH), F32)],
        compiler_params=_cp(("arbitrary",)),
    )(uc, uc, yconv, yconv, dco, dco, cwf, lg, lb)


def _in_bwd(duc, dq, dk, dv, x2, dh1, g1, wa, tm):
    S = x2.shape[0]

    def body(duc_ref, dq_ref, dk_ref, dv_ref, x_ref, dh1_ref, g_ref, w_ref, gx_ref, du_ref, dg_ref):
        i = pl.program_id(0)
        du = jnp.concatenate([duc_ref[...], dq_ref[...].astype(BF16), dk_ref[...].astype(BF16),
                              dv_ref[...].astype(BF16)], axis=1)
        du_ref[...] = du
        da = _dot_nt(du[:, 0:IN_SH], w_ref[0])
        for j in range(1, N_CHIPS):
            da = da + _dot_nt(du[:, IN_SH * j:IN_SH * (j + 1)], w_ref[j])
        x = x_ref[...]
        r = _rms(x)
        dx, dg = _rms_bwd(da, x * r, r, g_ref[...])
        gx_ref[...] = dh1_ref[...] + dx

        @pl.when(i == 0)
        def _():
            dg_ref[...] = jnp.zeros_like(dg_ref)

        dg_ref[...] += jnp.sum(dg, axis=0, keepdims=True)

    row = lambda w: pl.BlockSpec((tm, w), lambda i: (i, 0))
    return pl.pallas_call(
        body, name="in_bwd", grid=(S // tm,),
        in_specs=[row(2 * CONV_CH), row(CONV_CH), row(CONV_CH), row(CONV_CH), row(D_MODEL), row(D_MODEL),
                  _const_spec((1, D_MODEL)), _const_spec(wa.shape)],
        out_specs=[pl.BlockSpec((None, tm, D_MODEL), lambda i: (0, i, 0)), row(2560),
                   pl.BlockSpec((1, D_MODEL), lambda i: (0, 0))],
        out_shape=[jax.ShapeDtypeStruct((1, S, D_MODEL), F32), jax.ShapeDtypeStruct((S, 2560), BF16),
                   jax.ShapeDtypeStruct((1, D_MODEL), F32)],
        compiler_params=_cp(("arbitrary",)),
    )(duc, dq, dk, dv, x2, dh1, g1, wa)


def _matmul_tn(xm, ym, tm, ts, name, column_block=None):
    S, M = xm.shape
    N = ym.shape[1]

    def body(x_ref, y_ref, o_ref):
        @pl.when(pl.program_id(1) == 0)
        def _():
            o_ref[...] = jnp.zeros_like(o_ref)

        xt = x_ref[...].T
        if column_block is None:
            o_ref[...] += _dot(xt, y_ref[...])
        else:
            for j in range(N // column_block):
                o_ref[j] += _dot(xt, y_ref[:, column_block * j:column_block * (j + 1)])

    if column_block is None:
        out_spec = pl.BlockSpec((tm, N), lambda m, s: (m, 0))
        out_shape = jax.ShapeDtypeStruct((M, N), F32)
    else:
        out_spec = pl.BlockSpec((N // column_block, tm, column_block), lambda m, s: (0, m, 0))
        out_shape = jax.ShapeDtypeStruct((N // column_block, M, column_block), F32)
    return pl.pallas_call(
        body, name=name, grid=(M // tm, S // ts),
        in_specs=[pl.BlockSpec((ts, tm), lambda m, s: (s, m)), pl.BlockSpec((ts, N), lambda m, s: (s, 0))],
        out_specs=out_spec, out_shape=out_shape,
        compiler_params=_cp(("parallel", "arbitrary")),
    )(xm, ym)


def _sibling_halves(grads, name):
    n = len(grads)

    def body(*refs):
        ins, outs, ssem, rsem = refs[:n], refs[n:2 * n], refs[2 * n], refs[2 * n + 1]
        x, y, c = lax.axis_index("x"), lax.axis_index("y"), lax.axis_index("c")
        copies = []
        for k in range(n):
            for j in range(N_CHIPS):
                copies.append(pltpu.make_async_remote_copy(
                    src_ref=ins[k].at[j, 1 - c], dst_ref=outs[k].at[j],
                    send_sem=ssem.at[N_CHIPS * k + j], recv_sem=rsem.at[N_CHIPS * k + j],
                    device_id=(x, y, 1 - c), device_id_type=MESH))
        for cp in copies:
            cp.start()
        for cp in copies:
            cp.wait()

    shapes = [jax.ShapeDtypeStruct((g.shape[0],) + g.shape[2:], F32) for g in grads]
    return pl.pallas_call(
        body, name=name, out_shape=shapes,
        in_specs=[_hbm()] * n, out_specs=[_hbm()] * n,
        scratch_shapes=[pltpu.SemaphoreType.DMA((N_CHIPS * n,)), pltpu.SemaphoreType.DMA((N_CHIPS * n,))],
    )(*grads)


def _add_half(c_arr, g, landed, name):
    def body(c_ref, g_ref, l_ref, o_ref):
        o_ref[...] = (g_ref[...] + l_ref[...]).astype(BF16)

    rows, n = g.shape[2], g.shape[3]
    grid = (N_CHIPS,)
    g_spec = pl.BlockSpec((None, None, rows, n), lambda j, c: (j, c[0], 0, 0))
    l_spec = pl.BlockSpec((None, rows, n), lambda j, c: (j, 0, 0))
    return pl.pallas_call(
        body, name=name,
        grid_spec=pltpu.PrefetchScalarGridSpec(num_scalar_prefetch=1, grid=grid, in_specs=[g_spec, l_spec],
                                               out_specs=l_spec),
        out_shape=jax.ShapeDtypeStruct(landed.shape, BF16),
        compiler_params=_cp(("parallel",)),
    )(c_arr, g, landed)


class _ScatterPlan:
    def __init__(self, ins, outs, lsem, ssem, rsem):
        x, y, c = lax.axis_index("x"), lax.axis_index("y"), lax.axis_index("c")
        me = 2 * x + y
        self.copies = []
        for k in range(len(ins)):
            self.copies.append(pltpu.make_async_copy(ins[k].at[me], outs[k].at[me], lsem.at[k]))
            for r, chip in enumerate([(1 - x, y), (x, 1 - y), (1 - x, 1 - y)]):
                self.copies.append(pltpu.make_async_remote_copy(
                    src_ref=ins[k].at[2 * chip[0] + chip[1]], dst_ref=outs[k].at[me],
                    send_sem=ssem.at[3 * k + r], recv_sem=rsem.at[3 * k + r],
                    device_id=(chip[0], chip[1], c), device_id_type=MESH))

    def start(self):
        for cp in self.copies:
            cp.start()

    def finish(self):
        for cp in self.copies:
            cp.wait()


def _scatter_sems(n):
    return [pltpu.SemaphoreType.DMA((n,)), pltpu.SemaphoreType.DMA((3 * n,)), pltpu.SemaphoreType.DMA((3 * n,))]


def _chip_scatter(parts):
    n = len(parts)

    def body(*refs):
        plan = _ScatterPlan(refs[:n], refs[n:2 * n], *refs[2 * n:])
        plan.start()
        plan.finish()

    shapes = [jax.ShapeDtypeStruct(p.shape, p.dtype) for p in parts]
    return pl.pallas_call(
        body, name="grad_chip_scatter", out_shape=shapes,
        in_specs=[_hbm()] * n, out_specs=[_hbm()] * n, scratch_shapes=_scatter_sems(n),
    )(*parts)


def _sum_chips(landed, name):
    _, rows, n = landed.shape
    tr = 256 if rows % 256 == 0 else rows

    def body(a_ref, b_ref, c_ref, d_ref, o_ref):
        f = lambda ref: ref[...].astype(F32)
        o_ref[...] = ((f(a_ref) + f(b_ref)) + f(c_ref)) + f(d_ref)

    specs = [pl.BlockSpec((None, tr, n), functools.partial(lambda i, j: (j, i, 0), j=j)) for j in range(N_CHIPS)]
    return pl.pallas_call(
        body, name=name, grid=(rows // tr,), in_specs=specs,
        out_specs=pl.BlockSpec((tr, n), lambda i: (i, 0)),
        out_shape=jax.ShapeDtypeStruct((rows, n), F32),
        compiler_params=_cp(("parallel",)),
    )(landed, landed, landed, landed)


def _share_halves(halves):
    n = len(halves)

    def body(*refs):
        ins, outs = refs[:n], refs[n:2 * n]
        ssem, rsem = refs[2 * n:]
        x, y, c = lax.axis_index("x"), lax.axis_index("y"), lax.axis_index("c")
        copies = [pltpu.make_async_remote_copy(
            src_ref=ins[k], dst_ref=outs[k], send_sem=ssem.at[k], recv_sem=rsem.at[k],
            device_id=(x, y, 1 - c), device_id_type=MESH) for k in range(n)]
        for cp in copies:
            cp.start()
        for cp in copies:
            cp.wait()

    shapes = [jax.ShapeDtypeStruct(h.shape, F32) for h in halves]
    return pl.pallas_call(
        body, name="grad_share_halves", out_shape=shapes,
        in_specs=[_hbm()] * n, out_specs=[_hbm()] * n,
        scratch_shapes=[pltpu.SemaphoreType.DMA((n,)), pltpu.SemaphoreType.DMA((n,))],
    )(*halves)


def _allreduce_small(packed):
    rows, n = packed.shape

    def body(in_ref, out_ref, land_ref, ssem, rsem):
        x, y, c = lax.axis_index("x"), lax.axis_index("y"), lax.axis_index("c")
        me = 4 * x + 2 * y + c
        land_ref[me] = in_ref[...]
        copies = []
        for r in range(1, 8):
            tx = 1 - x if r & 4 else x
            ty = 1 - y if r & 2 else y
            tc = 1 - c if r & 1 else c
            cp = pltpu.make_async_remote_copy(
                src_ref=in_ref, dst_ref=land_ref.at[me], send_sem=ssem.at[r - 1], recv_sem=rsem.at[r - 1],
                device_id=(tx, ty, tc), device_id_type=MESH)
            cp.start()
            copies.append(cp)
        for cp in copies:
            cp.wait()
        acc = land_ref[0]
        for k in range(1, 8):
            acc = acc + land_ref[k]
        out_ref[...] = acc

    return pl.pallas_call(
        body, name="allreduce_small", out_shape=jax.ShapeDtypeStruct((rows, n), F32),
        in_specs=[pl.BlockSpec(memory_space=pltpu.VMEM)], out_specs=pl.BlockSpec(memory_space=pltpu.VMEM),
        scratch_shapes=[pltpu.VMEM((8, rows, n), F32), pltpu.SemaphoreType.DMA((7,)),
                        pltpu.SemaphoreType.DMA((7,))],
    )(packed)


def _adamw_math(w, g, m, v):
    m = ADAM_B1 * m + (1.0 - ADAM_B1) * g
    v = ADAM_B2 * v + (1.0 - ADAM_B2) * (g * g)
    m_hat = m / (1.0 - ADAM_B1 ** ADAM_STEP)
    v_hat = v / (1.0 - ADAM_B2 ** ADAM_STEP)
    return -ADAM_LR * (m_hat / (jnp.sqrt(v_hat) + ADAM_EPS) + ADAM_WD * w), m, v


def _adamw_halves(c_arr, w, mine, other, m, v, name):
    rows, n = mine.shape
    tr = 256 if rows % 256 == 0 else rows
    nb = rows // tr

    def body(c_ref, w_ref, a_ref, b_ref, m_ref, v_ref, g_ref, d_ref, mo_ref, vo_ref):
        g = jnp.where(pl.program_id(0) == c_ref[0], a_ref[...], b_ref[...])
        g_ref[...] = g
        d_ref[...], mo_ref[...], vo_ref[...] = _adamw_math(w_ref[...], g, m_ref[...], v_ref[...])

    full = pl.BlockSpec((None, tr, n), lambda h, i, c: (0, h * nb + i, 0))
    half = pl.BlockSpec((tr, n), lambda h, i, c: (i, 0))
    return pl.pallas_call(
        body, name=name,
        grid_spec=pltpu.PrefetchScalarGridSpec(num_scalar_prefetch=1, grid=(2, nb),
                                               in_specs=[full, half, half, full, full], out_specs=[full] * 4),
        out_shape=[jax.ShapeDtypeStruct((1, 2 * rows, n), F32)] * 4,
        compiler_params=_cp(("parallel", "parallel")),
    )(c_arr, w, mine, other, m, v)


def _adamw(w, g, m, v, name):
    rows, n = w.shape
    tr = 256 if rows % 256 == 0 else rows

    def body(w_ref, g_ref, m_ref, v_ref, d_ref, mo_ref, vo_ref):
        d_ref[...], mo_ref[...], vo_ref[...] = _adamw_math(w_ref[...], g_ref[...], m_ref[...], v_ref[...])

    spec = pl.BlockSpec((tr, n), lambda i: (i, 0))
    return pl.pallas_call(
        body, name=name, grid=(rows // tr,), in_specs=[spec] * 4, out_specs=[spec] * 3,
        out_shape=[jax.ShapeDtypeStruct((rows, n), F32)] * 3,
        compiler_params=_cp(("parallel",)),
    )(w, g, m, v)


def _rows8(a):
    a = a.reshape(-1, 128)
    return jnp.pad(a, ((0, (-a.shape[0]) % 8), (0, 0)))


def kernel(x, g_pre_mix, w_in, conv_w, conv_b, conv_ln_g, conv_ln_b, attn_norm_g, w_out, g_post_mix, g_pre_ffn, w_gate, w_up, w_down, g_post_ffn, loss_target, m_g_pre_mix, m_w_in, m_conv_w, m_conv_b, m_conv_ln_g, m_conv_ln_b, m_attn_norm_g, m_w_out, m_g_post_mix, m_g_pre_ffn, m_w_gate, m_w_up, m_w_down, m_g_post_ffn, v_g_pre_mix, v_w_in, v_conv_w, v_conv_b, v_conv_ln_g, v_conv_ln_b, v_attn_norm_g, v_w_out, v_g_post_mix, v_g_pre_ffn, v_w_gate, v_w_up, v_w_down, v_g_post_ffn):
    S = x.shape[1]
    tm_big = min(512, S)
    tm_ffn = min(256, S)
    tk_att = min(1024, S)
    t_att_fwd = min(1024, S)
    t_att_bwd = min(512, S)
    chip = 2 * lax.axis_index("x") + lax.axis_index("y")
    core = lax.axis_index("c")
    x2 = x.reshape(S, D_MODEL)
    tgt = loss_target.reshape(S, D_MODEL)
    ag = attn_norm_g.reshape(1, CONV_CH)

    a_sh = w_in[0].astype(BF16)
    b_sh = jnp.stack([w_gate[0], w_up[0]]).astype(BF16)
    c_sh = jnp.concatenate([w_out[0], w_down[0]], axis=0).astype(BF16)
    cw_sh = jnp.pad(conv_w[0, :, 0, :], ((0, 1), (0, 0)))
    own = lambda full, shard: lax.dynamic_update_index_in_dim(full, shard, chip, 0)
    cols = lambda w4: jnp.transpose(w4, (1, 0, 2)).reshape(w4.shape[1], N_CHIPS * w4.shape[2])
    wa4, cw4 = _gather_weights([a_sh, cw_sh], [False, False])
    wa = own(wa4, a_sh)
    cwf = cols(own(cw4, cw_sh))

    a_bf, uc, qkv = _in_proj(x2, g_pre_mix, wa, tm_big)
    conv_out, yconv = _conv_fwd(uc, cwf, conv_b, conv_ln_g, conv_ln_b, tm_big)
    o, wb4, wc4 = _attn_fwd(qkv, t_att_fwd, tk_att, [b_sh, c_sh], [True, False])
    wb4, wc4 = own(wb4, b_sh), own(wc4, c_sh)
    wg, wu = cols(wb4[:, 0]), cols(wb4[:, 1])
    wo = wc4[:, :OUT_SH].reshape(D_MODEL, D_MODEL)
    wd = wc4[:, OUT_SH:].reshape(D_FF, D_MODEL)
    mixed, yv, h1, f_in = _out_proj(conv_out, o, ag, wo, x2, g_post_mix, g_pre_ffn, tm_big)
    df, dh2, dg4, loss_part, gt_bf, up_bf, act = _ffn_fwd(f_in, h1, tgt, wg, wu, wd, g_post_ffn, tm_ffn)

    dgt, dup, dh1, dy, dg3, dg2 = _ffn_bwd(gt_bf, up_bf, df, dh2, h1, yv, wg, wu, wd, g_pre_ffn, g_post_mix, tm_ffn)
    dco, do, dag = _out_bwd(dy, o, ag, wo, tm_big)
    ts = min(512, S)
    gw_out = _matmul_tn(mixed, dy, D_MODEL, ts, "grad_w_out")
    gw_gate = _matmul_tn(f_in, dgt, D_MODEL, ts, "grad_w_gate")
    gw_up = _matmul_tn(f_in, dup, D_MODEL, ts, "grad_w_up")
    gw_down = _matmul_tn(act, df, D_FF // 2, ts, "grad_w_down")

    by_cols = lambda g: jnp.transpose(g.reshape(2, D_MODEL // 2, N_CHIPS, -1), (2, 0, 1, 3))
    by_rows = lambda g: g.reshape(N_CHIPS, 2, g.shape[0] // (2 * N_CHIPS), g.shape[1])
    c_arr = core.reshape(1).astype(jnp.int32)

    def chip_partials(views, nms):
        landed = _sibling_halves(views, "grad_sibling_halves_" + nms[0])
        return [_add_half(c_arr, g, l, "grad_half_" + nm) for g, l, nm in zip(views, landed, nms)]

    early = ["w_gate", "w_up", "w_out", "w_down"]
    parts = chip_partials([by_cols(gw_gate), by_cols(gw_up), by_rows(gw_out), by_rows(gw_down)], early)
    dq, dk, dv, *slots = _attn_bwd(qkv, do, t_att_bwd, tk_att, parts)
    duc, dcw, dcb, dlg, dlb = _conv_bwd(uc, yconv, dco, cwf, conv_ln_g, conv_ln_b, tm_big)
    grad_x, du, dg1 = _in_bwd(duc, dq, dk, dv, x2, dh1, g_pre_mix, wa, tm_big)
    gw_in = _matmul_tn(a_bf, du, D_MODEL, ts, "grad_w_in", column_block=IN_SH)
    slots += _chip_scatter(chip_partials([gw_in.reshape(N_CHIPS, 2, D_MODEL // 2, IN_SH)], ["w_in"]))
    names = early + ["w_in"]
    halves = [_sum_chips(s, "grad_sum_" + nm) for s, nm in zip(slots, names)]
    others = _share_halves(halves)
    mine = dict(zip(names, halves))
    other = dict(zip(names, others))

    small = [dg1, dcb, dlg, dlb, dag, dg2, dg3, dg4]
    packed = jnp.concatenate([_rows8(s) for s in small] + [_rows8(dcw), _rows8(loss_part)], axis=0)
    red = _allreduce_small(packed)
    sizes = [D_MODEL, CONV_CH, CONV_CH, CONV_CH, CONV_CH, D_MODEL, D_MODEL, D_MODEL]
    g_small = [red[8 * k:8 * k + n // 128].reshape(1, n) for k, n in enumerate(sizes)]
    cw_red = red[64:64 + 128].reshape(HALO, CONV_CH)
    g_cw = lax.dynamic_slice(cw_red, (0, chip * 128), (HALO, 128))
    loss = red[192, 0]

    big = []
    for w, m, v, nm in [(w_in, m_w_in, v_w_in, "w_in"), (w_out, m_w_out, v_w_out, "w_out"),
                        (w_gate, m_w_gate, v_w_gate, "w_gate"), (w_up, m_w_up, v_w_up, "w_up"),
                        (w_down, m_w_down, v_w_down, "w_down")]:
        big.append(_adamw_halves(c_arr, w, mine[nm], other[nm], m, v, "adamw_" + nm))
    sm_w = [g_pre_mix, conv_b, conv_ln_g, conv_ln_b, ag, g_post_mix, g_pre_ffn, g_post_ffn]
    sm_m = [m_g_pre_mix, m_conv_b, m_conv_ln_g, m_conv_ln_b, m_attn_norm_g, m_g_post_mix, m_g_pre_ffn, m_g_post_ffn]
    sm_v = [v_g_pre_mix, v_conv_b, v_conv_ln_g, v_conv_ln_b, v_attn_norm_g, v_g_post_mix, v_g_pre_ffn, v_g_post_ffn]
    pad_cw = lambda a: jnp.pad(a[0, :, 0, :], ((0, 1), (0, 0)))

    def pack(vecs, cw):
        return jnp.concatenate([_rows8(a) for a in vecs] + [cw], axis=0)

    sd, smn, svn = _adamw(pack(sm_w, pad_cw(conv_w)), pack(g_small, g_cw), pack(sm_m, pad_cw(m_conv_w)),
                          pack(sm_v, pad_cw(v_conv_w)), "adamw_small")

    def unpack(p):
        vecs = [p[8 * k:8 * k + n // 128].reshape(1, n) for k, n in enumerate(sizes)]
        return vecs, p[64:64 + CONV_WIDTH].reshape(1, CONV_WIDTH, 1, 128)

    def ordered(vecs, cw, w_in_, w_out_, w_gate_, w_up_, w_down_):
        g1_, cb_, lg_, lb_, ag_, g2_, g3_, g4_ = vecs
        return [g1_, w_in_, cw, cb_, lg_, lb_, ag_.reshape(1, 8, HEAD_DIM), w_out_, g2_, g3_,
                w_gate_, w_up_, w_down_, g4_]

    grads = ordered(g_small, g_cw[:CONV_WIDTH].reshape(1, CONV_WIDTH, 1, 128), *[b[0] for b in big])
    outs = []
    for idx, p in enumerate((sd, smn, svn)):
        vecs, cw = unpack(p)
        outs += ordered(vecs, cw, *[b[idx + 1] for b in big])
    return (loss, grad_x, *grads, *outs)
```

```python
import functools
import math

import jax
import jax.numpy as jnp
from jax import lax
from jax.experimental import pallas as pl
from jax.experimental.pallas import tpu as pltpu

F32 = jnp.float32
BF16 = jnp.bfloat16
MESH = pl.DeviceIdType.MESH

D_MODEL = 1024
CONV_CH = 512
CONV_WIDTH = 31
HEAD_DIM = 64
PAIR = 2 * HEAD_DIM
N_PAIRS = 4
D_FF = 2816
N_CHIPS = 4
IN_SH = 2560 // N_CHIPS
FF_SH = D_FF // N_CHIPS
OUT_SH = D_MODEL // N_CHIPS
C_ROWS = OUT_SH + FF_SH
EPS = 1e-6
HALO = 32

ADAM_LR = 0.001
ADAM_B1 = 0.9
ADAM_B2 = 0.999
ADAM_EPS = 1e-08
ADAM_WD = 0.01
ADAM_STEP = 10

VMEM_LIMIT = 56 * 2 ** 20
VMEM_LIMIT_ATTN_BWD = 62 * 2 ** 20


def _cp(sem=None, vmem=VMEM_LIMIT):
    return pltpu.CompilerParams(dimension_semantics=sem, vmem_limit_bytes=vmem)


def _hbm():
    return pl.BlockSpec(memory_space=pltpu.HBM)


def _const_spec(shape):
    nd = len(shape)
    return pl.BlockSpec(shape, lambda *_: (0,) * nd, pipeline_mode=pl.Buffered(1))


def _dot(a, b):
    return jnp.dot(a, b, preferred_element_type=F32)


def _dot_nt(a, b):
    return lax.dot_general(a, b, (((1,), (1,)), ((), ())), preferred_element_type=F32)


def _dot_tn(a, b):
    return lax.dot_general(a, b, (((0,), (0,)), ((), ())), preferred_element_type=F32)


def _split3(x):
    b0 = x.astype(BF16)
    r1 = x - b0.astype(F32)
    b1 = r1.astype(BF16)
    b2 = (r1 - b1.astype(F32)).astype(BF16)
    return b0, b1, b2


def _split2(x):
    hi = x.astype(BF16)
    lo = (x - hi.astype(F32)).astype(BF16)
    return hi, lo


def _sigmoid(x):
    return 1.0 / (1.0 + jnp.exp(-x))


def _head_mean(x, seg):
    b0, b1, b2 = _split3(x)
    return (_dot(b0, seg) + _dot(b1, seg) + _dot(b2, seg)) * (1.0 / HEAD_DIM)


def _seg_matrix(n):
    r = lax.broadcasted_iota(jnp.int32, (n, n), 0) // HEAD_DIM
    c = lax.broadcasted_iota(jnp.int32, (n, n), 1) // HEAD_DIM
    return (r == c).astype(BF16)


def _rms(x):
    return lax.rsqrt(jnp.mean(x * x, axis=-1, keepdims=True) + EPS)


def _rms_bwd(dy, n, r, g):
    dn = dy * g
    dx = r * (dn - n * jnp.mean(dn * n, axis=-1, keepdims=True))
    return dx, dy * n


class _GatherPlan:
    def __init__(self, srcs, outs, lead, ssem, rsem):
        self.srcs, self.outs, self.lead, self.ssem, self.rsem = srcs, outs, lead, ssem, rsem
        x, y, self.c = lax.axis_index("x"), lax.axis_index("y"), lax.axis_index("c")
        self.me = 2 * x + y
        self.sibling = (x, y, 1 - self.c)
        self.chips = [(1 - x, y), (x, 1 - y), (1 - x, 1 - y)]

    def _half(self, ref, i, h):
        if self.lead[i]:
            return ref.at[h]
        rows = ref.shape[0] // 2
        return ref.at[pl.ds(h * rows, rows)]

    def _ici(self, i, k, origin):
        return pltpu.make_async_remote_copy(
            src_ref=self._half(self.srcs[i], i, self.c), dst_ref=self._half(self.outs[i].at[origin], i, self.c),
            send_sem=self.ssem.at[6 * i + k], recv_sem=self.rsem.at[6 * i + k],
            device_id=(self.chips[k][0], self.chips[k][1], self.c), device_id_type=MESH)

    def _d2d(self, i, k, h):
        origin = 2 * self.chips[k][0] + self.chips[k][1]
        piece = self._half(self.outs[i].at[origin], i, h)
        return pltpu.make_async_remote_copy(
            src_ref=piece, dst_ref=piece, send_sem=self.ssem.at[6 * i + 3 + k],
            recv_sem=self.rsem.at[6 * i + 3 + k], device_id=self.sibling, device_id_type=MESH)

    def _each(self):
        return [(i, k) for i in range(len(self.srcs)) for k in range(3)]

    def start(self):
        for i, k in self._each():
            self._ici(i, k, self.me).start()

    def forward(self):
        for i, k in self._each():
            self._ici(i, k, 2 * self.chips[k][0] + self.chips[k][1]).wait_recv()
            self._d2d(i, k, self.c).start()

    def finish(self):
        for i, k in self._each():
            self._d2d(i, k, 1 - self.c).wait_recv()
        for i, k in self._each():
            self._ici(i, k, self.me).wait_send()
            self._d2d(i, k, self.c).wait_send()


def _gather_shapes(shards):
    return [jax.ShapeDtypeStruct((N_CHIPS,) + s.shape, s.dtype) for s in shards]


def _gather_weights(shards, lead):
    n = len(shards)

    def body(*refs):
        plan = _GatherPlan(refs[:n], refs[n:2 * n], lead, refs[2 * n], refs[2 * n + 1])
        plan.start()
        plan.forward()
        plan.finish()

    return pl.pallas_call(
        body, name="gather_weights", out_shape=_gather_shapes(shards),
        in_specs=[_hbm()] * n, out_specs=[_hbm()] * n,
        scratch_shapes=[pltpu.SemaphoreType.DMA((6 * n,)), pltpu.SemaphoreType.DMA((6 * n,))],
    )(*shards)


def _in_proj(x2, g1, wa, tm):
    S = x2.shape[0]

    def body(x_ref, g_ref, w_ref, a_ref, uc_ref, qkv_ref):
        x = x_ref[...]
        a = (x * _rms(x) * g_ref[...]).astype(BF16)
        a_ref[...] = a
        u = [_dot(a, w_ref[j]) for j in range(N_CHIPS)]
        uc_ref[:, 0:640] = u[0]
        uc_ref[:, 640:1024] = u[1][:, 0:384]
        qkv_ref[:, 0:256] = u[1][:, 384:640].astype(BF16)
        qkv_ref[:, 256:896] = u[2].astype(BF16)
        qkv_ref[:, 896:1536] = u[3].astype(BF16)

    return pl.pallas_call(
        body, name="in_proj", grid=(S // tm,),
        in_specs=[pl.BlockSpec((tm, D_MODEL), lambda i: (i, 0)), _const_spec((1, D_MODEL)),
                  _const_spec(wa.shape)],
        out_specs=[pl.BlockSpec((tm, D_MODEL), lambda i: (i, 0)),
                   pl.BlockSpec((tm, 2 * CONV_CH), lambda i: (i, 0)),
                   pl.BlockSpec((tm, 1536), lambda i: (i, 0))],
        out_shape=[jax.ShapeDtypeStruct((S, D_MODEL), BF16), jax.ShapeDtypeStruct((S, 2 * CONV_CH), F32),
                   jax.ShapeDtypeStruct((S, 1536), BF16)],
        compiler_params=_cp(("parallel",)),
    )(x2, g1, wa)


SUBLANES = 8


def _shift_copies(src_ref, sh_ref):
    rows = sh_ref.shape[1]
    for b in range(1, SUBLANES):
        sh_ref[b - 1] = src_ref[pl.ds(b, rows), :]


def _rows_at(src_ref, sh_ref, off, rows):
    a, b = divmod(off, SUBLANES)
    if b == 0:
        return src_ref[pl.ds(SUBLANES * a, rows), :]
    return sh_ref[b - 1, pl.ds(SUBLANES * a, rows), :]


def _conv_taps(cw_ref, src_ref, sh_ref, offs, rows):
    acc = None
    for w, off in enumerate(offs):
        term = cw_ref[w:w + 1, :] * _rows_at(src_ref, sh_ref, off, rows)
        acc = term if acc is None else acc + term
    return acc


def _glu(uc):
    return uc[:, :CONV_CH] * _sigmoid(uc[:, CONV_CH:])


def _conv_fwd(uc, cwf, cb, lg, lb, tm):
    S = uc.shape[0]
    hb = tm // HALO

    def body(uc_ref, prev_ref, cw_ref, cb_ref, lg_ref, lb_ref, out_ref, y_ref, glu_ref, sh_ref):
        i = pl.program_id(0)
        glu_ref[0:HALO, :] = jnp.where(i == 0, 0.0, _glu(prev_ref[...]))
        glu_ref[HALO:HALO + tm, :] = _glu(uc_ref[...])
        glu_ref[HALO + tm:HALO + tm + SUBLANES, :] = jnp.zeros((SUBLANES, CONV_CH), F32)
        _shift_copies(glu_ref, sh_ref)
        offs = [HALO - (CONV_WIDTH - 1) + w for w in range(CONV_WIDTH)]
        y = _conv_taps(cw_ref, glu_ref, sh_ref, offs, tm) + cb_ref[...]
        y_ref[...] = y
        mu = jnp.mean(y, axis=-1, keepdims=True)
        yc = y - mu
        rstd = lax.rsqrt(jnp.mean(yc * yc, axis=-1, keepdims=True) + EPS)
        ln = yc * rstd * lg_ref[...] + lb_ref[...]
        out_ref[...] = (ln * _sigmoid(ln)).astype(BF16)

    return pl.pallas_call(
        body, name="conv_fwd", grid=(S // tm,),
        in_specs=[pl.BlockSpec((tm, 2 * CONV_CH), lambda i: (i, 0)),
                  pl.BlockSpec((HALO, 2 * CONV_CH), lambda i: (jnp.maximum(i * hb - 1, 0), 0)),
                  _const_spec(cwf.shape), _const_spec((1, CONV_CH)), _const_spec((1, CONV_CH)),
                  _const_spec((1, CONV_CH))],
        out_specs=[pl.BlockSpec((tm, CONV_CH), lambda i: (i, 0))] * 2,
        out_shape=[jax.ShapeDtypeStruct((S, CONV_CH), BF16), jax.ShapeDtypeStruct((S, CONV_CH), F32)],
        scratch_shapes=[pltpu.VMEM((HALO + tm + SUBLANES, CONV_CH), F32),
                        pltpu.VMEM((SUBLANES - 1, HALO + tm, CONV_CH), F32)],
        compiler_params=_cp(("parallel",)),
    )(uc, uc, cwf, cb, lg, lb)


def _lane_mask(h):
    lane = lax.broadcasted_iota(jnp.int32, (1, PAIR), 1)
    return (lane >= HEAD_DIM * h) & (lane < HEAD_DIM * (h + 1))


def _neg_abs(x):
    bits = lax.bitcast_convert_type(x, jnp.uint32) | jnp.uint32(0x80000000)
    return lax.bitcast_convert_type(bits, F32)


def _tri_dot(x, m):
    return _dot(x.astype(BF16), m)


MASKED = -1e30
KEY_BLOCKS = 4


def _running_sums(x, m, reverse, start=None):
    t = m.shape[0]
    blocks = x.shape[1] // t
    order = range(blocks - 1, -1, -1) if reverse else range(blocks)
    out = [None] * blocks
    carry = start
    for b in order:
        xb = x[:, b * t:(b + 1) * t]
        cb = _tri_dot(xb, m)
        out[b] = cb if carry is None else cb + carry
        rs = jnp.sum(xb, axis=1, keepdims=True)
        carry = rs if carry is None else carry + rs
    return jnp.concatenate(out, axis=1), carry


def _sb_tile(z, r, m_suf):
    sp = jnp.maximum(z, 0.0) + jnp.log(1.0 + jnp.exp(_neg_abs(z)))
    c, rs = _running_sums(sp, m_suf, reverse=True, start=r)
    return jnp.exp(z - c), sp, rs


def _scores(qm, kt, mask):
    z = _dot_nt(qm, kt)
    return z if mask is None else jnp.where(mask, z, MASKED)


def _causal_mask(i, sb, t, tk, w=None):
    w = tk if w is None else w
    row = lax.broadcasted_iota(jnp.int32, (t, w), 0) + i * t
    col = lax.broadcasted_iota(jnp.int32, (t, w), 1) + sb * tk
    return col < row


def _sweep_plain(first, count, tile):
    def step(n, carry):
        tile(first + n)
        return carry

    lax.fori_loop(0, count + 1, step, 0)


def _sweep(first, count, down, fetch, load, work):
    lo, hi = (first - count, first) if down else (first, first + count)
    tile = lambda j: jnp.clip(first - j if down else first + j, lo, hi)
    fetch(first, 0, True)

    def step(n, carry):
        j = 2 * n
        vals = load(0)
        fetch(tile(j + 1), 1, False)
        work(tile(j), vals)
        vals = load(1)
        fetch(tile(j + 2), 0, False)
        work(tile(j + 1), vals)
        return carry

    lax.fori_loop(0, (count + 1) // 2, step, 0)

    @pl.when(lax.rem(count, 2) == 0)
    def _():
        work(tile(count), load(0))


def _suffix_matrix(t, prefix=False):
    row = lax.broadcasted_iota(jnp.int32, (t, t), 0)
    col = lax.broadcasted_iota(jnp.int32, (t, t), 1)
    return ((row <= col) if prefix else (row >= col)).astype(BF16)


def _attn_fwd(qkv, t, tk, shards, lead):
    S = qkv.shape[0]

    ng = len(shards)
    nq = S // t

    def body(*refs):
        q_ref, k_ref, v_ref = refs[:3]
        o_ref = refs[3 + ng]
        acc_ref, r_ref, z_buf, ssem, rsem = refs[4 + 2 * ng:]
        p = pl.program_id(0)
        i = pl.program_id(1)
        plan = _GatherPlan(refs[3:3 + ng], refs[4 + ng:4 + 2 * ng], lead, ssem, rsem)
        pl.when((p == 0) & (i == 0))(plan.start)
        pl.when((p == 1) & (i == 0))(plan.forward)
        last = (i * t + t - 1) // tk
        m_suf = _suffix_matrix(tk // KEY_BLOCKS)
        q = q_ref[...]
        hms = [_lane_mask(h) for h in range(2)]
        qms = [jnp.where(hm, q, 0) * 0.125 for hm in hms]
        acc_ref[...] = jnp.zeros_like(acc_ref)
        r_ref[...] = jnp.zeros_like(r_ref)

        def rows(sb):
            return pl.ds(pl.multiple_of(sb * tk, tk), tk)

        def fetch(sb, slot, diagonal):
            kt = k_ref[rows(sb), :]
            mask = _causal_mask(i, sb, t, tk) if diagonal else None
            for h in range(2):
                z_buf[slot, h] = _scores(qms[h], kt, mask)

        def load(slot):
            return [z_buf[slot, h] for h in range(2)]

        def work(sb, zs):
            vt = v_ref[rows(sb), :]
            for h in range(2):
                a_loc, _, rs = _sb_tile(zs[h], None, m_suf)
                r = r_ref[h]
                acc_ref[...] += _dot(a_loc.astype(BF16), jnp.where(hms[h], vt, 0)) * jnp.exp(-r)
                r_ref[h] = r + rs

        _sweep(last, last, True, fetch, load, work)
        o_ref[...] = acc_ref[...]
        pl.when((p == N_PAIRS - 1) & (i == nq - 1))(plan.finish)

    return pl.pallas_call(
        body, name="attn_fwd", grid=(N_PAIRS, nq),
        in_specs=[pl.BlockSpec((t, PAIR), lambda p, i: (i, p)),
                  pl.BlockSpec((S, PAIR), lambda p, i: (0, N_PAIRS + p)),
                  pl.BlockSpec((S, PAIR), lambda p, i: (0, 2 * N_PAIRS + p))] + [_hbm()] * ng,
        out_specs=[pl.BlockSpec((t, PAIR), lambda p, i: (i, p))] + [_hbm()] * ng,
        out_shape=[jax.ShapeDtypeStruct((S, N_PAIRS * PAIR), F32)] + _gather_shapes(shards),
        scratch_shapes=[pltpu.VMEM((t, PAIR), F32), pltpu.VMEM((2, t, 1), F32),
                        pltpu.VMEM((2, 2, t, tk), F32),
                        pltpu.SemaphoreType.DMA((6 * ng,)), pltpu.SemaphoreType.DMA((6 * ng,))],
        compiler_params=_cp(("arbitrary", "arbitrary")),
    )(qkv, qkv, qkv, *shards)


def _out_proj(conv_out, o, ag, wc, x2, g2, g3, tm):
    S = o.shape[0]

    def body(co_ref, o_ref, ag_ref, w_ref, x_ref, g2_ref, g3_ref, mix_ref, y_ref, h1_ref, fin_ref):
        seg = _seg_matrix(CONV_CH)
        o = o_ref[...]
        att = (o * lax.rsqrt(_head_mean(o * o, seg) + EPS) * ag_ref[...]).astype(BF16)
        co = co_ref[...]
        mix_ref[:, :CONV_CH] = co
        mix_ref[:, CONV_CH:] = att
        y = _dot(co, w_ref[0:CONV_CH, :]) + _dot(att, w_ref[CONV_CH:, :])
        y_ref[...] = y
        h1 = x_ref[...] + y * _rms(y) * g2_ref[...]
        h1_ref[...] = h1
        fin_ref[...] = (h1 * _rms(h1) * g3_ref[...]).astype(BF16)

    row = lambda w: pl.BlockSpec((tm, w), lambda i: (i, 0))
    return pl.pallas_call(
        body, name="out_proj", grid=(S // tm,),
        in_specs=[row(CONV_CH), row(CONV_CH), _const_spec((1, CONV_CH)), _const_spec(wc.shape),
                  row(D_MODEL), _const_spec((1, D_MODEL)), _const_spec((1, D_MODEL))],
        out_specs=[row(D_MODEL)] * 4,
        out_shape=[jax.ShapeDtypeStruct((S, D_MODEL), BF16), jax.ShapeDtypeStruct((S, D_MODEL), F32),
                   jax.ShapeDtypeStruct((S, D_MODEL), F32), jax.ShapeDtypeStruct((S, D_MODEL), BF16)],
        compiler_params=_cp(("parallel",)),
    )(conv_out, o, ag, wc, x2, g2, g3)


def _ffn_fwd(f_in, h1, tgt, wg, wu, wd, g4, tm):
    S = f_in.shape[0]

    def body(fin_ref, h1_ref, tgt_ref, wg_ref, wu_ref, wd_ref, g4_ref, df_ref, dh2_ref, dg4_ref, loss_ref,
             gt_ref, up_ref, act_ref):
        i = pl.program_id(0)
        fin = fin_ref[...]
        gt = _dot(fin, wg_ref[...])
        up = _dot(fin, wu_ref[...])
        act = (gt * _sigmoid(gt) * up).astype(BF16)
        gt_ref[...] = gt.astype(BF16)
        up_ref[...] = up.astype(BF16)
        act_ref[...] = act
        f = _dot(act, wd_ref[...])
        r = _rms(f)
        n = f * r
        g4 = g4_ref[...]
        err = h1_ref[...] + n * g4 - tgt_ref[...]
        dh2 = err * (1.0 / D_MODEL)
        dh2_ref[...] = dh2
        df, dg = _rms_bwd(dh2, n, r, g4)
        df_ref[...] = df.astype(BF16)

        @pl.when(i == 0)
        def _():
            dg4_ref[...] = jnp.zeros_like(dg4_ref)
            loss_ref[...] = jnp.zeros_like(loss_ref)

        dg4_ref[...] += jnp.sum(dg, axis=0, keepdims=True)
        part = jnp.sum(jnp.sum(err * err, axis=1, keepdims=True), axis=0, keepdims=True)
        loss_ref[...] += part * (0.5 / D_MODEL)

    row = lambda w: pl.BlockSpec((tm, w), lambda i: (i, 0))
    return pl.pallas_call(
        body, name="ffn_fwd", grid=(S // tm,),
        in_specs=[row(D_MODEL), row(D_MODEL), row(D_MODEL), _const_spec(wg.shape), _const_spec(wu.shape),
                  _const_spec(wd.shape), _const_spec((1, D_MODEL))],
        out_specs=[row(D_MODEL), row(D_MODEL), pl.BlockSpec((1, D_MODEL), lambda i: (0, 0)),
                   pl.BlockSpec((1, 128), lambda i: (0, 0)), row(D_FF), row(D_FF), row(D_FF)],
        out_shape=[jax.ShapeDtypeStruct((S, D_MODEL), BF16), jax.ShapeDtypeStruct((S, D_MODEL), F32),
                   jax.ShapeDtypeStruct((1, D_MODEL), F32), jax.ShapeDtypeStruct((1, 128), F32)]
        + [jax.ShapeDtypeStruct((S, D_FF), BF16)] * 3,
        compiler_params=_cp(("arbitrary",)),
    )(f_in, h1, tgt, wg, wu, wd, g4)


def _ffn_bwd(gt_bf, up_bf, df, dh2, h1, yv, wg, wu, wd, g3, g2, tm):
    S = df.shape[0]

    def body(gt_ref, up_ref, df_ref, dh2_ref, h1_ref, y_ref, wg_ref, wu_ref, wd_ref, g3_ref, g2_ref,
             dgt_ref, dup_ref, dh1_ref, dy_ref, dg3_ref, dg2_ref):
        i = pl.program_id(0)
        df = df_ref[...]
        gt = gt_ref[...].astype(F32)
        up = up_ref[...].astype(F32)
        sg = _sigmoid(gt)
        silu = gt * sg
        dact = _dot_nt(df, wd_ref[...])
        dgt = (dact * up * (sg * (1.0 + gt * (1.0 - sg)))).astype(BF16)
        dup = (dact * silu).astype(BF16)
        dgt_ref[...] = dgt
        dup_ref[...] = dup
        dfin = _dot_nt(dgt, wg_ref[...]) + _dot_nt(dup, wu_ref[...])
        h1 = h1_ref[...]
        r3 = _rms(h1)
        dh1_n, dg3 = _rms_bwd(dfin, h1 * r3, r3, g3_ref[...])
        dh1 = dh2_ref[...] + dh1_n
        dh1_ref[...] = dh1
        y = y_ref[...]
        r2 = _rms(y)
        dy, dg2 = _rms_bwd(dh1, y * r2, r2, g2_ref[...])
        dy_ref[...] = dy.astype(BF16)

        @pl.when(i == 0)
        def _():
            dg3_ref[...] = jnp.zeros_like(dg3_ref)
            dg2_ref[...] = jnp.zeros_like(dg2_ref)

        dg3_ref[...] += jnp.sum(dg3, axis=0, keepdims=True)
        dg2_ref[...] += jnp.sum(dg2, axis=0, keepdims=True)

    row = lambda w: pl.BlockSpec((tm, w), lambda i: (i, 0))
    vec = pl.BlockSpec((1, D_MODEL), lambda i: (0, 0))
    return pl.pallas_call(
        body, name="ffn_bwd", grid=(S // tm,),
        in_specs=[row(D_FF), row(D_FF)] + [row(D_MODEL)] * 4
        + [_const_spec(wg.shape), _const_spec(wu.shape), _const_spec(wd.shape),
           _const_spec((1, D_MODEL)), _const_spec((1, D_MODEL))],
        out_specs=[row(D_FF), row(D_FF), row(D_MODEL), row(D_MODEL), vec, vec],
        out_shape=[jax.ShapeDtypeStruct((S, D_FF), BF16)] * 2
        + [jax.ShapeDtypeStruct((S, D_MODEL), F32), jax.ShapeDtypeStruct((S, D_MODEL), BF16),
           jax.ShapeDtypeStruct((1, D_MODEL), F32), jax.ShapeDtypeStruct((1, D_MODEL), F32)],
        compiler_params=_cp(("arbitrary",)),
    )(gt_bf, up_bf, df, dh2, h1, yv, wg, wu, wd, g3, g2)


def _out_bwd(dy, o, ag, wc, tm):
    S = o.shape[0]

    def body(dy_ref, o_ref, ag_ref, w_ref, dco_ref, do_ref, dag_ref):
        i = pl.program_id(0)
        seg = _seg_matrix(CONV_CH)
        dy = dy_ref[...]
        dco_ref[...] = _dot_nt(dy, w_ref[0:CONV_CH, :])
        datt = _dot_nt(dy, w_ref[CONV_CH:, :])
        o = o_ref[...]
        r = lax.rsqrt(_head_mean(o * o, seg) + EPS)
        n = o * r
        dn = datt * ag_ref[...]
        do_ref[...] = (r * (dn - n * _head_mean(dn * n, seg))).astype(BF16)

        @pl.when(i == 0)
        def _():
            dag_ref[...] = jnp.zeros_like(dag_ref)

        dag_ref[...] += jnp.sum(datt * n, axis=0, keepdims=True)

    row = lambda w: pl.BlockSpec((tm, w), lambda i: (i, 0))
    return pl.pallas_call(
        body, name="out_bwd", grid=(S // tm,),
        in_specs=[row(D_MODEL), row(CONV_CH), _const_spec((1, CONV_CH)), _const_spec(wc.shape)],
        out_specs=[row(CONV_CH), row(CONV_CH), pl.BlockSpec((1, CONV_CH), lambda i: (0, 0))],
        out_shape=[jax.ShapeDtypeStruct((S, CONV_CH), F32), jax.ShapeDtypeStruct((S, CONV_CH), BF16),
                   jax.ShapeDtypeStruct((1, CONV_CH), F32)],
        compiler_params=_cp(("arbitrary",)),
    )(dy, o, ag, wc)


def _attn_bwd(qkv, do, t, tk, parts):
    S = qkv.shape[0]
    nk = S // tk
    ns = len(parts)

    def body(*refs):
        q_ref, k_ref, v_ref, do_ref = refs[:4]
        dq_ref, dk_hbm, dv_hbm = refs[4 + ns:7 + ns]
        g_buf, s_buf, r_ref, dq_acc, dk_ref, dv_ref, z_buf, da_buf = refs[7 + 2 * ns:15 + 2 * ns]
        p = pl.program_id(0)
        i = pl.program_id(1)
        plan = _ScatterPlan(refs[4:4 + ns], refs[7 + ns:7 + 2 * ns], *refs[15 + 2 * ns:])
        pl.when((p == 0) & (i == 0))(plan.start)
        last = (i * t + t - 1) // tk

        @pl.when(i == 0)
        def _():
            dk_ref[...] = jnp.zeros_like(dk_ref)
            dv_ref[...] = jnp.zeros_like(dv_ref)

        m_suf = _suffix_matrix(tk // KEY_BLOCKS)
        m_pre = _suffix_matrix(tk // KEY_BLOCKS, prefix=True)
        q = q_ref[...]
        do = do_ref[...]
        hms = [_lane_mask(h) for h in range(2)]
        qms = [jnp.where(hm, q, 0) * 0.125 for hm in hms]
        doms = [jnp.where(hm, do, 0) for hm in hms]
        dq_acc[...] = jnp.zeros_like(dq_acc)
        r_ref[...] = jnp.zeros_like(r_ref)

        def keys(sb, w=tk):
            return pl.ds(pl.multiple_of(sb * tk, tk), w)

        def matmuls1(sb, w, diagonal):
            kt = k_ref[keys(sb, w), :]
            vt = v_ref[keys(sb, w), :]
            mask = _causal_mask(i, sb, t, tk, w) if diagonal else None
            return [(_scores(qms[h], kt, mask), _dot_nt(doms[h], vt)) for h in range(2)]

        def sweep1(sb, w, vals):
            dv = jnp.zeros((w, PAIR), F32)
            for h in range(2):
                z, da = vals[h]
                A, sp, r_ref[h] = _sb_tile(z, r_ref[h], m_suf)
                g_buf[h, sb, :, 0:w] = A * da
                s_buf[h, sb, :, 0:w] = 1.0 - jnp.exp(-sp)
                dv = dv + _dot_tn(A.astype(BF16), doms[h])
            dv_ref[keys(sb, w), :] += dv

        def sweep2(sb, w):
            kt = k_ref[keys(sb, w), :]
            dk = jnp.zeros((w, PAIR), F32)
            for h in range(2):
                g = g_buf[h, sb, :, 0:w]
                pre, r_ref[h] = _running_sums(g, m_pre, reverse=False, start=r_ref[h])
                dzb = (g - s_buf[h, sb, :, 0:w] * pre).astype(BF16)
                dq_acc[...] += _dot(dzb, jnp.where(hms[h], kt, 0))
                dk = dk + _dot_tn(dzb, qms[h])
            dk_ref[keys(sb, w), :] += dk

        def diagonal_tile(tile):
            for nb in range(1, KEY_BLOCKS + 1):
                pl.when(lax.rem(i, KEY_BLOCKS) == nb - 1)(functools.partial(tile, nb * t))

        def fetch1(sb, slot, first):
            for h, (z, da) in enumerate(matmuls1(sb, tk, False)):
                z_buf[slot, h] = z
                da_buf[slot, h] = da

        def load1(slot):
            return [(z_buf[slot, h], da_buf[slot, h]) for h in range(2)]

        diagonal_tile(lambda w: sweep1(last, w, matmuls1(last, w, True)))
        pl.when(last >= 1)(lambda: _sweep(last - 1, last - 1, True, fetch1, load1,
                                          lambda sb, vals: sweep1(sb, tk, vals)))
        r_ref[...] = jnp.zeros_like(r_ref)
        pl.when(last >= 1)(lambda: _sweep_plain(0, last - 1, lambda sb: sweep2(sb, tk)))
        diagonal_tile(lambda w: sweep2(last, w))
        dq_ref[...] = dq_acc[...] * 0.125

        @pl.when(i == S // t - 1)
        def _():
            cols = pl.ds(pl.multiple_of(p * PAIR, PAIR), PAIR)
            pltpu.sync_copy(dk_ref, dk_hbm.at[:, cols])
            pltpu.sync_copy(dv_ref, dv_hbm.at[:, cols])

        pl.when((p == N_PAIRS - 1) & (i == S // t - 1))(plan.finish)

    once = lambda cb: pl.BlockSpec((S, PAIR), cb, pipeline_mode=pl.Buffered(1))
    return pl.pallas_call(
        body, name="attn_bwd", grid=(N_PAIRS, S // t),
        in_specs=[pl.BlockSpec((t, PAIR), lambda p, i: (i, p)),
                  once(lambda p, i: (0, N_PAIRS + p)), once(lambda p, i: (0, 2 * N_PAIRS + p)),
                  pl.BlockSpec((t, PAIR), lambda p, i: (i, p))] + [_hbm()] * ns,
        out_specs=[pl.BlockSpec((t, PAIR), lambda p, i: (i, p)), _hbm(), _hbm()] + [_hbm()] * ns,
        out_shape=[jax.ShapeDtypeStruct((S, N_PAIRS * PAIR), F32)] * 3
        + [jax.ShapeDtypeStruct(pt.shape, pt.dtype) for pt in parts],
        scratch_shapes=[pltpu.VMEM((2, nk, t, tk), F32), pltpu.VMEM((2, nk, t, tk), F32),
                        pltpu.VMEM((2, t, 1), F32), pltpu.VMEM((t, PAIR), F32),
                        pltpu.VMEM((S, PAIR), F32), pltpu.VMEM((S, PAIR), F32),
                        pltpu.VMEM((2, 2, t, tk), F32), pltpu.VMEM((2, 2, t, tk), F32)] + _scatter_sems(ns),
        compiler_params=_cp(("arbitrary", "arbitrary"), vmem=VMEM_LIMIT_ATTN_BWD),
    )(qkv, qkv, qkv, do, *parts)


def _conv_bwd(uc, yconv, dco, cwf, lg, lb, tm):
    S = uc.shape[0]
    hb = tm // HALO
    nb = S // tm
    ext = tm + HALO

    def body(uc_ref, prev_ref, y_ref, ynext_ref, dco_ref, dnext_ref, cw_ref, lg_ref, lb_ref,
             duc_ref, dcw_ref, dcb_ref, dlg_ref, dlb_ref, glu_ref, dyc_ref, shg_ref, shd_ref):
        i = pl.program_id(0)
        last = i == nb - 1

        @pl.when(i == 0)
        def _():
            for ref in (dcw_ref, dcb_ref, dlg_ref, dlb_ref):
                ref[...] = jnp.zeros_like(ref)

        uc = uc_ref[...]
        glu_ref[0:HALO, :] = jnp.where(i == 0, 0.0, _glu(prev_ref[...]))
        glu_ref[HALO:ext, :] = _glu(uc)
        glu_ref[ext:ext + SUBLANES, :] = jnp.zeros((SUBLANES, CONV_CH), F32)
        _shift_copies(glu_ref, shg_ref)
        fwd_offs = [HALO - (CONV_WIDTH - 1) + w for w in range(CONV_WIDTH)]
        y = jnp.concatenate([y_ref[...], ynext_ref[...]], axis=0)
        mu = jnp.mean(y, axis=-1, keepdims=True)
        yc = y - mu
        rstd = lax.rsqrt(jnp.mean(yc * yc, axis=-1, keepdims=True) + EPS)
        yhat = yc * rstd
        lg = lg_ref[...]
        ln = yhat * lg + lb_ref[...]
        sg = _sigmoid(ln)
        dout = jnp.concatenate([dco_ref[...], jnp.where(last, 0.0, dnext_ref[...])], axis=0)
        dln = dout * (sg * (1.0 + ln * (1.0 - sg)))
        dyh = dln * lg
        dyc = rstd * (dyh - jnp.mean(dyh, axis=-1, keepdims=True)
                      - yhat * jnp.mean(dyh * yhat, axis=-1, keepdims=True))
        dyc_ref[0:ext, :] = dyc
        dyc_ref[ext:ext + SUBLANES, :] = jnp.zeros((SUBLANES, CONV_CH), F32)
        _shift_copies(dyc_ref, shd_ref)
        dlg_ref[...] += jnp.sum((dln * yhat)[0:tm], axis=0, keepdims=True)
        dlb_ref[...] += jnp.sum(dln[0:tm], axis=0, keepdims=True)
        dcb_ref[...] += jnp.sum(dyc[0:tm], axis=0, keepdims=True)
        dglu = _conv_taps(cw_ref, dyc_ref, shd_ref, [CONV_WIDTH - 1 - w for w in range(CONV_WIDTH)], tm)
        d0 = dyc[0:tm]
        for w, off in enumerate(fwd_offs):
            dcw_ref[w:w + 1, :] += jnp.sum(d0 * _rows_at(glu_ref, shg_ref, off, tm), axis=0, keepdims=True)
        val, gate = uc[:, :CONV_CH], uc[:, CONV_CH:]
        sgate = _sigmoid(gate)
        duc_ref[:, :CONV_CH] = (dglu * sgate).astype(BF16)
        duc_ref[:, CONV_CH:] = (dglu * val * sgate * (1.0 - sgate)).astype(BF16)

    vec = pl.BlockSpec((1, CONV_CH), lambda i: (0, 0))
    nxt = lambda i: (jnp.minimum((i + 1) * hb, S // HALO - 1), 0)
    return pl.pallas_call(
        body, name="conv_bwd", grid=(nb,),
        in_specs=[pl.BlockSpec((tm, 2 * CONV_CH), lambda i: (i, 0)),
                  pl.BlockSpec((HALO, 2 * CONV_CH), lambda i: (jnp.maximum(i * hb - 1, 0), 0)),
                  pl.BlockSpec((tm, CONV_CH), lambda i: (i, 0)), pl.BlockSpec((HALO, CONV_CH), nxt),
                  pl.BlockSpec((tm, CONV_CH), lambda i: (i, 0)), pl.BlockSpec((HALO, CONV_CH), nxt),
                  _const_spec(cwf.shape), _const_spec((1, CONV_CH)), _const_spec((1, CONV_CH))],
        out_specs=[pl.BlockSpec((tm, 2 * CONV_CH), lambda i: (i, 0)),
                   pl.BlockSpec(cwf.shape, lambda i: (0, 0)), vec, vec, vec],
        out_shape=[jax.ShapeDtypeStruct((S, 2 * CONV_CH), BF16), jax.ShapeDtypeStruct(cwf.shape, F32)]
        + [jax.ShapeDtypeStruct((1, CONV_CH), F32)] * 3,
        scratch_shapes=[pltpu.VMEM((ext + SUBLANES, CONV_CH), F32), pltpu.VMEM((ext + SUBLANES, CONV_CH), F32),
                        pltpu.VMEM((SUBLANES - 1, ext, CONV_CH), F32),
                        pltpu.VMEM((SUBLANES - 1, ext, CONV_CH), F32)],
        compiler_params=_cp(("arbitrary",)),
    )(uc, uc, yconv, yconv, dco, dco, cwf, lg, lb)


def _in_bwd(duc, dq, dk, dv, x2, dh1, g1, wa, tm):
    S = x2.shape[0]

    def body(duc_ref, dq_ref, dk_ref, dv_ref, x_ref, dh1_ref, g_ref, w_ref, gx_ref, du_ref, dg_ref):
        i = pl.program_id(0)
        du = jnp.concatenate([duc_ref[...], dq_ref[...].astype(BF16), dk_ref[...].astype(BF16),
                              dv_ref[...].astype(BF16)], axis=1)
        du_ref[...] = du
        da = _dot_nt(du[:, 0:IN_SH], w_ref[0])
        for j in range(1, N_CHIPS):
            da = da + _dot_nt(du[:, IN_SH * j:IN_SH * (j + 1)], w_ref[j])
        x = x_ref[...]
        r = _rms(x)
        dx, dg = _rms_bwd(da, x * r, r, g_ref[...])
        gx_ref[...] = dh1_ref[...] + dx

        @pl.when(i == 0)
        def _():
            dg_ref[...] = jnp.zeros_like(dg_ref)

        dg_ref[...] += jnp.sum(dg, axis=0, keepdims=True)

    row = lambda w: pl.BlockSpec((tm, w), lambda i: (i, 0))
    return pl.pallas_call(
        body, name="in_bwd", grid=(S // tm,),
        in_specs=[row(2 * CONV_CH), row(CONV_CH), row(CONV_CH), row(CONV_CH), row(D_MODEL), row(D_MODEL),
                  _const_spec((1, D_MODEL)), _const_spec(wa.shape)],
        out_specs=[pl.BlockSpec((None, tm, D_MODEL), lambda i: (0, i, 0)), row(2560),
                   pl.BlockSpec((1, D_MODEL), lambda i: (0, 0))],
        out_shape=[jax.ShapeDtypeStruct((1, S, D_MODEL), F32), jax.ShapeDtypeStruct((S, 2560), BF16),
                   jax.ShapeDtypeStruct((1, D_MODEL), F32)],
        compiler_params=_cp(("arbitrary",)),
    )(duc, dq, dk, dv, x2, dh1, g1, wa)


def _matmul_tn(xm, ym, tm, ts, name, column_block=None):
    S, M = xm.shape
    N = ym.shape[1]

    def body(x_ref, y_ref, o_ref):
        @pl.when(pl.program_id(1) == 0)
        def _():
            o_ref[...] = jnp.zeros_like(o_ref)

        xt = x_ref[...].T
        if column_block is None:
            o_ref[...] += _dot(xt, y_ref[...])
        else:
            for j in range(N // column_block):
                o_ref[j] += _dot(xt, y_ref[:, column_block * j:column_block * (j + 1)])

    if column_block is None:
        out_spec = pl.BlockSpec((tm, N), lambda m, s: (m, 0))
        out_shape = jax.ShapeDtypeStruct((M, N), F32)
    else:
        out_spec = pl.BlockSpec((N // column_block, tm, column_block), lambda m, s: (0, m, 0))
        out_shape = jax.ShapeDtypeStruct((N // column_block, M, column_block), F32)
    return pl.pallas_call(
        body, name=name, grid=(M // tm, S // ts),
        in_specs=[pl.BlockSpec((ts, tm), lambda m, s: (s, m)), pl.BlockSpec((ts, N), lambda m, s: (s, 0))],
        out_specs=out_spec, out_shape=out_shape,
        compiler_params=_cp(("parallel", "arbitrary")),
    )(xm, ym)


def _sibling_halves(grads, name):
    n = len(grads)

    def body(*refs):
        ins, outs, ssem, rsem = refs[:n], refs[n:2 * n], refs[2 * n], refs[2 * n + 1]
        x, y, c = lax.axis_index("x"), lax.axis_index("y"), lax.axis_index("c")
        copies = []
        for k in range(n):
            for j in range(N_CHIPS):
                copies.append(pltpu.make_async_remote_copy(
                    src_ref=ins[k].at[j, 1 - c], dst_ref=outs[k].at[j],
                    send_sem=ssem.at[N_CHIPS * k + j], recv_sem=rsem.at[N_CHIPS * k + j],
                    device_id=(x, y, 1 - c), device_id_type=MESH))
        for cp in copies:
            cp.start()
        for cp in copies:
            cp.wait()

    shapes = [jax.ShapeDtypeStruct((g.shape[0],) + g.shape[2:], F32) for g in grads]
    return pl.pallas_call(
        body, name=name, out_shape=shapes,
        in_specs=[_hbm()] * n, out_specs=[_hbm()] * n,
        scratch_shapes=[pltpu.SemaphoreType.DMA((N_CHIPS * n,)), pltpu.SemaphoreType.DMA((N_CHIPS * n,))],
    )(*grads)


def _add_half(c_arr, g, landed, name):
    def body(c_ref, g_ref, l_ref, o_ref):
        o_ref[...] = (g_ref[...] + l_ref[...]).astype(BF16)

    rows, n = g.shape[2], g.shape[3]
    grid = (N_CHIPS,)
    g_spec = pl.BlockSpec((None, None, rows, n), lambda j, c: (j, c[0], 0, 0))
    l_spec = pl.BlockSpec((None, rows, n), lambda j, c: (j, 0, 0))
    return pl.pallas_call(
        body, name=name,
        grid_spec=pltpu.PrefetchScalarGridSpec(num_scalar_prefetch=1, grid=grid, in_specs=[g_spec, l_spec],
                                               out_specs=l_spec),
        out_shape=jax.ShapeDtypeStruct(landed.shape, BF16),
        compiler_params=_cp(("parallel",)),
    )(c_arr, g, landed)


class _ScatterPlan:
    def __init__(self, ins, outs, lsem, ssem, rsem):
        x, y, c = lax.axis_index("x"), lax.axis_index("y"), lax.axis_index("c")
        me = 2 * x + y
        self.copies = []
        for k in range(len(ins)):
            self.copies.append(pltpu.make_async_copy(ins[k].at[me], outs[k].at[me], lsem.at[k]))
            for r, chip in enumerate([(1 - x, y), (x, 1 - y), (1 - x, 1 - y)]):
                self.copies.append(pltpu.make_async_remote_copy(
                    src_ref=ins[k].at[2 * chip[0] + chip[1]], dst_ref=outs[k].at[me],
                    send_sem=ssem.at[3 * k + r], recv_sem=rsem.at[3 * k + r],
                    device_id=(chip[0], chip[1], c), device_id_type=MESH))

    def start(self):
        for cp in self.copies:
            cp.start()

    def finish(self):
        for cp in self.copies:
            cp.wait()


def _scatter_sems(n):
    return [pltpu.SemaphoreType.DMA((n,)), pltpu.SemaphoreType.DMA((3 * n,)), pltpu.SemaphoreType.DMA((3 * n,))]


def _chip_scatter(parts):
    n = len(parts)

    def body(*refs):
        plan = _ScatterPlan(refs[:n], refs[n:2 * n], *refs[2 * n:])
        plan.start()
        plan.finish()

    shapes = [jax.ShapeDtypeStruct(p.shape, p.dtype) for p in parts]
    return pl.pallas_call(
        body, name="grad_chip_scatter", out_shape=shapes,
        in_specs=[_hbm()] * n, out_specs=[_hbm()] * n, scratch_shapes=_scatter_sems(n),
    )(*parts)


def _sum_chips(landed, name):
    _, rows, n = landed.shape
    tr = 256 if rows % 256 == 0 else rows

    def body(a_ref, b_ref, c_ref, d_ref, o_ref):
        f = lambda ref: ref[...].astype(F32)
        o_ref[...] = ((f(a_ref) + f(b_ref)) + f(c_ref)) + f(d_ref)

    specs = [pl.BlockSpec((None, tr, n), functools.partial(lambda i, j: (j, i, 0), j=j)) for j in range(N_CHIPS)]
    return pl.pallas_call(
        body, name=name, grid=(rows // tr,), in_specs=specs,
        out_specs=pl.BlockSpec((tr, n), lambda i: (i, 0)),
        out_shape=jax.ShapeDtypeStruct((rows, n), F32),
        compiler_params=_cp(("parallel",)),
    )(landed, landed, landed, landed)


def _share_halves(halves):
    n = len(halves)

    def body(*refs):
        ins, outs = refs[:n], refs[n:2 * n]
        ssem, rsem = refs[2 * n:]
        x, y, c = lax.axis_index("x"), lax.axis_index("y"), lax.axis_index("c")
        copies = [pltpu.make_async_remote_copy(
            src_ref=ins[k], dst_ref=outs[k], send_sem=ssem.at[k], recv_sem=rsem.at[k],
            device_id=(x, y, 1 - c), device_id_type=MESH) for k in range(n)]
        for cp in copies:
            cp.start()
        for cp in copies:
            cp.wait()

    shapes = [jax.ShapeDtypeStruct(h.shape, F32) for h in halves]
    return pl.pallas_call(
        body, name="grad_share_halves", out_shape=shapes,
        in_specs=[_hbm()] * n, out_specs=[_hbm()] * n,
        scratch_shapes=[pltpu.SemaphoreType.DMA((n,)), pltpu.SemaphoreType.DMA((n,))],
    )(*halves)


def _allreduce_small(packed):
    rows, n = packed.shape

    def body(in_ref, out_ref, land_ref, ssem, rsem):
        x, y, c = lax.axis_index("x"), lax.axis_index("y"), lax.axis_index("c")
        me = 4 * x + 2 * y + c
        land_ref[me] = in_ref[...]
        copies = []
        for r in range(1, 8):
            tx = 1 - x if r & 4 else x
            ty = 1 - y if r & 2 else y
            tc = 1 - c if r & 1 else c
            cp = pltpu.make_async_remote_copy(
                src_ref=in_ref, dst_ref=land_ref.at[me], send_sem=ssem.at[r - 1], recv_sem=rsem.at[r - 1],
                device_id=(tx, ty, tc), device_id_type=MESH)
            cp.start()
            copies.append(cp)
        for cp in copies:
            cp.wait()
        acc = land_ref[0]
        for k in range(1, 8):
            acc = acc + land_ref[k]
        out_ref[...] = acc

    return pl.pallas_call(
        body, name="allreduce_small", out_shape=jax.ShapeDtypeStruct((rows, n), F32),
        in_specs=[pl.BlockSpec(memory_space=pltpu.VMEM)], out_specs=pl.BlockSpec(memory_space=pltpu.VMEM),
        scratch_shapes=[pltpu.VMEM((8, rows, n), F32), pltpu.SemaphoreType.DMA((7,)),
                        pltpu.SemaphoreType.DMA((7,))],
    )(packed)


def _adamw_math(w, g, m, v):
    m = ADAM_B1 * m + (1.0 - ADAM_B1) * g
    v = ADAM_B2 * v + (1.0 - ADAM_B2) * (g * g)
    m_hat = m / (1.0 - ADAM_B1 ** ADAM_STEP)
    v_hat = v / (1.0 - ADAM_B2 ** ADAM_STEP)
    return -ADAM_LR * (m_hat / (jnp.sqrt(v_hat) + ADAM_EPS) + ADAM_WD * w), m, v


def _adamw_halves(c_arr, w, mine, other, m, v, name):
    rows, n = mine.shape
    tr = 256 if rows % 256 == 0 else rows
    nb = rows // tr

    def body(c_ref, w_ref, a_ref, b_ref, m_ref, v_ref, g_ref, d_ref, mo_ref, vo_ref):
        g = jnp.where(pl.program_id(0) == c_ref[0], a_ref[...], b_ref[...])
        g_ref[...] = g
        d_ref[...], mo_ref[...], vo_ref[...] = _adamw_math(w_ref[...], g, m_ref[...], v_ref[...])

    full = pl.BlockSpec((None, tr, n), lambda h, i, c: (0, h * nb + i, 0))
    half = pl.BlockSpec((tr, n), lambda h, i, c: (i, 0))
    return pl.pallas_call(
        body, name=name,
        grid_spec=pltpu.PrefetchScalarGridSpec(num_scalar_prefetch=1, grid=(2, nb),
                                               in_specs=[full, half, half, full, full], out_specs=[full] * 4),
        out_shape=[jax.ShapeDtypeStruct((1, 2 * rows, n), F32)] * 4,
        compiler_params=_cp(("parallel", "parallel")),
    )(c_arr, w, mine, other, m, v)


def _adamw(w, g, m, v, name):
    rows, n = w.shape
    tr = 256 if rows % 256 == 0 else rows

    def body(w_ref, g_ref, m_ref, v_ref, d_ref, mo_ref, vo_ref):
        d_ref[...], mo_ref[...], vo_ref[...] = _adamw_math(w_ref[...], g_ref[...], m_ref[...], v_ref[...])

    spec = pl.BlockSpec((tr, n), lambda i: (i, 0))
    return pl.pallas_call(
        body, name=name, grid=(rows // tr,), in_specs=[spec] * 4, out_specs=[spec] * 3,
        out_shape=[jax.ShapeDtypeStruct((rows, n), F32)] * 3,
        compiler_params=_cp(("parallel",)),
    )(w, g, m, v)


def _rows8(a):
    a = a.reshape(-1, 128)
    return jnp.pad(a, ((0, (-a.shape[0]) % 8), (0, 0)))


def kernel(x, g_pre_mix, w_in, conv_w, conv_b, conv_ln_g, conv_ln_b, attn_norm_g, w_out, g_post_mix, g_pre_ffn, w_gate, w_up, w_down, g_post_ffn, loss_target, m_g_pre_mix, m_w_in, m_conv_w, m_conv_b, m_conv_ln_g, m_conv_ln_b, m_attn_norm_g, m_w_out, m_g_post_mix, m_g_pre_ffn, m_w_gate, m_w_up, m_w_down, m_g_post_ffn, v_g_pre_mix, v_w_in, v_conv_w, v_conv_b, v_conv_ln_g, v_conv_ln_b, v_attn_norm_g, v_w_out, v_g_post_mix, v_g_pre_ffn, v_w_gate, v_w_up, v_w_down, v_g_post_ffn):
    S = x.shape[1]
    tm_big = min(512, S)
    tm_ffn = min(256, S)
    tk_att = min(1024, S)
    t_att_fwd = min(1024, S)
    t_att_bwd = tk_att // KEY_BLOCKS
    chip = 2 * lax.axis_index("x") + lax.axis_index("y")
    core = lax.axis_index("c")
    x2 = x.reshape(S, D_MODEL)
    tgt = loss_target.reshape(S, D_MODEL)
    ag = attn_norm_g.reshape(1, CONV_CH)

    a_sh = w_in[0].astype(BF16)
    b_sh = jnp.stack([w_gate[0], w_up[0]]).astype(BF16)
    c_sh = jnp.concatenate([w_out[0], w_down[0]], axis=0).astype(BF16)
    cw_sh = jnp.pad(conv_w[0, :, 0, :], ((0, 1), (0, 0)))
    own = lambda full, shard: lax.dynamic_update_index_in_dim(full, shard, chip, 0)
    cols = lambda w4: jnp.transpose(w4, (1, 0, 2)).reshape(w4.shape[1], N_CHIPS * w4.shape[2])
    wa4, cw4 = _gather_weights([a_sh, cw_sh], [False, False])
    wa = own(wa4, a_sh)
    cwf = cols(own(cw4, cw_sh))

    a_bf, uc, qkv = _in_proj(x2, g_pre_mix, wa, tm_big)
    conv_out, yconv = _conv_fwd(uc, cwf, conv_b, conv_ln_g, conv_ln_b, tm_big)
    o, wb4, wc4 = _attn_fwd(qkv, t_att_fwd, tk_att, [b_sh, c_sh], [True, False])
    wb4, wc4 = own(wb4, b_sh), own(wc4, c_sh)
    wg, wu = cols(wb4[:, 0]), cols(wb4[:, 1])
    wo = wc4[:, :OUT_SH].reshape(D_MODEL, D_MODEL)
    wd = wc4[:, OUT_SH:].reshape(D_FF, D_MODEL)
    mixed, yv, h1, f_in = _out_proj(conv_out, o, ag, wo, x2, g_post_mix, g_pre_ffn, tm_big)
    df, dh2, dg4, loss_part, gt_bf, up_bf, act = _ffn_fwd(f_in, h1, tgt, wg, wu, wd, g_post_ffn, tm_ffn)

    dgt, dup, dh1, dy, dg3, dg2 = _ffn_bwd(gt_bf, up_bf, df, dh2, h1, yv, wg, wu, wd, g_pre_ffn, g_post_mix, tm_ffn)
    dco, do, dag = _out_bwd(dy, o, ag, wo, tm_big)
    ts = min(512, S)
    gw_out = _matmul_tn(mixed, dy, D_MODEL, ts, "grad_w_out")
    gw_gate = _matmul_tn(f_in, dgt, D_MODEL, ts, "grad_w_gate")
    gw_up = _matmul_tn(f_in, dup, D_MODEL, ts, "grad_w_up")
    gw_down = _matmul_tn(act, df, D_FF // 2, ts, "grad_w_down")

    by_cols = lambda g: jnp.transpose(g.reshape(2, D_MODEL // 2, N_CHIPS, -1), (2, 0, 1, 3))
    by_rows = lambda g: g.reshape(N_CHIPS, 2, g.shape[0] // (2 * N_CHIPS), g.shape[1])
    c_arr = core.reshape(1).astype(jnp.int32)

    def chip_partials(views, nms):
        landed = _sibling_halves(views, "grad_sibling_halves_" + nms[0])
        return [_add_half(c_arr, g, l, "grad_half_" + nm) for g, l, nm in zip(views, landed, nms)]

    early = ["w_gate", "w_up", "w_out", "w_down"]
    parts = chip_partials([by_cols(gw_gate), by_cols(gw_up), by_rows(gw_out), by_rows(gw_down)], early)
    dq, dk, dv, *slots = _attn_bwd(qkv, do, t_att_bwd, tk_att, parts)
    duc, dcw, dcb, dlg, dlb = _conv_bwd(uc, yconv, dco, cwf, conv_ln_g, conv_ln_b, tm_big)
    grad_x, du, dg1 = _in_bwd(duc, dq, dk, dv, x2, dh1, g_pre_mix, wa, tm_big)
    gw_in = _matmul_tn(a_bf, du, D_MODEL, ts, "grad_w_in", column_block=IN_SH)
    slots += _chip_scatter(chip_partials([gw_in.reshape(N_CHIPS, 2, D_MODEL // 2, IN_SH)], ["w_in"]))
    names = early + ["w_in"]
    halves = [_sum_chips(s, "grad_sum_" + nm) for s, nm in zip(slots, names)]
    others = _share_halves(halves)
    mine = dict(zip(names, halves))
    other = dict(zip(names, others))

    small = [dg1, dcb, dlg, dlb, dag, dg2, dg3, dg4]
    packed = jnp.concatenate([_rows8(s) for s in small] + [_rows8(dcw), _rows8(loss_part)], axis=0)
    red = _allreduce_small(packed)
    sizes = [D_MODEL, CONV_CH, CONV_CH, CONV_CH, CONV_CH, D_MODEL, D_MODEL, D_MODEL]
    g_small = [red[8 * k:8 * k + n // 128].reshape(1, n) for k, n in enumerate(sizes)]
    cw_red = red[64:64 + 128].reshape(HALO, CONV_CH)
    g_cw = lax.dynamic_slice(cw_red, (0, chip * 128), (HALO, 128))
    loss = red[192, 0]

    big = []
    for w, m, v, nm in [(w_in, m_w_in, v_w_in, "w_in"), (w_out, m_w_out, v_w_out, "w_out"),
                        (w_gate, m_w_gate, v_w_gate, "w_gate"), (w_up, m_w_up, v_w_up, "w_up"),
                        (w_down, m_w_down, v_w_down, "w_down")]:
        big.append(_adamw_halves(c_arr, w, mine[nm], other[nm], m, v, "adamw_" + nm))
    sm_w = [g_pre_mix, conv_b, conv_ln_g, conv_ln_b, ag, g_post_mix, g_pre_ffn, g_post_ffn]
    sm_m = [m_g_pre_mix, m_conv_b, m_conv_ln_g, m_conv_ln_b, m_attn_norm_g, m_g_post_mix, m_g_pre_ffn, m_g_post_ffn]
    sm_v = [v_g_pre_mix, v_conv_b, v_conv_ln_g, v_conv_ln_b, v_attn_norm_g, v_g_post_mix, v_g_pre_ffn, v_g_post_ffn]
    pad_cw = lambda a: jnp.pad(a[0, :, 0, :], ((0, 1), (0, 0)))

    def pack(vecs, cw):
        return jnp.concatenate([_rows8(a) for a in vecs] + [cw], axis=0)

    sd, smn, svn = _adamw(pack(sm_w, pad_cw(conv_w)), pack(g_small, g_cw), pack(sm_m, pad_cw(m_conv_w)),
                          pack(sm_v, pad_cw(v_conv_w)), "adamw_small")

    def unpack(p):
        vecs = [p[8 * k:8 * k + n // 128].reshape(1, n) for k, n in enumerate(sizes)]
        return vecs, p[64:64 + CONV_WIDTH].reshape(1, CONV_WIDTH, 1, 128)

    def ordered(vecs, cw, w_in_, w_out_, w_gate_, w_up_, w_down_):
        g1_, cb_, lg_, lb_, ag_, g2_, g3_, g4_ = vecs
        return [g1_, w_in_, cw, cb_, lg_, lb_, ag_.reshape(1, 8, HEAD_DIM), w_out_, g2_, g3_,
                w_gate_, w_up_, w_down_, g4_]

    grads = ordered(g_small, g_cw[:CONV_WIDTH].reshape(1, CONV_WIDTH, 1, 128), *[b[0] for b in big])
    outs = []
    for idx, p in enumerate((sd, smn, svn)):
        vecs, cw = unpack(p)
        outs += ordered(vecs, cw, *[b[idx + 1] for b in big])
    return (loss, grad_x, *grads, *outs)
```

```python
import functools
import math

import jax
import jax.numpy as jnp
from jax import lax
from jax.experimental import pallas as pl
from jax.experimental.pallas import tpu as pltpu

F32 = jnp.float32
BF16 = jnp.bfloat16
MESH = pl.DeviceIdType.MESH

D_MODEL = 1024
CONV_CH = 512
CONV_WIDTH = 31
HEAD_DIM = 64
PAIR = 2 * HEAD_DIM
N_PAIRS = 4
D_FF = 2816
N_CHIPS = 4
IN_SH = 2560 // N_CHIPS
FF_SH = D_FF // N_CHIPS
OUT_SH = D_MODEL // N_CHIPS
C_ROWS = OUT_SH + FF_SH
EPS = 1e-6
HALO = 32

ADAM_LR = 0.001
ADAM_B1 = 0.9
ADAM_B2 = 0.999
ADAM_EPS = 1e-08
ADAM_WD = 0.01
ADAM_STEP = 10

VMEM_LIMIT = 56 * 2 ** 20
VMEM_LIMIT_ATTN_BWD = 62 * 2 ** 20


def _cp(sem=None, vmem=VMEM_LIMIT):
    return pltpu.CompilerParams(dimension_semantics=sem, vmem_limit_bytes=vmem)


def _hbm():
    return pl.BlockSpec(memory_space=pltpu.HBM)


def _const_spec(shape):
    nd = len(shape)
    return pl.BlockSpec(shape, lambda *_: (0,) * nd, pipeline_mode=pl.Buffered(1))


def _dot(a, b):
    return jnp.dot(a, b, preferred_element_type=F32)


def _dot_nt(a, b):
    return lax.dot_general(a, b, (((1,), (1,)), ((), ())), preferred_element_type=F32)


def _dot_tn(a, b):
    return lax.dot_general(a, b, (((0,), (0,)), ((), ())), preferred_element_type=F32)


def _split3(x):
    b0 = x.astype(BF16)
    r1 = x - b0.astype(F32)
    b1 = r1.astype(BF16)
    b2 = (r1 - b1.astype(F32)).astype(BF16)
    return b0, b1, b2


def _split2(x):
    hi = x.astype(BF16)
    lo = (x - hi.astype(F32)).astype(BF16)
    return hi, lo


def _sigmoid(x):
    return 1.0 / (1.0 + jnp.exp(-x))


def _head_mean(x, seg):
    b0, b1, b2 = _split3(x)
    return (_dot(b0, seg) + _dot(b1, seg) + _dot(b2, seg)) * (1.0 / HEAD_DIM)


def _seg_matrix(n):
    r = lax.broadcasted_iota(jnp.int32, (n, n), 0) // HEAD_DIM
    c = lax.broadcasted_iota(jnp.int32, (n, n), 1) // HEAD_DIM
    return (r == c).astype(BF16)


def _rms(x):
    return lax.rsqrt(jnp.mean(x * x, axis=-1, keepdims=True) + EPS)


def _rms_bwd(dy, n, r, g):
    dn = dy * g
    dx = r * (dn - n * jnp.mean(dn * n, axis=-1, keepdims=True))
    return dx, dy * n


class _GatherPlan:
    def __init__(self, srcs, outs, lead, ssem, rsem):
        self.srcs, self.outs, self.lead, self.ssem, self.rsem = srcs, outs, lead, ssem, rsem
        x, y, self.c = lax.axis_index("x"), lax.axis_index("y"), lax.axis_index("c")
        self.me = 2 * x + y
        self.sibling = (x, y, 1 - self.c)
        self.chips = [(1 - x, y), (x, 1 - y), (1 - x, 1 - y)]

    def _half(self, ref, i, h):
        if self.lead[i]:
            return ref.at[h]
        rows = ref.shape[0] // 2
        return ref.at[pl.ds(h * rows, rows)]

    def _ici(self, i, k, origin):
        return pltpu.make_async_remote_copy(
            src_ref=self._half(self.srcs[i], i, self.c), dst_ref=self._half(self.outs[i].at[origin], i, self.c),
            send_sem=self.ssem.at[6 * i + k], recv_sem=self.rsem.at[6 * i + k],
            device_id=(self.chips[k][0], self.chips[k][1], self.c), device_id_type=MESH)

    def _d2d(self, i, k, h):
        origin = 2 * self.chips[k][0] + self.chips[k][1]
        piece = self._half(self.outs[i].at[origin], i, h)
        return pltpu.make_async_remote_copy(
            src_ref=piece, dst_ref=piece, send_sem=self.ssem.at[6 * i + 3 + k],
            recv_sem=self.rsem.at[6 * i + 3 + k], device_id=self.sibling, device_id_type=MESH)

    def _each(self):
        return [(i, k) for i in range(len(self.srcs)) for k in range(3)]

    def start(self):
        for i, k in self._each():
            self._ici(i, k, self.me).start()

    def forward(self):
        for i, k in self._each():
            self._ici(i, k, 2 * self.chips[k][0] + self.chips[k][1]).wait_recv()
            self._d2d(i, k, self.c).start()

    def finish(self):
        for i, k in self._each():
            self._d2d(i, k, 1 - self.c).wait_recv()
        for i, k in self._each():
            self._ici(i, k, self.me).wait_send()
            self._d2d(i, k, self.c).wait_send()


def _gather_shapes(shards):
    return [jax.ShapeDtypeStruct((N_CHIPS,) + s.shape, s.dtype) for s in shards]


def _gather_weights(shards, lead):
    n = len(shards)

    def body(*refs):
        plan = _GatherPlan(refs[:n], refs[n:2 * n], lead, refs[2 * n], refs[2 * n + 1])
        plan.start()
        plan.forward()
        plan.finish()

    return pl.pallas_call(
        body, name="gather_weights", out_shape=_gather_shapes(shards),
        in_specs=[_hbm()] * n, out_specs=[_hbm()] * n,
        scratch_shapes=[pltpu.SemaphoreType.DMA((6 * n,)), pltpu.SemaphoreType.DMA((6 * n,))],
    )(*shards)


def _in_proj(x2, g1, wa, tm):
    S = x2.shape[0]

    def body(x_ref, g_ref, w_ref, a_ref, uc_ref, qkv_ref):
        x = x_ref[...]
        a = (x * _rms(x) * g_ref[...]).astype(BF16)
        a_ref[...] = a
        u = [_dot(a, w_ref[j]) for j in range(N_CHIPS)]
        uc_ref[:, 0:640] = u[0]
        uc_ref[:, 640:1024] = u[1][:, 0:384]
        qkv_ref[:, 0:256] = u[1][:, 384:640].astype(BF16)
        qkv_ref[:, 256:896] = u[2].astype(BF16)
        qkv_ref[:, 896:1536] = u[3].astype(BF16)

    return pl.pallas_call(
        body, name="in_proj", grid=(S // tm,),
        in_specs=[pl.BlockSpec((tm, D_MODEL), lambda i: (i, 0)), _const_spec((1, D_MODEL)),
                  _const_spec(wa.shape)],
        out_specs=[pl.BlockSpec((tm, D_MODEL), lambda i: (i, 0)),
                   pl.BlockSpec((tm, 2 * CONV_CH), lambda i: (i, 0)),
                   pl.BlockSpec((tm, 1536), lambda i: (i, 0))],
        out_shape=[jax.ShapeDtypeStruct((S, D_MODEL), BF16), jax.ShapeDtypeStruct((S, 2 * CONV_CH), F32),
                   jax.ShapeDtypeStruct((S, 1536), BF16)],
        compiler_params=_cp(("parallel",)),
    )(x2, g1, wa)


SUBLANES = 8


def _shift_copies(src_ref, sh_ref):
    rows = sh_ref.shape[1]
    for b in range(1, SUBLANES):
        sh_ref[b - 1] = src_ref[pl.ds(b, rows), :]


def _rows_at(src_ref, sh_ref, off, rows):
    a, b = divmod(off, SUBLANES)
    if b == 0:
        return src_ref[pl.ds(SUBLANES * a, rows), :]
    return sh_ref[b - 1, pl.ds(SUBLANES * a, rows), :]


def _conv_taps(cw_ref, src_ref, sh_ref, offs, rows):
    acc = None
    for w, off in enumerate(offs):
        term = cw_ref[w:w + 1, :] * _rows_at(src_ref, sh_ref, off, rows)
        acc = term if acc is None else acc + term
    return acc


def _glu(uc):
    return uc[:, :CONV_CH] * _sigmoid(uc[:, CONV_CH:])


def _conv_fwd(uc, cwf, cb, lg, lb, tm):
    S = uc.shape[0]
    hb = tm // HALO

    def body(uc_ref, prev_ref, cw_ref, cb_ref, lg_ref, lb_ref, out_ref, y_ref, glu_ref, sh_ref):
        i = pl.program_id(0)
        glu_ref[0:HALO, :] = jnp.where(i == 0, 0.0, _glu(prev_ref[...]))
        glu_ref[HALO:HALO + tm, :] = _glu(uc_ref[...])
        glu_ref[HALO + tm:HALO + tm + SUBLANES, :] = jnp.zeros((SUBLANES, CONV_CH), F32)
        _shift_copies(glu_ref, sh_ref)
        offs = [HALO - (CONV_WIDTH - 1) + w for w in range(CONV_WIDTH)]
        y = _conv_taps(cw_ref, glu_ref, sh_ref, offs, tm) + cb_ref[...]
        y_ref[...] = y
        mu = jnp.mean(y, axis=-1, keepdims=True)
        yc = y - mu
        rstd = lax.rsqrt(jnp.mean(yc * yc, axis=-1, keepdims=True) + EPS)
        ln = yc * rstd * lg_ref[...] + lb_ref[...]
        out_ref[...] = (ln * _sigmoid(ln)).astype(BF16)

    return pl.pallas_call(
        body, name="conv_fwd", grid=(S // tm,),
        in_specs=[pl.BlockSpec((tm, 2 * CONV_CH), lambda i: (i, 0)),
                  pl.BlockSpec((HALO, 2 * CONV_CH), lambda i: (jnp.maximum(i * hb - 1, 0), 0)),
                  _const_spec(cwf.shape), _const_spec((1, CONV_CH)), _const_spec((1, CONV_CH)),
                  _const_spec((1, CONV_CH))],
        out_specs=[pl.BlockSpec((tm, CONV_CH), lambda i: (i, 0))] * 2,
        out_shape=[jax.ShapeDtypeStruct((S, CONV_CH), BF16), jax.ShapeDtypeStruct((S, CONV_CH), F32)],
        scratch_shapes=[pltpu.VMEM((HALO + tm + SUBLANES, CONV_CH), F32),
                        pltpu.VMEM((SUBLANES - 1, HALO + tm, CONV_CH), F32)],
        compiler_params=_cp(("parallel",)),
    )(uc, uc, cwf, cb, lg, lb)


def _lane_mask(h):
    lane = lax.broadcasted_iota(jnp.int32, (1, PAIR), 1)
    return (lane >= HEAD_DIM * h) & (lane < HEAD_DIM * (h + 1))


def _neg_abs(x):
    bits = lax.bitcast_convert_type(x, jnp.uint32) | jnp.uint32(0x80000000)
    return lax.bitcast_convert_type(bits, F32)


def _tri_dot(x, m):
    return _dot(x.astype(BF16), m)


MASKED = -1e30
KEY_BLOCKS = 4


def _running_sums(x, m, reverse, start=None):
    t = m.shape[0]
    blocks = x.shape[1] // t
    order = range(blocks - 1, -1, -1) if reverse else range(blocks)
    out = [None] * blocks
    carry = start
    for b in order:
        xb = x[:, b * t:(b + 1) * t]
        cb = _tri_dot(xb, m)
        out[b] = cb if carry is None else cb + carry
        rs = jnp.sum(xb, axis=1, keepdims=True)
        carry = rs if carry is None else carry + rs
    return jnp.concatenate(out, axis=1), carry


def _sb_tile(z, r, m_suf):
    sp = jnp.maximum(z, 0.0) + jnp.log(1.0 + jnp.exp(_neg_abs(z)))
    c, rs = _running_sums(sp, m_suf, reverse=True, start=r)
    return jnp.exp(z - c), sp, rs


def _scores(qm, kt, mask):
    z = _dot_nt(qm, kt)
    return z if mask is None else jnp.where(mask, z, MASKED)


def _causal_mask(i, sb, t, tk, w=None):
    w = tk if w is None else w
    row = lax.broadcasted_iota(jnp.int32, (t, w), 0) + i * t
    col = lax.broadcasted_iota(jnp.int32, (t, w), 1) + sb * tk
    return col < row


def _sweep_plain(first, count, tile):
    def step(n, carry):
        tile(first + n)
        return carry

    lax.fori_loop(0, count + 1, step, 0)


def _sweep(first, count, down, fetch, load, work):
    lo, hi = (first - count, first) if down else (first, first + count)
    tile = lambda j: jnp.clip(first - j if down else first + j, lo, hi)
    fetch(first, 0, True)

    def step(n, carry):
        j = 2 * n
        vals = load(0)
        fetch(tile(j + 1), 1, False)
        work(tile(j), vals)
        vals = load(1)
        fetch(tile(j + 2), 0, False)
        work(tile(j + 1), vals)
        return carry

    lax.fori_loop(0, (count + 1) // 2, step, 0)

    @pl.when(lax.rem(count, 2) == 0)
    def _():
        work(tile(count), load(0))


def _suffix_matrix(t, prefix=False):
    row = lax.broadcasted_iota(jnp.int32, (t, t), 0)
    col = lax.broadcasted_iota(jnp.int32, (t, t), 1)
    return ((row <= col) if prefix else (row >= col)).astype(BF16)


def _attn_fwd(qkv, t, tk, shards, lead):
    S = qkv.shape[0]
    assert t == tk, "the forward cuts its diagonal tile by blocks: query block and key tile must match"

    ng = len(shards)
    nq = S // t

    def body(*refs):
        q_ref, k_ref, v_ref = refs[:3]
        o_ref = refs[3 + ng]
        acc_ref, r_ref, z_buf, ssem, rsem = refs[4 + 2 * ng:]
        p = pl.program_id(0)
        i = pl.program_id(1)
        plan = _GatherPlan(refs[3:3 + ng], refs[4 + ng:4 + 2 * ng], lead, ssem, rsem)
        pl.when((p == 0) & (i == 0))(plan.start)
        pl.when((p == 1) & (i == 0))(plan.forward)
        blk = tk // KEY_BLOCKS
        m_suf = _suffix_matrix(blk)
        q = q_ref[...]
        hms = [_lane_mask(h) for h in range(2)]
        qms = [jnp.where(hm, q, 0) * 0.125 for hm in hms]

        def rows(sb, w=tk):
            return pl.ds(pl.multiple_of(sb * tk, tk), w)

        for rb in range(KEY_BLOCKS):
            w = (rb + 1) * blk
            part = slice(rb * blk, w)
            kt = k_ref[rows(i, w), :]
            vt = v_ref[rows(i, w), :]
            row = lax.broadcasted_iota(jnp.int32, (blk, w), 0) + rb * blk
            mask = lax.broadcasted_iota(jnp.int32, (blk, w), 1) < row
            out = jnp.zeros((blk, PAIR), F32)
            for h in range(2):
                a_loc, _, rs = _sb_tile(_scores(qms[h][part], kt, mask), None, m_suf)
                out = out + _dot(a_loc.astype(BF16), jnp.where(hms[h], vt, 0))
                r_ref[h, part] = rs
            acc_ref[part, :] = out

        def fetch(sb, slot, first):
            kt = k_ref[rows(sb), :]
            for h in range(2):
                z_buf[slot, h] = _scores(qms[h], kt, None)

        def load(slot):
            return [z_buf[slot, h] for h in range(2)]

        def work(sb, zs):
            vt = v_ref[rows(sb), :]
            for h in range(2):
                a_loc, _, rs = _sb_tile(zs[h], None, m_suf)
                r = r_ref[h]
                acc_ref[...] += _dot(a_loc.astype(BF16), jnp.where(hms[h], vt, 0)) * jnp.exp(-r)
                r_ref[h] = r + rs

        pl.when(i >= 1)(lambda: _sweep(i - 1, i - 1, True, fetch, load, work))
        o_ref[...] = acc_ref[...]
        pl.when((p == N_PAIRS - 1) & (i == nq - 1))(plan.finish)

    return pl.pallas_call(
        body, name="attn_fwd", grid=(N_PAIRS, nq),
        in_specs=[pl.BlockSpec((t, PAIR), lambda p, i: (i, p)),
                  pl.BlockSpec((S, PAIR), lambda p, i: (0, N_PAIRS + p)),
                  pl.BlockSpec((S, PAIR), lambda p, i: (0, 2 * N_PAIRS + p))] + [_hbm()] * ng,
        out_specs=[pl.BlockSpec((t, PAIR), lambda p, i: (i, p))] + [_hbm()] * ng,
        out_shape=[jax.ShapeDtypeStruct((S, N_PAIRS * PAIR), F32)] + _gather_shapes(shards),
        scratch_shapes=[pltpu.VMEM((t, PAIR), F32), pltpu.VMEM((2, t, 1), F32),
                        pltpu.VMEM((2, 2, t, tk), F32),
                        pltpu.SemaphoreType.DMA((6 * ng,)), pltpu.SemaphoreType.DMA((6 * ng,))],
        compiler_params=_cp(("arbitrary", "arbitrary")),
    )(qkv, qkv, qkv, *shards)


def _out_proj(conv_out, o, ag, wc, x2, g2, g3, tm):
    S = o.shape[0]

    def body(co_ref, o_ref, ag_ref, w_ref, x_ref, g2_ref, g3_ref, mix_ref, y_ref, h1_ref, fin_ref):
        seg = _seg_matrix(CONV_CH)
        o = o_ref[...]
        att = (o * lax.rsqrt(_head_mean(o * o, seg) + EPS) * ag_ref[...]).astype(BF16)
        co = co_ref[...]
        mix_ref[:, :CONV_CH] = co
        mix_ref[:, CONV_CH:] = att
        y = _dot(co, w_ref[0:CONV_CH, :]) + _dot(att, w_ref[CONV_CH:, :])
        y_ref[...] = y
        h1 = x_ref[...] + y * _rms(y) * g2_ref[...]
        h1_ref[...] = h1
        fin_ref[...] = (h1 * _rms(h1) * g3_ref[...]).astype(BF16)

    row = lambda w: pl.BlockSpec((tm, w), lambda i: (i, 0))
    return pl.pallas_call(
        body, name="out_proj", grid=(S // tm,),
        in_specs=[row(CONV_CH), row(CONV_CH), _const_spec((1, CONV_CH)), _const_spec(wc.shape),
                  row(D_MODEL), _const_spec((1, D_MODEL)), _const_spec((1, D_MODEL))],
        out_specs=[row(D_MODEL)] * 4,
        out_shape=[jax.ShapeDtypeStruct((S, D_MODEL), BF16), jax.ShapeDtypeStruct((S, D_MODEL), F32),
                   jax.ShapeDtypeStruct((S, D_MODEL), F32), jax.ShapeDtypeStruct((S, D_MODEL), BF16)],
        compiler_params=_cp(("parallel",)),
    )(conv_out, o, ag, wc, x2, g2, g3)


def _ffn_fwd(f_in, h1, tgt, wg, wu, wd, g4, tm):
    S = f_in.shape[0]

    def body(fin_ref, h1_ref, tgt_ref, wg_ref, wu_ref, wd_ref, g4_ref, df_ref, dh2_ref, dg4_ref, loss_ref,
             gt_ref, up_ref, act_ref):
        i = pl.program_id(0)
        fin = fin_ref[...]
        gt = _dot(fin, wg_ref[...])
        up = _dot(fin, wu_ref[...])
        act = (gt * _sigmoid(gt) * up).astype(BF16)
        gt_ref[...] = gt.astype(BF16)
        up_ref[...] = up.astype(BF16)
        act_ref[...] = act
        f = _dot(act, wd_ref[...])
        r = _rms(f)
        n = f * r
        g4 = g4_ref[...]
        err = h1_ref[...] + n * g4 - tgt_ref[...]
        dh2 = err * (1.0 / D_MODEL)
        dh2_ref[...] = dh2
        df, dg = _rms_bwd(dh2, n, r, g4)
        df_ref[...] = df.astype(BF16)

        @pl.when(i == 0)
        def _():
            dg4_ref[...] = jnp.zeros_like(dg4_ref)
            loss_ref[...] = jnp.zeros_like(loss_ref)

        dg4_ref[...] += jnp.sum(dg, axis=0, keepdims=True)
        part = jnp.sum(jnp.sum(err * err, axis=1, keepdims=True), axis=0, keepdims=True)
        loss_ref[...] += part * (0.5 / D_MODEL)

    row = lambda w: pl.BlockSpec((tm, w), lambda i: (i, 0))
    return pl.pallas_call(
        body, name="ffn_fwd", grid=(S // tm,),
        in_specs=[row(D_MODEL), row(D_MODEL), row(D_MODEL), _const_spec(wg.shape), _const_spec(wu.shape),
                  _const_spec(wd.shape), _const_spec((1, D_MODEL))],
        out_specs=[row(D_MODEL), row(D_MODEL), pl.BlockSpec((1, D_MODEL), lambda i: (0, 0)),
                   pl.BlockSpec((1, 128), lambda i: (0, 0)), row(D_FF), row(D_FF), row(D_FF)],
        out_shape=[jax.ShapeDtypeStruct((S, D_MODEL), BF16), jax.ShapeDtypeStruct((S, D_MODEL), F32),
                   jax.ShapeDtypeStruct((1, D_MODEL), F32), jax.ShapeDtypeStruct((1, 128), F32)]
        + [jax.ShapeDtypeStruct((S, D_FF), BF16)] * 3,
        compiler_params=_cp(("arbitrary",)),
    )(f_in, h1, tgt, wg, wu, wd, g4)


def _ffn_bwd(gt_bf, up_bf, df, dh2, h1, yv, wg, wu, wd, g3, g2, tm):
    S = df.shape[0]

    def body(gt_ref, up_ref, df_ref, dh2_ref, h1_ref, y_ref, wg_ref, wu_ref, wd_ref, g3_ref, g2_ref,
             dgt_ref, dup_ref, dh1_ref, dy_ref, dg3_ref, dg2_ref):
        i = pl.program_id(0)
        df = df_ref[...]
        gt = gt_ref[...].astype(F32)
        up = up_ref[...].astype(F32)
        sg = _sigmoid(gt)
        silu = gt * sg
        dact = _dot_nt(df, wd_ref[...])
        dgt = (dact * up * (sg * (1.0 + gt * (1.0 - sg)))).astype(BF16)
        dup = (dact * silu).astype(BF16)
        dgt_ref[...] = dgt
        dup_ref[...] = dup
        dfin = _dot_nt(dgt, wg_ref[...]) + _dot_nt(dup, wu_ref[...])
        h1 = h1_ref[...]
        r3 = _rms(h1)
        dh1_n, dg3 = _rms_bwd(dfin, h1 * r3, r3, g3_ref[...])
        dh1 = dh2_ref[...] + dh1_n
        dh1_ref[...] = dh1
        y = y_ref[...]
        r2 = _rms(y)
        dy, dg2 = _rms_bwd(dh1, y * r2, r2, g2_ref[...])
        dy_ref[...] = dy.astype(BF16)

        @pl.when(i == 0)
        def _():
            dg3_ref[...] = jnp.zeros_like(dg3_ref)
            dg2_ref[...] = jnp.zeros_like(dg2_ref)

        dg3_ref[...] += jnp.sum(dg3, axis=0, keepdims=True)
        dg2_ref[...] += jnp.sum(dg2, axis=0, keepdims=True)

    row = lambda w: pl.BlockSpec((tm, w), lambda i: (i, 0))
    vec = pl.BlockSpec((1, D_MODEL), lambda i: (0, 0))
    return pl.pallas_call(
        body, name="ffn_bwd", grid=(S // tm,),
        in_specs=[row(D_FF), row(D_FF)] + [row(D_MODEL)] * 4
        + [_const_spec(wg.shape), _const_spec(wu.shape), _const_spec(wd.shape),
           _const_spec((1, D_MODEL)), _const_spec((1, D_MODEL))],
        out_specs=[row(D_FF), row(D_FF), row(D_MODEL), row(D_MODEL), vec, vec],
        out_shape=[jax.ShapeDtypeStruct((S, D_FF), BF16)] * 2
        + [jax.ShapeDtypeStruct((S, D_MODEL), F32), jax.ShapeDtypeStruct((S, D_MODEL), BF16),
           jax.ShapeDtypeStruct((1, D_MODEL), F32), jax.ShapeDtypeStruct((1, D_MODEL), F32)],
        compiler_params=_cp(("arbitrary",)),
    )(gt_bf, up_bf, df, dh2, h1, yv, wg, wu, wd, g3, g2)


def _out_bwd(dy, o, ag, wc, tm):
    S = o.shape[0]

    def body(dy_ref, o_ref, ag_ref, w_ref, dco_ref, do_ref, dag_ref):
        i = pl.program_id(0)
        seg = _seg_matrix(CONV_CH)
        dy = dy_ref[...]
        dco_ref[...] = _dot_nt(dy, w_ref[0:CONV_CH, :])
        datt = _dot_nt(dy, w_ref[CONV_CH:, :])
        o = o_ref[...]
        r = lax.rsqrt(_head_mean(o * o, seg) + EPS)
        n = o * r
        dn = datt * ag_ref[...]
        do_ref[...] = (r * (dn - n * _head_mean(dn * n, seg))).astype(BF16)

        @pl.when(i == 0)
        def _():
            dag_ref[...] = jnp.zeros_like(dag_ref)

        dag_ref[...] += jnp.sum(datt * n, axis=0, keepdims=True)

    row = lambda w: pl.BlockSpec((tm, w), lambda i: (i, 0))
    return pl.pallas_call(
        body, name="out_bwd", grid=(S // tm,),
        in_specs=[row(D_MODEL), row(CONV_CH), _const_spec((1, CONV_CH)), _const_spec(wc.shape)],
        out_specs=[row(CONV_CH), row(CONV_CH), pl.BlockSpec((1, CONV_CH), lambda i: (0, 0))],
        out_shape=[jax.ShapeDtypeStruct((S, CONV_CH), F32), jax.ShapeDtypeStruct((S, CONV_CH), BF16),
                   jax.ShapeDtypeStruct((1, CONV_CH), F32)],
        compiler_params=_cp(("arbitrary",)),
    )(dy, o, ag, wc)


def _attn_bwd(qkv, do, t, tk, parts):
    S = qkv.shape[0]
    nk = S // tk
    ns = len(parts)

    def body(*refs):
        q_ref, k_ref, v_ref, do_ref = refs[:4]
        dq_ref, dk_hbm, dv_hbm = refs[4 + ns:7 + ns]
        g_buf, s_buf, r_ref, dq_acc, dk_ref, dv_ref, z_buf, da_buf = refs[7 + 2 * ns:15 + 2 * ns]
        p = pl.program_id(0)
        i = pl.program_id(1)
        plan = _ScatterPlan(refs[4:4 + ns], refs[7 + ns:7 + 2 * ns], *refs[15 + 2 * ns:])
        pl.when((p == 0) & (i == 0))(plan.start)
        last = (i * t + t - 1) // tk

        @pl.when(i == 0)
        def _():
            dk_ref[...] = jnp.zeros_like(dk_ref)
            dv_ref[...] = jnp.zeros_like(dv_ref)

        m_suf = _suffix_matrix(tk // KEY_BLOCKS)
        m_pre = _suffix_matrix(tk // KEY_BLOCKS, prefix=True)
        q = q_ref[...]
        do = do_ref[...]
        hms = [_lane_mask(h) for h in range(2)]
        qms = [jnp.where(hm, q, 0) * 0.125 for hm in hms]
        doms = [jnp.where(hm, do, 0) for hm in hms]
        dq_acc[...] = jnp.zeros_like(dq_acc)
        r_ref[...] = jnp.zeros_like(r_ref)

        def keys(sb, w=tk):
            return pl.ds(pl.multiple_of(sb * tk, tk), w)

        def matmuls1(sb, w, diagonal):
            kt = k_ref[keys(sb, w), :]
            vt = v_ref[keys(sb, w), :]
            mask = _causal_mask(i, sb, t, tk, w) if diagonal else None
            return [(_scores(qms[h], kt, mask), _dot_nt(doms[h], vt)) for h in range(2)]

        def sweep1(sb, w, vals):
            dv = jnp.zeros((w, PAIR), F32)
            for h in range(2):
                z, da = vals[h]
                A, sp, r_ref[h] = _sb_tile(z, r_ref[h], m_suf)
                g_buf[h, sb, :, 0:w] = A * da
                s_buf[h, sb, :, 0:w] = 1.0 - jnp.exp(-sp)
                dv = dv + _dot_tn(A.astype(BF16), doms[h])
            dv_ref[keys(sb, w), :] += dv

        def sweep2(sb, w):
            kt = k_ref[keys(sb, w), :]
            dk = jnp.zeros((w, PAIR), F32)
            for h in range(2):
                g = g_buf[h, sb, :, 0:w]
                pre, r_ref[h] = _running_sums(g, m_pre, reverse=False, start=r_ref[h])
                dzb = (g - s_buf[h, sb, :, 0:w] * pre).astype(BF16)
                dq_acc[...] += _dot(dzb, jnp.where(hms[h], kt, 0))
                dk = dk + _dot_tn(dzb, qms[h])
            dk_ref[keys(sb, w), :] += dk

        def diagonal_tile(tile):
            for nb in range(1, KEY_BLOCKS + 1):
                pl.when(lax.rem(i, KEY_BLOCKS) == nb - 1)(functools.partial(tile, nb * t))

        def fetch1(sb, slot, first):
            for h, (z, da) in enumerate(matmuls1(sb, tk, False)):
                z_buf[slot, h] = z
                da_buf[slot, h] = da

        def load1(slot):
            return [(z_buf[slot, h], da_buf[slot, h]) for h in range(2)]

        diagonal_tile(lambda w: sweep1(last, w, matmuls1(last, w, True)))
        pl.when(last >= 1)(lambda: _sweep(last - 1, last - 1, True, fetch1, load1,
                                          lambda sb, vals: sweep1(sb, tk, vals)))
        r_ref[...] = jnp.zeros_like(r_ref)
        pl.when(last >= 1)(lambda: _sweep_plain(0, last - 1, lambda sb: sweep2(sb, tk)))
        diagonal_tile(lambda w: sweep2(last, w))
        dq_ref[...] = dq_acc[...] * 0.125

        @pl.when(i == S // t - 1)
        def _():
            cols = pl.ds(pl.multiple_of(p * PAIR, PAIR), PAIR)
            pltpu.sync_copy(dk_ref, dk_hbm.at[:, cols])
            pltpu.sync_copy(dv_ref, dv_hbm.at[:, cols])

        pl.when((p == N_PAIRS - 1) & (i == S // t - 1))(plan.finish)

    once = lambda cb: pl.BlockSpec((S, PAIR), cb, pipeline_mode=pl.Buffered(1))
    return pl.pallas_call(
        body, name="attn_bwd", grid=(N_PAIRS, S // t),
        in_specs=[pl.BlockSpec((t, PAIR), lambda p, i: (i, p)),
                  once(lambda p, i: (0, N_PAIRS + p)), once(lambda p, i: (0, 2 * N_PAIRS + p)),
                  pl.BlockSpec((t, PAIR), lambda p, i: (i, p))] + [_hbm()] * ns,
        out_specs=[pl.BlockSpec((t, PAIR), lambda p, i: (i, p)), _hbm(), _hbm()] + [_hbm()] * ns,
        out_shape=[jax.ShapeDtypeStruct((S, N_PAIRS * PAIR), F32)] * 3
        + [jax.ShapeDtypeStruct(pt.shape, pt.dtype) for pt in parts],
        scratch_shapes=[pltpu.VMEM((2, nk, t, tk), F32), pltpu.VMEM((2, nk, t, tk), F32),
                        pltpu.VMEM((2, t, 1), F32), pltpu.VMEM((t, PAIR), F32),
                        pltpu.VMEM((S, PAIR), F32), pltpu.VMEM((S, PAIR), F32),
                        pltpu.VMEM((2, 2, t, tk), F32), pltpu.VMEM((2, 2, t, tk), F32)] + _scatter_sems(ns),
        compiler_params=_cp(("arbitrary", "arbitrary"), vmem=VMEM_LIMIT_ATTN_BWD),
    )(qkv, qkv, qkv, do, *parts)


def _conv_bwd(uc, yconv, dco, cwf, lg, lb, tm):
    S = uc.shape[0]
    hb = tm // HALO
    nb = S // tm
    ext = tm + HALO

    def body(uc_ref, prev_ref, y_ref, ynext_ref, dco_ref, dnext_ref, cw_ref, lg_ref, lb_ref,
             duc_ref, dcw_ref, dcb_ref, dlg_ref, dlb_ref, glu_ref, dyc_ref, shg_ref, shd_ref):
        i = pl.program_id(0)
        last = i == nb - 1

        @pl.when(i == 0)
        def _():
            for ref in (dcw_ref, dcb_ref, dlg_ref, dlb_ref):
                ref[...] = jnp.zeros_like(ref)

        uc = uc_ref[...]
        glu_ref[0:HALO, :] = jnp.where(i == 0, 0.0, _glu(prev_ref[...]))
        glu_ref[HALO:ext, :] = _glu(uc)
        glu_ref[ext:ext + SUBLANES, :] = jnp.zeros((SUBLANES, CONV_CH), F32)
        _shift_copies(glu_ref, shg_ref)
        fwd_offs = [HALO - (CONV_WIDTH - 1) + w for w in range(CONV_WIDTH)]
        y = jnp.concatenate([y_ref[...], ynext_ref[...]], axis=0)
        mu = jnp.mean(y, axis=-1, keepdims=True)
        yc = y - mu
        rstd = lax.rsqrt(jnp.mean(yc * yc, axis=-1, keepdims=True) + EPS)
        yhat = yc * rstd
        lg = lg_ref[...]
        ln = yhat * lg + lb_ref[...]
        sg = _sigmoid(ln)
        dout = jnp.concatenate([dco_ref[...], jnp.where(last, 0.0, dnext_ref[...])], axis=0)
        dln = dout * (sg * (1.0 + ln * (1.0 - sg)))
        dyh = dln * lg
        dyc = rstd * (dyh - jnp.mean(dyh, axis=-1, keepdims=True)
                      - yhat * jnp.mean(dyh * yhat, axis=-1, keepdims=True))
        dyc_ref[0:ext, :] = dyc
        dyc_ref[ext:ext + SUBLANES, :] = jnp.zeros((SUBLANES, CONV_CH), F32)
        _shift_copies(dyc_ref, shd_ref)
        dlg_ref[...] += jnp.sum((dln * yhat)[0:tm], axis=0, keepdims=True)
        dlb_ref[...] += jnp.sum(dln[0:tm], axis=0, keepdims=True)
        dcb_ref[...] += jnp.sum(dyc[0:tm], axis=0, keepdims=True)
        dglu = _conv_taps(cw_ref, dyc_ref, shd_ref, [CONV_WIDTH - 1 - w for w in range(CONV_WIDTH)], tm)
        d0 = dyc[0:tm]
        for w, off in enumerate(fwd_offs):
            dcw_ref[w:w + 1, :] += jnp.sum(d0 * _rows_at(glu_ref, shg_ref, off, tm), axis=0, keepdims=True)
        val, gate = uc[:, :CONV_CH], uc[:, CONV_CH:]
        sgate = _sigmoid(gate)
        duc_ref[:, :CONV_CH] = (dglu * sgate).astype(BF16)
        duc_ref[:, CONV_CH:] = (dglu * val * sgate * (1.0 - sgate)).astype(BF16)

    vec = pl.BlockSpec((1, CONV_CH), lambda i: (0, 0))
    nxt = lambda i: (jnp.minimum((i + 1) * hb, S // HALO - 1), 0)
    return pl.pallas_call(
        body, name="conv_bwd", grid=(nb,),
        in_specs=[pl.BlockSpec((tm, 2 * CONV_CH), lambda i: (i, 0)),
                  pl.BlockSpec((HALO, 2 * CONV_CH), lambda i: (jnp.maximum(i * hb - 1, 0), 0)),
                  pl.BlockSpec((tm, CONV_CH), lambda i: (i, 0)), pl.BlockSpec((HALO, CONV_CH), nxt),
                  pl.BlockSpec((tm, CONV_CH), lambda i: (i, 0)), pl.BlockSpec((HALO, CONV_CH), nxt),
                  _const_spec(cwf.shape), _const_spec((1, CONV_CH)), _const_spec((1, CONV_CH))],
        out_specs=[pl.BlockSpec((tm, 2 * CONV_CH), lambda i: (i, 0)),
                   pl.BlockSpec(cwf.shape, lambda i: (0, 0)), vec, vec, vec],
        out_shape=[jax.ShapeDtypeStruct((S, 2 * CONV_CH), BF16), jax.ShapeDtypeStruct(cwf.shape, F32)]
        + [jax.ShapeDtypeStruct((1, CONV_CH), F32)] * 3,
        scratch_shapes=[pltpu.VMEM((ext + SUBLANES, CONV_CH), F32), pltpu.VMEM((ext + SUBLANES, CONV_CH), F32),
                        pltpu.VMEM((SUBLANES - 1, ext, CONV_CH), F32),
                        pltpu.VMEM((SUBLANES - 1, ext, CONV_CH), F32)],
        compiler_params=_cp(("arbitrary",)),
    )(uc, uc, yconv, yconv, dco, dco, cwf, lg, lb)


def _in_bwd(duc, dq, dk, dv, x2, dh1, g1, wa, tm):
    S = x2.shape[0]

    def body(duc_ref, dq_ref, dk_ref, dv_ref, x_ref, dh1_ref, g_ref, w_ref, gx_ref, du_ref, dg_ref):
        i = pl.program_id(0)
        du = jnp.concatenate([duc_ref[...], dq_ref[...].astype(BF16), dk_ref[...].astype(BF16),
                              dv_ref[...].astype(BF16)], axis=1)
        du_ref[...] = du
        da = _dot_nt(du[:, 0:IN_SH], w_ref[0])
        for j in range(1, N_CHIPS):
            da = da + _dot_nt(du[:, IN_SH * j:IN_SH * (j + 1)], w_ref[j])
        x = x_ref[...]
        r = _rms(x)
        dx, dg = _rms_bwd(da, x * r, r, g_ref[...])
        gx_ref[...] = dh1_ref[...] + dx

        @pl.when(i == 0)
        def _():
            dg_ref[...] = jnp.zeros_like(dg_ref)

        dg_ref[...] += jnp.sum(dg, axis=0, keepdims=True)

    row = lambda w: pl.BlockSpec((tm, w), lambda i: (i, 0))
    return pl.pallas_call(
        body, name="in_bwd", grid=(S // tm,),
        in_specs=[row(2 * CONV_CH), row(CONV_CH), row(CONV_CH), row(CONV_CH), row(D_MODEL), row(D_MODEL),
                  _const_spec((1, D_MODEL)), _const_spec(wa.shape)],
        out_specs=[pl.BlockSpec((None, tm, D_MODEL), lambda i: (0, i, 0)), row(2560),
                   pl.BlockSpec((1, D_MODEL), lambda i: (0, 0))],
        out_shape=[jax.ShapeDtypeStruct((1, S, D_MODEL), F32), jax.ShapeDtypeStruct((S, 2560), BF16),
                   jax.ShapeDtypeStruct((1, D_MODEL), F32)],
        compiler_params=_cp(("arbitrary",)),
    )(duc, dq, dk, dv, x2, dh1, g1, wa)


def _matmul_tn(xm, ym, tm, ts, name, column_block=None):
    S, M = xm.shape
    N = ym.shape[1]

    def body(x_ref, y_ref, o_ref):
        @pl.when(pl.program_id(1) == 0)
        def _():
            o_ref[...] = jnp.zeros_like(o_ref)

        xt = x_ref[...].T
        if column_block is None:
            o_ref[...] += _dot(xt, y_ref[...])
        else:
            for j in range(N // column_block):
                o_ref[j] += _dot(xt, y_ref[:, column_block * j:column_block * (j + 1)])

    if column_block is None:
        out_spec = pl.BlockSpec((tm, N), lambda m, s: (m, 0))
        out_shape = jax.ShapeDtypeStruct((M, N), F32)
    else:
        out_spec = pl.BlockSpec((N // column_block, tm, column_block), lambda m, s: (0, m, 0))
        out_shape = jax.ShapeDtypeStruct((N // column_block, M, column_block), F32)
    return pl.pallas_call(
        body, name=name, grid=(M // tm, S // ts),
        in_specs=[pl.BlockSpec((ts, tm), lambda m, s: (s, m)), pl.BlockSpec((ts, N), lambda m, s: (s, 0))],
        out_specs=out_spec, out_shape=out_shape,
        compiler_params=_cp(("parallel", "arbitrary")),
    )(xm, ym)


def _sibling_halves(grads, name):
    n = len(grads)

    def body(*refs):
        ins, outs, ssem, rsem = refs[:n], refs[n:2 * n], refs[2 * n], refs[2 * n + 1]
        x, y, c = lax.axis_index("x"), lax.axis_index("y"), lax.axis_index("c")
        copies = []
        for k in range(n):
            for j in range(N_CHIPS):
                copies.append(pltpu.make_async_remote_copy(
                    src_ref=ins[k].at[j, 1 - c], dst_ref=outs[k].at[j],
                    send_sem=ssem.at[N_CHIPS * k + j], recv_sem=rsem.at[N_CHIPS * k + j],
                    device_id=(x, y, 1 - c), device_id_type=MESH))
        for cp in copies:
            cp.start()
        for cp in copies:
            cp.wait()

    shapes = [jax.ShapeDtypeStruct((g.shape[0],) + g.shape[2:], F32) for g in grads]
    return pl.pallas_call(
        body, name=name, out_shape=shapes,
        in_specs=[_hbm()] * n, out_specs=[_hbm()] * n,
        scratch_shapes=[pltpu.SemaphoreType.DMA((N_CHIPS * n,)), pltpu.SemaphoreType.DMA((N_CHIPS * n,))],
    )(*grads)


def _add_half(c_arr, g, landed, name):
    def body(c_ref, g_ref, l_ref, o_ref):
        o_ref[...] = (g_ref[...] + l_ref[...]).astype(BF16)

    rows, n = g.shape[2], g.shape[3]
    grid = (N_CHIPS,)
    g_spec = pl.BlockSpec((None, None, rows, n), lambda j, c: (j, c[0], 0, 0))
    l_spec = pl.BlockSpec((None, rows, n), lambda j, c: (j, 0, 0))
    return pl.pallas_call(
        body, name=name,
        grid_spec=pltpu.PrefetchScalarGridSpec(num_scalar_prefetch=1, grid=grid, in_specs=[g_spec, l_spec],
                                               out_specs=l_spec),
        out_shape=jax.ShapeDtypeStruct(landed.shape, BF16),
        compiler_params=_cp(("parallel",)),
    )(c_arr, g, landed)


class _ScatterPlan:
    def __init__(self, ins, outs, lsem, ssem, rsem):
        x, y, c = lax.axis_index("x"), lax.axis_index("y"), lax.axis_index("c")
        me = 2 * x + y
        self.copies = []
        for k in range(len(ins)):
            self.copies.append(pltpu.make_async_copy(ins[k].at[me], outs[k].at[me], lsem.at[k]))
            for r, chip in enumerate([(1 - x, y), (x, 1 - y), (1 - x, 1 - y)]):
                self.copies.append(pltpu.make_async_remote_copy(
                    src_ref=ins[k].at[2 * chip[0] + chip[1]], dst_ref=outs[k].at[me],
                    send_sem=ssem.at[3 * k + r], recv_sem=rsem.at[3 * k + r],
                    device_id=(chip[0], chip[1], c), device_id_type=MESH))

    def start(self):
        for cp in self.copies:
            cp.start()

    def finish(self):
        for cp in self.copies:
            cp.wait()


def _scatter_sems(n):
    return [pltpu.SemaphoreType.DMA((n,)), pltpu.SemaphoreType.DMA((3 * n,)), pltpu.SemaphoreType.DMA((3 * n,))]


def _chip_scatter(parts):
    n = len(parts)

    def body(*refs):
        plan = _ScatterPlan(refs[:n], refs[n:2 * n], *refs[2 * n:])
        plan.start()
        plan.finish()

    shapes = [jax.ShapeDtypeStruct(p.shape, p.dtype) for p in parts]
    return pl.pallas_call(
        body, name="grad_chip_scatter", out_shape=shapes,
        in_specs=[_hbm()] * n, out_specs=[_hbm()] * n, scratch_shapes=_scatter_sems(n),
    )(*parts)


def _sum_chips(landed, name):
    _, rows, n = landed.shape
    tr = 256 if rows % 256 == 0 else rows

    def body(a_ref, b_ref, c_ref, d_ref, o_ref):
        f = lambda ref: ref[...].astype(F32)
        o_ref[...] = ((f(a_ref) + f(b_ref)) + f(c_ref)) + f(d_ref)

    specs = [pl.BlockSpec((None, tr, n), functools.partial(lambda i, j: (j, i, 0), j=j)) for j in range(N_CHIPS)]
    return pl.pallas_call(
        body, name=name, grid=(rows // tr,), in_specs=specs,
        out_specs=pl.BlockSpec((tr, n), lambda i: (i, 0)),
        out_shape=jax.ShapeDtypeStruct((rows, n), F32),
        compiler_params=_cp(("parallel",)),
    )(landed, landed, landed, landed)


def _share_halves(halves):
    n = len(halves)

    def body(*refs):
        ins, outs = refs[:n], refs[n:2 * n]
        ssem, rsem = refs[2 * n:]
        x, y, c = lax.axis_index("x"), lax.axis_index("y"), lax.axis_index("c")
        copies = [pltpu.make_async_remote_copy(
            src_ref=ins[k], dst_ref=outs[k], send_sem=ssem.at[k], recv_sem=rsem.at[k],
            device_id=(x, y, 1 - c), device_id_type=MESH) for k in range(n)]
        for cp in copies:
            cp.start()
        for cp in copies:
            cp.wait()

    shapes = [jax.ShapeDtypeStruct(h.shape, F32) for h in halves]
    return pl.pallas_call(
        body, name="grad_share_halves", out_shape=shapes,
        in_specs=[_hbm()] * n, out_specs=[_hbm()] * n,
        scratch_shapes=[pltpu.SemaphoreType.DMA((n,)), pltpu.SemaphoreType.DMA((n,))],
    )(*halves)


def _allreduce_small(packed):
    rows, n = packed.shape

    def body(in_ref, out_ref, land_ref, ssem, rsem):
        x, y, c = lax.axis_index("x"), lax.axis_index("y"), lax.axis_index("c")
        me = 4 * x + 2 * y + c
        land_ref[me] = in_ref[...]
        copies = []
        for r in range(1, 8):
            tx = 1 - x if r & 4 else x
            ty = 1 - y if r & 2 else y
            tc = 1 - c if r & 1 else c
            cp = pltpu.make_async_remote_copy(
                src_ref=in_ref, dst_ref=land_ref.at[me], send_sem=ssem.at[r - 1], recv_sem=rsem.at[r - 1],
                device_id=(tx, ty, tc), device_id_type=MESH)
            cp.start()
            copies.append(cp)
        for cp in copies:
            cp.wait()
        acc = land_ref[0]
        for k in range(1, 8):
            acc = acc + land_ref[k]
        out_ref[...] = acc

    return pl.pallas_call(
        body, name="allreduce_small", out_shape=jax.ShapeDtypeStruct((rows, n), F32),
        in_specs=[pl.BlockSpec(memory_space=pltpu.VMEM)], out_specs=pl.BlockSpec(memory_space=pltpu.VMEM),
        scratch_shapes=[pltpu.VMEM((8, rows, n), F32), pltpu.SemaphoreType.DMA((7,)),
                        pltpu.SemaphoreType.DMA((7,))],
    )(packed)


def _adamw_math(w, g, m, v):
    m = ADAM_B1 * m + (1.0 - ADAM_B1) * g
    v = ADAM_B2 * v + (1.0 - ADAM_B2) * (g * g)
    m_hat = m / (1.0 - ADAM_B1 ** ADAM_STEP)
    v_hat = v / (1.0 - ADAM_B2 ** ADAM_STEP)
    return -ADAM_LR * (m_hat / (jnp.sqrt(v_hat) + ADAM_EPS) + ADAM_WD * w), m, v


def _adamw_halves(c_arr, w, mine, other, m, v, name):
    rows, n = mine.shape
    tr = 256 if rows % 256 == 0 else rows
    nb = rows // tr

    def body(c_ref, w_ref, a_ref, b_ref, m_ref, v_ref, g_ref, d_ref, mo_ref, vo_ref):
        g = jnp.where(pl.program_id(0) == c_ref[0], a_ref[...], b_ref[...])
        g_ref[...] = g
        d_ref[...], mo_ref[...], vo_ref[...] = _adamw_math(w_ref[...], g, m_ref[...], v_ref[...])

    full = pl.BlockSpec((None, tr, n), lambda h, i, c: (0, h * nb + i, 0))
    half = pl.BlockSpec((tr, n), lambda h, i, c: (i, 0))
    return pl.pallas_call(
        body, name=name,
        grid_spec=pltpu.PrefetchScalarGridSpec(num_scalar_prefetch=1, grid=(2, nb),
                                               in_specs=[full, half, half, full, full], out_specs=[full] * 4),
        out_shape=[jax.ShapeDtypeStruct((1, 2 * rows, n), F32)] * 4,
        compiler_params=_cp(("parallel", "parallel")),
    )(c_arr, w, mine, other, m, v)


def _adamw(w, g, m, v, name):
    rows, n = w.shape
    tr = 256 if rows % 256 == 0 else rows

    def body(w_ref, g_ref, m_ref, v_ref, d_ref, mo_ref, vo_ref):
        d_ref[...], mo_ref[...], vo_ref[...] = _adamw_math(w_ref[...], g_ref[...], m_ref[...], v_ref[...])

    spec = pl.BlockSpec((tr, n), lambda i: (i, 0))
    return pl.pallas_call(
        body, name=name, grid=(rows // tr,), in_specs=[spec] * 4, out_specs=[spec] * 3,
        out_shape=[jax.ShapeDtypeStruct((rows, n), F32)] * 3,
        compiler_params=_cp(("parallel",)),
    )(w, g, m, v)


def _rows8(a):
    a = a.reshape(-1, 128)
    return jnp.pad(a, ((0, (-a.shape[0]) % 8), (0, 0)))


def kernel(x, g_pre_mix, w_in, conv_w, conv_b, conv_ln_g, conv_ln_b, attn_norm_g, w_out, g_post_mix, g_pre_ffn, w_gate, w_up, w_down, g_post_ffn, loss_target, m_g_pre_mix, m_w_in, m_conv_w, m_conv_b, m_conv_ln_g, m_conv_ln_b, m_attn_norm_g, m_w_out, m_g_post_mix, m_g_pre_ffn, m_w_gate, m_w_up, m_w_down, m_g_post_ffn, v_g_pre_mix, v_w_in, v_conv_w, v_conv_b, v_conv_ln_g, v_conv_ln_b, v_attn_norm_g, v_w_out, v_g_post_mix, v_g_pre_ffn, v_w_gate, v_w_up, v_w_down, v_g_post_ffn):
    S = x.shape[1]
    tm_big = min(512, S)
    tm_ffn = min(256, S)
    tk_att = min(1024, S)
    t_att_fwd = min(1024, S)
    t_att_bwd = tk_att // KEY_BLOCKS
    chip = 2 * lax.axis_index("x") + lax.axis_index("y")
    core = lax.axis_index("c")
    x2 = x.reshape(S, D_MODEL)
    tgt = loss_target.reshape(S, D_MODEL)
    ag = attn_norm_g.reshape(1, CONV_CH)

    a_sh = w_in[0].astype(BF16)
    b_sh = jnp.stack([w_gate[0], w_up[0]]).astype(BF16)
    c_sh = jnp.concatenate([w_out[0], w_down[0]], axis=0).astype(BF16)
    cw_sh = jnp.pad(conv_w[0, :, 0, :], ((0, 1), (0, 0)))
    own = lambda full, shard: lax.dynamic_update_index_in_dim(full, shard, chip, 0)
    cols = lambda w4: jnp.transpose(w4, (1, 0, 2)).reshape(w4.shape[1], N_CHIPS * w4.shape[2])
    wa4, cw4 = _gather_weights([a_sh, cw_sh], [False, False])
    wa = own(wa4, a_sh)
    cwf = cols(own(cw4, cw_sh))

    a_bf, uc, qkv = _in_proj(x2, g_pre_mix, wa, tm_big)
    conv_out, yconv = _conv_fwd(uc, cwf, conv_b, conv_ln_g, conv_ln_b, tm_big)
    o, wb4, wc4 = _attn_fwd(qkv, t_att_fwd, tk_att, [b_sh, c_sh], [True, False])
    wb4, wc4 = own(wb4, b_sh), own(wc4, c_sh)
    wg, wu = cols(wb4[:, 0]), cols(wb4[:, 1])
    wo = wc4[:, :OUT_SH].reshape(D_MODEL, D_MODEL)
    wd = wc4[:, OUT_SH:].reshape(D_FF, D_MODEL)
    mixed, yv, h1, f_in = _out_proj(conv_out, o, ag, wo, x2, g_post_mix, g_pre_ffn, tm_big)
    df, dh2, dg4, loss_part, gt_bf, up_bf, act = _ffn_fwd(f_in, h1, tgt, wg, wu, wd, g_post_ffn, tm_ffn)

    dgt, dup, dh1, dy, dg3, dg2 = _ffn_bwd(gt_bf, up_bf, df, dh2, h1, yv, wg, wu, wd, g_pre_ffn, g_post_mix, tm_ffn)
    dco, do, dag = _out_bwd(dy, o, ag, wo, tm_big)
    ts = min(512, S)
    gw_out = _matmul_tn(mixed, dy, D_MODEL, ts, "grad_w_out")
    gw_gate = _matmul_tn(f_in, dgt, D_MODEL, ts, "grad_w_gate")
    gw_up = _matmul_tn(f_in, dup, D_MODEL, ts, "grad_w_up")
    gw_down = _matmul_tn(act, df, D_FF // 2, ts, "grad_w_down")

    by_cols = lambda g: jnp.transpose(g.reshape(2, D_MODEL // 2, N_CHIPS, -1), (2, 0, 1, 3))
    by_rows = lambda g: g.reshape(N_CHIPS, 2, g.shape[0] // (2 * N_CHIPS), g.shape[1])
    c_arr = core.reshape(1).astype(jnp.int32)

    def chip_partials(views, nms):
        landed = _sibling_halves(views, "grad_sibling_halves_" + nms[0])
        return [_add_half(c_arr, g, l, "grad_half_" + nm) for g, l, nm in zip(views, landed, nms)]

    early = ["w_gate", "w_up", "w_out", "w_down"]
    parts = chip_partials([by_cols(gw_gate), by_cols(gw_up), by_rows(gw_out), by_rows(gw_down)], early)
    dq, dk, dv, *slots = _attn_bwd(qkv, do, t_att_bwd, tk_att, parts)
    duc, dcw, dcb, dlg, dlb = _conv_bwd(uc, yconv, dco, cwf, conv_ln_g, conv_ln_b, tm_big)
    grad_x, du, dg1 = _in_bwd(duc, dq, dk, dv, x2, dh1, g_pre_mix, wa, tm_big)
    gw_in = _matmul_tn(a_bf, du, D_MODEL, ts, "grad_w_in", column_block=IN_SH)
    slots += _chip_scatter(chip_partials([gw_in.reshape(N_CHIPS, 2, D_MODEL // 2, IN_SH)], ["w_in"]))
    names = early + ["w_in"]
    halves = [_sum_chips(s, "grad_sum_" + nm) for s, nm in zip(slots, names)]
    others = _share_halves(halves)
    mine = dict(zip(names, halves))
    other = dict(zip(names, others))

    small = [dg1, dcb, dlg, dlb, dag, dg2, dg3, dg4]
    packed = jnp.concatenate([_rows8(s) for s in small] + [_rows8(dcw), _rows8(loss_part)], axis=0)
    red = _allreduce_small(packed)
    sizes = [D_MODEL, CONV_CH, CONV_CH, CONV_CH, CONV_CH, D_MODEL, D_MODEL, D_MODEL]
    g_small = [red[8 * k:8 * k + n // 128].reshape(1, n) for k, n in enumerate(sizes)]
    cw_red = red[64:64 + 128].reshape(HALO, CONV_CH)
    g_cw = lax.dynamic_slice(cw_red, (0, chip * 128), (HALO, 128))
    loss = red[192, 0]

    big = []
    for w, m, v, nm in [(w_in, m_w_in, v_w_in, "w_in"), (w_out, m_w_out, v_w_out, "w_out"),
                        (w_gate, m_w_gate, v_w_gate, "w_gate"), (w_up, m_w_up, v_w_up, "w_up"),
                        (w_down, m_w_down, v_w_down, "w_down")]:
        big.append(_adamw_halves(c_arr, w, mine[nm], other[nm], m, v, "adamw_" + nm))
    sm_w = [g_pre_mix, conv_b, conv_ln_g, conv_ln_b, ag, g_post_mix, g_pre_ffn, g_post_ffn]
    sm_m = [m_g_pre_mix, m_conv_b, m_conv_ln_g, m_conv_ln_b, m_attn_norm_g, m_g_post_mix, m_g_pre_ffn, m_g_post_ffn]
    sm_v = [v_g_pre_mix, v_conv_b, v_conv_ln_g, v_conv_ln_b, v_attn_norm_g, v_g_post_mix, v_g_pre_ffn, v_g_post_ffn]
    pad_cw = lambda a: jnp.pad(a[0, :, 0, :], ((0, 1), (0, 0)))

    def pack(vecs, cw):
        return jnp.concatenate([_rows8(a) for a in vecs] + [cw], axis=0)

    sd, smn, svn = _adamw(pack(sm_w, pad_cw(conv_w)), pack(g_small, g_cw), pack(sm_m, pad_cw(m_conv_w)),
                          pack(sm_v, pad_cw(v_conv_w)), "adamw_small")

    def unpack(p):
        vecs = [p[8 * k:8 * k + n // 128].reshape(1, n) for k, n in enumerate(sizes)]
        return vecs, p[64:64 + CONV_WIDTH].reshape(1, CONV_WIDTH, 1, 128)

    def ordered(vecs, cw, w_in_, w_out_, w_gate_, w_up_, w_down_):
        g1_, cb_, lg_, lb_, ag_, g2_, g3_, g4_ = vecs
        return [g1_, w_in_, cw, cb_, lg_, lb_, ag_.reshape(1, 8, HEAD_DIM), w_out_, g2_, g3_,
                w_gate_, w_up_, w_down_, g4_]

    grads = ordered(g_small, g_cw[:CONV_WIDTH].reshape(1, CONV_WIDTH, 1, 128), *[b[0] for b in big])
    outs = []
    for idx, p in enumerate((sd, smn, svn)):
        vecs, cw = unpack(p)
        outs += ordered(vecs, cw, *[b[idx + 1] for b in big])
    return (loss, grad_x, *grads, *outs)
```

```python
import functools

import jax
import jax.numpy as jnp
from jax import lax
from jax.experimental import pallas as pl
from jax.experimental.pallas import tpu as pltpu

F32 = jnp.float32
BF16 = jnp.bfloat16
MESH = pl.DeviceIdType.MESH

D_MODEL = 1024
CONV_CH = 512
CONV_WIDTH = 31
HEAD_DIM = 64
PAIR = 2 * HEAD_DIM
N_PAIRS = 4
D_FF = 2816
N_CHIPS = 4
IN_SH = 2560 // N_CHIPS
OUT_SH = D_MODEL // N_CHIPS
EPS = 1e-6
HALO = 32

ADAM_LR = 0.001
ADAM_B1 = 0.9
ADAM_B2 = 0.999
ADAM_EPS = 1e-08
ADAM_WD = 0.01
ADAM_STEP = 10

VMEM_LIMIT = 56 * 2 ** 20
VMEM_LIMIT_ATTN_BWD = 62 * 2 ** 20


def _cp(sem=None, vmem=VMEM_LIMIT):
    return pltpu.CompilerParams(dimension_semantics=sem, vmem_limit_bytes=vmem)


def _hbm():
    return pl.BlockSpec(memory_space=pltpu.HBM)


def _const_spec(shape):
    nd = len(shape)
    return pl.BlockSpec(shape, lambda *_: (0,) * nd, pipeline_mode=pl.Buffered(1))


def _dot(a, b):
    return jnp.dot(a, b, preferred_element_type=F32)


def _dot_nt(a, b):
    return lax.dot_general(a, b, (((1,), (1,)), ((), ())), preferred_element_type=F32)


def _dot_tn(a, b):
    return lax.dot_general(a, b, (((0,), (0,)), ((), ())), preferred_element_type=F32)


def _split3(x):
    b0 = x.astype(BF16)
    r1 = x - b0.astype(F32)
    b1 = r1.astype(BF16)
    b2 = (r1 - b1.astype(F32)).astype(BF16)
    return b0, b1, b2


def _sigmoid(x):
    return 1.0 / (1.0 + jnp.exp(-x))


def _head_mean(x, seg):
    b0, b1, b2 = _split3(x)
    return (_dot(b0, seg) + _dot(b1, seg) + _dot(b2, seg)) * (1.0 / HEAD_DIM)


def _seg_matrix(n):
    r = lax.broadcasted_iota(jnp.int32, (n, n), 0) // HEAD_DIM
    c = lax.broadcasted_iota(jnp.int32, (n, n), 1) // HEAD_DIM
    return (r == c).astype(BF16)


def _rms(x):
    return lax.rsqrt(jnp.mean(x * x, axis=-1, keepdims=True) + EPS)


def _rms_bwd(dy, n, r, g):
    dn = dy * g
    dx = r * (dn - n * jnp.mean(dn * n, axis=-1, keepdims=True))
    return dx, dy * n


class _GatherPlan:
    def __init__(self, srcs, outs, lead, ssem, rsem):
        self.srcs, self.outs, self.lead, self.ssem, self.rsem = srcs, outs, lead, ssem, rsem
        x, y, self.c = lax.axis_index("x"), lax.axis_index("y"), lax.axis_index("c")
        self.me = 2 * x + y
        self.sibling = (x, y, 1 - self.c)
        self.chips = [(1 - x, y), (x, 1 - y), (1 - x, 1 - y)]

    def _half(self, ref, i, h):
        if self.lead[i]:
            return ref.at[h]
        rows = ref.shape[0] // 2
        return ref.at[pl.ds(h * rows, rows)]

    def _ici(self, i, k, origin):
        return pltpu.make_async_remote_copy(
            src_ref=self._half(self.srcs[i], i, self.c), dst_ref=self._half(self.outs[i].at[origin], i, self.c),
            send_sem=self.ssem.at[6 * i + k], recv_sem=self.rsem.at[6 * i + k],
            device_id=(self.chips[k][0], self.chips[k][1], self.c), device_id_type=MESH)

    def _d2d(self, i, k, h):
        origin = 2 * self.chips[k][0] + self.chips[k][1]
        piece = self._half(self.outs[i].at[origin], i, h)
        return pltpu.make_async_remote_copy(
            src_ref=piece, dst_ref=piece, send_sem=self.ssem.at[6 * i + 3 + k],
            recv_sem=self.rsem.at[6 * i + 3 + k], device_id=self.sibling, device_id_type=MESH)

    def _each(self):
        return [(i, k) for i in range(len(self.srcs)) for k in range(3)]

    def start(self):
        for i, k in self._each():
            self._ici(i, k, self.me).start()

    def forward(self):
        for i, k in self._each():
            self._ici(i, k, 2 * self.chips[k][0] + self.chips[k][1]).wait_recv()
            self._d2d(i, k, self.c).start()

    def finish(self):
        for i, k in self._each():
            self._d2d(i, k, 1 - self.c).wait_recv()
        for i, k in self._each():
            self._ici(i, k, self.me).wait_send()
            self._d2d(i, k, self.c).wait_send()


def _gather_shapes(shards):
    return [jax.ShapeDtypeStruct((N_CHIPS,) + s.shape, s.dtype) for s in shards]


def _gather_weights(shards, lead):
    n = len(shards)

    def body(*refs):
        plan = _GatherPlan(refs[:n], refs[n:2 * n], lead, refs[2 * n], refs[2 * n + 1])
        plan.start()
        plan.forward()
        plan.finish()

    return pl.pallas_call(
        body, name="gather_weights", out_shape=_gather_shapes(shards),
        in_specs=[_hbm()] * n, out_specs=[_hbm()] * n,
        scratch_shapes=[pltpu.SemaphoreType.DMA((6 * n,)), pltpu.SemaphoreType.DMA((6 * n,))],
    )(*shards)


def _in_proj(x2, g1, wa, tm):
    S = x2.shape[0]

    def body(x_ref, g_ref, w_ref, a_ref, uc_ref, qkv_ref):
        x = x_ref[...]
        a = (x * _rms(x) * g_ref[...]).astype(BF16)
        a_ref[...] = a
        u = [_dot(a, w_ref[j]) for j in range(N_CHIPS)]
        uc_ref[:, 0:640] = u[0]
        uc_ref[:, 640:1024] = u[1][:, 0:384]
        qkv_ref[:, 0:256] = u[1][:, 384:640].astype(BF16)
        qkv_ref[:, 256:896] = u[2].astype(BF16)
        qkv_ref[:, 896:1536] = u[3].astype(BF16)

    return pl.pallas_call(
        body, name="in_proj", grid=(S // tm,),
        in_specs=[pl.BlockSpec((tm, D_MODEL), lambda i: (i, 0)), _const_spec((1, D_MODEL)),
                  _const_spec(wa.shape)],
        out_specs=[pl.BlockSpec((tm, D_MODEL), lambda i: (i, 0)),
                   pl.BlockSpec((tm, 2 * CONV_CH), lambda i: (i, 0)),
                   pl.BlockSpec((tm, 1536), lambda i: (i, 0))],
        out_shape=[jax.ShapeDtypeStruct((S, D_MODEL), BF16), jax.ShapeDtypeStruct((S, 2 * CONV_CH), F32),
                   jax.ShapeDtypeStruct((S, 1536), BF16)],
        compiler_params=_cp(("parallel",)),
    )(x2, g1, wa)


SUBLANES = 8


def _shift_copies(src_ref, sh_ref):
    rows = sh_ref.shape[1]
    for b in range(1, SUBLANES):
        sh_ref[b - 1] = src_ref[pl.ds(b, rows), :]


def _rows_at(src_ref, sh_ref, off, rows):
    a, b = divmod(off, SUBLANES)
    if b == 0:
        return src_ref[pl.ds(SUBLANES * a, rows), :]
    return sh_ref[b - 1, pl.ds(SUBLANES * a, rows), :]


def _conv_taps(cw_ref, src_ref, sh_ref, offs, rows):
    acc = None
    for w, off in enumerate(offs):
        term = cw_ref[w:w + 1, :] * _rows_at(src_ref, sh_ref, off, rows)
        acc = term if acc is None else acc + term
    return acc


def _glu(uc):
    return uc[:, :CONV_CH] * _sigmoid(uc[:, CONV_CH:])


def _conv_fwd(uc, cwf, cb, lg, lb, tm):
    S = uc.shape[0]
    hb = tm // HALO

    def body(uc_ref, prev_ref, cw_ref, cb_ref, lg_ref, lb_ref, out_ref, y_ref, glu_ref, sh_ref):
        i = pl.program_id(0)
        glu_ref[0:HALO, :] = jnp.where(i == 0, 0.0, _glu(prev_ref[...]))
        glu_ref[HALO:HALO + tm, :] = _glu(uc_ref[...])
        glu_ref[HALO + tm:HALO + tm + SUBLANES, :] = jnp.zeros((SUBLANES, CONV_CH), F32)
        _shift_copies(glu_ref, sh_ref)
        offs = [HALO - (CONV_WIDTH - 1) + w for w in range(CONV_WIDTH)]
        y = _conv_taps(cw_ref, glu_ref, sh_ref, offs, tm) + cb_ref[...]
        y_ref[...] = y
        mu = jnp.mean(y, axis=-1, keepdims=True)
        yc = y - mu
        rstd = lax.rsqrt(jnp.mean(yc * yc, axis=-1, keepdims=True) + EPS)
        ln = yc * rstd * lg_ref[...] + lb_ref[...]
        out_ref[...] = (ln * _sigmoid(ln)).astype(BF16)

    return pl.pallas_call(
        body, name="conv_fwd", grid=(S // tm,),
        in_specs=[pl.BlockSpec((tm, 2 * CONV_CH), lambda i: (i, 0)),
                  pl.BlockSpec((HALO, 2 * CONV_CH), lambda i: (jnp.maximum(i * hb - 1, 0), 0)),
                  _const_spec(cwf.shape), _const_spec((1, CONV_CH)), _const_spec((1, CONV_CH)),
                  _const_spec((1, CONV_CH))],
        out_specs=[pl.BlockSpec((tm, CONV_CH), lambda i: (i, 0))] * 2,
        out_shape=[jax.ShapeDtypeStruct((S, CONV_CH), BF16), jax.ShapeDtypeStruct((S, CONV_CH), F32)],
        scratch_shapes=[pltpu.VMEM((HALO + tm + SUBLANES, CONV_CH), F32),
                        pltpu.VMEM((SUBLANES - 1, HALO + tm, CONV_CH), F32)],
        compiler_params=_cp(("parallel",)),
    )(uc, uc, cwf, cb, lg, lb)


def _lane_mask(h):
    lane = lax.broadcasted_iota(jnp.int32, (1, PAIR), 1)
    return (lane >= HEAD_DIM * h) & (lane < HEAD_DIM * (h + 1))


def _neg_abs(x):
    bits = lax.bitcast_convert_type(x, jnp.uint32) | jnp.uint32(0x80000000)
    return lax.bitcast_convert_type(bits, F32)


def _tri_dot(x, m):
    return _dot(x.astype(BF16), m)


MASKED = -1e30
KEY_BLOCKS = 4


def _running_sums(x, m, reverse, start=None):
    t = m.shape[0]
    blocks = x.shape[1] // t
    order = range(blocks - 1, -1, -1) if reverse else range(blocks)
    out = [None] * blocks
    carry = start
    for b in order:
        xb = x[:, b * t:(b + 1) * t]
        cb = _tri_dot(xb, m)
        out[b] = cb if carry is None else cb + carry
        rs = jnp.sum(xb, axis=1, keepdims=True)
        carry = rs if carry is None else carry + rs
    return jnp.concatenate(out, axis=1), carry


def _sb_tile(z, r, m_suf):
    sp = jnp.maximum(z, 0.0) + jnp.log(1.0 + jnp.exp(_neg_abs(z)))
    c, rs = _running_sums(sp, m_suf, reverse=True, start=r)
    return jnp.exp(z - c), sp, rs


def _scores(qm, kt, mask):
    z = _dot_nt(qm, kt)
    return z if mask is None else jnp.where(mask, z, MASKED)


def _causal_mask(i, sb, t, tk, w=None):
    w = tk if w is None else w
    row = lax.broadcasted_iota(jnp.int32, (t, w), 0) + i * t
    col = lax.broadcasted_iota(jnp.int32, (t, w), 1) + sb * tk
    return col < row


def _sweep_pairs(count, tile):
    def step(n, carry):
        tile(2 * n)
        tile(2 * n + 1)
        return carry

    lax.fori_loop(0, count // 2, step, 0)
    pl.when(lax.rem(count, 2) == 1)(lambda: tile(count - 1))


def _sweep(first, count, down, fetch, load, work):
    lo, hi = (first - count, first) if down else (first, first + count)
    tile = lambda j: jnp.clip(first - j if down else first + j, lo, hi)
    fetch(first, 0, True)

    def step(n, carry):
        j = 2 * n
        vals = load(0)
        fetch(tile(j + 1), 1, False)
        work(tile(j), vals)
        vals = load(1)
        fetch(tile(j + 2), 0, False)
        work(tile(j + 1), vals)
        return carry

    lax.fori_loop(0, (count + 1) // 2, step, 0)

    @pl.when(lax.rem(count, 2) == 0)
    def _():
        work(tile(count), load(0))


def _suffix_matrix(t, prefix=False):
    row = lax.broadcasted_iota(jnp.int32, (t, t), 0)
    col = lax.broadcasted_iota(jnp.int32, (t, t), 1)
    return ((row <= col) if prefix else (row >= col)).astype(BF16)


def _attn_fwd(qkv, t, tk, shards, lead):
    S = qkv.shape[0]
    assert t == tk, "the forward cuts its diagonal tile by blocks: query block and key tile must match"

    ng = len(shards)
    nq = S // t

    def body(*refs):
        q_ref, k_ref, v_ref = refs[:3]
        o_ref = refs[3 + ng]
        acc_ref, r_ref, z_buf, ssem, rsem = refs[4 + 2 * ng:]
        p = pl.program_id(0)
        i = pl.program_id(1)
        plan = _GatherPlan(refs[3:3 + ng], refs[4 + ng:4 + 2 * ng], lead, ssem, rsem)
        pl.when((p == 0) & (i == 0))(plan.start)
        pl.when((p == 1) & (i == 0))(plan.forward)
        blk = tk // KEY_BLOCKS
        m_suf = _suffix_matrix(blk)
        q = q_ref[...]
        hms = [_lane_mask(h) for h in range(2)]
        qms = [jnp.where(hm, q, 0) * 0.125 for hm in hms]

        def rows(sb, w=tk):
            return pl.ds(pl.multiple_of(sb * tk, tk), w)

        for rb in range(KEY_BLOCKS):
            w = (rb + 1) * blk
            part = slice(rb * blk, w)
            kt = k_ref[rows(i, w), :]
            vt = v_ref[rows(i, w), :]
            row = lax.broadcasted_iota(jnp.int32, (blk, w), 0) + rb * blk
            mask = lax.broadcasted_iota(jnp.int32, (blk, w), 1) < row
            out = jnp.zeros((blk, PAIR), F32)
            for h in range(2):
                a_loc, _, rs = _sb_tile(_scores(qms[h][part], kt, mask), None, m_suf)
                out = out + _dot(a_loc.astype(BF16), jnp.where(hms[h], vt, 0))
                r_ref[h, part] = rs
            acc_ref[part, :] = out

        def fetch(sb, slot, first):
            kt = k_ref[rows(sb), :]
            for h in range(2):
                z_buf[slot, h] = _scores(qms[h], kt, None)

        def load(slot):
            return [z_buf[slot, h] for h in range(2)]

        def work(sb, zs):
            vt = v_ref[rows(sb), :]
            for h in range(2):
                a_loc, _, rs = _sb_tile(zs[h], None, m_suf)
                r = r_ref[h]
                acc_ref[...] += _dot(a_loc.astype(BF16), jnp.where(hms[h], vt, 0)) * jnp.exp(-r)
                r_ref[h] = r + rs

        pl.when(i >= 1)(lambda: _sweep(i - 1, i - 1, True, fetch, load, work))
        o_ref[...] = acc_ref[...]
        pl.when((p == N_PAIRS - 1) & (i == nq - 1))(plan.finish)

    return pl.pallas_call(
        body, name="attn_fwd", grid=(N_PAIRS, nq),
        in_specs=[pl.BlockSpec((t, PAIR), lambda p, i: (i, p)),
                  pl.BlockSpec((S, PAIR), lambda p, i: (0, N_PAIRS + p)),
                  pl.BlockSpec((S, PAIR), lambda p, i: (0, 2 * N_PAIRS + p))] + [_hbm()] * ng,
        out_specs=[pl.BlockSpec((t, PAIR), lambda p, i: (i, p))] + [_hbm()] * ng,
        out_shape=[jax.ShapeDtypeStruct((S, N_PAIRS * PAIR), F32)] + _gather_shapes(shards),
        scratch_shapes=[pltpu.VMEM((t, PAIR), F32), pltpu.VMEM((2, t, 1), F32),
                        pltpu.VMEM((2, 2, t, tk), F32),
                        pltpu.SemaphoreType.DMA((6 * ng,)), pltpu.SemaphoreType.DMA((6 * ng,))],
        compiler_params=_cp(("arbitrary", "arbitrary")),
    )(qkv, qkv, qkv, *shards)


def _out_proj(conv_out, o, ag, wc, x2, g2, g3, tm):
    S = o.shape[0]

    def body(co_ref, o_ref, ag_ref, w_ref, x_ref, g2_ref, g3_ref, mix_ref, y_ref, h1_ref, fin_ref):
        seg = _seg_matrix(CONV_CH)
        o = o_ref[...]
        att = (o * lax.rsqrt(_head_mean(o * o, seg) + EPS) * ag_ref[...]).astype(BF16)
        co = co_ref[...]
        mix_ref[:, :CONV_CH] = co
        mix_ref[:, CONV_CH:] = att
        y = _dot(co, w_ref[0:CONV_CH, :]) + _dot(att, w_ref[CONV_CH:, :])
        y_ref[...] = y
        h1 = x_ref[...] + y * _rms(y) * g2_ref[...]
        h1_ref[...] = h1
        fin_ref[...] = (h1 * _rms(h1) * g3_ref[...]).astype(BF16)

    row = lambda w: pl.BlockSpec((tm, w), lambda i: (i, 0))
    return pl.pallas_call(
        body, name="out_proj", grid=(S // tm,),
        in_specs=[row(CONV_CH), row(CONV_CH), _const_spec((1, CONV_CH)), _const_spec(wc.shape),
                  row(D_MODEL), _const_spec((1, D_MODEL)), _const_spec((1, D_MODEL))],
        out_specs=[row(D_MODEL)] * 4,
        out_shape=[jax.ShapeDtypeStruct((S, D_MODEL), BF16), jax.ShapeDtypeStruct((S, D_MODEL), F32),
                   jax.ShapeDtypeStruct((S, D_MODEL), F32), jax.ShapeDtypeStruct((S, D_MODEL), BF16)],
        compiler_params=_cp(("parallel",)),
    )(conv_out, o, ag, wc, x2, g2, g3)


def _ffn_fwd(f_in, h1, tgt, wg, wu, wd, g4, tm):
    S = f_in.shape[0]

    def body(fin_ref, h1_ref, tgt_ref, wg_ref, wu_ref, wd_ref, g4_ref, df_ref, dh2_ref, dg4_ref, loss_ref,
             gt_ref, up_ref, act_ref):
        i = pl.program_id(0)
        fin = fin_ref[...]
        gt = _dot(fin, wg_ref[...])
        up = _dot(fin, wu_ref[...])
        act = (gt * _sigmoid(gt) * up).astype(BF16)
        gt_ref[...] = gt.astype(BF16)
        up_ref[...] = up.astype(BF16)
        act_ref[...] = act
        f = _dot(act, wd_ref[...])
        r = _rms(f)
        n = f * r
        g4 = g4_ref[...]
        err = h1_ref[...] + n * g4 - tgt_ref[...]
        dh2 = err * (1.0 / D_MODEL)
        dh2_ref[...] = dh2
        df, dg = _rms_bwd(dh2, n, r, g4)
        df_ref[...] = df.astype(BF16)

        @pl.when(i == 0)
        def _():
            dg4_ref[...] = jnp.zeros_like(dg4_ref)
            loss_ref[...] = jnp.zeros_like(loss_ref)

        dg4_ref[...] += jnp.sum(dg, axis=0, keepdims=True)
        part = jnp.sum(jnp.sum(err * err, axis=1, keepdims=True), axis=0, keepdims=True)
        loss_ref[...] += part * (0.5 / D_MODEL)

    row = lambda w: pl.BlockSpec((tm, w), lambda i: (i, 0))
    return pl.pallas_call(
        body, name="ffn_fwd", grid=(S // tm,),
        in_specs=[row(D_MODEL), row(D_MODEL), row(D_MODEL), _const_spec(wg.shape), _const_spec(wu.shape),
                  _const_spec(wd.shape), _const_spec((1, D_MODEL))],
        out_specs=[row(D_MODEL), row(D_MODEL), pl.BlockSpec((1, D_MODEL), lambda i: (0, 0)),
                   pl.BlockSpec((1, 128), lambda i: (0, 0)), row(D_FF), row(D_FF), row(D_FF)],
        out_shape=[jax.ShapeDtypeStruct((S, D_MODEL), BF16), jax.ShapeDtypeStruct((S, D_MODEL), F32),
                   jax.ShapeDtypeStruct((1, D_MODEL), F32), jax.ShapeDtypeStruct((1, 128), F32)]
        + [jax.ShapeDtypeStruct((S, D_FF), BF16)] * 3,
        compiler_params=_cp(("arbitrary",)),
    )(f_in, h1, tgt, wg, wu, wd, g4)


def _ffn_bwd(gt_bf, up_bf, df, dh2, h1, yv, wg, wu, wd, g3, g2, tm):
    S = df.shape[0]

    def body(gt_ref, up_ref, df_ref, dh2_ref, h1_ref, y_ref, wg_ref, wu_ref, wd_ref, g3_ref, g2_ref,
             dgt_ref, dup_ref, dh1_ref, dy_ref, dg3_ref, dg2_ref):
        i = pl.program_id(0)
        df = df_ref[...]
        gt = gt_ref[...].astype(F32)
        up = up_ref[...].astype(F32)
        sg = _sigmoid(gt)
        silu = gt * sg
        dact = _dot_nt(df, wd_ref[...])
        dgt = (dact * up * (sg * (1.0 + gt * (1.0 - sg)))).astype(BF16)
        dup = (dact * silu).astype(BF16)
        dgt_ref[...] = dgt
        dup_ref[...] = dup
        dfin = _dot_nt(dgt, wg_ref[...]) + _dot_nt(dup, wu_ref[...])
        h1 = h1_ref[...]
        r3 = _rms(h1)
        dh1_n, dg3 = _rms_bwd(dfin, h1 * r3, r3, g3_ref[...])
        dh1 = dh2_ref[...] + dh1_n
        dh1_ref[...] = dh1
        y = y_ref[...]
        r2 = _rms(y)
        dy, dg2 = _rms_bwd(dh1, y * r2, r2, g2_ref[...])
        dy_ref[...] = dy.astype(BF16)

        @pl.when(i == 0)
        def _():
            dg3_ref[...] = jnp.zeros_like(dg3_ref)
            dg2_ref[...] = jnp.zeros_like(dg2_ref)

        dg3_ref[...] += jnp.sum(dg3, axis=0, keepdims=True)
        dg2_ref[...] += jnp.sum(dg2, axis=0, keepdims=True)

    row = lambda w: pl.BlockSpec((tm, w), lambda i: (i, 0))
    vec = pl.BlockSpec((1, D_MODEL), lambda i: (0, 0))
    return pl.pallas_call(
        body, name="ffn_bwd", grid=(S // tm,),
        in_specs=[row(D_FF), row(D_FF)] + [row(D_MODEL)] * 4
        + [_const_spec(wg.shape), _const_spec(wu.shape), _const_spec(wd.shape),
           _const_spec((1, D_MODEL)), _const_spec((1, D_MODEL))],
        out_specs=[row(D_FF), row(D_FF), row(D_MODEL), row(D_MODEL), vec, vec],
        out_shape=[jax.ShapeDtypeStruct((S, D_FF), BF16)] * 2
        + [jax.ShapeDtypeStruct((S, D_MODEL), F32), jax.ShapeDtypeStruct((S, D_MODEL), BF16),
           jax.ShapeDtypeStruct((1, D_MODEL), F32), jax.ShapeDtypeStruct((1, D_MODEL), F32)],
        compiler_params=_cp(("arbitrary",)),
    )(gt_bf, up_bf, df, dh2, h1, yv, wg, wu, wd, g3, g2)


def _out_bwd(dy, o, ag, wc, tm):
    S = o.shape[0]

    def body(dy_ref, o_ref, ag_ref, w_ref, dco_ref, do_ref, dag_ref):
        i = pl.program_id(0)
        seg = _seg_matrix(CONV_CH)
        dy = dy_ref[...]
        dco_ref[...] = _dot_nt(dy, w_ref[0:CONV_CH, :])
        datt = _dot_nt(dy, w_ref[CONV_CH:, :])
        o = o_ref[...]
        r = lax.rsqrt(_head_mean(o * o, seg) + EPS)
        n = o * r
        dn = datt * ag_ref[...]
        do_ref[...] = (r * (dn - n * _head_mean(dn * n, seg))).astype(BF16)

        @pl.when(i == 0)
        def _():
            dag_ref[...] = jnp.zeros_like(dag_ref)

        dag_ref[...] += jnp.sum(datt * n, axis=0, keepdims=True)

    row = lambda w: pl.BlockSpec((tm, w), lambda i: (i, 0))
    return pl.pallas_call(
        body, name="out_bwd", grid=(S // tm,),
        in_specs=[row(D_MODEL), row(CONV_CH), _const_spec((1, CONV_CH)), _const_spec(wc.shape)],
        out_specs=[row(CONV_CH), row(CONV_CH), pl.BlockSpec((1, CONV_CH), lambda i: (0, 0))],
        out_shape=[jax.ShapeDtypeStruct((S, CONV_CH), F32), jax.ShapeDtypeStruct((S, CONV_CH), BF16),
                   jax.ShapeDtypeStruct((1, CONV_CH), F32)],
        compiler_params=_cp(("arbitrary",)),
    )(dy, o, ag, wc)


def _attn_bwd(qkv, do, t, tk, parts):
    S = qkv.shape[0]
    nk = S // tk
    ns = len(parts)

    def body(*refs):
        q_ref, k_ref, v_ref, do_ref = refs[:4]
        dq_ref, dk_hbm, dv_hbm = refs[4 + ns:7 + ns]
        g_buf, s_buf, r_ref, dq_acc, dk_ref, dv_ref, z_buf, da_buf = refs[7 + 2 * ns:15 + 2 * ns]
        p = pl.program_id(0)
        i = pl.program_id(1)
        plan = _ScatterPlan(refs[4:4 + ns], refs[7 + ns:7 + 2 * ns], *refs[15 + 2 * ns:])
        pl.when((p == 0) & (i == 0))(plan.start)
        last = (i * t + t - 1) // tk

        @pl.when(i == 0)
        def _():
            dk_ref[...] = jnp.zeros_like(dk_ref)
            dv_ref[...] = jnp.zeros_like(dv_ref)

        m_suf = _suffix_matrix(tk // KEY_BLOCKS)
        m_pre = _suffix_matrix(tk // KEY_BLOCKS, prefix=True)
        q = q_ref[...]
        do = do_ref[...]
        hms = [_lane_mask(h) for h in range(2)]
        qms = [jnp.where(hm, q, 0) * 0.125 for hm in hms]
        doms = [jnp.where(hm, do, 0) for hm in hms]
        dq_acc[...] = jnp.zeros_like(dq_acc)
        r_ref[...] = jnp.zeros_like(r_ref)

        def keys(sb, w=tk):
            return pl.ds(pl.multiple_of(sb * tk, tk), w)

        def matmuls1(sb, w, diagonal):
            kt = k_ref[keys(sb, w), :]
            vt = v_ref[keys(sb, w), :]
            mask = _causal_mask(i, sb, t, tk, w) if diagonal else None
            return [(_scores(qms[h], kt, mask), _dot_nt(doms[h], vt)) for h in range(2)]

        def sweep1(sb, w, vals):
            dv = jnp.zeros((w, PAIR), F32)
            for h in range(2):
                z, da = vals[h]
                A, sp, r_ref[h] = _sb_tile(z, r_ref[h], m_suf)
                g_buf[h, sb, :, 0:w] = A * da
                s_buf[h, sb, :, 0:w] = 1.0 - jnp.exp(-sp)
                dv = dv + _dot_tn(A.astype(BF16), doms[h])
            dv_ref[keys(sb, w), :] += dv

        def sweep2(sb, w):
            kt = k_ref[keys(sb, w), :]
            dk = jnp.zeros((w, PAIR), F32)
            for h in range(2):
                g = g_buf[h, sb, :, 0:w]
                pre, r_ref[h] = _running_sums(g, m_pre, reverse=False, start=r_ref[h])
                dzb = (g - s_buf[h, sb, :, 0:w] * pre).astype(BF16)
                dq_acc[...] += _dot(dzb, jnp.where(hms[h], kt, 0))
                dk = dk + _dot_tn(dzb, qms[h])
            dk_ref[keys(sb, w), :] += dk

        def diagonal_tile(tile):
            for nb in range(1, KEY_BLOCKS + 1):
                pl.when(lax.rem(i, KEY_BLOCKS) == nb - 1)(functools.partial(tile, nb * t))

        def fetch1(sb, slot, first):
            for h, (z, da) in enumerate(matmuls1(sb, tk, False)):
                z_buf[slot, h] = z
                da_buf[slot, h] = da

        def load1(slot):
            return [(z_buf[slot, h], da_buf[slot, h]) for h in range(2)]

        diagonal_tile(lambda w: sweep1(last, w, matmuls1(last, w, True)))
        pl.when(last >= 1)(lambda: _sweep(last - 1, last - 1, True, fetch1, load1,
                                          lambda sb, vals: sweep1(sb, tk, vals)))
        r_ref[...] = jnp.zeros_like(r_ref)
        _sweep_pairs(last, lambda sb: sweep2(sb, tk))
        diagonal_tile(lambda w: sweep2(last, w))
        dq_ref[...] = dq_acc[...] * 0.125

        @pl.when(i == S // t - 1)
        def _():
            cols = pl.ds(pl.multiple_of(p * PAIR, PAIR), PAIR)
            pltpu.sync_copy(dk_ref, dk_hbm.at[:, cols])
            pltpu.sync_copy(dv_ref, dv_hbm.at[:, cols])

        pl.when((p == N_PAIRS - 1) & (i == S // t - 1))(plan.finish)

    once = lambda cb: pl.BlockSpec((S, PAIR), cb, pipeline_mode=pl.Buffered(1))
    return pl.pallas_call(
        body, name="attn_bwd", grid=(N_PAIRS, S // t),
        in_specs=[pl.BlockSpec((t, PAIR), lambda p, i: (i, p)),
                  once(lambda p, i: (0, N_PAIRS + p)), once(lambda p, i: (0, 2 * N_PAIRS + p)),
                  pl.BlockSpec((t, PAIR), lambda p, i: (i, p))] + [_hbm()] * ns,
        out_specs=[pl.BlockSpec((t, PAIR), lambda p, i: (i, p)), _hbm(), _hbm()] + [_hbm()] * ns,
        out_shape=[jax.ShapeDtypeStruct((S, N_PAIRS * PAIR), F32)] * 3
        + [jax.ShapeDtypeStruct(pt.shape, pt.dtype) for pt in parts],
        scratch_shapes=[pltpu.VMEM((2, nk, t, tk), F32), pltpu.VMEM((2, nk, t, tk), F32),
                        pltpu.VMEM((2, t, 1), F32), pltpu.VMEM((t, PAIR), F32),
                        pltpu.VMEM((S, PAIR), F32), pltpu.VMEM((S, PAIR), F32),
                        pltpu.VMEM((2, 2, t, tk), F32), pltpu.VMEM((2, 2, t, tk), F32)] + _scatter_sems(ns),
        compiler_params=_cp(("arbitrary", "arbitrary"), vmem=VMEM_LIMIT_ATTN_BWD),
    )(qkv, qkv, qkv, do, *parts)


def _conv_bwd(uc, yconv, dco, cwf, lg, lb, tm):
    S = uc.shape[0]
    hb = tm // HALO
    nb = S // tm
    ext = tm + HALO

    def body(uc_ref, prev_ref, y_ref, ynext_ref, dco_ref, dnext_ref, cw_ref, lg_ref, lb_ref,
             duc_ref, dcw_ref, dcb_ref, dlg_ref, dlb_ref, glu_ref, dyc_ref, shg_ref, shd_ref):
        i = pl.program_id(0)
        last = i == nb - 1

        @pl.when(i == 0)
        def _():
            for ref in (dcw_ref, dcb_ref, dlg_ref, dlb_ref):
                ref[...] = jnp.zeros_like(ref)

        uc = uc_ref[...]
        glu_ref[0:HALO, :] = jnp.where(i == 0, 0.0, _glu(prev_ref[...]))
        glu_ref[HALO:ext, :] = _glu(uc)
        glu_ref[ext:ext + SUBLANES, :] = jnp.zeros((SUBLANES, CONV_CH), F32)
        _shift_copies(glu_ref, shg_ref)
        fwd_offs = [HALO - (CONV_WIDTH - 1) + w for w in range(CONV_WIDTH)]
        y = jnp.concatenate([y_ref[...], ynext_ref[...]], axis=0)
        mu = jnp.mean(y, axis=-1, keepdims=True)
        yc = y - mu
        rstd = lax.rsqrt(jnp.mean(yc * yc, axis=-1, keepdims=True) + EPS)
        yhat = yc * rstd
        lg = lg_ref[...]
        ln = yhat * lg + lb_ref[...]
        sg = _sigmoid(ln)
        dout = jnp.concatenate([dco_ref[...], jnp.where(last, 0.0, dnext_ref[...])], axis=0)
        dln = dout * (sg * (1.0 + ln * (1.0 - sg)))
        dyh = dln * lg
        dyc = rstd * (dyh - jnp.mean(dyh, axis=-1, keepdims=True)
                      - yhat * jnp.mean(dyh * yhat, axis=-1, keepdims=True))
        dyc_ref[0:ext, :] = dyc
        dyc_ref[ext:ext + SUBLANES, :] = jnp.zeros((SUBLANES, CONV_CH), F32)
        _shift_copies(dyc_ref, shd_ref)
        dlg_ref[...] += jnp.sum((dln * yhat)[0:tm], axis=0, keepdims=True)
        dlb_ref[...] += jnp.sum(dln[0:tm], axis=0, keepdims=True)
        dcb_ref[...] += jnp.sum(dyc[0:tm], axis=0, keepdims=True)
        dglu = _conv_taps(cw_ref, dyc_ref, shd_ref, [CONV_WIDTH - 1 - w for w in range(CONV_WIDTH)], tm)
        d0 = dyc[0:tm]
        for w, off in enumerate(fwd_offs):
            dcw_ref[w:w + 1, :] += jnp.sum(d0 * _rows_at(glu_ref, shg_ref, off, tm), axis=0, keepdims=True)
        val, gate = uc[:, :CONV_CH], uc[:, CONV_CH:]
        sgate = _sigmoid(gate)
        duc_ref[:, :CONV_CH] = (dglu * sgate).astype(BF16)
        duc_ref[:, CONV_CH:] = (dglu * val * sgate * (1.0 - sgate)).astype(BF16)

    vec = pl.BlockSpec((1, CONV_CH), lambda i: (0, 0))
    nxt = lambda i: (jnp.minimum((i + 1) * hb, S // HALO - 1), 0)
    return pl.pallas_call(
        body, name="conv_bwd", grid=(nb,),
        in_specs=[pl.BlockSpec((tm, 2 * CONV_CH), lambda i: (i, 0)),
                  pl.BlockSpec((HALO, 2 * CONV_CH), lambda i: (jnp.maximum(i * hb - 1, 0), 0)),
                  pl.BlockSpec((tm, CONV_CH), lambda i: (i, 0)), pl.BlockSpec((HALO, CONV_CH), nxt),
                  pl.BlockSpec((tm, CONV_CH), lambda i: (i, 0)), pl.BlockSpec((HALO, CONV_CH), nxt),
                  _const_spec(cwf.shape), _const_spec((1, CONV_CH)), _const_spec((1, CONV_CH))],
        out_specs=[pl.BlockSpec((tm, 2 * CONV_CH), lambda i: (i, 0)),
                   pl.BlockSpec(cwf.shape, lambda i: (0, 0)), vec, vec, vec],
        out_shape=[jax.ShapeDtypeStruct((S, 2 * CONV_CH), BF16), jax.ShapeDtypeStruct(cwf.shape, F32)]
        + [jax.ShapeDtypeStruct((1, CONV_CH), F32)] * 3,
        scratch_shapes=[pltpu.VMEM((ext + SUBLANES, CONV_CH), F32), pltpu.VMEM((ext + SUBLANES, CONV_CH), F32),
                        pltpu.VMEM((SUBLANES - 1, ext, CONV_CH), F32),
                        pltpu.VMEM((SUBLANES - 1, ext, CONV_CH), F32)],
        compiler_params=_cp(("arbitrary",)),
    )(uc, uc, yconv, yconv, dco, dco, cwf, lg, lb)


def _in_bwd(duc, dq, dk, dv, x2, dh1, g1, wa, tm):
    S = x2.shape[0]

    def body(duc_ref, dq_ref, dk_ref, dv_ref, x_ref, dh1_ref, g_ref, w_ref, gx_ref, du_ref, dg_ref):
        i = pl.program_id(0)
        du = jnp.concatenate([duc_ref[...], dq_ref[...].astype(BF16), dk_ref[...].astype(BF16),
                              dv_ref[...].astype(BF16)], axis=1)
        du_ref[...] = du
        da = _dot_nt(du[:, 0:IN_SH], w_ref[0])
        for j in range(1, N_CHIPS):
            da = da + _dot_nt(du[:, IN_SH * j:IN_SH * (j + 1)], w_ref[j])
        x = x_ref[...]
        r = _rms(x)
        dx, dg = _rms_bwd(da, x * r, r, g_ref[...])
        gx_ref[...] = dh1_ref[...] + dx

        @pl.when(i == 0)
        def _():
            dg_ref[...] = jnp.zeros_like(dg_ref)

        dg_ref[...] += jnp.sum(dg, axis=0, keepdims=True)

    row = lambda w: pl.BlockSpec((tm, w), lambda i: (i, 0))
    return pl.pallas_call(
        body, name="in_bwd", grid=(S // tm,),
        in_specs=[row(2 * CONV_CH), row(CONV_CH), row(CONV_CH), row(CONV_CH), row(D_MODEL), row(D_MODEL),
                  _const_spec((1, D_MODEL)), _const_spec(wa.shape)],
        out_specs=[pl.BlockSpec((None, tm, D_MODEL), lambda i: (0, i, 0)), row(2560),
                   pl.BlockSpec((1, D_MODEL), lambda i: (0, 0))],
        out_shape=[jax.ShapeDtypeStruct((1, S, D_MODEL), F32), jax.ShapeDtypeStruct((S, 2560), BF16),
                   jax.ShapeDtypeStruct((1, D_MODEL), F32)],
        compiler_params=_cp(("arbitrary",)),
    )(duc, dq, dk, dv, x2, dh1, g1, wa)


def _matmul_tn(xm, ym, tm, ts, name, column_block=None):
    S, M = xm.shape
    N = ym.shape[1]

    def body(x_ref, y_ref, o_ref):
        @pl.when(pl.program_id(1) == 0)
        def _():
            o_ref[...] = jnp.zeros_like(o_ref)

        xt = x_ref[...].T
        if column_block is None:
            o_ref[...] += _dot(xt, y_ref[...])
        else:
            for j in range(N // column_block):
                o_ref[j] += _dot(xt, y_ref[:, column_block * j:column_block * (j + 1)])

    if column_block is None:
        out_spec = pl.BlockSpec((tm, N), lambda m, s: (m, 0))
        out_shape = jax.ShapeDtypeStruct((M, N), F32)
    else:
        out_spec = pl.BlockSpec((N // column_block, tm, column_block), lambda m, s: (0, m, 0))
        out_shape = jax.ShapeDtypeStruct((N // column_block, M, column_block), F32)
    return pl.pallas_call(
        body, name=name, grid=(M // tm, S // ts),
        in_specs=[pl.BlockSpec((ts, tm), lambda m, s: (s, m)), pl.BlockSpec((ts, N), lambda m, s: (s, 0))],
        out_specs=out_spec, out_shape=out_shape,
        compiler_params=_cp(("parallel", "arbitrary")),
    )(xm, ym)


def _sibling_halves(grads, name):
    n = len(grads)

    def body(*refs):
        ins, outs, ssem, rsem = refs[:n], refs[n:2 * n], refs[2 * n], refs[2 * n + 1]
        x, y, c = lax.axis_index("x"), lax.axis_index("y"), lax.axis_index("c")
        copies = []
        for k in range(n):
            for j in range(N_CHIPS):
                copies.append(pltpu.make_async_remote_copy(
                    src_ref=ins[k].at[j, 1 - c], dst_ref=outs[k].at[j],
                    send_sem=ssem.at[N_CHIPS * k + j], recv_sem=rsem.at[N_CHIPS * k + j],
                    device_id=(x, y, 1 - c), device_id_type=MESH))
        for cp in copies:
            cp.start()
        for cp in copies:
            cp.wait()

    shapes = [jax.ShapeDtypeStruct((g.shape[0],) + g.shape[2:], F32) for g in grads]
    return pl.pallas_call(
        body, name=name, out_shape=shapes,
        in_specs=[_hbm()] * n, out_specs=[_hbm()] * n,
        scratch_shapes=[pltpu.SemaphoreType.DMA((N_CHIPS * n,)), pltpu.SemaphoreType.DMA((N_CHIPS * n,))],
    )(*grads)


def _add_half(c_arr, g, landed, name):
    def body(c_ref, g_ref, l_ref, o_ref):
        o_ref[...] = (g_ref[...] + l_ref[...]).astype(BF16)

    rows, n = g.shape[2], g.shape[3]
    grid = (N_CHIPS,)
    g_spec = pl.BlockSpec((None, None, rows, n), lambda j, c: (j, c[0], 0, 0))
    l_spec = pl.BlockSpec((None, rows, n), lambda j, c: (j, 0, 0))
    return pl.pallas_call(
        body, name=name,
        grid_spec=pltpu.PrefetchScalarGridSpec(num_scalar_prefetch=1, grid=grid, in_specs=[g_spec, l_spec],
                                               out_specs=l_spec),
        out_shape=jax.ShapeDtypeStruct(landed.shape, BF16),
        compiler_params=_cp(("parallel",)),
    )(c_arr, g, landed)


class _ScatterPlan:
    def __init__(self, ins, outs, lsem, ssem, rsem):
        x, y, c = lax.axis_index("x"), lax.axis_index("y"), lax.axis_index("c")
        me = 2 * x + y
        self.copies = []
        for k in range(len(ins)):
            self.copies.append(pltpu.make_async_copy(ins[k].at[me], outs[k].at[me], lsem.at[k]))
            for r, chip in enumerate([(1 - x, y), (x, 1 - y), (1 - x, 1 - y)]):
                self.copies.append(pltpu.make_async_remote_copy(
                    src_ref=ins[k].at[2 * chip[0] + chip[1]], dst_ref=outs[k].at[me],
                    send_sem=ssem.at[3 * k + r], recv_sem=rsem.at[3 * k + r],
                    device_id=(chip[0], chip[1], c), device_id_type=MESH))

    def start(self):
        for cp in self.copies:
            cp.start()

    def finish(self):
        for cp in self.copies:
            cp.wait()


def _scatter_sems(n):
    return [pltpu.SemaphoreType.DMA((n,)), pltpu.SemaphoreType.DMA((3 * n,)), pltpu.SemaphoreType.DMA((3 * n,))]


def _chip_scatter(parts):
    n = len(parts)

    def body(*refs):
        plan = _ScatterPlan(refs[:n], refs[n:2 * n], *refs[2 * n:])
        plan.start()
        plan.finish()

    shapes = [jax.ShapeDtypeStruct(p.shape, p.dtype) for p in parts]
    return pl.pallas_call(
        body, name="grad_chip_scatter", out_shape=shapes,
        in_specs=[_hbm()] * n, out_specs=[_hbm()] * n, scratch_shapes=_scatter_sems(n),
    )(*parts)


def _sum_chips(landed, name):
    _, rows, n = landed.shape
    tr = 256 if rows % 256 == 0 else rows

    def body(a_ref, b_ref, c_ref, d_ref, o_ref):
        f = lambda ref: ref[...].astype(F32)
        o_ref[...] = ((f(a_ref) + f(b_ref)) + f(c_ref)) + f(d_ref)

    specs = [pl.BlockSpec((None, tr, n), functools.partial(lambda i, j: (j, i, 0), j=j)) for j in range(N_CHIPS)]
    return pl.pallas_call(
        body, name=name, grid=(rows // tr,), in_specs=specs,
        out_specs=pl.BlockSpec((tr, n), lambda i: (i, 0)),
        out_shape=jax.ShapeDtypeStruct((rows, n), F32),
        compiler_params=_cp(("parallel",)),
    )(landed, landed, landed, landed)


def _share_halves(halves):
    n = len(halves)

    def body(*refs):
        ins, outs = refs[:n], refs[n:2 * n]
        ssem, rsem = refs[2 * n:]
        x, y, c = lax.axis_index("x"), lax.axis_index("y"), lax.axis_index("c")
        copies = [pltpu.make_async_remote_copy(
            src_ref=ins[k], dst_ref=outs[k], send_sem=ssem.at[k], recv_sem=rsem.at[k],
            device_id=(x, y, 1 - c), device_id_type=MESH) for k in range(n)]
        for cp in copies:
            cp.start()
        for cp in copies:
            cp.wait()

    shapes = [jax.ShapeDtypeStruct(h.shape, F32) for h in halves]
    return pl.pallas_call(
        body, name="grad_share_halves", out_shape=shapes,
        in_specs=[_hbm()] * n, out_specs=[_hbm()] * n,
        scratch_shapes=[pltpu.SemaphoreType.DMA((n,)), pltpu.SemaphoreType.DMA((n,))],
    )(*halves)


def _allreduce_small(packed):
    rows, n = packed.shape

    def body(in_ref, out_ref, land_ref, ssem, rsem):
        x, y, c = lax.axis_index("x"), lax.axis_index("y"), lax.axis_index("c")
        me = 4 * x + 2 * y + c
        land_ref[me] = in_ref[...]
        copies = []
        for r in range(1, 8):
            tx = 1 - x if r & 4 else x
            ty = 1 - y if r & 2 else y
            tc = 1 - c if r & 1 else c
            cp = pltpu.make_async_remote_copy(
                src_ref=in_ref, dst_ref=land_ref.at[me], send_sem=ssem.at[r - 1], recv_sem=rsem.at[r - 1],
                device_id=(tx, ty, tc), device_id_type=MESH)
            cp.start()
            copies.append(cp)
        for cp in copies:
            cp.wait()
        acc = land_ref[0]
        for k in range(1, 8):
            acc = acc + land_ref[k]
        out_ref[...] = acc

    return pl.pallas_call(
        body, name="allreduce_small", out_shape=jax.ShapeDtypeStruct((rows, n), F32),
        in_specs=[pl.BlockSpec(memory_space=pltpu.VMEM)], out_specs=pl.BlockSpec(memory_space=pltpu.VMEM),
        scratch_shapes=[pltpu.VMEM((8, rows, n), F32), pltpu.SemaphoreType.DMA((7,)),
                        pltpu.SemaphoreType.DMA((7,))],
    )(packed)


def _adamw_math(w, g, m, v):
    m = ADAM_B1 * m + (1.0 - ADAM_B1) * g
    v = ADAM_B2 * v + (1.0 - ADAM_B2) * (g * g)
    m_hat = m / (1.0 - ADAM_B1 ** ADAM_STEP)
    v_hat = v / (1.0 - ADAM_B2 ** ADAM_STEP)
    return -ADAM_LR * (m_hat / (jnp.sqrt(v_hat) + ADAM_EPS) + ADAM_WD * w), m, v


def _adamw_halves(c_arr, w, mine, other, m, v, name):
    rows, n = mine.shape
    tr = 256 if rows % 256 == 0 else rows
    nb = rows // tr

    def body(c_ref, w_ref, a_ref, b_ref, m_ref, v_ref, g_ref, d_ref, mo_ref, vo_ref):
        g = jnp.where(pl.program_id(0) == c_ref[0], a_ref[...], b_ref[...])
        g_ref[...] = g
        d_ref[...], mo_ref[...], vo_ref[...] = _adamw_math(w_ref[...], g, m_ref[...], v_ref[...])

    full = pl.BlockSpec((None, tr, n), lambda h, i, c: (0, h * nb + i, 0))
    half = pl.BlockSpec((tr, n), lambda h, i, c: (i, 0))
    return pl.pallas_call(
        body, name=name,
        grid_spec=pltpu.PrefetchScalarGridSpec(num_scalar_prefetch=1, grid=(2, nb),
                                               in_specs=[full, half, half, full, full], out_specs=[full] * 4),
        out_shape=[jax.ShapeDtypeStruct((1, 2 * rows, n), F32)] * 4,
        compiler_params=_cp(("parallel", "parallel")),
    )(c_arr, w, mine, other, m, v)


def _adamw(w, g, m, v, name):
    rows, n = w.shape
    tr = 256 if rows % 256 == 0 else rows

    def body(w_ref, g_ref, m_ref, v_ref, d_ref, mo_ref, vo_ref):
        d_ref[...], mo_ref[...], vo_ref[...] = _adamw_math(w_ref[...], g_ref[...], m_ref[...], v_ref[...])

    spec = pl.BlockSpec((tr, n), lambda i: (i, 0))
    return pl.pallas_call(
        body, name=name, grid=(rows // tr,), in_specs=[spec] * 4, out_specs=[spec] * 3,
        out_shape=[jax.ShapeDtypeStruct((rows, n), F32)] * 3,
        compiler_params=_cp(("parallel",)),
    )(w, g, m, v)


def _rows8(a):
    a = a.reshape(-1, 128)
    return jnp.pad(a, ((0, (-a.shape[0]) % 8), (0, 0)))


def kernel(x, g_pre_mix, w_in, conv_w, conv_b, conv_ln_g, conv_ln_b, attn_norm_g, w_out, g_post_mix, g_pre_ffn, w_gate, w_up, w_down, g_post_ffn, loss_target, m_g_pre_mix, m_w_in, m_conv_w, m_conv_b, m_conv_ln_g, m_conv_ln_b, m_attn_norm_g, m_w_out, m_g_post_mix, m_g_pre_ffn, m_w_gate, m_w_up, m_w_down, m_g_post_ffn, v_g_pre_mix, v_w_in, v_conv_w, v_conv_b, v_conv_ln_g, v_conv_ln_b, v_attn_norm_g, v_w_out, v_g_post_mix, v_g_pre_ffn, v_w_gate, v_w_up, v_w_down, v_g_post_ffn):
    S = x.shape[1]
    tm_big = min(512, S)
    tm_ffn = min(256, S)
    tk_att = min(1024, S)
    t_att_fwd = min(1024, S)
    t_att_bwd = tk_att // KEY_BLOCKS
    chip = 2 * lax.axis_index("x") + lax.axis_index("y")
    core = lax.axis_index("c")
    x2 = x.reshape(S, D_MODEL)
    tgt = loss_target.reshape(S, D_MODEL)
    ag = attn_norm_g.reshape(1, CONV_CH)

    a_sh = w_in[0].astype(BF16)
    b_sh = jnp.stack([w_gate[0], w_up[0]]).astype(BF16)
    c_sh = jnp.concatenate([w_out[0], w_down[0]], axis=0).astype(BF16)
    cw_sh = jnp.pad(conv_w[0, :, 0, :], ((0, 1), (0, 0)))
    own = lambda full, shard: lax.dynamic_update_index_in_dim(full, shard, chip, 0)
    cols = lambda w4: jnp.transpose(w4, (1, 0, 2)).reshape(w4.shape[1], N_CHIPS * w4.shape[2])
    wa4, cw4 = _gather_weights([a_sh, cw_sh], [False, False])
    wa = own(wa4, a_sh)
    cwf = cols(own(cw4, cw_sh))

    a_bf, uc, qkv = _in_proj(x2, g_pre_mix, wa, tm_big)
    conv_out, yconv = _conv_fwd(uc, cwf, conv_b, conv_ln_g, conv_ln_b, tm_big)
    o, wb4, wc4 = _attn_fwd(qkv, t_att_fwd, tk_att, [b_sh, c_sh], [True, False])
    wb4, wc4 = own(wb4, b_sh), own(wc4, c_sh)
    wg, wu = cols(wb4[:, 0]), cols(wb4[:, 1])
    wo = wc4[:, :OUT_SH].reshape(D_MODEL, D_MODEL)
    wd = wc4[:, OUT_SH:].reshape(D_FF, D_MODEL)
    mixed, yv, h1, f_in = _out_proj(conv_out, o, ag, wo, x2, g_post_mix, g_pre_ffn, tm_big)
    df, dh2, dg4, loss_part, gt_bf, up_bf, act = _ffn_fwd(f_in, h1, tgt, wg, wu, wd, g_post_ffn, tm_ffn)

    dgt, dup, dh1, dy, dg3, dg2 = _ffn_bwd(gt_bf, up_bf, df, dh2, h1, yv, wg, wu, wd, g_pre_ffn, g_post_mix, tm_ffn)
    dco, do, dag = _out_bwd(dy, o, ag, wo, tm_big)
    ts = min(512, S)
    gw_out = _matmul_tn(mixed, dy, D_MODEL, ts, "grad_w_out")
    gw_gate = _matmul_tn(f_in, dgt, D_MODEL, ts, "grad_w_gate")
    gw_up = _matmul_tn(f_in, dup, D_MODEL, ts, "grad_w_up")
    gw_down = _matmul_tn(act, df, D_FF // 2, ts, "grad_w_down")

    by_cols = lambda g: jnp.transpose(g.reshape(2, D_MODEL // 2, N_CHIPS, -1), (2, 0, 1, 3))
    by_rows = lambda g: g.reshape(N_CHIPS, 2, g.shape[0] // (2 * N_CHIPS), g.shape[1])
    c_arr = core.reshape(1).astype(jnp.int32)

    def chip_partials(views, nms):
        landed = _sibling_halves(views, "grad_sibling_halves_" + nms[0])
        return [_add_half(c_arr, g, l, "grad_half_" + nm) for g, l, nm in zip(views, landed, nms)]

    early = ["w_gate", "w_up", "w_out", "w_down"]
    parts = chip_partials([by_cols(gw_gate), by_cols(gw_up), by_rows(gw_out), by_rows(gw_down)], early)
    dq, dk, dv, *slots = _attn_bwd(qkv, do, t_att_bwd, tk_att, parts)
    duc, dcw, dcb, dlg, dlb = _conv_bwd(uc, yconv, dco, cwf, conv_ln_g, conv_ln_b, tm_big)
    grad_x, du, dg1 = _in_bwd(duc, dq, dk, dv, x2, dh1, g_pre_mix, wa, tm_big)
    gw_in = _matmul_tn(a_bf, du, D_MODEL, ts, "grad_w_in", column_block=IN_SH)
    slots += _chip_scatter(chip_partials([gw_in.reshape(N_CHIPS, 2, D_MODEL // 2, IN_SH)], ["w_in"]))
    names = early + ["w_in"]
    halves = [_sum_chips(s, "grad_sum_" + nm) for s, nm in zip(slots, names)]
    others = _share_halves(halves)
    mine = dict(zip(names, halves))
    other = dict(zip(names, others))

    small = [dg1, dcb, dlg, dlb, dag, dg2, dg3, dg4]
    packed = jnp.concatenate([_rows8(s) for s in small] + [_rows8(dcw), _rows8(loss_part)], axis=0)
    red = _allreduce_small(packed)
    sizes = [D_MODEL, CONV_CH, CONV_CH, CONV_CH, CONV_CH, D_MODEL, D_MODEL, D_MODEL]
    g_small = [red[8 * k:8 * k + n // 128].reshape(1, n) for k, n in enumerate(sizes)]
    cw_red = red[64:64 + 128].reshape(HALO, CONV_CH)
    g_cw = lax.dynamic_slice(cw_red, (0, chip * 128), (HALO, 128))
    loss = red[192, 0]

    big = []
    for w, m, v, nm in [(w_in, m_w_in, v_w_in, "w_in"), (w_out, m_w_out, v_w_out, "w_out"),
                        (w_gate, m_w_gate, v_w_gate, "w_gate"), (w_up, m_w_up, v_w_up, "w_up"),
                        (w_down, m_w_down, v_w_down, "w_down")]:
        big.append(_adamw_halves(c_arr, w, mine[nm], other[nm], m, v, "adamw_" + nm))
    sm_w = [g_pre_mix, conv_b, conv_ln_g, conv_ln_b, ag, g_post_mix, g_pre_ffn, g_post_ffn]
    sm_m = [m_g_pre_mix, m_conv_b, m_conv_ln_g, m_conv_ln_b, m_attn_norm_g, m_g_post_mix, m_g_pre_ffn, m_g_post_ffn]
    sm_v = [v_g_pre_mix, v_conv_b, v_conv_ln_g, v_conv_ln_b, v_attn_norm_g, v_g_post_mix, v_g_pre_ffn, v_g_post_ffn]
    pad_cw = lambda a: jnp.pad(a[0, :, 0, :], ((0, 1), (0, 0)))

    def pack(vecs, cw):
        return jnp.concatenate([_rows8(a) for a in vecs] + [cw], axis=0)

    sd, smn, svn = _adamw(pack(sm_w, pad_cw(conv_w)), pack(g_small, g_cw), pack(sm_m, pad_cw(m_conv_w)),
                          pack(sm_v, pad_cw(v_conv_w)), "adamw_small")

    def unpack(p):
        vecs = [p[8 * k:8 * k + n // 128].reshape(1, n) for k, n in enumerate(sizes)]
        return vecs, p[64:64 + CONV_WIDTH].reshape(1, CONV_WIDTH, 1, 128)

    def ordered(vecs, cw, w_in_, w_out_, w_gate_, w_up_, w_down_):
        g1_, cb_, lg_, lb_, ag_, g2_, g3_, g4_ = vecs
        return [g1_, w_in_, cw, cb_, lg_, lb_, ag_.reshape(1, 8, HEAD_DIM), w_out_, g2_, g3_,
                w_gate_, w_up_, w_down_, g4_]

    grads = ordered(g_small, g_cw[:CONV_WIDTH].reshape(1, CONV_WIDTH, 1, 128), *[b[0] for b in big])
    outs = []
    for idx, p in enumerate((sd, smn, svn)):
        vecs, cw = unpack(p)
        outs += ordered(vecs, cw, *[b[idx + 1] for b in big])
    return (loss, grad_x, *grads, *outs)
```

```python
import functools

import jax
import jax.numpy as jnp
from jax import lax
from jax.experimental import pallas as pl
from jax.experimental.pallas import tpu as pltpu

F32 = jnp.float32
BF16 = jnp.bfloat16
MESH = pl.DeviceIdType.MESH

D_MODEL = 1024
CONV_CH = 512
CONV_WIDTH = 31
HEAD_DIM = 64
PAIR = 2 * HEAD_DIM
N_PAIRS = 4
D_FF = 2816
N_CHIPS = 4
IN_SH = 2560 // N_CHIPS
OUT_SH = D_MODEL // N_CHIPS
EPS = 1e-6
HALO = 32

ADAM_LR = 0.001
ADAM_B1 = 0.9
ADAM_B2 = 0.999
ADAM_EPS = 1e-08
ADAM_WD = 0.01
ADAM_STEP = 10

VMEM_LIMIT = 56 * 2 ** 20
VMEM_LIMIT_ATTN_BWD = 62 * 2 ** 20


def _cp(sem=None, vmem=VMEM_LIMIT):
    return pltpu.CompilerParams(dimension_semantics=sem, vmem_limit_bytes=vmem)


def _hbm():
    return pl.BlockSpec(memory_space=pltpu.HBM)


def _const_spec(shape):
    nd = len(shape)
    return pl.BlockSpec(shape, lambda *_: (0,) * nd, pipeline_mode=pl.Buffered(1))


def _dot(a, b):
    return jnp.dot(a, b, preferred_element_type=F32)


def _dot_nt(a, b):
    return lax.dot_general(a, b, (((1,), (1,)), ((), ())), preferred_element_type=F32)


def _dot_tn(a, b):
    return lax.dot_general(a, b, (((0,), (0,)), ((), ())), preferred_element_type=F32)


def _split3(x):
    b0 = x.astype(BF16)
    r1 = x - b0.astype(F32)
    b1 = r1.astype(BF16)
    b2 = (r1 - b1.astype(F32)).astype(BF16)
    return b0, b1, b2


def _sigmoid(x):
    return 1.0 / (1.0 + jnp.exp(-x))


def _head_mean(x, seg):
    b0, b1, b2 = _split3(x)
    return (_dot(b0, seg) + _dot(b1, seg) + _dot(b2, seg)) * (1.0 / HEAD_DIM)


def _seg_matrix(n):
    r = lax.broadcasted_iota(jnp.int32, (n, n), 0) // HEAD_DIM
    c = lax.broadcasted_iota(jnp.int32, (n, n), 1) // HEAD_DIM
    return (r == c).astype(BF16)


def _rms(x):
    return lax.rsqrt(jnp.mean(x * x, axis=-1, keepdims=True) + EPS)


def _rms_bwd(dy, n, r, g):
    dn = dy * g
    dx = r * (dn - n * jnp.mean(dn * n, axis=-1, keepdims=True))
    return dx, dy * n


class _GatherPlan:
    def __init__(self, srcs, outs, lead, ssem, rsem):
        self.srcs, self.outs, self.lead, self.ssem, self.rsem = srcs, outs, lead, ssem, rsem
        x, y, self.c = lax.axis_index("x"), lax.axis_index("y"), lax.axis_index("c")
        self.me = 2 * x + y
        self.sibling = (x, y, 1 - self.c)
        self.chips = [(1 - x, y), (x, 1 - y), (1 - x, 1 - y)]

    def _half(self, ref, i, h):
        if self.lead[i]:
            return ref.at[h]
        rows = ref.shape[0] // 2
        return ref.at[pl.ds(h * rows, rows)]

    def _ici(self, i, k, origin):
        return pltpu.make_async_remote_copy(
            src_ref=self._half(self.srcs[i], i, self.c), dst_ref=self._half(self.outs[i].at[origin], i, self.c),
            send_sem=self.ssem.at[6 * i + k], recv_sem=self.rsem.at[6 * i + k],
            device_id=(self.chips[k][0], self.chips[k][1], self.c), device_id_type=MESH)

    def _d2d(self, i, k, h):
        origin = 2 * self.chips[k][0] + self.chips[k][1]
        piece = self._half(self.outs[i].at[origin], i, h)
        return pltpu.make_async_remote_copy(
            src_ref=piece, dst_ref=piece, send_sem=self.ssem.at[6 * i + 3 + k],
            recv_sem=self.rsem.at[6 * i + 3 + k], device_id=self.sibling, device_id_type=MESH)

    def _each(self):
        return [(i, k) for i in range(len(self.srcs)) for k in range(3)]

    def start(self):
        for i, k in self._each():
            self._ici(i, k, self.me).start()

    def forward(self):
        for i, k in self._each():
            self._ici(i, k, 2 * self.chips[k][0] + self.chips[k][1]).wait_recv()
            self._d2d(i, k, self.c).start()

    def finish(self):
        for i, k in self._each():
            self._d2d(i, k, 1 - self.c).wait_recv()
        for i, k in self._each():
            self._ici(i, k, self.me).wait_send()
            self._d2d(i, k, self.c).wait_send()


def _gather_shapes(shards):
    return [jax.ShapeDtypeStruct((N_CHIPS,) + s.shape, s.dtype) for s in shards]


def _gather_weights(shards, lead):
    n = len(shards)

    def body(*refs):
        plan = _GatherPlan(refs[:n], refs[n:2 * n], lead, refs[2 * n], refs[2 * n + 1])
        plan.start()
        plan.forward()
        plan.finish()

    return pl.pallas_call(
        body, name="gather_weights", out_shape=_gather_shapes(shards),
        in_specs=[_hbm()] * n, out_specs=[_hbm()] * n,
        scratch_shapes=[pltpu.SemaphoreType.DMA((6 * n,)), pltpu.SemaphoreType.DMA((6 * n,))],
    )(*shards)


def _in_proj(x2, g1, wa, tm):
    S = x2.shape[0]

    def body(x_ref, g_ref, w_ref, a_ref, uc_ref, qkv_ref):
        x = x_ref[...]
        a = (x * _rms(x) * g_ref[...]).astype(BF16)
        a_ref[...] = a
        u = [_dot(a, w_ref[j]) for j in range(N_CHIPS)]
        uc_ref[:, 0:640] = u[0]
        uc_ref[:, 640:1024] = u[1][:, 0:384]
        qkv_ref[:, 0:256] = u[1][:, 384:640].astype(BF16)
        qkv_ref[:, 256:896] = u[2].astype(BF16)
        qkv_ref[:, 896:1536] = u[3].astype(BF16)

    return pl.pallas_call(
        body, name="in_proj", grid=(S // tm,),
        in_specs=[pl.BlockSpec((tm, D_MODEL), lambda i: (i, 0)), _const_spec((1, D_MODEL)),
                  _const_spec(wa.shape)],
        out_specs=[pl.BlockSpec((tm, D_MODEL), lambda i: (i, 0)),
                   pl.BlockSpec((tm, 2 * CONV_CH), lambda i: (i, 0)),
                   pl.BlockSpec((tm, 1536), lambda i: (i, 0))],
        out_shape=[jax.ShapeDtypeStruct((S, D_MODEL), BF16), jax.ShapeDtypeStruct((S, 2 * CONV_CH), F32),
                   jax.ShapeDtypeStruct((S, 1536), BF16)],
        compiler_params=_cp(("parallel",)),
    )(x2, g1, wa)


SUBLANES = 8


def _shift_copies(src_ref, sh_ref):
    rows = sh_ref.shape[1]
    for b in range(1, SUBLANES):
        sh_ref[b - 1] = src_ref[pl.ds(b, rows), :]


def _rows_at(src_ref, sh_ref, off, rows):
    a, b = divmod(off, SUBLANES)
    if b == 0:
        return src_ref[pl.ds(SUBLANES * a, rows), :]
    return sh_ref[b - 1, pl.ds(SUBLANES * a, rows), :]


def _conv_taps(cw_ref, src_ref, sh_ref, offs, rows):
    acc = None
    for w, off in enumerate(offs):
        term = cw_ref[w:w + 1, :] * _rows_at(src_ref, sh_ref, off, rows)
        acc = term if acc is None else acc + term
    return acc


def _glu(uc):
    return uc[:, :CONV_CH] * _sigmoid(uc[:, CONV_CH:])


def _conv_fwd(uc, cwf, cb, lg, lb, tm):
    S = uc.shape[0]
    hb = tm // HALO

    def body(uc_ref, prev_ref, cw_ref, cb_ref, lg_ref, lb_ref, out_ref, y_ref, glu_ref, sh_ref):
        i = pl.program_id(0)
        glu_ref[0:HALO, :] = jnp.where(i == 0, 0.0, _glu(prev_ref[...]))
        glu_ref[HALO:HALO + tm, :] = _glu(uc_ref[...])
        glu_ref[HALO + tm:HALO + tm + SUBLANES, :] = jnp.zeros((SUBLANES, CONV_CH), F32)
        _shift_copies(glu_ref, sh_ref)
        offs = [HALO - (CONV_WIDTH - 1) + w for w in range(CONV_WIDTH)]
        y = _conv_taps(cw_ref, glu_ref, sh_ref, offs, tm) + cb_ref[...]
        y_ref[...] = y
        mu = jnp.mean(y, axis=-1, keepdims=True)
        yc = y - mu
        rstd = lax.rsqrt(jnp.mean(yc * yc, axis=-1, keepdims=True) + EPS)
        ln = yc * rstd * lg_ref[...] + lb_ref[...]
        out_ref[...] = (ln * _sigmoid(ln)).astype(BF16)

    return pl.pallas_call(
        body, name="conv_fwd", grid=(S // tm,),
        in_specs=[pl.BlockSpec((tm, 2 * CONV_CH), lambda i: (i, 0)),
                  pl.BlockSpec((HALO, 2 * CONV_CH), lambda i: (jnp.maximum(i * hb - 1, 0), 0)),
                  _const_spec(cwf.shape), _const_spec((1, CONV_CH)), _const_spec((1, CONV_CH)),
                  _const_spec((1, CONV_CH))],
        out_specs=[pl.BlockSpec((tm, CONV_CH), lambda i: (i, 0))] * 2,
        out_shape=[jax.ShapeDtypeStruct((S, CONV_CH), BF16), jax.ShapeDtypeStruct((S, CONV_CH), F32)],
        scratch_shapes=[pltpu.VMEM((HALO + tm + SUBLANES, CONV_CH), F32),
                        pltpu.VMEM((SUBLANES - 1, HALO + tm, CONV_CH), F32)],
        compiler_params=_cp(("parallel",)),
    )(uc, uc, cwf, cb, lg, lb)


def _lane_mask(h):
    lane = lax.broadcasted_iota(jnp.int32, (1, PAIR), 1)
    return (lane >= HEAD_DIM * h) & (lane < HEAD_DIM * (h + 1))


def _neg_abs(x):
    bits = lax.bitcast_convert_type(x, jnp.uint32) | jnp.uint32(0x80000000)
    return lax.bitcast_convert_type(bits, F32)


def _tri_dot(x, m):
    return _dot(x.astype(BF16), m)


MASKED = -1e30
KEY_BLOCKS = 4


def _running_sums(x, m, reverse, start=None):
    t = m.shape[0]
    blocks = x.shape[1] // t
    order = range(blocks - 1, -1, -1) if reverse else range(blocks)
    out = [None] * blocks
    carry = start
    for b in order:
        xb = x[:, b * t:(b + 1) * t]
        cb = _tri_dot(xb, m)
        out[b] = cb if carry is None else cb + carry
        rs = jnp.sum(xb, axis=1, keepdims=True)
        carry = rs if carry is None else carry + rs
    return jnp.concatenate(out, axis=1), carry


def _sb_tile(z, r, m_suf):
    sp = jnp.maximum(z, 0.0) + jnp.log(1.0 + jnp.exp(_neg_abs(z)))
    c, rs = _running_sums(sp, m_suf, reverse=True, start=r)
    return jnp.exp(z - c), sp, rs


def _scores(qm, kt, mask):
    z = _dot_nt(qm, kt)
    return z if mask is None else jnp.where(mask, z, MASKED)


def _causal_mask(i, sb, t, tk, w=None):
    w = tk if w is None else w
    row = lax.broadcasted_iota(jnp.int32, (t, w), 0) + i * t
    col = lax.broadcasted_iota(jnp.int32, (t, w), 1) + sb * tk
    return col < row


def _sweep_pairs(count, tile):
    def step(n, carry):
        tile(2 * n)
        tile(2 * n + 1)
        return carry

    lax.fori_loop(0, count // 2, step, 0)
    pl.when(lax.rem(count, 2) == 1)(lambda: tile(count - 1))


def _sweep(first, count, down, fetch, load, work):
    lo, hi = (first - count, first) if down else (first, first + count)
    tile = lambda j: jnp.clip(first - j if down else first + j, lo, hi)
    fetch(first, 0, True)

    def step(n, carry):
        j = 2 * n
        vals = load(0)
        fetch(tile(j + 1), 1, False)
        work(tile(j), vals)
        vals = load(1)
        fetch(tile(j + 2), 0, False)
        work(tile(j + 1), vals)
        return carry

    lax.fori_loop(0, (count + 1) // 2, step, 0)

    @pl.when(lax.rem(count, 2) == 0)
    def _():
        work(tile(count), load(0))


def _suffix_matrix(t, prefix=False):
    row = lax.broadcasted_iota(jnp.int32, (t, t), 0)
    col = lax.broadcasted_iota(jnp.int32, (t, t), 1)
    return ((row <= col) if prefix else (row >= col)).astype(BF16)


def _attn_fwd(qkv, t, tk, shards, lead):
    S = qkv.shape[0]
    assert t == tk, "the forward cuts its diagonal tile by blocks: query block and key tile must match"

    ng = len(shards)
    nq = S // t

    def body(*refs):
        q_ref, k_ref, v_ref = refs[:3]
        o_ref = refs[3 + ng]
        acc_ref, r_ref, z_buf, ssem, rsem = refs[4 + 2 * ng:]
        p = pl.program_id(0)
        i = pl.program_id(1)
        plan = _GatherPlan(refs[3:3 + ng], refs[4 + ng:4 + 2 * ng], lead, ssem, rsem)
        pl.when((p == 0) & (i == 0))(plan.start)
        pl.when((p == 1) & (i == 0))(plan.forward)
        blk = tk // KEY_BLOCKS
        m_suf = _suffix_matrix(blk)
        q = q_ref[...]
        hms = [_lane_mask(h) for h in range(2)]
        qms = [jnp.where(hm, q, 0) * 0.125 for hm in hms]

        def rows(sb, w=tk):
            return pl.ds(pl.multiple_of(sb * tk, tk), w)

        for rb in range(KEY_BLOCKS):
            w = (rb + 1) * blk
            part = slice(rb * blk, w)
            kt = k_ref[rows(i, w), :]
            vt = v_ref[rows(i, w), :]
            row = lax.broadcasted_iota(jnp.int32, (blk, w), 0) + rb * blk
            mask = lax.broadcasted_iota(jnp.int32, (blk, w), 1) < row
            out = jnp.zeros((blk, PAIR), F32)
            for h in range(2):
                a_loc, _, rs = _sb_tile(_scores(qms[h][part], kt, mask), None, m_suf)
                out = out + _dot(a_loc.astype(BF16), jnp.where(hms[h], vt, 0))
                r_ref[h, part] = rs
            acc_ref[part, :] = out

        def fetch(sb, slot, first):
            kt = k_ref[rows(sb), :]
            for h in range(2):
                z_buf[slot, h] = _scores(qms[h], kt, None)

        def load(slot):
            return [z_buf[slot, h] for h in range(2)]

        def work(sb, zs):
            vt = v_ref[rows(sb), :]
            for h in range(2):
                a_loc, _, rs = _sb_tile(zs[h], None, m_suf)
                r = r_ref[h]
                acc_ref[...] += _dot(a_loc.astype(BF16), jnp.where(hms[h], vt, 0)) * jnp.exp(-r)
                r_ref[h] = r + rs

        pl.when(i >= 1)(lambda: _sweep(i - 1, i - 1, True, fetch, load, work))
        o_ref[...] = acc_ref[...]
        pl.when((p == N_PAIRS - 1) & (i == nq - 1))(plan.finish)

    return pl.pallas_call(
        body, name="attn_fwd", grid=(N_PAIRS, nq),
        in_specs=[pl.BlockSpec((t, PAIR), lambda p, i: (i, p)),
                  pl.BlockSpec((S, PAIR), lambda p, i: (0, N_PAIRS + p)),
                  pl.BlockSpec((S, PAIR), lambda p, i: (0, 2 * N_PAIRS + p))] + [_hbm()] * ng,
        out_specs=[pl.BlockSpec((t, PAIR), lambda p, i: (i, p))] + [_hbm()] * ng,
        out_shape=[jax.ShapeDtypeStruct((S, N_PAIRS * PAIR), F32)] + _gather_shapes(shards),
        scratch_shapes=[pltpu.VMEM((t, PAIR), F32), pltpu.VMEM((2, t, 1), F32),
                        pltpu.VMEM((2, 2, t, tk), F32),
                        pltpu.SemaphoreType.DMA((6 * ng,)), pltpu.SemaphoreType.DMA((6 * ng,))],
        compiler_params=_cp(("arbitrary", "arbitrary")),
    )(qkv, qkv, qkv, *shards)


def _out_proj(conv_out, o, ag, wc, x2, g2, g3, tm):
    S = o.shape[0]

    def body(co_ref, o_ref, ag_ref, w_ref, x_ref, g2_ref, g3_ref, mix_ref, y_ref, h1_ref, fin_ref):
        seg = _seg_matrix(CONV_CH)
        o = o_ref[...]
        att = (o * lax.rsqrt(_head_mean(o * o, seg) + EPS) * ag_ref[...]).astype(BF16)
        co = co_ref[...]
        mix_ref[:, :CONV_CH] = co
        mix_ref[:, CONV_CH:] = att
        y = _dot(co, w_ref[0:CONV_CH, :]) + _dot(att, w_ref[CONV_CH:, :])
        y_ref[...] = y
        h1 = x_ref[...] + y * _rms(y) * g2_ref[...]
        h1_ref[...] = h1
        fin_ref[...] = (h1 * _rms(h1) * g3_ref[...]).astype(BF16)

    row = lambda w: pl.BlockSpec((tm, w), lambda i: (i, 0))
    return pl.pallas_call(
        body, name="out_proj", grid=(S // tm,),
        in_specs=[row(CONV_CH), row(CONV_CH), _const_spec((1, CONV_CH)), _const_spec(wc.shape),
                  row(D_MODEL), _const_spec((1, D_MODEL)), _const_spec((1, D_MODEL))],
        out_specs=[row(D_MODEL)] * 4,
        out_shape=[jax.ShapeDtypeStruct((S, D_MODEL), BF16), jax.ShapeDtypeStruct((S, D_MODEL), F32),
                   jax.ShapeDtypeStruct((S, D_MODEL), F32), jax.ShapeDtypeStruct((S, D_MODEL), BF16)],
        compiler_params=_cp(("parallel",)),
    )(conv_out, o, ag, wc, x2, g2, g3)


def _ffn_fwd(f_in, h1, tgt, wg, wu, wd, g4, tm):
    S = f_in.shape[0]

    def body(fin_ref, h1_ref, tgt_ref, wg_ref, wu_ref, wd_ref, g4_ref, df_ref, dh2_ref, dg4_ref, loss_ref,
             gt_ref, up_ref, act_ref):
        i = pl.program_id(0)
        fin = fin_ref[...]
        gt = _dot(fin, wg_ref[...])
        up = _dot(fin, wu_ref[...])
        act = (gt * _sigmoid(gt) * up).astype(BF16)
        gt_ref[...] = gt.astype(BF16)
        up_ref[...] = up.astype(BF16)
        act_ref[...] = act
        f = _dot(act, wd_ref[...])
        r = _rms(f)
        n = f * r
        g4 = g4_ref[...]
        err = h1_ref[...] + n * g4 - tgt_ref[...]
        dh2 = err * (1.0 / D_MODEL)
        dh2_ref[...] = dh2
        df, dg = _rms_bwd(dh2, n, r, g4)
        df_ref[...] = df.astype(BF16)

        @pl.when(i == 0)
        def _():
            dg4_ref[...] = jnp.zeros_like(dg4_ref)
            loss_ref[...] = jnp.zeros_like(loss_ref)

        dg4_ref[...] += jnp.sum(dg, axis=0, keepdims=True)
        part = jnp.sum(jnp.sum(err * err, axis=1, keepdims=True), axis=0, keepdims=True)
        loss_ref[...] += part * (0.5 / D_MODEL)

    row = lambda w: pl.BlockSpec((tm, w), lambda i: (i, 0))
    return pl.pallas_call(
        body, name="ffn_fwd", grid=(S // tm,),
        in_specs=[row(D_MODEL), row(D_MODEL), row(D_MODEL), _const_spec(wg.shape), _const_spec(wu.shape),
                  _const_spec(wd.shape), _const_spec((1, D_MODEL))],
        out_specs=[row(D_MODEL), row(D_MODEL), pl.BlockSpec((1, D_MODEL), lambda i: (0, 0)),
                   pl.BlockSpec((1, 128), lambda i: (0, 0)), row(D_FF), row(D_FF), row(D_FF)],
        out_shape=[jax.ShapeDtypeStruct((S, D_MODEL), BF16), jax.ShapeDtypeStruct((S, D_MODEL), F32),
                   jax.ShapeDtypeStruct((1, D_MODEL), F32), jax.ShapeDtypeStruct((1, 128), F32)]
        + [jax.ShapeDtypeStruct((S, D_FF), BF16)] * 3,
        compiler_params=_cp(("arbitrary",)),
    )(f_in, h1, tgt, wg, wu, wd, g4)


def _ffn_bwd(gt_bf, up_bf, df, dh2, h1, yv, wg, wu, wd, g3, g2, tm):
    S = df.shape[0]

    def body(gt_ref, up_ref, df_ref, dh2_ref, h1_ref, y_ref, wg_ref, wu_ref, wd_ref, g3_ref, g2_ref,
             dgt_ref, dup_ref, dh1_ref, dy_ref, dg3_ref, dg2_ref):
        i = pl.program_id(0)
        df = df_ref[...]
        gt = gt_ref[...].astype(F32)
        up = up_ref[...].astype(F32)
        sg = _sigmoid(gt)
        silu = gt * sg
        dact = _dot_nt(df, wd_ref[...])
        dgt = (dact * up * (sg * (1.0 + gt * (1.0 - sg)))).astype(BF16)
        dup = (dact * silu).astype(BF16)
        dgt_ref[...] = dgt
        dup_ref[...] = dup
        dfin = _dot_nt(dgt, wg_ref[...]) + _dot_nt(dup, wu_ref[...])
        h1 = h1_ref[...]
        r3 = _rms(h1)
        dh1_n, dg3 = _rms_bwd(dfin, h1 * r3, r3, g3_ref[...])
        dh1 = dh2_ref[...] + dh1_n
        dh1_ref[...] = dh1
        y = y_ref[...]
        r2 = _rms(y)
        dy, dg2 = _rms_bwd(dh1, y * r2, r2, g2_ref[...])
        dy_ref[...] = dy.astype(BF16)

        @pl.when(i == 0)
        def _():
            dg3_ref[...] = jnp.zeros_like(dg3_ref)
            dg2_ref[...] = jnp.zeros_like(dg2_ref)

        dg3_ref[...] += jnp.sum(dg3, axis=0, keepdims=True)
        dg2_ref[...] += jnp.sum(dg2, axis=0, keepdims=True)

    row = lambda w: pl.BlockSpec((tm, w), lambda i: (i, 0))
    vec = pl.BlockSpec((1, D_MODEL), lambda i: (0, 0))
    return pl.pallas_call(
        body, name="ffn_bwd", grid=(S // tm,),
        in_specs=[row(D_FF), row(D_FF)] + [row(D_MODEL)] * 4
        + [_const_spec(wg.shape), _const_spec(wu.shape), _const_spec(wd.shape),
           _const_spec((1, D_MODEL)), _const_spec((1, D_MODEL))],
        out_specs=[row(D_FF), row(D_FF), row(D_MODEL), row(D_MODEL), vec, vec],
        out_shape=[jax.ShapeDtypeStruct((S, D_FF), BF16)] * 2
        + [jax.ShapeDtypeStruct((S, D_MODEL), F32), jax.ShapeDtypeStruct((S, D_MODEL), BF16),
           jax.ShapeDtypeStruct((1, D_MODEL), F32), jax.ShapeDtypeStruct((1, D_MODEL), F32)],
        compiler_params=_cp(("arbitrary",)),
    )(gt_bf, up_bf, df, dh2, h1, yv, wg, wu, wd, g3, g2)


def _out_bwd(dy, o, ag, wc, tm):
    S = o.shape[0]

    def body(dy_ref, o_ref, ag_ref, w_ref, dco_ref, do_ref, dag_ref):
        i = pl.program_id(0)
        seg = _seg_matrix(CONV_CH)
        dy = dy_ref[...]
        dco_ref[...] = _dot_nt(dy, w_ref[0:CONV_CH, :])
        datt = _dot_nt(dy, w_ref[CONV_CH:, :])
        o = o_ref[...]
        r = lax.rsqrt(_head_mean(o * o, seg) + EPS)
        n = o * r
        dn = datt * ag_ref[...]
        do_ref[...] = (r * (dn - n * _head_mean(dn * n, seg))).astype(BF16)

        @pl.when(i == 0)
        def _():
            dag_ref[...] = jnp.zeros_like(dag_ref)

        dag_ref[...] += jnp.sum(datt * n, axis=0, keepdims=True)

    row = lambda w: pl.BlockSpec((tm, w), lambda i: (i, 0))
    return pl.pallas_call(
        body, name="out_bwd", grid=(S // tm,),
        in_specs=[row(D_MODEL), row(CONV_CH), _const_spec((1, CONV_CH)), _const_spec(wc.shape)],
        out_specs=[row(CONV_CH), row(CONV_CH), pl.BlockSpec((1, CONV_CH), lambda i: (0, 0))],
        out_shape=[jax.ShapeDtypeStruct((S, CONV_CH), F32), jax.ShapeDtypeStruct((S, CONV_CH), BF16),
                   jax.ShapeDtypeStruct((1, CONV_CH), F32)],
        compiler_params=_cp(("arbitrary",)),
    )(dy, o, ag, wc)


def _attn_bwd(qkv, do, t, tk, parts):
    S = qkv.shape[0]
    nk = S // tk
    ns = len(parts)

    def body(*refs):
        q_ref, k_ref, v_ref, do_ref = refs[:4]
        dq_ref, dk_hbm, dv_hbm = refs[4 + ns:7 + ns]
        g_buf, s_buf, r_ref, dq_acc, dk_ref, dv_ref, z_buf = refs[7 + 2 * ns:14 + 2 * ns]
        p = pl.program_id(0)
        i = pl.program_id(1)
        plan = _ScatterPlan(refs[4:4 + ns], refs[7 + ns:7 + 2 * ns], *refs[14 + 2 * ns:])
        pl.when((p == 0) & (i == 0))(plan.start)
        last = (i * t + t - 1) // tk

        @pl.when(i == 0)
        def _():
            dk_ref[...] = jnp.zeros_like(dk_ref)
            dv_ref[...] = jnp.zeros_like(dv_ref)

        m_suf = _suffix_matrix(tk // KEY_BLOCKS)
        m_pre = _suffix_matrix(tk // KEY_BLOCKS, prefix=True)
        q = q_ref[...]
        do = do_ref[...]
        hms = [_lane_mask(h) for h in range(2)]
        qms = [jnp.where(hm, q, 0) * 0.125 for hm in hms]
        doms = [jnp.where(hm, do, 0) for hm in hms]
        dq_acc[...] = jnp.zeros_like(dq_acc)
        r_ref[...] = jnp.zeros_like(r_ref)

        def keys(sb, w=tk):
            return pl.ds(pl.multiple_of(sb * tk, tk), w)

        def matmuls1(sb, w, diagonal):
            kt = k_ref[keys(sb, w), :]
            vt = v_ref[keys(sb, w), :]
            mask = _causal_mask(i, sb, t, tk, w) if diagonal else None
            return [(_scores(qms[h], kt, mask), _dot_nt(doms[h], vt)) for h in range(2)]

        def sweep1(sb, w, vals):
            dv = jnp.zeros((w, PAIR), F32)
            for h in range(2):
                z, da = vals[h]
                A, sp, r_ref[h] = _sb_tile(z, r_ref[h], m_suf)
                g_buf[h, sb, :, 0:w] = A * da
                s_buf[h, sb, :, 0:w] = 1.0 - jnp.exp(-sp)
                dv = dv + _dot_tn(A.astype(BF16), doms[h])
            dv_ref[keys(sb, w), :] += dv

        def sweep2(sb, w):
            kt = k_ref[keys(sb, w), :]
            dk = jnp.zeros((w, PAIR), F32)
            for h in range(2):
                g = g_buf[h, sb, :, 0:w]
                pre, r_ref[h] = _running_sums(g, m_pre, reverse=False, start=r_ref[h])
                dzb = (g - s_buf[h, sb, :, 0:w] * pre).astype(BF16)
                dq_acc[...] += _dot(dzb, jnp.where(hms[h], kt, 0))
                dk = dk + _dot_tn(dzb, qms[h])
            dk_ref[keys(sb, w), :] += dk

        def diagonal_tile(tile):
            for nb in range(1, KEY_BLOCKS + 1):
                pl.when(lax.rem(i, KEY_BLOCKS) == nb - 1)(functools.partial(tile, nb * t))

        def fetch1(sb, slot, first):
            kt = k_ref[keys(sb), :]
            for h in range(2):
                z_buf[slot, h] = _scores(qms[h], kt, None)

        def load1(slot):
            return [z_buf[slot, h] for h in range(2)]

        def work1(sb, zs):
            vt = v_ref[keys(sb), :]
            sweep1(sb, tk, [(zs[h], _dot_nt(doms[h], vt)) for h in range(2)])

        diagonal_tile(lambda w: sweep1(last, w, matmuls1(last, w, True)))
        pl.when(last >= 1)(lambda: _sweep(last - 1, last - 1, True, fetch1, load1, work1))
        r_ref[...] = jnp.zeros_like(r_ref)
        _sweep_pairs(last, lambda sb: sweep2(sb, tk))
        diagonal_tile(lambda w: sweep2(last, w))
        dq_ref[...] = dq_acc[...] * 0.125

        @pl.when(i == S // t - 1)
        def _():
            cols = pl.ds(pl.multiple_of(p * PAIR, PAIR), PAIR)
            pltpu.sync_copy(dk_ref, dk_hbm.at[:, cols])
            pltpu.sync_copy(dv_ref, dv_hbm.at[:, cols])

        pl.when((p == N_PAIRS - 1) & (i == S // t - 1))(plan.finish)

    once = lambda cb: pl.BlockSpec((S, PAIR), cb, pipeline_mode=pl.Buffered(1))
    return pl.pallas_call(
        body, name="attn_bwd", grid=(N_PAIRS, S // t),
        in_specs=[pl.BlockSpec((t, PAIR), lambda p, i: (i, p)),
                  once(lambda p, i: (0, N_PAIRS + p)), once(lambda p, i: (0, 2 * N_PAIRS + p)),
                  pl.BlockSpec((t, PAIR), lambda p, i: (i, p))] + [_hbm()] * ns,
        out_specs=[pl.BlockSpec((t, PAIR), lambda p, i: (i, p)), _hbm(), _hbm()] + [_hbm()] * ns,
        out_shape=[jax.ShapeDtypeStruct((S, N_PAIRS * PAIR), F32)] * 3
        + [jax.ShapeDtypeStruct(pt.shape, pt.dtype) for pt in parts],
        scratch_shapes=[pltpu.VMEM((2, nk, t, tk), F32), pltpu.VMEM((2, nk, t, tk), F32),
                        pltpu.VMEM((2, t, 1), F32), pltpu.VMEM((t, PAIR), F32),
                        pltpu.VMEM((S, PAIR), F32), pltpu.VMEM((S, PAIR), F32),
                        pltpu.VMEM((2, 2, t, tk), F32)] + _scatter_sems(ns),
        compiler_params=_cp(("arbitrary", "arbitrary"), vmem=VMEM_LIMIT_ATTN_BWD),
    )(qkv, qkv, qkv, do, *parts)


def _conv_bwd(uc, yconv, dco, cwf, lg, lb, tm):
    S = uc.shape[0]
    hb = tm // HALO
    nb = S // tm
    ext = tm + HALO

    def body(uc_ref, prev_ref, y_ref, ynext_ref, dco_ref, dnext_ref, cw_ref, lg_ref, lb_ref,
             duc_ref, dcw_ref, dcb_ref, dlg_ref, dlb_ref, glu_ref, dyc_ref, shg_ref, shd_ref):
        i = pl.program_id(0)
        last = i == nb - 1

        @pl.when(i == 0)
        def _():
            for ref in (dcw_ref, dcb_ref, dlg_ref, dlb_ref):
                ref[...] = jnp.zeros_like(ref)

        uc = uc_ref[...]
        glu_ref[0:HALO, :] = jnp.where(i == 0, 0.0, _glu(prev_ref[...]))
        glu_ref[HALO:ext, :] = _glu(uc)
        glu_ref[ext:ext + SUBLANES, :] = jnp.zeros((SUBLANES, CONV_CH), F32)
        _shift_copies(glu_ref, shg_ref)
        fwd_offs = [HALO - (CONV_WIDTH - 1) + w for w in range(CONV_WIDTH)]
        y = jnp.concatenate([y_ref[...], ynext_ref[...]], axis=0)
        mu = jnp.mean(y, axis=-1, keepdims=True)
        yc = y - mu
        rstd = lax.rsqrt(jnp.mean(yc * yc, axis=-1, keepdims=True) + EPS)
        yhat = yc * rstd
        lg = lg_ref[...]
        ln = yhat * lg + lb_ref[...]
        sg = _sigmoid(ln)
        dout = jnp.concatenate([dco_ref[...], jnp.where(last, 0.0, dnext_ref[...])], axis=0)
        dln = dout * (sg * (1.0 + ln * (1.0 - sg)))
        dyh = dln * lg
        dyc = rstd * (dyh - jnp.mean(dyh, axis=-1, keepdims=True)
                      - yhat * jnp.mean(dyh * yhat, axis=-1, keepdims=True))
        dyc_ref[0:ext, :] = dyc
        dyc_ref[ext:ext + SUBLANES, :] = jnp.zeros((SUBLANES, CONV_CH), F32)
        _shift_copies(dyc_ref, shd_ref)
        dlg_ref[...] += jnp.sum((dln * yhat)[0:tm], axis=0, keepdims=True)
        dlb_ref[...] += jnp.sum(dln[0:tm], axis=0, keepdims=True)
        dcb_ref[...] += jnp.sum(dyc[0:tm], axis=0, keepdims=True)
        dglu = _conv_taps(cw_ref, dyc_ref, shd_ref, [CONV_WIDTH - 1 - w for w in range(CONV_WIDTH)], tm)
        d0 = dyc[0:tm]
        for w, off in enumerate(fwd_offs):
            dcw_ref[w:w + 1, :] += jnp.sum(d0 * _rows_at(glu_ref, shg_ref, off, tm), axis=0, keepdims=True)
        val, gate = uc[:, :CONV_CH], uc[:, CONV_CH:]
        sgate = _sigmoid(gate)
        duc_ref[:, :CONV_CH] = (dglu * sgate).astype(BF16)
        duc_ref[:, CONV_CH:] = (dglu * val * sgate * (1.0 - sgate)).astype(BF16)

    vec = pl.BlockSpec((1, CONV_CH), lambda i: (0, 0))
    nxt = lambda i: (jnp.minimum((i + 1) * hb, S // HALO - 1), 0)
    return pl.pallas_call(
        body, name="conv_bwd", grid=(nb,),
        in_specs=[pl.BlockSpec((tm, 2 * CONV_CH), lambda i: (i, 0)),
                  pl.BlockSpec((HALO, 2 * CONV_CH), lambda i: (jnp.maximum(i * hb - 1, 0), 0)),
                  pl.BlockSpec((tm, CONV_CH), lambda i: (i, 0)), pl.BlockSpec((HALO, CONV_CH), nxt),
                  pl.BlockSpec((tm, CONV_CH), lambda i: (i, 0)), pl.BlockSpec((HALO, CONV_CH), nxt),
                  _const_spec(cwf.shape), _const_spec((1, CONV_CH)), _const_spec((1, CONV_CH))],
        out_specs=[pl.BlockSpec((tm, 2 * CONV_CH), lambda i: (i, 0)),
                   pl.BlockSpec(cwf.shape, lambda i: (0, 0)), vec, vec, vec],
        out_shape=[jax.ShapeDtypeStruct((S, 2 * CONV_CH), BF16), jax.ShapeDtypeStruct(cwf.shape, F32)]
        + [jax.ShapeDtypeStruct((1, CONV_CH), F32)] * 3,
        scratch_shapes=[pltpu.VMEM((ext + SUBLANES, CONV_CH), F32), pltpu.VMEM((ext + SUBLANES, CONV_CH), F32),
                        pltpu.VMEM((SUBLANES - 1, ext, CONV_CH), F32),
                        pltpu.VMEM((SUBLANES - 1, ext, CONV_CH), F32)],
        compiler_params=_cp(("arbitrary",)),
    )(uc, uc, yconv, yconv, dco, dco, cwf, lg, lb)


def _in_bwd(duc, dq, dk, dv, x2, dh1, g1, wa, tm):
    S = x2.shape[0]

    def body(duc_ref, dq_ref, dk_ref, dv_ref, x_ref, dh1_ref, g_ref, w_ref, gx_ref, du_ref, dg_ref):
        i = pl.program_id(0)
        du = jnp.concatenate([duc_ref[...], dq_ref[...].astype(BF16), dk_ref[...].astype(BF16),
                              dv_ref[...].astype(BF16)], axis=1)
        du_ref[...] = du
        da = _dot_nt(du[:, 0:IN_SH], w_ref[0])
        for j in range(1, N_CHIPS):
            da = da + _dot_nt(du[:, IN_SH * j:IN_SH * (j + 1)], w_ref[j])
        x = x_ref[...]
        r = _rms(x)
        dx, dg = _rms_bwd(da, x * r, r, g_ref[...])
        gx_ref[...] = dh1_ref[...] + dx

        @pl.when(i == 0)
        def _():
            dg_ref[...] = jnp.zeros_like(dg_ref)

        dg_ref[...] += jnp.sum(dg, axis=0, keepdims=True)

    row = lambda w: pl.BlockSpec((tm, w), lambda i: (i, 0))
    return pl.pallas_call(
        body, name="in_bwd", grid=(S // tm,),
        in_specs=[row(2 * CONV_CH), row(CONV_CH), row(CONV_CH), row(CONV_CH), row(D_MODEL), row(D_MODEL),
                  _const_spec((1, D_MODEL)), _const_spec(wa.shape)],
        out_specs=[pl.BlockSpec((None, tm, D_MODEL), lambda i: (0, i, 0)), row(2560),
                   pl.BlockSpec((1, D_MODEL), lambda i: (0, 0))],
        out_shape=[jax.ShapeDtypeStruct((1, S, D_MODEL), F32), jax.ShapeDtypeStruct((S, 2560), BF16),
                   jax.ShapeDtypeStruct((1, D_MODEL), F32)],
        compiler_params=_cp(("arbitrary",)),
    )(duc, dq, dk, dv, x2, dh1, g1, wa)


def _matmul_tn(xm, ym, tm, ts, name, column_block=None):
    S, M = xm.shape
    N = ym.shape[1]

    def body(x_ref, y_ref, o_ref):
        @pl.when(pl.program_id(1) == 0)
        def _():
            o_ref[...] = jnp.zeros_like(o_ref)

        xt = x_ref[...].T
        if column_block is None:
            o_ref[...] += _dot(xt, y_ref[...])
        else:
            for j in range(N // column_block):
                o_ref[j] += _dot(xt, y_ref[:, column_block * j:column_block * (j + 1)])

    if column_block is None:
        out_spec = pl.BlockSpec((tm, N), lambda m, s: (m, 0))
        out_shape = jax.ShapeDtypeStruct((M, N), F32)
    else:
        out_spec = pl.BlockSpec((N // column_block, tm, column_block), lambda m, s: (0, m, 0))
        out_shape = jax.ShapeDtypeStruct((N // column_block, M, column_block), F32)
    return pl.pallas_call(
        body, name=name, grid=(M // tm, S // ts),
        in_specs=[pl.BlockSpec((ts, tm), lambda m, s: (s, m)), pl.BlockSpec((ts, N), lambda m, s: (s, 0))],
        out_specs=out_spec, out_shape=out_shape,
        compiler_params=_cp(("parallel", "arbitrary")),
    )(xm, ym)


def _sibling_halves(grads, name):
    n = len(grads)

    def body(*refs):
        ins, outs, ssem, rsem = refs[:n], refs[n:2 * n], refs[2 * n], refs[2 * n + 1]
        x, y, c = lax.axis_index("x"), lax.axis_index("y"), lax.axis_index("c")
        copies = []
        for k in range(n):
            for j in range(N_CHIPS):
                copies.append(pltpu.make_async_remote_copy(
                    src_ref=ins[k].at[j, 1 - c], dst_ref=outs[k].at[j],
                    send_sem=ssem.at[N_CHIPS * k + j], recv_sem=rsem.at[N_CHIPS * k + j],
                    device_id=(x, y, 1 - c), device_id_type=MESH))
        for cp in copies:
            cp.start()
        for cp in copies:
            cp.wait()

    shapes = [jax.ShapeDtypeStruct((g.shape[0],) + g.shape[2:], F32) for g in grads]
    return pl.pallas_call(
        body, name=name, out_shape=shapes,
        in_specs=[_hbm()] * n, out_specs=[_hbm()] * n,
        scratch_shapes=[pltpu.SemaphoreType.DMA((N_CHIPS * n,)), pltpu.SemaphoreType.DMA((N_CHIPS * n,))],
    )(*grads)


def _add_half(c_arr, g, landed, name):
    def body(c_ref, g_ref, l_ref, o_ref):
        o_ref[...] = (g_ref[...] + l_ref[...]).astype(BF16)

    rows, n = g.shape[2], g.shape[3]
    grid = (N_CHIPS,)
    g_spec = pl.BlockSpec((None, None, rows, n), lambda j, c: (j, c[0], 0, 0))
    l_spec = pl.BlockSpec((None, rows, n), lambda j, c: (j, 0, 0))
    return pl.pallas_call(
        body, name=name,
        grid_spec=pltpu.PrefetchScalarGridSpec(num_scalar_prefetch=1, grid=grid, in_specs=[g_spec, l_spec],
                                               out_specs=l_spec),
        out_shape=jax.ShapeDtypeStruct(landed.shape, BF16),
        compiler_params=_cp(("parallel",)),
    )(c_arr, g, landed)


class _ScatterPlan:
    def __init__(self, ins, outs, lsem, ssem, rsem):
        x, y, c = lax.axis_index("x"), lax.axis_index("y"), lax.axis_index("c")
        me = 2 * x + y
        self.copies = []
        for k in range(len(ins)):
            self.copies.append(pltpu.make_async_copy(ins[k].at[me], outs[k].at[me], lsem.at[k]))
            for r, chip in enumerate([(1 - x, y), (x, 1 - y), (1 - x, 1 - y)]):
                self.copies.append(pltpu.make_async_remote_copy(
                    src_ref=ins[k].at[2 * chip[0] + chip[1]], dst_ref=outs[k].at[me],
                    send_sem=ssem.at[3 * k + r], recv_sem=rsem.at[3 * k + r],
                    device_id=(chip[0], chip[1], c), device_id_type=MESH))

    def start(self):
        for cp in self.copies:
            cp.start()

    def finish(self):
        for cp in self.copies:
            cp.wait()


def _scatter_sems(n):
    return [pltpu.SemaphoreType.DMA((n,)), pltpu.SemaphoreType.DMA((3 * n,)), pltpu.SemaphoreType.DMA((3 * n,))]


def _chip_scatter(parts):
    n = len(parts)

    def body(*refs):
        plan = _ScatterPlan(refs[:n], refs[n:2 * n], *refs[2 * n:])
        plan.start()
        plan.finish()

    shapes = [jax.ShapeDtypeStruct(p.shape, p.dtype) for p in parts]
    return pl.pallas_call(
        body, name="grad_chip_scatter", out_shape=shapes,
        in_specs=[_hbm()] * n, out_specs=[_hbm()] * n, scratch_shapes=_scatter_sems(n),
    )(*parts)


def _sum_chips(landed, name):
    _, rows, n = landed.shape
    tr = 256 if rows % 256 == 0 else rows

    def body(a_ref, b_ref, c_ref, d_ref, o_ref):
        f = lambda ref: ref[...].astype(F32)
        o_ref[...] = ((f(a_ref) + f(b_ref)) + f(c_ref)) + f(d_ref)

    specs = [pl.BlockSpec((None, tr, n), functools.partial(lambda i, j: (j, i, 0), j=j)) for j in range(N_CHIPS)]
    return pl.pallas_call(
        body, name=name, grid=(rows // tr,), in_specs=specs,
        out_specs=pl.BlockSpec((tr, n), lambda i: (i, 0)),
        out_shape=jax.ShapeDtypeStruct((rows, n), F32),
        compiler_params=_cp(("parallel",)),
    )(landed, landed, landed, landed)


def _share_halves(halves):
    n = len(halves)

    def body(*refs):
        ins, outs = refs[:n], refs[n:2 * n]
        ssem, rsem = refs[2 * n:]
        x, y, c = lax.axis_index("x"), lax.axis_index("y"), lax.axis_index("c")
        copies = [pltpu.make_async_remote_copy(
            src_ref=ins[k], dst_ref=outs[k], send_sem=ssem.at[k], recv_sem=rsem.at[k],
            device_id=(x, y, 1 - c), device_id_type=MESH) for k in range(n)]
        for cp in copies:
            cp.start()
        for cp in copies:
            cp.wait()

    shapes = [jax.ShapeDtypeStruct(h.shape, F32) for h in halves]
    return pl.pallas_call(
        body, name="grad_share_halves", out_shape=shapes,
        in_specs=[_hbm()] * n, out_specs=[_hbm()] * n,
        scratch_shapes=[pltpu.SemaphoreType.DMA((n,)), pltpu.SemaphoreType.DMA((n,))],
    )(*halves)


def _allreduce_small(packed):
    rows, n = packed.shape

    def body(in_ref, out_ref, land_ref, ssem, rsem):
        x, y, c = lax.axis_index("x"), lax.axis_index("y"), lax.axis_index("c")
        me = 4 * x + 2 * y + c
        land_ref[me] = in_ref[...]
        copies = []
        for r in range(1, 8):
            tx = 1 - x if r & 4 else x
            ty = 1 - y if r & 2 else y
            tc = 1 - c if r & 1 else c
            cp = pltpu.make_async_remote_copy(
                src_ref=in_ref, dst_ref=land_ref.at[me], send_sem=ssem.at[r - 1], recv_sem=rsem.at[r - 1],
                device_id=(tx, ty, tc), device_id_type=MESH)
            cp.start()
            copies.append(cp)
        for cp in copies:
            cp.wait()
        acc = land_ref[0]
        for k in range(1, 8):
            acc = acc + land_ref[k]
        out_ref[...] = acc

    return pl.pallas_call(
        body, name="allreduce_small", out_shape=jax.ShapeDtypeStruct((rows, n), F32),
        in_specs=[pl.BlockSpec(memory_space=pltpu.VMEM)], out_specs=pl.BlockSpec(memory_space=pltpu.VMEM),
        scratch_shapes=[pltpu.VMEM((8, rows, n), F32), pltpu.SemaphoreType.DMA((7,)),
                        pltpu.SemaphoreType.DMA((7,))],
    )(packed)


def _adamw_math(w, g, m, v):
    m = ADAM_B1 * m + (1.0 - ADAM_B1) * g
    v = ADAM_B2 * v + (1.0 - ADAM_B2) * (g * g)
    m_hat = m / (1.0 - ADAM_B1 ** ADAM_STEP)
    v_hat = v / (1.0 - ADAM_B2 ** ADAM_STEP)
    return -ADAM_LR * (m_hat / (jnp.sqrt(v_hat) + ADAM_EPS) + ADAM_WD * w), m, v


def _adamw_halves(c_arr, w, mine, other, m, v, name):
    rows, n = mine.shape
    tr = 256 if rows % 256 == 0 else rows
    nb = rows // tr

    def body(c_ref, w_ref, a_ref, b_ref, m_ref, v_ref, g_ref, d_ref, mo_ref, vo_ref):
        g = jnp.where(pl.program_id(0) == c_ref[0], a_ref[...], b_ref[...])
        g_ref[...] = g
        d_ref[...], mo_ref[...], vo_ref[...] = _adamw_math(w_ref[...], g, m_ref[...], v_ref[...])

    full = pl.BlockSpec((None, tr, n), lambda h, i, c: (0, h * nb + i, 0))
    half = pl.BlockSpec((tr, n), lambda h, i, c: (i, 0))
    return pl.pallas_call(
        body, name=name,
        grid_spec=pltpu.PrefetchScalarGridSpec(num_scalar_prefetch=1, grid=(2, nb),
                                               in_specs=[full, half, half, full, full], out_specs=[full] * 4),
        out_shape=[jax.ShapeDtypeStruct((1, 2 * rows, n), F32)] * 4,
        compiler_params=_cp(("parallel", "parallel")),
    )(c_arr, w, mine, other, m, v)


def _adamw(w, g, m, v, name):
    rows, n = w.shape
    tr = 256 if rows % 256 == 0 else rows

    def body(w_ref, g_ref, m_ref, v_ref, d_ref, mo_ref, vo_ref):
        d_ref[...], mo_ref[...], vo_ref[...] = _adamw_math(w_ref[...], g_ref[...], m_ref[...], v_ref[...])

    spec = pl.BlockSpec((tr, n), lambda i: (i, 0))
    return pl.pallas_call(
        body, name=name, grid=(rows // tr,), in_specs=[spec] * 4, out_specs=[spec] * 3,
        out_shape=[jax.ShapeDtypeStruct((rows, n), F32)] * 3,
        compiler_params=_cp(("parallel",)),
    )(w, g, m, v)


def _rows8(a):
    a = a.reshape(-1, 128)
    return jnp.pad(a, ((0, (-a.shape[0]) % 8), (0, 0)))


def kernel(x, g_pre_mix, w_in, conv_w, conv_b, conv_ln_g, conv_ln_b, attn_norm_g, w_out, g_post_mix, g_pre_ffn, w_gate, w_up, w_down, g_post_ffn, loss_target, m_g_pre_mix, m_w_in, m_conv_w, m_conv_b, m_conv_ln_g, m_conv_ln_b, m_attn_norm_g, m_w_out, m_g_post_mix, m_g_pre_ffn, m_w_gate, m_w_up, m_w_down, m_g_post_ffn, v_g_pre_mix, v_w_in, v_conv_w, v_conv_b, v_conv_ln_g, v_conv_ln_b, v_attn_norm_g, v_w_out, v_g_post_mix, v_g_pre_ffn, v_w_gate, v_w_up, v_w_down, v_g_post_ffn):
    S = x.shape[1]
    tm_big = min(512, S)
    tm_ffn = min(256, S)
    tk_att = min(1024, S)
    t_att_fwd = min(1024, S)
    t_att_bwd = tk_att // KEY_BLOCKS
    chip = 2 * lax.axis_index("x") + lax.axis_index("y")
    core = lax.axis_index("c")
    x2 = x.reshape(S, D_MODEL)
    tgt = loss_target.reshape(S, D_MODEL)
    ag = attn_norm_g.reshape(1, CONV_CH)

    a_sh = w_in[0].astype(BF16)
    b_sh = jnp.stack([w_gate[0], w_up[0]]).astype(BF16)
    c_sh = jnp.concatenate([w_out[0], w_down[0]], axis=0).astype(BF16)
    cw_sh = jnp.pad(conv_w[0, :, 0, :], ((0, 1), (0, 0)))
    own = lambda full, shard: lax.dynamic_update_index_in_dim(full, shard, chip, 0)
    cols = lambda w4: jnp.transpose(w4, (1, 0, 2)).reshape(w4.shape[1], N_CHIPS * w4.shape[2])
    wa4, cw4 = _gather_weights([a_sh, cw_sh], [False, False])
    wa = own(wa4, a_sh)
    cwf = cols(own(cw4, cw_sh))

    a_bf, uc, qkv = _in_proj(x2, g_pre_mix, wa, tm_big)
    conv_out, yconv = _conv_fwd(uc, cwf, conv_b, conv_ln_g, conv_ln_b, tm_big)
    o, wb4, wc4 = _attn_fwd(qkv, t_att_fwd, tk_att, [b_sh, c_sh], [True, False])
    wb4, wc4 = own(wb4, b_sh), own(wc4, c_sh)
    wg, wu = cols(wb4[:, 0]), cols(wb4[:, 1])
    wo = wc4[:, :OUT_SH].reshape(D_MODEL, D_MODEL)
    wd = wc4[:, OUT_SH:].reshape(D_FF, D_MODEL)
    mixed, yv, h1, f_in = _out_proj(conv_out, o, ag, wo, x2, g_post_mix, g_pre_ffn, tm_big)
    df, dh2, dg4, loss_part, gt_bf, up_bf, act = _ffn_fwd(f_in, h1, tgt, wg, wu, wd, g_post_ffn, tm_ffn)

    dgt, dup, dh1, dy, dg3, dg2 = _ffn_bwd(gt_bf, up_bf, df, dh2, h1, yv, wg, wu, wd, g_pre_ffn, g_post_mix, tm_ffn)
    dco, do, dag = _out_bwd(dy, o, ag, wo, tm_big)
    ts = min(512, S)
    gw_out = _matmul_tn(mixed, dy, D_MODEL, ts, "grad_w_out")
    gw_gate = _matmul_tn(f_in, dgt, D_MODEL, ts, "grad_w_gate")
    gw_up = _matmul_tn(f_in, dup, D_MODEL, ts, "grad_w_up")
    gw_down = _matmul_tn(act, df, D_FF // 2, ts, "grad_w_down")

    by_cols = lambda g: jnp.transpose(g.reshape(2, D_MODEL // 2, N_CHIPS, -1), (2, 0, 1, 3))
    by_rows = lambda g: g.reshape(N_CHIPS, 2, g.shape[0] // (2 * N_CHIPS), g.shape[1])
    c_arr = core.reshape(1).astype(jnp.int32)

    def chip_partials(views, nms):
        landed = _sibling_halves(views, "grad_sibling_halves_" + nms[0])
        return [_add_half(c_arr, g, l, "grad_half_" + nm) for g, l, nm in zip(views, landed, nms)]

    early = ["w_gate", "w_up", "w_out", "w_down"]
    parts = chip_partials([by_cols(gw_gate), by_cols(gw_up), by_rows(gw_out), by_rows(gw_down)], early)
    dq, dk, dv, *slots = _attn_bwd(qkv, do, t_att_bwd, tk_att, parts)
    duc, dcw, dcb, dlg, dlb = _conv_bwd(uc, yconv, dco, cwf, conv_ln_g, conv_ln_b, tm_big)
    grad_x, du, dg1 = _in_bwd(duc, dq, dk, dv, x2, dh1, g_pre_mix, wa, tm_big)
    gw_in = _matmul_tn(a_bf, du, D_MODEL, ts, "grad_w_in", column_block=IN_SH)
    slots += _chip_scatter(chip_partials([gw_in.reshape(N_CHIPS, 2, D_MODEL // 2, IN_SH)], ["w_in"]))
    names = early + ["w_in"]
    halves = [_sum_chips(s, "grad_sum_" + nm) for s, nm in zip(slots, names)]
    others = _share_halves(halves)
    mine = dict(zip(names, halves))
    other = dict(zip(names, others))

    small = [dg1, dcb, dlg, dlb, dag, dg2, dg3, dg4]
    packed = jnp.concatenate([_rows8(s) for s in small] + [_rows8(dcw), _rows8(loss_part)], axis=0)
    red = _allreduce_small(packed)
    sizes = [D_MODEL, CONV_CH, CONV_CH, CONV_CH, CONV_CH, D_MODEL, D_MODEL, D_MODEL]
    g_small = [red[8 * k:8 * k + n // 128].reshape(1, n) for k, n in enumerate(sizes)]
    cw_red = red[64:64 + 128].reshape(HALO, CONV_CH)
    g_cw = lax.dynamic_slice(cw_red, (0, chip * 128), (HALO, 128))
    loss = red[192, 0]

    big = []
    for w, m, v, nm in [(w_in, m_w_in, v_w_in, "w_in"), (w_out, m_w_out, v_w_out, "w_out"),
                        (w_gate, m_w_gate, v_w_gate, "w_gate"), (w_up, m_w_up, v_w_up, "w_up"),
                        (w_down, m_w_down, v_w_down, "w_down")]:
        big.append(_adamw_halves(c_arr, w, mine[nm], other[nm], m, v, "adamw_" + nm))
    sm_w = [g_pre_mix, conv_b, conv_ln_g, conv_ln_b, ag, g_post_mix, g_pre_ffn, g_post_ffn]
    sm_m = [m_g_pre_mix, m_conv_b, m_conv_ln_g, m_conv_ln_b, m_attn_norm_g, m_g_post_mix, m_g_pre_ffn, m_g_post_ffn]
    sm_v = [v_g_pre_mix, v_conv_b, v_conv_ln_g, v_conv_ln_b, v_attn_norm_g, v_g_post_mix, v_g_pre_ffn, v_g_post_ffn]
    pad_cw = lambda a: jnp.pad(a[0, :, 0, :], ((0, 1), (0, 0)))

    def pack(vecs, cw):
        return jnp.concatenate([_rows8(a) for a in vecs] + [cw], axis=0)

    sd, smn, svn = _adamw(pack(sm_w, pad_cw(conv_w)), pack(g_small, g_cw), pack(sm_m, pad_cw(m_conv_w)),
                          pack(sm_v, pad_cw(v_conv_w)), "adamw_small")

    def unpack(p):
        vecs = [p[8 * k:8 * k + n // 128].reshape(1, n) for k, n in enumerate(sizes)]
        return vecs, p[64:64 + CONV_WIDTH].reshape(1, CONV_WIDTH, 1, 128)

    def ordered(vecs, cw, w_in_, w_out_, w_gate_, w_up_, w_down_):
        g1_, cb_, lg_, lb_, ag_, g2_, g3_, g4_ = vecs
        return [g1_, w_in_, cw, cb_, lg_, lb_, ag_.reshape(1, 8, HEAD_DIM), w_out_, g2_, g3_,
                w_gate_, w_up_, w_down_, g4_]

    grads = ordered(g_small, g_cw[:CONV_WIDTH].reshape(1, CONV_WIDTH, 1, 128), *[b[0] for b in big])
    outs = []
    for idx, p in enumerate((sd, smn, svn)):
        vecs, cw = unpack(p)
        outs += ordered(vecs, cw, *[b[idx + 1] for b in big])
    return (loss, grad_x, *grads, *outs)
```

```python
import functools

import jax
import jax.numpy as jnp
from jax import lax
from jax.experimental import pallas as pl
from jax.experimental.pallas import tpu as pltpu

F32 = jnp.float32
BF16 = jnp.bfloat16
MESH = pl.DeviceIdType.MESH

D_MODEL = 1024
CONV_CH = 512
CONV_WIDTH = 31
HEAD_DIM = 64
PAIR = 2 * HEAD_DIM
N_PAIRS = 4
D_FF = 2816
N_CHIPS = 4
IN_SH = 2560 // N_CHIPS
OUT_SH = D_MODEL // N_CHIPS
EPS = 1e-6
HALO = 32

ADAM_LR = 0.001
ADAM_B1 = 0.9
ADAM_B2 = 0.999
ADAM_EPS = 1e-08
ADAM_WD = 0.01
ADAM_STEP = 10

VMEM_LIMIT = 56 * 2 ** 20
VMEM_LIMIT_ATTN_BWD = 62 * 2 ** 20


def _cp(sem=None, vmem=VMEM_LIMIT):
    return pltpu.CompilerParams(dimension_semantics=sem, vmem_limit_bytes=vmem)


def _hbm():
    return pl.BlockSpec(memory_space=pltpu.HBM)


def _const_spec(shape):
    nd = len(shape)
    return pl.BlockSpec(shape, lambda *_: (0,) * nd, pipeline_mode=pl.Buffered(1))


def _dot(a, b):
    return jnp.dot(a, b, preferred_element_type=F32)


def _dot_nt(a, b):
    return lax.dot_general(a, b, (((1,), (1,)), ((), ())), preferred_element_type=F32)


def _dot_tn(a, b):
    return lax.dot_general(a, b, (((0,), (0,)), ((), ())), preferred_element_type=F32)


def _split3(x):
    b0 = x.astype(BF16)
    r1 = x - b0.astype(F32)
    b1 = r1.astype(BF16)
    b2 = (r1 - b1.astype(F32)).astype(BF16)
    return b0, b1, b2


def _sigmoid(x):
    return 1.0 / (1.0 + jnp.exp(-x))


def _head_mean(x, seg):
    b0, b1, b2 = _split3(x)
    return (_dot(b0, seg) + _dot(b1, seg) + _dot(b2, seg)) * (1.0 / HEAD_DIM)


def _seg_matrix(n):
    r = lax.broadcasted_iota(jnp.int32, (n, n), 0) // HEAD_DIM
    c = lax.broadcasted_iota(jnp.int32, (n, n), 1) // HEAD_DIM
    return (r == c).astype(BF16)


def _rms(x):
    return lax.rsqrt(jnp.mean(x * x, axis=-1, keepdims=True) + EPS)


def _rms_bwd(dy, n, r, g):
    dn = dy * g
    dx = r * (dn - n * jnp.mean(dn * n, axis=-1, keepdims=True))
    return dx, dy * n


class _GatherPlan:
    def __init__(self, srcs, outs, lead, ssem, rsem):
        self.srcs, self.outs, self.lead, self.ssem, self.rsem = srcs, outs, lead, ssem, rsem
        x, y, self.c = lax.axis_index("x"), lax.axis_index("y"), lax.axis_index("c")
        self.me = 2 * x + y
        self.sibling = (x, y, 1 - self.c)
        self.chips = [(1 - x, y), (x, 1 - y), (1 - x, 1 - y)]

    def _half(self, ref, i, h):
        if self.lead[i]:
            return ref.at[h]
        rows = ref.shape[0] // 2
        return ref.at[pl.ds(h * rows, rows)]

    def _ici(self, i, k, origin):
        return pltpu.make_async_remote_copy(
            src_ref=self._half(self.srcs[i], i, self.c), dst_ref=self._half(self.outs[i].at[origin], i, self.c),
            send_sem=self.ssem.at[6 * i + k], recv_sem=self.rsem.at[6 * i + k],
            device_id=(self.chips[k][0], self.chips[k][1], self.c), device_id_type=MESH)

    def _d2d(self, i, k, h):
        origin = 2 * self.chips[k][0] + self.chips[k][1]
        piece = self._half(self.outs[i].at[origin], i, h)
        return pltpu.make_async_remote_copy(
            src_ref=piece, dst_ref=piece, send_sem=self.ssem.at[6 * i + 3 + k],
            recv_sem=self.rsem.at[6 * i + 3 + k], device_id=self.sibling, device_id_type=MESH)

    def _each(self):
        return [(i, k) for i in range(len(self.srcs)) for k in range(3)]

    def start(self):
        for i, k in self._each():
            self._ici(i, k, self.me).start()

    def forward(self):
        for i, k in self._each():
            self._ici(i, k, 2 * self.chips[k][0] + self.chips[k][1]).wait_recv()
            self._d2d(i, k, self.c).start()

    def finish(self):
        for i, k in self._each():
            self._d2d(i, k, 1 - self.c).wait_recv()
        for i, k in self._each():
            self._ici(i, k, self.me).wait_send()
            self._d2d(i, k, self.c).wait_send()


def _gather_shapes(shards):
    return [jax.ShapeDtypeStruct((N_CHIPS,) + s.shape, s.dtype) for s in shards]


def _gather_weights(shards, lead):
    n = len(shards)

    def body(*refs):
        plan = _GatherPlan(refs[:n], refs[n:2 * n], lead, refs[2 * n], refs[2 * n + 1])
        plan.start()
        plan.forward()
        plan.finish()

    return pl.pallas_call(
        body, name="gather_weights", out_shape=_gather_shapes(shards),
        in_specs=[_hbm()] * n, out_specs=[_hbm()] * n,
        scratch_shapes=[pltpu.SemaphoreType.DMA((6 * n,)), pltpu.SemaphoreType.DMA((6 * n,))],
    )(*shards)


def _in_proj(x2, g1, wa, tm):
    S = x2.shape[0]

    def body(x_ref, g_ref, w_ref, a_ref, uc_ref, qkv_ref):
        x = x_ref[...]
        a = (x * _rms(x) * g_ref[...]).astype(BF16)
        a_ref[...] = a
        u = [_dot(a, w_ref[j]) for j in range(N_CHIPS)]
        uc_ref[:, 0:640] = u[0]
        uc_ref[:, 640:1024] = u[1][:, 0:384]
        qkv_ref[:, 0:256] = u[1][:, 384:640].astype(BF16)
        qkv_ref[:, 256:896] = u[2].astype(BF16)
        qkv_ref[:, 896:1536] = u[3].astype(BF16)

    return pl.pallas_call(
        body, name="in_proj", grid=(S // tm,),
        in_specs=[pl.BlockSpec((tm, D_MODEL), lambda i: (i, 0)), _const_spec((1, D_MODEL)),
                  _const_spec(wa.shape)],
        out_specs=[pl.BlockSpec((tm, D_MODEL), lambda i: (i, 0)),
                   pl.BlockSpec((tm, 2 * CONV_CH), lambda i: (i, 0)),
                   pl.BlockSpec((tm, 1536), lambda i: (i, 0))],
        out_shape=[jax.ShapeDtypeStruct((S, D_MODEL), BF16), jax.ShapeDtypeStruct((S, 2 * CONV_CH), F32),
                   jax.ShapeDtypeStruct((S, 1536), BF16)],
        compiler_params=_cp(("parallel",)),
    )(x2, g1, wa)


SUBLANES = 8


def _shift_copies(src_ref, sh_ref):
    rows = sh_ref.shape[1]
    for b in range(1, SUBLANES):
        sh_ref[b - 1] = src_ref[pl.ds(b, rows), :]


def _rows_at(src_ref, sh_ref, off, rows):
    a, b = divmod(off, SUBLANES)
    if b == 0:
        return src_ref[pl.ds(SUBLANES * a, rows), :]
    return sh_ref[b - 1, pl.ds(SUBLANES * a, rows), :]


def _conv_taps(cw_ref, src_ref, sh_ref, offs, rows):
    acc = None
    for w, off in enumerate(offs):
        term = cw_ref[w:w + 1, :] * _rows_at(src_ref, sh_ref, off, rows)
        acc = term if acc is None else acc + term
    return acc


def _glu(uc):
    return uc[:, :CONV_CH] * _sigmoid(uc[:, CONV_CH:])


def _conv_fwd(uc, cwf, cb, lg, lb, tm):
    S = uc.shape[0]
    hb = tm // HALO

    def body(uc_ref, prev_ref, cw_ref, cb_ref, lg_ref, lb_ref, out_ref, y_ref, glu_ref, sh_ref):
        i = pl.program_id(0)
        glu_ref[0:HALO, :] = jnp.where(i == 0, 0.0, _glu(prev_ref[...]))
        glu_ref[HALO:HALO + tm, :] = _glu(uc_ref[...])
        glu_ref[HALO + tm:HALO + tm + SUBLANES, :] = jnp.zeros((SUBLANES, CONV_CH), F32)
        _shift_copies(glu_ref, sh_ref)
        offs = [HALO - (CONV_WIDTH - 1) + w for w in range(CONV_WIDTH)]
        y = _conv_taps(cw_ref, glu_ref, sh_ref, offs, tm) + cb_ref[...]
        y_ref[...] = y
        mu = jnp.mean(y, axis=-1, keepdims=True)
        yc = y - mu
        rstd = lax.rsqrt(jnp.mean(yc * yc, axis=-1, keepdims=True) + EPS)
        ln = yc * rstd * lg_ref[...] + lb_ref[...]
        out_ref[...] = (ln * _sigmoid(ln)).astype(BF16)

    return pl.pallas_call(
        body, name="conv_fwd", grid=(S // tm,),
        in_specs=[pl.BlockSpec((tm, 2 * CONV_CH), lambda i: (i, 0)),
                  pl.BlockSpec((HALO, 2 * CONV_CH), lambda i: (jnp.maximum(i * hb - 1, 0), 0)),
                  _const_spec(cwf.shape), _const_spec((1, CONV_CH)), _const_spec((1, CONV_CH)),
                  _const_spec((1, CONV_CH))],
        out_specs=[pl.BlockSpec((tm, CONV_CH), lambda i: (i, 0))] * 2,
        out_shape=[jax.ShapeDtypeStruct((S, CONV_CH), BF16), jax.ShapeDtypeStruct((S, CONV_CH), F32)],
        scratch_shapes=[pltpu.VMEM((HALO + tm + SUBLANES, CONV_CH), F32),
                        pltpu.VMEM((SUBLANES - 1, HALO + tm, CONV_CH), F32)],
        compiler_params=_cp(("parallel",)),
    )(uc, uc, cwf, cb, lg, lb)


def _lane_mask(h):
    lane = lax.broadcasted_iota(jnp.int32, (1, PAIR), 1)
    return (lane >= HEAD_DIM * h) & (lane < HEAD_DIM * (h + 1))


def _neg_abs(x):
    bits = lax.bitcast_convert_type(x, jnp.uint32) | jnp.uint32(0x80000000)
    return lax.bitcast_convert_type(bits, F32)


def _tri_dot(x, m):
    return _dot(x.astype(BF16), m)


MASKED = -1e30
KEY_BLOCKS = 4


def _running_sums(x, m, reverse, start=None):
    t = m.shape[0]
    blocks = x.shape[1] // t
    order = range(blocks - 1, -1, -1) if reverse else range(blocks)
    out = [None] * blocks
    carry = start
    for b in order:
        xb = x[:, b * t:(b + 1) * t]
        cb = _tri_dot(xb, m)
        out[b] = cb if carry is None else cb + carry
        rs = jnp.sum(xb, axis=1, keepdims=True)
        carry = rs if carry is None else carry + rs
    return jnp.concatenate(out, axis=1), carry


def _sb_tile(z, r, m_suf):
    sp = jnp.maximum(z, 0.0) + jnp.log(1.0 + jnp.exp(_neg_abs(z)))
    c, rs = _running_sums(sp, m_suf, reverse=True, start=r)
    return jnp.exp(z - c), sp, rs


def _scores(qm, kt, mask):
    z = _dot_nt(qm, kt)
    return z if mask is None else jnp.where(mask, z, MASKED)


def _causal_mask(i, sb, t, tk, w=None):
    w = tk if w is None else w
    row = lax.broadcasted_iota(jnp.int32, (t, w), 0) + i * t
    col = lax.broadcasted_iota(jnp.int32, (t, w), 1) + sb * tk
    return col < row


def _sweep_pairs(count, tile):
    def step(n, carry):
        tile(2 * n)
        tile(2 * n + 1)
        return carry

    lax.fori_loop(0, count // 2, step, 0)
    pl.when(lax.rem(count, 2) == 1)(lambda: tile(count - 1))


def _sweep(first, count, down, fetch, load, work, fetched=False):
    lo, hi = (first - count, first) if down else (first, first + count)
    tile = lambda j: jnp.clip(first - j if down else first + j, lo, hi)
    if not fetched:
        fetch(first, 0, True)

    def step(n, carry):
        j = 2 * n
        vals = load(0)
        fetch(tile(j + 1), 1, False)
        work(tile(j), vals)
        vals = load(1)
        fetch(tile(j + 2), 0, False)
        work(tile(j + 1), vals)
        return carry

    lax.fori_loop(0, (count + 1) // 2, step, 0)

    @pl.when(lax.rem(count, 2) == 0)
    def _():
        work(tile(count), load(0))


def _suffix_matrix(t, prefix=False):
    row = lax.broadcasted_iota(jnp.int32, (t, t), 0)
    col = lax.broadcasted_iota(jnp.int32, (t, t), 1)
    return ((row <= col) if prefix else (row >= col)).astype(BF16)


def _attn_fwd(qkv, t, tk, shards, lead):
    S = qkv.shape[0]
    assert t == tk, "the forward cuts its diagonal tile by blocks: query block and key tile must match"

    ng = len(shards)
    nq = S // t

    def body(*refs):
        q_ref, k_ref, v_ref = refs[:3]
        o_ref = refs[3 + ng]
        acc_ref, r_ref, z_buf, ssem, rsem = refs[4 + 2 * ng:]
        p = pl.program_id(0)
        i = pl.program_id(1)
        plan = _GatherPlan(refs[3:3 + ng], refs[4 + ng:4 + 2 * ng], lead, ssem, rsem)
        pl.when((p == 0) & (i == 0))(plan.start)
        pl.when((p == 1) & (i == 0))(plan.forward)
        blk = tk // KEY_BLOCKS
        m_suf = _suffix_matrix(blk)
        q = q_ref[...]
        hms = [_lane_mask(h) for h in range(2)]
        qms = [jnp.where(hm, q, 0) * 0.125 for hm in hms]

        def rows(sb, w=tk):
            return pl.ds(pl.multiple_of(sb * tk, tk), w)

        def fetch(sb, slot, first):
            kt = k_ref[rows(sb), :]
            for h in range(2):
                z_buf[slot, h] = _scores(qms[h], kt, None)

        fetch(jnp.maximum(i - 1, 0), 0, True)
        for rb in range(KEY_BLOCKS):
            w = (rb + 1) * blk
            part = slice(rb * blk, w)
            kt = k_ref[rows(i, w), :]
            vt = v_ref[rows(i, w), :]
            row = lax.broadcasted_iota(jnp.int32, (blk, w), 0) + rb * blk
            mask = lax.broadcasted_iota(jnp.int32, (blk, w), 1) < row
            out = jnp.zeros((blk, PAIR), F32)
            for h in range(2):
                a_loc, _, rs = _sb_tile(_scores(qms[h][part], kt, mask), None, m_suf)
                out = out + _dot(a_loc.astype(BF16), jnp.where(hms[h], vt, 0))
                r_ref[h, part] = rs
            acc_ref[part, :] = out

        def load(slot):
            return [z_buf[slot, h] for h in range(2)]

        def work(sb, zs):
            vt = v_ref[rows(sb), :]
            for h in range(2):
                a_loc, _, rs = _sb_tile(zs[h], None, m_suf)
                r = r_ref[h]
                acc_ref[...] += _dot(a_loc.astype(BF16), jnp.where(hms[h], vt, 0)) * jnp.exp(-r)
                r_ref[h] = r + rs

        pl.when(i >= 1)(lambda: _sweep(i - 1, i - 1, True, fetch, load, work, fetched=True))
        o_ref[...] = acc_ref[...]
        pl.when((p == N_PAIRS - 1) & (i == nq - 1))(plan.finish)

    return pl.pallas_call(
        body, name="attn_fwd", grid=(N_PAIRS, nq),
        in_specs=[pl.BlockSpec((t, PAIR), lambda p, i: (i, p)),
                  pl.BlockSpec((S, PAIR), lambda p, i: (0, N_PAIRS + p)),
                  pl.BlockSpec((S, PAIR), lambda p, i: (0, 2 * N_PAIRS + p))] + [_hbm()] * ng,
        out_specs=[pl.BlockSpec((t, PAIR), lambda p, i: (i, p))] + [_hbm()] * ng,
        out_shape=[jax.ShapeDtypeStruct((S, N_PAIRS * PAIR), F32)] + _gather_shapes(shards),
        scratch_shapes=[pltpu.VMEM((t, PAIR), F32), pltpu.VMEM((2, t, 1), F32),
                        pltpu.VMEM((2, 2, t, tk), F32),
                        pltpu.SemaphoreType.DMA((6 * ng,)), pltpu.SemaphoreType.DMA((6 * ng,))],
        compiler_params=_cp(("arbitrary", "arbitrary")),
    )(qkv, qkv, qkv, *shards)


def _out_proj(conv_out, o, ag, wc, x2, g2, g3, tm):
    S = o.shape[0]

    def body(co_ref, o_ref, ag_ref, w_ref, x_ref, g2_ref, g3_ref, mix_ref, y_ref, h1_ref, fin_ref):
        seg = _seg_matrix(CONV_CH)
        o = o_ref[...]
        att = (o * lax.rsqrt(_head_mean(o * o, seg) + EPS) * ag_ref[...]).astype(BF16)
        co = co_ref[...]
        mix_ref[:, :CONV_CH] = co
        mix_ref[:, CONV_CH:] = att
        y = _dot(co, w_ref[0:CONV_CH, :]) + _dot(att, w_ref[CONV_CH:, :])
        y_ref[...] = y
        h1 = x_ref[...] + y * _rms(y) * g2_ref[...]
        h1_ref[...] = h1
        fin_ref[...] = (h1 * _rms(h1) * g3_ref[...]).astype(BF16)

    row = lambda w: pl.BlockSpec((tm, w), lambda i: (i, 0))
    return pl.pallas_call(
        body, name="out_proj", grid=(S // tm,),
        in_specs=[row(CONV_CH), row(CONV_CH), _const_spec((1, CONV_CH)), _const_spec(wc.shape),
                  row(D_MODEL), _const_spec((1, D_MODEL)), _const_spec((1, D_MODEL))],
        out_specs=[row(D_MODEL)] * 4,
        out_shape=[jax.ShapeDtypeStruct((S, D_MODEL), BF16), jax.ShapeDtypeStruct((S, D_MODEL), F32),
                   jax.ShapeDtypeStruct((S, D_MODEL), F32), jax.ShapeDtypeStruct((S, D_MODEL), BF16)],
        compiler_params=_cp(("parallel",)),
    )(conv_out, o, ag, wc, x2, g2, g3)


def _ffn_fwd(f_in, h1, tgt, wg, wu, wd, g4, tm):
    S = f_in.shape[0]

    def body(fin_ref, h1_ref, tgt_ref, wg_ref, wu_ref, wd_ref, g4_ref, df_ref, dh2_ref, dg4_ref, loss_ref,
             gt_ref, up_ref, act_ref):
        i = pl.program_id(0)
        fin = fin_ref[...]
        gt = _dot(fin, wg_ref[...])
        up = _dot(fin, wu_ref[...])
        act = (gt * _sigmoid(gt) * up).astype(BF16)
        gt_ref[...] = gt.astype(BF16)
        up_ref[...] = up.astype(BF16)
        act_ref[...] = act
        f = _dot(act, wd_ref[...])
        r = _rms(f)
        n = f * r
        g4 = g4_ref[...]
        err = h1_ref[...] + n * g4 - tgt_ref[...]
        dh2 = err * (1.0 / D_MODEL)
        dh2_ref[...] = dh2
        df, dg = _rms_bwd(dh2, n, r, g4)
        df_ref[...] = df.astype(BF16)

        @pl.when(i == 0)
        def _():
            dg4_ref[...] = jnp.zeros_like(dg4_ref)
            loss_ref[...] = jnp.zeros_like(loss_ref)

        dg4_ref[...] += jnp.sum(dg, axis=0, keepdims=True)
        part = jnp.sum(jnp.sum(err * err, axis=1, keepdims=True), axis=0, keepdims=True)
        loss_ref[...] += part * (0.5 / D_MODEL)

    row = lambda w: pl.BlockSpec((tm, w), lambda i: (i, 0))
    return pl.pallas_call(
        body, name="ffn_fwd", grid=(S // tm,),
        in_specs=[row(D_MODEL), row(D_MODEL), row(D_MODEL), _const_spec(wg.shape), _const_spec(wu.shape),
                  _const_spec(wd.shape), _const_spec((1, D_MODEL))],
        out_specs=[row(D_MODEL), row(D_MODEL), pl.BlockSpec((1, D_MODEL), lambda i: (0, 0)),
                   pl.BlockSpec((1, 128), lambda i: (0, 0)), row(D_FF), row(D_FF), row(D_FF)],
        out_shape=[jax.ShapeDtypeStruct((S, D_MODEL), BF16), jax.ShapeDtypeStruct((S, D_MODEL), F32),
                   jax.ShapeDtypeStruct((1, D_MODEL), F32), jax.ShapeDtypeStruct((1, 128), F32)]
        + [jax.ShapeDtypeStruct((S, D_FF), BF16)] * 3,
        compiler_params=_cp(("arbitrary",)),
    )(f_in, h1, tgt, wg, wu, wd, g4)


def _ffn_bwd(gt_bf, up_bf, df, dh2, h1, yv, wg, wu, wd, g3, g2, tm):
    S = df.shape[0]

    def body(gt_ref, up_ref, df_ref, dh2_ref, h1_ref, y_ref, wg_ref, wu_ref, wd_ref, g3_ref, g2_ref,
             dgt_ref, dup_ref, dh1_ref, dy_ref, dg3_ref, dg2_ref):
        i = pl.program_id(0)
        df = df_ref[...]
        gt = gt_ref[...].astype(F32)
        up = up_ref[...].astype(F32)
        sg = _sigmoid(gt)
        silu = gt * sg
        dact = _dot_nt(df, wd_ref[...])
        dgt = (dact * up * (sg * (1.0 + gt * (1.0 - sg)))).astype(BF16)
        dup = (dact * silu).astype(BF16)
        dgt_ref[...] = dgt
        dup_ref[...] = dup
        dfin = _dot_nt(dgt, wg_ref[...]) + _dot_nt(dup, wu_ref[...])
        h1 = h1_ref[...]
        r3 = _rms(h1)
        dh1_n, dg3 = _rms_bwd(dfin, h1 * r3, r3, g3_ref[...])
        dh1 = dh2_ref[...] + dh1_n
        dh1_ref[...] = dh1
        y = y_ref[...]
        r2 = _rms(y)
        dy, dg2 = _rms_bwd(dh1, y * r2, r2, g2_ref[...])
        dy_ref[...] = dy.astype(BF16)

        @pl.when(i == 0)
        def _():
            dg3_ref[...] = jnp.zeros_like(dg3_ref)
            dg2_ref[...] = jnp.zeros_like(dg2_ref)

        dg3_ref[...] += jnp.sum(dg3, axis=0, keepdims=True)
        dg2_ref[...] += jnp.sum(dg2, axis=0, keepdims=True)

    row = lambda w: pl.BlockSpec((tm, w), lambda i: (i, 0))
    vec = pl.BlockSpec((1, D_MODEL), lambda i: (0, 0))
    return pl.pallas_call(
        body, name="ffn_bwd", grid=(S // tm,),
        in_specs=[row(D_FF), row(D_FF)] + [row(D_MODEL)] * 4
        + [_const_spec(wg.shape), _const_spec(wu.shape), _const_spec(wd.shape),
           _const_spec((1, D_MODEL)), _const_spec((1, D_MODEL))],
        out_specs=[row(D_FF), row(D_FF), row(D_MODEL), row(D_MODEL), vec, vec],
        out_shape=[jax.ShapeDtypeStruct((S, D_FF), BF16)] * 2
        + [jax.ShapeDtypeStruct((S, D_MODEL), F32), jax.ShapeDtypeStruct((S, D_MODEL), BF16),
           jax.ShapeDtypeStruct((1, D_MODEL), F32), jax.ShapeDtypeStruct((1, D_MODEL), F32)],
        compiler_params=_cp(("arbitrary",)),
    )(gt_bf, up_bf, df, dh2, h1, yv, wg, wu, wd, g3, g2)


def _out_bwd(dy, o, ag, wc, tm):
    S = o.shape[0]

    def body(dy_ref, o_ref, ag_ref, w_ref, dco_ref, do_ref, dag_ref):
        i = pl.program_id(0)
        seg = _seg_matrix(CONV_CH)
        dy = dy_ref[...]
        dco_ref[...] = _dot_nt(dy, w_ref[0:CONV_CH, :])
        datt = _dot_nt(dy, w_ref[CONV_CH:, :])
        o = o_ref[...]
        r = lax.rsqrt(_head_mean(o * o, seg) + EPS)
        n = o * r
        dn = datt * ag_ref[...]
        do_ref[...] = (r * (dn - n * _head_mean(dn * n, seg))).astype(BF16)

        @pl.when(i == 0)
        def _():
            dag_ref[...] = jnp.zeros_like(dag_ref)

        dag_ref[...] += jnp.sum(datt * n, axis=0, keepdims=True)

    row = lambda w: pl.BlockSpec((tm, w), lambda i: (i, 0))
    return pl.pallas_call(
        body, name="out_bwd", grid=(S // tm,),
        in_specs=[row(D_MODEL), row(CONV_CH), _const_spec((1, CONV_CH)), _const_spec(wc.shape)],
        out_specs=[row(CONV_CH), row(CONV_CH), pl.BlockSpec((1, CONV_CH), lambda i: (0, 0))],
        out_shape=[jax.ShapeDtypeStruct((S, CONV_CH), F32), jax.ShapeDtypeStruct((S, CONV_CH), BF16),
                   jax.ShapeDtypeStruct((1, CONV_CH), F32)],
        compiler_params=_cp(("arbitrary",)),
    )(dy, o, ag, wc)


def _attn_bwd(qkv, do, t, tk, parts):
    S = qkv.shape[0]
    nk = S // tk
    ns = len(parts)

    def body(*refs):
        q_ref, k_ref, v_ref, do_ref = refs[:4]
        dq_ref, dk_hbm, dv_hbm = refs[4 + ns:7 + ns]
        g_buf, s_buf, r_ref, dq_acc, dk_ref, dv_ref, z_buf = refs[7 + 2 * ns:14 + 2 * ns]
        p = pl.program_id(0)
        i = pl.program_id(1)
        plan = _ScatterPlan(refs[4:4 + ns], refs[7 + ns:7 + 2 * ns], *refs[14 + 2 * ns:])
        pl.when((p == 0) & (i == 0))(plan.start)
        last = (i * t + t - 1) // tk

        @pl.when(i == 0)
        def _():
            dk_ref[...] = jnp.zeros_like(dk_ref)
            dv_ref[...] = jnp.zeros_like(dv_ref)

        m_suf = _suffix_matrix(tk // KEY_BLOCKS)
        m_pre = _suffix_matrix(tk // KEY_BLOCKS, prefix=True)
        q = q_ref[...]
        do = do_ref[...]
        hms = [_lane_mask(h) for h in range(2)]
        qms = [jnp.where(hm, q, 0) * 0.125 for hm in hms]
        doms = [jnp.where(hm, do, 0) for hm in hms]
        dq_acc[...] = jnp.zeros_like(dq_acc)
        r_ref[...] = jnp.zeros_like(r_ref)

        def keys(sb, w=tk):
            return pl.ds(pl.multiple_of(sb * tk, tk), w)

        def matmuls1(sb, w, diagonal):
            kt = k_ref[keys(sb, w), :]
            vt = v_ref[keys(sb, w), :]
            mask = _causal_mask(i, sb, t, tk, w) if diagonal else None
            return [(_scores(qms[h], kt, mask), _dot_nt(doms[h], vt)) for h in range(2)]

        def sweep1(sb, w, vals):
            dv = jnp.zeros((w, PAIR), F32)
            for h in range(2):
                z, da = vals[h]
                A, sp, r_ref[h] = _sb_tile(z, r_ref[h], m_suf)
                g_buf[h, sb, :, 0:w] = A * da
                s_buf[h, sb, :, 0:w] = 1.0 - jnp.exp(-sp)
                dv = dv + _dot_tn(A.astype(BF16), doms[h])
            dv_ref[keys(sb, w), :] += dv

        def sweep2(sb, w):
            kt = k_ref[keys(sb, w), :]
            dk = jnp.zeros((w, PAIR), F32)
            for h in range(2):
                g = g_buf[h, sb, :, 0:w]
                pre, r_ref[h] = _running_sums(g, m_pre, reverse=False, start=r_ref[h])
                dzb = (g - s_buf[h, sb, :, 0:w] * pre).astype(BF16)
                dq_acc[...] += _dot(dzb, jnp.where(hms[h], kt, 0))
                dk = dk + _dot_tn(dzb, qms[h])
            dk_ref[keys(sb, w), :] += dk

        def diagonal_tile(tile):
            for nb in range(1, KEY_BLOCKS + 1):
                pl.when(lax.rem(i, KEY_BLOCKS) == nb - 1)(functools.partial(tile, nb * t))

        def fetch1(sb, slot, first):
            kt = k_ref[keys(sb), :]
            for h in range(2):
                z_buf[slot, h] = _scores(qms[h], kt, None)

        def load1(slot):
            return [z_buf[slot, h] for h in range(2)]

        def work1(sb, zs):
            vt = v_ref[keys(sb), :]
            sweep1(sb, tk, [(zs[h], _dot_nt(doms[h], vt)) for h in range(2)])

        def diagonal1(w):
            fetch1(jnp.maximum(last - 1, 0), 0, True)
            sweep1(last, w, matmuls1(last, w, True))

        diagonal_tile(diagonal1)
        pl.when(last >= 1)(lambda: _sweep(last - 1, last - 1, True, fetch1, load1, work1, fetched=True))
        r_ref[...] = jnp.zeros_like(r_ref)
        _sweep_pairs(last, lambda sb: sweep2(sb, tk))
        diagonal_tile(lambda w: sweep2(last, w))
        dq_ref[...] = dq_acc[...] * 0.125

        @pl.when(i == S // t - 1)
        def _():
            cols = pl.ds(pl.multiple_of(p * PAIR, PAIR), PAIR)
            pltpu.sync_copy(dk_ref, dk_hbm.at[:, cols])
            pltpu.sync_copy(dv_ref, dv_hbm.at[:, cols])

        pl.when((p == N_PAIRS - 1) & (i == S // t - 1))(plan.finish)

    once = lambda cb: pl.BlockSpec((S, PAIR), cb, pipeline_mode=pl.Buffered(1))
    return pl.pallas_call(
        body, name="attn_bwd", grid=(N_PAIRS, S // t),
        in_specs=[pl.BlockSpec((t, PAIR), lambda p, i: (i, p)),
                  once(lambda p, i: (0, N_PAIRS + p)), once(lambda p, i: (0, 2 * N_PAIRS + p)),
                  pl.BlockSpec((t, PAIR), lambda p, i: (i, p))] + [_hbm()] * ns,
        out_specs=[pl.BlockSpec((t, PAIR), lambda p, i: (i, p)), _hbm(), _hbm()] + [_hbm()] * ns,
        out_shape=[jax.ShapeDtypeStruct((S, N_PAIRS * PAIR), F32)] * 3
        + [jax.ShapeDtypeStruct(pt.shape, pt.dtype) for pt in parts],
        scratch_shapes=[pltpu.VMEM((2, nk, t, tk), F32), pltpu.VMEM((2, nk, t, tk), F32),
                        pltpu.VMEM((2, t, 1), F32), pltpu.VMEM((t, PAIR), F32),
                        pltpu.VMEM((S, PAIR), F32), pltpu.VMEM((S, PAIR), F32),
                        pltpu.VMEM((2, 2, t, tk), F32)] + _scatter_sems(ns),
        compiler_params=_cp(("arbitrary", "arbitrary"), vmem=VMEM_LIMIT_ATTN_BWD),
    )(qkv, qkv, qkv, do, *parts)


def _conv_bwd(uc, yconv, dco, cwf, lg, lb, tm):
    S = uc.shape[0]
    hb = tm // HALO
    nb = S // tm
    ext = tm + HALO

    def body(uc_ref, prev_ref, y_ref, ynext_ref, dco_ref, dnext_ref, cw_ref, lg_ref, lb_ref,
             duc_ref, dcw_ref, dcb_ref, dlg_ref, dlb_ref, glu_ref, dyc_ref, shg_ref, shd_ref):
        i = pl.program_id(0)
        last = i == nb - 1

        @pl.when(i == 0)
        def _():
            for ref in (dcw_ref, dcb_ref, dlg_ref, dlb_ref):
                ref[...] = jnp.zeros_like(ref)

        uc = uc_ref[...]
        glu_ref[0:HALO, :] = jnp.where(i == 0, 0.0, _glu(prev_ref[...]))
        glu_ref[HALO:ext, :] = _glu(uc)
        glu_ref[ext:ext + SUBLANES, :] = jnp.zeros((SUBLANES, CONV_CH), F32)
        _shift_copies(glu_ref, shg_ref)
        fwd_offs = [HALO - (CONV_WIDTH - 1) + w for w in range(CONV_WIDTH)]
        y = jnp.concatenate([y_ref[...], ynext_ref[...]], axis=0)
        mu = jnp.mean(y, axis=-1, keepdims=True)
        yc = y - mu
        rstd = lax.rsqrt(jnp.mean(yc * yc, axis=-1, keepdims=True) + EPS)
        yhat = yc * rstd
        lg = lg_ref[...]
        ln = yhat * lg + lb_ref[...]
        sg = _sigmoid(ln)
        dout = jnp.concatenate([dco_ref[...], jnp.where(last, 0.0, dnext_ref[...])], axis=0)
        dln = dout * (sg * (1.0 + ln * (1.0 - sg)))
        dyh = dln * lg
        dyc = rstd * (dyh - jnp.mean(dyh, axis=-1, keepdims=True)
                      - yhat * jnp.mean(dyh * yhat, axis=-1, keepdims=True))
        dyc_ref[0:ext, :] = dyc
        dyc_ref[ext:ext + SUBLANES, :] = jnp.zeros((SUBLANES, CONV_CH), F32)
        _shift_copies(dyc_ref, shd_ref)
        dlg_ref[...] += jnp.sum((dln * yhat)[0:tm], axis=0, keepdims=True)
        dlb_ref[...] += jnp.sum(dln[0:tm], axis=0, keepdims=True)
        dcb_ref[...] += jnp.sum(dyc[0:tm], axis=0, keepdims=True)
        dglu = _conv_taps(cw_ref, dyc_ref, shd_ref, [CONV_WIDTH - 1 - w for w in range(CONV_WIDTH)], tm)
        d0 = dyc[0:tm]
        for w, off in enumerate(fwd_offs):
            dcw_ref[w:w + 1, :] += jnp.sum(d0 * _rows_at(glu_ref, shg_ref, off, tm), axis=0, keepdims=True)
        val, gate = uc[:, :CONV_CH], uc[:, CONV_CH:]
        sgate = _sigmoid(gate)
        duc_ref[:, :CONV_CH] = (dglu * sgate).astype(BF16)
        duc_ref[:, CONV_CH:] = (dglu * val * sgate * (1.0 - sgate)).astype(BF16)

    vec = pl.BlockSpec((1, CONV_CH), lambda i: (0, 0))
    nxt = lambda i: (jnp.minimum((i + 1) * hb, S // HALO - 1), 0)
    return pl.pallas_call(
        body, name="conv_bwd", grid=(nb,),
        in_specs=[pl.BlockSpec((tm, 2 * CONV_CH), lambda i: (i, 0)),
                  pl.BlockSpec((HALO, 2 * CONV_CH), lambda i: (jnp.maximum(i * hb - 1, 0), 0)),
                  pl.BlockSpec((tm, CONV_CH), lambda i: (i, 0)), pl.BlockSpec((HALO, CONV_CH), nxt),
                  pl.BlockSpec((tm, CONV_CH), lambda i: (i, 0)), pl.BlockSpec((HALO, CONV_CH), nxt),
                  _const_spec(cwf.shape), _const_spec((1, CONV_CH)), _const_spec((1, CONV_CH))],
        out_specs=[pl.BlockSpec((tm, 2 * CONV_CH), lambda i: (i, 0)),
                   pl.BlockSpec(cwf.shape, lambda i: (0, 0)), vec, vec, vec],
        out_shape=[jax.ShapeDtypeStruct((S, 2 * CONV_CH), BF16), jax.ShapeDtypeStruct(cwf.shape, F32)]
        + [jax.ShapeDtypeStruct((1, CONV_CH), F32)] * 3,
        scratch_shapes=[pltpu.VMEM((ext + SUBLANES, CONV_CH), F32), pltpu.VMEM((ext + SUBLANES, CONV_CH), F32),
                        pltpu.VMEM((SUBLANES - 1, ext, CONV_CH), F32),
                        pltpu.VMEM((SUBLANES - 1, ext, CONV_CH), F32)],
        compiler_params=_cp(("arbitrary",)),
    )(uc, uc, yconv, yconv, dco, dco, cwf, lg, lb)


def _in_bwd(duc, dq, dk, dv, x2, dh1, g1, wa, tm):
    S = x2.shape[0]

    def body(duc_ref, dq_ref, dk_ref, dv_ref, x_ref, dh1_ref, g_ref, w_ref, gx_ref, du_ref, dg_ref):
        i = pl.program_id(0)
        du = jnp.concatenate([duc_ref[...], dq_ref[...].astype(BF16), dk_ref[...].astype(BF16),
                              dv_ref[...].astype(BF16)], axis=1)
        du_ref[...] = du
        da = _dot_nt(du[:, 0:IN_SH], w_ref[0])
        for j in range(1, N_CHIPS):
            da = da + _dot_nt(du[:, IN_SH * j:IN_SH * (j + 1)], w_ref[j])
        x = x_ref[...]
        r = _rms(x)
        dx, dg = _rms_bwd(da, x * r, r, g_ref[...])
        gx_ref[...] = dh1_ref[...] + dx

        @pl.when(i == 0)
        def _():
            dg_ref[...] = jnp.zeros_like(dg_ref)

        dg_ref[...] += jnp.sum(dg, axis=0, keepdims=True)

    row = lambda w: pl.BlockSpec((tm, w), lambda i: (i, 0))
    return pl.pallas_call(
        body, name="in_bwd", grid=(S // tm,),
        in_specs=[row(2 * CONV_CH), row(CONV_CH), row(CONV_CH), row(CONV_CH), row(D_MODEL), row(D_MODEL),
                  _const_spec((1, D_MODEL)), _const_spec(wa.shape)],
        out_specs=[pl.BlockSpec((None, tm, D_MODEL), lambda i: (0, i, 0)), row(2560),
                   pl.BlockSpec((1, D_MODEL), lambda i: (0, 0))],
        out_shape=[jax.ShapeDtypeStruct((1, S, D_MODEL), F32), jax.ShapeDtypeStruct((S, 2560), BF16),
                   jax.ShapeDtypeStruct((1, D_MODEL), F32)],
        compiler_params=_cp(("arbitrary",)),
    )(duc, dq, dk, dv, x2, dh1, g1, wa)


def _matmul_tn(xm, ym, tm, ts, name, column_block=None):
    S, M = xm.shape
    N = ym.shape[1]

    def body(x_ref, y_ref, o_ref):
        @pl.when(pl.program_id(1) == 0)
        def _():
            o_ref[...] = jnp.zeros_like(o_ref)

        xt = x_ref[...].T
        if column_block is None:
            o_ref[...] += _dot(xt, y_ref[...])
        else:
            for j in range(N // column_block):
                o_ref[j] += _dot(xt, y_ref[:, column_block * j:column_block * (j + 1)])

    if column_block is None:
        out_spec = pl.BlockSpec((tm, N), lambda m, s: (m, 0))
        out_shape = jax.ShapeDtypeStruct((M, N), F32)
    else:
        out_spec = pl.BlockSpec((N // column_block, tm, column_block), lambda m, s: (0, m, 0))
        out_shape = jax.ShapeDtypeStruct((N // column_block, M, column_block), F32)
    return pl.pallas_call(
        body, name=name, grid=(M // tm, S // ts),
        in_specs=[pl.BlockSpec((ts, tm), lambda m, s: (s, m)), pl.BlockSpec((ts, N), lambda m, s: (s, 0))],
        out_specs=out_spec, out_shape=out_shape,
        compiler_params=_cp(("parallel", "arbitrary")),
    )(xm, ym)


def _sibling_halves(grads, name):
    n = len(grads)

    def body(*refs):
        ins, outs, ssem, rsem = refs[:n], refs[n:2 * n], refs[2 * n], refs[2 * n + 1]
        x, y, c = lax.axis_index("x"), lax.axis_index("y"), lax.axis_index("c")
        copies = []
        for k in range(n):
            for j in range(N_CHIPS):
                copies.append(pltpu.make_async_remote_copy(
                    src_ref=ins[k].at[j, 1 - c], dst_ref=outs[k].at[j],
                    send_sem=ssem.at[N_CHIPS * k + j], recv_sem=rsem.at[N_CHIPS * k + j],
                    device_id=(x, y, 1 - c), device_id_type=MESH))
        for cp in copies:
            cp.start()
        for cp in copies:
            cp.wait()

    shapes = [jax.ShapeDtypeStruct((g.shape[0],) + g.shape[2:], F32) for g in grads]
    return pl.pallas_call(
        body, name=name, out_shape=shapes,
        in_specs=[_hbm()] * n, out_specs=[_hbm()] * n,
        scratch_shapes=[pltpu.SemaphoreType.DMA((N_CHIPS * n,)), pltpu.SemaphoreType.DMA((N_CHIPS * n,))],
    )(*grads)


def _add_half(c_arr, g, landed, name):
    def body(c_ref, g_ref, l_ref, o_ref):
        o_ref[...] = (g_ref[...] + l_ref[...]).astype(BF16)

    rows, n = g.shape[2], g.shape[3]
    grid = (N_CHIPS,)
    g_spec = pl.BlockSpec((None, None, rows, n), lambda j, c: (j, c[0], 0, 0))
    l_spec = pl.BlockSpec((None, rows, n), lambda j, c: (j, 0, 0))
    return pl.pallas_call(
        body, name=name,
        grid_spec=pltpu.PrefetchScalarGridSpec(num_scalar_prefetch=1, grid=grid, in_specs=[g_spec, l_spec],
                                               out_specs=l_spec),
        out_shape=jax.ShapeDtypeStruct(landed.shape, BF16),
        compiler_params=_cp(("parallel",)),
    )(c_arr, g, landed)


class _ScatterPlan:
    def __init__(self, ins, outs, lsem, ssem, rsem):
        x, y, c = lax.axis_index("x"), lax.axis_index("y"), lax.axis_index("c")
        me = 2 * x + y
        self.copies = []
        for k in range(len(ins)):
            self.copies.append(pltpu.make_async_copy(ins[k].at[me], outs[k].at[me], lsem.at[k]))
            for r, chip in enumerate([(1 - x, y), (x, 1 - y), (1 - x, 1 - y)]):
                self.copies.append(pltpu.make_async_remote_copy(
                    src_ref=ins[k].at[2 * chip[0] + chip[1]], dst_ref=outs[k].at[me],
                    send_sem=ssem.at[3 * k + r], recv_sem=rsem.at[3 * k + r],
                    device_id=(chip[0], chip[1], c), device_id_type=MESH))

    def start(self):
        for cp in self.copies:
            cp.start()

    def finish(self):
        for cp in self.copies:
            cp.wait()


def _scatter_sems(n):
    return [pltpu.SemaphoreType.DMA((n,)), pltpu.SemaphoreType.DMA((3 * n,)), pltpu.SemaphoreType.DMA((3 * n,))]


def _chip_scatter(parts):
    n = len(parts)

    def body(*refs):
        plan = _ScatterPlan(refs[:n], refs[n:2 * n], *refs[2 * n:])
        plan.start()
        plan.finish()

    shapes = [jax.ShapeDtypeStruct(p.shape, p.dtype) for p in parts]
    return pl.pallas_call(
        body, name="grad_chip_scatter", out_shape=shapes,
        in_specs=[_hbm()] * n, out_specs=[_hbm()] * n, scratch_shapes=_scatter_sems(n),
    )(*parts)


def _sum_chips(landed, name):
    _, rows, n = landed.shape
    tr = 256 if rows % 256 == 0 else rows

    def body(a_ref, b_ref, c_ref, d_ref, o_ref):
        f = lambda ref: ref[...].astype(F32)
        o_ref[...] = ((f(a_ref) + f(b_ref)) + f(c_ref)) + f(d_ref)

    specs = [pl.BlockSpec((None, tr, n), functools.partial(lambda i, j: (j, i, 0), j=j)) for j in range(N_CHIPS)]
    return pl.pallas_call(
        body, name=name, grid=(rows // tr,), in_specs=specs,
        out_specs=pl.BlockSpec((tr, n), lambda i: (i, 0)),
        out_shape=jax.ShapeDtypeStruct((rows, n), F32),
        compiler_params=_cp(("parallel",)),
    )(landed, landed, landed, landed)


def _share_halves(halves):
    n = len(halves)

    def body(*refs):
        ins, outs = refs[:n], refs[n:2 * n]
        ssem, rsem = refs[2 * n:]
        x, y, c = lax.axis_index("x"), lax.axis_index("y"), lax.axis_index("c")
        copies = [pltpu.make_async_remote_copy(
            src_ref=ins[k], dst_ref=outs[k], send_sem=ssem.at[k], recv_sem=rsem.at[k],
            device_id=(x, y, 1 - c), device_id_type=MESH) for k in range(n)]
        for cp in copies:
            cp.start()
        for cp in copies:
            cp.wait()

    shapes = [jax.ShapeDtypeStruct(h.shape, F32) for h in halves]
    return pl.pallas_call(
        body, name="grad_share_halves", out_shape=shapes,
        in_specs=[_hbm()] * n, out_specs=[_hbm()] * n,
        scratch_shapes=[pltpu.SemaphoreType.DMA((n,)), pltpu.SemaphoreType.DMA((n,))],
    )(*halves)


def _allreduce_small(packed):
    rows, n = packed.shape

    def body(in_ref, out_ref, land_ref, ssem, rsem):
        x, y, c = lax.axis_index("x"), lax.axis_index("y"), lax.axis_index("c")
        me = 4 * x + 2 * y + c
        land_ref[me] = in_ref[...]
        copies = []
        for r in range(1, 8):
            tx = 1 - x if r & 4 else x
            ty = 1 - y if r & 2 else y
            tc = 1 - c if r & 1 else c
            cp = pltpu.make_async_remote_copy(
                src_ref=in_ref, dst_ref=land_ref.at[me], send_sem=ssem.at[r - 1], recv_sem=rsem.at[r - 1],
                device_id=(tx, ty, tc), device_id_type=MESH)
            cp.start()
            copies.append(cp)
        for cp in copies:
            cp.wait()
        acc = land_ref[0]
        for k in range(1, 8):
            acc = acc + land_ref[k]
        out_ref[...] = acc

    return pl.pallas_call(
        body, name="allreduce_small", out_shape=jax.ShapeDtypeStruct((rows, n), F32),
        in_specs=[pl.BlockSpec(memory_space=pltpu.VMEM)], out_specs=pl.BlockSpec(memory_space=pltpu.VMEM),
        scratch_shapes=[pltpu.VMEM((8, rows, n), F32), pltpu.SemaphoreType.DMA((7,)),
                        pltpu.SemaphoreType.DMA((7,))],
    )(packed)


def _adamw_math(w, g, m, v):
    m = ADAM_B1 * m + (1.0 - ADAM_B1) * g
    v = ADAM_B2 * v + (1.0 - ADAM_B2) * (g * g)
    m_hat = m / (1.0 - ADAM_B1 ** ADAM_STEP)
    v_hat = v / (1.0 - ADAM_B2 ** ADAM_STEP)
    return -ADAM_LR * (m_hat / (jnp.sqrt(v_hat) + ADAM_EPS) + ADAM_WD * w), m, v


def _adamw_halves(c_arr, w, mine, other, m, v, name):
    rows, n = mine.shape
    tr = 256 if rows % 256 == 0 else rows
    nb = rows // tr

    def body(c_ref, w_ref, a_ref, b_ref, m_ref, v_ref, g_ref, d_ref, mo_ref, vo_ref):
        g = jnp.where(pl.program_id(0) == c_ref[0], a_ref[...], b_ref[...])
        g_ref[...] = g
        d_ref[...], mo_ref[...], vo_ref[...] = _adamw_math(w_ref[...], g, m_ref[...], v_ref[...])

    full = pl.BlockSpec((None, tr, n), lambda h, i, c: (0, h * nb + i, 0))
    half = pl.BlockSpec((tr, n), lambda h, i, c: (i, 0))
    return pl.pallas_call(
        body, name=name,
        grid_spec=pltpu.PrefetchScalarGridSpec(num_scalar_prefetch=1, grid=(2, nb),
                                               in_specs=[full, half, half, full, full], out_specs=[full] * 4),
        out_shape=[jax.ShapeDtypeStruct((1, 2 * rows, n), F32)] * 4,
        compiler_params=_cp(("parallel", "parallel")),
    )(c_arr, w, mine, other, m, v)


def _adamw(w, g, m, v, name):
    rows, n = w.shape
    tr = 256 if rows % 256 == 0 else rows

    def body(w_ref, g_ref, m_ref, v_ref, d_ref, mo_ref, vo_ref):
        d_ref[...], mo_ref[...], vo_ref[...] = _adamw_math(w_ref[...], g_ref[...], m_ref[...], v_ref[...])

    spec = pl.BlockSpec((tr, n), lambda i: (i, 0))
    return pl.pallas_call(
        body, name=name, grid=(rows // tr,), in_specs=[spec] * 4, out_specs=[spec] * 3,
        out_shape=[jax.ShapeDtypeStruct((rows, n), F32)] * 3,
        compiler_params=_cp(("parallel",)),
    )(w, g, m, v)


def _rows8(a):
    a = a.reshape(-1, 128)
    return jnp.pad(a, ((0, (-a.shape[0]) % 8), (0, 0)))


def kernel(x, g_pre_mix, w_in, conv_w, conv_b, conv_ln_g, conv_ln_b, attn_norm_g, w_out, g_post_mix, g_pre_ffn, w_gate, w_up, w_down, g_post_ffn, loss_target, m_g_pre_mix, m_w_in, m_conv_w, m_conv_b, m_conv_ln_g, m_conv_ln_b, m_attn_norm_g, m_w_out, m_g_post_mix, m_g_pre_ffn, m_w_gate, m_w_up, m_w_down, m_g_post_ffn, v_g_pre_mix, v_w_in, v_conv_w, v_conv_b, v_conv_ln_g, v_conv_ln_b, v_attn_norm_g, v_w_out, v_g_post_mix, v_g_pre_ffn, v_w_gate, v_w_up, v_w_down, v_g_post_ffn):
    S = x.shape[1]
    tm_big = min(512, S)
    tm_ffn = min(256, S)
    tk_att = min(1024, S)
    t_att_fwd = min(1024, S)
    t_att_bwd = tk_att // KEY_BLOCKS
    chip = 2 * lax.axis_index("x") + lax.axis_index("y")
    core = lax.axis_index("c")
    x2 = x.reshape(S, D_MODEL)
    tgt = loss_target.reshape(S, D_MODEL)
    ag = attn_norm_g.reshape(1, CONV_CH)

    a_sh = w_in[0].astype(BF16)
    b_sh = jnp.stack([w_gate[0], w_up[0]]).astype(BF16)
    c_sh = jnp.concatenate([w_out[0], w_down[0]], axis=0).astype(BF16)
    cw_sh = jnp.pad(conv_w[0, :, 0, :], ((0, 1), (0, 0)))
    own = lambda full, shard: lax.dynamic_update_index_in_dim(full, shard, chip, 0)
    cols = lambda w4: jnp.transpose(w4, (1, 0, 2)).reshape(w4.shape[1], N_CHIPS * w4.shape[2])
    wa4, cw4 = _gather_weights([a_sh, cw_sh], [False, False])
    wa = own(wa4, a_sh)
    cwf = cols(own(cw4, cw_sh))

    a_bf, uc, qkv = _in_proj(x2, g_pre_mix, wa, tm_big)
    conv_out, yconv = _conv_fwd(uc, cwf, conv_b, conv_ln_g, conv_ln_b, tm_big)
    o, wb4, wc4 = _attn_fwd(qkv, t_att_fwd, tk_att, [b_sh, c_sh], [True, False])
    wb4, wc4 = own(wb4, b_sh), own(wc4, c_sh)
    wg, wu = cols(wb4[:, 0]), cols(wb4[:, 1])
    wo = wc4[:, :OUT_SH].reshape(D_MODEL, D_MODEL)
    wd = wc4[:, OUT_SH:].reshape(D_FF, D_MODEL)
    mixed, yv, h1, f_in = _out_proj(conv_out, o, ag, wo, x2, g_post_mix, g_pre_ffn, tm_big)
    df, dh2, dg4, loss_part, gt_bf, up_bf, act = _ffn_fwd(f_in, h1, tgt, wg, wu, wd, g_post_ffn, tm_ffn)

    dgt, dup, dh1, dy, dg3, dg2 = _ffn_bwd(gt_bf, up_bf, df, dh2, h1, yv, wg, wu, wd, g_pre_ffn, g_post_mix, tm_ffn)
    dco, do, dag = _out_bwd(dy, o, ag, wo, tm_big)
    ts = min(512, S)
    gw_out = _matmul_tn(mixed, dy, D_MODEL, ts, "grad_w_out")
    gw_gate = _matmul_tn(f_in, dgt, D_MODEL, ts, "grad_w_gate")
    gw_up = _matmul_tn(f_in, dup, D_MODEL, ts, "grad_w_up")
    gw_down = _matmul_tn(act, df, D_FF // 2, ts, "grad_w_down")

    by_cols = lambda g: jnp.transpose(g.reshape(2, D_MODEL // 2, N_CHIPS, -1), (2, 0, 1, 3))
    by_rows = lambda g: g.reshape(N_CHIPS, 2, g.shape[0] // (2 * N_CHIPS), g.shape[1])
    c_arr = core.reshape(1).astype(jnp.int32)

    def chip_partials(views, nms):
        landed = _sibling_halves(views, "grad_sibling_halves_" + nms[0])
        return [_add_half(c_arr, g, l, "grad_half_" + nm) for g, l, nm in zip(views, landed, nms)]

    early = ["w_gate", "w_up", "w_out", "w_down"]
    parts = chip_partials([by_cols(gw_gate), by_cols(gw_up), by_rows(gw_out), by_rows(gw_down)], early)
    dq, dk, dv, *slots = _attn_bwd(qkv, do, t_att_bwd, tk_att, parts)
    duc, dcw, dcb, dlg, dlb = _conv_bwd(uc, yconv, dco, cwf, conv_ln_g, conv_ln_b, tm_big)
    grad_x, du, dg1 = _in_bwd(duc, dq, dk, dv, x2, dh1, g_pre_mix, wa, tm_big)
    gw_in = _matmul_tn(a_bf, du, D_MODEL, ts, "grad_w_in", column_block=IN_SH)
    slots += _chip_scatter(chip_partials([gw_in.reshape(N_CHIPS, 2, D_MODEL // 2, IN_SH)], ["w_in"]))
    names = early + ["w_in"]
    halves = [_sum_chips(s, "grad_sum_" + nm) for s, nm in zip(slots, names)]
    others = _share_halves(halves)
    mine = dict(zip(names, halves))
    other = dict(zip(names, others))

    small = [dg1, dcb, dlg, dlb, dag, dg2, dg3, dg4]
    packed = jnp.concatenate([_rows8(s) for s in small] + [_rows8(dcw), _rows8(loss_part)], axis=0)
    red = _allreduce_small(packed)
    sizes = [D_MODEL, CONV_CH, CONV_CH, CONV_CH, CONV_CH, D_MODEL, D_MODEL, D_MODEL]
    g_small = [red[8 * k:8 * k + n // 128].reshape(1, n) for k, n in enumerate(sizes)]
    cw_red = red[64:64 + 128].reshape(HALO, CONV_CH)
    g_cw = lax.dynamic_slice(cw_red, (0, chip * 128), (HALO, 128))
    loss = red[192, 0]

    big = []
    for w, m, v, nm in [(w_in, m_w_in, v_w_in, "w_in"), (w_out, m_w_out, v_w_out, "w_out"),
                        (w_gate, m_w_gate, v_w_gate, "w_gate"), (w_up, m_w_up, v_w_up, "w_up"),
                        (w_down, m_w_down, v_w_down, "w_down")]:
        big.append(_adamw_halves(c_arr, w, mine[nm], other[nm], m, v, "adamw_" + nm))
    sm_w = [g_pre_mix, conv_b, conv_ln_g, conv_ln_b, ag, g_post_mix, g_pre_ffn, g_post_ffn]
    sm_m = [m_g_pre_mix, m_conv_b, m_conv_ln_g, m_conv_ln_b, m_attn_norm_g, m_g_post_mix, m_g_pre_ffn, m_g_post_ffn]
    sm_v = [v_g_pre_mix, v_conv_b, v_conv_ln_g, v_conv_ln_b, v_attn_norm_g, v_g_post_mix, v_g_pre_ffn, v_g_post_ffn]
    pad_cw = lambda a: jnp.pad(a[0, :, 0, :], ((0, 1), (0, 0)))

    def pack(vecs, cw):
        return jnp.concatenate([_rows8(a) for a in vecs] + [cw], axis=0)

    sd, smn, svn = _adamw(pack(sm_w, pad_cw(conv_w)), pack(g_small, g_cw), pack(sm_m, pad_cw(m_conv_w)),
                          pack(sm_v, pad_cw(v_conv_w)), "adamw_small")

    def unpack(p):
        vecs = [p[8 * k:8 * k + n // 128].reshape(1, n) for k, n in enumerate(sizes)]
        return vecs, p[64:64 + CONV_WIDTH].reshape(1, CONV_WIDTH, 1, 128)

    def ordered(vecs, cw, w_in_, w_out_, w_gate_, w_up_, w_down_):
        g1_, cb_, lg_, lb_, ag_, g2_, g3_, g4_ = vecs
        return [g1_, w_in_, cw, cb_, lg_, lb_, ag_.reshape(1, 8, HEAD_DIM), w_out_, g2_, g3_,
                w_gate_, w_up_, w_down_, g4_]

    grads = ordered(g_small, g_cw[:CONV_WIDTH].reshape(1, CONV_WIDTH, 1, 128), *[b[0] for b in big])
    outs = []
    for idx, p in enumerate((sd, smn, svn)):
        vecs, cw = unpack(p)
        outs += ordered(vecs, cw, *[b[idx + 1] for b in big])
    return (loss, grad_x, *grads, *outs)
```

```python
import functools

import jax
import jax.numpy as jnp
from jax import lax
from jax.experimental import pallas as pl
from jax.experimental.pallas import tpu as pltpu

F32 = jnp.float32
BF16 = jnp.bfloat16
MESH = pl.DeviceIdType.MESH

D_MODEL = 1024
CONV_CH = 512
CONV_WIDTH = 31
HEAD_DIM = 64
PAIR = 2 * HEAD_DIM
N_PAIRS = 4
D_FF = 2816
N_CHIPS = 4
IN_SH = 2560 // N_CHIPS
OUT_SH = D_MODEL // N_CHIPS
EPS = 1e-6
HALO = 32

ADAM_LR = 0.001
ADAM_B1 = 0.9
ADAM_B2 = 0.999
ADAM_EPS = 1e-08
ADAM_WD = 0.01
ADAM_STEP = 10

VMEM_LIMIT = 56 * 2 ** 20
VMEM_LIMIT_ATTN_BWD = 62 * 2 ** 20


def _cp(sem=None, vmem=VMEM_LIMIT):
    return pltpu.CompilerParams(dimension_semantics=sem, vmem_limit_bytes=vmem)


def _hbm():
    return pl.BlockSpec(memory_space=pltpu.HBM)


def _const_spec(shape):
    nd = len(shape)
    return pl.BlockSpec(shape, lambda *_: (0,) * nd, pipeline_mode=pl.Buffered(1))


def _dot(a, b):
    return jnp.dot(a, b, preferred_element_type=F32)


def _dot_nt(a, b):
    return lax.dot_general(a, b, (((1,), (1,)), ((), ())), preferred_element_type=F32)


def _dot_tn(a, b):
    return lax.dot_general(a, b, (((0,), (0,)), ((), ())), preferred_element_type=F32)


def _split3(x):
    b0 = x.astype(BF16)
    r1 = x - b0.astype(F32)
    b1 = r1.astype(BF16)
    b2 = (r1 - b1.astype(F32)).astype(BF16)
    return b0, b1, b2


def _sigmoid(x):
    return 1.0 / (1.0 + jnp.exp(-x))


def _head_mean(x, seg):
    b0, b1, b2 = _split3(x)
    return (_dot(b0, seg) + _dot(b1, seg) + _dot(b2, seg)) * (1.0 / HEAD_DIM)


def _seg_matrix(n):
    r = lax.broadcasted_iota(jnp.int32, (n, n), 0) // HEAD_DIM
    c = lax.broadcasted_iota(jnp.int32, (n, n), 1) // HEAD_DIM
    return (r == c).astype(BF16)


def _rms(x):
    return lax.rsqrt(jnp.mean(x * x, axis=-1, keepdims=True) + EPS)


def _rms_bwd(dy, n, r, g):
    dn = dy * g
    dx = r * (dn - n * jnp.mean(dn * n, axis=-1, keepdims=True))
    return dx, dy * n


class _GatherPlan:
    def __init__(self, srcs, outs, lead, ssem, rsem):
        self.srcs, self.outs, self.lead, self.ssem, self.rsem = srcs, outs, lead, ssem, rsem
        x, y, self.c = lax.axis_index("x"), lax.axis_index("y"), lax.axis_index("c")
        self.me = 2 * x + y
        self.sibling = (x, y, 1 - self.c)
        self.chips = [(1 - x, y), (x, 1 - y), (1 - x, 1 - y)]

    def _half(self, ref, i, h):
        if self.lead[i]:
            return ref.at[h]
        rows = ref.shape[0] // 2
        return ref.at[pl.ds(h * rows, rows)]

    def _ici(self, i, k, origin):
        return pltpu.make_async_remote_copy(
            src_ref=self._half(self.srcs[i], i, self.c), dst_ref=self._half(self.outs[i].at[origin], i, self.c),
            send_sem=self.ssem.at[6 * i + k], recv_sem=self.rsem.at[6 * i + k],
            device_id=(self.chips[k][0], self.chips[k][1], self.c), device_id_type=MESH)

    def _d2d(self, i, k, h):
        origin = 2 * self.chips[k][0] + self.chips[k][1]
        piece = self._half(self.outs[i].at[origin], i, h)
        return pltpu.make_async_remote_copy(
            src_ref=piece, dst_ref=piece, send_sem=self.ssem.at[6 * i + 3 + k],
            recv_sem=self.rsem.at[6 * i + 3 + k], device_id=self.sibling, device_id_type=MESH)

    def _each(self):
        return [(i, k) for i in range(len(self.srcs)) for k in range(3)]

    def start(self):
        for i, k in self._each():
            self._ici(i, k, self.me).start()

    def forward(self):
        for i, k in self._each():
            self._ici(i, k, 2 * self.chips[k][0] + self.chips[k][1]).wait_recv()
            self._d2d(i, k, self.c).start()

    def finish(self):
        for i, k in self._each():
            self._d2d(i, k, 1 - self.c).wait_recv()
        for i, k in self._each():
            self._ici(i, k, self.me).wait_send()
            self._d2d(i, k, self.c).wait_send()


def _gather_shapes(shards):
    return [jax.ShapeDtypeStruct((N_CHIPS,) + s.shape, s.dtype) for s in shards]


def _gather_weights(shards, lead):
    n = len(shards)

    def body(*refs):
        plan = _GatherPlan(refs[:n], refs[n:2 * n], lead, refs[2 * n], refs[2 * n + 1])
        plan.start()
        plan.forward()
        plan.finish()

    return pl.pallas_call(
        body, name="gather_weights", out_shape=_gather_shapes(shards),
        in_specs=[_hbm()] * n, out_specs=[_hbm()] * n,
        scratch_shapes=[pltpu.SemaphoreType.DMA((6 * n,)), pltpu.SemaphoreType.DMA((6 * n,))],
    )(*shards)


def _in_proj(x2, g1, wa, tm):
    S = x2.shape[0]

    def body(x_ref, g_ref, w_ref, a_ref, uc_ref, qkv_ref):
        x = x_ref[...]
        a = (x * _rms(x) * g_ref[...]).astype(BF16)
        a_ref[...] = a
        u = [_dot(a, w_ref[j]) for j in range(N_CHIPS)]
        uc_ref[:, 0:640] = u[0]
        uc_ref[:, 640:1024] = u[1][:, 0:384]
        qkv_ref[:, 0:256] = u[1][:, 384:640].astype(BF16)
        qkv_ref[:, 256:896] = u[2].astype(BF16)
        qkv_ref[:, 896:1536] = u[3].astype(BF16)

    return pl.pallas_call(
        body, name="in_proj", grid=(S // tm,),
        in_specs=[pl.BlockSpec((tm, D_MODEL), lambda i: (i, 0)), _const_spec((1, D_MODEL)),
                  _const_spec(wa.shape)],
        out_specs=[pl.BlockSpec((tm, D_MODEL), lambda i: (i, 0)),
                   pl.BlockSpec((tm, 2 * CONV_CH), lambda i: (i, 0)),
                   pl.BlockSpec((tm, 1536), lambda i: (i, 0))],
        out_shape=[jax.ShapeDtypeStruct((S, D_MODEL), BF16), jax.ShapeDtypeStruct((S, 2 * CONV_CH), F32),
                   jax.ShapeDtypeStruct((S, 1536), BF16)],
        compiler_params=_cp(("parallel",)),
    )(x2, g1, wa)


SUBLANES = 8


def _shift_copies(src_ref, sh_ref):
    rows = sh_ref.shape[1]
    for b in range(1, SUBLANES):
        sh_ref[b - 1] = src_ref[pl.ds(b, rows), :]


def _rows_at(src_ref, sh_ref, off, rows):
    a, b = divmod(off, SUBLANES)
    if b == 0:
        return src_ref[pl.ds(SUBLANES * a, rows), :]
    return sh_ref[b - 1, pl.ds(SUBLANES * a, rows), :]


def _conv_taps(cw_ref, src_ref, sh_ref, offs, rows):
    acc = None
    for w, off in enumerate(offs):
        term = cw_ref[w:w + 1, :] * _rows_at(src_ref, sh_ref, off, rows)
        acc = term if acc is None else acc + term
    return acc


def _glu(uc):
    return uc[:, :CONV_CH] * _sigmoid(uc[:, CONV_CH:])


def _conv_fwd(uc, cwf, cb, lg, lb, tm):
    S = uc.shape[0]
    hb = tm // HALO

    def body(uc_ref, prev_ref, cw_ref, cb_ref, lg_ref, lb_ref, out_ref, y_ref, glu_ref, sh_ref):
        i = pl.program_id(0)
        glu_ref[0:HALO, :] = jnp.where(i == 0, 0.0, _glu(prev_ref[...]))
        glu_ref[HALO:HALO + tm, :] = _glu(uc_ref[...])
        glu_ref[HALO + tm:HALO + tm + SUBLANES, :] = jnp.zeros((SUBLANES, CONV_CH), F32)
        _shift_copies(glu_ref, sh_ref)
        offs = [HALO - (CONV_WIDTH - 1) + w for w in range(CONV_WIDTH)]
        y = _conv_taps(cw_ref, glu_ref, sh_ref, offs, tm) + cb_ref[...]
        y_ref[...] = y
        mu = jnp.mean(y, axis=-1, keepdims=True)
        yc = y - mu
        rstd = lax.rsqrt(jnp.mean(yc * yc, axis=-1, keepdims=True) + EPS)
        ln = yc * rstd * lg_ref[...] + lb_ref[...]
        out_ref[...] = (ln * _sigmoid(ln)).astype(BF16)

    return pl.pallas_call(
        body, name="conv_fwd", grid=(S // tm,),
        in_specs=[pl.BlockSpec((tm, 2 * CONV_CH), lambda i: (i, 0)),
                  pl.BlockSpec((HALO, 2 * CONV_CH), lambda i: (jnp.maximum(i * hb - 1, 0), 0)),
                  _const_spec(cwf.shape), _const_spec((1, CONV_CH)), _const_spec((1, CONV_CH)),
                  _const_spec((1, CONV_CH))],
        out_specs=[pl.BlockSpec((tm, CONV_CH), lambda i: (i, 0))] * 2,
        out_shape=[jax.ShapeDtypeStruct((S, CONV_CH), BF16), jax.ShapeDtypeStruct((S, CONV_CH), F32)],
        scratch_shapes=[pltpu.VMEM((HALO + tm + SUBLANES, CONV_CH), F32),
                        pltpu.VMEM((SUBLANES - 1, HALO + tm, CONV_CH), F32)],
        compiler_params=_cp(("parallel",)),
    )(uc, uc, cwf, cb, lg, lb)


def _lane_mask(h):
    lane = lax.broadcasted_iota(jnp.int32, (1, PAIR), 1)
    return (lane >= HEAD_DIM * h) & (lane < HEAD_DIM * (h + 1))


def _neg_abs(x):
    bits = lax.bitcast_convert_type(x, jnp.uint32) | jnp.uint32(0x80000000)
    return lax.bitcast_convert_type(bits, F32)


def _tri_dot(x, m):
    return _dot(x.astype(BF16), m)


MASKED = -1e30
KEY_BLOCKS = 4


def _running_sums(x, m, reverse, start=None):
    t = m.shape[0]
    blocks = x.shape[1] // t
    order = range(blocks - 1, -1, -1) if reverse else range(blocks)
    out = [None] * blocks
    carry = start
    for b in order:
        xb = x[:, b * t:(b + 1) * t]
        cb = _tri_dot(xb, m)
        out[b] = cb if carry is None else cb + carry
        rs = jnp.sum(xb, axis=1, keepdims=True)
        carry = rs if carry is None else carry + rs
    return jnp.concatenate(out, axis=1), carry


def _sb_tile(z, r, m_suf):
    sp = jnp.maximum(z, 0.0) + jnp.log(1.0 + jnp.exp(_neg_abs(z)))
    c, rs = _running_sums(sp, m_suf, reverse=True, start=r)
    return jnp.exp(z - c), sp, rs


def _scores(qm, kt, mask):
    z = _dot_nt(qm, kt)
    return z if mask is None else jnp.where(mask, z, MASKED)


def _causal_mask(i, sb, t, tk, w=None):
    w = tk if w is None else w
    row = lax.broadcasted_iota(jnp.int32, (t, w), 0) + i * t
    col = lax.broadcasted_iota(jnp.int32, (t, w), 1) + sb * tk
    return col < row


def _sweep_pairs(count, tile):
    def step(n, carry):
        tile(2 * n)
        tile(2 * n + 1)
        return carry

    lax.fori_loop(0, count // 2, step, 0)
    pl.when(lax.rem(count, 2) == 1)(lambda: tile(count - 1))


def _sweep(first, count, down, fetch, load, work, fetched=False):
    lo, hi = (first - count, first) if down else (first, first + count)
    tile = lambda j: jnp.clip(first - j if down else first + j, lo, hi)
    if not fetched:
        fetch(first, 0, True)

    def step(n, carry):
        j = 2 * n
        vals = load(0)
        fetch(tile(j + 1), 1, False)
        work(tile(j), vals)
        vals = load(1)
        fetch(tile(j + 2), 0, False)
        work(tile(j + 1), vals)
        return carry

    lax.fori_loop(0, (count + 1) // 2, step, 0)

    @pl.when(lax.rem(count, 2) == 0)
    def _():
        work(tile(count), load(0))


def _suffix_matrix(t, prefix=False):
    row = lax.broadcasted_iota(jnp.int32, (t, t), 0)
    col = lax.broadcasted_iota(jnp.int32, (t, t), 1)
    return ((row <= col) if prefix else (row >= col)).astype(BF16)


def _attn_fwd(qkv, t, tk, shards, lead):
    S = qkv.shape[0]
    assert t == tk, "the forward cuts its diagonal tile by blocks: query block and key tile must match"

    ng = len(shards)
    nq = S // t

    def body(*refs):
        q_ref, k_ref, v_ref = refs[:3]
        o_ref = refs[3 + ng]
        acc_ref, r_ref, z_buf, ssem, rsem = refs[4 + 2 * ng:]
        p = pl.program_id(0)
        i = pl.program_id(1)
        plan = _GatherPlan(refs[3:3 + ng], refs[4 + ng:4 + 2 * ng], lead, ssem, rsem)
        pl.when((p == 0) & (i == 0))(plan.start)
        pl.when((p == 1) & (i == 0))(plan.forward)
        blk = tk // KEY_BLOCKS
        m_suf = _suffix_matrix(blk)
        q = q_ref[...]
        hms = [_lane_mask(h) for h in range(2)]
        qms = [jnp.where(hm, q, 0) * 0.125 for hm in hms]

        def rows(sb, w=tk):
            return pl.ds(pl.multiple_of(sb * tk, tk), w)

        def fetch(sb, slot, first):
            kt = k_ref[rows(sb), :]
            for h in range(2):
                z_buf[slot, h] = _scores(qms[h], kt, None)

        fetch(jnp.maximum(i - 1, 0), 0, True)
        for rb in range(KEY_BLOCKS):
            w = (rb + 1) * blk
            part = slice(rb * blk, w)
            kt = k_ref[rows(i, w), :]
            vt = v_ref[rows(i, w), :]
            row = lax.broadcasted_iota(jnp.int32, (blk, w), 0) + rb * blk
            mask = lax.broadcasted_iota(jnp.int32, (blk, w), 1) < row
            out = jnp.zeros((blk, PAIR), F32)
            for h in range(2):
                a_loc, _, rs = _sb_tile(_scores(qms[h][part], kt, mask), None, m_suf)
                out = out + _dot(a_loc.astype(BF16), jnp.where(hms[h], vt, 0))
                r_ref[h, part] = rs
            acc_ref[part, :] = out

        def load(slot):
            return [z_buf[slot, h] for h in range(2)]

        def work(sb, zs):
            vt = v_ref[rows(sb), :]
            for h in range(2):
                a_loc, _, rs = _sb_tile(zs[h], None, m_suf)
                r = r_ref[h]
                acc_ref[...] += _dot(a_loc.astype(BF16), jnp.where(hms[h], vt, 0)) * jnp.exp(-r)
                r_ref[h] = r + rs

        pl.when(i >= 1)(lambda: _sweep(i - 1, i - 1, True, fetch, load, work, fetched=True))
        o_ref[...] = acc_ref[...]
        pl.when((p == N_PAIRS - 1) & (i == nq - 1))(plan.finish)

    return pl.pallas_call(
        body, name="attn_fwd", grid=(N_PAIRS, nq),
        in_specs=[pl.BlockSpec((t, PAIR), lambda p, i: (i, p)),
                  pl.BlockSpec((S, PAIR), lambda p, i: (0, N_PAIRS + p)),
                  pl.BlockSpec((S, PAIR), lambda p, i: (0, 2 * N_PAIRS + p))] + [_hbm()] * ng,
        out_specs=[pl.BlockSpec((t, PAIR), lambda p, i: (i, p))] + [_hbm()] * ng,
        out_shape=[jax.ShapeDtypeStruct((S, N_PAIRS * PAIR), F32)] + _gather_shapes(shards),
        scratch_shapes=[pltpu.VMEM((t, PAIR), F32), pltpu.VMEM((2, t, 1), F32),
                        pltpu.VMEM((2, 2, t, tk), F32),
                        pltpu.SemaphoreType.DMA((6 * ng,)), pltpu.SemaphoreType.DMA((6 * ng,))],
        compiler_params=_cp(("arbitrary", "arbitrary")),
    )(qkv, qkv, qkv, *shards)


def _out_proj(conv_out, o, ag, wc, x2, g2, g3, tm):
    S = o.shape[0]

    def body(co_ref, o_ref, ag_ref, w_ref, x_ref, g2_ref, g3_ref, mix_ref, y_ref, h1_ref, fin_ref):
        seg = _seg_matrix(CONV_CH)
        o = o_ref[...]
        att = (o * lax.rsqrt(_head_mean(o * o, seg) + EPS) * ag_ref[...]).astype(BF16)
        co = co_ref[...]
        mix_ref[:, :CONV_CH] = co
        mix_ref[:, CONV_CH:] = att
        y = _dot(co, w_ref[0:CONV_CH, :]) + _dot(att, w_ref[CONV_CH:, :])
        y_ref[...] = y
        h1 = x_ref[...] + y * _rms(y) * g2_ref[...]
        h1_ref[...] = h1
        fin_ref[...] = (h1 * _rms(h1) * g3_ref[...]).astype(BF16)

    row = lambda w: pl.BlockSpec((tm, w), lambda i: (i, 0))
    return pl.pallas_call(
        body, name="out_proj", grid=(S // tm,),
        in_specs=[row(CONV_CH), row(CONV_CH), _const_spec((1, CONV_CH)), _const_spec(wc.shape),
                  row(D_MODEL), _const_spec((1, D_MODEL)), _const_spec((1, D_MODEL))],
        out_specs=[row(D_MODEL)] * 4,
        out_shape=[jax.ShapeDtypeStruct((S, D_MODEL), BF16), jax.ShapeDtypeStruct((S, D_MODEL), F32),
                   jax.ShapeDtypeStruct((S, D_MODEL), F32), jax.ShapeDtypeStruct((S, D_MODEL), BF16)],
        compiler_params=_cp(("parallel",)),
    )(conv_out, o, ag, wc, x2, g2, g3)


def _ffn_fwd(f_in, h1, tgt, wg, wu, wd, g4, tm):
    S = f_in.shape[0]

    def body(fin_ref, h1_ref, tgt_ref, wg_ref, wu_ref, wd_ref, g4_ref, df_ref, dh2_ref, dg4_ref, loss_ref,
             gt_ref, up_ref, act_ref):
        i = pl.program_id(0)
        fin = fin_ref[...]
        gt = _dot(fin, wg_ref[...])
        up = _dot(fin, wu_ref[...])
        act = (gt * _sigmoid(gt) * up).astype(BF16)
        gt_ref[...] = gt.astype(BF16)
        up_ref[...] = up.astype(BF16)
        act_ref[...] = act
        f = _dot(act, wd_ref[...])
        r = _rms(f)
        n = f * r
        g4 = g4_ref[...]
        err = h1_ref[...] + n * g4 - tgt_ref[...]
        dh2 = err * (1.0 / D_MODEL)
        dh2_ref[...] = dh2
        df, dg = _rms_bwd(dh2, n, r, g4)
        df_ref[...] = df.astype(BF16)

        @pl.when(i == 0)
        def _():
            dg4_ref[...] = jnp.zeros_like(dg4_ref)
            loss_ref[...] = jnp.zeros_like(loss_ref)

        dg4_ref[...] += jnp.sum(dg, axis=0, keepdims=True)
        part = jnp.sum(jnp.sum(err * err, axis=1, keepdims=True), axis=0, keepdims=True)
        loss_ref[...] += part * (0.5 / D_MODEL)

    row = lambda w: pl.BlockSpec((tm, w), lambda i: (i, 0))
    return pl.pallas_call(
        body, name="ffn_fwd", grid=(S // tm,),
        in_specs=[row(D_MODEL), row(D_MODEL), row(D_MODEL), _const_spec(wg.shape), _const_spec(wu.shape),
                  _const_spec(wd.shape), _const_spec((1, D_MODEL))],
        out_specs=[row(D_MODEL), row(D_MODEL), pl.BlockSpec((1, D_MODEL), lambda i: (0, 0)),
                   pl.BlockSpec((1, 128), lambda i: (0, 0)), row(D_FF), row(D_FF), row(D_FF)],
        out_shape=[jax.ShapeDtypeStruct((S, D_MODEL), BF16), jax.ShapeDtypeStruct((S, D_MODEL), F32),
                   jax.ShapeDtypeStruct((1, D_MODEL), F32), jax.ShapeDtypeStruct((1, 128), F32)]
        + [jax.ShapeDtypeStruct((S, D_FF), BF16)] * 3,
        compiler_params=_cp(("arbitrary",)),
    )(f_in, h1, tgt, wg, wu, wd, g4)


def _ffn_bwd(gt_bf, up_bf, df, dh2, h1, yv, wg, wu, wd, g3, g2, tm):
    S = df.shape[0]

    def body(gt_ref, up_ref, df_ref, dh2_ref, h1_ref, y_ref, wg_ref, wu_ref, wd_ref, g3_ref, g2_ref,
             dgt_ref, dup_ref, dh1_ref, dy_ref, dg3_ref, dg2_ref):
        i = pl.program_id(0)
        df = df_ref[...]
        gt = gt_ref[...].astype(F32)
        up = up_ref[...].astype(F32)
        sg = _sigmoid(gt)
        silu = gt * sg
        dact = _dot_nt(df, wd_ref[...])
        dgt = (dact * up * (sg * (1.0 + gt * (1.0 - sg)))).astype(BF16)
        dup = (dact * silu).astype(BF16)
        dgt_ref[...] = dgt
        dup_ref[...] = dup
        dfin = _dot_nt(dgt, wg_ref[...]) + _dot_nt(dup, wu_ref[...])
        h1 = h1_ref[...]
        r3 = _rms(h1)
        dh1_n, dg3 = _rms_bwd(dfin, h1 * r3, r3, g3_ref[...])
        dh1 = dh2_ref[...] + dh1_n
        dh1_ref[...] = dh1
        y = y_ref[...]
        r2 = _rms(y)
        dy, dg2 = _rms_bwd(dh1, y * r2, r2, g2_ref[...])
        dy_ref[...] = dy.astype(BF16)

        @pl.when(i == 0)
        def _():
            dg3_ref[...] = jnp.zeros_like(dg3_ref)
            dg2_ref[...] = jnp.zeros_like(dg2_ref)

        dg3_ref[...] += jnp.sum(dg3, axis=0, keepdims=True)
        dg2_ref[...] += jnp.sum(dg2, axis=0, keepdims=True)

    row = lambda w: pl.BlockSpec((tm, w), lambda i: (i, 0))
    vec = pl.BlockSpec((1, D_MODEL), lambda i: (0, 0))
    return pl.pallas_call(
        body, name="ffn_bwd", grid=(S // tm,),
        in_specs=[row(D_FF), row(D_FF)] + [row(D_MODEL)] * 4
        + [_const_spec(wg.shape), _const_spec(wu.shape), _const_spec(wd.shape),
           _const_spec((1, D_MODEL)), _const_spec((1, D_MODEL))],
        out_specs=[row(D_FF), row(D_FF), row(D_MODEL), row(D_MODEL), vec, vec],
        out_shape=[jax.ShapeDtypeStruct((S, D_FF), BF16)] * 2
        + [jax.ShapeDtypeStruct((S, D_MODEL), F32), jax.ShapeDtypeStruct((S, D_MODEL), BF16),
           jax.ShapeDtypeStruct((1, D_MODEL), F32), jax.ShapeDtypeStruct((1, D_MODEL), F32)],
        compiler_params=_cp(("arbitrary",)),
    )(gt_bf, up_bf, df, dh2, h1, yv, wg, wu, wd, g3, g2)


def _out_bwd(dy, o, ag, wc, tm):
    S = o.shape[0]

    def body(dy_ref, o_ref, ag_ref, w_ref, dco_ref, do_ref, dag_ref):
        i = pl.program_id(0)
        seg = _seg_matrix(CONV_CH)
        dy = dy_ref[...]
        dco_ref[...] = _dot_nt(dy, w_ref[0:CONV_CH, :])
        datt = _dot_nt(dy, w_ref[CONV_CH:, :])
        o = o_ref[...]
        r = lax.rsqrt(_head_mean(o * o, seg) + EPS)
        n = o * r
        dn = datt * ag_ref[...]
        do_ref[...] = (r * (dn - n * _head_mean(dn * n, seg))).astype(BF16)

        @pl.when(i == 0)
        def _():
            dag_ref[...] = jnp.zeros_like(dag_ref)

        dag_ref[...] += jnp.sum(datt * n, axis=0, keepdims=True)

    row = lambda w: pl.BlockSpec((tm, w), lambda i: (i, 0))
    return pl.pallas_call(
        body, name="out_bwd", grid=(S // tm,),
        in_specs=[row(D_MODEL), row(CONV_CH), _const_spec((1, CONV_CH)), _const_spec(wc.shape)],
        out_specs=[row(CONV_CH), row(CONV_CH), pl.BlockSpec((1, CONV_CH), lambda i: (0, 0))],
        out_shape=[jax.ShapeDtypeStruct((S, CONV_CH), F32), jax.ShapeDtypeStruct((S, CONV_CH), BF16),
                   jax.ShapeDtypeStruct((1, CONV_CH), F32)],
        compiler_params=_cp(("arbitrary",)),
    )(dy, o, ag, wc)


def _attn_bwd(qkv, do, t, tk, parts):
    S = qkv.shape[0]
    nk = S // tk
    ns = len(parts)

    def body(*refs):
        q_ref, k_ref, v_ref, do_ref = refs[:4]
        dq_ref, dk_hbm, dv_hbm = refs[4 + ns:7 + ns]
        g_buf, s_buf, r_ref, dq_acc, dk_ref, dv_ref, z_buf = refs[7 + 2 * ns:14 + 2 * ns]
        p = pl.program_id(0)
        i = pl.program_id(1)
        plan = _ScatterPlan(refs[4:4 + ns], refs[7 + ns:7 + 2 * ns], *refs[14 + 2 * ns:])
        pl.when((p == 0) & (i == 0))(plan.start)
        last = (i * t + t - 1) // tk

        @pl.when(i == 0)
        def _():
            dk_ref[...] = jnp.zeros_like(dk_ref)
            dv_ref[...] = jnp.zeros_like(dv_ref)

        m_suf = _suffix_matrix(tk // KEY_BLOCKS)
        m_pre = _suffix_matrix(tk // KEY_BLOCKS, prefix=True)
        q = q_ref[...]
        do = do_ref[...]
        hms = [_lane_mask(h) for h in range(2)]
        qms = [jnp.where(hm, q, 0) * 0.125 for hm in hms]
        doms = [jnp.where(hm, do, 0) for hm in hms]
        dq_acc[...] = jnp.zeros_like(dq_acc)
        r_ref[...] = jnp.zeros_like(r_ref)

        def keys(sb, w=tk):
            return pl.ds(pl.multiple_of(sb * tk, tk), w)

        def matmuls1(sb, w, diagonal):
            kt = k_ref[keys(sb, w), :]
            vt = v_ref[keys(sb, w), :]
            mask = _causal_mask(i, sb, t, tk, w) if diagonal else None
            return [(_scores(qms[h], kt, mask), _dot_nt(doms[h], vt)) for h in range(2)]

        def sweep1(sb, w, vals):
            dv = jnp.zeros((w, PAIR), F32)
            for h in range(2):
                z, da = vals[h]
                A, sp, r_ref[h] = _sb_tile(z, r_ref[h], m_suf)
                g_buf[h, sb, :, 0:w] = A * da
                s_buf[h, sb, :, 0:w] = 1.0 - jnp.exp(-sp)
                dv = dv + _dot_tn(A.astype(BF16), doms[h])
            dv_ref[keys(sb, w), :] += dv

        def sweep2(sb, w):
            kt = k_ref[keys(sb, w), :]
            dk = jnp.zeros((w, PAIR), F32)
            for h in range(2):
                g = g_buf[h, sb, :, 0:w]
                pre, r_ref[h] = _running_sums(g, m_pre, reverse=False, start=r_ref[h])
                dzb = (g - s_buf[h, sb, :, 0:w] * pre).astype(BF16)
                dq_acc[...] += _dot(dzb, jnp.where(hms[h], kt, 0))
                dk = dk + _dot_tn(dzb, qms[h])
            dk_ref[keys(sb, w), :] += dk

        def diagonal_tile(tile):
            for nb in range(1, KEY_BLOCKS + 1):
                pl.when(lax.rem(i, KEY_BLOCKS) == nb - 1)(functools.partial(tile, nb * t))

        def fetch1(sb, slot, first):
            kt = k_ref[keys(sb), :]
            for h in range(2):
                z_buf[slot, h] = _scores(qms[h], kt, None)

        def load1(slot):
            return [z_buf[slot, h] for h in range(2)]

        def work1(sb, zs):
            vt = v_ref[keys(sb), :]
            sweep1(sb, tk, [(zs[h], _dot_nt(doms[h], vt)) for h in range(2)])

        def diagonal1(w):
            fetch1(jnp.maximum(last - 1, 0), 0, True)
            sweep1(last, w, matmuls1(last, w, True))

        diagonal_tile(diagonal1)
        pl.when(last >= 1)(lambda: _sweep(last - 1, last - 1, True, fetch1, load1, work1, fetched=True))
        r_ref[...] = jnp.zeros_like(r_ref)
        _sweep_pairs(last, lambda sb: sweep2(sb, tk))
        diagonal_tile(lambda w: sweep2(last, w))
        dq_ref[...] = dq_acc[...] * 0.125

        @pl.when(i == S // t - 1)
        def _():
            cols = pl.ds(pl.multiple_of(p * PAIR, PAIR), PAIR)
            pltpu.sync_copy(dk_ref, dk_hbm.at[:, cols])
            pltpu.sync_copy(dv_ref, dv_hbm.at[:, cols])

        pl.when((p == N_PAIRS - 1) & (i == S // t - 1))(plan.finish)

    once = lambda cb: pl.BlockSpec((S, PAIR), cb, pipeline_mode=pl.Buffered(1))
    return pl.pallas_call(
        body, name="attn_bwd", grid=(N_PAIRS, S // t),
        in_specs=[pl.BlockSpec((t, PAIR), lambda p, i: (i, p)),
                  once(lambda p, i: (0, N_PAIRS + p)), once(lambda p, i: (0, 2 * N_PAIRS + p)),
                  pl.BlockSpec((t, PAIR), lambda p, i: (i, p))] + [_hbm()] * ns,
        out_specs=[pl.BlockSpec((t, PAIR), lambda p, i: (i, p)), _hbm(), _hbm()] + [_hbm()] * ns,
        out_shape=[jax.ShapeDtypeStruct((S, N_PAIRS * PAIR), F32)] * 3
        + [jax.ShapeDtypeStruct(pt.shape, pt.dtype) for pt in parts],
        scratch_shapes=[pltpu.VMEM((2, nk, t, tk), F32), pltpu.VMEM((2, nk, t, tk), F32),
                        pltpu.VMEM((2, t, 1), F32), pltpu.VMEM((t, PAIR), F32),
                        pltpu.VMEM((S, PAIR), F32), pltpu.VMEM((S, PAIR), F32),
                        pltpu.VMEM((2, 2, t, tk), F32)] + _scatter_sems(ns),
        compiler_params=_cp(("arbitrary", "arbitrary"), vmem=VMEM_LIMIT_ATTN_BWD),
    )(qkv, qkv, qkv, do, *parts)


def _conv_bwd(uc, yconv, dco, cwf, lg, lb, tm):
    S = uc.shape[0]
    hb = tm // HALO
    nb = S // tm
    ext = tm + HALO

    def body(uc_ref, prev_ref, y_ref, ynext_ref, dco_ref, dnext_ref, cw_ref, lg_ref, lb_ref,
             duc_ref, dcw_ref, dcb_ref, dlg_ref, dlb_ref, glu_ref, dyc_ref, shg_ref, shd_ref):
        i = pl.program_id(0)
        last = i == nb - 1

        @pl.when(i == 0)
        def _():
            for ref in (dcw_ref, dcb_ref, dlg_ref, dlb_ref):
                ref[...] = jnp.zeros_like(ref)

        uc = uc_ref[...]
        glu_ref[0:HALO, :] = jnp.where(i == 0, 0.0, _glu(prev_ref[...]))
        glu_ref[HALO:ext, :] = _glu(uc)
        glu_ref[ext:ext + SUBLANES, :] = jnp.zeros((SUBLANES, CONV_CH), F32)
        _shift_copies(glu_ref, shg_ref)
        fwd_offs = [HALO - (CONV_WIDTH - 1) + w for w in range(CONV_WIDTH)]
        y = jnp.concatenate([y_ref[...], ynext_ref[...]], axis=0)
        mu = jnp.mean(y, axis=-1, keepdims=True)
        yc = y - mu
        rstd = lax.rsqrt(jnp.mean(yc * yc, axis=-1, keepdims=True) + EPS)
        yhat = yc * rstd
        lg = lg_ref[...]
        ln = yhat * lg + lb_ref[...]
        sg = _sigmoid(ln)
        dout = jnp.concatenate([dco_ref[...], jnp.where(last, 0.0, dnext_ref[...])], axis=0)
        dln = dout * (sg * (1.0 + ln * (1.0 - sg)))
        dyh = dln * lg
        dyc = rstd * (dyh - jnp.mean(dyh, axis=-1, keepdims=True)
                      - yhat * jnp.mean(dyh * yhat, axis=-1, keepdims=True))
        dyc_ref[0:ext, :] = dyc
        dyc_ref[ext:ext + SUBLANES, :] = jnp.zeros((SUBLANES, CONV_CH), F32)
        _shift_copies(dyc_ref, shd_ref)
        dlg_ref[...] += jnp.sum((dln * yhat)[0:tm], axis=0, keepdims=True)
        dlb_ref[...] += jnp.sum(dln[0:tm], axis=0, keepdims=True)
        dcb_ref[...] += jnp.sum(dyc[0:tm], axis=0, keepdims=True)
        dglu = _conv_taps(cw_ref, dyc_ref, shd_ref, [CONV_WIDTH - 1 - w for w in range(CONV_WIDTH)], tm)
        d0 = dyc[0:tm]
        for w, off in enumerate(fwd_offs):
            dcw_ref[w:w + 1, :] += jnp.sum(d0 * _rows_at(glu_ref, shg_ref, off, tm), axis=0, keepdims=True)
        val, gate = uc[:, :CONV_CH], uc[:, CONV_CH:]
        sgate = _sigmoid(gate)
        duc_ref[:, :CONV_CH] = (dglu * sgate).astype(BF16)
        duc_ref[:, CONV_CH:] = (dglu * val * sgate * (1.0 - sgate)).astype(BF16)

    vec = pl.BlockSpec((1, CONV_CH), lambda i: (0, 0))
    nxt = lambda i: (jnp.minimum((i + 1) * hb, S // HALO - 1), 0)
    return pl.pallas_call(
        body, name="conv_bwd", grid=(nb,),
        in_specs=[pl.BlockSpec((tm, 2 * CONV_CH), lambda i: (i, 0)),
                  pl.BlockSpec((HALO, 2 * CONV_CH), lambda i: (jnp.maximum(i * hb - 1, 0), 0)),
                  pl.BlockSpec((tm, CONV_CH), lambda i: (i, 0)), pl.BlockSpec((HALO, CONV_CH), nxt),
                  pl.BlockSpec((tm, CONV_CH), lambda i: (i, 0)), pl.BlockSpec((HALO, CONV_CH), nxt),
                  _const_spec(cwf.shape), _const_spec((1, CONV_CH)), _const_spec((1, CONV_CH))],
        out_specs=[pl.BlockSpec((tm, 2 * CONV_CH), lambda i: (i, 0)),
                   pl.BlockSpec(cwf.shape, lambda i: (0, 0)), vec, vec, vec],
        out_shape=[jax.ShapeDtypeStruct((S, 2 * CONV_CH), BF16), jax.ShapeDtypeStruct(cwf.shape, F32)]
        + [jax.ShapeDtypeStruct((1, CONV_CH), F32)] * 3,
        scratch_shapes=[pltpu.VMEM((ext + SUBLANES, CONV_CH), F32), pltpu.VMEM((ext + SUBLANES, CONV_CH), F32),
                        pltpu.VMEM((SUBLANES - 1, ext, CONV_CH), F32),
                        pltpu.VMEM((SUBLANES - 1, ext, CONV_CH), F32)],
        compiler_params=_cp(("arbitrary",)),
    )(uc, uc, yconv, yconv, dco, dco, cwf, lg, lb)


def _in_bwd(duc, dq, dk, dv, x2, dh1, g1, wa, tm):
    S = x2.shape[0]

    def body(duc_ref, dq_ref, dk_ref, dv_ref, x_ref, dh1_ref, g_ref, w_ref, gx_ref, du_ref, dg_ref):
        i = pl.program_id(0)
        du = jnp.concatenate([duc_ref[...], dq_ref[...].astype(BF16), dk_ref[...].astype(BF16),
                              dv_ref[...].astype(BF16)], axis=1)
        du_ref[...] = du
        da = _dot_nt(du[:, 0:IN_SH], w_ref[0])
        for j in range(1, N_CHIPS):
            da = da + _dot_nt(du[:, IN_SH * j:IN_SH * (j + 1)], w_ref[j])
        x = x_ref[...]
        r = _rms(x)
        dx, dg = _rms_bwd(da, x * r, r, g_ref[...])
        gx_ref[...] = dh1_ref[...] + dx

        @pl.when(i == 0)
        def _():
            dg_ref[...] = jnp.zeros_like(dg_ref)

        dg_ref[...] += jnp.sum(dg, axis=0, keepdims=True)

    row = lambda w: pl.BlockSpec((tm, w), lambda i: (i, 0))
    return pl.pallas_call(
        body, name="in_bwd", grid=(S // tm,),
        in_specs=[row(2 * CONV_CH), row(CONV_CH), row(CONV_CH), row(CONV_CH), row(D_MODEL), row(D_MODEL),
                  _const_spec((1, D_MODEL)), _const_spec(wa.shape)],
        out_specs=[pl.BlockSpec((None, tm, D_MODEL), lambda i: (0, i, 0)), row(2560),
                   pl.BlockSpec((1, D_MODEL), lambda i: (0, 0))],
        out_shape=[jax.ShapeDtypeStruct((1, S, D_MODEL), F32), jax.ShapeDtypeStruct((S, 2560), BF16),
                   jax.ShapeDtypeStruct((1, D_MODEL), F32)],
        compiler_params=_cp(("arbitrary",)),
    )(duc, dq, dk, dv, x2, dh1, g1, wa)


def _matmul_tn(xm, ym, tm, ts, name, column_block=None):
    S, M = xm.shape
    N = ym.shape[1]

    def body(x_ref, y_ref, o_ref):
        @pl.when(pl.program_id(1) == 0)
        def _():
            o_ref[...] = jnp.zeros_like(o_ref)

        xt = x_ref[...].T
        if column_block is None:
            o_ref[...] += _dot(xt, y_ref[...])
        else:
            for j in range(N // column_block):
                o_ref[j] += _dot(xt, y_ref[:, column_block * j:column_block * (j + 1)])

    if column_block is None:
        out_spec = pl.BlockSpec((tm, N), lambda m, s: (m, 0))
        out_shape = jax.ShapeDtypeStruct((M, N), F32)
    else:
        out_spec = pl.BlockSpec((N // column_block, tm, column_block), lambda m, s: (0, m, 0))
        out_shape = jax.ShapeDtypeStruct((N // column_block, M, column_block), F32)
    return pl.pallas_call(
        body, name=name, grid=(M // tm, S // ts),
        in_specs=[pl.BlockSpec((ts, tm), lambda m, s: (s, m)), pl.BlockSpec((ts, N), lambda m, s: (s, 0))],
        out_specs=out_spec, out_shape=out_shape,
        compiler_params=_cp(("parallel", "arbitrary")),
    )(xm, ym)


def _sibling_halves(grads, name):
    n = len(grads)

    def body(*refs):
        ins, outs, ssem, rsem = refs[:n], refs[n:2 * n], refs[2 * n], refs[2 * n + 1]
        x, y, c = lax.axis_index("x"), lax.axis_index("y"), lax.axis_index("c")
        copies = []
        for k in range(n):
            for j in range(N_CHIPS):
                copies.append(pltpu.make_async_remote_copy(
                    src_ref=ins[k].at[j, 1 - c], dst_ref=outs[k].at[j],
                    send_sem=ssem.at[N_CHIPS * k + j], recv_sem=rsem.at[N_CHIPS * k + j],
                    device_id=(x, y, 1 - c), device_id_type=MESH))
        for cp in copies:
            cp.start()
        for cp in copies:
            cp.wait()

    shapes = [jax.ShapeDtypeStruct((g.shape[0],) + g.shape[2:], F32) for g in grads]
    return pl.pallas_call(
        body, name=name, out_shape=shapes,
        in_specs=[_hbm()] * n, out_specs=[_hbm()] * n,
        scratch_shapes=[pltpu.SemaphoreType.DMA((N_CHIPS * n,)), pltpu.SemaphoreType.DMA((N_CHIPS * n,))],
    )(*grads)


def _add_half(c_arr, g, landed, name):
    def body(c_ref, g_ref, l_ref, o_ref):
        o_ref[...] = (g_ref[...] + l_ref[...]).astype(BF16)

    rows, n = g.shape[2], g.shape[3]
    grid = (N_CHIPS,)
    g_spec = pl.BlockSpec((None, None, rows, n), lambda j, c: (j, c[0], 0, 0))
    l_spec = pl.BlockSpec((None, rows, n), lambda j, c: (j, 0, 0))
    return pl.pallas_call(
        body, name=name,
        grid_spec=pltpu.PrefetchScalarGridSpec(num_scalar_prefetch=1, grid=grid, in_specs=[g_spec, l_spec],
                                               out_specs=l_spec),
        out_shape=jax.ShapeDtypeStruct(landed.shape, BF16),
        compiler_params=_cp(("parallel",)),
    )(c_arr, g, landed)


class _ScatterPlan:
    def __init__(self, ins, outs, lsem, ssem, rsem):
        x, y, c = lax.axis_index("x"), lax.axis_index("y"), lax.axis_index("c")
        me = 2 * x + y
        self.copies = []
        for k in range(len(ins)):
            self.copies.append(pltpu.make_async_copy(ins[k].at[me], outs[k].at[me], lsem.at[k]))
            for r, chip in enumerate([(1 - x, y), (x, 1 - y), (1 - x, 1 - y)]):
                self.copies.append(pltpu.make_async_remote_copy(
                    src_ref=ins[k].at[2 * chip[0] + chip[1]], dst_ref=outs[k].at[me],
                    send_sem=ssem.at[3 * k + r], recv_sem=rsem.at[3 * k + r],
                    device_id=(chip[0], chip[1], c), device_id_type=MESH))

    def start(self):
        for cp in self.copies:
            cp.start()

    def finish(self):
        for cp in self.copies:
            cp.wait()


def _scatter_sems(n):
    return [pltpu.SemaphoreType.DMA((n,)), pltpu.SemaphoreType.DMA((3 * n,)), pltpu.SemaphoreType.DMA((3 * n,))]


def _chip_scatter(parts):
    n = len(parts)

    def body(*refs):
        plan = _ScatterPlan(refs[:n], refs[n:2 * n], *refs[2 * n:])
        plan.start()
        plan.finish()

    shapes = [jax.ShapeDtypeStruct(p.shape, p.dtype) for p in parts]
    return pl.pallas_call(
        body, name="grad_chip_scatter", out_shape=shapes,
        in_specs=[_hbm()] * n, out_specs=[_hbm()] * n, scratch_shapes=_scatter_sems(n),
    )(*parts)


def _sum_chips(landed, name):
    _, rows, n = landed.shape
    tr = 256 if rows % 256 == 0 else rows

    def body(a_ref, b_ref, c_ref, d_ref, o_ref):
        f = lambda ref: ref[...].astype(F32)
        o_ref[...] = ((f(a_ref) + f(b_ref)) + f(c_ref)) + f(d_ref)

    specs = [pl.BlockSpec((None, tr, n), functools.partial(lambda i, j: (j, i, 0), j=j)) for j in range(N_CHIPS)]
    return pl.pallas_call(
        body, name=name, grid=(rows // tr,), in_specs=specs,
        out_specs=pl.BlockSpec((tr, n), lambda i: (i, 0)),
        out_shape=jax.ShapeDtypeStruct((rows, n), F32),
        compiler_params=_cp(("parallel",)),
    )(landed, landed, landed, landed)


def _share_halves(halves):
    n = len(halves)

    def body(*refs):
        ins, outs = refs[:n], refs[n:2 * n]
        ssem, rsem = refs[2 * n:]
        x, y, c = lax.axis_index("x"), lax.axis_index("y"), lax.axis_index("c")
        copies = [pltpu.make_async_remote_copy(
            src_ref=ins[k], dst_ref=outs[k], send_sem=ssem.at[k], recv_sem=rsem.at[k],
            device_id=(x, y, 1 - c), device_id_type=MESH) for k in range(n)]
        for cp in copies:
            cp.start()
        for cp in copies:
            cp.wait()

    shapes = [jax.ShapeDtypeStruct(h.shape, F32) for h in halves]
    return pl.pallas_call(
        body, name="grad_share_halves", out_shape=shapes,
        in_specs=[_hbm()] * n, out_specs=[_hbm()] * n,
        scratch_shapes=[pltpu.SemaphoreType.DMA((n,)), pltpu.SemaphoreType.DMA((n,))],
    )(*halves)


def _allreduce_small(packed):
    rows, n = packed.shape

    def body(in_ref, out_ref, land_ref, ssem, rsem):
        x, y, c = lax.axis_index("x"), lax.axis_index("y"), lax.axis_index("c")
        me = 4 * x + 2 * y + c
        land_ref[me] = in_ref[...]
        copies = []
        for r in range(1, 8):
            tx = 1 - x if r & 4 else x
            ty = 1 - y if r & 2 else y
            tc = 1 - c if r & 1 else c
            cp = pltpu.make_async_remote_copy(
                src_ref=in_ref, dst_ref=land_ref.at[me], send_sem=ssem.at[r - 1], recv_sem=rsem.at[r - 1],
                device_id=(tx, ty, tc), device_id_type=MESH)
            cp.start()
            copies.append(cp)
        for cp in copies:
            cp.wait()
        acc = land_ref[0]
        for k in range(1, 8):
            acc = acc + land_ref[k]
        out_ref[...] = acc

    return pl.pallas_call(
        body, name="allreduce_small", out_shape=jax.ShapeDtypeStruct((rows, n), F32),
        in_specs=[pl.BlockSpec(memory_space=pltpu.VMEM)], out_specs=pl.BlockSpec(memory_space=pltpu.VMEM),
        scratch_shapes=[pltpu.VMEM((8, rows, n), F32), pltpu.SemaphoreType.DMA((7,)),
                        pltpu.SemaphoreType.DMA((7,))],
    )(packed)


def _adamw_math(w, g, m, v):
    m = ADAM_B1 * m + (1.0 - ADAM_B1) * g
    v = ADAM_B2 * v + (1.0 - ADAM_B2) * (g * g)
    m_hat = m / (1.0 - ADAM_B1 ** ADAM_STEP)
    v_hat = v / (1.0 - ADAM_B2 ** ADAM_STEP)
    return -ADAM_LR * (m_hat / (jnp.sqrt(v_hat) + ADAM_EPS) + ADAM_WD * w), m, v


def _adamw_halves(c_arr, w, mine, other, m, v, name):
    rows, n = mine.shape
    tr = 256 if rows % 256 == 0 else rows
    nb = rows // tr

    def body(c_ref, w_ref, a_ref, b_ref, m_ref, v_ref, g_ref, d_ref, mo_ref, vo_ref):
        g = jnp.where(pl.program_id(0) == c_ref[0], a_ref[...], b_ref[...])
        g_ref[...] = g
        d_ref[...], mo_ref[...], vo_ref[...] = _adamw_math(w_ref[...], g, m_ref[...], v_ref[...])

    full = pl.BlockSpec((None, tr, n), lambda h, i, c: (0, h * nb + i, 0))
    half = pl.BlockSpec((tr, n), lambda h, i, c: (i, 0))
    return pl.pallas_call(
        body, name=name,
        grid_spec=pltpu.PrefetchScalarGridSpec(num_scalar_prefetch=1, grid=(2, nb),
                                               in_specs=[full, half, half, full, full], out_specs=[full] * 4),
        out_shape=[jax.ShapeDtypeStruct((1, 2 * rows, n), F32)] * 4,
        compiler_params=_cp(("parallel", "parallel")),
    )(c_arr, w, mine, other, m, v)


def _adamw(w, g, m, v, name):
    rows, n = w.shape
    tr = 256 if rows % 256 == 0 else rows

    def body(w_ref, g_ref, m_ref, v_ref, d_ref, mo_ref, vo_ref):
        d_ref[...], mo_ref[...], vo_ref[...] = _adamw_math(w_ref[...], g_ref[...], m_ref[...], v_ref[...])

    spec = pl.BlockSpec((tr, n), lambda i: (i, 0))
    return pl.pallas_call(
        body, name=name, grid=(rows // tr,), in_specs=[spec] * 4, out_specs=[spec] * 3,
        out_shape=[jax.ShapeDtypeStruct((rows, n), F32)] * 3,
        compiler_params=_cp(("parallel",)),
    )(w, g, m, v)


def _rows8(a):
    a = a.reshape(-1, 128)
    return jnp.pad(a, ((0, (-a.shape[0]) % 8), (0, 0)))


def kernel(x, g_pre_mix, w_in, conv_w, conv_b, conv_ln_g, conv_ln_b, attn_norm_g, w_out, g_post_mix, g_pre_ffn, w_gate, w_up, w_down, g_post_ffn, loss_target, m_g_pre_mix, m_w_in, m_conv_w, m_conv_b, m_conv_ln_g, m_conv_ln_b, m_attn_norm_g, m_w_out, m_g_post_mix, m_g_pre_ffn, m_w_gate, m_w_up, m_w_down, m_g_post_ffn, v_g_pre_mix, v_w_in, v_conv_w, v_conv_b, v_conv_ln_g, v_conv_ln_b, v_attn_norm_g, v_w_out, v_g_post_mix, v_g_pre_ffn, v_w_gate, v_w_up, v_w_down, v_g_post_ffn):
    S = x.shape[1]
    tm_big = min(512, S)
    tm_ffn = min(256, S)
    tk_att = min(1024, S)
    t_att_fwd = min(1024, S)
    t_att_bwd = tk_att // KEY_BLOCKS
    chip = 2 * lax.axis_index("x") + lax.axis_index("y")
    core = lax.axis_index("c")
    x2 = x.reshape(S, D_MODEL)
    tgt = loss_target.reshape(S, D_MODEL)
    ag = attn_norm_g.reshape(1, CONV_CH)

    a_sh = w_in[0].astype(BF16)
    b_sh = jnp.stack([w_gate[0], w_up[0]]).astype(BF16)
    c_sh = jnp.concatenate([w_out[0], w_down[0]], axis=0).astype(BF16)
    cw_sh = jnp.pad(conv_w[0, :, 0, :], ((0, 1), (0, 0)))
    own = lambda full, shard: lax.dynamic_update_index_in_dim(full, shard, chip, 0)
    cols = lambda w4: jnp.concatenate([w4[j] for j in range(N_CHIPS)], axis=1)
    wa4, cw4 = _gather_weights([a_sh, cw_sh], [False, False])
    wa = own(wa4, a_sh)
    cwf = cols(own(cw4, cw_sh))

    a_bf, uc, qkv = _in_proj(x2, g_pre_mix, wa, tm_big)
    conv_out, yconv = _conv_fwd(uc, cwf, conv_b, conv_ln_g, conv_ln_b, tm_big)
    o, wb4, wc4 = _attn_fwd(qkv, t_att_fwd, tk_att, [b_sh, c_sh], [True, False])
    wb4, wc4 = own(wb4, b_sh), own(wc4, c_sh)
    wg, wu = cols(wb4[:, 0]), cols(wb4[:, 1])
    wo = wc4[:, :OUT_SH].reshape(D_MODEL, D_MODEL)
    wd = wc4[:, OUT_SH:].reshape(D_FF, D_MODEL)
    mixed, yv, h1, f_in = _out_proj(conv_out, o, ag, wo, x2, g_post_mix, g_pre_ffn, tm_big)
    df, dh2, dg4, loss_part, gt_bf, up_bf, act = _ffn_fwd(f_in, h1, tgt, wg, wu, wd, g_post_ffn, tm_ffn)

    dgt, dup, dh1, dy, dg3, dg2 = _ffn_bwd(gt_bf, up_bf, df, dh2, h1, yv, wg, wu, wd, g_pre_ffn, g_post_mix, tm_ffn)
    dco, do, dag = _out_bwd(dy, o, ag, wo, tm_big)
    ts = min(1024, S)
    gw_out = _matmul_tn(mixed, dy, D_MODEL, ts, "grad_w_out")
    gw_gate = _matmul_tn(f_in, dgt, D_MODEL, ts, "grad_w_gate")
    gw_up = _matmul_tn(f_in, dup, D_MODEL, ts, "grad_w_up")
    gw_down = _matmul_tn(act, df, D_FF // 2, ts, "grad_w_down")

    by_cols = lambda g: jnp.transpose(g.reshape(2, D_MODEL // 2, N_CHIPS, -1), (2, 0, 1, 3))
    by_rows = lambda g: g.reshape(N_CHIPS, 2, g.shape[0] // (2 * N_CHIPS), g.shape[1])
    c_arr = core.reshape(1).astype(jnp.int32)

    def chip_partials(views, nms):
        landed = _sibling_halves(views, "grad_sibling_halves_" + nms[0])
        return [_add_half(c_arr, g, l, "grad_half_" + nm) for g, l, nm in zip(views, landed, nms)]

    early = ["w_gate", "w_up", "w_out", "w_down"]
    parts = chip_partials([by_cols(gw_gate), by_cols(gw_up), by_rows(gw_out), by_rows(gw_down)], early)
    dq, dk, dv, *slots = _attn_bwd(qkv, do, t_att_bwd, tk_att, parts)
    duc, dcw, dcb, dlg, dlb = _conv_bwd(uc, yconv, dco, cwf, conv_ln_g, conv_ln_b, tm_big)
    grad_x, du, dg1 = _in_bwd(duc, dq, dk, dv, x2, dh1, g_pre_mix, wa, tm_big)
    gw_in = _matmul_tn(a_bf, du, D_MODEL, ts, "grad_w_in", column_block=IN_SH)
    slots += _chip_scatter(chip_partials([gw_in.reshape(N_CHIPS, 2, D_MODEL // 2, IN_SH)], ["w_in"]))
    names = early + ["w_in"]
    halves = [_sum_chips(s, "grad_sum_" + nm) for s, nm in zip(slots, names)]
    others = _share_halves(halves)
    mine = dict(zip(names, halves))
    other = dict(zip(names, others))

    small = [dg1, dcb, dlg, dlb, dag, dg2, dg3, dg4]
    packed = jnp.concatenate([_rows8(s) for s in small] + [_rows8(dcw), _rows8(loss_part)], axis=0)
    red = _allreduce_small(packed)
    sizes = [D_MODEL, CONV_CH, CONV_CH, CONV_CH, CONV_CH, D_MODEL, D_MODEL, D_MODEL]
    g_small = [red[8 * k:8 * k + n // 128].reshape(1, n) for k, n in enumerate(sizes)]
    cw_red = red[64:64 + 128].reshape(HALO, CONV_CH)
    g_cw = lax.dynamic_slice(cw_red, (0, chip * 128), (HALO, 128))
    loss = red[192, 0]

    big = []
    for w, m, v, nm in [(w_in, m_w_in, v_w_in, "w_in"), (w_out, m_w_out, v_w_out, "w_out"),
                        (w_gate, m_w_gate, v_w_gate, "w_gate"), (w_up, m_w_up, v_w_up, "w_up"),
                        (w_down, m_w_down, v_w_down, "w_down")]:
        big.append(_adamw_halves(c_arr, w, mine[nm], other[nm], m, v, "adamw_" + nm))
    sm_w = [g_pre_mix, conv_b, conv_ln_g, conv_ln_b, ag, g_post_mix, g_pre_ffn, g_post_ffn]
    sm_m = [m_g_pre_mix, m_conv_b, m_conv_ln_g, m_conv_ln_b, m_attn_norm_g, m_g_post_mix, m_g_pre_ffn, m_g_post_ffn]
    sm_v = [v_g_pre_mix, v_conv_b, v_conv_ln_g, v_conv_ln_b, v_attn_norm_g, v_g_post_mix, v_g_pre_ffn, v_g_post_ffn]
    pad_cw = lambda a: jnp.pad(a[0, :, 0, :], ((0, 1), (0, 0)))

    def pack(vecs, cw):
        return jnp.concatenate([_rows8(a) for a in vecs] + [cw], axis=0)

    sd, smn, svn = _adamw(pack(sm_w, pad_cw(conv_w)), pack(g_small, g_cw), pack(sm_m, pad_cw(m_conv_w)),
                          pack(sm_v, pad_cw(v_conv_w)), "adamw_small")

    def unpack(p):
        vecs = [p[8 * k:8 * k + n // 128].reshape(1, n) for k, n in enumerate(sizes)]
        return vecs, p[64:64 + CONV_WIDTH].reshape(1, CONV_WIDTH, 1, 128)

    def ordered(vecs, cw, w_in_, w_out_, w_gate_, w_up_, w_down_):
        g1_, cb_, lg_, lb_, ag_, g2_, g3_, g4_ = vecs
        return [g1_, w_in_, cw, cb_, lg_, lb_, ag_.reshape(1, 8, HEAD_DIM), w_out_, g2_, g3_,
                w_gate_, w_up_, w_down_, g4_]

    grads = ordered(g_small, g_cw[:CONV_WIDTH].reshape(1, CONV_WIDTH, 1, 128), *[b[0] for b in big])
    outs = []
    for idx, p in enumerate((sd, smn, svn)):
        vecs, cw = unpack(p)
        outs += ordered(vecs, cw, *[b[idx + 1] for b in big])
    return (loss, grad_x, *grads, *outs)
```

```python
import functools

import jax
import jax.numpy as jnp
from jax import lax
from jax.experimental import pallas as pl
from jax.experimental.pallas import tpu as pltpu

F32 = jnp.float32
BF16 = jnp.bfloat16
MESH = pl.DeviceIdType.MESH

D_MODEL = 1024
CONV_CH = 512
CONV_WIDTH = 31
HEAD_DIM = 64
PAIR = 2 * HEAD_DIM
N_PAIRS = 4
D_FF = 2816
N_CHIPS = 4
IN_SH = 2560 // N_CHIPS
OUT_SH = D_MODEL // N_CHIPS
EPS = 1e-6
HALO = 32

ADAM_LR = 0.001
ADAM_B1 = 0.9
ADAM_B2 = 0.999
ADAM_EPS = 1e-08
ADAM_WD = 0.01
ADAM_STEP = 10

VMEM_LIMIT = 56 * 2 ** 20
VMEM_LIMIT_ATTN_BWD = 62 * 2 ** 20


def _cp(sem=None, vmem=VMEM_LIMIT):
    return pltpu.CompilerParams(dimension_semantics=sem, vmem_limit_bytes=vmem)


def _hbm():
    return pl.BlockSpec(memory_space=pltpu.HBM)


def _const_spec(shape):
    nd = len(shape)
    return pl.BlockSpec(shape, lambda *_: (0,) * nd, pipeline_mode=pl.Buffered(1))


def _dot(a, b):
    return jnp.dot(a, b, preferred_element_type=F32)


def _dot_nt(a, b):
    return lax.dot_general(a, b, (((1,), (1,)), ((), ())), preferred_element_type=F32)


def _dot_tn(a, b):
    return lax.dot_general(a, b, (((0,), (0,)), ((), ())), preferred_element_type=F32)


def _split3(x):
    b0 = x.astype(BF16)
    r1 = x - b0.astype(F32)
    b1 = r1.astype(BF16)
    b2 = (r1 - b1.astype(F32)).astype(BF16)
    return b0, b1, b2


def _sigmoid(x):
    return 1.0 / (1.0 + jnp.exp(-x))


def _head_mean(x, seg):
    b0, b1, b2 = _split3(x)
    return (_dot(b0, seg) + _dot(b1, seg) + _dot(b2, seg)) * (1.0 / HEAD_DIM)


def _seg_matrix(n):
    r = lax.broadcasted_iota(jnp.int32, (n, n), 0) // HEAD_DIM
    c = lax.broadcasted_iota(jnp.int32, (n, n), 1) // HEAD_DIM
    return (r == c).astype(BF16)


def _rms(x):
    return lax.rsqrt(jnp.mean(x * x, axis=-1, keepdims=True) + EPS)


def _rms_bwd(dy, n, r, g):
    dn = dy * g
    dx = r * (dn - n * jnp.mean(dn * n, axis=-1, keepdims=True))
    return dx, dy * n


class _GatherPlan:
    def __init__(self, srcs, outs, lead, ssem, rsem):
        self.srcs, self.outs, self.lead, self.ssem, self.rsem = srcs, outs, lead, ssem, rsem
        x, y, self.c = lax.axis_index("x"), lax.axis_index("y"), lax.axis_index("c")
        self.me = 2 * x + y
        self.sibling = (x, y, 1 - self.c)
        self.chips = [(1 - x, y), (x, 1 - y), (1 - x, 1 - y)]

    def _half(self, ref, i, h):
        if self.lead[i]:
            return ref.at[h]
        rows = ref.shape[0] // 2
        return ref.at[pl.ds(h * rows, rows)]

    def _ici(self, i, k, origin):
        return pltpu.make_async_remote_copy(
            src_ref=self._half(self.srcs[i], i, self.c), dst_ref=self._half(self.outs[i].at[origin], i, self.c),
            send_sem=self.ssem.at[6 * i + k], recv_sem=self.rsem.at[6 * i + k],
            device_id=(self.chips[k][0], self.chips[k][1], self.c), device_id_type=MESH)

    def _d2d(self, i, k, h):
        origin = 2 * self.chips[k][0] + self.chips[k][1]
        piece = self._half(self.outs[i].at[origin], i, h)
        return pltpu.make_async_remote_copy(
            src_ref=piece, dst_ref=piece, send_sem=self.ssem.at[6 * i + 3 + k],
            recv_sem=self.rsem.at[6 * i + 3 + k], device_id=self.sibling, device_id_type=MESH)

    def _each(self):
        return [(i, k) for i in range(len(self.srcs)) for k in range(3)]

    def start(self):
        for i, k in self._each():
            self._ici(i, k, self.me).start()

    def forward(self):
        for i, k in self._each():
            self._ici(i, k, 2 * self.chips[k][0] + self.chips[k][1]).wait_recv()
            self._d2d(i, k, self.c).start()

    def finish(self):
        for i, k in self._each():
            self._d2d(i, k, 1 - self.c).wait_recv()
        for i, k in self._each():
            self._ici(i, k, self.me).wait_send()
            self._d2d(i, k, self.c).wait_send()


def _gather_shapes(shards):
    return [jax.ShapeDtypeStruct((N_CHIPS,) + s.shape, s.dtype) for s in shards]


def _gather_weights(shards, lead):
    n = len(shards)

    def body(*refs):
        plan = _GatherPlan(refs[:n], refs[n:2 * n], lead, refs[2 * n], refs[2 * n + 1])
        plan.start()
        plan.forward()
        plan.finish()

    return pl.pallas_call(
        body, name="gather_weights", out_shape=_gather_shapes(shards),
        in_specs=[_hbm()] * n, out_specs=[_hbm()] * n,
        scratch_shapes=[pltpu.SemaphoreType.DMA((6 * n,)), pltpu.SemaphoreType.DMA((6 * n,))],
    )(*shards)


def _in_proj(x2, g1, wa, tm):
    S = x2.shape[0]

    def body(x_ref, g_ref, w_ref, a_ref, uc_ref, qkv_ref):
        x = x_ref[...]
        a = (x * _rms(x) * g_ref[...]).astype(BF16)
        a_ref[...] = a
        u = [_dot(a, w_ref[j]) for j in range(N_CHIPS)]
        uc_ref[:, 0:640] = u[0]
        uc_ref[:, 640:1024] = u[1][:, 0:384]
        qkv_ref[:, 0:256] = u[1][:, 384:640].astype(BF16)
        qkv_ref[:, 256:896] = u[2].astype(BF16)
        qkv_ref[:, 896:1536] = u[3].astype(BF16)

    return pl.pallas_call(
        body, name="in_proj", grid=(S // tm,),
        in_specs=[pl.BlockSpec((tm, D_MODEL), lambda i: (i, 0)), _const_spec((1, D_MODEL)),
                  _const_spec(wa.shape)],
        out_specs=[pl.BlockSpec((tm, D_MODEL), lambda i: (i, 0)),
                   pl.BlockSpec((tm, 2 * CONV_CH), lambda i: (i, 0)),
                   pl.BlockSpec((tm, 1536), lambda i: (i, 0))],
        out_shape=[jax.ShapeDtypeStruct((S, D_MODEL), BF16), jax.ShapeDtypeStruct((S, 2 * CONV_CH), F32),
                   jax.ShapeDtypeStruct((S, 1536), BF16)],
        compiler_params=_cp(("parallel",)),
    )(x2, g1, wa)


SUBLANES = 8


def _shift_copies(src_ref, sh_ref):
    rows = sh_ref.shape[1]
    for b in range(1, SUBLANES):
        sh_ref[b - 1] = src_ref[pl.ds(b, rows), :]


def _rows_at(src_ref, sh_ref, off, rows):
    a, b = divmod(off, SUBLANES)
    if b == 0:
        return src_ref[pl.ds(SUBLANES * a, rows), :]
    return sh_ref[b - 1, pl.ds(SUBLANES * a, rows), :]


def _conv_taps(cw_ref, src_ref, sh_ref, offs, rows):
    acc = None
    for w, off in enumerate(offs):
        term = cw_ref[w:w + 1, :] * _rows_at(src_ref, sh_ref, off, rows)
        acc = term if acc is None else acc + term
    return acc


def _glu(uc):
    return uc[:, :CONV_CH] * _sigmoid(uc[:, CONV_CH:])


def _conv_fwd(uc, cwf, cb, lg, lb, tm):
    S = uc.shape[0]
    hb = tm // HALO

    def body(uc_ref, prev_ref, cw_ref, cb_ref, lg_ref, lb_ref, out_ref, y_ref, glu_ref, sh_ref):
        i = pl.program_id(0)
        glu_ref[0:HALO, :] = jnp.where(i == 0, 0.0, _glu(prev_ref[...]))
        glu_ref[HALO:HALO + tm, :] = _glu(uc_ref[...])
        glu_ref[HALO + tm:HALO + tm + SUBLANES, :] = jnp.zeros((SUBLANES, CONV_CH), F32)
        _shift_copies(glu_ref, sh_ref)
        offs = [HALO - (CONV_WIDTH - 1) + w for w in range(CONV_WIDTH)]
        y = _conv_taps(cw_ref, glu_ref, sh_ref, offs, tm) + cb_ref[...]
        y_ref[...] = y
        mu = jnp.mean(y, axis=-1, keepdims=True)
        yc = y - mu
        rstd = lax.rsqrt(jnp.mean(yc * yc, axis=-1, keepdims=True) + EPS)
        ln = yc * rstd * lg_ref[...] + lb_ref[...]
        out_ref[...] = (ln * _sigmoid(ln)).astype(BF16)

    return pl.pallas_call(
        body, name="conv_fwd", grid=(S // tm,),
        in_specs=[pl.BlockSpec((tm, 2 * CONV_CH), lambda i: (i, 0)),
                  pl.BlockSpec((HALO, 2 * CONV_CH), lambda i: (jnp.maximum(i * hb - 1, 0), 0)),
                  _const_spec(cwf.shape), _const_spec((1, CONV_CH)), _const_spec((1, CONV_CH)),
                  _const_spec((1, CONV_CH))],
        out_specs=[pl.BlockSpec((tm, CONV_CH), lambda i: (i, 0))] * 2,
        out_shape=[jax.ShapeDtypeStruct((S, CONV_CH), BF16), jax.ShapeDtypeStruct((S, CONV_CH), F32)],
        scratch_shapes=[pltpu.VMEM((HALO + tm + SUBLANES, CONV_CH), F32),
                        pltpu.VMEM((SUBLANES - 1, HALO + tm, CONV_CH), F32)],
        compiler_params=_cp(("parallel",)),
    )(uc, uc, cwf, cb, lg, lb)


def _lane_mask(h):
    lane = lax.broadcasted_iota(jnp.int32, (1, PAIR), 1)
    return (lane >= HEAD_DIM * h) & (lane < HEAD_DIM * (h + 1))


def _neg_abs(x):
    bits = lax.bitcast_convert_type(x, jnp.uint32) | jnp.uint32(0x80000000)
    return lax.bitcast_convert_type(bits, F32)


def _tri_dot(x, m):
    return _dot(x.astype(BF16), m)


MASKED = -1e30
KEY_BLOCKS = 4


def _running_sums(x, m, reverse, start=None):
    t = m.shape[0]
    blocks = x.shape[1] // t
    order = range(blocks - 1, -1, -1) if reverse else range(blocks)
    out = [None] * blocks
    carry = start
    for b in order:
        xb = x[:, b * t:(b + 1) * t]
        cb = _tri_dot(xb, m)
        out[b] = cb if carry is None else cb + carry
        rs = jnp.sum(xb, axis=1, keepdims=True)
        carry = rs if carry is None else carry + rs
    return jnp.concatenate(out, axis=1), carry


def _sb_tile(z, r, m_suf):
    sp = jnp.maximum(z, 0.0) + jnp.log(1.0 + jnp.exp(_neg_abs(z)))
    c, rs = _running_sums(sp, m_suf, reverse=True, start=r)
    return jnp.exp(z - c), sp, rs


def _scores(qm, kt, mask):
    z = _dot_nt(qm, kt)
    return z if mask is None else jnp.where(mask, z, MASKED)


def _causal_mask(i, sb, t, tk, w=None):
    w = tk if w is None else w
    row = lax.broadcasted_iota(jnp.int32, (t, w), 0) + i * t
    col = lax.broadcasted_iota(jnp.int32, (t, w), 1) + sb * tk
    return col < row


def _sweep_pairs(count, tile):
    def step(n, carry):
        tile(2 * n)
        tile(2 * n + 1)
        return carry

    lax.fori_loop(0, count // 2, step, 0)
    pl.when(lax.rem(count, 2) == 1)(lambda: tile(count - 1))


def _sweep(first, count, down, fetch, load, work, fetched=False):
    lo, hi = (first - count, first) if down else (first, first + count)
    tile = lambda j: jnp.clip(first - j if down else first + j, lo, hi)
    if not fetched:
        fetch(first, 0, True)

    def step(n, carry):
        j = 2 * n
        vals = load(0)
        fetch(tile(j + 1), 1, False)
        work(tile(j), vals)
        vals = load(1)
        fetch(tile(j + 2), 0, False)
        work(tile(j + 1), vals)
        return carry

    lax.fori_loop(0, (count + 1) // 2, step, 0)

    @pl.when(lax.rem(count, 2) == 0)
    def _():
        work(tile(count), load(0))


def _suffix_matrix(t, prefix=False):
    row = lax.broadcasted_iota(jnp.int32, (t, t), 0)
    col = lax.broadcasted_iota(jnp.int32, (t, t), 1)
    return ((row <= col) if prefix else (row >= col)).astype(BF16)


def _attn_fwd(qkv, t, tk, shards, lead):
    S = qkv.shape[0]
    assert t == tk, "the forward cuts its diagonal tile by blocks: query block and key tile must match"

    ng = len(shards)
    nq = S // t

    def body(*refs):
        q_ref, k_ref, v_ref = refs[:3]
        o_ref = refs[3 + ng]
        acc_ref, r_ref, z_buf, ssem, rsem = refs[4 + 2 * ng:]
        p = pl.program_id(0)
        i = pl.program_id(1)
        plan = _GatherPlan(refs[3:3 + ng], refs[4 + ng:4 + 2 * ng], lead, ssem, rsem)
        pl.when((p == 0) & (i == 0))(plan.start)
        pl.when((p == 1) & (i == 0))(plan.forward)
        blk = tk // KEY_BLOCKS
        m_suf = _suffix_matrix(blk)
        q = q_ref[...]
        hms = [_lane_mask(h) for h in range(2)]
        qms = [jnp.where(hm, q, 0) * 0.125 for hm in hms]

        def rows(sb, w=tk):
            return pl.ds(pl.multiple_of(sb * tk, tk), w)

        def fetch(sb, slot, first):
            kt = k_ref[rows(sb), :]
            for h in range(2):
                z_buf[slot, h] = _scores(qms[h], kt, None)

        fetch(jnp.maximum(i - 1, 0), 0, True)
        for rb in range(KEY_BLOCKS):
            w = (rb + 1) * blk
            part = slice(rb * blk, w)
            kt = k_ref[rows(i, w), :]
            vt = v_ref[rows(i, w), :]
            row = lax.broadcasted_iota(jnp.int32, (blk, w), 0) + rb * blk
            mask = lax.broadcasted_iota(jnp.int32, (blk, w), 1) < row
            out = jnp.zeros((blk, PAIR), F32)
            for h in range(2):
                a_loc, _, rs = _sb_tile(_scores(qms[h][part], kt, mask), None, m_suf)
                out = out + _dot(a_loc.astype(BF16), jnp.where(hms[h], vt, 0))
                r_ref[h, part] = rs
            acc_ref[part, :] = out

        def load(slot):
            return slot

        def work(sb, slot):
            vt = v_ref[rows(sb), :]
            for h in range(2):
                vm = jnp.where(hms[h], vt, 0)
                for c in range(t // blk):
                    part = slice(c * blk, (c + 1) * blk)
                    a_loc, _, rs = _sb_tile(z_buf[slot, h, part, :], None, m_suf)
                    r = r_ref[h, part]
                    acc_ref[part, :] += _dot(a_loc.astype(BF16), vm) * jnp.exp(-r)
                    r_ref[h, part] = r + rs

        pl.when(i >= 1)(lambda: _sweep(i - 1, i - 1, True, fetch, load, work, fetched=True))
        o_ref[...] = acc_ref[...]
        pl.when((p == N_PAIRS - 1) & (i == nq - 1))(plan.finish)

    return pl.pallas_call(
        body, name="attn_fwd", grid=(N_PAIRS, nq),
        in_specs=[pl.BlockSpec((t, PAIR), lambda p, i: (i, p)),
                  pl.BlockSpec((S, PAIR), lambda p, i: (0, N_PAIRS + p)),
                  pl.BlockSpec((S, PAIR), lambda p, i: (0, 2 * N_PAIRS + p))] + [_hbm()] * ng,
        out_specs=[pl.BlockSpec((t, PAIR), lambda p, i: (i, p))] + [_hbm()] * ng,
        out_shape=[jax.ShapeDtypeStruct((S, N_PAIRS * PAIR), F32)] + _gather_shapes(shards),
        scratch_shapes=[pltpu.VMEM((t, PAIR), F32), pltpu.VMEM((2, t, 1), F32),
                        pltpu.VMEM((2, 2, t, tk), F32),
                        pltpu.SemaphoreType.DMA((6 * ng,)), pltpu.SemaphoreType.DMA((6 * ng,))],
        compiler_params=_cp(("arbitrary", "arbitrary")),
    )(qkv, qkv, qkv, *shards)


def _out_proj(conv_out, o, ag, wc, x2, g2, g3, tm):
    S = o.shape[0]

    def body(co_ref, o_ref, ag_ref, w_ref, x_ref, g2_ref, g3_ref, mix_ref, y_ref, h1_ref, fin_ref):
        seg = _seg_matrix(CONV_CH)
        o = o_ref[...]
        att = (o * lax.rsqrt(_head_mean(o * o, seg) + EPS) * ag_ref[...]).astype(BF16)
        co = co_ref[...]
        mix_ref[:, :CONV_CH] = co
        mix_ref[:, CONV_CH:] = att
        y = _dot(co, w_ref[0:CONV_CH, :]) + _dot(att, w_ref[CONV_CH:, :])
        y_ref[...] = y
        h1 = x_ref[...] + y * _rms(y) * g2_ref[...]
        h1_ref[...] = h1
        fin_ref[...] = (h1 * _rms(h1) * g3_ref[...]).astype(BF16)

    row = lambda w: pl.BlockSpec((tm, w), lambda i: (i, 0))
    return pl.pallas_call(
        body, name="out_proj", grid=(S // tm,),
        in_specs=[row(CONV_CH), row(CONV_CH), _const_spec((1, CONV_CH)), _const_spec(wc.shape),
                  row(D_MODEL), _const_spec((1, D_MODEL)), _const_spec((1, D_MODEL))],
        out_specs=[row(D_MODEL)] * 4,
        out_shape=[jax.ShapeDtypeStruct((S, D_MODEL), BF16), jax.ShapeDtypeStruct((S, D_MODEL), F32),
                   jax.ShapeDtypeStruct((S, D_MODEL), F32), jax.ShapeDtypeStruct((S, D_MODEL), BF16)],
        compiler_params=_cp(("parallel",)),
    )(conv_out, o, ag, wc, x2, g2, g3)


def _ffn_fwd(f_in, h1, tgt, wg, wu, wd, g4, tm):
    S = f_in.shape[0]

    def body(fin_ref, h1_ref, tgt_ref, wg_ref, wu_ref, wd_ref, g4_ref, df_ref, dh2_ref, dg4_ref, loss_ref,
             gt_ref, up_ref, act_ref):
        i = pl.program_id(0)
        fin = fin_ref[...]
        gt = _dot(fin, wg_ref[...])
        up = _dot(fin, wu_ref[...])
        act = (gt * _sigmoid(gt) * up).astype(BF16)
        gt_ref[...] = gt.astype(BF16)
        up_ref[...] = up.astype(BF16)
        act_ref[...] = act
        f = _dot(act, wd_ref[...])
        r = _rms(f)
        n = f * r
        g4 = g4_ref[...]
        err = h1_ref[...] + n * g4 - tgt_ref[...]
        dh2 = err * (1.0 / D_MODEL)
        dh2_ref[...] = dh2
        df, dg = _rms_bwd(dh2, n, r, g4)
        df_ref[...] = df.astype(BF16)

        @pl.when(i == 0)
        def _():
            dg4_ref[...] = jnp.zeros_like(dg4_ref)
            loss_ref[...] = jnp.zeros_like(loss_ref)

        dg4_ref[...] += jnp.sum(dg, axis=0, keepdims=True)
        part = jnp.sum(jnp.sum(err * err, axis=1, keepdims=True), axis=0, keepdims=True)
        loss_ref[...] += part * (0.5 / D_MODEL)

    row = lambda w: pl.BlockSpec((tm, w), lambda i: (i, 0))
    return pl.pallas_call(
        body, name="ffn_fwd", grid=(S // tm,),
        in_specs=[row(D_MODEL), row(D_MODEL), row(D_MODEL), _const_spec(wg.shape), _const_spec(wu.shape),
                  _const_spec(wd.shape), _const_spec((1, D_MODEL))],
        out_specs=[row(D_MODEL), row(D_MODEL), pl.BlockSpec((1, D_MODEL), lambda i: (0, 0)),
                   pl.BlockSpec((1, 128), lambda i: (0, 0)), row(D_FF), row(D_FF), row(D_FF)],
        out_shape=[jax.ShapeDtypeStruct((S, D_MODEL), BF16), jax.ShapeDtypeStruct((S, D_MODEL), F32),
                   jax.ShapeDtypeStruct((1, D_MODEL), F32), jax.ShapeDtypeStruct((1, 128), F32)]
        + [jax.ShapeDtypeStruct((S, D_FF), BF16)] * 3,
        compiler_params=_cp(("arbitrary",)),
    )(f_in, h1, tgt, wg, wu, wd, g4)


def _ffn_bwd(gt_bf, up_bf, df, dh2, h1, yv, wg, wu, wd, g3, g2, tm):
    S = df.shape[0]

    def body(gt_ref, up_ref, df_ref, dh2_ref, h1_ref, y_ref, wg_ref, wu_ref, wd_ref, g3_ref, g2_ref,
             dgt_ref, dup_ref, dh1_ref, dy_ref, dg3_ref, dg2_ref):
        i = pl.program_id(0)
        df = df_ref[...]
        gt = gt_ref[...].astype(F32)
        up = up_ref[...].astype(F32)
        sg = _sigmoid(gt)
        silu = gt * sg
        dact = _dot_nt(df, wd_ref[...])
        dgt = (dact * up * (sg * (1.0 + gt * (1.0 - sg)))).astype(BF16)
        dup = (dact * silu).astype(BF16)
        dgt_ref[...] = dgt
        dup_ref[...] = dup
        dfin = _dot_nt(dgt, wg_ref[...]) + _dot_nt(dup, wu_ref[...])
        h1 = h1_ref[...]
        r3 = _rms(h1)
        dh1_n, dg3 = _rms_bwd(dfin, h1 * r3, r3, g3_ref[...])
        dh1 = dh2_ref[...] + dh1_n
        dh1_ref[...] = dh1
        y = y_ref[...]
        r2 = _rms(y)
        dy, dg2 = _rms_bwd(dh1, y * r2, r2, g2_ref[...])
        dy_ref[...] = dy.astype(BF16)

        @pl.when(i == 0)
        def _():
            dg3_ref[...] = jnp.zeros_like(dg3_ref)
            dg2_ref[...] = jnp.zeros_like(dg2_ref)

        dg3_ref[...] += jnp.sum(dg3, axis=0, keepdims=True)
        dg2_ref[...] += jnp.sum(dg2, axis=0, keepdims=True)

    row = lambda w: pl.BlockSpec((tm, w), lambda i: (i, 0))
    vec = pl.BlockSpec((1, D_MODEL), lambda i: (0, 0))
    return pl.pallas_call(
        body, name="ffn_bwd", grid=(S // tm,),
        in_specs=[row(D_FF), row(D_FF)] + [row(D_MODEL)] * 4
        + [_const_spec(wg.shape), _const_spec(wu.shape), _const_spec(wd.shape),
           _const_spec((1, D_MODEL)), _const_spec((1, D_MODEL))],
        out_specs=[row(D_FF), row(D_FF), row(D_MODEL), row(D_MODEL), vec, vec],
        out_shape=[jax.ShapeDtypeStruct((S, D_FF), BF16)] * 2
        + [jax.ShapeDtypeStruct((S, D_MODEL), F32), jax.ShapeDtypeStruct((S, D_MODEL), BF16),
           jax.ShapeDtypeStruct((1, D_MODEL), F32), jax.ShapeDtypeStruct((1, D_MODEL), F32)],
        compiler_params=_cp(("arbitrary",)),
    )(gt_bf, up_bf, df, dh2, h1, yv, wg, wu, wd, g3, g2)


def _out_bwd(dy, o, ag, wc, tm):
    S = o.shape[0]

    def body(dy_ref, o_ref, ag_ref, w_ref, dco_ref, do_ref, dag_ref):
        i = pl.program_id(0)
        seg = _seg_matrix(CONV_CH)
        dy = dy_ref[...]
        dco_ref[...] = _dot_nt(dy, w_ref[0:CONV_CH, :])
        datt = _dot_nt(dy, w_ref[CONV_CH:, :])
        o = o_ref[...]
        r = lax.rsqrt(_head_mean(o * o, seg) + EPS)
        n = o * r
        dn = datt * ag_ref[...]
        do_ref[...] = (r * (dn - n * _head_mean(dn * n, seg))).astype(BF16)

        @pl.when(i == 0)
        def _():
            dag_ref[...] = jnp.zeros_like(dag_ref)

        dag_ref[...] += jnp.sum(datt * n, axis=0, keepdims=True)

    row = lambda w: pl.BlockSpec((tm, w), lambda i: (i, 0))
    return pl.pallas_call(
        body, name="out_bwd", grid=(S // tm,),
        in_specs=[row(D_MODEL), row(CONV_CH), _const_spec((1, CONV_CH)), _const_spec(wc.shape)],
        out_specs=[row(CONV_CH), row(CONV_CH), pl.BlockSpec((1, CONV_CH), lambda i: (0, 0))],
        out_shape=[jax.ShapeDtypeStruct((S, CONV_CH), F32), jax.ShapeDtypeStruct((S, CONV_CH), BF16),
                   jax.ShapeDtypeStruct((1, CONV_CH), F32)],
        compiler_params=_cp(("arbitrary",)),
    )(dy, o, ag, wc)


def _attn_bwd(qkv, do, t, tk, parts):
    S = qkv.shape[0]
    nk = S // tk
    ns = len(parts)

    def body(*refs):
        q_ref, k_ref, v_ref, do_ref = refs[:4]
        dq_ref, dk_hbm, dv_hbm = refs[4 + ns:7 + ns]
        g_buf, s_buf, r_ref, dq_acc, dk_ref, dv_ref, z_buf = refs[7 + 2 * ns:14 + 2 * ns]
        p = pl.program_id(0)
        i = pl.program_id(1)
        plan = _ScatterPlan(refs[4:4 + ns], refs[7 + ns:7 + 2 * ns], *refs[14 + 2 * ns:])
        pl.when((p == 0) & (i == 0))(plan.start)
        last = (i * t + t - 1) // tk

        @pl.when(i == 0)
        def _():
            dk_ref[...] = jnp.zeros_like(dk_ref)
            dv_ref[...] = jnp.zeros_like(dv_ref)

        m_suf = _suffix_matrix(tk // KEY_BLOCKS)
        m_pre = _suffix_matrix(tk // KEY_BLOCKS, prefix=True)
        q = q_ref[...]
        do = do_ref[...]
        hms = [_lane_mask(h) for h in range(2)]
        qms = [jnp.where(hm, q, 0) * 0.125 for hm in hms]
        doms = [jnp.where(hm, do, 0) for hm in hms]
        dq_acc[...] = jnp.zeros_like(dq_acc)
        r_ref[...] = jnp.zeros_like(r_ref)

        def keys(sb, w=tk):
            return pl.ds(pl.multiple_of(sb * tk, tk), w)

        def matmuls1(sb, w, diagonal):
            kt = k_ref[keys(sb, w), :]
            vt = v_ref[keys(sb, w), :]
            mask = _causal_mask(i, sb, t, tk, w) if diagonal else None
            return [(_scores(qms[h], kt, mask), _dot_nt(doms[h], vt)) for h in range(2)]

        def sweep1(sb, w, vals):
            dv = jnp.zeros((w, PAIR), F32)
            for h in range(2):
                z, da = vals[h]
                A, sp, r_ref[h] = _sb_tile(z, r_ref[h], m_suf)
                g_buf[h, sb, :, 0:w] = A * da
                s_buf[h, sb, :, 0:w] = 1.0 - jnp.exp(-sp)
                dv = dv + _dot_tn(A.astype(BF16), doms[h])
            dv_ref[keys(sb, w), :] += dv

        def sweep2(sb, w):
            kt = k_ref[keys(sb, w), :]
            dk = jnp.zeros((w, PAIR), F32)
            for h in range(2):
                g = g_buf[h, sb, :, 0:w]
                pre, r_ref[h] = _running_sums(g, m_pre, reverse=False, start=r_ref[h])
                dzb = (g - s_buf[h, sb, :, 0:w] * pre).astype(BF16)
                dq_acc[...] += _dot(dzb, jnp.where(hms[h], kt, 0))
                dk = dk + _dot_tn(dzb, qms[h])
            dk_ref[keys(sb, w), :] += dk

        def diagonal_tile(tile):
            for nb in range(1, KEY_BLOCKS + 1):
                pl.when(lax.rem(i, KEY_BLOCKS) == nb - 1)(functools.partial(tile, nb * t))

        def fetch1(sb, slot, first):
            kt = k_ref[keys(sb), :]
            for h in range(2):
                z_buf[slot, h] = _scores(qms[h], kt, None)

        def load1(slot):
            return [z_buf[slot, h] for h in range(2)]

        def work1(sb, zs):
            vt = v_ref[keys(sb), :]
            sweep1(sb, tk, [(zs[h], _dot_nt(doms[h], vt)) for h in range(2)])

        def diagonal1(w):
            fetch1(jnp.maximum(last - 1, 0), 0, True)
            sweep1(last, w, matmuls1(last, w, True))

        diagonal_tile(diagonal1)
        pl.when(last >= 1)(lambda: _sweep(last - 1, last - 1, True, fetch1, load1, work1, fetched=True))
        r_ref[...] = jnp.zeros_like(r_ref)
        _sweep_pairs(last, lambda sb: sweep2(sb, tk))
        diagonal_tile(lambda w: sweep2(last, w))
        dq_ref[...] = dq_acc[...] * 0.125

        @pl.when(i == S // t - 1)
        def _():
            cols = pl.ds(pl.multiple_of(p * PAIR, PAIR), PAIR)
            pltpu.sync_copy(dk_ref, dk_hbm.at[:, cols])
            pltpu.sync_copy(dv_ref, dv_hbm.at[:, cols])

        pl.when((p == N_PAIRS - 1) & (i == S // t - 1))(plan.finish)

    once = lambda cb: pl.BlockSpec((S, PAIR), cb, pipeline_mode=pl.Buffered(1))
    return pl.pallas_call(
        body, name="attn_bwd", grid=(N_PAIRS, S // t),
        in_specs=[pl.BlockSpec((t, PAIR), lambda p, i: (i, p)),
                  once(lambda p, i: (0, N_PAIRS + p)), once(lambda p, i: (0, 2 * N_PAIRS + p)),
                  pl.BlockSpec((t, PAIR), lambda p, i: (i, p))] + [_hbm()] * ns,
        out_specs=[pl.BlockSpec((t, PAIR), lambda p, i: (i, p)), _hbm(), _hbm()] + [_hbm()] * ns,
        out_shape=[jax.ShapeDtypeStruct((S, N_PAIRS * PAIR), F32)] * 3
        + [jax.ShapeDtypeStruct(pt.shape, pt.dtype) for pt in parts],
        scratch_shapes=[pltpu.VMEM((2, nk, t, tk), F32), pltpu.VMEM((2, nk, t, tk), F32),
                        pltpu.VMEM((2, t, 1), F32), pltpu.VMEM((t, PAIR), F32),
                        pltpu.VMEM((S, PAIR), F32), pltpu.VMEM((S, PAIR), F32),
                        pltpu.VMEM((2, 2, t, tk), F32)] + _scatter_sems(ns),
        compiler_params=_cp(("arbitrary", "arbitrary"), vmem=VMEM_LIMIT_ATTN_BWD),
    )(qkv, qkv, qkv, do, *parts)


def _conv_bwd(uc, yconv, dco, cwf, lg, lb, tm):
    S = uc.shape[0]
    hb = tm // HALO
    nb = S // tm
    ext = tm + HALO

    def body(uc_ref, prev_ref, y_ref, ynext_ref, dco_ref, dnext_ref, cw_ref, lg_ref, lb_ref,
             duc_ref, dcw_ref, dcb_ref, dlg_ref, dlb_ref, glu_ref, dyc_ref, shg_ref, shd_ref):
        i = pl.program_id(0)
        last = i == nb - 1

        @pl.when(i == 0)
        def _():
            for ref in (dcw_ref, dcb_ref, dlg_ref, dlb_ref):
                ref[...] = jnp.zeros_like(ref)

        uc = uc_ref[...]
        glu_ref[0:HALO, :] = jnp.where(i == 0, 0.0, _glu(prev_ref[...]))
        glu_ref[HALO:ext, :] = _glu(uc)
        glu_ref[ext:ext + SUBLANES, :] = jnp.zeros((SUBLANES, CONV_CH), F32)
        _shift_copies(glu_ref, shg_ref)
        fwd_offs = [HALO - (CONV_WIDTH - 1) + w for w in range(CONV_WIDTH)]
        y = jnp.concatenate([y_ref[...], ynext_ref[...]], axis=0)
        mu = jnp.mean(y, axis=-1, keepdims=True)
        yc = y - mu
        rstd = lax.rsqrt(jnp.mean(yc * yc, axis=-1, keepdims=True) + EPS)
        yhat = yc * rstd
        lg = lg_ref[...]
        ln = yhat * lg + lb_ref[...]
        sg = _sigmoid(ln)
        dout = jnp.concatenate([dco_ref[...], jnp.where(last, 0.0, dnext_ref[...])], axis=0)
        dln = dout * (sg * (1.0 + ln * (1.0 - sg)))
        dyh = dln * lg
        dyc = rstd * (dyh - jnp.mean(dyh, axis=-1, keepdims=True)
                      - yhat * jnp.mean(dyh * yhat, axis=-1, keepdims=True))
        dyc_ref[0:ext, :] = dyc
        dyc_ref[ext:ext + SUBLANES, :] = jnp.zeros((SUBLANES, CONV_CH), F32)
        _shift_copies(dyc_ref, shd_ref)
        dlg_ref[...] += jnp.sum((dln * yhat)[0:tm], axis=0, keepdims=True)
        dlb_ref[...] += jnp.sum(dln[0:tm], axis=0, keepdims=True)
        dcb_ref[...] += jnp.sum(dyc[0:tm], axis=0, keepdims=True)
        dglu = _conv_taps(cw_ref, dyc_ref, shd_ref, [CONV_WIDTH - 1 - w for w in range(CONV_WIDTH)], tm)
        d0 = dyc[0:tm]
        for w, off in enumerate(fwd_offs):
            dcw_ref[w:w + 1, :] += jnp.sum(d0 * _rows_at(glu_ref, shg_ref, off, tm), axis=0, keepdims=True)
        val, gate = uc[:, :CONV_CH], uc[:, CONV_CH:]
        sgate = _sigmoid(gate)
        duc_ref[:, :CONV_CH] = (dglu * sgate).astype(BF16)
        duc_ref[:, CONV_CH:] = (dglu * val * sgate * (1.0 - sgate)).astype(BF16)

    vec = pl.BlockSpec((1, CONV_CH), lambda i: (0, 0))
    nxt = lambda i: (jnp.minimum((i + 1) * hb, S // HALO - 1), 0)
    return pl.pallas_call(
        body, name="conv_bwd", grid=(nb,),
        in_specs=[pl.BlockSpec((tm, 2 * CONV_CH), lambda i: (i, 0)),
                  pl.BlockSpec((HALO, 2 * CONV_CH), lambda i: (jnp.maximum(i * hb - 1, 0), 0)),
                  pl.BlockSpec((tm, CONV_CH), lambda i: (i, 0)), pl.BlockSpec((HALO, CONV_CH), nxt),
                  pl.BlockSpec((tm, CONV_CH), lambda i: (i, 0)), pl.BlockSpec((HALO, CONV_CH), nxt),
                  _const_spec(cwf.shape), _const_spec((1, CONV_CH)), _const_spec((1, CONV_CH))],
        out_specs=[pl.BlockSpec((tm, 2 * CONV_CH), lambda i: (i, 0)),
                   pl.BlockSpec(cwf.shape, lambda i: (0, 0)), vec, vec, vec],
        out_shape=[jax.ShapeDtypeStruct((S, 2 * CONV_CH), BF16), jax.ShapeDtypeStruct(cwf.shape, F32)]
        + [jax.ShapeDtypeStruct((1, CONV_CH), F32)] * 3,
        scratch_shapes=[pltpu.VMEM((ext + SUBLANES, CONV_CH), F32), pltpu.VMEM((ext + SUBLANES, CONV_CH), F32),
                        pltpu.VMEM((SUBLANES - 1, ext, CONV_CH), F32),
                        pltpu.VMEM((SUBLANES - 1, ext, CONV_CH), F32)],
        compiler_params=_cp(("arbitrary",)),
    )(uc, uc, yconv, yconv, dco, dco, cwf, lg, lb)


def _in_bwd(duc, dq, dk, dv, x2, dh1, g1, wa, tm):
    S = x2.shape[0]

    def body(duc_ref, dq_ref, dk_ref, dv_ref, x_ref, dh1_ref, g_ref, w_ref, gx_ref, du_ref, dg_ref):
        i = pl.program_id(0)
        du = jnp.concatenate([duc_ref[...], dq_ref[...].astype(BF16), dk_ref[...].astype(BF16),
                              dv_ref[...].astype(BF16)], axis=1)
        du_ref[...] = du
        da = _dot_nt(du[:, 0:IN_SH], w_ref[0])
        for j in range(1, N_CHIPS):
            da = da + _dot_nt(du[:, IN_SH * j:IN_SH * (j + 1)], w_ref[j])
        x = x_ref[...]
        r = _rms(x)
        dx, dg = _rms_bwd(da, x * r, r, g_ref[...])
        gx_ref[...] = dh1_ref[...] + dx

        @pl.when(i == 0)
        def _():
            dg_ref[...] = jnp.zeros_like(dg_ref)

        dg_ref[...] += jnp.sum(dg, axis=0, keepdims=True)

    row = lambda w: pl.BlockSpec((tm, w), lambda i: (i, 0))
    return pl.pallas_call(
        body, name="in_bwd", grid=(S // tm,),
        in_specs=[row(2 * CONV_CH), row(CONV_CH), row(CONV_CH), row(CONV_CH), row(D_MODEL), row(D_MODEL),
                  _const_spec((1, D_MODEL)), _const_spec(wa.shape)],
        out_specs=[pl.BlockSpec((None, tm, D_MODEL), lambda i: (0, i, 0)), row(2560),
                   pl.BlockSpec((1, D_MODEL), lambda i: (0, 0))],
        out_shape=[jax.ShapeDtypeStruct((1, S, D_MODEL), F32), jax.ShapeDtypeStruct((S, 2560), BF16),
                   jax.ShapeDtypeStruct((1, D_MODEL), F32)],
        compiler_params=_cp(("arbitrary",)),
    )(duc, dq, dk, dv, x2, dh1, g1, wa)


def _matmul_tn(xm, ym, tm, ts, name, column_block=None):
    S, M = xm.shape
    N = ym.shape[1]

    def body(x_ref, y_ref, o_ref):
        @pl.when(pl.program_id(1) == 0)
        def _():
            o_ref[...] = jnp.zeros_like(o_ref)

        xt = x_ref[...].T
        if column_block is None:
            o_ref[...] += _dot(xt, y_ref[...])
        else:
            for j in range(N // column_block):
                o_ref[j] += _dot(xt, y_ref[:, column_block * j:column_block * (j + 1)])

    if column_block is None:
        out_spec = pl.BlockSpec((tm, N), lambda m, s: (m, 0))
        out_shape = jax.ShapeDtypeStruct((M, N), F32)
    else:
        out_spec = pl.BlockSpec((N // column_block, tm, column_block), lambda m, s: (0, m, 0))
        out_shape = jax.ShapeDtypeStruct((N // column_block, M, column_block), F32)
    return pl.pallas_call(
        body, name=name, grid=(M // tm, S // ts),
        in_specs=[pl.BlockSpec((ts, tm), lambda m, s: (s, m)), pl.BlockSpec((ts, N), lambda m, s: (s, 0))],
        out_specs=out_spec, out_shape=out_shape,
        compiler_params=_cp(("parallel", "arbitrary")),
    )(xm, ym)


def _sibling_halves(grads, name):
    n = len(grads)

    def body(*refs):
        ins, outs, ssem, rsem = refs[:n], refs[n:2 * n], refs[2 * n], refs[2 * n + 1]
        x, y, c = lax.axis_index("x"), lax.axis_index("y"), lax.axis_index("c")
        copies = []
        for k in range(n):
            for j in range(N_CHIPS):
                copies.append(pltpu.make_async_remote_copy(
                    src_ref=ins[k].at[j, 1 - c], dst_ref=outs[k].at[j],
                    send_sem=ssem.at[N_CHIPS * k + j], recv_sem=rsem.at[N_CHIPS * k + j],
                    device_id=(x, y, 1 - c), device_id_type=MESH))
        for cp in copies:
            cp.start()
        for cp in copies:
            cp.wait()

    shapes = [jax.ShapeDtypeStruct((g.shape[0],) + g.shape[2:], F32) for g in grads]
    return pl.pallas_call(
        body, name=name, out_shape=shapes,
        in_specs=[_hbm()] * n, out_specs=[_hbm()] * n,
        scratch_shapes=[pltpu.SemaphoreType.DMA((N_CHIPS * n,)), pltpu.SemaphoreType.DMA((N_CHIPS * n,))],
    )(*grads)


def _add_half(c_arr, g, landed, name):
    def body(c_ref, g_ref, l_ref, o_ref):
        o_ref[...] = (g_ref[...] + l_ref[...]).astype(BF16)

    rows, n = g.shape[2], g.shape[3]
    grid = (N_CHIPS,)
    g_spec = pl.BlockSpec((None, None, rows, n), lambda j, c: (j, c[0], 0, 0))
    l_spec = pl.BlockSpec((None, rows, n), lambda j, c: (j, 0, 0))
    return pl.pallas_call(
        body, name=name,
        grid_spec=pltpu.PrefetchScalarGridSpec(num_scalar_prefetch=1, grid=grid, in_specs=[g_spec, l_spec],
                                               out_specs=l_spec),
        out_shape=jax.ShapeDtypeStruct(landed.shape, BF16),
        compiler_params=_cp(("parallel",)),
    )(c_arr, g, landed)


class _ScatterPlan:
    def __init__(self, ins, outs, lsem, ssem, rsem):
        x, y, c = lax.axis_index("x"), lax.axis_index("y"), lax.axis_index("c")
        me = 2 * x + y
        self.copies = []
        for k in range(len(ins)):
            self.copies.append(pltpu.make_async_copy(ins[k].at[me], outs[k].at[me], lsem.at[k]))
            for r, chip in enumerate([(1 - x, y), (x, 1 - y), (1 - x, 1 - y)]):
                self.copies.append(pltpu.make_async_remote_copy(
                    src_ref=ins[k].at[2 * chip[0] + chip[1]], dst_ref=outs[k].at[me],
                    send_sem=ssem.at[3 * k + r], recv_sem=rsem.at[3 * k + r],
                    device_id=(chip[0], chip[1], c), device_id_type=MESH))

    def start(self):
        for cp in self.copies:
            cp.start()

    def finish(self):
        for cp in self.copies:
            cp.wait()


def _scatter_sems(n):
    return [pltpu.SemaphoreType.DMA((n,)), pltpu.SemaphoreType.DMA((3 * n,)), pltpu.SemaphoreType.DMA((3 * n,))]


def _chip_scatter(parts):
    n = len(parts)

    def body(*refs):
        plan = _ScatterPlan(refs[:n], refs[n:2 * n], *refs[2 * n:])
        plan.start()
        plan.finish()

    shapes = [jax.ShapeDtypeStruct(p.shape, p.dtype) for p in parts]
    return pl.pallas_call(
        body, name="grad_chip_scatter", out_shape=shapes,
        in_specs=[_hbm()] * n, out_specs=[_hbm()] * n, scratch_shapes=_scatter_sems(n),
    )(*parts)


def _sum_chips(landed, name):
    _, rows, n = landed.shape
    tr = 256 if rows % 256 == 0 else rows

    def body(a_ref, b_ref, c_ref, d_ref, o_ref):
        f = lambda ref: ref[...].astype(F32)
        o_ref[...] = ((f(a_ref) + f(b_ref)) + f(c_ref)) + f(d_ref)

    specs = [pl.BlockSpec((None, tr, n), functools.partial(lambda i, j: (j, i, 0), j=j)) for j in range(N_CHIPS)]
    return pl.pallas_call(
        body, name=name, grid=(rows // tr,), in_specs=specs,
        out_specs=pl.BlockSpec((tr, n), lambda i: (i, 0)),
        out_shape=jax.ShapeDtypeStruct((rows, n), F32),
        compiler_params=_cp(("parallel",)),
    )(landed, landed, landed, landed)


def _share_halves(halves):
    n = len(halves)

    def body(*refs):
        ins, outs = refs[:n], refs[n:2 * n]
        ssem, rsem = refs[2 * n:]
        x, y, c = lax.axis_index("x"), lax.axis_index("y"), lax.axis_index("c")
        copies = [pltpu.make_async_remote_copy(
            src_ref=ins[k], dst_ref=outs[k], send_sem=ssem.at[k], recv_sem=rsem.at[k],
            device_id=(x, y, 1 - c), device_id_type=MESH) for k in range(n)]
        for cp in copies:
            cp.start()
        for cp in copies:
            cp.wait()

    shapes = [jax.ShapeDtypeStruct(h.shape, F32) for h in halves]
    return pl.pallas_call(
        body, name="grad_share_halves", out_shape=shapes,
        in_specs=[_hbm()] * n, out_specs=[_hbm()] * n,
        scratch_shapes=[pltpu.SemaphoreType.DMA((n,)), pltpu.SemaphoreType.DMA((n,))],
    )(*halves)


def _allreduce_small(packed):
    rows, n = packed.shape

    def body(in_ref, out_ref, land_ref, ssem, rsem):
        x, y, c = lax.axis_index("x"), lax.axis_index("y"), lax.axis_index("c")
        me = 4 * x + 2 * y + c
        land_ref[me] = in_ref[...]
        copies = []
        for r in range(1, 8):
            tx = 1 - x if r & 4 else x
            ty = 1 - y if r & 2 else y
            tc = 1 - c if r & 1 else c
            cp = pltpu.make_async_remote_copy(
                src_ref=in_ref, dst_ref=land_ref.at[me], send_sem=ssem.at[r - 1], recv_sem=rsem.at[r - 1],
                device_id=(tx, ty, tc), device_id_type=MESH)
            cp.start()
            copies.append(cp)
        for cp in copies:
            cp.wait()
        acc = land_ref[0]
        for k in range(1, 8):
            acc = acc + land_ref[k]
        out_ref[...] = acc

    return pl.pallas_call(
        body, name="allreduce_small", out_shape=jax.ShapeDtypeStruct((rows, n), F32),
        in_specs=[pl.BlockSpec(memory_space=pltpu.VMEM)], out_specs=pl.BlockSpec(memory_space=pltpu.VMEM),
        scratch_shapes=[pltpu.VMEM((8, rows, n), F32), pltpu.SemaphoreType.DMA((7,)),
                        pltpu.SemaphoreType.DMA((7,))],
    )(packed)


def _adamw_math(w, g, m, v):
    m = ADAM_B1 * m + (1.0 - ADAM_B1) * g
    v = ADAM_B2 * v + (1.0 - ADAM_B2) * (g * g)
    m_hat = m / (1.0 - ADAM_B1 ** ADAM_STEP)
    v_hat = v / (1.0 - ADAM_B2 ** ADAM_STEP)
    return -ADAM_LR * (m_hat / (jnp.sqrt(v_hat) + ADAM_EPS) + ADAM_WD * w), m, v


def _adamw_halves(c_arr, w, mine, other, m, v, name):
    rows, n = mine.shape
    tr = 256 if rows % 256 == 0 else rows
    nb = rows // tr

    def body(c_ref, w_ref, a_ref, b_ref, m_ref, v_ref, g_ref, d_ref, mo_ref, vo_ref):
        g = jnp.where(pl.program_id(0) == c_ref[0], a_ref[...], b_ref[...])
        g_ref[...] = g
        d_ref[...], mo_ref[...], vo_ref[...] = _adamw_math(w_ref[...], g, m_ref[...], v_ref[...])

    full = pl.BlockSpec((None, tr, n), lambda h, i, c: (0, h * nb + i, 0))
    half = pl.BlockSpec((tr, n), lambda h, i, c: (i, 0))
    return pl.pallas_call(
        body, name=name,
        grid_spec=pltpu.PrefetchScalarGridSpec(num_scalar_prefetch=1, grid=(2, nb),
                                               in_specs=[full, half, half, full, full], out_specs=[full] * 4),
        out_shape=[jax.ShapeDtypeStruct((1, 2 * rows, n), F32)] * 4,
        compiler_params=_cp(("parallel", "parallel")),
    )(c_arr, w, mine, other, m, v)


def _adamw(w, g, m, v, name):
    rows, n = w.shape
    tr = 256 if rows % 256 == 0 else rows

    def body(w_ref, g_ref, m_ref, v_ref, d_ref, mo_ref, vo_ref):
        d_ref[...], mo_ref[...], vo_ref[...] = _adamw_math(w_ref[...], g_ref[...], m_ref[...], v_ref[...])

    spec = pl.BlockSpec((tr, n), lambda i: (i, 0))
    return pl.pallas_call(
        body, name=name, grid=(rows // tr,), in_specs=[spec] * 4, out_specs=[spec] * 3,
        out_shape=[jax.ShapeDtypeStruct((rows, n), F32)] * 3,
        compiler_params=_cp(("parallel",)),
    )(w, g, m, v)


def _rows8(a):
    a = a.reshape(-1, 128)
    return jnp.pad(a, ((0, (-a.shape[0]) % 8), (0, 0)))


def kernel(x, g_pre_mix, w_in, conv_w, conv_b, conv_ln_g, conv_ln_b, attn_norm_g, w_out, g_post_mix, g_pre_ffn, w_gate, w_up, w_down, g_post_ffn, loss_target, m_g_pre_mix, m_w_in, m_conv_w, m_conv_b, m_conv_ln_g, m_conv_ln_b, m_attn_norm_g, m_w_out, m_g_post_mix, m_g_pre_ffn, m_w_gate, m_w_up, m_w_down, m_g_post_ffn, v_g_pre_mix, v_w_in, v_conv_w, v_conv_b, v_conv_ln_g, v_conv_ln_b, v_attn_norm_g, v_w_out, v_g_post_mix, v_g_pre_ffn, v_w_gate, v_w_up, v_w_down, v_g_post_ffn):
    S = x.shape[1]
    tm_big = min(512, S)
    tm_ffn = min(256, S)
    tk_att = min(1024, S)
    t_att_fwd = min(1024, S)
    t_att_bwd = tk_att // KEY_BLOCKS
    chip = 2 * lax.axis_index("x") + lax.axis_index("y")
    core = lax.axis_index("c")
    x2 = x.reshape(S, D_MODEL)
    tgt = loss_target.reshape(S, D_MODEL)
    ag = attn_norm_g.reshape(1, CONV_CH)

    a_sh = w_in[0].astype(BF16)
    b_sh = jnp.stack([w_gate[0], w_up[0]]).astype(BF16)
    c_sh = jnp.concatenate([w_out[0], w_down[0]], axis=0).astype(BF16)
    cw_sh = jnp.pad(conv_w[0, :, 0, :], ((0, 1), (0, 0)))
    own = lambda full, shard: lax.dynamic_update_index_in_dim(full, shard, chip, 0)
    cols = lambda w4: jnp.concatenate([w4[j] for j in range(N_CHIPS)], axis=1)
    wa4, cw4 = _gather_weights([a_sh, cw_sh], [False, False])
    wa = own(wa4, a_sh)
    cwf = cols(own(cw4, cw_sh))

    a_bf, uc, qkv = _in_proj(x2, g_pre_mix, wa, tm_big)
    conv_out, yconv = _conv_fwd(uc, cwf, conv_b, conv_ln_g, conv_ln_b, tm_big)
    o, wb4, wc4 = _attn_fwd(qkv, t_att_fwd, tk_att, [b_sh, c_sh], [True, False])
    wb4, wc4 = own(wb4, b_sh), own(wc4, c_sh)
    wg, wu = cols(wb4[:, 0]), cols(wb4[:, 1])
    wo = wc4[:, :OUT_SH].reshape(D_MODEL, D_MODEL)
    wd = wc4[:, OUT_SH:].reshape(D_FF, D_MODEL)
    mixed, yv, h1, f_in = _out_proj(conv_out, o, ag, wo, x2, g_post_mix, g_pre_ffn, tm_big)
    df, dh2, dg4, loss_part, gt_bf, up_bf, act = _ffn_fwd(f_in, h1, tgt, wg, wu, wd, g_post_ffn, tm_ffn)

    dgt, dup, dh1, dy, dg3, dg2 = _ffn_bwd(gt_bf, up_bf, df, dh2, h1, yv, wg, wu, wd, g_pre_ffn, g_post_mix, tm_ffn)
    dco, do, dag = _out_bwd(dy, o, ag, wo, tm_big)
    ts = min(1024, S)
    gw_out = _matmul_tn(mixed, dy, D_MODEL, ts, "grad_w_out")
    gw_gate = _matmul_tn(f_in, dgt, D_MODEL, ts, "grad_w_gate")
    gw_up = _matmul_tn(f_in, dup, D_MODEL, ts, "grad_w_up")
    gw_down = _matmul_tn(act, df, D_FF // 2, ts, "grad_w_down")

    by_cols = lambda g: jnp.transpose(g.reshape(2, D_MODEL // 2, N_CHIPS, -1), (2, 0, 1, 3))
    by_rows = lambda g: g.reshape(N_CHIPS, 2, g.shape[0] // (2 * N_CHIPS), g.shape[1])
    c_arr = core.reshape(1).astype(jnp.int32)

    def chip_partials(views, nms):
        landed = _sibling_halves(views, "grad_sibling_halves_" + nms[0])
        return [_add_half(c_arr, g, l, "grad_half_" + nm) for g, l, nm in zip(views, landed, nms)]

    early = ["w_gate", "w_up", "w_out", "w_down"]
    parts = chip_partials([by_cols(gw_gate), by_cols(gw_up), by_rows(gw_out), by_rows(gw_down)], early)
    dq, dk, dv, *slots = _attn_bwd(qkv, do, t_att_bwd, tk_att, parts)
    duc, dcw, dcb, dlg, dlb = _conv_bwd(uc, yconv, dco, cwf, conv_ln_g, conv_ln_b, tm_big)
    grad_x, du, dg1 = _in_bwd(duc, dq, dk, dv, x2, dh1, g_pre_mix, wa, tm_big)
    gw_in = _matmul_tn(a_bf, du, D_MODEL, ts, "grad_w_in", column_block=IN_SH)
    slots += _chip_scatter(chip_partials([gw_in.reshape(N_CHIPS, 2, D_MODEL // 2, IN_SH)], ["w_in"]))
    names = early + ["w_in"]
    halves = [_sum_chips(s, "grad_sum_" + nm) for s, nm in zip(slots, names)]
    others = _share_halves(halves)
    mine = dict(zip(names, halves))
    other = dict(zip(names, others))

    small = [dg1, dcb, dlg, dlb, dag, dg2, dg3, dg4]
    packed = jnp.concatenate([_rows8(s) for s in small] + [_rows8(dcw), _rows8(loss_part)], axis=0)
    red = _allreduce_small(packed)
    sizes = [D_MODEL, CONV_CH, CONV_CH, CONV_CH, CONV_CH, D_MODEL, D_MODEL, D_MODEL]
    g_small = [red[8 * k:8 * k + n // 128].reshape(1, n) for k, n in enumerate(sizes)]
    cw_red = red[64:64 + 128].reshape(HALO, CONV_CH)
    g_cw = lax.dynamic_slice(cw_red, (0, chip * 128), (HALO, 128))
    loss = red[192, 0]

    big = []
    for w, m, v, nm in [(w_in, m_w_in, v_w_in, "w_in"), (w_out, m_w_out, v_w_out, "w_out"),
                        (w_gate, m_w_gate, v_w_gate, "w_gate"), (w_up, m_w_up, v_w_up, "w_up"),
                        (w_down, m_w_down, v_w_down, "w_down")]:
        big.append(_adamw_halves(c_arr, w, mine[nm], other[nm], m, v, "adamw_" + nm))
    sm_w = [g_pre_mix, conv_b, conv_ln_g, conv_ln_b, ag, g_post_mix, g_pre_ffn, g_post_ffn]
    sm_m = [m_g_pre_mix, m_conv_b, m_conv_ln_g, m_conv_ln_b, m_attn_norm_g, m_g_post_mix, m_g_pre_ffn, m_g_post_ffn]
    sm_v = [v_g_pre_mix, v_conv_b, v_conv_ln_g, v_conv_ln_b, v_attn_norm_g, v_g_post_mix, v_g_pre_ffn, v_g_post_ffn]
    pad_cw = lambda a: jnp.pad(a[0, :, 0, :], ((0, 1), (0, 0)))

    def pack(vecs, cw):
        return jnp.concatenate([_rows8(a) for a in vecs] + [cw], axis=0)

    sd, smn, svn = _adamw(pack(sm_w, pad_cw(conv_w)), pack(g_small, g_cw), pack(sm_m, pad_cw(m_conv_w)),
                          pack(sm_v, pad_cw(v_conv_w)), "adamw_small")

    def unpack(p):
        vecs = [p[8 * k:8 * k + n // 128].reshape(1, n) for k, n in enumerate(sizes)]
        return vecs, p[64:64 + CONV_WIDTH].reshape(1, CONV_WIDTH, 1, 128)

    def ordered(vecs, cw, w_in_, w_out_, w_gate_, w_up_, w_down_):
        g1_, cb_, lg_, lb_, ag_, g2_, g3_, g4_ = vecs
        return [g1_, w_in_, cw, cb_, lg_, lb_, ag_.reshape(1, 8, HEAD_DIM), w_out_, g2_, g3_,
                w_gate_, w_up_, w_down_, g4_]

    grads = ordered(g_small, g_cw[:CONV_WIDTH].reshape(1, CONV_WIDTH, 1, 128), *[b[0] for b in big])
    outs = []
    for idx, p in enumerate((sd, smn, svn)):
        vecs, cw = unpack(p)
        outs += ordered(vecs, cw, *[b[idx + 1] for b in big])
    return (loss, grad_x, *grads, *outs)
```

```python
import functools

import jax
import jax.numpy as jnp
from jax import lax
from jax.experimental import pallas as pl
from jax.experimental.pallas import tpu as pltpu

F32 = jnp.float32
BF16 = jnp.bfloat16
MESH = pl.DeviceIdType.MESH

D_MODEL = 1024
CONV_CH = 512
CONV_WIDTH = 31
HEAD_DIM = 64
PAIR = 2 * HEAD_DIM
N_PAIRS = 4
D_FF = 2816
N_CHIPS = 4
IN_SH = 2560 // N_CHIPS
OUT_SH = D_MODEL // N_CHIPS
EPS = 1e-6
HALO = 32

ADAM_LR = 0.001
ADAM_B1 = 0.9
ADAM_B2 = 0.999
ADAM_EPS = 1e-08
ADAM_WD = 0.01
ADAM_STEP = 10

VMEM_LIMIT = 56 * 2 ** 20
VMEM_LIMIT_ATTN_BWD = 62 * 2 ** 20


def _cp(sem=None, vmem=VMEM_LIMIT):
    return pltpu.CompilerParams(dimension_semantics=sem, vmem_limit_bytes=vmem)


def _hbm():
    return pl.BlockSpec(memory_space=pltpu.HBM)


def _const_spec(shape):
    nd = len(shape)
    return pl.BlockSpec(shape, lambda *_: (0,) * nd, pipeline_mode=pl.Buffered(1))


def _dot(a, b):
    return jnp.dot(a, b, preferred_element_type=F32)


def _dot_nt(a, b):
    return lax.dot_general(a, b, (((1,), (1,)), ((), ())), preferred_element_type=F32)


def _dot_tn(a, b):
    return lax.dot_general(a, b, (((0,), (0,)), ((), ())), preferred_element_type=F32)


def _split3(x):
    b0 = x.astype(BF16)
    r1 = x - b0.astype(F32)
    b1 = r1.astype(BF16)
    b2 = (r1 - b1.astype(F32)).astype(BF16)
    return b0, b1, b2


def _sigmoid(x):
    return 1.0 / (1.0 + jnp.exp(-x))


def _head_mean(x, seg):
    b0, b1, b2 = _split3(x)
    return (_dot(b0, seg) + _dot(b1, seg) + _dot(b2, seg)) * (1.0 / HEAD_DIM)


def _seg_matrix(n):
    r = lax.broadcasted_iota(jnp.int32, (n, n), 0) // HEAD_DIM
    c = lax.broadcasted_iota(jnp.int32, (n, n), 1) // HEAD_DIM
    return (r == c).astype(BF16)


def _rms(x):
    return lax.rsqrt(jnp.mean(x * x, axis=-1, keepdims=True) + EPS)


def _rms_bwd(dy, n, r, g):
    dn = dy * g
    dx = r * (dn - n * jnp.mean(dn * n, axis=-1, keepdims=True))
    return dx, dy * n


class _GatherPlan:
    def __init__(self, srcs, outs, lead, ssem, rsem):
        self.srcs, self.outs, self.lead, self.ssem, self.rsem = srcs, outs, lead, ssem, rsem
        x, y, self.c = lax.axis_index("x"), lax.axis_index("y"), lax.axis_index("c")
        self.me = 2 * x + y
        self.sibling = (x, y, 1 - self.c)
        self.chips = [(1 - x, y), (x, 1 - y), (1 - x, 1 - y)]

    def _half(self, ref, i, h):
        if self.lead[i]:
            return ref.at[h]
        rows = ref.shape[0] // 2
        return ref.at[pl.ds(h * rows, rows)]

    def _ici(self, i, k, origin):
        return pltpu.make_async_remote_copy(
            src_ref=self._half(self.srcs[i], i, self.c), dst_ref=self._half(self.outs[i].at[origin], i, self.c),
            send_sem=self.ssem.at[6 * i + k], recv_sem=self.rsem.at[6 * i + k],
            device_id=(self.chips[k][0], self.chips[k][1], self.c), device_id_type=MESH)

    def _d2d(self, i, k, h):
        origin = 2 * self.chips[k][0] + self.chips[k][1]
        piece = self._half(self.outs[i].at[origin], i, h)
        return pltpu.make_async_remote_copy(
            src_ref=piece, dst_ref=piece, send_sem=self.ssem.at[6 * i + 3 + k],
            recv_sem=self.rsem.at[6 * i + 3 + k], device_id=self.sibling, device_id_type=MESH)

    def _each(self):
        return [(i, k) for i in range(len(self.srcs)) for k in range(3)]

    def start(self):
        for i, k in self._each():
            self._ici(i, k, self.me).start()

    def forward(self):
        for i, k in self._each():
            self._ici(i, k, 2 * self.chips[k][0] + self.chips[k][1]).wait_recv()
            self._d2d(i, k, self.c).start()

    def finish(self):
        for i, k in self._each():
            self._d2d(i, k, 1 - self.c).wait_recv()
        for i, k in self._each():
            self._ici(i, k, self.me).wait_send()
            self._d2d(i, k, self.c).wait_send()


def _gather_shapes(shards):
    return [jax.ShapeDtypeStruct((N_CHIPS,) + s.shape, s.dtype) for s in shards]


def _gather_weights(shards, lead):
    n = len(shards)

    def body(*refs):
        plan = _GatherPlan(refs[:n], refs[n:2 * n], lead, refs[2 * n], refs[2 * n + 1])
        plan.start()
        plan.forward()
        plan.finish()

    return pl.pallas_call(
        body, name="gather_weights", out_shape=_gather_shapes(shards),
        in_specs=[_hbm()] * n, out_specs=[_hbm()] * n,
        scratch_shapes=[pltpu.SemaphoreType.DMA((6 * n,)), pltpu.SemaphoreType.DMA((6 * n,))],
    )(*shards)


def _in_proj(x2, g1, wa, tm):
    S = x2.shape[0]

    def body(x_ref, g_ref, w_ref, a_ref, uc_ref, qkv_ref):
        x = x_ref[...]
        a = (x * _rms(x) * g_ref[...]).astype(BF16)
        a_ref[...] = a
        u = [_dot(a, w_ref[j]) for j in range(N_CHIPS)]
        uc_ref[:, 0:640] = u[0]
        uc_ref[:, 640:1024] = u[1][:, 0:384]
        qkv_ref[:, 0:256] = u[1][:, 384:640].astype(BF16)
        qkv_ref[:, 256:896] = u[2].astype(BF16)
        qkv_ref[:, 896:1536] = u[3].astype(BF16)

    return pl.pallas_call(
        body, name="in_proj", grid=(S // tm,),
        in_specs=[pl.BlockSpec((tm, D_MODEL), lambda i: (i, 0)), _const_spec((1, D_MODEL)),
                  _const_spec(wa.shape)],
        out_specs=[pl.BlockSpec((tm, D_MODEL), lambda i: (i, 0)),
                   pl.BlockSpec((tm, 2 * CONV_CH), lambda i: (i, 0)),
                   pl.BlockSpec((tm, 1536), lambda i: (i, 0))],
        out_shape=[jax.ShapeDtypeStruct((S, D_MODEL), BF16), jax.ShapeDtypeStruct((S, 2 * CONV_CH), F32),
                   jax.ShapeDtypeStruct((S, 1536), BF16)],
        compiler_params=_cp(("parallel",)),
    )(x2, g1, wa)


SUBLANES = 8


def _shift_copies(src_ref, sh_ref):
    rows = sh_ref.shape[1]
    for b in range(1, SUBLANES):
        sh_ref[b - 1] = src_ref[pl.ds(b, rows), :]


def _rows_at(src_ref, sh_ref, off, rows):
    a, b = divmod(off, SUBLANES)
    if b == 0:
        return src_ref[pl.ds(SUBLANES * a, rows), :]
    return sh_ref[b - 1, pl.ds(SUBLANES * a, rows), :]


def _conv_taps(cw_ref, src_ref, sh_ref, offs, rows):
    acc = None
    for w, off in enumerate(offs):
        term = cw_ref[w:w + 1, :] * _rows_at(src_ref, sh_ref, off, rows)
        acc = term if acc is None else acc + term
    return acc


def _glu(uc):
    return uc[:, :CONV_CH] * _sigmoid(uc[:, CONV_CH:])


def _conv_fwd(uc, cwf, cb, lg, lb, tm):
    S = uc.shape[0]
    hb = tm // HALO

    def body(uc_ref, prev_ref, cw_ref, cb_ref, lg_ref, lb_ref, out_ref, y_ref, glu_ref, sh_ref):
        i = pl.program_id(0)
        glu_ref[0:HALO, :] = jnp.where(i == 0, 0.0, _glu(prev_ref[...]))
        glu_ref[HALO:HALO + tm, :] = _glu(uc_ref[...])
        glu_ref[HALO + tm:HALO + tm + SUBLANES, :] = jnp.zeros((SUBLANES, CONV_CH), F32)
        _shift_copies(glu_ref, sh_ref)
        offs = [HALO - (CONV_WIDTH - 1) + w for w in range(CONV_WIDTH)]
        y = _conv_taps(cw_ref, glu_ref, sh_ref, offs, tm) + cb_ref[...]
        y_ref[...] = y
        mu = jnp.mean(y, axis=-1, keepdims=True)
        yc = y - mu
        rstd = lax.rsqrt(jnp.mean(yc * yc, axis=-1, keepdims=True) + EPS)
        ln = yc * rstd * lg_ref[...] + lb_ref[...]
        out_ref[...] = (ln * _sigmoid(ln)).astype(BF16)

    return pl.pallas_call(
        body, name="conv_fwd", grid=(S // tm,),
        in_specs=[pl.BlockSpec((tm, 2 * CONV_CH), lambda i: (i, 0)),
                  pl.BlockSpec((HALO, 2 * CONV_CH), lambda i: (jnp.maximum(i * hb - 1, 0), 0)),
                  _const_spec(cwf.shape), _const_spec((1, CONV_CH)), _const_spec((1, CONV_CH)),
                  _const_spec((1, CONV_CH))],
        out_specs=[pl.BlockSpec((tm, CONV_CH), lambda i: (i, 0))] * 2,
        out_shape=[jax.ShapeDtypeStruct((S, CONV_CH), BF16), jax.ShapeDtypeStruct((S, CONV_CH), F32)],
        scratch_shapes=[pltpu.VMEM((HALO + tm + SUBLANES, CONV_CH), F32),
                        pltpu.VMEM((SUBLANES - 1, HALO + tm, CONV_CH), F32)],
        compiler_params=_cp(("parallel",)),
    )(uc, uc, cwf, cb, lg, lb)


def _lane_mask(h):
    lane = lax.broadcasted_iota(jnp.int32, (1, PAIR), 1)
    return (lane >= HEAD_DIM * h) & (lane < HEAD_DIM * (h + 1))


def _neg_abs(x):
    bits = lax.bitcast_convert_type(x, jnp.uint32) | jnp.uint32(0x80000000)
    return lax.bitcast_convert_type(bits, F32)


def _tri_dot(x, m):
    return _dot(x.astype(BF16), m)


MASKED = -1e30
KEY_BLOCKS = 4


def _running_sums(x, m, reverse, start=None):
    t = m.shape[0]
    blocks = x.shape[1] // t
    order = range(blocks - 1, -1, -1) if reverse else range(blocks)
    out = [None] * blocks
    carry = start
    for b in order:
        xb = x[:, b * t:(b + 1) * t]
        cb = _tri_dot(xb, m)
        out[b] = cb if carry is None else cb + carry
        rs = jnp.sum(xb, axis=1, keepdims=True)
        carry = rs if carry is None else carry + rs
    return jnp.concatenate(out, axis=1), carry


def _sb_tile(z, r, m_suf):
    sp = jnp.maximum(z, 0.0) + jnp.log(1.0 + jnp.exp(_neg_abs(z)))
    c, rs = _running_sums(sp, m_suf, reverse=True, start=r)
    return jnp.exp(z - c), sp, rs


def _scores(qm, kt, mask):
    z = _dot_nt(qm, kt)
    return z if mask is None else jnp.where(mask, z, MASKED)


def _causal_mask(i, sb, t, tk, w=None):
    w = tk if w is None else w
    row = lax.broadcasted_iota(jnp.int32, (t, w), 0) + i * t
    col = lax.broadcasted_iota(jnp.int32, (t, w), 1) + sb * tk
    return col < row


def _sweep_pairs(count, tile):
    def step(n, carry):
        tile(2 * n)
        tile(2 * n + 1)
        return carry

    lax.fori_loop(0, count // 2, step, 0)
    pl.when(lax.rem(count, 2) == 1)(lambda: tile(count - 1))


def _sweep(first, count, down, fetch, load, work, fetched=False):
    lo, hi = (first - count, first) if down else (first, first + count)
    tile = lambda j: jnp.clip(first - j if down else first + j, lo, hi)
    if not fetched:
        fetch(first, 0, True)

    def step(n, carry):
        j = 2 * n
        vals = load(0)
        fetch(tile(j + 1), 1, False)
        work(tile(j), vals)
        vals = load(1)
        fetch(tile(j + 2), 0, False)
        work(tile(j + 1), vals)
        return carry

    lax.fori_loop(0, (count + 1) // 2, step, 0)

    @pl.when(lax.rem(count, 2) == 0)
    def _():
        work(tile(count), load(0))


def _suffix_matrix(t, prefix=False):
    row = lax.broadcasted_iota(jnp.int32, (t, t), 0)
    col = lax.broadcasted_iota(jnp.int32, (t, t), 1)
    return ((row <= col) if prefix else (row >= col)).astype(BF16)


def _attn_fwd(qkv, t, tk, shards, lead):
    S = qkv.shape[0]
    assert t == tk, "the forward cuts its diagonal tile by blocks: query block and key tile must match"

    ng = len(shards)
    nq = S // t

    def body(*refs):
        q_ref, k_ref, v_ref = refs[:3]
        o_ref = refs[3 + ng]
        acc_ref, r_ref, z_buf, ssem, rsem = refs[4 + 2 * ng:]
        p = pl.program_id(0)
        i = pl.program_id(1)
        plan = _GatherPlan(refs[3:3 + ng], refs[4 + ng:4 + 2 * ng], lead, ssem, rsem)
        pl.when((p == 0) & (i == 0))(plan.start)
        pl.when((p == 1) & (i == 0))(plan.forward)
        blk = tk // KEY_BLOCKS
        m_suf = _suffix_matrix(blk)
        q = q_ref[...]
        hms = [_lane_mask(h) for h in range(2)]
        qms = [jnp.where(hm, q, 0) * 0.125 for hm in hms]

        def rows(sb, w=tk):
            return pl.ds(pl.multiple_of(sb * tk, tk), w)

        def fetch(sb, slot, first):
            kt = k_ref[rows(sb), :]
            for h in range(2):
                z_buf[slot, h] = _scores(qms[h], kt, None)

        fetch(jnp.maximum(i - 1, 0), 0, True)
        for rb in range(KEY_BLOCKS):
            w = (rb + 1) * blk
            part = slice(rb * blk, w)
            kt = k_ref[rows(i, w), :]
            vt = v_ref[rows(i, w), :]
            row = lax.broadcasted_iota(jnp.int32, (blk, w), 0) + rb * blk
            mask = lax.broadcasted_iota(jnp.int32, (blk, w), 1) < row
            out = jnp.zeros((blk, PAIR), F32)
            for h in range(2):
                a_loc, _, rs = _sb_tile(_scores(qms[h][part], kt, mask), None, m_suf)
                out = out + _dot(a_loc.astype(BF16), jnp.where(hms[h], vt, 0))
                r_ref[h, part] = rs
            acc_ref[part, :] = out

        def load(slot):
            return [z_buf[slot, h] for h in range(2)]

        def work(sb, zs):
            vt = v_ref[rows(sb), :]
            for h in range(2):
                a_loc, _, rs = _sb_tile(zs[h], None, m_suf)
                r = r_ref[h]
                acc_ref[...] += _dot(a_loc.astype(BF16), jnp.where(hms[h], vt, 0)) * jnp.exp(-r)
                r_ref[h] = r + rs

        pl.when(i >= 1)(lambda: _sweep(i - 1, i - 1, True, fetch, load, work, fetched=True))
        o_ref[...] = acc_ref[...]
        pl.when((p == N_PAIRS - 1) & (i == nq - 1))(plan.finish)

    return pl.pallas_call(
        body, name="attn_fwd", grid=(N_PAIRS, nq),
        in_specs=[pl.BlockSpec((t, PAIR), lambda p, i: (i, p)),
                  pl.BlockSpec((S, PAIR), lambda p, i: (0, N_PAIRS + p)),
                  pl.BlockSpec((S, PAIR), lambda p, i: (0, 2 * N_PAIRS + p))] + [_hbm()] * ng,
        out_specs=[pl.BlockSpec((t, PAIR), lambda p, i: (i, p))] + [_hbm()] * ng,
        out_shape=[jax.ShapeDtypeStruct((S, N_PAIRS * PAIR), F32)] + _gather_shapes(shards),
        scratch_shapes=[pltpu.VMEM((t, PAIR), F32), pltpu.VMEM((2, t, 1), F32),
                        pltpu.VMEM((2, 2, t, tk), F32),
                        pltpu.SemaphoreType.DMA((6 * ng,)), pltpu.SemaphoreType.DMA((6 * ng,))],
        compiler_params=_cp(("arbitrary", "arbitrary")),
    )(qkv, qkv, qkv, *shards)


def _out_proj(conv_out, o, ag, wc, x2, g2, g3, tm):
    S = o.shape[0]

    def body(co_ref, o_ref, ag_ref, w_ref, x_ref, g2_ref, g3_ref, mix_ref, y_ref, h1_ref, fin_ref):
        seg = _seg_matrix(CONV_CH)
        o = o_ref[...]
        att = (o * lax.rsqrt(_head_mean(o * o, seg) + EPS) * ag_ref[...]).astype(BF16)
        co = co_ref[...]
        mix_ref[:, :CONV_CH] = co
        mix_ref[:, CONV_CH:] = att
        y = _dot(co, w_ref[0:CONV_CH, :]) + _dot(att, w_ref[CONV_CH:, :])
        y_ref[...] = y
        h1 = x_ref[...] + y * _rms(y) * g2_ref[...]
        h1_ref[...] = h1
        fin_ref[...] = (h1 * _rms(h1) * g3_ref[...]).astype(BF16)

    row = lambda w: pl.BlockSpec((tm, w), lambda i: (i, 0))
    return pl.pallas_call(
        body, name="out_proj", grid=(S // tm,),
        in_specs=[row(CONV_CH), row(CONV_CH), _const_spec((1, CONV_CH)), _const_spec(wc.shape),
                  row(D_MODEL), _const_spec((1, D_MODEL)), _const_spec((1, D_MODEL))],
        out_specs=[row(D_MODEL)] * 4,
        out_shape=[jax.ShapeDtypeStruct((S, D_MODEL), BF16), jax.ShapeDtypeStruct((S, D_MODEL), F32),
                   jax.ShapeDtypeStruct((S, D_MODEL), F32), jax.ShapeDtypeStruct((S, D_MODEL), BF16)],
        compiler_params=_cp(("parallel",)),
    )(conv_out, o, ag, wc, x2, g2, g3)


def _ffn_fwd(f_in, h1, tgt, wg, wu, wd, g4, tm):
    S = f_in.shape[0]

    def body(fin_ref, h1_ref, tgt_ref, wg_ref, wu_ref, wd_ref, g4_ref, df_ref, dh2_ref, dg4_ref, loss_ref,
             gt_ref, up_ref, act_ref):
        i = pl.program_id(0)
        fin = fin_ref[...]
        gt = _dot(fin, wg_ref[...])
        up = _dot(fin, wu_ref[...])
        act = (gt * _sigmoid(gt) * up).astype(BF16)
        gt_ref[...] = gt.astype(BF16)
        up_ref[...] = up.astype(BF16)
        act_ref[...] = act
        f = _dot(act, wd_ref[...])
        r = _rms(f)
        n = f * r
        g4 = g4_ref[...]
        err = h1_ref[...] + n * g4 - tgt_ref[...]
        dh2 = err * (1.0 / D_MODEL)
        dh2_ref[...] = dh2
        df, dg = _rms_bwd(dh2, n, r, g4)
        df_ref[...] = df.astype(BF16)

        @pl.when(i == 0)
        def _():
            dg4_ref[...] = jnp.zeros_like(dg4_ref)
            loss_ref[...] = jnp.zeros_like(loss_ref)

        dg4_ref[...] += jnp.sum(dg, axis=0, keepdims=True)
        part = jnp.sum(jnp.sum(err * err, axis=1, keepdims=True), axis=0, keepdims=True)
        loss_ref[...] += part * (0.5 / D_MODEL)

    row = lambda w: pl.BlockSpec((tm, w), lambda i: (i, 0))
    return pl.pallas_call(
        body, name="ffn_fwd", grid=(S // tm,),
        in_specs=[row(D_MODEL), row(D_MODEL), row(D_MODEL), _const_spec(wg.shape), _const_spec(wu.shape),
                  _const_spec(wd.shape), _const_spec((1, D_MODEL))],
        out_specs=[row(D_MODEL), row(D_MODEL), pl.BlockSpec((1, D_MODEL), lambda i: (0, 0)),
                   pl.BlockSpec((1, 128), lambda i: (0, 0)), row(D_FF), row(D_FF), row(D_FF)],
        out_shape=[jax.ShapeDtypeStruct((S, D_MODEL), BF16), jax.ShapeDtypeStruct((S, D_MODEL), F32),
                   jax.ShapeDtypeStruct((1, D_MODEL), F32), jax.ShapeDtypeStruct((1, 128), F32)]
        + [jax.ShapeDtypeStruct((S, D_FF), BF16)] * 3,
        compiler_params=_cp(("arbitrary",)),
    )(f_in, h1, tgt, wg, wu, wd, g4)


def _ffn_bwd(gt_bf, up_bf, df, dh2, h1, yv, wg, wu, wd, g3, g2, tm):
    S = df.shape[0]

    def body(gt_ref, up_ref, df_ref, dh2_ref, h1_ref, y_ref, wg_ref, wu_ref, wd_ref, g3_ref, g2_ref,
             dgt_ref, dup_ref, dh1_ref, dy_ref, dg3_ref, dg2_ref):
        i = pl.program_id(0)
        df = df_ref[...]
        gt = gt_ref[...].astype(F32)
        up = up_ref[...].astype(F32)
        sg = _sigmoid(gt)
        silu = gt * sg
        dact = _dot_nt(df, wd_ref[...])
        dgt = (dact * up * (sg * (1.0 + gt * (1.0 - sg)))).astype(BF16)
        dup = (dact * silu).astype(BF16)
        dgt_ref[...] = dgt
        dup_ref[...] = dup
        dfin = _dot_nt(dgt, wg_ref[...]) + _dot_nt(dup, wu_ref[...])
        h1 = h1_ref[...]
        r3 = _rms(h1)
        dh1_n, dg3 = _rms_bwd(dfin, h1 * r3, r3, g3_ref[...])
        dh1 = dh2_ref[...] + dh1_n
        dh1_ref[...] = dh1
        y = y_ref[...]
        r2 = _rms(y)
        dy, dg2 = _rms_bwd(dh1, y * r2, r2, g2_ref[...])
        dy_ref[...] = dy.astype(BF16)

        @pl.when(i == 0)
        def _():
            dg3_ref[...] = jnp.zeros_like(dg3_ref)
            dg2_ref[...] = jnp.zeros_like(dg2_ref)

        dg3_ref[...] += jnp.sum(dg3, axis=0, keepdims=True)
        dg2_ref[...] += jnp.sum(dg2, axis=0, keepdims=True)

    row = lambda w: pl.BlockSpec((tm, w), lambda i: (i, 0))
    vec = pl.BlockSpec((1, D_MODEL), lambda i: (0, 0))
    return pl.pallas_call(
        body, name="ffn_bwd", grid=(S // tm,),
        in_specs=[row(D_FF), row(D_FF)] + [row(D_MODEL)] * 4
        + [_const_spec(wg.shape), _const_spec(wu.shape), _const_spec(wd.shape),
           _const_spec((1, D_MODEL)), _const_spec((1, D_MODEL))],
        out_specs=[row(D_FF), row(D_FF), row(D_MODEL), row(D_MODEL), vec, vec],
        out_shape=[jax.ShapeDtypeStruct((S, D_FF), BF16)] * 2
        + [jax.ShapeDtypeStruct((S, D_MODEL), F32), jax.ShapeDtypeStruct((S, D_MODEL), BF16),
           jax.ShapeDtypeStruct((1, D_MODEL), F32), jax.ShapeDtypeStruct((1, D_MODEL), F32)],
        compiler_params=_cp(("arbitrary",)),
    )(gt_bf, up_bf, df, dh2, h1, yv, wg, wu, wd, g3, g2)


def _out_bwd(dy, o, ag, wc, tm):
    S = o.shape[0]

    def body(dy_ref, o_ref, ag_ref, w_ref, dco_ref, do_ref, dag_ref):
        i = pl.program_id(0)
        seg = _seg_matrix(CONV_CH)
        dy = dy_ref[...]
        dco_ref[...] = _dot_nt(dy, w_ref[0:CONV_CH, :])
        datt = _dot_nt(dy, w_ref[CONV_CH:, :])
        o = o_ref[...]
        r = lax.rsqrt(_head_mean(o * o, seg) + EPS)
        n = o * r
        dn = datt * ag_ref[...]
        do_ref[...] = (r * (dn - n * _head_mean(dn * n, seg))).astype(BF16)

        @pl.when(i == 0)
        def _():
            dag_ref[...] = jnp.zeros_like(dag_ref)

        dag_ref[...] += jnp.sum(datt * n, axis=0, keepdims=True)

    row = lambda w: pl.BlockSpec((tm, w), lambda i: (i, 0))
    return pl.pallas_call(
        body, name="out_bwd", grid=(S // tm,),
        in_specs=[row(D_MODEL), row(CONV_CH), _const_spec((1, CONV_CH)), _const_spec(wc.shape)],
        out_specs=[row(CONV_CH), row(CONV_CH), pl.BlockSpec((1, CONV_CH), lambda i: (0, 0))],
        out_shape=[jax.ShapeDtypeStruct((S, CONV_CH), F32), jax.ShapeDtypeStruct((S, CONV_CH), BF16),
                   jax.ShapeDtypeStruct((1, CONV_CH), F32)],
        compiler_params=_cp(("arbitrary",)),
    )(dy, o, ag, wc)


def _attn_bwd(qkv, do, t, tk, parts):
    S = qkv.shape[0]
    nk = S // tk
    ns = len(parts)

    def body(*refs):
        q_ref, k_ref, v_ref, do_ref = refs[:4]
        dq_ref, dk_hbm, dv_hbm = refs[4 + ns:7 + ns]
        g_buf, s_buf, r_ref, dq_acc, dk_ref, dv_ref, z_buf = refs[7 + 2 * ns:14 + 2 * ns]
        p = pl.program_id(0)
        i = pl.program_id(1)
        plan = _ScatterPlan(refs[4:4 + ns], refs[7 + ns:7 + 2 * ns], *refs[14 + 2 * ns:])
        pl.when((p == 0) & (i == 0))(plan.start)
        last = (i * t + t - 1) // tk

        @pl.when(i == 0)
        def _():
            dk_ref[...] = jnp.zeros_like(dk_ref)
            dv_ref[...] = jnp.zeros_like(dv_ref)

        m_suf = _suffix_matrix(tk // KEY_BLOCKS)
        m_pre = _suffix_matrix(tk // KEY_BLOCKS, prefix=True)
        q = q_ref[...]
        do = do_ref[...]
        hms = [_lane_mask(h) for h in range(2)]
        qms = [jnp.where(hm, q, 0) * 0.125 for hm in hms]
        doms = [jnp.where(hm, do, 0) for hm in hms]
        dq_acc[...] = jnp.zeros_like(dq_acc)
        r_ref[...] = jnp.zeros_like(r_ref)

        def keys(sb, w=tk):
            return pl.ds(pl.multiple_of(sb * tk, tk), w)

        def matmuls1(sb, w, diagonal):
            kt = k_ref[keys(sb, w), :]
            vt = v_ref[keys(sb, w), :]
            mask = _causal_mask(i, sb, t, tk, w) if diagonal else None
            return [(_scores(qms[h], kt, mask), _dot_nt(doms[h], vt)) for h in range(2)]

        def sweep1(sb, w, vals):
            dv = jnp.zeros((w, PAIR), F32)
            stack = lambda a, b: jnp.concatenate([a, b], axis=0)
            A2, sp2, r2 = _sb_tile(stack(vals[0][0], vals[1][0]), stack(r_ref[0], r_ref[1]), m_suf)
            for h in range(2):
                part = slice(h * t, (h + 1) * t)
                A = A2[part]
                r_ref[h] = r2[part]
                g_buf[h, sb, :, 0:w] = A * vals[h][1]
                s_buf[h, sb, :, 0:w] = 1.0 - jnp.exp(-sp2[part])
                dv = dv + _dot_tn(A.astype(BF16), doms[h])
            dv_ref[keys(sb, w), :] += dv

        def sweep2(sb, w):
            kt = k_ref[keys(sb, w), :]
            dk = jnp.zeros((w, PAIR), F32)
            g2 = jnp.concatenate([g_buf[0, sb, :, 0:w], g_buf[1, sb, :, 0:w]], axis=0)
            pre2, r2 = _running_sums(g2, m_pre, reverse=False,
                                     start=jnp.concatenate([r_ref[0], r_ref[1]], axis=0))
            for h in range(2):
                part = slice(h * t, (h + 1) * t)
                g, pre = g2[part], pre2[part]
                r_ref[h] = r2[part]
                dzb = (g - s_buf[h, sb, :, 0:w] * pre).astype(BF16)
                dq_acc[...] += _dot(dzb, jnp.where(hms[h], kt, 0))
                dk = dk + _dot_tn(dzb, qms[h])
            dk_ref[keys(sb, w), :] += dk

        def diagonal_tile(tile):
            for nb in range(1, KEY_BLOCKS + 1):
                pl.when(lax.rem(i, KEY_BLOCKS) == nb - 1)(functools.partial(tile, nb * t))

        def fetch1(sb, slot, first):
            kt = k_ref[keys(sb), :]
            for h in range(2):
                z_buf[slot, h] = _scores(qms[h], kt, None)

        def load1(slot):
            return [z_buf[slot, h] for h in range(2)]

        def work1(sb, zs):
            vt = v_ref[keys(sb), :]
            sweep1(sb, tk, [(zs[h], _dot_nt(doms[h], vt)) for h in range(2)])

        def diagonal1(w):
            fetch1(jnp.maximum(last - 1, 0), 0, True)
            sweep1(last, w, matmuls1(last, w, True))

        diagonal_tile(diagonal1)
        pl.when(last >= 1)(lambda: _sweep(last - 1, last - 1, True, fetch1, load1, work1, fetched=True))
        r_ref[...] = jnp.zeros_like(r_ref)
        _sweep_pairs(last, lambda sb: sweep2(sb, tk))
        diagonal_tile(lambda w: sweep2(last, w))
        dq_ref[...] = dq_acc[...] * 0.125

        @pl.when(i == S // t - 1)
        def _():
            cols = pl.ds(pl.multiple_of(p * PAIR, PAIR), PAIR)
            pltpu.sync_copy(dk_ref, dk_hbm.at[:, cols])
            pltpu.sync_copy(dv_ref, dv_hbm.at[:, cols])

        pl.when((p == N_PAIRS - 1) & (i == S // t - 1))(plan.finish)

    once = lambda cb: pl.BlockSpec((S, PAIR), cb, pipeline_mode=pl.Buffered(1))
    return pl.pallas_call(
        body, name="attn_bwd", grid=(N_PAIRS, S // t),
        in_specs=[pl.BlockSpec((t, PAIR), lambda p, i: (i, p)),
                  once(lambda p, i: (0, N_PAIRS + p)), once(lambda p, i: (0, 2 * N_PAIRS + p)),
                  pl.BlockSpec((t, PAIR), lambda p, i: (i, p))] + [_hbm()] * ns,
        out_specs=[pl.BlockSpec((t, PAIR), lambda p, i: (i, p)), _hbm(), _hbm()] + [_hbm()] * ns,
        out_shape=[jax.ShapeDtypeStruct((S, N_PAIRS * PAIR), F32)] * 3
        + [jax.ShapeDtypeStruct(pt.shape, pt.dtype) for pt in parts],
        scratch_shapes=[pltpu.VMEM((2, nk, t, tk), F32), pltpu.VMEM((2, nk, t, tk), F32),
                        pltpu.VMEM((2, t, 1), F32), pltpu.VMEM((t, PAIR), F32),
                        pltpu.VMEM((S, PAIR), F32), pltpu.VMEM((S, PAIR), F32),
                        pltpu.VMEM((2, 2, t, tk), F32)] + _scatter_sems(ns),
        compiler_params=_cp(("arbitrary", "arbitrary"), vmem=VMEM_LIMIT_ATTN_BWD),
    )(qkv, qkv, qkv, do, *parts)


def _conv_bwd(uc, yconv, dco, cwf, lg, lb, tm):
    S = uc.shape[0]
    hb = tm // HALO
    nb = S // tm
    ext = tm + HALO

    def body(uc_ref, prev_ref, y_ref, ynext_ref, dco_ref, dnext_ref, cw_ref, lg_ref, lb_ref,
             duc_ref, dcw_ref, dcb_ref, dlg_ref, dlb_ref, glu_ref, dyc_ref, shg_ref, shd_ref):
        i = pl.program_id(0)
        last = i == nb - 1

        @pl.when(i == 0)
        def _():
            for ref in (dcw_ref, dcb_ref, dlg_ref, dlb_ref):
                ref[...] = jnp.zeros_like(ref)

        uc = uc_ref[...]
        glu_ref[0:HALO, :] = jnp.where(i == 0, 0.0, _glu(prev_ref[...]))
        glu_ref[HALO:ext, :] = _glu(uc)
        glu_ref[ext:ext + SUBLANES, :] = jnp.zeros((SUBLANES, CONV_CH), F32)
        _shift_copies(glu_ref, shg_ref)
        fwd_offs = [HALO - (CONV_WIDTH - 1) + w for w in range(CONV_WIDTH)]
        y = jnp.concatenate([y_ref[...], ynext_ref[...]], axis=0)
        mu = jnp.mean(y, axis=-1, keepdims=True)
        yc = y - mu
        rstd = lax.rsqrt(jnp.mean(yc * yc, axis=-1, keepdims=True) + EPS)
        yhat = yc * rstd
        lg = lg_ref[...]
        ln = yhat * lg + lb_ref[...]
        sg = _sigmoid(ln)
        dout = jnp.concatenate([dco_ref[...], jnp.where(last, 0.0, dnext_ref[...])], axis=0)
        dln = dout * (sg * (1.0 + ln * (1.0 - sg)))
        dyh = dln * lg
        dyc = rstd * (dyh - jnp.mean(dyh, axis=-1, keepdims=True)
                      - yhat * jnp.mean(dyh * yhat, axis=-1, keepdims=True))
        dyc_ref[0:ext, :] = dyc
        dyc_ref[ext:ext + SUBLANES, :] = jnp.zeros((SUBLANES, CONV_CH), F32)
        _shift_copies(dyc_ref, shd_ref)
        dlg_ref[...] += jnp.sum((dln * yhat)[0:tm], axis=0, keepdims=True)
        dlb_ref[...] += jnp.sum(dln[0:tm], axis=0, keepdims=True)
        dcb_ref[...] += jnp.sum(dyc[0:tm], axis=0, keepdims=True)
        dglu = _conv_taps(cw_ref, dyc_ref, shd_ref, [CONV_WIDTH - 1 - w for w in range(CONV_WIDTH)], tm)
        d0 = dyc[0:tm]
        for w, off in enumerate(fwd_offs):
            dcw_ref[w:w + 1, :] += jnp.sum(d0 * _rows_at(glu_ref, shg_ref, off, tm), axis=0, keepdims=True)
        val, gate = uc[:, :CONV_CH], uc[:, CONV_CH:]
        sgate = _sigmoid(gate)
        duc_ref[:, :CONV_CH] = (dglu * sgate).astype(BF16)
        duc_ref[:, CONV_CH:] = (dglu * val * sgate * (1.0 - sgate)).astype(BF16)

    vec = pl.BlockSpec((1, CONV_CH), lambda i: (0, 0))
    nxt = lambda i: (jnp.minimum((i + 1) * hb, S // HALO - 1), 0)
    return pl.pallas_call(
        body, name="conv_bwd", grid=(nb,),
        in_specs=[pl.BlockSpec((tm, 2 * CONV_CH), lambda i: (i, 0)),
                  pl.BlockSpec((HALO, 2 * CONV_CH), lambda i: (jnp.maximum(i * hb - 1, 0), 0)),
                  pl.BlockSpec((tm, CONV_CH), lambda i: (i, 0)), pl.BlockSpec((HALO, CONV_CH), nxt),
                  pl.BlockSpec((tm, CONV_CH), lambda i: (i, 0)), pl.BlockSpec((HALO, CONV_CH), nxt),
                  _const_spec(cwf.shape), _const_spec((1, CONV_CH)), _const_spec((1, CONV_CH))],
        out_specs=[pl.BlockSpec((tm, 2 * CONV_CH), lambda i: (i, 0)),
                   pl.BlockSpec(cwf.shape, lambda i: (0, 0)), vec, vec, vec],
        out_shape=[jax.ShapeDtypeStruct((S, 2 * CONV_CH), BF16), jax.ShapeDtypeStruct(cwf.shape, F32)]
        + [jax.ShapeDtypeStruct((1, CONV_CH), F32)] * 3,
        scratch_shapes=[pltpu.VMEM((ext + SUBLANES, CONV_CH), F32), pltpu.VMEM((ext + SUBLANES, CONV_CH), F32),
                        pltpu.VMEM((SUBLANES - 1, ext, CONV_CH), F32),
                        pltpu.VMEM((SUBLANES - 1, ext, CONV_CH), F32)],
        compiler_params=_cp(("arbitrary",)),
    )(uc, uc, yconv, yconv, dco, dco, cwf, lg, lb)


def _in_bwd(duc, dq, dk, dv, x2, dh1, g1, wa, tm):
    S = x2.shape[0]

    def body(duc_ref, dq_ref, dk_ref, dv_ref, x_ref, dh1_ref, g_ref, w_ref, gx_ref, du_ref, dg_ref):
        i = pl.program_id(0)
        du = jnp.concatenate([duc_ref[...], dq_ref[...].astype(BF16), dk_ref[...].astype(BF16),
                              dv_ref[...].astype(BF16)], axis=1)
        du_ref[...] = du
        da = _dot_nt(du[:, 0:IN_SH], w_ref[0])
        for j in range(1, N_CHIPS):
            da = da + _dot_nt(du[:, IN_SH * j:IN_SH * (j + 1)], w_ref[j])
        x = x_ref[...]
        r = _rms(x)
        dx, dg = _rms_bwd(da, x * r, r, g_ref[...])
        gx_ref[...] = dh1_ref[...] + dx

        @pl.when(i == 0)
        def _():
            dg_ref[...] = jnp.zeros_like(dg_ref)

        dg_ref[...] += jnp.sum(dg, axis=0, keepdims=True)

    row = lambda w: pl.BlockSpec((tm, w), lambda i: (i, 0))
    return pl.pallas_call(
        body, name="in_bwd", grid=(S // tm,),
        in_specs=[row(2 * CONV_CH), row(CONV_CH), row(CONV_CH), row(CONV_CH), row(D_MODEL), row(D_MODEL),
                  _const_spec((1, D_MODEL)), _const_spec(wa.shape)],
        out_specs=[pl.BlockSpec((None, tm, D_MODEL), lambda i: (0, i, 0)), row(2560),
                   pl.BlockSpec((1, D_MODEL), lambda i: (0, 0))],
        out_shape=[jax.ShapeDtypeStruct((1, S, D_MODEL), F32), jax.ShapeDtypeStruct((S, 2560), BF16),
                   jax.ShapeDtypeStruct((1, D_MODEL), F32)],
        compiler_params=_cp(("arbitrary",)),
    )(duc, dq, dk, dv, x2, dh1, g1, wa)


def _matmul_tn(xm, ym, tm, ts, name, column_block=None):
    S, M = xm.shape
    N = ym.shape[1]

    def body(x_ref, y_ref, o_ref):
        @pl.when(pl.program_id(1) == 0)
        def _():
            o_ref[...] = jnp.zeros_like(o_ref)

        xt = x_ref[...].T
        if column_block is None:
            o_ref[...] += _dot(xt, y_ref[...])
        else:
            for j in range(N // column_block):
                o_ref[j] += _dot(xt, y_ref[:, column_block * j:column_block * (j + 1)])

    if column_block is None:
        out_spec = pl.BlockSpec((tm, N), lambda m, s: (m, 0))
        out_shape = jax.ShapeDtypeStruct((M, N), F32)
    else:
        out_spec = pl.BlockSpec((N // column_block, tm, column_block), lambda m, s: (0, m, 0))
        out_shape = jax.ShapeDtypeStruct((N // column_block, M, column_block), F32)
    return pl.pallas_call(
        body, name=name, grid=(M // tm, S // ts),
        in_specs=[pl.BlockSpec((ts, tm), lambda m, s: (s, m)), pl.BlockSpec((ts, N), lambda m, s: (s, 0))],
        out_specs=out_spec, out_shape=out_shape,
        compiler_params=_cp(("parallel", "arbitrary")),
    )(xm, ym)


def _sibling_halves(grads, name):
    n = len(grads)

    def body(*refs):
        ins, outs, ssem, rsem = refs[:n], refs[n:2 * n], refs[2 * n], refs[2 * n + 1]
        x, y, c = lax.axis_index("x"), lax.axis_index("y"), lax.axis_index("c")
        copies = []
        for k in range(n):
            for j in range(N_CHIPS):
                copies.append(pltpu.make_async_remote_copy(
                    src_ref=ins[k].at[j, 1 - c], dst_ref=outs[k].at[j],
                    send_sem=ssem.at[N_CHIPS * k + j], recv_sem=rsem.at[N_CHIPS * k + j],
                    device_id=(x, y, 1 - c), device_id_type=MESH))
        for cp in copies:
            cp.start()
        for cp in copies:
            cp.wait()

    shapes = [jax.ShapeDtypeStruct((g.shape[0],) + g.shape[2:], F32) for g in grads]
    return pl.pallas_call(
        body, name=name, out_shape=shapes,
        in_specs=[_hbm()] * n, out_specs=[_hbm()] * n,
        scratch_shapes=[pltpu.SemaphoreType.DMA((N_CHIPS * n,)), pltpu.SemaphoreType.DMA((N_CHIPS * n,))],
    )(*grads)


def _add_half(c_arr, g, landed, name):
    def body(c_ref, g_ref, l_ref, o_ref):
        o_ref[...] = (g_ref[...] + l_ref[...]).astype(BF16)

    rows, n = g.shape[2], g.shape[3]
    grid = (N_CHIPS,)
    g_spec = pl.BlockSpec((None, None, rows, n), lambda j, c: (j, c[0], 0, 0))
    l_spec = pl.BlockSpec((None, rows, n), lambda j, c: (j, 0, 0))
    return pl.pallas_call(
        body, name=name,
        grid_spec=pltpu.PrefetchScalarGridSpec(num_scalar_prefetch=1, grid=grid, in_specs=[g_spec, l_spec],
                                               out_specs=l_spec),
        out_shape=jax.ShapeDtypeStruct(landed.shape, BF16),
        compiler_params=_cp(("parallel",)),
    )(c_arr, g, landed)


class _ScatterPlan:
    def __init__(self, ins, outs, lsem, ssem, rsem):
        x, y, c = lax.axis_index("x"), lax.axis_index("y"), lax.axis_index("c")
        me = 2 * x + y
        self.copies = []
        for k in range(len(ins)):
            self.copies.append(pltpu.make_async_copy(ins[k].at[me], outs[k].at[me], lsem.at[k]))
            for r, chip in enumerate([(1 - x, y), (x, 1 - y), (1 - x, 1 - y)]):
                self.copies.append(pltpu.make_async_remote_copy(
                    src_ref=ins[k].at[2 * chip[0] + chip[1]], dst_ref=outs[k].at[me],
                    send_sem=ssem.at[3 * k + r], recv_sem=rsem.at[3 * k + r],
                    device_id=(chip[0], chip[1], c), device_id_type=MESH))

    def start(self):
        for cp in self.copies:
            cp.start()

    def finish(self):
        for cp in self.copies:
            cp.wait()


def _scatter_sems(n):
    return [pltpu.SemaphoreType.DMA((n,)), pltpu.SemaphoreType.DMA((3 * n,)), pltpu.SemaphoreType.DMA((3 * n,))]


def _chip_scatter(parts):
    n = len(parts)

    def body(*refs):
        plan = _ScatterPlan(refs[:n], refs[n:2 * n], *refs[2 * n:])
        plan.start()
        plan.finish()

    shapes = [jax.ShapeDtypeStruct(p.shape, p.dtype) for p in parts]
    return pl.pallas_call(
        body, name="grad_chip_scatter", out_shape=shapes,
        in_specs=[_hbm()] * n, out_specs=[_hbm()] * n, scratch_shapes=_scatter_sems(n),
    )(*parts)


def _sum_chips(landed, name):
    _, rows, n = landed.shape
    tr = 256 if rows % 256 == 0 else rows

    def body(a_ref, b_ref, c_ref, d_ref, o_ref):
        f = lambda ref: ref[...].astype(F32)
        o_ref[...] = ((f(a_ref) + f(b_ref)) + f(c_ref)) + f(d_ref)

    specs = [pl.BlockSpec((None, tr, n), functools.partial(lambda i, j: (j, i, 0), j=j)) for j in range(N_CHIPS)]
    return pl.pallas_call(
        body, name=name, grid=(rows // tr,), in_specs=specs,
        out_specs=pl.BlockSpec((tr, n), lambda i: (i, 0)),
        out_shape=jax.ShapeDtypeStruct((rows, n), F32),
        compiler_params=_cp(("parallel",)),
    )(landed, landed, landed, landed)


def _share_halves(halves):
    n = len(halves)

    def body(*refs):
        ins, outs = refs[:n], refs[n:2 * n]
        ssem, rsem = refs[2 * n:]
        x, y, c = lax.axis_index("x"), lax.axis_index("y"), lax.axis_index("c")
        copies = [pltpu.make_async_remote_copy(
            src_ref=ins[k], dst_ref=outs[k], send_sem=ssem.at[k], recv_sem=rsem.at[k],
            device_id=(x, y, 1 - c), device_id_type=MESH) for k in range(n)]
        for cp in copies:
            cp.start()
        for cp in copies:
            cp.wait()

    shapes = [jax.ShapeDtypeStruct(h.shape, F32) for h in halves]
    return pl.pallas_call(
        body, name="grad_share_halves", out_shape=shapes,
        in_specs=[_hbm()] * n, out_specs=[_hbm()] * n,
        scratch_shapes=[pltpu.SemaphoreType.DMA((n,)), pltpu.SemaphoreType.DMA((n,))],
    )(*halves)


def _allreduce_small(packed):
    rows, n = packed.shape

    def body(in_ref, out_ref, land_ref, ssem, rsem):
        x, y, c = lax.axis_index("x"), lax.axis_index("y"), lax.axis_index("c")
        me = 4 * x + 2 * y + c
        land_ref[me] = in_ref[...]
        copies = []
        for r in range(1, 8):
            tx = 1 - x if r & 4 else x
            ty = 1 - y if r & 2 else y
            tc = 1 - c if r & 1 else c
            cp = pltpu.make_async_remote_copy(
                src_ref=in_ref, dst_ref=land_ref.at[me], send_sem=ssem.at[r - 1], recv_sem=rsem.at[r - 1],
                device_id=(tx, ty, tc), device_id_type=MESH)
            cp.start()
            copies.append(cp)
        for cp in copies:
            cp.wait()
        acc = land_ref[0]
        for k in range(1, 8):
            acc = acc + land_ref[k]
        out_ref[...] = acc

    return pl.pallas_call(
        body, name="allreduce_small", out_shape=jax.ShapeDtypeStruct((rows, n), F32),
        in_specs=[pl.BlockSpec(memory_space=pltpu.VMEM)], out_specs=pl.BlockSpec(memory_space=pltpu.VMEM),
        scratch_shapes=[pltpu.VMEM((8, rows, n), F32), pltpu.SemaphoreType.DMA((7,)),
                        pltpu.SemaphoreType.DMA((7,))],
    )(packed)


def _adamw_math(w, g, m, v):
    m = ADAM_B1 * m + (1.0 - ADAM_B1) * g
    v = ADAM_B2 * v + (1.0 - ADAM_B2) * (g * g)
    m_hat = m / (1.0 - ADAM_B1 ** ADAM_STEP)
    v_hat = v / (1.0 - ADAM_B2 ** ADAM_STEP)
    return -ADAM_LR * (m_hat / (jnp.sqrt(v_hat) + ADAM_EPS) + ADAM_WD * w), m, v


def _adamw_halves(c_arr, w, mine, other, m, v, name):
    rows, n = mine.shape
    tr = 256 if rows % 256 == 0 else rows
    nb = rows // tr

    def body(c_ref, w_ref, a_ref, b_ref, m_ref, v_ref, g_ref, d_ref, mo_ref, vo_ref):
        g = jnp.where(pl.program_id(0) == c_ref[0], a_ref[...], b_ref[...])
        g_ref[...] = g
        d_ref[...], mo_ref[...], vo_ref[...] = _adamw_math(w_ref[...], g, m_ref[...], v_ref[...])

    full = pl.BlockSpec((None, tr, n), lambda h, i, c: (0, h * nb + i, 0))
    half = pl.BlockSpec((tr, n), lambda h, i, c: (i, 0))
    return pl.pallas_call(
        body, name=name,
        grid_spec=pltpu.PrefetchScalarGridSpec(num_scalar_prefetch=1, grid=(2, nb),
                                               in_specs=[full, half, half, full, full], out_specs=[full] * 4),
        out_shape=[jax.ShapeDtypeStruct((1, 2 * rows, n), F32)] * 4,
        compiler_params=_cp(("parallel", "parallel")),
    )(c_arr, w, mine, other, m, v)


def _adamw(w, g, m, v, name):
    rows, n = w.shape
    tr = 256 if rows % 256 == 0 else rows

    def body(w_ref, g_ref, m_ref, v_ref, d_ref, mo_ref, vo_ref):
        d_ref[...], mo_ref[...], vo_ref[...] = _adamw_math(w_ref[...], g_ref[...], m_ref[...], v_ref[...])

    spec = pl.BlockSpec((tr, n), lambda i: (i, 0))
    return pl.pallas_call(
        body, name=name, grid=(rows // tr,), in_specs=[spec] * 4, out_specs=[spec] * 3,
        out_shape=[jax.ShapeDtypeStruct((rows, n), F32)] * 3,
        compiler_params=_cp(("parallel",)),
    )(w, g, m, v)


def _rows8(a):
    a = a.reshape(-1, 128)
    return jnp.pad(a, ((0, (-a.shape[0]) % 8), (0, 0)))


def kernel(x, g_pre_mix, w_in, conv_w, conv_b, conv_ln_g, conv_ln_b, attn_norm_g, w_out, g_post_mix, g_pre_ffn, w_gate, w_up, w_down, g_post_ffn, loss_target, m_g_pre_mix, m_w_in, m_conv_w, m_conv_b, m_conv_ln_g, m_conv_ln_b, m_attn_norm_g, m_w_out, m_g_post_mix, m_g_pre_ffn, m_w_gate, m_w_up, m_w_down, m_g_post_ffn, v_g_pre_mix, v_w_in, v_conv_w, v_conv_b, v_conv_ln_g, v_conv_ln_b, v_attn_norm_g, v_w_out, v_g_post_mix, v_g_pre_ffn, v_w_gate, v_w_up, v_w_down, v_g_post_ffn):
    S = x.shape[1]
    tm_big = min(512, S)
    tm_ffn = min(256, S)
    tk_att = min(1024, S)
    t_att_fwd = min(1024, S)
    t_att_bwd = tk_att // KEY_BLOCKS
    chip = 2 * lax.axis_index("x") + lax.axis_index("y")
    core = lax.axis_index("c")
    x2 = x.reshape(S, D_MODEL)
    tgt = loss_target.reshape(S, D_MODEL)
    ag = attn_norm_g.reshape(1, CONV_CH)

    a_sh = w_in[0].astype(BF16)
    b_sh = jnp.stack([w_gate[0], w_up[0]]).astype(BF16)
    c_sh = jnp.concatenate([w_out[0], w_down[0]], axis=0).astype(BF16)
    cw_sh = jnp.pad(conv_w[0, :, 0, :], ((0, 1), (0, 0)))
    own = lambda full, shard: lax.dynamic_update_index_in_dim(full, shard, chip, 0)
    cols = lambda w4: jnp.concatenate([w4[j] for j in range(N_CHIPS)], axis=1)
    wa4, cw4 = _gather_weights([a_sh, cw_sh], [False, False])
    wa = own(wa4, a_sh)
    cwf = cols(own(cw4, cw_sh))

    a_bf, uc, qkv = _in_proj(x2, g_pre_mix, wa, tm_big)
    conv_out, yconv = _conv_fwd(uc, cwf, conv_b, conv_ln_g, conv_ln_b, tm_big)
    o, wb4, wc4 = _attn_fwd(qkv, t_att_fwd, tk_att, [b_sh, c_sh], [True, False])
    wb4, wc4 = own(wb4, b_sh), own(wc4, c_sh)
    wg, wu = cols(wb4[:, 0]), cols(wb4[:, 1])
    wo = wc4[:, :OUT_SH].reshape(D_MODEL, D_MODEL)
    wd = wc4[:, OUT_SH:].reshape(D_FF, D_MODEL)
    mixed, yv, h1, f_in = _out_proj(conv_out, o, ag, wo, x2, g_post_mix, g_pre_ffn, tm_big)
    df, dh2, dg4, loss_part, gt_bf, up_bf, act = _ffn_fwd(f_in, h1, tgt, wg, wu, wd, g_post_ffn, tm_ffn)

    dgt, dup, dh1, dy, dg3, dg2 = _ffn_bwd(gt_bf, up_bf, df, dh2, h1, yv, wg, wu, wd, g_pre_ffn, g_post_mix, tm_ffn)
    dco, do, dag = _out_bwd(dy, o, ag, wo, tm_big)
    ts = min(1024, S)
    gw_out = _matmul_tn(mixed, dy, D_MODEL, ts, "grad_w_out")
    gw_gate = _matmul_tn(f_in, dgt, D_MODEL, ts, "grad_w_gate")
    gw_up = _matmul_tn(f_in, dup, D_MODEL, ts, "grad_w_up")
    gw_down = _matmul_tn(act, df, D_FF // 2, ts, "grad_w_down")

    by_cols = lambda g: jnp.transpose(g.reshape(2, D_MODEL // 2, N_CHIPS, -1), (2, 0, 1, 3))
    by_rows = lambda g: g.reshape(N_CHIPS, 2, g.shape[0] // (2 * N_CHIPS), g.shape[1])
    c_arr = core.reshape(1).astype(jnp.int32)

    def chip_partials(views, nms):
        landed = _sibling_halves(views, "grad_sibling_halves_" + nms[0])
        return [_add_half(c_arr, g, l, "grad_half_" + nm) for g, l, nm in zip(views, landed, nms)]

    early = ["w_gate", "w_up", "w_out", "w_down"]
    parts = chip_partials([by_cols(gw_gate), by_cols(gw_up), by_rows(gw_out), by_rows(gw_down)], early)
    dq, dk, dv, *slots = _attn_bwd(qkv, do, t_att_bwd, tk_att, parts)
    duc, dcw, dcb, dlg, dlb = _conv_bwd(uc, yconv, dco, cwf, conv_ln_g, conv_ln_b, tm_big)
    grad_x, du, dg1 = _in_bwd(duc, dq, dk, dv, x2, dh1, g_pre_mix, wa, tm_big)
    gw_in = _matmul_tn(a_bf, du, D_MODEL, ts, "grad_w_in", column_block=IN_SH)
    slots += _chip_scatter(chip_partials([gw_in.reshape(N_CHIPS, 2, D_MODEL // 2, IN_SH)], ["w_in"]))
    names = early + ["w_in"]
    halves = [_sum_chips(s, "grad_sum_" + nm) for s, nm in zip(slots, names)]
    others = _share_halves(halves)
    mine = dict(zip(names, halves))
    other = dict(zip(names, others))

    small = [dg1, dcb, dlg, dlb, dag, dg2, dg3, dg4]
    packed = jnp.concatenate([_rows8(s) for s in small] + [_rows8(dcw), _rows8(loss_part)], axis=0)
    red = _allreduce_small(packed)
    sizes = [D_MODEL, CONV_CH, CONV_CH, CONV_CH, CONV_CH, D_MODEL, D_MODEL, D_MODEL]
    g_small = [red[8 * k:8 * k + n // 128].reshape(1, n) for k, n in enumerate(sizes)]
    cw_red = red[64:64 + 128].reshape(HALO, CONV_CH)
    g_cw = lax.dynamic_slice(cw_red, (0, chip * 128), (HALO, 128))
    loss = red[192, 0]

    big = []
    for w, m, v, nm in [(w_in, m_w_in, v_w_in, "w_in"), (w_out, m_w_out, v_w_out, "w_out"),
                        (w_gate, m_w_gate, v_w_gate, "w_gate"), (w_up, m_w_up, v_w_up, "w_up"),
                        (w_down, m_w_down, v_w_down, "w_down")]:
        big.append(_adamw_halves(c_arr, w, mine[nm], other[nm], m, v, "adamw_" + nm))
    sm_w = [g_pre_mix, conv_b, conv_ln_g, conv_ln_b, ag, g_post_mix, g_pre_ffn, g_post_ffn]
    sm_m = [m_g_pre_mix, m_conv_b, m_conv_ln_g, m_conv_ln_b, m_attn_norm_g, m_g_post_mix, m_g_pre_ffn, m_g_post_ffn]
    sm_v = [v_g_pre_mix, v_conv_b, v_conv_ln_g, v_conv_ln_b, v_attn_norm_g, v_g_post_mix, v_g_pre_ffn, v_g_post_ffn]
    pad_cw = lambda a: jnp.pad(a[0, :, 0, :], ((0, 1), (0, 0)))

    def pack(vecs, cw):
        return jnp.concatenate([_rows8(a) for a in vecs] + [cw], axis=0)

    sd, smn, svn = _adamw(pack(sm_w, pad_cw(conv_w)), pack(g_small, g_cw), pack(sm_m, pad_cw(m_conv_w)),
                          pack(sm_v, pad_cw(v_conv_w)), "adamw_small")

    def unpack(p):
        vecs = [p[8 * k:8 * k + n // 128].reshape(1, n) for k, n in enumerate(sizes)]
        return vecs, p[64:64 + CONV_WIDTH].reshape(1, CONV_WIDTH, 1, 128)

    def ordered(vecs, cw, w_in_, w_out_, w_gate_, w_up_, w_down_):
        g1_, cb_, lg_, lb_, ag_, g2_, g3_, g4_ = vecs
        return [g1_, w_in_, cw, cb_, lg_, lb_, ag_.reshape(1, 8, HEAD_DIM), w_out_, g2_, g3_,
                w_gate_, w_up_, w_down_, g4_]

    grads = ordered(g_small, g_cw[:CONV_WIDTH].reshape(1, CONV_WIDTH, 1, 128), *[b[0] for b in big])
    outs = []
    for idx, p in enumerate((sd, smn, svn)):
        vecs, cw = unpack(p)
        outs += ordered(vecs, cw, *[b[idx + 1] for b in big])
    return (loss, grad_x, *grads, *outs)
```
